```python
import math
import jax, jax.numpy as jnp
from jax import lax
import numpy as np

D_MODEL = 2048
BATCH = 16
SEQ = 2048
DEPTH = 4

D_MIX = D_MODEL
SSD_INNER = D_MIX // 2
SSD_HEAD_DIM = 64
SSD_HEADS = SSD_INNER // SSD_HEAD_DIM
SSD_GROUPS = 2
SSD_HEADS_PER_GROUP = SSD_HEADS // SSD_GROUPS
SSD_STATE = 128
SSD_CONV = 4
SSD_CHUNK = 128
SSD_CONV_DIM = SSD_INNER + 2 * SSD_GROUPS * SSD_STATE
DT_MIN = 0.001
DT_MAX = 0.1
MLA_V_HEAD = 128
MLA_HEADS = (D_MIX - SSD_INNER) // MLA_V_HEAD
MLA_NOPE = 128
MLA_ROPE = 64
MLA_QK_HEAD = MLA_NOPE + MLA_ROPE
Q_LORA = D_MODEL // 4
KV_LORA = D_MODEL // 4
ROPE_THETA = 10000.0
Q_BLOCK = 128
MEM_LEN = 256
X_HEADS = 4
X_HEAD_DIM = 128
X_INNER = X_HEADS * X_HEAD_DIM
FFN_HIDDEN = ((8 * D_MODEL + 3 * 256 - 1) // (3 * 256)) * 256
IN_COLS = SSD_INNER + SSD_CONV_DIM + SSD_HEADS + Q_LORA + KV_LORA + MLA_ROPE
RMS_EPS = 1e-6

kernel_name = 'hymba_ssd_mla_memory_hybrid'


def rms_norm(x, g):
    xf = x.astype(jnp.float32)
    y = xf * lax.rsqrt(jnp.mean(xf * xf, axis=-1, keepdims=True) + RMS_EPS)
    return (y * g.astype(jnp.float32)).astype(x.dtype)


def rope_tables(positions):
    inv_freq = 1.0 / (ROPE_THETA ** (jnp.arange(0, MLA_ROPE, 2, dtype=jnp.float32) / MLA_ROPE))
    ang = positions.astype(jnp.float32)[..., None] * inv_freq
    return jnp.cos(ang), jnp.sin(ang)


def apply_rope(x, cos, sin):
    half = x.shape[-1] // 2
    x1 = x[..., :half].astype(jnp.float32)
    x2 = x[..., half:].astype(jnp.float32)
    c = cos[:, :, None, :]
    s = sin[:, :, None, :]
    return jnp.concatenate([x1 * c - x2 * s, x2 * c + x1 * s], axis=-1).astype(x.dtype)


def causal_depthwise_conv(u, w, b):
    k = w.shape[0]
    y = lax.conv_general_dilated(u, w[:, None, :].astype(u.dtype), window_strides=(1,),
                                 padding=[(k - 1, 0)],
                                 dimension_numbers=('NWC', 'WIO', 'NWC'),
                                 feature_group_count=u.shape[-1])
    return y + b.astype(u.dtype)


def _swap(t):
    return jnp.transpose(t, (0, 3, 4, 1, 2))


def ssd_scan(xdt, a, b_in, c_in):
    bsz, seq = xdt.shape[:2]
    nc = seq // SSD_CHUNK
    xc = xdt.reshape(bsz, nc, SSD_CHUNK, SSD_GROUPS, SSD_HEADS_PER_GROUP, SSD_HEAD_DIM)
    bc = b_in.reshape(bsz, nc, SSD_CHUNK, SSD_GROUPS, SSD_STATE)
    cc = c_in.reshape(bsz, nc, SSD_CHUNK, SSD_GROUPS, SSD_STATE)
    ac = _swap(a.reshape(bsz, nc, SSD_CHUNK, SSD_GROUPS, SSD_HEADS_PER_GROUP))
    a_cs = jnp.cumsum(ac, axis=-1)
    causal = jnp.tril(jnp.ones((SSD_CHUNK, SSD_CHUNK), dtype=bool))
    decay_in = jnp.exp(jnp.where(causal, a_cs[..., :, None] - a_cs[..., None, :], -jnp.inf))
    cb = jnp.einsum('bclgn,bcsgn->bgcls', cc, bc)
    y_diag = jnp.einsum('bgecls,bcsgep->bclgep', cb[:, :, None] * decay_in, xc)
    decay_to_end = jnp.exp(a_cs[..., -1:] - a_cs)
    chunk_states = jnp.einsum('bclgn,bclgep->bcgepn', bc, xc * _swap(decay_to_end)[..., None])
    chunk_states = jnp.concatenate([jnp.zeros_like(chunk_states[:, :1]), chunk_states], axis=1)
    tot = jnp.pad(a_cs[..., -1], ((0, 0), (0, 0), (0, 0), (1, 0)))
    tot_cs = jnp.cumsum(tot, axis=-1)
    causal_c = jnp.tril(jnp.ones((nc + 1, nc + 1), dtype=bool))
    decay_chunk = jnp.exp(jnp.where(causal_c, tot_cs[..., :, None] - tot_cs[..., None, :], -jnp.inf))
    states_in = jnp.einsum('bgezc,bcgepn->bzgepn', decay_chunk, chunk_states)[:, :-1]
    y_off = jnp.einsum('bclgn,bcgepn->bclgep', cc, states_in) * _swap(jnp.exp(a_cs))[..., None]
    return (y_diag + y_off).reshape(bsz, seq, SSD_GROUPS, SSD_HEADS_PER_GROUP, SSD_HEAD_DIM)


def ssd_mixer(z, xbc, dt_raw, conv_w, conv_b, dt_bias, a_log, d_skip, norm_g):
    bsz, seq, _ = z.shape
    gn = SSD_GROUPS * SSD_STATE
    xbc = jax.nn.silu(causal_depthwise_conv(xbc, conv_w, conv_b))
    xs = xbc[..., :SSD_INNER].reshape(bsz, seq, SSD_GROUPS, SSD_HEADS_PER_GROUP, SSD_HEAD_DIM)
    b_in = xbc[..., SSD_INNER:SSD_INNER + gn].reshape(bsz, seq, SSD_GROUPS, SSD_STATE)
    c_in = xbc[..., SSD_INNER + gn:].reshape(bsz, seq, SSD_GROUPS, SSD_STATE)
    dt = jax.nn.softplus(dt_raw.astype(jnp.float32) + dt_bias.astype(jnp.float32))
    dt = dt.reshape(bsz, seq, SSD_GROUPS, SSD_HEADS_PER_GROUP)
    a = dt * (-jnp.exp(a_log.astype(jnp.float32))).reshape(SSD_GROUPS, SSD_HEADS_PER_GROUP)
    y = ssd_scan(xs * dt[..., None], a, b_in, c_in)
    y = y + xs * d_skip.reshape(SSD_GROUPS, SSD_HEADS_PER_GROUP, 1)
    y = (y.reshape(bsz, seq, SSD_INNER) * jax.nn.silu(z)).reshape(bsz, seq, SSD_GROUPS, SSD_INNER // SSD_GROUPS)
    y = rms_norm(y, norm_g.reshape(SSD_GROUPS, SSD_INNER // SSD_GROUPS))
    return y.reshape(bsz, seq, SSD_INNER).astype(z.dtype)


def causal_block_attention(q, k, v, scale):
    seq = q.shape[1]
    outs = []
    for blk in range(seq // Q_BLOCK):
        q0 = blk * Q_BLOCK
        kend = q0 + Q_BLOCK
        s = jnp.einsum('bqhd,bkhd->bhqk', q[:, q0:kend], k[:, :kend]).astype(jnp.float32) * scale
        mask = (q0 + jnp.arange(Q_BLOCK))[:, None] >= jnp.arange(kend)[None, :]
        p = jax.nn.softmax(jnp.where(mask, s, -jnp.inf), axis=-1)
        outs.append(jnp.einsum('bhqk,bkhd->bqhd', p.astype(v.dtype), v[:, :kend]))
    return jnp.concatenate(outs, axis=1)


def mla_mixer(q_a, kv_a, k_rope, cos, sin, q_a_norm_g, w_q_b, kv_a_norm_g, w_kv_b, q_norm_g, k_norm_g):
    bsz, seq, _ = q_a.shape
    q = (rms_norm(q_a, q_a_norm_g) @ w_q_b).reshape(bsz, seq, MLA_HEADS, MLA_QK_HEAD)
    kv = (rms_norm(kv_a, kv_a_norm_g) @ w_kv_b).reshape(bsz, seq, MLA_HEADS, MLA_NOPE + MLA_V_HEAD)
    k_nope = kv[..., :MLA_NOPE]
    v = kv[..., MLA_NOPE:]
    k_pe = jnp.broadcast_to(k_rope[:, :, None, :], (bsz, seq, MLA_HEADS, MLA_ROPE))
    k = jnp.concatenate([k_nope, k_pe], axis=-1)
    q = rms_norm(q, q_norm_g)
    k = rms_norm(k, k_norm_g)
    q = jnp.concatenate([q[..., :MLA_NOPE], apply_rope(q[..., MLA_NOPE:], cos, sin)], axis=-1)
    k = jnp.concatenate([k[..., :MLA_NOPE], apply_rope(k[..., MLA_NOPE:], cos, sin)], axis=-1)
    o = causal_block_attention(q, k, v, MLA_QK_HEAD ** -0.5)
    return o.reshape(bsz, seq, MLA_HEADS * MLA_V_HEAD)


def memory_cross_attention(h, m, w_q, w_k, w_v, q_norm_g, k_norm_g, w_o):
    bsz, seq, _ = h.shape
    mlen = m.shape[1]
    q = rms_norm((h @ w_q).reshape(bsz, seq, X_HEADS, X_HEAD_DIM), q_norm_g)
    k = rms_norm((m @ w_k).reshape(bsz, mlen, X_HEADS, X_HEAD_DIM), k_norm_g)
    v = (m @ w_v).reshape(bsz, mlen, X_HEADS, X_HEAD_DIM)
    s = jnp.einsum('bshd,bmhd->bhsm', q, k).astype(jnp.float32) * (X_HEAD_DIM ** -0.5)
    p = jax.nn.softmax(s, axis=-1)
    o = jnp.einsum('bhsm,bmhd->bshd', p.astype(v.dtype), v)
    return o.reshape(bsz, seq, X_INNER) @ w_o


def swiglu(h, w_gate, w_up, w_down):
    return (jax.nn.silu(h @ w_gate) * (h @ w_up)) @ w_down


def _fwd_setup_inputs(seed: int = 0) -> dict:
    key = jax.random.key(seed)
    ks = jax.random.split(key, 32)
    f32 = jnp.float32
    L = DEPTH

    def nrm(k, shape, fan_in):
        return jax.random.normal(k, shape, f32) * (fan_in ** -0.5)

    def gain(k, shape):
        return 1.0 + 0.02 * jax.random.normal(k, shape, f32)

    x = jax.random.normal(ks[0], (BATCH, SEQ, D_MODEL), f32)
    mem = jax.random.normal(ks[1], (BATCH, MEM_LEN, D_MODEL), f32)
    positions = (jnp.arange(SEQ, dtype=jnp.int32)[None, :]
                 + jax.random.randint(ks[2], (BATCH, 1), 0, 4096, dtype=jnp.int32))
    dt0 = jnp.exp(jax.random.uniform(ks[6], (L, SSD_HEADS), f32, math.log(DT_MIN), math.log(DT_MAX)))
    dt_bias = dt0 + jnp.log(-jnp.expm1(-dt0))
    a_log = jnp.log(jax.random.uniform(ks[7], (L, SSD_HEADS), f32, 1.0, 16.0))
    return {
        'x': x,
        'mem': mem,
        'positions': positions,
        'attn_norm_g': gain(ks[3], (L, D_MODEL)),
        'w_in': nrm(ks[4], (L, D_MODEL, IN_COLS), D_MODEL),
        'conv_w': nrm(ks[5], (L, SSD_CONV, SSD_CONV_DIM), SSD_CONV),
        'conv_b': 0.01 * jax.random.normal(ks[8], (L, SSD_CONV_DIM), f32),
        'dt_bias': dt_bias,
        'a_log': a_log,
        'd_skip': gain(ks[9], (L, SSD_HEADS)),
        'ssd_norm_g': gain(ks[10], (L, SSD_INNER)),
        'q_a_norm_g': gain(ks[11], (L, Q_LORA)),
        'w_q_b': nrm(ks[12], (L, Q_LORA, MLA_HEADS * MLA_QK_HEAD), Q_LORA),
        'kv_a_norm_g': gain(ks[13], (L, KV_LORA)),
        'w_kv_b': nrm(ks[14], (L, KV_LORA, MLA_HEADS * (MLA_NOPE + MLA_V_HEAD)), KV_LORA),
        'mla_q_norm_g': gain(ks[15], (L, MLA_QK_HEAD)),
        'mla_k_norm_g': gain(ks[16], (L, MLA_QK_HEAD)),
        'w_out': nrm(ks[17], (L, D_MIX, D_MODEL), D_MIX),
        'xattn_norm_g': gain(ks[18], (L, D_MODEL)),
        'mem_norm_g': gain(ks[19], (L, D_MODEL)),
        'w_xq': nrm(ks[20], (L, D_MODEL, X_INNER), D_MODEL),
        'w_xk': nrm(ks[21], (L, D_MODEL, X_INNER), D_MODEL),
        'w_xv': nrm(ks[22], (L, D_MODEL, X_INNER), D_MODEL),
        'xq_norm_g': gain(ks[23], (L, X_HEAD_DIM)),
        'xk_norm_g': gain(ks[24], (L, X_HEAD_DIM)),
        'w_xo': nrm(ks[25], (L, X_INNER, D_MODEL), X_INNER),
        'ffn_norm_g': gain(ks[26], (L, D_MODEL)),
        'w_gate': nrm(ks[27], (L, D_MODEL, FFN_HIDDEN), D_MODEL),
        'w_up': nrm(ks[28], (L, D_MODEL, FFN_HIDDEN), D_MODEL),
        'w_down': nrm(ks[29], (L, FFN_HIDDEN, D_MODEL), FFN_HIDDEN),
    }


def _fwd_reference(x, mem, positions, attn_norm_g, w_in, conv_w, conv_b, dt_bias, a_log, d_skip, ssd_norm_g,
              q_a_norm_g, w_q_b, kv_a_norm_g, w_kv_b, mla_q_norm_g, mla_k_norm_g, w_out,
              xattn_norm_g, mem_norm_g, w_xq, w_xk, w_xv, xq_norm_g, xk_norm_g, w_xo,
              ffn_norm_g, w_gate, w_up, w_down):
    cos, sin = rope_tables(positions)
    c0 = SSD_INNER
    c1 = c0 + SSD_CONV_DIM
    c2 = c1 + SSD_HEADS
    c3 = c2 + Q_LORA
    c4 = c3 + KV_LORA
    for l in range(DEPTH):
        h = rms_norm(x, attn_norm_g[l])
        proj = h @ w_in[l]
        y_ssd = ssd_mixer(proj[..., :c0], proj[..., c0:c1], proj[..., c1:c2],
                          conv_w[l], conv_b[l], dt_bias[l], a_log[l], d_skip[l], ssd_norm_g[l])
        y_mla = mla_mixer(proj[..., c2:c3], proj[..., c3:c4], proj[..., c4:], cos, sin,
                          q_a_norm_g[l], w_q_b[l], kv_a_norm_g[l], w_kv_b[l],
                          mla_q_norm_g[l], mla_k_norm_g[l])
        mixed = jnp.concatenate([y_ssd, y_mla.astype(y_ssd.dtype)], axis=-1) @ w_out[l]
        x = x + mixed.astype(x.dtype)
        h = rms_norm(x, xattn_norm_g[l])
        m = rms_norm(mem, mem_norm_g[l])
        x = x + memory_cross_attention(h, m, w_xq[l], w_xk[l], w_xv[l],
                                       xq_norm_g[l], xk_norm_g[l], w_xo[l]).astype(x.dtype)
        h = rms_norm(x, ffn_norm_g[l])
        x = x + swiglu(h, w_gate[l], w_up[l], w_down[l]).astype(x.dtype)
    return x


import jax as _jax
import jax.numpy as _jnp

TWIN_FORMAT = 'train_step'
FWD_PARAMS = ['x', 'mem', 'positions', 'attn_norm_g', 'w_in', 'conv_w', 'conv_b', 'dt_bias', 'a_log', 'd_skip', 'ssd_norm_g', 'q_a_norm_g', 'w_q_b', 'kv_a_norm_g', 'w_kv_b', 'mla_q_norm_g', 'mla_k_norm_g', 'w_out', 'xattn_norm_g', 'mem_norm_g', 'w_xq', 'w_xk', 'w_xv', 'xq_norm_g', 'xk_norm_g', 'w_xo', 'ffn_norm_g', 'w_gate', 'w_up', 'w_down']
TWIN_WEIGHTS = ['attn_norm_g', 'w_in', 'conv_w', 'conv_b', 'dt_bias', 'a_log', 'd_skip', 'ssd_norm_g', 'q_a_norm_g', 'w_q_b', 'kv_a_norm_g', 'w_kv_b', 'mla_q_norm_g', 'mla_k_norm_g', 'w_out', 'xattn_norm_g', 'mem_norm_g', 'w_xq', 'w_xk', 'w_xv', 'xq_norm_g', 'xk_norm_g', 'w_xo', 'ffn_norm_g', 'w_gate', 'w_up', 'w_down']
TWIN_DIFF_INPUT = 'x'
TWIN_INPUTS = ['x', 'mem', 'positions', 'attn_norm_g', 'w_in', 'conv_w', 'conv_b', 'dt_bias', 'a_log', 'd_skip', 'ssd_norm_g', 'q_a_norm_g', 'w_q_b', 'kv_a_norm_g', 'w_kv_b', 'mla_q_norm_g', 'mla_k_norm_g', 'w_out', 'xattn_norm_g', 'mem_norm_g', 'w_xq', 'w_xk', 'w_xv', 'xq_norm_g', 'xk_norm_g', 'w_xo', 'ffn_norm_g', 'w_gate', 'w_up', 'w_down', 'loss_target', 'm_attn_norm_g', 'm_w_in', 'm_conv_w', 'm_conv_b', 'm_dt_bias', 'm_a_log', 'm_d_skip', 'm_ssd_norm_g', 'm_q_a_norm_g', 'm_w_q_b', 'm_kv_a_norm_g', 'm_w_kv_b', 'm_mla_q_norm_g', 'm_mla_k_norm_g', 'm_w_out', 'm_xattn_norm_g', 'm_mem_norm_g', 'm_w_xq', 'm_w_xk', 'm_w_xv', 'm_xq_norm_g', 'm_xk_norm_g', 'm_w_xo', 'm_ffn_norm_g', 'm_w_gate', 'm_w_up', 'm_w_down', 'v_attn_norm_g', 'v_w_in', 'v_conv_w', 'v_conv_b', 'v_dt_bias', 'v_a_log', 'v_d_skip', 'v_ssd_norm_g', 'v_q_a_norm_g', 'v_w_q_b', 'v_kv_a_norm_g', 'v_w_kv_b', 'v_mla_q_norm_g', 'v_mla_k_norm_g', 'v_w_out', 'v_xattn_norm_g', 'v_mem_norm_g', 'v_w_xq', 'v_w_xk', 'v_w_xv', 'v_xq_norm_g', 'v_xk_norm_g', 'v_w_xo', 'v_ffn_norm_g', 'v_w_gate', 'v_w_up', 'v_w_down']
TWIN_OUTPUTS = ['loss', 'grad_x', 'grad_attn_norm_g', 'grad_w_in', 'grad_conv_w', 'grad_conv_b', 'grad_dt_bias', 'grad_a_log', 'grad_d_skip', 'grad_ssd_norm_g', 'grad_q_a_norm_g', 'grad_w_q_b', 'grad_kv_a_norm_g', 'grad_w_kv_b', 'grad_mla_q_norm_g', 'grad_mla_k_norm_g', 'grad_w_out', 'grad_xattn_norm_g', 'grad_mem_norm_g', 'grad_w_xq', 'grad_w_xk', 'grad_w_xv', 'grad_xq_norm_g', 'grad_xk_norm_g', 'grad_w_xo', 'grad_ffn_norm_g', 'grad_w_gate', 'grad_w_up', 'grad_w_down', 'delta_attn_norm_g', 'delta_w_in', 'delta_conv_w', 'delta_conv_b', 'delta_dt_bias', 'delta_a_log', 'delta_d_skip', 'delta_ssd_norm_g', 'delta_q_a_norm_g', 'delta_w_q_b', 'delta_kv_a_norm_g', 'delta_w_kv_b', 'delta_mla_q_norm_g', 'delta_mla_k_norm_g', 'delta_w_out', 'delta_xattn_norm_g', 'delta_mem_norm_g', 'delta_w_xq', 'delta_w_xk', 'delta_w_xv', 'delta_xq_norm_g', 'delta_xk_norm_g', 'delta_w_xo', 'delta_ffn_norm_g', 'delta_w_gate', 'delta_w_up', 'delta_w_down', 'new_m_attn_norm_g', 'new_m_w_in', 'new_m_conv_w', 'new_m_conv_b', 'new_m_dt_bias', 'new_m_a_log', 'new_m_d_skip', 'new_m_ssd_norm_g', 'new_m_q_a_norm_g', 'new_m_w_q_b', 'new_m_kv_a_norm_g', 'new_m_w_kv_b', 'new_m_mla_q_norm_g', 'new_m_mla_k_norm_g', 'new_m_w_out', 'new_m_xattn_norm_g', 'new_m_mem_norm_g', 'new_m_w_xq', 'new_m_w_xk', 'new_m_w_xv', 'new_m_xq_norm_g', 'new_m_xk_norm_g', 'new_m_w_xo', 'new_m_ffn_norm_g', 'new_m_w_gate', 'new_m_w_up', 'new_m_w_down', 'new_v_attn_norm_g', 'new_v_w_in', 'new_v_conv_w', 'new_v_conv_b', 'new_v_dt_bias', 'new_v_a_log', 'new_v_d_skip', 'new_v_ssd_norm_g', 'new_v_q_a_norm_g', 'new_v_w_q_b', 'new_v_kv_a_norm_g', 'new_v_w_kv_b', 'new_v_mla_q_norm_g', 'new_v_mla_k_norm_g', 'new_v_w_out', 'new_v_xattn_norm_g', 'new_v_mem_norm_g', 'new_v_w_xq', 'new_v_w_xk', 'new_v_w_xv', 'new_v_xq_norm_g', 'new_v_xk_norm_g', 'new_v_w_xo', 'new_v_ffn_norm_g', 'new_v_w_gate', 'new_v_w_up', 'new_v_w_down']
TWIN_LEAF_KINDS = {'loss': 'loss', 'grad_x': 'grad_x', 'grad_attn_norm_g': 'grad_w', 'grad_w_in': 'grad_w', 'grad_conv_w': 'grad_w', 'grad_conv_b': 'grad_w', 'grad_dt_bias': 'grad_w', 'grad_a_log': 'grad_w', 'grad_d_skip': 'grad_w', 'grad_ssd_norm_g': 'grad_w', 'grad_q_a_norm_g': 'grad_w', 'grad_w_q_b': 'grad_w', 'grad_kv_a_norm_g': 'grad_w', 'grad_w_kv_b': 'grad_w', 'grad_mla_q_norm_g': 'grad_w', 'grad_mla_k_norm_g': 'grad_w', 'grad_w_out': 'grad_w', 'grad_xattn_norm_g': 'grad_w', 'grad_mem_norm_g': 'grad_w', 'grad_w_xq': 'grad_w', 'grad_w_xk': 'grad_w', 'grad_w_xv': 'grad_w', 'grad_xq_norm_g': 'grad_w', 'grad_xk_norm_g': 'grad_w', 'grad_w_xo': 'grad_w', 'grad_ffn_norm_g': 'grad_w', 'grad_w_gate': 'grad_w', 'grad_w_up': 'grad_w', 'grad_w_down': 'grad_w', 'delta_attn_norm_g': 'delta_w', 'delta_w_in': 'delta_w', 'delta_conv_w': 'delta_w', 'delta_conv_b': 'delta_w', 'delta_dt_bias': 'delta_w', 'delta_a_log': 'delta_w', 'delta_d_skip': 'delta_w', 'delta_ssd_norm_g': 'delta_w', 'delta_q_a_norm_g': 'delta_w', 'delta_w_q_b': 'delta_w', 'delta_kv_a_norm_g': 'delta_w', 'delta_w_kv_b': 'delta_w', 'delta_mla_q_norm_g': 'delta_w', 'delta_mla_k_norm_g': 'delta_w', 'delta_w_out': 'delta_w', 'delta_xattn_norm_g': 'delta_w', 'delta_mem_norm_g': 'delta_w', 'delta_w_xq': 'delta_w', 'delta_w_xk': 'delta_w', 'delta_w_xv': 'delta_w', 'delta_xq_norm_g': 'delta_w', 'delta_xk_norm_g': 'delta_w', 'delta_w_xo': 'delta_w', 'delta_ffn_norm_g': 'delta_w', 'delta_w_gate': 'delta_w', 'delta_w_up': 'delta_w', 'delta_w_down': 'delta_w', 'new_m_attn_norm_g': 'new_m', 'new_m_w_in': 'new_m', 'new_m_conv_w': 'new_m', 'new_m_conv_b': 'new_m', 'new_m_dt_bias': 'new_m', 'new_m_a_log': 'new_m', 'new_m_d_skip': 'new_m', 'new_m_ssd_norm_g': 'new_m', 'new_m_q_a_norm_g': 'new_m', 'new_m_w_q_b': 'new_m', 'new_m_kv_a_norm_g': 'new_m', 'new_m_w_kv_b': 'new_m', 'new_m_mla_q_norm_g': 'new_m', 'new_m_mla_k_norm_g': 'new_m', 'new_m_w_out': 'new_m', 'new_m_xattn_norm_g': 'new_m', 'new_m_mem_norm_g': 'new_m', 'new_m_w_xq': 'new_m', 'new_m_w_xk': 'new_m', 'new_m_w_xv': 'new_m', 'new_m_xq_norm_g': 'new_m', 'new_m_xk_norm_g': 'new_m', 'new_m_w_xo': 'new_m', 'new_m_ffn_norm_g': 'new_m', 'new_m_w_gate': 'new_m', 'new_m_w_up': 'new_m', 'new_m_w_down': 'new_m', 'new_v_attn_norm_g': 'new_v', 'new_v_w_in': 'new_v', 'new_v_conv_w': 'new_v', 'new_v_conv_b': 'new_v', 'new_v_dt_bias': 'new_v', 'new_v_a_log': 'new_v', 'new_v_d_skip': 'new_v', 'new_v_ssd_norm_g': 'new_v', 'new_v_q_a_norm_g': 'new_v', 'new_v_w_q_b': 'new_v', 'new_v_kv_a_norm_g': 'new_v', 'new_v_w_kv_b': 'new_v', 'new_v_mla_q_norm_g': 'new_v', 'new_v_mla_k_norm_g': 'new_v', 'new_v_w_out': 'new_v', 'new_v_xattn_norm_g': 'new_v', 'new_v_mem_norm_g': 'new_v', 'new_v_w_xq': 'new_v', 'new_v_w_xk': 'new_v', 'new_v_w_xv': 'new_v', 'new_v_xq_norm_g': 'new_v', 'new_v_xk_norm_g': 'new_v', 'new_v_w_xo': 'new_v', 'new_v_ffn_norm_g': 'new_v', 'new_v_w_gate': 'new_v', 'new_v_w_up': 'new_v', 'new_v_w_down': 'new_v'}


def _forward(args):
    return _fwd_reference(*[args[k] for k in FWD_PARAMS])


def _output_shape():
    out = _jax.eval_shape(lambda: _forward(_fwd_setup_inputs(0)))
    return out.shape, out.dtype

N_MICROBATCH = 1
ADAM_LR = 0.001
ADAM_B1 = 0.9
ADAM_B2 = 0.999
ADAM_EPS = 1e-08
ADAM_WD = 0.01
ADAM_STEP = 10
PER_EXAMPLE_BATCH_AXIS = {'x': 0, 'mem': 0, 'positions': 0, 'loss_target': 0}
SHARED_INPUTS = []
_WEIGHT_DTYPES = {'attn_norm_g': _jnp.float32, 'w_in': _jnp.float32, 'conv_w': _jnp.float32, 'conv_b': _jnp.float32, 'dt_bias': _jnp.float32, 'a_log': _jnp.float32, 'd_skip': _jnp.float32, 'ssd_norm_g': _jnp.float32, 'q_a_norm_g': _jnp.float32, 'w_q_b': _jnp.float32, 'kv_a_norm_g': _jnp.float32, 'w_kv_b': _jnp.float32, 'mla_q_norm_g': _jnp.float32, 'mla_k_norm_g': _jnp.float32, 'w_out': _jnp.float32, 'xattn_norm_g': _jnp.float32, 'mem_norm_g': _jnp.float32, 'w_xq': _jnp.float32, 'w_xk': _jnp.float32, 'w_xv': _jnp.float32, 'xq_norm_g': _jnp.float32, 'xk_norm_g': _jnp.float32, 'w_xo': _jnp.float32, 'ffn_norm_g': _jnp.float32, 'w_gate': _jnp.float32, 'w_up': _jnp.float32, 'w_down': _jnp.float32}
MOMENT_SCALE = {'attn_norm_g': 6.388364e-01, 'w_in': 4.519483e-01, 'conv_w': 7.810192e-01, 'conv_b': 2.696942e+00, 'dt_bias': 1.211045e+00, 'a_log': 6.321994e+00, 'd_skip': 6.728806e+00, 'ssd_norm_g': 2.355143e+01, 'q_a_norm_g': 1.182495e-01, 'w_q_b': 6.749062e-02, 'kv_a_norm_g': 9.394532e-01, 'w_kv_b': 3.617512e-01, 'mla_q_norm_g': 4.219046e-01, 'mla_k_norm_g': 4.212039e-01, 'w_out': 9.894556e-01, 'xattn_norm_g': 5.071596e-02, 'mem_norm_g': 2.823658e-01, 'w_xq': 1.049911e-01, 'w_xk': 1.044633e-01, 'w_xv': 4.820836e-01, 'xq_norm_g': 2.194293e+00, 'xk_norm_g': 2.192003e+00, 'w_xo': 2.278421e-01, 'ffn_norm_g': 1.225779e+01, 'w_gate': 2.009276e-01, 'w_up': 1.722975e-01, 'w_down': 2.763810e-01}


def _to_microbatches(a, axis):
    t = _jnp.moveaxis(a, axis, 0)
    t = t.reshape((N_MICROBATCH, t.shape[0] // N_MICROBATCH) + t.shape[1:])
    return _jnp.moveaxis(t, 1, axis + 1)


def setup_inputs(seed: int = 0) -> dict:
    inp = _fwd_setup_inputs(seed)
    key = _jax.random.fold_in(_jax.random.key(seed), 7919)
    shape, _ = _output_shape()
    out = dict(inp)
    out["loss_target"] = _jax.random.normal(_jax.random.fold_in(key, 0), shape, _jnp.float32)
    for i, name in enumerate(TWIN_WEIGHTS):
        w = inp[name].astype(_jnp.float32)
        if MOMENT_SCALE is None:
            s = _jnp.sqrt(_jnp.mean(_jnp.square(w)) + 1e-30)
        else:
            s = MOMENT_SCALE[name]
        km, kv = _jax.random.split(_jax.random.fold_in(key, i + 1))
        out[name] = w
        out["m_" + name] = s * _jax.random.normal(km, w.shape, _jnp.float32)
        out["v_" + name] = (s * s) * _jax.random.uniform(kv, w.shape, _jnp.float32, 0.5, 1.5)
    if N_MICROBATCH > 1:
        for name, axis in PER_EXAMPLE_BATCH_AXIS.items():
            out[name] = _to_microbatches(out[name], axis)
    return {'x': out['x'], 'mem': out['mem'], 'positions': out['positions'], 'attn_norm_g': out['attn_norm_g'], 'w_in': out['w_in'], 'conv_w': out['conv_w'], 'conv_b': out['conv_b'], 'dt_bias': out['dt_bias'], 'a_log': out['a_log'], 'd_skip': out['d_skip'], 'ssd_norm_g': out['ssd_norm_g'], 'q_a_norm_g': out['q_a_norm_g'], 'w_q_b': out['w_q_b'], 'kv_a_norm_g': out['kv_a_norm_g'], 'w_kv_b': out['w_kv_b'], 'mla_q_norm_g': out['mla_q_norm_g'], 'mla_k_norm_g': out['mla_k_norm_g'], 'w_out': out['w_out'], 'xattn_norm_g': out['xattn_norm_g'], 'mem_norm_g': out['mem_norm_g'], 'w_xq': out['w_xq'], 'w_xk': out['w_xk'], 'w_xv': out['w_xv'], 'xq_norm_g': out['xq_norm_g'], 'xk_norm_g': out['xk_norm_g'], 'w_xo': out['w_xo'], 'ffn_norm_g': out['ffn_norm_g'], 'w_gate': out['w_gate'], 'w_up': out['w_up'], 'w_down': out['w_down'], 'loss_target': out['loss_target'], 'm_attn_norm_g': out['m_attn_norm_g'], 'm_w_in': out['m_w_in'], 'm_conv_w': out['m_conv_w'], 'm_conv_b': out['m_conv_b'], 'm_dt_bias': out['m_dt_bias'], 'm_a_log': out['m_a_log'], 'm_d_skip': out['m_d_skip'], 'm_ssd_norm_g': out['m_ssd_norm_g'], 'm_q_a_norm_g': out['m_q_a_norm_g'], 'm_w_q_b': out['m_w_q_b'], 'm_kv_a_norm_g': out['m_kv_a_norm_g'], 'm_w_kv_b': out['m_w_kv_b'], 'm_mla_q_norm_g': out['m_mla_q_norm_g'], 'm_mla_k_norm_g': out['m_mla_k_norm_g'], 'm_w_out': out['m_w_out'], 'm_xattn_norm_g': out['m_xattn_norm_g'], 'm_mem_norm_g': out['m_mem_norm_g'], 'm_w_xq': out['m_w_xq'], 'm_w_xk': out['m_w_xk'], 'm_w_xv': out['m_w_xv'], 'm_xq_norm_g': out['m_xq_norm_g'], 'm_xk_norm_g': out['m_xk_norm_g'], 'm_w_xo': out['m_w_xo'], 'm_ffn_norm_g': out['m_ffn_norm_g'], 'm_w_gate': out['m_w_gate'], 'm_w_up': out['m_w_up'], 'm_w_down': out['m_w_down'], 'v_attn_norm_g': out['v_attn_norm_g'], 'v_w_in': out['v_w_in'], 'v_conv_w': out['v_conv_w'], 'v_conv_b': out['v_conv_b'], 'v_dt_bias': out['v_dt_bias'], 'v_a_log': out['v_a_log'], 'v_d_skip': out['v_d_skip'], 'v_ssd_norm_g': out['v_ssd_norm_g'], 'v_q_a_norm_g': out['v_q_a_norm_g'], 'v_w_q_b': out['v_w_q_b'], 'v_kv_a_norm_g': out['v_kv_a_norm_g'], 'v_w_kv_b': out['v_w_kv_b'], 'v_mla_q_norm_g': out['v_mla_q_norm_g'], 'v_mla_k_norm_g': out['v_mla_k_norm_g'], 'v_w_out': out['v_w_out'], 'v_xattn_norm_g': out['v_xattn_norm_g'], 'v_mem_norm_g': out['v_mem_norm_g'], 'v_w_xq': out['v_w_xq'], 'v_w_xk': out['v_w_xk'], 'v_w_xv': out['v_w_xv'], 'v_xq_norm_g': out['v_xq_norm_g'], 'v_xk_norm_g': out['v_xk_norm_g'], 'v_w_xo': out['v_w_xo'], 'v_ffn_norm_g': out['v_ffn_norm_g'], 'v_w_gate': out['v_w_gate'], 'v_w_up': out['v_w_up'], 'v_w_down': out['v_w_down']}


def _loss(weights, diff, rest, loss_target):
    with _jax.named_scope("forward"):
        args = {**rest, TWIN_DIFF_INPUT: diff, **{k: w.astype(_WEIGHT_DTYPES[k]) for k, w in weights.items()}}
        y = _forward(args)
    with _jax.named_scope("loss_head"):
        err = _jnp.square(y.astype(_jnp.float32) - loss_target)
        return 0.5 * _jnp.sum(_jnp.mean(err, axis=-1)) if err.ndim else 0.5 * err


def _adamw(w, g, m, v):
    m = ADAM_B1 * m + (1.0 - ADAM_B1) * g
    v = ADAM_B2 * v + (1.0 - ADAM_B2) * _jnp.square(g)
    m_hat = m / (1.0 - ADAM_B1 ** ADAM_STEP)
    v_hat = v / (1.0 - ADAM_B2 ** ADAM_STEP)
    delta = -ADAM_LR * (m_hat / (_jnp.sqrt(v_hat) + ADAM_EPS) + ADAM_WD * w)
    return delta, m, v


def reference(x, mem, positions, attn_norm_g, w_in, conv_w, conv_b, dt_bias, a_log, d_skip, ssd_norm_g, q_a_norm_g, w_q_b, kv_a_norm_g, w_kv_b, mla_q_norm_g, mla_k_norm_g, w_out, xattn_norm_g, mem_norm_g, w_xq, w_xk, w_xv, xq_norm_g, xk_norm_g, w_xo, ffn_norm_g, w_gate, w_up, w_down, loss_target, m_attn_norm_g, m_w_in, m_conv_w, m_conv_b, m_dt_bias, m_a_log, m_d_skip, m_ssd_norm_g, m_q_a_norm_g, m_w_q_b, m_kv_a_norm_g, m_w_kv_b, m_mla_q_norm_g, m_mla_k_norm_g, m_w_out, m_xattn_norm_g, m_mem_norm_g, m_w_xq, m_w_xk, m_w_xv, m_xq_norm_g, m_xk_norm_g, m_w_xo, m_ffn_norm_g, m_w_gate, m_w_up, m_w_down, v_attn_norm_g, v_w_in, v_conv_w, v_conv_b, v_dt_bias, v_a_log, v_d_skip, v_ssd_norm_g, v_q_a_norm_g, v_w_q_b, v_kv_a_norm_g, v_w_kv_b, v_mla_q_norm_g, v_mla_k_norm_g, v_w_out, v_xattn_norm_g, v_mem_norm_g, v_w_xq, v_w_xk, v_w_xv, v_xq_norm_g, v_xk_norm_g, v_w_xo, v_ffn_norm_g, v_w_gate, v_w_up, v_w_down):
    given = dict(x=x, mem=mem, positions=positions, attn_norm_g=attn_norm_g, w_in=w_in, conv_w=conv_w, conv_b=conv_b, dt_bias=dt_bias, a_log=a_log, d_skip=d_skip, ssd_norm_g=ssd_norm_g, q_a_norm_g=q_a_norm_g, w_q_b=w_q_b, kv_a_norm_g=kv_a_norm_g, w_kv_b=w_kv_b, mla_q_norm_g=mla_q_norm_g, mla_k_norm_g=mla_k_norm_g, w_out=w_out, xattn_norm_g=xattn_norm_g, mem_norm_g=mem_norm_g, w_xq=w_xq, w_xk=w_xk, w_xv=w_xv, xq_norm_g=xq_norm_g, xk_norm_g=xk_norm_g, w_xo=w_xo, ffn_norm_g=ffn_norm_g, w_gate=w_gate, w_up=w_up, w_down=w_down, loss_target=loss_target, m_attn_norm_g=m_attn_norm_g, m_w_in=m_w_in, m_conv_w=m_conv_w, m_conv_b=m_conv_b, m_dt_bias=m_dt_bias, m_a_log=m_a_log, m_d_skip=m_d_skip, m_ssd_norm_g=m_ssd_norm_g, m_q_a_norm_g=m_q_a_norm_g, m_w_q_b=m_w_q_b, m_kv_a_norm_g=m_kv_a_norm_g, m_w_kv_b=m_w_kv_b, m_mla_q_norm_g=m_mla_q_norm_g, m_mla_k_norm_g=m_mla_k_norm_g, m_w_out=m_w_out, m_xattn_norm_g=m_xattn_norm_g, m_mem_norm_g=m_mem_norm_g, m_w_xq=m_w_xq, m_w_xk=m_w_xk, m_w_xv=m_w_xv, m_xq_norm_g=m_xq_norm_g, m_xk_norm_g=m_xk_norm_g, m_w_xo=m_w_xo, m_ffn_norm_g=m_ffn_norm_g, m_w_gate=m_w_gate, m_w_up=m_w_up, m_w_down=m_w_down, v_attn_norm_g=v_attn_norm_g, v_w_in=v_w_in, v_conv_w=v_conv_w, v_conv_b=v_conv_b, v_dt_bias=v_dt_bias, v_a_log=v_a_log, v_d_skip=v_d_skip, v_ssd_norm_g=v_ssd_norm_g, v_q_a_norm_g=v_q_a_norm_g, v_w_q_b=v_w_q_b, v_kv_a_norm_g=v_kv_a_norm_g, v_w_kv_b=v_w_kv_b, v_mla_q_norm_g=v_mla_q_norm_g, v_mla_k_norm_g=v_mla_k_norm_g, v_w_out=v_w_out, v_xattn_norm_g=v_xattn_norm_g, v_mem_norm_g=v_mem_norm_g, v_w_xq=v_w_xq, v_w_xk=v_w_xk, v_w_xv=v_w_xv, v_xq_norm_g=v_xq_norm_g, v_xk_norm_g=v_xk_norm_g, v_w_xo=v_w_xo, v_ffn_norm_g=v_ffn_norm_g, v_w_gate=v_w_gate, v_w_up=v_w_up, v_w_down=v_w_down)
    weights = {n: given[n] for n in TWIN_WEIGHTS}
    shared = {n: given[n] for n in SHARED_INPUTS}
    per_example = {n: given[n] for n in ['x', 'mem', 'positions']}
    grad_fn = _jax.value_and_grad(_loss, argnums=(0, 1))

    def one_microbatch(ex, loss_target):
        ex = dict(ex)
        diff = ex.pop(TWIN_DIFF_INPUT)
        return grad_fn(weights, diff, {**shared, **ex}, loss_target)

    if N_MICROBATCH == 1:
        loss, (grad_w, grad_x) = one_microbatch(per_example, given["loss_target"])
    else:
        def body(carry, xs):
            loss_sum, grad_sum = carry
            l_k, (gw_k, gx_k) = one_microbatch(xs[0], xs[1])
            with _jax.named_scope("update"):
                return (loss_sum + l_k, _jax.tree.map(_jnp.add, grad_sum, gw_k)), gx_k

        init = (_jnp.zeros((), _jnp.float32), _jax.tree.map(_jnp.zeros_like, weights))
        (loss, grad_w), grad_x = _jax.lax.scan(body, init, (per_example, given["loss_target"]))
    with _jax.named_scope("update"):
        delta_w, new_m, new_v = {}, {}, {}
        for n in TWIN_WEIGHTS:
            delta_w[n], new_m[n], new_v[n] = _adamw(weights[n], grad_w[n], given["m_" + n], given["v_" + n])
    return (loss, grad_x, *[grad_w[n] for n in TWIN_WEIGHTS], *[delta_w[n] for n in TWIN_WEIGHTS],
            *[new_m[n] for n in TWIN_WEIGHTS], *[new_v[n] for n in TWIN_WEIGHTS])
```

```python
import functools
import math

import numpy as np
import jax
import jax.numpy as jnp
from jax import lax
from jax.experimental import pallas as pl
from jax.experimental.pallas import tpu as pltpu

F32 = jnp.float32
BF16 = jnp.bfloat16
MESH = pl.DeviceIdType.MESH
HI = lax.Precision.HIGHEST

D_MODEL = 2048
DEPTH = 4
SSD_INNER = 1024
SSD_HEAD_DIM = 64
SSD_GROUPS = 2
SSD_HPG = 8
SSD_STATE = 128
SSD_CONV = 4
SSD_CHUNK = 128
SSD_CONV_DIM = 1536
SSD_GW = SSD_HPG * SSD_HEAD_DIM
MLA_HEADS = 8
MLA_NOPE = 128
MLA_ROPE = 64
MLA_QK = 192
MLA_V = 128
Q_LORA = 512
KV_LORA = 512
ROPE_THETA = 10000.0
X_HEADS = 4
X_HEAD_DIM = 128
X_INNER = 512
FFN = 5632
IN_COLS = 3664
EPS = 1e-6
ADAM_LR, ADAM_B1, ADAM_B2, ADAM_EPS, ADAM_WD, ADAM_STEP = 0.001, 0.9, 0.999, 1e-08, 0.01, 10

C_Z, C_XBC, C_QA, C_KVA, C_KR, C_DT, CAT_COLS = 0, 1024, 2560, 3072, 3584, 3712, 3840

LANES = 128
VMEM_LIMIT = 56 * 1024 * 1024
PACK_COLS = 1024
PACK_ALIGN = 64

SHARDED = ("w_in", "conv_w", "w_q_b", "w_kv_b", "w_out", "w_xq", "w_xk", "w_xv", "w_xo", "w_gate", "w_up", "w_down")
COL_SHARDED = ("w_in", "conv_w", "w_q_b", "w_kv_b", "w_xo", "w_gate", "w_up")
SMALL = ("attn_norm_g", "conv_b", "dt_bias", "a_log", "d_skip", "ssd_norm_g", "q_a_norm_g", "kv_a_norm_g",
         "mla_q_norm_g", "mla_k_norm_g", "xattn_norm_g", "mem_norm_g", "xq_norm_g", "xk_norm_g", "ffn_norm_g")
WEIGHTS = ("attn_norm_g", "w_in", "conv_w", "conv_b", "dt_bias", "a_log", "d_skip", "ssd_norm_g", "q_a_norm_g", "w_q_b",
           "kv_a_norm_g", "w_kv_b", "mla_q_norm_g", "mla_k_norm_g", "w_out", "xattn_norm_g", "mem_norm_g", "w_xq", "w_xk",
           "w_xv", "xq_norm_g", "xk_norm_g", "w_xo", "ffn_norm_g", "w_gate", "w_up", "w_down")


def _pcall(body, **kw):
    return pl.pallas_call(body, **kw)


def _params(sem):
    return pltpu.CompilerParams(dimension_semantics=sem, vmem_limit_bytes=VMEM_LIMIT)


def _tile(n, cands):
    for t in cands:
        if n % t == 0:
            return t
    return n


def _sigmoid(z):
    return 1.0 / (1.0 + jnp.exp(-z))


def mm(a, b, mode, name, add=None, out_dtype=F32):
    if mode == "nn":
        (M, K), (K2, N) = a.shape, b.shape
    elif mode == "nt":
        (M, K), (N, K2) = a.shape, b.shape
    else:
        (K, M), (K2, N) = a.shape, b.shape
    assert K == K2, (a.shape, b.shape, mode)
    tm = _tile(M, (512, 256, 128))
    tn = _tile(N, (1024, 768, 512, 256, 128))
    tk = _tile(K, (1024, 512, 256, 128))
    nk = K // tk
    dn = {"nn": (((1,), (0,)), ((), ())), "nt": (((1,), (1,)), ((), ())), "tn": (((0,), (0,)), ((), ()))}[mode]
    has_add = add is not None

    def body(a_ref, b_ref, *rest):
        if has_add:
            c_ref, o_ref, acc = rest
        else:
            o_ref, acc = rest
        k = pl.program_id(2)

        @pl.when(k == 0)
        def _():
            acc[...] = jnp.zeros_like(acc)

        acc[...] += lax.dot_general(a_ref[...].astype(BF16), b_ref[...].astype(BF16), dn, preferred_element_type=F32)

        @pl.when(k == nk - 1)
        def _():
            r = acc[...]
            if has_add:
                r = r + c_ref[...].astype(F32)
            o_ref[...] = r.astype(out_dtype)

    if mode == "nn":
        a_spec = pl.BlockSpec((tm, tk), lambda i, j, k: (i, k))
        b_spec = pl.BlockSpec((tk, tn), lambda i, j, k: (k, j))
    elif mode == "nt":
        a_spec = pl.BlockSpec((tm, tk), lambda i, j, k: (i, k))
        b_spec = pl.BlockSpec((tn, tk), lambda i, j, k: (j, k))
    else:
        a_spec = pl.BlockSpec((tk, tm), lambda i, j, k: (k, i))
        b_spec = pl.BlockSpec((tk, tn), lambda i, j, k: (k, j))
    o_spec = pl.BlockSpec((tm, tn), lambda i, j, k: (i, j))
    in_specs = [a_spec, b_spec] + ([o_spec] if has_add else [])
    args = (a, b) + ((add,) if has_add else ())
    return _pcall(
        body, name=name, grid=(M // tm, N // tn, nk), in_specs=in_specs, out_specs=o_spec,
        out_shape=jax.ShapeDtypeStruct((M, N), out_dtype), scratch_shapes=[pltpu.VMEM((tm, tn), F32)],
        compiler_params=_params(("parallel", "parallel", "arbitrary")))(*args)


def norm_fwd(x, gain, W, shared, name, gate=None, out_dtype=BF16):
    T, C = x.shape
    nb = C // W
    tq = _tile(T, (256, 128))
    g2 = gain.reshape(1, -1).astype(F32)
    gated = gate is not None

    def body(*refs):
        if gated:
            x_ref, z_ref, g_ref, o_ref = refs
        else:
            x_ref, g_ref, o_ref = refs
        u = x_ref[...].astype(F32)
        if gated:
            z = z_ref[...]
            u = u * (z * _sigmoid(z))
        ms = jnp.mean(u * u, axis=-1, keepdims=True)
        o_ref[...] = (u * lax.rsqrt(ms + EPS) * g_ref[...]).astype(out_dtype)

    xs = pl.BlockSpec((tq, W), lambda j, i: (i, j))
    gs = pl.BlockSpec((1, W), (lambda j, i: (0, 0)) if shared else (lambda j, i: (0, j)))
    in_specs = [xs] + ([xs] if gated else []) + [gs]
    args = (x,) + ((gate,) if gated else ()) + (g2,)
    return _pcall(body, name=name, grid=(nb, T // tq), in_specs=in_specs, out_specs=xs,
                  out_shape=jax.ShapeDtypeStruct((T, C), out_dtype),
                  compiler_params=_params(("parallel", "parallel")))(*args)


def norm_bwd(x, gain, dy, W, shared, name, gate=None, add=None):
    T, C = x.shape
    nb = C // W
    tq = _tile(T, (256, 128))
    g2 = gain.reshape(1, -1).astype(F32)
    gated = gate is not None
    has_add = add is not None
    GC = W if shared else C

    def body(*refs):
        refs = list(refs)
        x_ref = refs.pop(0)
        z_ref = refs.pop(0) if gated else None
        g_ref = refs.pop(0)
        dy_ref = refs.pop(0)
        a_ref = refs.pop(0) if has_add else None
        dx_ref = refs.pop(0)
        dz_ref = refs.pop(0) if gated else None
        dg_ref = refs.pop(0)
        j, i = pl.program_id(0), pl.program_id(1)
        first = (i == 0) & (j == 0) if shared else (i == 0)

        @pl.when(first)
        def _():
            dg_ref[...] = jnp.zeros_like(dg_ref)

        xv = x_ref[...].astype(F32)
        if gated:
            z = z_ref[...]
            s = _sigmoid(z)
            u = xv * (z * s)
        else:
            u = xv
        rstd = lax.rsqrt(jnp.mean(u * u, axis=-1, keepdims=True) + EPS)
        uh = u * rstd
        dyv = dy_ref[...].astype(F32)
        dg_ref[...] += jnp.sum(dyv * uh, axis=0, keepdims=True)
        dyg = dyv * g_ref[...]
        du = rstd * (dyg - uh * jnp.mean(dyg * uh, axis=-1, keepdims=True))
        if gated:
            dx = du * (z * s)
            dz_ref[...] = du * xv * (s * (1.0 + z * (1.0 - s)))
        else:
            dx = du
        if has_add:
            dx = dx + a_ref[...]
        dx_ref[...] = dx

    xs = pl.BlockSpec((tq, W), lambda j, i: (i, j))
    gs = pl.BlockSpec((1, W), (lambda j, i: (0, 0)) if shared else (lambda j, i: (0, j)))
    in_specs = [xs] + ([xs] if gated else []) + [gs, xs] + ([xs] if has_add else [])
    args = (x,) + ((gate,) if gated else ()) + (g2, dy) + ((add,) if has_add else ())
    out_specs = [xs] + ([xs] if gated else []) + [gs]
    out_shape = [jax.ShapeDtypeStruct((T, C), F32)] + ([jax.ShapeDtypeStruct((T, C), F32)] if gated else []) \
        + [jax.ShapeDtypeStruct((1, GC), F32)]
    return _pcall(body, name=name, grid=(nb, T // tq), in_specs=in_specs, out_specs=out_specs, out_shape=out_shape,
                  compiler_params=_params(("arbitrary", "arbitrary")))(*args)


def _inv_freq_row():
    inv = 1.0 / (ROPE_THETA ** (np.arange(0, MLA_ROPE, 2, dtype=np.float32) / MLA_ROPE))
    row = np.zeros((1, LANES), np.float32)
    row[0, :32] = inv
    row[0, 32:64] = inv
    return jnp.asarray(row)


def rope_tables(pos):
    T = pos.shape[0]
    tq = _tile(T, (512, 256, 128))

    def body(p_ref, f_ref, c_ref, s_ref):
        ang = p_ref[...] * f_ref[...]
        lane = lax.broadcasted_iota(jnp.int32, ang.shape, 1)
        c_ref[...] = jnp.where(lane < 64, jnp.cos(ang), 0.0)
        sn = jnp.sin(ang)
        s_ref[...] = jnp.where(lane < 32, -sn, jnp.where(lane < 64, sn, 0.0))

    ts = pl.BlockSpec((tq, LANES), lambda i: (i, 0))
    return _pcall(body, name="rope_tables", grid=(T // tq,),
                  in_specs=[pl.BlockSpec((tq, 1), lambda i: (i, 0)), pl.BlockSpec((1, LANES), lambda i: (0, 0))],
                  out_specs=[ts, ts], out_shape=[jax.ShapeDtypeStruct((T, LANES), F32)] * 2,
                  compiler_params=_params(("parallel",)))(pos, _inv_freq_row())


def _swap_halves(r):
    lane = lax.broadcasted_iota(jnp.int32, r.shape, 1)
    return jnp.where(lane < 32, pltpu.roll(r, 96, 1), pltpu.roll(r, 32, 1))


def qk_fwd(nope_arr, nope_blk0, rope_arr, rope_blk0, rope_shared, gn, gr, cos, sin, name):
    T = nope_arr.shape[0]
    tq = _tile(T, (256, 128))

    def body(n_ref, r_ref, gn_ref, gr_ref, c_ref, s_ref, o_ref):
        n = n_ref[...]
        r = r_ref[...]
        ms = (jnp.sum(n * n, axis=-1, keepdims=True) + jnp.sum(r * r, axis=-1, keepdims=True)) * (1.0 / MLA_QK)
        rstd = lax.rsqrt(ms + EPS)
        rn = r * rstd * gr_ref[...]
        rr = rn * c_ref[...] + _swap_halves(rn) * s_ref[...]
        o_ref[:, :LANES] = (n * rstd * gn_ref[...]).astype(BF16)
        o_ref[:, LANES:] = rr.astype(BF16)

    blk = lambda off: pl.BlockSpec((tq, LANES), lambda i, h: (i, off + h))
    rspec = pl.BlockSpec((tq, LANES), lambda i, h: (i, rope_blk0)) if rope_shared else blk(rope_blk0)
    row = pl.BlockSpec((1, LANES), lambda i, h: (0, 0))
    tab = pl.BlockSpec((tq, LANES), lambda i, h: (i, 0))
    return _pcall(body, name=name, grid=(T // tq, MLA_HEADS),
                  in_specs=[blk(nope_blk0), rspec, row, row, tab, tab],
                  out_specs=pl.BlockSpec((tq, 2 * LANES), lambda i, h: (i, h)),
                  out_shape=jax.ShapeDtypeStruct((T, MLA_HEADS * 2 * LANES), BF16),
                  compiler_params=_params(("parallel", "parallel")))(nope_arr, rope_arr, gn, gr, cos, sin)


def qk_bwd(nope_arr, nope_blk0, rope_arr, rope_blk0, rope_shared, gn, gr, cos, sin, dout, name):
    T = nope_arr.shape[0]
    tq = _tile(T, (256, 128))

    def body(n_ref, r_ref, gn_ref, gr_ref, c_ref, s_ref, d_ref, dn_ref, dr_ref, dgn_ref, dgr_ref):
        i, h = pl.program_id(0), pl.program_id(1)

        @pl.when((i == 0) & (h == 0))
        def _():
            dgn_ref[...] = jnp.zeros_like(dgn_ref)
            dgr_ref[...] = jnp.zeros_like(dgr_ref)

        n = n_ref[...]
        r = r_ref[...]
        ms = (jnp.sum(n * n, axis=-1, keepdims=True) + jnp.sum(r * r, axis=-1, keepdims=True)) * (1.0 / MLA_QK)
        rstd = lax.rsqrt(ms + EPS)
        nh = n * rstd
        rh = r * rstd
        d_n = d_ref[:, :LANES].astype(F32)
        d_rr = d_ref[:, LANES:].astype(F32)
        d_rn = d_rr * c_ref[...] - _swap_halves(d_rr) * s_ref[...]
        dgn_ref[...] += jnp.sum(d_n * nh, axis=0, keepdims=True)
        dgr_ref[...] += jnp.sum(d_rn * rh, axis=0, keepdims=True)
        dng = d_n * gn_ref[...]
        drg = d_rn * gr_ref[...]
        mean = (jnp.sum(dng * nh, axis=-1, keepdims=True) + jnp.sum(drg * rh, axis=-1, keepdims=True)) * (1.0 / MLA_QK)
        dn_ref[...] = rstd * (dng - nh * mean)
        drope = rstd * (drg - rh * mean)
        if rope_shared:
            @pl.when(h == 0)
            def _():
                dr_ref[...] = drope

            @pl.when(h != 0)
            def _():
                dr_ref[...] += drope
        else:
            dr_ref[...] = drope

    blk = lambda off: pl.BlockSpec((tq, LANES), lambda i, h: (i, off + h))
    rspec = pl.BlockSpec((tq, LANES), lambda i, h: (i, rope_blk0)) if rope_shared else blk(rope_blk0)
    row = pl.BlockSpec((1, LANES), lambda i, h: (0, 0))
    tab = pl.BlockSpec((tq, LANES), lambda i, h: (i, 0))
    dr_spec = pl.BlockSpec((tq, LANES), lambda i, h: (i, 0)) if rope_shared else blk(0)
    dr_cols = LANES if rope_shared else MLA_HEADS * LANES
    return _pcall(body, name=name, grid=(T // tq, MLA_HEADS),
                  in_specs=[blk(nope_blk0), rspec, row, row, tab, tab, pl.BlockSpec((tq, 2 * LANES), lambda i, h: (i, h))],
                  out_specs=[blk(0), dr_spec, row, row],
                  out_shape=[jax.ShapeDtypeStruct((T, MLA_HEADS * LANES), F32), jax.ShapeDtypeStruct((T, dr_cols), F32),
                             jax.ShapeDtypeStruct((1, LANES), F32), jax.ShapeDtypeStruct((1, LANES), F32)],
                  compiler_params=_params(("arbitrary", "arbitrary")))(nope_arr, rope_arr, gn, gr, cos, sin, dout)


def _scores(q, k, scale, causal, row0):
    s = lax.dot_general(q, k, (((1,), (1,)), ((), ())), preferred_element_type=F32) * scale
    if causal:
        qi = row0 + lax.broadcasted_iota(jnp.int32, s.shape, 0)
        ki = lax.broadcasted_iota(jnp.int32, s.shape, 1)
        s = jnp.where(qi >= ki, s, -1e30)
    return s


def attn_fwd(q, k, v, H, dqk, dv, scale, causal, name):
    B, S, _ = q.shape
    Sk = k.shape[1]
    tq = _tile(S, (256, 128))

    def body(q_ref, k_ref, v_ref, o_ref):
        s = _scores(q_ref[0], k_ref[0], scale, causal, pl.program_id(2) * tq)
        p = jnp.exp(s - jnp.max(s, axis=-1, keepdims=True))
        l = jnp.sum(p, axis=-1, keepdims=True)
        o = lax.dot_general(p.astype(BF16), v_ref[0], (((1,), (0,)), ((), ())), preferred_element_type=F32)
        o_ref[0] = (o / l).astype(BF16)

    return _pcall(body, name=name, grid=(B, H, S // tq),
                  in_specs=[pl.BlockSpec((1, tq, dqk), lambda b, h, i: (b, i, h)),
                            pl.BlockSpec((1, Sk, dqk), lambda b, h, i: (b, 0, h)),
                            pl.BlockSpec((1, Sk, dv), lambda b, h, i: (b, 0, h))],
                  out_specs=pl.BlockSpec((1, tq, dv), lambda b, h, i: (b, i, h)),
                  out_shape=jax.ShapeDtypeStruct((B, S, H * dv), BF16),
                  compiler_params=_params(("parallel", "parallel", "parallel")))(q, k, v)


def attn_bwd(q, k, v, do, H, dqk, dv, scale, causal, name):
    B, S, _ = q.shape
    Sk = k.shape[1]
    tq = _tile(S, (256, 128))

    def body(q_ref, k_ref, v_ref, do_ref, dq_ref, dk_ref, dv_ref):
        i = pl.program_id(2)

        @pl.when(i == 0)
        def _():
            dk_ref[...] = jnp.zeros_like(dk_ref)
            dv_ref[...] = jnp.zeros_like(dv_ref)

        qv, kv, vv = q_ref[0], k_ref[0], v_ref[0]
        s = _scores(qv, kv, scale, causal, i * tq)
        p = jnp.exp(s - jnp.max(s, axis=-1, keepdims=True))
        p = p / jnp.sum(p, axis=-1, keepdims=True)
        dob = do_ref[0].astype(BF16)
        pb = p.astype(BF16)
        dv_ref[0] += lax.dot_general(pb, dob, (((0,), (0,)), ((), ())), preferred_element_type=F32)
        dp = lax.dot_general(dob, vv, (((1,), (1,)), ((), ())), preferred_element_type=F32)
        delta = jnp.sum(p * dp, axis=-1, keepdims=True)
        ds = (p * (dp - delta) * scale).astype(BF16)
        dq_ref[0] = lax.dot_general(ds, kv, (((1,), (0,)), ((), ())), preferred_element_type=F32)
        dk_ref[0] += lax.dot_general(ds, qv, (((0,), (0,)), ((), ())), preferred_element_type=F32)

    qs = pl.BlockSpec((1, tq, dqk), lambda b, h, i: (b, i, h))
    ks = pl.BlockSpec((1, Sk, dqk), lambda b, h, i: (b, 0, h))
    vs = pl.BlockSpec((1, Sk, dv), lambda b, h, i: (b, 0, h))
    return _pcall(body, name=name, grid=(B, H, S // tq),
                  in_specs=[qs, ks, vs, pl.BlockSpec((1, tq, dv), lambda b, h, i: (b, i, h))],
                  out_specs=[qs, ks, vs],
                  out_shape=[jax.ShapeDtypeStruct((B, S, H * dqk), F32), jax.ShapeDtypeStruct((B, Sk, H * dqk), F32),
                             jax.ShapeDtypeStruct((B, Sk, H * dv), F32)],
                  compiler_params=_params(("parallel", "parallel", "arbitrary")))(q, k, v, do)


def _shift_down(u, j):
    if j == 0:
        return u
    row = lax.broadcasted_iota(jnp.int32, u.shape, 0)
    return jnp.where(row >= j, pltpu.roll(u, j, 0), 0.0)


def _shift_up(u, j):
    if j == 0:
        return u
    n = u.shape[0]
    row = lax.broadcasted_iota(jnp.int32, u.shape, 0)
    return jnp.where(row < n - j, pltpu.roll(u, n - j, 0), 0.0)


def conv_fwd(u, col0_blk, w8, b, name):
    B, S, _ = u.shape
    tc = 256
    nb = SSD_CONV_DIM // tc

    def body(u_ref, w_ref, b_ref, o_ref):
        uv = u_ref[0]
        pre = b_ref[...] + jnp.zeros_like(uv)
        for kk in range(SSD_CONV):
            pre = pre + w_ref[kk:kk + 1, :] * _shift_down(uv, SSD_CONV - 1 - kk)
        o_ref[0] = pre * _sigmoid(pre)

    return _pcall(body, name=name, grid=(B, nb),
                  in_specs=[pl.BlockSpec((1, S, tc), lambda bb, j: (bb, 0, col0_blk + j)),
                            pl.BlockSpec((8, tc), lambda bb, j: (0, j)), pl.BlockSpec((1, tc), lambda bb, j: (0, j))],
                  out_specs=pl.BlockSpec((1, S, tc), lambda bb, j: (bb, 0, j)),
                  out_shape=jax.ShapeDtypeStruct((B, S, SSD_CONV_DIM), F32),
                  compiler_params=_params(("parallel", "parallel")))(u, w8, b)


def conv_bwd(u, col0_blk, w8, b, dy, name):
    B, S, _ = u.shape
    tc = 256
    nb = SSD_CONV_DIM // tc

    def body(u_ref, w_ref, b_ref, dy_ref, du_ref, dw_ref, db_ref):
        bb = pl.program_id(1)

        @pl.when(bb == 0)
        def _():
            dw_ref[...] = jnp.zeros_like(dw_ref)
            db_ref[...] = jnp.zeros_like(db_ref)

        uv = u_ref[0]
        pre = b_ref[...] + jnp.zeros_like(uv)
        for kk in range(SSD_CONV):
            pre = pre + w_ref[kk:kk + 1, :] * _shift_down(uv, SSD_CONV - 1 - kk)
        s = _sigmoid(pre)
        dpre = dy_ref[0] * (s * (1.0 + pre * (1.0 - s)))
        du = jnp.zeros_like(uv)
        for kk in range(SSD_CONV):
            j = SSD_CONV - 1 - kk
            du = du + w_ref[kk:kk + 1, :] * _shift_up(dpre, j)
            dw_ref[kk:kk + 1, :] += jnp.sum(dpre * _shift_down(uv, j), axis=0, keepdims=True)
        db_ref[...] += jnp.sum(dpre, axis=0, keepdims=True)
        du_ref[0] = du

    return _pcall(body, name=name, grid=(nb, B),
                  in_specs=[pl.BlockSpec((1, S, tc), lambda j, bb: (bb, 0, col0_blk + j)),
                            pl.BlockSpec((8, tc), lambda j, bb: (0, j)), pl.BlockSpec((1, tc), lambda j, bb: (0, j)),
                            pl.BlockSpec((1, S, tc), lambda j, bb: (bb, 0, j))],
                  out_specs=[pl.BlockSpec((1, S, tc), lambda j, bb: (bb, 0, j)),
                             pl.BlockSpec((8, tc), lambda j, bb: (0, j)), pl.BlockSpec((1, tc), lambda j, bb: (0, j))],
                  out_shape=[jax.ShapeDtypeStruct((B, S, SSD_CONV_DIM), F32), jax.ShapeDtypeStruct((8, SSD_CONV_DIM), F32),
                             jax.ShapeDtypeStruct((1, SSD_CONV_DIM), F32)],
                  compiler_params=_params(("arbitrary", "arbitrary")))(u, w8, b, dy)


def _expand_mat():
    e = np.zeros((LANES, SSD_GW), np.float32)
    for h in range(SSD_HPG):
        e[h, h * SSD_HEAD_DIM:(h + 1) * SSD_HEAD_DIM] = 1.0
    return jnp.asarray(e)


def _tri_mat():
    return jnp.asarray(np.tril(np.ones((SSD_CHUNK, SSD_CHUNK), np.float32)))


def _dotf(a, b, dn=(((1,), (0,)), ((), ()))):
    return lax.dot_general(a, b, dn, precision=HI, preferred_element_type=F32)


def _dotb(a, b, dn=(((1,), (0,)), ((), ()))):
    return lax.dot_general(a.astype(BF16), b.astype(BF16), dn, preferred_element_type=F32)


NT = (((1,), (1,)), ((), ()))
TN = (((0,), (0,)), ((), ()))


def _ssd_common(dt_ref, par_ref, tri_ref, e_ref):
    dtr = dt_ref[0]
    xb = dtr + par_ref[0:1, :]
    dt = jnp.maximum(xb, 0.0) + jnp.log(1.0 + jnp.exp(-jnp.abs(xb)))
    A = -jnp.exp(par_ref[1:2, :])
    a = dt * A
    acs = _dotf(tri_ref[...], a)
    alast = acs[SSD_CHUNK - 1:SSD_CHUNK, :]
    E = e_ref[...]
    acsE = _dotf(acs, E)
    alastE = acsE[SSD_CHUNK - 1:SSD_CHUNK, :]
    return dict(xb=xb, dt=dt, A=A, acs=acs, acsT=acs.T, ealast=jnp.exp(alast), w=jnp.exp(alast - acs),
                dtE=_dotf(dt, E), eacsE=jnp.exp(acsE), wE=jnp.exp(alastE - acsE), ealastE=jnp.exp(alastE),
                DE=_dotf(par_ref[...], E)[2:3, :])


def _decay(c, e):
    diff = c["acs"][:, e:e + 1] - c["acsT"][e:e + 1, :]
    li = lax.broadcasted_iota(jnp.int32, diff.shape, 0)
    si = lax.broadcasted_iota(jnp.int32, diff.shape, 1)
    return jnp.where(li >= si, jnp.exp(jnp.minimum(diff, 0.0)), 0.0)


def ssd_fwd(xbc, dtp, par, name):
    B, S, _ = xbc.shape
    nc = S // SSD_CHUNK
    L, N, GW, P = SSD_CHUNK, SSD_STATE, SSD_GW, SSD_HEAD_DIM

    def body(x_ref, b_ref, c_ref, dt_ref, par_ref, tri_ref, e_ref, y_ref, st_ref, S_scr):
        @pl.when(pl.program_id(2) == 0)
        def _():
            S_scr[...] = jnp.zeros_like(S_scr)

        Sin = S_scr[...]
        st_ref[0, 0, 0] = Sin
        c = _ssd_common(dt_ref, par_ref.at[0], tri_ref, e_ref)
        X = x_ref[0]
        Bm = b_ref[0]
        Cm = c_ref[0]
        xdt = X * c["dtE"]
        CB = _dotb(Cm, Bm, NT)
        y_ref[0] = c["eacsE"] * _dotb(Cm, Sin) + c["DE"] * X
        for e in range(SSD_HPG):
            M = CB * _decay(c, e)
            y_ref[0, :, e * P:(e + 1) * P] += _dotb(M, xdt[:, e * P:(e + 1) * P])
        S_scr[...] = c["ealastE"] * Sin + _dotb(Bm, xdt * c["wE"], TN)

    return _pcall(
        body, name=name, grid=(SSD_GROUPS, B, nc),
        in_specs=[pl.BlockSpec((1, L, GW), lambda g, b, ci: (b, ci, g)),
                  pl.BlockSpec((1, L, N), lambda g, b, ci: (b, ci, 8 + g)),
                  pl.BlockSpec((1, L, N), lambda g, b, ci: (b, ci, 10 + g)),
                  pl.BlockSpec((1, L, LANES), lambda g, b, ci: (b, ci, g)),
                  pl.BlockSpec((1, 8, LANES), lambda g, b, ci: (g, 0, 0)),
                  pl.BlockSpec((L, L), lambda g, b, ci: (0, 0)),
                  pl.BlockSpec((LANES, GW), lambda g, b, ci: (0, 0))],
        out_specs=[pl.BlockSpec((1, L, GW), lambda g, b, ci: (b, ci, g)),
                   pl.BlockSpec((1, 1, 1, N, GW), lambda g, b, ci: (g, b, ci, 0, 0))],
        out_shape=[jax.ShapeDtypeStruct((B, S, SSD_INNER), F32), jax.ShapeDtypeStruct((SSD_GROUPS, B, nc, N, GW), F32)],
        scratch_shapes=[pltpu.VMEM((N, GW), F32)],
        compiler_params=_params(("arbitrary", "arbitrary", "arbitrary")))(xbc, xbc, xbc, dtp, par, _tri_mat(), _expand_mat())


def ssd_bwd(xbc, dtp, par, states, dy, name):
    B, S, _ = xbc.shape
    nc = S // SSD_CHUNK
    L, N, GW, P = SSD_CHUNK, SSD_STATE, SSD_GW, SSD_HEAD_DIM

    def body(x_ref, b_ref, c_ref, dt_ref, par_ref, tri_ref, e_ref, st_ref, dy_ref,
             dx_ref, db_ref, dc_ref, ddt_ref, dpar_ref, dS_scr, dxdt_scr):
        @pl.when(pl.program_id(2) == 0)
        def _():
            dS_scr[...] = jnp.zeros_like(dS_scr)

        @pl.when((pl.program_id(1) == 0) & (pl.program_id(2) == 0))
        def _():
            dpar_ref[...] = jnp.zeros_like(dpar_ref)

        c = _ssd_common(dt_ref, par_ref.at[0], tri_ref, e_ref)
        E = e_ref[...]
        red = lambda t: _dotf(t, E, NT)
        red_row = lambda t: red(jnp.broadcast_to(t, (8, GW)))[0:1, :]
        X, Bm, Cm = x_ref[0], b_ref[0], c_ref[0]
        Sin = st_ref[0, 0, 0]
        dY = dy_ref[0]
        dSo = dS_scr[...]
        xdt = X * c["dtE"]
        CB = _dotb(Cm, Bm, NT)
        dD = red_row(jnp.sum(dY * X, axis=0, keepdims=True))
        yoff = c["eacsE"] * _dotb(Cm, Sin)
        dacs = red(dY * yoff)
        dYe = dY * c["eacsE"]
        dC = _dotb(dYe, Sin, NT)
        dSin = _dotb(Cm, dYe, TN) + c["ealastE"] * dSo
        dB = _dotb(xdt * c["wE"], dSo, NT)
        dxw = _dotb(Bm, dSo)
        t = red(dxw * xdt) * c["w"]
        dacs = dacs - t
        dalast = jnp.sum(t, axis=0, keepdims=True) + c["ealast"] * red_row(jnp.sum(Sin * dSo, axis=0, keepdims=True))
        dxdt_scr[...] = dxw * c["wE"]
        dCB = jnp.zeros((L, L), F32)
        dacsT = jnp.zeros((LANES, L), F32)
        lane = lax.broadcasted_iota(jnp.int32, (1, LANES), 1)
        subl = lax.broadcasted_iota(jnp.int32, (LANES, 1), 0)
        for e in range(SSD_HPG):
            Lm = _decay(c, e)
            M = CB * Lm
            dYh = dY[:, e * P:(e + 1) * P]
            dM = _dotb(dYh, xdt[:, e * P:(e + 1) * P], NT)
            dxdt_scr[:, e * P:(e + 1) * P] += _dotb(M, dYh, TN)
            dCBe = dM * Lm
            dCB = dCB + dCBe
            Gm = dCBe * CB
            dacs = dacs + jnp.sum(Gm, axis=1, keepdims=True) * (lane == e).astype(F32)
            dacsT = dacsT - jnp.sum(Gm, axis=0, keepdims=True) * (subl == e).astype(F32)
        dC = dC + _dotb(dCB, Bm)
        dB = dB + _dotb(dCB, Cm, TN)
        dacs = dacs + dacsT.T
        row = lax.broadcasted_iota(jnp.int32, (L, 1), 0)
        dacs = dacs + jnp.where(row == L - 1, dalast, 0.0)
        da = _dotf(tri_ref[...], dacs, TN)
        dxdt = dxdt_scr[...]
        ddt = da * c["A"] + red(dxdt * X)
        dA = jnp.sum(da * c["dt"], axis=0, keepdims=True)
        ddtr = ddt * _sigmoid(c["xb"])
        dx_ref[0] = c["DE"] * dY + dxdt * c["dtE"]
        db_ref[0] = dB
        dc_ref[0] = dC
        ddt_ref[0] = ddtr
        dpar_ref[0, 0:1, :] += jnp.sum(ddtr, axis=0, keepdims=True)
        dpar_ref[0, 1:2, :] += dA * c["A"]
        dpar_ref[0, 2:3, :] += dD
        dS_scr[...] = dSin

    rc = lambda ci: nc - 1 - ci
    xspec = pl.BlockSpec((1, L, GW), lambda g, b, ci: (b, rc(ci), g))
    return _pcall(
        body, name=name, grid=(SSD_GROUPS, B, nc),
        in_specs=[xspec,
                  pl.BlockSpec((1, L, N), lambda g, b, ci: (b, rc(ci), 8 + g)),
                  pl.BlockSpec((1, L, N), lambda g, b, ci: (b, rc(ci), 10 + g)),
                  pl.BlockSpec((1, L, LANES), lambda g, b, ci: (b, rc(ci), g)),
                  pl.BlockSpec((1, 8, LANES), lambda g, b, ci: (g, 0, 0)),
                  pl.BlockSpec((L, L), lambda g, b, ci: (0, 0)),
                  pl.BlockSpec((LANES, GW), lambda g, b, ci: (0, 0)),
                  pl.BlockSpec((1, 1, 1, N, GW), lambda g, b, ci: (g, b, rc(ci), 0, 0)),
                  xspec],
        out_specs=[xspec,
                   pl.BlockSpec((1, L, N), lambda g, b, ci: (b, rc(ci), g)),
                   pl.BlockSpec((1, L, N), lambda g, b, ci: (b, rc(ci), g)),
                   pl.BlockSpec((1, L, LANES), lambda g, b, ci: (b, rc(ci), g)),
                   pl.BlockSpec((1, 8, LANES), lambda g, b, ci: (g, 0, 0))],
        out_shape=[jax.ShapeDtypeStruct((B, S, SSD_INNER), F32), jax.ShapeDtypeStruct((B, S, SSD_GROUPS * N), F32),
                   jax.ShapeDtypeStruct((B, S, SSD_GROUPS * N), F32), jax.ShapeDtypeStruct((B, S, SSD_GROUPS * LANES), F32),
                   jax.ShapeDtypeStruct((SSD_GROUPS, 8, LANES), F32)],
        scratch_shapes=[pltpu.VMEM((N, GW), F32), pltpu.VMEM((L, GW), F32)],
        compiler_params=_params(("arbitrary", "arbitrary", "arbitrary")))(
            xbc, xbc, xbc, dtp, par, _tri_mat(), _expand_mat(), states, dy)


def swiglu_fwd(gu, name):
    T = gu.shape[0]
    tq, tc = _tile(T, (512, 256, 128)), 512
    nf = FFN // tc

    def body(g_ref, u_ref, o_ref):
        g = g_ref[...].astype(F32)
        o_ref[...] = (g * _sigmoid(g) * u_ref[...].astype(F32)).astype(BF16)

    return _pcall(body, name=name, grid=(T // tq, nf),
                  in_specs=[pl.BlockSpec((tq, tc), lambda i, j: (i, j)), pl.BlockSpec((tq, tc), lambda i, j: (i, nf + j))],
                  out_specs=pl.BlockSpec((tq, tc), lambda i, j: (i, j)), out_shape=jax.ShapeDtypeStruct((T, FFN), BF16),
                  compiler_params=_params(("parallel", "parallel")))(gu, gu)


def swiglu_bwd(gu, dact, name):
    T = gu.shape[0]
    tq, tc = _tile(T, (512, 256, 128)), 512
    nf = FFN // tc

    def body(g_ref, u_ref, d_ref, dg_ref, du_ref):
        g = g_ref[...].astype(F32)
        u = u_ref[...].astype(F32)
        d = d_ref[...]
        s = _sigmoid(g)
        dg_ref[...] = (d * u * (s * (1.0 + g * (1.0 - s)))).astype(BF16)
        du_ref[...] = (d * g * s).astype(BF16)

    lo = pl.BlockSpec((tq, tc), lambda i, j: (i, j))
    hi = pl.BlockSpec((tq, tc), lambda i, j: (i, nf + j))
    dg, du = _pcall(body, name=name, grid=(T // tq, nf), in_specs=[lo, hi, lo], out_specs=[lo, lo],
                    out_shape=[jax.ShapeDtypeStruct((T, FFN), BF16)] * 2,
                    compiler_params=_params(("parallel", "parallel")))(gu, gu, dact)
    return jnp.concatenate([dg, du], axis=1)


def loss_head(y, target):
    T, D = y.shape
    tq = _tile(T, (256, 128))

    def body(y_ref, t_ref, dy_ref, l_ref):
        @pl.when(pl.program_id(0) == 0)
        def _():
            l_ref[...] = jnp.zeros_like(l_ref)

        e = y_ref[...] - t_ref[...]
        dy_ref[...] = e * (1.0 / D)
        l_ref[...] += 0.5 * jnp.sum(jnp.mean(e * e, axis=-1, keepdims=True))

    xs = pl.BlockSpec((tq, D), lambda i: (i, 0))
    return _pcall(body, name="loss_head", grid=(T // tq,), in_specs=[xs, xs],
                  out_specs=[xs, pl.BlockSpec((8, LANES), lambda i: (0, 0))],
                  out_shape=[jax.ShapeDtypeStruct((T, D), F32), jax.ShapeDtypeStruct((8, LANES), F32)],
                  compiler_params=_params(("arbitrary",)))(y, target)


def sum_arrays(arrs, out_dtype, name):
    R, C = arrs[0].shape
    tr = _tile(R, (512, 256, 128, 64, 32, 16, 8))
    n = len(arrs)

    def body(*refs):
        acc = refs[0][...].astype(F32)
        for r in refs[1:n]:
            acc = acc + r[...].astype(F32)
        refs[n][...] = acc.astype(out_dtype)

    xs = pl.BlockSpec((tr, C), lambda i: (i, 0))
    return _pcall(body, name=name, grid=(R // tr,), in_specs=[xs] * n, out_specs=xs,
                  out_shape=jax.ShapeDtypeStruct((R, C), out_dtype), compiler_params=_params(("parallel",)))(*arrs)


def adamw(w, g, m, v, name):
    R, C = w.shape
    tr = _tile(R, (256, 128, 64, 32, 16, 8))
    c1 = 1.0 - ADAM_B1 ** ADAM_STEP
    c2 = 1.0 - ADAM_B2 ** ADAM_STEP

    def body(w_ref, g_ref, m_ref, v_ref, d_ref, nm_ref, nv_ref):
        gv = g_ref[...]
        mn = ADAM_B1 * m_ref[...] + (1.0 - ADAM_B1) * gv
        vn = ADAM_B2 * v_ref[...] + (1.0 - ADAM_B2) * (gv * gv)
        nm_ref[...] = mn
        nv_ref[...] = vn
        d_ref[...] = -ADAM_LR * ((mn / c1) / (jnp.sqrt(vn / c2) + ADAM_EPS) + ADAM_WD * w_ref[...])

    xs = pl.BlockSpec((tr, C), lambda i: (i, 0))
    return _pcall(body, name=name, grid=(R // tr,), in_specs=[xs] * 4, out_specs=[xs] * 3,
                  out_shape=[jax.ShapeDtypeStruct((R, C), F32)] * 3, compiler_params=_params(("parallel",)))(w, g, m, v)


ANY = pl.BlockSpec(memory_space=pl.ANY)


def _place():
    x, y, c = lax.axis_index("x"), lax.axis_index("y"), lax.axis_index("c")
    chips = [(1 - x, y), (x, 1 - y), (1 - x, 1 - y)]
    return x, y, c, chips


def _rcopy(src, dst, ssem, rsem, dev):
    return pltpu.make_async_remote_copy(src_ref=src, dst_ref=dst, send_sem=ssem, recv_sem=rsem, device_id=dev,
                                        device_id_type=MESH)


def _half(c, h):
    return pl.ds(pl.multiple_of(c * h, PACK_ALIGN // 2), h)


def all_gather_chips(flat, name):
    R, C = flat.shape
    h = R // 2

    def body(src, out, ssem, rsem, lsem):
        x, y, c, chips = _place()
        me = 2 * x + y
        sib = (x, y, 1 - c)
        mine, other = _half(c, h), _half(1 - c, h)
        local = pltpu.make_async_copy(src, out.at[me], lsem)
        local.start()
        sends = []
        for j, (px, py) in enumerate(chips):
            cp = _rcopy(src.at[mine], out.at[me, mine], ssem.at[j], rsem.at[j], (px, py, c))
            cp.start()
            sends.append(cp)
        for j, (px, py) in enumerate(chips):
            slot = 2 * px + py
            _rcopy(src.at[mine], out.at[slot, mine], ssem.at[j], rsem.at[j], (px, py, c)).wait_recv()
            cp = _rcopy(out.at[slot, mine], out.at[slot, mine], ssem.at[3 + j], rsem.at[3 + j], sib)
            cp.start()
            sends.append(cp)
        for j, (px, py) in enumerate(chips):
            slot = 2 * px + py
            _rcopy(out.at[slot, other], out.at[slot, other], ssem.at[3 + j], rsem.at[3 + j], sib).wait_recv()
        for cp in sends:
            cp.wait_send()
        local.wait()

    return _pcall(body, name=name, in_specs=[ANY], out_specs=ANY, out_shape=jax.ShapeDtypeStruct((4, R, C), flat.dtype),
                  scratch_shapes=[pltpu.SemaphoreType.DMA((6,)), pltpu.SemaphoreType.DMA((6,)), pltpu.SemaphoreType.DMA(())])(flat)


def exchange_sibling_half(g, name):
    _, R, C = g.shape
    h = R // 2

    def body(src, out, ssem, rsem):
        x, y, c, _ = _place()
        cp = _rcopy(src.at[:, _half(1 - c, h)], out, ssem, rsem, (x, y, 1 - c))
        cp.start()
        cp.wait()

    return _pcall(body, name=name, in_specs=[ANY], out_specs=ANY, out_shape=jax.ShapeDtypeStruct((4, h, C), g.dtype),
                  scratch_shapes=[pltpu.SemaphoreType.DMA(()), pltpu.SemaphoreType.DMA(())])(g)


def scatter_chips(p, name):
    _, R, C = p.shape

    def body(src, out, ssem, rsem, lsem):
        x, y, c, chips = _place()
        me = 2 * x + y
        local = pltpu.make_async_copy(src.at[me], out.at[me], lsem)
        local.start()
        sends = []
        for j, (px, py) in enumerate(chips):
            cp = _rcopy(src.at[2 * px + py], out.at[me], ssem.at[j], rsem.at[j], (px, py, c))
            cp.start()
            sends.append(cp)
        for j, (px, py) in enumerate(chips):
            _rcopy(src.at[me], out.at[2 * px + py], ssem.at[j], rsem.at[j], (px, py, c)).wait_recv()
        for cp in sends:
            cp.wait_send()
        local.wait()

    return _pcall(body, name=name, in_specs=[ANY], out_specs=ANY, out_shape=jax.ShapeDtypeStruct((4, R, C), p.dtype),
                  scratch_shapes=[pltpu.SemaphoreType.DMA((3,)), pltpu.SemaphoreType.DMA((3,)), pltpu.SemaphoreType.DMA(())])(p)


def join_sibling_halves(r, name):
    h, C = r.shape

    def body(src, out, ssem, rsem, lsem):
        x, y, c, _ = _place()
        local = pltpu.make_async_copy(src, out.at[_half(c, h)], lsem)
        local.start()
        cp = _rcopy(src, out.at[_half(c, h)], ssem, rsem, (x, y, 1 - c))
        cp.start()
        _rcopy(src, out.at[_half(1 - c, h)], ssem, rsem, (x, y, 1 - c)).wait_recv()
        cp.wait_send()
        local.wait()

    return _pcall(body, name=name, in_specs=[ANY], out_specs=ANY, out_shape=jax.ShapeDtypeStruct((2 * h, C), r.dtype),
                  scratch_shapes=[pltpu.SemaphoreType.DMA(()), pltpu.SemaphoreType.DMA(()), pltpu.SemaphoreType.DMA(())])(r)


def gather_all_devices(v, name):
    R, C = v.shape

    def body(src, out, ssem, rsem, lsem):
        x, y, c, _ = _place()
        me = 4 * x + 2 * y + c
        local = pltpu.make_async_copy(src, out.at[me], lsem)
        local.start()
        sends = []
        for j in range(1, 8):
            dx, dy, dc = (j >> 2) & 1, (j >> 1) & 1, j & 1
            px, py, pc = (1 - x if dx else x), (1 - y if dy else y), (1 - c if dc else c)
            cp = _rcopy(src, out.at[me], ssem.at[j - 1], rsem.at[j - 1], (px, py, pc))
            cp.start()
            sends.append(cp)
        for j in range(1, 8):
            dx, dy, dc = (j >> 2) & 1, (j >> 1) & 1, j & 1
            px, py, pc = (1 - x if dx else x), (1 - y if dy else y), (1 - c if dc else c)
            _rcopy(src, out.at[4 * px + 2 * py + pc], ssem.at[j - 1], rsem.at[j - 1], (px, py, pc)).wait_recv()
        for cp in sends:
            cp.wait_send()
        local.wait()

    return _pcall(body, name=name, in_specs=[ANY], out_specs=ANY, out_shape=jax.ShapeDtypeStruct((8, R, C), v.dtype),
                  scratch_shapes=[pltpu.SemaphoreType.DMA((7,)), pltpu.SemaphoreType.DMA((7,)), pltpu.SemaphoreType.DMA(())])(v)


def reduce_scatter_grads(g, tag):
    _, R, C = g.shape
    h = R // 2
    got = exchange_sibling_half(g, "rs_pair_" + tag)
    mine = lax.dynamic_slice_in_dim(g, lax.axis_index("c") * h, h, axis=1)
    p = sum_arrays([mine.reshape(4 * h, C), got.reshape(4 * h, C)], BF16, "rs_add_pair_" + tag).reshape(4, h, C)
    b = scatter_chips(p, "rs_chips_" + tag)
    r = sum_arrays([b[0], b[1], b[2], b[3]], F32, "rs_add_chips_" + tag)
    return join_sibling_halves(r, "rs_join_" + tag)


def _shard_shape(name, full):
    r, cdim = full
    return (r, cdim // 4) if name in COL_SHARDED else (r // 4, cdim)


FULL_SHAPES = {"w_in": (D_MODEL, IN_COLS), "conv_w": (SSD_CONV, SSD_CONV_DIM), "w_q_b": (Q_LORA, MLA_HEADS * MLA_QK),
               "w_kv_b": (KV_LORA, MLA_HEADS * (MLA_NOPE + MLA_V)), "w_out": (D_MODEL, D_MODEL), "w_xq": (D_MODEL, X_INNER),
               "w_xk": (D_MODEL, X_INNER), "w_xv": (D_MODEL, X_INNER), "w_xo": (X_INNER, D_MODEL), "w_gate": (D_MODEL, FFN),
               "w_up": (D_MODEL, FFN), "w_down": (FFN, D_MODEL)}


def _pack_rows():
    n = sum(int(np.prod(_shard_shape(k, FULL_SHAPES[k]))) for k in SHARDED)
    rows = -(-n // PACK_COLS)
    return -(-rows // PACK_ALIGN) * PACK_ALIGN, n


def pack_shards(shards, dtype):
    rows, n = _pack_rows()
    flat = jnp.concatenate([shards[k].reshape(-1).astype(dtype) for k in SHARDED] + [jnp.zeros((rows * PACK_COLS - n,), dtype)])
    return flat.reshape(rows, PACK_COLS)


def unpack_shards(flat):
    out, off = {}, 0
    v = flat.reshape(-1)
    for k in SHARDED:
        r, cdim = _shard_shape(k, FULL_SHAPES[k])
        out[k] = v[off:off + r * cdim].reshape(r, cdim)
        off += r * cdim
    return out


def unpack_full(gathered):
    out, off = {}, 0
    v = gathered.reshape(4, -1)
    for k in SHARDED:
        r, cdim = _shard_shape(k, FULL_SHAPES[k])
        blk = v[:, off:off + r * cdim].reshape(4, r, cdim)
        out[k] = blk.transpose(1, 0, 2).reshape(r, 4 * cdim) if k in COL_SHARDED else blk.reshape(4 * r, cdim)
        off += r * cdim
    return out


def pack_full(full, dtype):
    rows, n = _pack_rows()
    parts = []
    for k in SHARDED:
        R, Cc = FULL_SHAPES[k]
        a = full[k].astype(dtype)
        if k in COL_SHARDED:
            parts.append(a.reshape(R, 4, Cc // 4).transpose(1, 0, 2).reshape(4, -1))
        else:
            parts.append(a.reshape(4, -1))
    parts.append(jnp.zeros((4, rows * PACK_COLS - n), dtype))
    return jnp.concatenate(parts, axis=1).reshape(4, rows, PACK_COLS)


def kernel_weights(full, conv_w_full):
    w_in = full["w_in"]
    zc = lambda n: jnp.zeros((w_in.shape[0], n), w_in.dtype)
    wcat = jnp.concatenate([w_in[:, :2560], w_in[:, 2576:], zc(64), w_in[:, 2560:2576], zc(112)], axis=1)
    wq = full["w_q_b"].reshape(Q_LORA, MLA_HEADS, MLA_QK)
    wq = jnp.concatenate([wq[:, :, :MLA_NOPE].reshape(Q_LORA, -1),
                          jnp.pad(wq[:, :, MLA_NOPE:], ((0, 0), (0, 0), (0, LANES - MLA_ROPE))).reshape(Q_LORA, -1)], axis=1)
    wkv = full["w_kv_b"].reshape(KV_LORA, MLA_HEADS, MLA_NOPE + MLA_V)
    wkv = jnp.concatenate([wkv[:, :, :MLA_NOPE].reshape(KV_LORA, -1), wkv[:, :, MLA_NOPE:].reshape(KV_LORA, -1)], axis=1)
    return dict(wcat=wcat, wq=wq, wkv=wkv, w_out=full["w_out"], w_xq=full["w_xq"],
                w_xkv=jnp.concatenate([full["w_xk"], full["w_xv"]], axis=1), w_xo=full["w_xo"],
                w_gu=jnp.concatenate([full["w_gate"], full["w_up"]], axis=1), w_down=full["w_down"],
                conv_w=jnp.pad(conv_w_full, ((0, 8 - SSD_CONV), (0, 0))))


def full_grads(g):
    dwcat = g["wcat"]
    dwq = g["wq"]
    dq_n = dwq[:, :1024].reshape(Q_LORA, MLA_HEADS, MLA_NOPE)
    dq_r = dwq[:, 1024:].reshape(Q_LORA, MLA_HEADS, LANES)[:, :, :MLA_ROPE]
    dwkv = g["wkv"]
    dk = dwkv[:, :1024].reshape(KV_LORA, MLA_HEADS, MLA_NOPE)
    dv = dwkv[:, 1024:].reshape(KV_LORA, MLA_HEADS, MLA_V)
    return {"w_in": jnp.concatenate([dwcat[:, :2560], dwcat[:, C_DT:C_DT + 16], dwcat[:, 2560:3648]], axis=1),
            "conv_w": g["conv_w"][:SSD_CONV],
            "w_q_b": jnp.concatenate([dq_n, dq_r], axis=2).reshape(Q_LORA, -1),
            "w_kv_b": jnp.concatenate([dk, dv], axis=2).reshape(KV_LORA, -1),
            "w_out": g["w_out"], "w_xq": g["w_xq"], "w_xk": g["w_xkv"][:, :X_INNER], "w_xv": g["w_xkv"][:, X_INNER:],
            "w_xo": g["w_xo"], "w_gate": g["w_gu"][:, :FFN], "w_up": g["w_gu"][:, FFN:], "w_down": g["w_down"]}


def _ssd_par(sm):
    rows = jnp.stack([sm["dt_bias"], sm["a_log"], sm["d_skip"]]).reshape(3, SSD_GROUPS, SSD_HPG).transpose(1, 0, 2)
    return jnp.pad(rows, ((0, 0), (0, 5), (0, LANES - SSD_HPG)))


def _dt_pad(proj, B, S):
    dt = proj[:, C_DT:C_DT + 16].reshape(B, S, SSD_GROUPS, SSD_HPG)
    return jnp.pad(dt, ((0, 0), (0, 0), (0, 0), (0, LANES - SSD_HPG))).reshape(B, S, SSD_GROUPS * LANES)


def _mla_gains(g):
    return g[:MLA_NOPE].reshape(1, LANES), jnp.pad(g[MLA_NOPE:], (0, LANES - MLA_ROPE)).reshape(1, LANES)


def layer_fwd(x, memf, cos, sin, kw, sm, B, S, l):
    T = x.shape[0]
    n = lambda s: "%s_l%d" % (s, l)
    sv = dict(x0=x)
    h1 = norm_fwd(x, sm["attn_norm_g"], D_MODEL, False, n("attn_norm"))
    proj = mm(h1, kw["wcat"], "nn", n("in_proj"))
    sv.update(h1=h1, proj=proj)
    proj3 = proj.reshape(B, S, CAT_COLS)
    xbc = conv_fwd(proj3, C_XBC // 256, kw["conv_w"], sm["conv_b"].reshape(1, -1), n("conv"))
    dtp = _dt_pad(proj, B, S)
    par = _ssd_par(sm)
    y_raw, states = ssd_fwd(xbc, dtp, par, n("ssd"))
    z = proj[:, C_Z:C_Z + SSD_INNER]
    y_ssd = norm_fwd(y_raw.reshape(T, SSD_INNER), sm["ssd_norm_g"], SSD_INNER // SSD_GROUPS, False, n("ssd_norm"), gate=z)
    sv.update(xbc=xbc, states=states, y_raw=y_raw)
    qn = norm_fwd(proj[:, C_QA:C_QA + Q_LORA], sm["q_a_norm_g"], Q_LORA, False, n("q_a_norm"))
    kvn = norm_fwd(proj[:, C_KVA:C_KVA + KV_LORA], sm["kv_a_norm_g"], KV_LORA, False, n("kv_a_norm"))
    q_raw = mm(qn, kw["wq"], "nn", n("q_b"))
    kv_raw = mm(kvn, kw["wkv"], "nn", n("kv_b"))
    gqn, gqr = _mla_gains(sm["mla_q_norm_g"])
    gkn, gkr = _mla_gains(sm["mla_k_norm_g"])
    qh = qk_fwd(q_raw, 0, q_raw, MLA_HEADS, False, gqn, gqr, cos, sin, n("q_norm_rope"))
    kh = qk_fwd(kv_raw, 0, proj, C_KR // LANES, True, gkn, gkr, cos, sin, n("k_norm_rope"))
    vv = kv_raw[:, MLA_HEADS * MLA_NOPE:].astype(BF16)
    y_mla = attn_fwd(qh.reshape(B, S, -1), kh.reshape(B, S, -1), vv.reshape(B, S, -1), MLA_HEADS, 2 * LANES, MLA_V,
                     MLA_QK ** -0.5, True, n("mla_attn"))
    ycat = jnp.concatenate([y_ssd, y_mla.reshape(T, -1)], axis=1)
    x1 = mm(ycat, kw["w_out"], "nn", n("out_proj"), add=x)
    sv.update(qn=qn, kvn=kvn, q_raw=q_raw, kv_raw=kv_raw, qh=qh, kh=kh, vv=vv, ycat=ycat, x1=x1)
    h2 = norm_fwd(x1, sm["xattn_norm_g"], D_MODEL, False, n("xattn_norm"))
    mn = norm_fwd(memf, sm["mem_norm_g"], D_MODEL, False, n("mem_norm"))
    xq = mm(h2, kw["w_xq"], "nn", n("xq"))
    xkv = mm(mn, kw["w_xkv"], "nn", n("xkv"))
    xqn = norm_fwd(xq, sm["xq_norm_g"], X_HEAD_DIM, True, n("xq_norm"))
    xkn = norm_fwd(xkv[:, :X_INNER], sm["xk_norm_g"], X_HEAD_DIM, True, n("xk_norm"))
    xv = xkv[:, X_INNER:].astype(BF16)
    ML = memf.shape[0] // B
    xo = attn_fwd(xqn.reshape(B, S, -1), xkn.reshape(B, ML, -1), xv.reshape(B, ML, -1), X_HEADS, X_HEAD_DIM, X_HEAD_DIM,
                  X_HEAD_DIM ** -0.5, False, n("x_attn"))
    x2 = mm(xo.reshape(T, -1), kw["w_xo"], "nn", n("xo_proj"), add=x1)
    sv.update(h2=h2, mn=mn, xq=xq, xkv=xkv, xqn=xqn, xkn=xkn, xv=xv, xo=xo, x2=x2)
    h3 = norm_fwd(x2, sm["ffn_norm_g"], D_MODEL, False, n("ffn_norm"))
    gu = mm(h3, kw["w_gu"], "nn", n("gate_up"), out_dtype=BF16)
    act = swiglu_fwd(gu, n("swiglu"))
    x3 = mm(act, kw["w_down"], "nn", n("down"), add=x2)
    sv.update(h3=h3, gu=gu, act=act)
    return x3, sv


def layer_bwd(dx3, sv, memf, cos, sin, kw, sm, B, S, l):
    T = dx3.shape[0]
    n = lambda s: "%s_bwd_l%d" % (s, l)
    gw, gs = {}, {}
    dact = mm(dx3, kw["w_down"], "nt", n("down_dx"))
    gw["w_down"] = mm(sv["act"], dx3, "tn", n("down_dw"))
    dgu = swiglu_bwd(sv["gu"], dact, n("swiglu"))
    dh3 = mm(dgu, kw["w_gu"], "nt", n("gate_up_dx"))
    gw["w_gu"] = mm(sv["h3"], dgu, "tn", n("gate_up_dw"))
    dx2, gs["ffn_norm_g"] = norm_bwd(sv["x2"], sm["ffn_norm_g"], dh3, D_MODEL, False, n("ffn_norm"), add=dx3)
    dxo = mm(dx2, kw["w_xo"], "nt", n("xo_dx"))
    gw["w_xo"] = mm(sv["xo"].reshape(T, -1), dx2, "tn", n("xo_dw"))
    ML = memf.shape[0] // B
    dxqn, dxkn, dxv = attn_bwd(sv["xqn"].reshape(B, S, -1), sv["xkn"].reshape(B, ML, -1), sv["xv"].reshape(B, ML, -1),
                               dxo.reshape(B, S, -1), X_HEADS, X_HEAD_DIM, X_HEAD_DIM, X_HEAD_DIM ** -0.5, False, n("x_attn"))
    dxq, gs["xq_norm_g"] = norm_bwd(sv["xq"], sm["xq_norm_g"], dxqn.reshape(T, -1), X_HEAD_DIM, True, n("xq_norm"))
    dxk, gs["xk_norm_g"] = norm_bwd(sv["xkv"][:, :X_INNER], sm["xk_norm_g"], dxkn.reshape(B * ML, -1), X_HEAD_DIM, True,
                                    n("xk_norm"))
    dxkv = jnp.concatenate([dxk, dxv.reshape(B * ML, -1)], axis=1)
    dmn = mm(dxkv, kw["w_xkv"], "nt", n("xkv_dx"))
    gw["w_xkv"] = mm(sv["mn"], dxkv, "tn", n("xkv_dw"))
    _, gs["mem_norm_g"] = norm_bwd(memf, sm["mem_norm_g"], dmn, D_MODEL, False, n("mem_norm"))
    dh2 = mm(dxq, kw["w_xq"], "nt", n("xq_dx"))
    gw["w_xq"] = mm(sv["h2"], dxq, "tn", n("xq_dw"))
    dx1, gs["xattn_norm_g"] = norm_bwd(sv["x1"], sm["xattn_norm_g"], dh2, D_MODEL, False, n("xattn_norm"), add=dx2)
    dycat = mm(dx1, kw["w_out"], "nt", n("out_dx"))
    gw["w_out"] = mm(sv["ycat"], dx1, "tn", n("out_dw"))
    dqh, dkh, dvv = attn_bwd(sv["qh"].reshape(B, S, -1), sv["kh"].reshape(B, S, -1), sv["vv"].reshape(B, S, -1),
                             dycat[:, SSD_INNER:].reshape(B, S, -1), MLA_HEADS, 2 * LANES, MLA_V, MLA_QK ** -0.5, True,
                             n("mla_attn"))
    gqn, gqr = _mla_gains(sm["mla_q_norm_g"])
    gkn, gkr = _mla_gains(sm["mla_k_norm_g"])
    dq_n, dq_r, dgqn, dgqr = qk_bwd(sv["q_raw"], 0, sv["q_raw"], MLA_HEADS, False, gqn, gqr, cos, sin, dqh.reshape(T, -1),
                                    n("q_norm_rope"))
    dk_n, dk_rope, dgkn, dgkr = qk_bwd(sv["kv_raw"], 0, sv["proj"], C_KR // LANES, True, gkn, gkr, cos, sin,
                                       dkh.reshape(T, -1), n("k_norm_rope"))
    gs["mla_q_norm_g"] = jnp.concatenate([dgqn[0], dgqr[0, :MLA_ROPE]])
    gs["mla_k_norm_g"] = jnp.concatenate([dgkn[0], dgkr[0, :MLA_ROPE]])
    dq_raw = jnp.concatenate([dq_n, dq_r], axis=1)
    dkv_raw = jnp.concatenate([dk_n, dvv.reshape(T, -1)], axis=1)
    dqn = mm(dq_raw, kw["wq"], "nt", n("q_b_dx"))
    gw["wq"] = mm(sv["qn"], dq_raw, "tn", n("q_b_dw"))
    dkvn = mm(dkv_raw, kw["wkv"], "nt", n("kv_b_dx"))
    gw["wkv"] = mm(sv["kvn"], dkv_raw, "tn", n("kv_b_dw"))
    proj = sv["proj"]
    dq_a, gs["q_a_norm_g"] = norm_bwd(proj[:, C_QA:C_QA + Q_LORA], sm["q_a_norm_g"], dqn, Q_LORA, False, n("q_a_norm"))
    dkv_a, gs["kv_a_norm_g"] = norm_bwd(proj[:, C_KVA:C_KVA + KV_LORA], sm["kv_a_norm_g"], dkvn, KV_LORA, False,
                                        n("kv_a_norm"))
    z = proj[:, C_Z:C_Z + SSD_INNER]
    dy_raw, dz, gs["ssd_norm_g"] = norm_bwd(sv["y_raw"].reshape(T, -1), sm["ssd_norm_g"], dycat[:, :SSD_INNER],
                                            SSD_INNER // SSD_GROUPS, False, n("ssd_norm"), gate=z)
    dtp = _dt_pad(proj, B, S)
    par = _ssd_par(sm)
    dxs, dBm, dCm, ddt, dpar = ssd_bwd(sv["xbc"], dtp, par, sv["states"], dy_raw.reshape(B, S, -1), n("ssd"))
    gs["dt_bias"] = dpar[:, 0, :SSD_HPG].reshape(-1)
    gs["a_log"] = dpar[:, 1, :SSD_HPG].reshape(-1)
    gs["d_skip"] = dpar[:, 2, :SSD_HPG].reshape(-1)
    dxbc_act = jnp.concatenate([dxs, dBm, dCm], axis=2)
    dxbc, dconv_w, dconv_b = conv_bwd(proj.reshape(B, S, -1), C_XBC // 256, kw["conv_w"], sm["conv_b"].reshape(1, -1), dxbc_act,
                                      n("conv"))
    gw["conv_w"] = dconv_w
    gs["conv_b"] = dconv_b[0]
    ddt16 = ddt.reshape(T, SSD_GROUPS, LANES)[:, :, :SSD_HPG].reshape(T, 16)
    dproj = jnp.concatenate([dz, dxbc.reshape(T, -1), dq_a, dkv_a, dk_rope, ddt16, jnp.zeros((T, 112), F32)], axis=1)
    dh1 = mm(dproj, kw["wcat"], "nt", n("in_dx"))
    gw["wcat"] = mm(sv["h1"], dproj, "tn", n("in_dw"))
    dx0, gs["attn_norm_g"] = norm_bwd(sv["x0"], sm["attn_norm_g"], dh1, D_MODEL, False, n("attn_norm"), add=dx1)
    gs = {k: v.reshape(-1) for k, v in gs.items()}
    return dx0, gw, gs


def _small_pack(vals):
    flat = jnp.concatenate([vals[l][k].reshape(-1) for l in range(DEPTH) for k in SMALL])
    n = flat.shape[0]
    rows = -(-n // LANES)
    rows = -(-rows // 8) * 8
    return jnp.pad(flat, (0, rows * LANES - n)).reshape(rows, LANES), n


def _small_unpack(flat, shapes):
    v = flat.reshape(-1)
    out, off = [dict() for _ in range(DEPTH)], 0
    for l in range(DEPTH):
        for k in SMALL:
            sz = shapes[k]
            out[l][k] = v[off:off + sz]
            off += sz
    return out


def _adam_2d(a):
    if a.ndim == 2:
        return a
    return a.reshape(a.shape[0] * a.shape[1], a.shape[2])


def kernel(x, mem, positions, attn_norm_g, w_in, conv_w, conv_b, dt_bias, a_log, d_skip, ssd_norm_g, q_a_norm_g, w_q_b, kv_a_norm_g, w_kv_b, mla_q_norm_g, mla_k_norm_g, w_out, xattn_norm_g, mem_norm_g, w_xq, w_xk, w_xv, xq_norm_g, xk_norm_g, w_xo, ffn_norm_g, w_gate, w_up, w_down, loss_target, m_attn_norm_g, m_w_in, m_conv_w, m_conv_b, m_dt_bias, m_a_log, m_d_skip, m_ssd_norm_g, m_q_a_norm_g, m_w_q_b, m_kv_a_norm_g, m_w_kv_b, m_mla_q_norm_g, m_mla_k_norm_g, m_w_out, m_xattn_norm_g, m_mem_norm_g, m_w_xq, m_w_xk, m_w_xv, m_xq_norm_g, m_xk_norm_g, m_w_xo, m_ffn_norm_g, m_w_gate, m_w_up, m_w_down, v_attn_norm_g, v_w_in, v_conv_w, v_conv_b, v_dt_bias, v_a_log, v_d_skip, v_ssd_norm_g, v_q_a_norm_g, v_w_q_b, v_kv_a_norm_g, v_w_kv_b, v_mla_q_norm_g, v_mla_k_norm_g, v_w_out, v_xattn_norm_g, v_mem_norm_g, v_w_xq, v_w_xk, v_w_xv, v_xq_norm_g, v_xk_norm_g, v_w_xo, v_ffn_norm_g, v_w_gate, v_w_up, v_w_down):
    loc = locals()
    W = {k: loc[k] for k in WEIGHTS}
    Mo = {k: loc["m_" + k] for k in WEIGHTS}
    Vo = {k: loc["v_" + k] for k in WEIGHTS}
    B, S, D = x.shape
    T = B * S
    xf = x.reshape(T, D)
    memf = mem.reshape(-1, D)
    tf = loss_target.reshape(T, D)
    cos, sin = rope_tables(positions.reshape(T, 1).astype(F32))

    rows, _ = _pack_rows()
    packed = jnp.concatenate([pack_shards({k: W[k][l] for k in SHARDED}, BF16) for l in range(DEPTH)], axis=0)
    gathered = all_gather_chips(packed, "all_gather_weights")
    cw = gather_all_devices(conv_w.reshape(DEPTH * SSD_CONV, -1), "gather_conv_w")[0::2]
    cw = cw.transpose(1, 0, 2).reshape(DEPTH, SSD_CONV, SSD_CONV_DIM)
    kws, sms = [], []
    for l in range(DEPTH):
        kws.append(kernel_weights(unpack_full(gathered[:, l * rows:(l + 1) * rows]), cw[l]))
        sms.append({k: W[k][l] for k in SMALL})

    saved = []
    h = xf
    for l in range(DEPTH):
        h, sv = layer_fwd(h, memf, cos, sin, kws[l], sms[l], B, S, l)
        saved.append(sv)
    dh, lpart = loss_head(h, tf)
    loss = lax.psum(lpart[0, 0], ("x", "y", "c"))

    gshard = [None] * DEPTH
    gsmall = [None] * DEPTH
    for l in reversed(range(DEPTH)):
        dh, gw, gs = layer_bwd(dh, saved[l], memf, cos, sin, kws[l], sms[l], B, S, l)
        saved[l] = None
        gshard[l] = unpack_shards(reduce_scatter_grads(pack_full(full_grads(gw), BF16), "l%d" % l))
        gsmall[l] = gs
    grad_x = dh.reshape(B, S, D)

    spack, _ = _small_pack(gsmall)
    allv = gather_all_devices(spack, "gather_small_grads")
    ssum = sum_arrays([allv[i] for i in range(8)], F32, "sum_small_grads")
    gsm = _small_unpack(ssum, {k: int(np.prod(W[k].shape[1:])) for k in SMALL})

    grads = {}
    for k in WEIGHTS:
        if k in SHARDED:
            grads[k] = jnp.stack([gshard[l][k] for l in range(DEPTH)]).reshape(W[k].shape)
        else:
            grads[k] = jnp.stack([gsm[l][k] for l in range(DEPTH)]).reshape(W[k].shape)

    delta, new_m, new_v = {}, {}, {}
    for k in WEIGHTS:
        shp = W[k].shape
        d, nm, nv = adamw(_adam_2d(W[k]), _adam_2d(grads[k]), _adam_2d(Mo[k]), _adam_2d(Vo[k]), "adamw_" + k)
        delta[k], new_m[k], new_v[k] = d.reshape(shp), nm.reshape(shp), nv.reshape(shp)

    return (loss, grad_x, *[grads[k] for k in WEIGHTS], *[delta[k] for k in WEIGHTS], *[new_m[k] for k in WEIGHTS],
            *[new_v[k] for k in WEIGHTS])
```

```python
import functools
import math

import numpy as np
import jax
import jax.numpy as jnp
from jax import lax
from jax.experimental import pallas as pl
from jax.experimental.pallas import tpu as pltpu

F32 = jnp.float32
BF16 = jnp.bfloat16
MESH = pl.DeviceIdType.MESH
HI = lax.Precision.HIGHEST

D_MODEL = 2048
DEPTH = 4
SSD_INNER = 1024
SSD_HEAD_DIM = 64
SSD_GROUPS = 2
SSD_HPG = 8
SSD_STATE = 128
SSD_CONV = 4
SSD_CHUNK = 128
SSD_CONV_DIM = 1536
SSD_GW = SSD_HPG * SSD_HEAD_DIM
MLA_HEADS = 8
MLA_NOPE = 128
MLA_ROPE = 64
MLA_QK = 192
MLA_V = 128
Q_LORA = 512
KV_LORA = 512
ROPE_THETA = 10000.0
X_HEADS = 4
X_HEAD_DIM = 128
X_INNER = 512
FFN = 5632
IN_COLS = 3664
EPS = 1e-6
ADAM_LR, ADAM_B1, ADAM_B2, ADAM_EPS, ADAM_WD, ADAM_STEP = 0.001, 0.9, 0.999, 1e-08, 0.01, 10

C_Z, C_XBC, C_QA, C_KVA, C_KR, C_DT, CAT_COLS = 0, 1024, 2560, 3072, 3584, 3712, 3840

LANES = 128
VMEM_LIMIT = 56 * 1024 * 1024

SMALL = ("attn_norm_g", "conv_b", "dt_bias", "a_log", "d_skip", "ssd_norm_g", "q_a_norm_g", "kv_a_norm_g",
         "mla_q_norm_g", "mla_k_norm_g", "xattn_norm_g", "mem_norm_g", "xq_norm_g", "xk_norm_g", "ffn_norm_g")
SMALL_RS = SMALL + ("conv_w",)
WEIGHTS = ("attn_norm_g", "w_in", "conv_w", "conv_b", "dt_bias", "a_log", "d_skip", "ssd_norm_g", "q_a_norm_g", "w_q_b",
           "kv_a_norm_g", "w_kv_b", "mla_q_norm_g", "mla_k_norm_g", "w_out", "xattn_norm_g", "mem_norm_g", "w_xq", "w_xk",
           "w_xv", "xq_norm_g", "xk_norm_g", "w_xo", "ffn_norm_g", "w_gate", "w_up", "w_down")


def _pcall(body, **kw):
    return pl.pallas_call(body, **kw)


def _params(sem):
    return pltpu.CompilerParams(dimension_semantics=sem, vmem_limit_bytes=VMEM_LIMIT)


def _tile(n, cands):
    for t in cands:
        if n % t == 0:
            return t
    return n


def _sigmoid(z):
    return 1.0 / (1.0 + jnp.exp(-z))


def _mtile(n, cap):
    if n % LANES:
        return n
    q = n // LANES
    t = LANES * max(d for d in range(1, q + 1) if q % d == 0 and LANES * d <= cap)
    return n if (t < 512 and n <= 2048) else t


def mm(a, b, mode, name, add=None, out_dtype=F32, wv=None, out_wv=None):
    if wv is not None:
        kind, l = wv
        _, _, r, c = b.shape
        bshape = (4 * r, c) if kind == "row" else (r, 4 * c)
    else:
        kind, bshape = None, b.shape
    if mode == "nn":
        (M, K), (K2, N) = a.shape, bshape
    elif mode == "nt":
        (M, K), (N, K2) = a.shape, bshape
    else:
        (K, M), (K2, N) = a.shape, bshape
    assert K == K2, (a.shape, bshape, mode)
    m_unit = M // 4 if out_wv == "row" else M
    n_unit = N // 4 if (out_wv == "col" or (kind == "col" and mode == "nn") or (kind == "row" and mode == "nt")) else N
    k_unit = K // 4 if ((kind == "row" and mode == "nn") or (kind == "col" and mode == "nt")) else K
    tm = _mtile(m_unit, 512)
    tn = _mtile(n_unit, 1024)
    tk = _mtile(k_unit, 1024)
    nk = K // tk
    qm, qn, qk = m_unit // tm, n_unit // tn, k_unit // tk
    dn = {"nn": (((1,), (0,)), ((), ())), "nt": (((1,), (1,)), ((), ())), "tn": (((0,), (0,)), ((), ()))}[mode]
    has_add = add is not None

    def body(a_ref, b_ref, *rest):
        if has_add:
            c_ref, o_ref, acc = rest
        else:
            o_ref, acc = rest
        k = pl.program_id(2)

        @pl.when(k == 0)
        def _():
            acc[...] = jnp.zeros_like(acc)

        acc[...] += lax.dot_general(a_ref[...].astype(BF16), b_ref[...].astype(BF16), dn, preferred_element_type=F32)

        @pl.when(k == nk - 1)
        def _():
            res = acc[...]
            if has_add:
                res = res + c_ref[...].astype(F32)
            o_ref[...] = res.astype(out_dtype)

    if mode == "tn":
        a_spec = pl.BlockSpec((tk, tm), lambda i, j, k: (k, i))
    else:
        a_spec = pl.BlockSpec((tm, tk), lambda i, j, k: (i, k))
    if kind is None:
        b_spec = (pl.BlockSpec((tn, tk), lambda i, j, k: (j, k)) if mode == "nt"
                  else pl.BlockSpec((tk, tn), lambda i, j, k: (k, j)))
    elif mode == "nn" and kind == "row":
        b_spec = pl.BlockSpec((None, None, tk, tn), lambda i, j, k: (k // qk, l, k % qk, j))
    elif mode == "nn" and kind == "col":
        b_spec = pl.BlockSpec((None, None, tk, tn), lambda i, j, k: (j // qn, l, k, j % qn))
    elif mode == "nt" and kind == "row":
        b_spec = pl.BlockSpec((None, None, tn, tk), lambda i, j, k: (j // qn, l, j % qn, k))
    elif mode == "nt" and kind == "col":
        b_spec = pl.BlockSpec((None, None, tn, tk), lambda i, j, k: (k // qk, l, j, k % qk))
    else:
        raise ValueError((mode, kind))
    if out_wv == "row":
        o_spec = pl.BlockSpec((None, tm, tn), lambda i, j, k: (i // qm, i % qm, j))
        out_shape = jax.ShapeDtypeStruct((4, M // 4, N), out_dtype)
    elif out_wv == "col":
        o_spec = pl.BlockSpec((None, tm, tn), lambda i, j, k: (j // qn, i, j % qn))
        out_shape = jax.ShapeDtypeStruct((4, M, N // 4), out_dtype)
    else:
        o_spec = pl.BlockSpec((tm, tn), lambda i, j, k: (i, j))
        out_shape = jax.ShapeDtypeStruct((M, N), out_dtype)
    in_specs = [a_spec, b_spec] + ([o_spec] if has_add else [])
    args = (a, b) + ((add,) if has_add else ())
    return _pcall(
        body, name=name, grid=(M // tm, N // tn, nk), in_specs=in_specs, out_specs=o_spec,
        out_shape=out_shape, scratch_shapes=[pltpu.VMEM((tm, tn), F32)],
        compiler_params=_params(("parallel", "parallel", "arbitrary")))(*args)


def norm_fwd(x, gain, W, shared, name, gate=None, out_dtype=BF16):
    T, C = x.shape
    nb = C // W
    tq = _tile(T, (256, 128))
    g2 = gain.reshape(1, -1).astype(F32)
    gated = gate is not None

    def body(*refs):
        if gated:
            x_ref, z_ref, g_ref, o_ref = refs
        else:
            x_ref, g_ref, o_ref = refs
        u = x_ref[...].astype(F32)
        if gated:
            z = z_ref[...]
            u = u * (z * _sigmoid(z))
        ms = jnp.mean(u * u, axis=-1, keepdims=True)
        o_ref[...] = (u * lax.rsqrt(ms + EPS) * g_ref[...]).astype(out_dtype)

    xs = pl.BlockSpec((tq, W), lambda j, i: (i, j))
    gs = pl.BlockSpec((1, W), (lambda j, i: (0, 0)) if shared else (lambda j, i: (0, j)))
    in_specs = [xs] + ([xs] if gated else []) + [gs]
    args = (x,) + ((gate,) if gated else ()) + (g2,)
    return _pcall(body, name=name, grid=(nb, T // tq), in_specs=in_specs, out_specs=xs,
                  out_shape=jax.ShapeDtypeStruct((T, C), out_dtype),
                  compiler_params=_params(("parallel", "parallel")))(*args)


def norm_bwd(x, gain, dy, W, shared, name, gate=None, add=None):
    T, C = x.shape
    nb = C // W
    tq = _tile(T, (256, 128))
    g2 = gain.reshape(1, -1).astype(F32)
    gated = gate is not None
    has_add = add is not None
    GC = W if shared else C

    def body(*refs):
        refs = list(refs)
        x_ref = refs.pop(0)
        z_ref = refs.pop(0) if gated else None
        g_ref = refs.pop(0)
        dy_ref = refs.pop(0)
        a_ref = refs.pop(0) if has_add else None
        dx_ref = refs.pop(0)
        dz_ref = refs.pop(0) if gated else None
        dg_ref = refs.pop(0)
        j, i = pl.program_id(0), pl.program_id(1)
        first = (i == 0) & (j == 0) if shared else (i == 0)

        @pl.when(first)
        def _():
            dg_ref[...] = jnp.zeros_like(dg_ref)

        xv = x_ref[...].astype(F32)
        if gated:
            z = z_ref[...]
            s = _sigmoid(z)
            u = xv * (z * s)
        else:
            u = xv
        rstd = lax.rsqrt(jnp.mean(u * u, axis=-1, keepdims=True) + EPS)
        uh = u * rstd
        dyv = dy_ref[...].astype(F32)
        dg_ref[...] += jnp.sum(dyv * uh, axis=0, keepdims=True)
        dyg = dyv * g_ref[...]
        du = rstd * (dyg - uh * jnp.mean(dyg * uh, axis=-1, keepdims=True))
        if gated:
            dx = du * (z * s)
            dz_ref[...] = du * xv * (s * (1.0 + z * (1.0 - s)))
        else:
            dx = du
        if has_add:
            dx = dx + a_ref[...]
        dx_ref[...] = dx

    xs = pl.BlockSpec((tq, W), lambda j, i: (i, j))
    gs = pl.BlockSpec((1, W), (lambda j, i: (0, 0)) if shared else (lambda j, i: (0, j)))
    in_specs = [xs] + ([xs] if gated else []) + [gs, xs] + ([xs] if has_add else [])
    args = (x,) + ((gate,) if gated else ()) + (g2, dy) + ((add,) if has_add else ())
    out_specs = [xs] + ([xs] if gated else []) + [gs]
    out_shape = [jax.ShapeDtypeStruct((T, C), F32)] + ([jax.ShapeDtypeStruct((T, C), F32)] if gated else []) \
        + [jax.ShapeDtypeStruct((1, GC), F32)]
    return _pcall(body, name=name, grid=(nb, T // tq), in_specs=in_specs, out_specs=out_specs, out_shape=out_shape,
                  compiler_params=_params(("arbitrary", "arbitrary")))(*args)


def _inv_freq_row():
    inv = 1.0 / (ROPE_THETA ** (np.arange(0, MLA_ROPE, 2, dtype=np.float32) / MLA_ROPE))
    row = np.zeros((1, LANES), np.float32)
    row[0, :32] = inv
    row[0, 32:64] = inv
    return jnp.asarray(row)


def rope_tables(pos):
    T = pos.shape[0]
    tq = _tile(T, (512, 256, 128))

    def body(p_ref, f_ref, c_ref, s_ref):
        ang = p_ref[...] * f_ref[...]
        lane = lax.broadcasted_iota(jnp.int32, ang.shape, 1)
        c_ref[...] = jnp.where(lane < 64, jnp.cos(ang), 0.0)
        sn = jnp.sin(ang)
        s_ref[...] = jnp.where(lane < 32, -sn, jnp.where(lane < 64, sn, 0.0))

    ts = pl.BlockSpec((tq, LANES), lambda i: (i, 0))
    return _pcall(body, name="rope_tables", grid=(T // tq,),
                  in_specs=[pl.BlockSpec((tq, 1), lambda i: (i, 0)), pl.BlockSpec((1, LANES), lambda i: (0, 0))],
                  out_specs=[ts, ts], out_shape=[jax.ShapeDtypeStruct((T, LANES), F32)] * 2,
                  compiler_params=_params(("parallel",)))(pos, _inv_freq_row())


def _swap_halves(r):
    lane = lax.broadcasted_iota(jnp.int32, r.shape, 1)
    return jnp.where(lane < 32, pltpu.roll(r, 96, 1), pltpu.roll(r, 32, 1))


def qk_fwd(nope_arr, nope_blk0, rope_arr, rope_blk0, rope_shared, gn, gr, cos, sin, name):
    T = nope_arr.shape[0]
    tq = _tile(T, (256, 128))

    def body(n_ref, r_ref, gn_ref, gr_ref, c_ref, s_ref, o_ref):
        n = n_ref[...]
        r = r_ref[...]
        ms = (jnp.sum(n * n, axis=-1, keepdims=True) + jnp.sum(r * r, axis=-1, keepdims=True)) * (1.0 / MLA_QK)
        rstd = lax.rsqrt(ms + EPS)
        rn = r * rstd * gr_ref[...]
        rr = rn * c_ref[...] + _swap_halves(rn) * s_ref[...]
        o_ref[:, :LANES] = (n * rstd * gn_ref[...]).astype(BF16)
        o_ref[:, LANES:] = rr.astype(BF16)

    blk = lambda off: pl.BlockSpec((tq, LANES), lambda i, h: (i, off(h) if callable(off) else off + h))
    rspec = pl.BlockSpec((tq, LANES), lambda i, h: (i, rope_blk0)) if rope_shared else blk(rope_blk0)
    row = pl.BlockSpec((1, LANES), lambda i, h: (0, 0))
    tab = pl.BlockSpec((tq, LANES), lambda i, h: (i, 0))
    return _pcall(body, name=name, grid=(T // tq, MLA_HEADS),
                  in_specs=[blk(nope_blk0), rspec, row, row, tab, tab],
                  out_specs=pl.BlockSpec((tq, 2 * LANES), lambda i, h: (i, h)),
                  out_shape=jax.ShapeDtypeStruct((T, MLA_HEADS * 2 * LANES), BF16),
                  compiler_params=_params(("parallel", "parallel")))(nope_arr, rope_arr, gn, gr, cos, sin)


def qk_bwd(nope_arr, nope_blk0, rope_arr, rope_blk0, rope_shared, gn, gr, cos, sin, dout, name):
    T = nope_arr.shape[0]
    tq = _tile(T, (256, 128))

    def body(n_ref, r_ref, gn_ref, gr_ref, c_ref, s_ref, d_ref, dn_ref, dr_ref, dgn_ref, dgr_ref):
        i, h = pl.program_id(0), pl.program_id(1)

        @pl.when((i == 0) & (h == 0))
        def _():
            dgn_ref[...] = jnp.zeros_like(dgn_ref)
            dgr_ref[...] = jnp.zeros_like(dgr_ref)

        n = n_ref[...]
        r = r_ref[...]
        ms = (jnp.sum(n * n, axis=-1, keepdims=True) + jnp.sum(r * r, axis=-1, keepdims=True)) * (1.0 / MLA_QK)
        rstd = lax.rsqrt(ms + EPS)
        nh = n * rstd
        rh = r * rstd
        d_n = d_ref[:, :LANES].astype(F32)
        d_rr = d_ref[:, LANES:].astype(F32)
        d_rn = d_rr * c_ref[...] - _swap_halves(d_rr) * s_ref[...]
        dgn_ref[...] += jnp.sum(d_n * nh, axis=0, keepdims=True)
        dgr_ref[...] += jnp.sum(d_rn * rh, axis=0, keepdims=True)
        dng = d_n * gn_ref[...]
        drg = d_rn * gr_ref[...]
        mean = (jnp.sum(dng * nh, axis=-1, keepdims=True) + jnp.sum(drg * rh, axis=-1, keepdims=True)) * (1.0 / MLA_QK)
        dn_ref[...] = rstd * (dng - nh * mean)
        drope = rstd * (drg - rh * mean)
        if rope_shared:
            @pl.when(h == 0)
            def _():
                dr_ref[...] = drope

            @pl.when(h != 0)
            def _():
                dr_ref[...] += drope
        else:
            dr_ref[...] = drope

    blk = lambda off: pl.BlockSpec((tq, LANES), lambda i, h: (i, off(h) if callable(off) else off + h))
    rspec = pl.BlockSpec((tq, LANES), lambda i, h: (i, rope_blk0)) if rope_shared else blk(rope_blk0)
    row = pl.BlockSpec((1, LANES), lambda i, h: (0, 0))
    tab = pl.BlockSpec((tq, LANES), lambda i, h: (i, 0))
    dr_spec = pl.BlockSpec((tq, LANES), lambda i, h: (i, 0)) if rope_shared else blk(0)
    dr_cols = LANES if rope_shared else MLA_HEADS * LANES
    return _pcall(body, name=name, grid=(T // tq, MLA_HEADS),
                  in_specs=[blk(nope_blk0), rspec, row, row, tab, tab, pl.BlockSpec((tq, 2 * LANES), lambda i, h: (i, h))],
                  out_specs=[blk(0), dr_spec, row, row],
                  out_shape=[jax.ShapeDtypeStruct((T, MLA_HEADS * LANES), F32), jax.ShapeDtypeStruct((T, dr_cols), F32),
                             jax.ShapeDtypeStruct((1, LANES), F32), jax.ShapeDtypeStruct((1, LANES), F32)],
                  compiler_params=_params(("arbitrary", "arbitrary")))(nope_arr, rope_arr, gn, gr, cos, sin, dout)


def _scores(q, k, scale, causal, row0):
    s = lax.dot_general(q, k, (((1,), (1,)), ((), ())), preferred_element_type=F32) * scale
    if causal:
        qi = row0 + lax.broadcasted_iota(jnp.int32, s.shape, 0)
        ki = lax.broadcasted_iota(jnp.int32, s.shape, 1)
        s = jnp.where(qi >= ki, s, -1e30)
    return s


def attn_fwd(q, k, v, H, dqk, dv, scale, causal, name, v_idx=lambda h: h):
    B, S, _ = q.shape
    Sk = k.shape[1]
    tq = _tile(S, (256, 128))

    def body(q_ref, k_ref, v_ref, o_ref):
        s = _scores(q_ref[0], k_ref[0], scale, causal, pl.program_id(2) * tq)
        p = jnp.exp(s - jnp.max(s, axis=-1, keepdims=True))
        l = jnp.sum(p, axis=-1, keepdims=True)
        o = lax.dot_general(p.astype(BF16), v_ref[0], (((1,), (0,)), ((), ())), preferred_element_type=F32)
        o_ref[0] = (o / l).astype(BF16)

    return _pcall(body, name=name, grid=(B, H, S // tq),
                  in_specs=[pl.BlockSpec((1, tq, dqk), lambda b, h, i: (b, i, h)),
                            pl.BlockSpec((1, Sk, dqk), lambda b, h, i: (b, 0, h)),
                            pl.BlockSpec((1, Sk, dv), lambda b, h, i: (b, 0, v_idx(h)))],
                  out_specs=pl.BlockSpec((1, tq, dv), lambda b, h, i: (b, i, h)),
                  out_shape=jax.ShapeDtypeStruct((B, S, H * dv), BF16),
                  compiler_params=_params(("parallel", "parallel", "parallel")))(q, k, v)


def attn_bwd(q, k, v, do, H, dqk, dv, scale, causal, name, v_idx=lambda h: h):
    B, S, _ = q.shape
    Sk = k.shape[1]
    tq = _tile(S, (256, 128))

    def body(q_ref, k_ref, v_ref, do_ref, dq_ref, dk_ref, dv_ref):
        i = pl.program_id(2)

        @pl.when(i == 0)
        def _():
            dk_ref[...] = jnp.zeros_like(dk_ref)
            dv_ref[...] = jnp.zeros_like(dv_ref)

        qv, kv, vv = q_ref[0], k_ref[0], v_ref[0]
        s = _scores(qv, kv, scale, causal, i * tq)
        p = jnp.exp(s - jnp.max(s, axis=-1, keepdims=True))
        p = p / jnp.sum(p, axis=-1, keepdims=True)
        dob = do_ref[0].astype(BF16)
        pb = p.astype(BF16)
        dv_ref[0] += lax.dot_general(pb, dob, (((0,), (0,)), ((), ())), preferred_element_type=F32)
        dp = lax.dot_general(dob, vv, (((1,), (1,)), ((), ())), preferred_element_type=F32)
        delta = jnp.sum(p * dp, axis=-1, keepdims=True)
        ds = (p * (dp - delta) * scale).astype(BF16)
        dq_ref[0] = lax.dot_general(ds, kv, (((1,), (0,)), ((), ())), preferred_element_type=F32)
        dk_ref[0] += lax.dot_general(ds, qv, (((0,), (0,)), ((), ())), preferred_element_type=F32)

    qs = pl.BlockSpec((1, tq, dqk), lambda b, h, i: (b, i, h))
    ks = pl.BlockSpec((1, Sk, dqk), lambda b, h, i: (b, 0, h))
    vs = pl.BlockSpec((1, Sk, dv), lambda b, h, i: (b, 0, h))
    vin = pl.BlockSpec((1, Sk, dv), lambda b, h, i: (b, 0, v_idx(h)))
    return _pcall(body, name=name, grid=(B, H, S // tq),
                  in_specs=[qs, ks, vin, pl.BlockSpec((1, tq, dv), lambda b, h, i: (b, i, h))],
                  out_specs=[qs, ks, vs],
                  out_shape=[jax.ShapeDtypeStruct((B, S, H * dqk), F32), jax.ShapeDtypeStruct((B, Sk, H * dqk), F32),
                             jax.ShapeDtypeStruct((B, Sk, H * dv), F32)],
                  compiler_params=_params(("parallel", "parallel", "arbitrary")))(q, k, v, do)


def _shift_down(u, j):
    if j == 0:
        return u
    row = lax.broadcasted_iota(jnp.int32, u.shape, 0)
    return jnp.where(row >= j, pltpu.roll(u, j, 0), 0.0)


def _shift_up(u, j):
    if j == 0:
        return u
    n = u.shape[0]
    row = lax.broadcasted_iota(jnp.int32, u.shape, 0)
    return jnp.where(row < n - j, pltpu.roll(u, n - j, 0), 0.0)


def conv_fwd(u, col0_blk, w8, b, name):
    B, S, _ = u.shape
    tc = 256
    nb = SSD_CONV_DIM // tc

    def body(u_ref, w_ref, b_ref, o_ref):
        uv = u_ref[0]
        pre = b_ref[...] + jnp.zeros_like(uv)
        for kk in range(SSD_CONV):
            pre = pre + w_ref[kk:kk + 1, :] * _shift_down(uv, SSD_CONV - 1 - kk)
        o_ref[0] = pre * _sigmoid(pre)

    return _pcall(body, name=name, grid=(B, nb),
                  in_specs=[pl.BlockSpec((1, S, tc), lambda bb, j: (bb, 0, col0_blk + j)),
                            pl.BlockSpec((8, tc), lambda bb, j: (0, j)), pl.BlockSpec((1, tc), lambda bb, j: (0, j))],
                  out_specs=pl.BlockSpec((1, S, tc), lambda bb, j: (bb, 0, j)),
                  out_shape=jax.ShapeDtypeStruct((B, S, SSD_CONV_DIM), F32),
                  compiler_params=_params(("parallel", "parallel")))(u, w8, b)


def conv_bwd(u, col0_blk, w8, b, dy, name):
    B, S, _ = u.shape
    tc = 256
    nb = SSD_CONV_DIM // tc

    def body(u_ref, w_ref, b_ref, dy_ref, du_ref, dw_ref, db_ref):
        bb = pl.program_id(1)

        @pl.when(bb == 0)
        def _():
            dw_ref[...] = jnp.zeros_like(dw_ref)
            db_ref[...] = jnp.zeros_like(db_ref)

        uv = u_ref[0]
        pre = b_ref[...] + jnp.zeros_like(uv)
        for kk in range(SSD_CONV):
            pre = pre + w_ref[kk:kk + 1, :] * _shift_down(uv, SSD_CONV - 1 - kk)
        s = _sigmoid(pre)
        dpre = dy_ref[0] * (s * (1.0 + pre * (1.0 - s)))
        du = jnp.zeros_like(uv)
        for kk in range(SSD_CONV):
            j = SSD_CONV - 1 - kk
            du = du + w_ref[kk:kk + 1, :] * _shift_up(dpre, j)
            dw_ref[kk:kk + 1, :] += jnp.sum(dpre * _shift_down(uv, j), axis=0, keepdims=True)
        db_ref[...] += jnp.sum(dpre, axis=0, keepdims=True)
        du_ref[0] = du

    return _pcall(body, name=name, grid=(nb, B),
                  in_specs=[pl.BlockSpec((1, S, tc), lambda j, bb: (bb, 0, col0_blk + j)),
                            pl.BlockSpec((8, tc), lambda j, bb: (0, j)), pl.BlockSpec((1, tc), lambda j, bb: (0, j)),
                            pl.BlockSpec((1, S, tc), lambda j, bb: (bb, 0, j))],
                  out_specs=[pl.BlockSpec((1, S, tc), lambda j, bb: (bb, 0, j)),
                             pl.BlockSpec((8, tc), lambda j, bb: (0, j)), pl.BlockSpec((1, tc), lambda j, bb: (0, j))],
                  out_shape=[jax.ShapeDtypeStruct((B, S, SSD_CONV_DIM), F32), jax.ShapeDtypeStruct((8, SSD_CONV_DIM), F32),
                             jax.ShapeDtypeStruct((1, SSD_CONV_DIM), F32)],
                  compiler_params=_params(("arbitrary", "arbitrary")))(u, w8, b, dy)


def _expand_mat():
    e = np.zeros((LANES, SSD_GW), np.float32)
    for h in range(SSD_HPG):
        e[h, h * SSD_HEAD_DIM:(h + 1) * SSD_HEAD_DIM] = 1.0
    return jnp.asarray(e)


def _tri_mat():
    return jnp.asarray(np.tril(np.ones((SSD_CHUNK, SSD_CHUNK), np.float32)))


def _dotf(a, b, dn=(((1,), (0,)), ((), ()))):
    return lax.dot_general(a, b, dn, precision=HI, preferred_element_type=F32)


def _dotb(a, b, dn=(((1,), (0,)), ((), ()))):
    return lax.dot_general(a.astype(BF16), b.astype(BF16), dn, preferred_element_type=F32)


NT = (((1,), (1,)), ((), ()))
TN = (((0,), (0,)), ((), ()))


def _ssd_common(dt_ref, par_ref, tri_ref, e_ref):
    dtr = dt_ref[0]
    xb = dtr + par_ref[0:1, :]
    dt = jnp.maximum(xb, 0.0) + jnp.log(1.0 + jnp.exp(-jnp.abs(xb)))
    A = -jnp.exp(par_ref[1:2, :])
    a = dt * A
    acs = _dotf(tri_ref[...], a)
    alast = acs[SSD_CHUNK - 1:SSD_CHUNK, :]
    E = e_ref[...]
    acsE = _dotf(acs, E)
    alastE = acsE[SSD_CHUNK - 1:SSD_CHUNK, :]
    return dict(xb=xb, dt=dt, A=A, acs=acs, acsT=acs.T, ealast=jnp.exp(alast), w=jnp.exp(alast - acs),
                dtE=_dotf(dt, E), eacsE=jnp.exp(acsE), wE=jnp.exp(alastE - acsE), ealastE=jnp.exp(alastE),
                DE=_dotf(par_ref[...], E)[2:3, :])


def _decay(c, e):
    diff = c["acs"][:, e:e + 1] - c["acsT"][e:e + 1, :]
    li = lax.broadcasted_iota(jnp.int32, diff.shape, 0)
    si = lax.broadcasted_iota(jnp.int32, diff.shape, 1)
    return jnp.where(li >= si, jnp.exp(jnp.minimum(diff, 0.0)), 0.0)


def ssd_fwd(xbc, dtp, par, name):
    B, S, _ = xbc.shape
    nc = S // SSD_CHUNK
    L, N, GW, P = SSD_CHUNK, SSD_STATE, SSD_GW, SSD_HEAD_DIM

    def body(x_ref, b_ref, c_ref, dt_ref, par_ref, tri_ref, e_ref, y_ref, st_ref, S_scr):
        @pl.when(pl.program_id(2) == 0)
        def _():
            S_scr[...] = jnp.zeros_like(S_scr)

        Sin = S_scr[...]
        st_ref[0, 0, 0] = Sin
        c = _ssd_common(dt_ref, par_ref.at[0], tri_ref, e_ref)
        X = x_ref[0]
        Bm = b_ref[0]
        Cm = c_ref[0]
        xdt = X * c["dtE"]
        CB = _dotb(Cm, Bm, NT)
        y_ref[0] = c["eacsE"] * _dotb(Cm, Sin) + c["DE"] * X
        for e in range(SSD_HPG):
            M = CB * _decay(c, e)
            y_ref[0, :, e * P:(e + 1) * P] += _dotb(M, xdt[:, e * P:(e + 1) * P])
        S_scr[...] = c["ealastE"] * Sin + _dotb(Bm, xdt * c["wE"], TN)

    return _pcall(
        body, name=name, grid=(SSD_GROUPS, B, nc),
        in_specs=[pl.BlockSpec((1, L, GW), lambda g, b, ci: (b, ci, g)),
                  pl.BlockSpec((1, L, N), lambda g, b, ci: (b, ci, 8 + g)),
                  pl.BlockSpec((1, L, N), lambda g, b, ci: (b, ci, 10 + g)),
                  pl.BlockSpec((1, L, LANES), lambda g, b, ci: (b, ci, g)),
                  pl.BlockSpec((1, 8, LANES), lambda g, b, ci: (g, 0, 0)),
                  pl.BlockSpec((L, L), lambda g, b, ci: (0, 0)),
                  pl.BlockSpec((LANES, GW), lambda g, b, ci: (0, 0))],
        out_specs=[pl.BlockSpec((1, L, GW), lambda g, b, ci: (b, ci, g)),
                   pl.BlockSpec((1, 1, 1, N, GW), lambda g, b, ci: (g, b, ci, 0, 0))],
        out_shape=[jax.ShapeDtypeStruct((B, S, SSD_INNER), F32), jax.ShapeDtypeStruct((SSD_GROUPS, B, nc, N, GW), F32)],
        scratch_shapes=[pltpu.VMEM((N, GW), F32)],
        compiler_params=_params(("arbitrary", "arbitrary", "arbitrary")))(xbc, xbc, xbc, dtp, par, _tri_mat(), _expand_mat())


def ssd_bwd(xbc, dtp, par, states, dy, name):
    B, S, _ = xbc.shape
    nc = S // SSD_CHUNK
    L, N, GW, P = SSD_CHUNK, SSD_STATE, SSD_GW, SSD_HEAD_DIM

    def body(x_ref, b_ref, c_ref, dt_ref, par_ref, tri_ref, e_ref, st_ref, dy_ref,
             dx_ref, db_ref, dc_ref, ddt_ref, dpar_ref, dS_scr, dxdt_scr):
        @pl.when(pl.program_id(2) == 0)
        def _():
            dS_scr[...] = jnp.zeros_like(dS_scr)

        @pl.when((pl.program_id(1) == 0) & (pl.program_id(2) == 0))
        def _():
            dpar_ref[...] = jnp.zeros_like(dpar_ref)

        c = _ssd_common(dt_ref, par_ref.at[0], tri_ref, e_ref)
        E = e_ref[...]
        red = lambda t: _dotf(t, E, NT)
        red_row = lambda t: red(jnp.broadcast_to(t, (8, GW)))[0:1, :]
        X, Bm, Cm = x_ref[0], b_ref[0], c_ref[0]
        Sin = st_ref[0, 0, 0]
        dY = dy_ref[0]
        dSo = dS_scr[...]
        xdt = X * c["dtE"]
        CB = _dotb(Cm, Bm, NT)
        dD = red_row(jnp.sum(dY * X, axis=0, keepdims=True))
        yoff = c["eacsE"] * _dotb(Cm, Sin)
        dacs = red(dY * yoff)
        dYe = dY * c["eacsE"]
        dC = _dotb(dYe, Sin, NT)
        dSin = _dotb(Cm, dYe, TN) + c["ealastE"] * dSo
        dB = _dotb(xdt * c["wE"], dSo, NT)
        dxw = _dotb(Bm, dSo)
        t = red(dxw * xdt) * c["w"]
        dacs = dacs - t
        dalast = jnp.sum(t, axis=0, keepdims=True) + c["ealast"] * red_row(jnp.sum(Sin * dSo, axis=0, keepdims=True))
        dxdt_scr[...] = dxw * c["wE"]
        dCB = jnp.zeros((L, L), F32)
        dacsT = jnp.zeros((LANES, L), F32)
        lane = lax.broadcasted_iota(jnp.int32, (1, LANES), 1)
        subl = lax.broadcasted_iota(jnp.int32, (LANES, 1), 0)
        for e in range(SSD_HPG):
            Lm = _decay(c, e)
            M = CB * Lm
            dYh = dY[:, e * P:(e + 1) * P]
            dM = _dotb(dYh, xdt[:, e * P:(e + 1) * P], NT)
            dxdt_scr[:, e * P:(e + 1) * P] += _dotb(M, dYh, TN)
            dCBe = dM * Lm
            dCB = dCB + dCBe
            Gm = dCBe * CB
            dacs = dacs + jnp.sum(Gm, axis=1, keepdims=True) * (lane == e).astype(F32)
            dacsT = dacsT - jnp.sum(Gm, axis=0, keepdims=True) * (subl == e).astype(F32)
        dC = dC + _dotb(dCB, Bm)
        dB = dB + _dotb(dCB, Cm, TN)
        dacs = dacs + dacsT.T
        row = lax.broadcasted_iota(jnp.int32, (L, 1), 0)
        dacs = dacs + jnp.where(row == L - 1, dalast, 0.0)
        da = _dotf(tri_ref[...], dacs, TN)
        dxdt = dxdt_scr[...]
        ddt = da * c["A"] + red(dxdt * X)
        dA = jnp.sum(da * c["dt"], axis=0, keepdims=True)
        ddtr = ddt * _sigmoid(c["xb"])
        dx_ref[0] = c["DE"] * dY + dxdt * c["dtE"]
        db_ref[0] = dB
        dc_ref[0] = dC
        ddt_ref[0] = ddtr
        dpar_ref[0, 0:1, :] += jnp.sum(ddtr, axis=0, keepdims=True)
        dpar_ref[0, 1:2, :] += dA * c["A"]
        dpar_ref[0, 2:3, :] += dD
        dS_scr[...] = dSin

    rc = lambda ci: nc - 1 - ci
    xspec = pl.BlockSpec((1, L, GW), lambda g, b, ci: (b, rc(ci), g))
    return _pcall(
        body, name=name, grid=(SSD_GROUPS, B, nc),
        in_specs=[xspec,
                  pl.BlockSpec((1, L, N), lambda g, b, ci: (b, rc(ci), 8 + g)),
                  pl.BlockSpec((1, L, N), lambda g, b, ci: (b, rc(ci), 10 + g)),
                  pl.BlockSpec((1, L, LANES), lambda g, b, ci: (b, rc(ci), g)),
                  pl.BlockSpec((1, 8, LANES), lambda g, b, ci: (g, 0, 0)),
                  pl.BlockSpec((L, L), lambda g, b, ci: (0, 0)),
                  pl.BlockSpec((LANES, GW), lambda g, b, ci: (0, 0)),
                  pl.BlockSpec((1, 1, 1, N, GW), lambda g, b, ci: (g, b, rc(ci), 0, 0)),
                  xspec],
        out_specs=[xspec,
                   pl.BlockSpec((1, L, N), lambda g, b, ci: (b, rc(ci), g)),
                   pl.BlockSpec((1, L, N), lambda g, b, ci: (b, rc(ci), g)),
                   pl.BlockSpec((1, L, LANES), lambda g, b, ci: (b, rc(ci), g)),
                   pl.BlockSpec((1, 8, LANES), lambda g, b, ci: (g, 0, 0))],
        out_shape=[jax.ShapeDtypeStruct((B, S, SSD_INNER), F32), jax.ShapeDtypeStruct((B, S, SSD_GROUPS * N), F32),
                   jax.ShapeDtypeStruct((B, S, SSD_GROUPS * N), F32), jax.ShapeDtypeStruct((B, S, SSD_GROUPS * LANES), F32),
                   jax.ShapeDtypeStruct((SSD_GROUPS, 8, LANES), F32)],
        scratch_shapes=[pltpu.VMEM((N, GW), F32), pltpu.VMEM((L, GW), F32)],
        compiler_params=_params(("arbitrary", "arbitrary", "arbitrary")))(
            xbc, xbc, xbc, dtp, par, _tri_mat(), _expand_mat(), states, dy)


def swiglu_fwd(g, u, name):
    T = g.shape[0]
    tq, tc = _tile(T, (512, 256, 128)), 512
    nf = FFN // tc

    def body(g_ref, u_ref, o_ref):
        gv = g_ref[...].astype(F32)
        o_ref[...] = (gv * _sigmoid(gv) * u_ref[...].astype(F32)).astype(BF16)

    lo = pl.BlockSpec((tq, tc), lambda i, j: (i, j))
    return _pcall(body, name=name, grid=(T // tq, nf), in_specs=[lo, lo], out_specs=lo,
                  out_shape=jax.ShapeDtypeStruct((T, FFN), BF16), compiler_params=_params(("parallel", "parallel")))(g, u)


def swiglu_bwd(gate, up, dact, name):
    T = gate.shape[0]
    tq, tc = _tile(T, (512, 256, 128)), 512
    nf = FFN // tc

    def body(g_ref, u_ref, d_ref, dg_ref, du_ref):
        g = g_ref[...].astype(F32)
        u = u_ref[...].astype(F32)
        d = d_ref[...]
        s = _sigmoid(g)
        dg_ref[...] = (d * u * (s * (1.0 + g * (1.0 - s)))).astype(BF16)
        du_ref[...] = (d * g * s).astype(BF16)

    lo = pl.BlockSpec((tq, tc), lambda i, j: (i, j))
    return _pcall(body, name=name, grid=(T // tq, nf), in_specs=[lo, lo, lo], out_specs=[lo, lo],
                  out_shape=[jax.ShapeDtypeStruct((T, FFN), BF16)] * 2,
                  compiler_params=_params(("parallel", "parallel")))(gate, up, dact)


def loss_head(y, target):
    T, D = y.shape
    tq = _tile(T, (256, 128))

    def body(y_ref, t_ref, dy_ref, l_ref):
        @pl.when(pl.program_id(0) == 0)
        def _():
            l_ref[...] = jnp.zeros_like(l_ref)

        e = y_ref[...] - t_ref[...]
        dy_ref[...] = e * (1.0 / D)
        l_ref[...] += 0.5 * jnp.sum(jnp.mean(e * e, axis=-1, keepdims=True))

    xs = pl.BlockSpec((tq, D), lambda i: (i, 0))
    return _pcall(body, name="loss_head", grid=(T // tq,), in_specs=[xs, xs],
                  out_specs=[xs, pl.BlockSpec((8, LANES), lambda i: (0, 0))],
                  out_shape=[jax.ShapeDtypeStruct((T, D), F32), jax.ShapeDtypeStruct((8, LANES), F32)],
                  compiler_params=_params(("arbitrary",)))(y, target)


def sum_arrays(arrs, out_dtype, name):
    R, C = arrs[0].shape
    tr = _tile(R, (512, 256, 128, 64, 32, 16, 8))
    n = len(arrs)

    def body(*refs):
        acc = refs[0][...].astype(F32)
        for r in refs[1:n]:
            acc = acc + r[...].astype(F32)
        refs[n][...] = acc.astype(out_dtype)

    xs = pl.BlockSpec((tr, C), lambda i: (i, 0))
    return _pcall(body, name=name, grid=(R // tr,), in_specs=[xs] * n, out_specs=xs,
                  out_shape=jax.ShapeDtypeStruct((R, C), out_dtype), compiler_params=_params(("parallel",)))(*arrs)


def adamw(w, g, m, v, name):
    R, C = w.shape
    tr = _tile(R, (256, 128, 64, 32, 16, 8))
    c1 = 1.0 - ADAM_B1 ** ADAM_STEP
    c2 = 1.0 - ADAM_B2 ** ADAM_STEP

    def body(w_ref, g_ref, m_ref, v_ref, d_ref, nm_ref, nv_ref):
        gv = g_ref[...]
        mn = ADAM_B1 * m_ref[...] + (1.0 - ADAM_B1) * gv
        vn = ADAM_B2 * v_ref[...] + (1.0 - ADAM_B2) * (gv * gv)
        nm_ref[...] = mn
        nv_ref[...] = vn
        d_ref[...] = -ADAM_LR * ((mn / c1) / (jnp.sqrt(vn / c2) + ADAM_EPS) + ADAM_WD * w_ref[...])

    xs = pl.BlockSpec((tr, C), lambda i: (i, 0))
    return _pcall(body, name=name, grid=(R // tr,), in_specs=[xs] * 4, out_specs=[xs] * 3,
                  out_shape=[jax.ShapeDtypeStruct((R, C), F32)] * 3, compiler_params=_params(("parallel",)))(w, g, m, v)


def adamw_layer(w, g, m, v, l, prev, name):
    L, r, c = w.shape
    tr = _tile(r, (256, 128, 64, 32, 16, 8))
    c1 = 1.0 - ADAM_B1 ** ADAM_STEP
    c2 = 1.0 - ADAM_B2 ** ADAM_STEP
    chained = prev is not None

    def body(w_ref, g_ref, m_ref, v_ref, *rest):
        go_ref, d_ref, nm_ref, nv_ref = rest[-4:]
        gv = g_ref[...]
        mn = ADAM_B1 * m_ref[...] + (1.0 - ADAM_B1) * gv
        vn = ADAM_B2 * v_ref[...] + (1.0 - ADAM_B2) * (gv * gv)
        go_ref[...] = gv
        nm_ref[...] = mn
        nv_ref[...] = vn
        d_ref[...] = -ADAM_LR * ((mn / c1) / (jnp.sqrt(vn / c2) + ADAM_EPS) + ADAM_WD * w_ref[...])

    ls = pl.BlockSpec((None, tr, c), lambda i: (l, i, 0))
    gs = pl.BlockSpec((tr, c), lambda i: (i, 0))
    anys = [pl.BlockSpec(memory_space=pl.ANY)] * 4 if chained else []
    return _pcall(body, name=name, grid=(r // tr,), in_specs=[ls, gs, ls, ls] + anys, out_specs=[ls] * 4,
                  out_shape=[jax.ShapeDtypeStruct((L, r, c), F32)] * 4,
                  input_output_aliases={4: 0, 5: 1, 6: 2, 7: 3} if chained else {},
                  compiler_params=_params(("parallel",)))(w, g, m, v, *(prev if chained else ()))


def add_pair(g, got, name):
    _, r, c = g.shape
    h = r // 2

    def body(c_ref, g_ref, o_ref, out_ref):
        del c_ref
        out_ref[...] = (g_ref[...].astype(F32) + o_ref[...].astype(F32)).astype(BF16)

    spec = pltpu.PrefetchScalarGridSpec(
        num_scalar_prefetch=1, grid=(4,),
        in_specs=[pl.BlockSpec((None, h, c), lambda k, cs: (k, cs[0], 0)), pl.BlockSpec((None, h, c), lambda k, cs: (k, 0, 0))],
        out_specs=pl.BlockSpec((None, h, c), lambda k, cs: (k, 0, 0)))
    return _pcall(body, name=name, grid_spec=spec, out_shape=jax.ShapeDtypeStruct((4, h, c), BF16),
                  compiler_params=_params(("arbitrary",)))(lax.axis_index("c").reshape(1).astype(jnp.int32), g, got)


def add_chips(p, b, name):
    _, h, c = p.shape
    tr = _tile(h, (256, 128, 64, 32, 16, 8))

    def body(me_ref, p_ref, b0, b1, b2, b3, out_ref):
        me = me_ref[0]
        acc = jnp.zeros(out_ref.shape, F32)
        for k, bk in enumerate((b0, b1, b2, b3)):
            acc = acc + jnp.where(me == k, p_ref[...], bk[...]).astype(F32)
        out_ref[...] = acc

    slot = lambda k: pl.BlockSpec((None, tr, c), lambda i, ms: (jnp.where(ms[0] == k, (k + 1) % 4, k), i, 0))
    spec = pltpu.PrefetchScalarGridSpec(
        num_scalar_prefetch=1, grid=(h // tr,),
        in_specs=[pl.BlockSpec((None, tr, c), lambda i, ms: (ms[0], i, 0)), slot(0), slot(1), slot(2), slot(3)],
        out_specs=pl.BlockSpec((tr, c), lambda i, ms: (i, 0)))
    me = (2 * lax.axis_index("x") + lax.axis_index("y")).reshape(1).astype(jnp.int32)
    return _pcall(body, name=name, grid_spec=spec, out_shape=jax.ShapeDtypeStruct((h, c), F32),
                  compiler_params=_params(("arbitrary",)))(me, p, b, b, b, b)


ANY = pl.BlockSpec(memory_space=pl.ANY)


def _place():
    x, y, c = lax.axis_index("x"), lax.axis_index("y"), lax.axis_index("c")
    chips = [(1 - x, y), (x, 1 - y), (1 - x, 1 - y)]
    return x, y, c, chips


def _rcopy(src, dst, ssem, rsem, dev):
    return pltpu.make_async_remote_copy(src_ref=src, dst_ref=dst, send_sem=ssem, recv_sem=rsem, device_id=dev,
                                        device_id_type=MESH)


def _half(c, h):
    return pl.ds(pl.multiple_of(c * h, 64), h)


def _me_chip():
    return 2 * lax.axis_index("x") + lax.axis_index("y")


def all_gather_weights(shards, name):
    n = len(shards)
    L = shards[0].shape[0]
    hl = L // 2

    def body(*refs):
        srcs, outs, (ssem, rsem) = refs[:n], refs[n:2 * n], refs[2 * n:]
        x, y, c, chips = _place()
        me = 2 * x + y
        sib = (x, y, 1 - c)
        mine, other = pl.ds(c * hl, hl), pl.ds((1 - c) * hl, hl)
        sends = []
        for w in range(n):
            for j, (px, py) in enumerate(chips):
                cp = _rcopy(srcs[w].at[mine], outs[w].at[me, mine], ssem.at[w, j], rsem.at[w, j], (px, py, c))
                cp.start()
                sends.append(cp)
        for j, (px, py) in enumerate(chips):
            slot = 2 * px + py
            for w in range(n):
                _rcopy(srcs[w].at[mine], outs[w].at[slot, mine], ssem.at[w, j], rsem.at[w, j], (px, py, c)).wait_recv()
                cp = _rcopy(outs[w].at[slot, mine], outs[w].at[slot, mine], ssem.at[w, 3 + j], rsem.at[w, 3 + j], sib)
                cp.start()
                sends.append(cp)
        for j, (px, py) in enumerate(chips):
            slot = 2 * px + py
            for w in range(n):
                _rcopy(outs[w].at[slot, other], outs[w].at[slot, other], ssem.at[w, 3 + j], rsem.at[w, 3 + j], sib).wait_recv()
        for cp in sends:
            cp.wait_send()

    outs = _pcall(body, name=name, in_specs=[ANY] * n, out_specs=[ANY] * n,
                  out_shape=[jax.ShapeDtypeStruct((4,) + s.shape, s.dtype) for s in shards],
                  scratch_shapes=[pltpu.SemaphoreType.DMA((n, 6)), pltpu.SemaphoreType.DMA((n, 6))])(*shards)
    me = _me_chip()
    return [lax.dynamic_update_slice(o, s[None], (me, 0, 0, 0)) for o, s in zip(outs, shards)]


def exchange_pair(gs, name):
    n = len(gs)

    def body(*refs):
        srcs, outs, (ssem, rsem) = refs[:n], refs[n:2 * n], refs[2 * n:]
        x, y, c, _ = _place()
        cps = []
        for w in range(n):
            h = srcs[w].shape[1] // 2
            cp = _rcopy(srcs[w].at[:, _half(1 - c, h)], outs[w], ssem.at[w], rsem.at[w], (x, y, 1 - c))
            cp.start()
            cps.append(cp)
        for cp in cps:
            cp.wait()

    return _pcall(body, name=name, in_specs=[ANY] * n, out_specs=[ANY] * n,
                  out_shape=[jax.ShapeDtypeStruct((4, g.shape[1] // 2, g.shape[2]), g.dtype) for g in gs],
                  scratch_shapes=[pltpu.SemaphoreType.DMA((n,)), pltpu.SemaphoreType.DMA((n,))])(*gs)


def scatter_chips(ps, name):
    n = len(ps)

    def body(*refs):
        srcs, outs, (ssem, rsem) = refs[:n], refs[n:2 * n], refs[2 * n:]
        x, y, c, chips = _place()
        me = 2 * x + y
        sends = []
        for w in range(n):
            for j, (px, py) in enumerate(chips):
                cp = _rcopy(srcs[w].at[2 * px + py], outs[w].at[me], ssem.at[w, j], rsem.at[w, j], (px, py, c))
                cp.start()
                sends.append(cp)
        for w in range(n):
            for j, (px, py) in enumerate(chips):
                _rcopy(srcs[w].at[me], outs[w].at[2 * px + py], ssem.at[w, j], rsem.at[w, j], (px, py, c)).wait_recv()
        for cp in sends:
            cp.wait_send()

    return _pcall(body, name=name, in_specs=[ANY] * n, out_specs=[ANY] * n,
                  out_shape=[jax.ShapeDtypeStruct(p.shape, p.dtype) for p in ps],
                  scratch_shapes=[pltpu.SemaphoreType.DMA((n, 3)), pltpu.SemaphoreType.DMA((n, 3))])(*ps)


def join_pair(rs, name):
    n = len(rs)

    def body(*refs):
        srcs, outs, (ssem, rsem) = refs[:n], refs[n:2 * n], refs[2 * n:]
        x, y, c, _ = _place()
        cps = []
        for w in range(n):
            h = srcs[w].shape[0]
            cp = _rcopy(srcs[w], outs[w].at[_half(c, h)], ssem.at[w], rsem.at[w], (x, y, 1 - c))
            cp.start()
            cps.append(cp)
        for w in range(n):
            h = srcs[w].shape[0]
            _rcopy(srcs[w], outs[w].at[_half(1 - c, h)], ssem.at[w], rsem.at[w], (x, y, 1 - c)).wait_recv()
        for cp in cps:
            cp.wait_send()

    outs = _pcall(body, name=name, in_specs=[ANY] * n, out_specs=[ANY] * n,
                  out_shape=[jax.ShapeDtypeStruct((2 * r.shape[0], r.shape[1]), r.dtype) for r in rs],
                  scratch_shapes=[pltpu.SemaphoreType.DMA((n,)), pltpu.SemaphoreType.DMA((n,))])(*rs)
    c = lax.axis_index("c")
    return [lax.dynamic_update_slice(o, r, (c * r.shape[0], 0)) for o, r in zip(outs, rs)]


def gather_all_devices(v, name):
    R, C = v.shape

    def body(src, out, ssem, rsem, lsem):
        x, y, c, _ = _place()
        me = 4 * x + 2 * y + c
        local = pltpu.make_async_copy(src, out.at[me], lsem)
        local.start()
        sends = []
        for j in range(1, 8):
            dx, dy, dc = (j >> 2) & 1, (j >> 1) & 1, j & 1
            px, py, pc = (1 - x if dx else x), (1 - y if dy else y), (1 - c if dc else c)
            cp = _rcopy(src, out.at[me], ssem.at[j - 1], rsem.at[j - 1], (px, py, pc))
            cp.start()
            sends.append(cp)
        for j in range(1, 8):
            dx, dy, dc = (j >> 2) & 1, (j >> 1) & 1, j & 1
            px, py, pc = (1 - x if dx else x), (1 - y if dy else y), (1 - c if dc else c)
            _rcopy(src, out.at[4 * px + 2 * py + pc], ssem.at[j - 1], rsem.at[j - 1], (px, py, pc)).wait_recv()
        for cp in sends:
            cp.wait_send()
        local.wait()

    return _pcall(body, name=name, in_specs=[ANY], out_specs=ANY, out_shape=jax.ShapeDtypeStruct((8, R, C), v.dtype),
                  scratch_shapes=[pltpu.SemaphoreType.DMA((7,)), pltpu.SemaphoreType.DMA((7,)), pltpu.SemaphoreType.DMA(())])(v)


def reduce_scatter_layer(gw, tag):
    names = list(gw)
    gs = [gw[k] for k in names]
    got = exchange_pair(gs, "rs_pair_" + tag)
    ps = [add_pair(g, o, "rs_add_pair_%s_%s" % (k, tag)) for k, g, o in zip(names, gs, got)]
    bs = scatter_chips(ps, "rs_chips_" + tag)
    rs = [add_chips(p, b, "rs_add_chips_%s_%s" % (k, tag)) for k, p, b in zip(names, ps, bs)]
    return dict(zip(names, join_pair(rs, "rs_join_" + tag)))


GATHERED = ("w_in", "w_q_b", "w_kv_b", "w_out", "w_xq", "w_xk", "w_xv", "w_xo", "w_gate", "w_up", "w_down")
KIND = {"w_q_b": "col", "w_kv_b": "col", "w_out": "row", "w_xq": "row", "w_xk": "row", "w_xv": "row", "w_xo": "col",
        "w_gate": "col", "w_up": "col", "w_down": "row"}


def wq_to_storage(w):
    lead = w.shape[:-1]
    w = w.reshape(lead + (2, MLA_QK))
    rope = jnp.pad(w[..., MLA_NOPE:], [(0, 0)] * (w.ndim - 1) + [(0, LANES - MLA_ROPE)])
    return jnp.concatenate([w[..., :MLA_NOPE].reshape(lead + (2 * LANES,)), rope.reshape(lead + (2 * LANES,))], axis=-1)


def wq_from_storage(g):
    lead = g.shape[:-1]
    nope = g[..., :2 * LANES].reshape(lead + (2, LANES))
    rope = g[..., 2 * LANES:].reshape(lead + (2, LANES))[..., :MLA_ROPE]
    return jnp.concatenate([nope, rope], axis=-1).reshape(lead + (2 * MLA_QK,))


def wcat_from_storage(w_in_l):
    w = w_in_l.transpose(1, 0, 2).reshape(D_MODEL, IN_COLS)
    zc = lambda n: jnp.zeros((D_MODEL, n), w.dtype)
    return jnp.concatenate([w[:, :2560], w[:, 2576:], zc(64), w[:, 2560:2576], zc(112)], axis=1)


def w_in_grad_storage(dwcat):
    g = jnp.concatenate([dwcat[:, :2560], dwcat[:, C_DT:C_DT + 16], dwcat[:, 2560:3648]], axis=1)
    return g.reshape(D_MODEL, 4, IN_COLS // 4).transpose(1, 0, 2)


def _ssd_par(sm):
    rows = jnp.stack([sm["dt_bias"], sm["a_log"], sm["d_skip"]]).reshape(3, SSD_GROUPS, SSD_HPG).transpose(1, 0, 2)
    return jnp.pad(rows, ((0, 0), (0, 5), (0, LANES - SSD_HPG)))


def _dt_pad(proj, B, S):
    dt = proj[:, C_DT:C_DT + 16].reshape(B, S, SSD_GROUPS, SSD_HPG)
    return jnp.pad(dt, ((0, 0), (0, 0), (0, 0), (0, LANES - SSD_HPG))).reshape(B, S, SSD_GROUPS * LANES)


def _mla_gains(g):
    return g[:MLA_NOPE].reshape(1, LANES), jnp.pad(g[MLA_NOPE:], (0, LANES - MLA_ROPE)).reshape(1, LANES)


_Q_NOPE = lambda h: 4 * (h // 2) + h % 2
_Q_ROPE = lambda h: 4 * (h // 2) + 2 + h % 2
_K_NOPE = lambda h: 2 * h
_V_BLK = lambda h: 2 * h + 1


def layer_fwd(x, memf, cos, sin, G, kw, sm, B, S, l):
    T = x.shape[0]
    n = lambda s: "%s_l%d" % (s, l)
    wv = lambda k: (KIND[k], l)
    sv = dict(x0=x)
    h1 = norm_fwd(x, sm["attn_norm_g"], D_MODEL, False, n("attn_norm"))
    proj = mm(h1, kw["wcat"], "nn", n("in_proj"))
    sv.update(h1=h1, proj=proj)
    proj3 = proj.reshape(B, S, CAT_COLS)
    xbc = conv_fwd(proj3, C_XBC // 256, kw["conv_w"], sm["conv_b"].reshape(1, -1), n("conv"))
    dtp = _dt_pad(proj, B, S)
    par = _ssd_par(sm)
    y_raw, states = ssd_fwd(xbc, dtp, par, n("ssd"))
    z = proj[:, C_Z:C_Z + SSD_INNER]
    y_ssd = norm_fwd(y_raw.reshape(T, SSD_INNER), sm["ssd_norm_g"], SSD_INNER // SSD_GROUPS, False, n("ssd_norm"), gate=z)
    sv.update(xbc=xbc, states=states, y_raw=y_raw)
    qn = norm_fwd(proj[:, C_QA:C_QA + Q_LORA], sm["q_a_norm_g"], Q_LORA, False, n("q_a_norm"))
    kvn = norm_fwd(proj[:, C_KVA:C_KVA + KV_LORA], sm["kv_a_norm_g"], KV_LORA, False, n("kv_a_norm"))
    q_raw = mm(qn, G["w_q_b"], "nn", n("q_b"), wv=wv("w_q_b"))
    kv_raw = mm(kvn, G["w_kv_b"], "nn", n("kv_b"), wv=wv("w_kv_b"))
    gqn, gqr = _mla_gains(sm["mla_q_norm_g"])
    gkn, gkr = _mla_gains(sm["mla_k_norm_g"])
    qh = qk_fwd(q_raw, _Q_NOPE, q_raw, _Q_ROPE, False, gqn, gqr, cos, sin, n("q_norm_rope"))
    kh = qk_fwd(kv_raw, _K_NOPE, proj, C_KR // LANES, True, gkn, gkr, cos, sin, n("k_norm_rope"))
    vv = kv_raw.astype(BF16)
    y_mla = attn_fwd(qh.reshape(B, S, -1), kh.reshape(B, S, -1), vv.reshape(B, S, -1), MLA_HEADS, 2 * LANES, MLA_V,
                     MLA_QK ** -0.5, True, n("mla_attn"), v_idx=_V_BLK)
    ycat = jnp.concatenate([y_ssd, y_mla.reshape(T, -1)], axis=1)
    x1 = mm(ycat, G["w_out"], "nn", n("out_proj"), add=x, wv=wv("w_out"))
    sv.update(qn=qn, kvn=kvn, q_raw=q_raw, kv_raw=kv_raw, qh=qh, kh=kh, vv=vv, ycat=ycat, x1=x1)
    h2 = norm_fwd(x1, sm["xattn_norm_g"], D_MODEL, False, n("xattn_norm"))
    mn = norm_fwd(memf, sm["mem_norm_g"], D_MODEL, False, n("mem_norm"))
    xq = mm(h2, G["w_xq"], "nn", n("xq"), wv=wv("w_xq"))
    xk = mm(mn, G["w_xk"], "nn", n("xk"), wv=wv("w_xk"))
    xv = mm(mn, G["w_xv"], "nn", n("xv"), wv=wv("w_xv"), out_dtype=BF16)
    xqn = norm_fwd(xq, sm["xq_norm_g"], X_HEAD_DIM, True, n("xq_norm"))
    xkn = norm_fwd(xk, sm["xk_norm_g"], X_HEAD_DIM, True, n("xk_norm"))
    ML = memf.shape[0] // B
    xo = attn_fwd(xqn.reshape(B, S, -1), xkn.reshape(B, ML, -1), xv.reshape(B, ML, -1), X_HEADS, X_HEAD_DIM, X_HEAD_DIM,
                  X_HEAD_DIM ** -0.5, False, n("x_attn"))
    x2 = mm(xo.reshape(T, -1), G["w_xo"], "nn", n("xo_proj"), add=x1, wv=wv("w_xo"))
    sv.update(h2=h2, mn=mn, xq=xq, xk=xk, xqn=xqn, xkn=xkn, xv=xv, xo=xo, x2=x2)
    h3 = norm_fwd(x2, sm["ffn_norm_g"], D_MODEL, False, n("ffn_norm"))
    gate = mm(h3, G["w_gate"], "nn", n("gate"), out_dtype=BF16, wv=wv("w_gate"))
    up = mm(h3, G["w_up"], "nn", n("up"), out_dtype=BF16, wv=wv("w_up"))
    act = swiglu_fwd(gate, up, n("swiglu"))
    x3 = mm(act, G["w_down"], "nn", n("down"), add=x2, wv=wv("w_down"))
    sv.update(h3=h3, gate=gate, up=up, act=act)
    return x3, sv


def layer_bwd(dx3, sv, memf, cos, sin, G, kw, sm, B, S, l):
    T = dx3.shape[0]
    n = lambda s: "%s_bwd_l%d" % (s, l)
    wv = lambda k: (KIND[k], l)
    dw = lambda k, a, b: mm(a, b, "tn", n(k + "_dw"), out_dtype=BF16, out_wv=KIND[k])
    gw, gs = {}, {}
    dact = mm(dx3, G["w_down"], "nt", n("down_dx"), wv=wv("w_down"))
    gw["w_down"] = dw("w_down", sv["act"], dx3)
    dgate, dup = swiglu_bwd(sv["gate"], sv["up"], dact, n("swiglu"))
    dh3 = mm(dgate, G["w_gate"], "nt", n("gate_dx"), wv=wv("w_gate"))
    dh3 = mm(dup, G["w_up"], "nt", n("up_dx"), wv=wv("w_up"), add=dh3)
    gw["w_gate"] = dw("w_gate", sv["h3"], dgate)
    gw["w_up"] = dw("w_up", sv["h3"], dup)
    dx2, gs["ffn_norm_g"] = norm_bwd(sv["x2"], sm["ffn_norm_g"], dh3, D_MODEL, False, n("ffn_norm"), add=dx3)
    dxo = mm(dx2, G["w_xo"], "nt", n("xo_dx"), wv=wv("w_xo"))
    gw["w_xo"] = dw("w_xo", sv["xo"].reshape(T, -1), dx2)
    ML = memf.shape[0] // B
    dxqn, dxkn, dxv = attn_bwd(sv["xqn"].reshape(B, S, -1), sv["xkn"].reshape(B, ML, -1), sv["xv"].reshape(B, ML, -1),
                               dxo.reshape(B, S, -1), X_HEADS, X_HEAD_DIM, X_HEAD_DIM, X_HEAD_DIM ** -0.5, False, n("x_attn"))
    dxq, gs["xq_norm_g"] = norm_bwd(sv["xq"], sm["xq_norm_g"], dxqn.reshape(T, -1), X_HEAD_DIM, True, n("xq_norm"))
    dxk, gs["xk_norm_g"] = norm_bwd(sv["xk"], sm["xk_norm_g"], dxkn.reshape(B * ML, -1), X_HEAD_DIM, True, n("xk_norm"))
    dxv = dxv.reshape(B * ML, -1)
    dmn = mm(dxk, G["w_xk"], "nt", n("xk_dx"), wv=wv("w_xk"))
    dmn = mm(dxv, G["w_xv"], "nt", n("xv_dx"), wv=wv("w_xv"), add=dmn)
    gw["w_xk"] = dw("w_xk", sv["mn"], dxk)
    gw["w_xv"] = dw("w_xv", sv["mn"], dxv)
    _, gs["mem_norm_g"] = norm_bwd(memf, sm["mem_norm_g"], dmn, D_MODEL, False, n("mem_norm"))
    dh2 = mm(dxq, G["w_xq"], "nt", n("xq_dx"), wv=wv("w_xq"))
    gw["w_xq"] = dw("w_xq", sv["h2"], dxq)
    dx1, gs["xattn_norm_g"] = norm_bwd(sv["x1"], sm["xattn_norm_g"], dh2, D_MODEL, False, n("xattn_norm"), add=dx2)
    dycat = mm(dx1, G["w_out"], "nt", n("out_dx"), wv=wv("w_out"))
    gw["w_out"] = dw("w_out", sv["ycat"], dx1)
    dqh, dkh, dvv = attn_bwd(sv["qh"].reshape(B, S, -1), sv["kh"].reshape(B, S, -1), sv["vv"].reshape(B, S, -1),
                             dycat[:, SSD_INNER:].reshape(B, S, -1), MLA_HEADS, 2 * LANES, MLA_V, MLA_QK ** -0.5, True,
                             n("mla_attn"), v_idx=_V_BLK)
    gqn, gqr = _mla_gains(sm["mla_q_norm_g"])
    gkn, gkr = _mla_gains(sm["mla_k_norm_g"])
    dq_n, dq_r, dgqn, dgqr = qk_bwd(sv["q_raw"], _Q_NOPE, sv["q_raw"], _Q_ROPE, False, gqn, gqr, cos, sin,
                                    dqh.reshape(T, -1), n("q_norm_rope"))
    dk_n, dk_rope, dgkn, dgkr = qk_bwd(sv["kv_raw"], _K_NOPE, sv["proj"], C_KR // LANES, True, gkn, gkr, cos, sin,
                                       dkh.reshape(T, -1), n("k_norm_rope"))
    gs["mla_q_norm_g"] = jnp.concatenate([dgqn[0], dgqr[0, :MLA_ROPE]])
    gs["mla_k_norm_g"] = jnp.concatenate([dgkn[0], dgkr[0, :MLA_ROPE]])
    dq_raw = jnp.concatenate([dq_n.reshape(T, 4, 2 * LANES), dq_r.reshape(T, 4, 2 * LANES)], axis=2).reshape(T, -1)
    dkv_raw = jnp.concatenate([dk_n.reshape(T, MLA_HEADS, LANES), dvv.reshape(T, MLA_HEADS, LANES)], axis=2).reshape(T, -1)
    dqn = mm(dq_raw, G["w_q_b"], "nt", n("q_b_dx"), wv=wv("w_q_b"))
    gw["w_q_b"] = dw("w_q_b", sv["qn"], dq_raw)
    dkvn = mm(dkv_raw, G["w_kv_b"], "nt", n("kv_b_dx"), wv=wv("w_kv_b"))
    gw["w_kv_b"] = dw("w_kv_b", sv["kvn"], dkv_raw)
    proj = sv["proj"]
    dq_a, gs["q_a_norm_g"] = norm_bwd(proj[:, C_QA:C_QA + Q_LORA], sm["q_a_norm_g"], dqn, Q_LORA, False, n("q_a_norm"))
    dkv_a, gs["kv_a_norm_g"] = norm_bwd(proj[:, C_KVA:C_KVA + KV_LORA], sm["kv_a_norm_g"], dkvn, KV_LORA, False,
                                        n("kv_a_norm"))
    z = proj[:, C_Z:C_Z + SSD_INNER]
    dy_raw, dz, gs["ssd_norm_g"] = norm_bwd(sv["y_raw"].reshape(T, -1), sm["ssd_norm_g"], dycat[:, :SSD_INNER],
                                            SSD_INNER // SSD_GROUPS, False, n("ssd_norm"), gate=z)
    dtp = _dt_pad(proj, B, S)
    par = _ssd_par(sm)
    dxs, dBm, dCm, ddt, dpar = ssd_bwd(sv["xbc"], dtp, par, sv["states"], dy_raw.reshape(B, S, -1), n("ssd"))
    gs["dt_bias"] = dpar[:, 0, :SSD_HPG].reshape(-1)
    gs["a_log"] = dpar[:, 1, :SSD_HPG].reshape(-1)
    gs["d_skip"] = dpar[:, 2, :SSD_HPG].reshape(-1)
    dxbc_act = jnp.concatenate([dxs, dBm, dCm], axis=2)
    dxbc, dconv_w, dconv_b = conv_bwd(proj.reshape(B, S, -1), C_XBC // 256, kw["conv_w"], sm["conv_b"].reshape(1, -1), dxbc_act,
                                      n("conv"))
    gs["conv_w"] = dconv_w[:SSD_CONV]
    gs["conv_b"] = dconv_b[0]
    ddt16 = ddt.reshape(T, SSD_GROUPS, LANES)[:, :, :SSD_HPG].reshape(T, 16)
    dproj = jnp.concatenate([dz, dxbc.reshape(T, -1), dq_a, dkv_a, dk_rope, ddt16, jnp.zeros((T, 112), F32)], axis=1)
    dh1 = mm(dproj, kw["wcat"], "nt", n("in_dx"))
    gw["w_in"] = w_in_grad_storage(mm(sv["h1"], dproj, "tn", n("in_dw"), out_dtype=BF16))
    dx0, gs["attn_norm_g"] = norm_bwd(sv["x0"], sm["attn_norm_g"], dh1, D_MODEL, False, n("attn_norm"), add=dx1)
    gs = {k: v.reshape(-1) for k, v in gs.items()}
    return dx0, gw, gs


def _small_pack(vals):
    flat = jnp.concatenate([vals[l][k].reshape(-1) for l in range(DEPTH) for k in SMALL_RS])
    n = flat.shape[0]
    rows = -(-n // LANES)
    rows = -(-rows // 8) * 8
    return jnp.pad(flat, (0, rows * LANES - n)).reshape(rows, LANES), n


def _small_unpack(flat, shapes):
    v = flat.reshape(-1)
    out, off = [dict() for _ in range(DEPTH)], 0
    for l in range(DEPTH):
        for k in SMALL_RS:
            sz = shapes[k]
            out[l][k] = v[off:off + sz]
            off += sz
    return out


def _adam_2d(a):
    if a.ndim == 2:
        return a
    return a.reshape(a.shape[0] * a.shape[1], a.shape[2])


def kernel(x, mem, positions, attn_norm_g, w_in, conv_w, conv_b, dt_bias, a_log, d_skip, ssd_norm_g, q_a_norm_g, w_q_b, kv_a_norm_g, w_kv_b, mla_q_norm_g, mla_k_norm_g, w_out, xattn_norm_g, mem_norm_g, w_xq, w_xk, w_xv, xq_norm_g, xk_norm_g, w_xo, ffn_norm_g, w_gate, w_up, w_down, loss_target, m_attn_norm_g, m_w_in, m_conv_w, m_conv_b, m_dt_bias, m_a_log, m_d_skip, m_ssd_norm_g, m_q_a_norm_g, m_w_q_b, m_kv_a_norm_g, m_w_kv_b, m_mla_q_norm_g, m_mla_k_norm_g, m_w_out, m_xattn_norm_g, m_mem_norm_g, m_w_xq, m_w_xk, m_w_xv, m_xq_norm_g, m_xk_norm_g, m_w_xo, m_ffn_norm_g, m_w_gate, m_w_up, m_w_down, v_attn_norm_g, v_w_in, v_conv_w, v_conv_b, v_dt_bias, v_a_log, v_d_skip, v_ssd_norm_g, v_q_a_norm_g, v_w_q_b, v_kv_a_norm_g, v_w_kv_b, v_mla_q_norm_g, v_mla_k_norm_g, v_w_out, v_xattn_norm_g, v_mem_norm_g, v_w_xq, v_w_xk, v_w_xv, v_xq_norm_g, v_xk_norm_g, v_w_xo, v_ffn_norm_g, v_w_gate, v_w_up, v_w_down):
    loc = locals()
    W = {k: loc[k] for k in WEIGHTS}
    Mo = {k: loc["m_" + k] for k in WEIGHTS}
    Vo = {k: loc["v_" + k] for k in WEIGHTS}
    B, S, D = x.shape
    T = B * S
    xf = x.reshape(T, D)
    memf = mem.reshape(-1, D)
    tf = loss_target.reshape(T, D)
    cos, sin = rope_tables(positions.reshape(T, 1).astype(F32))

    shards = [(wq_to_storage(W[k]) if k == "w_q_b" else W[k]).astype(BF16) for k in GATHERED]
    G = dict(zip(GATHERED, all_gather_weights(shards, "all_gather_weights")))
    cw = gather_all_devices(conv_w.reshape(DEPTH * SSD_CONV, -1), "gather_conv_w")[0::2]
    cw = cw.transpose(1, 0, 2).reshape(DEPTH, SSD_CONV, SSD_CONV_DIM)
    kws, sms = [], []
    for l in range(DEPTH):
        kws.append(dict(wcat=wcat_from_storage(G["w_in"][:, l]), conv_w=jnp.pad(cw[l], ((0, 8 - SSD_CONV), (0, 0)))))
        sms.append({k: W[k][l] for k in SMALL})

    saved = []
    h = xf
    for l in range(DEPTH):
        h, sv = layer_fwd(h, memf, cos, sin, G, kws[l], sms[l], B, S, l)
        saved.append(sv)
    dh, lpart = loss_head(h, tf)
    loss = lax.psum(lpart[0, 0], ("x", "y", "c"))

    grads, delta, new_m, new_v = {}, {}, {}, {}
    chain = {k: None for k in GATHERED}
    gsmall = [None] * DEPTH
    for l in reversed(range(DEPTH)):
        dh, gw, gs = layer_bwd(dh, saved[l], memf, cos, sin, G, kws[l], sms[l], B, S, l)
        saved[l] = None
        gsmall[l] = gs
        red = reduce_scatter_layer(gw, "l%d" % l)
        for k in GATHERED:
            g = wq_from_storage(red[k]) if k == "w_q_b" else red[k]
            chain[k] = adamw_layer(W[k], g, Mo[k], Vo[k], l, chain[k], "adamw_%s_l%d" % (k, l))
    for k in GATHERED:
        grads[k], delta[k], new_m[k], new_v[k] = chain[k]
    grad_x = dh.reshape(B, S, D)

    spack, _ = _small_pack(gsmall)
    allv = gather_all_devices(spack, "gather_small_grads")
    ssum = sum_arrays([allv[i] for i in range(8)], F32, "sum_small_grads")
    sizes = {k: int(np.prod(W[k].shape[1:])) for k in SMALL}
    sizes["conv_w"] = SSD_CONV * SSD_CONV_DIM
    gsm = _small_unpack(ssum, sizes)
    for k in SMALL_RS:
        g = jnp.stack([gsm[l][k] for l in range(DEPTH)])
        if k == "conv_w":
            cs = SSD_CONV_DIM // 4
            g = lax.dynamic_slice_in_dim(g.reshape(DEPTH, SSD_CONV, SSD_CONV_DIM), _me_chip() * cs, cs, axis=2)
        shp = W[k].shape
        grads[k] = g.reshape(shp)
        d, nm, nv = adamw(_adam_2d(W[k]), _adam_2d(grads[k]), _adam_2d(Mo[k]), _adam_2d(Vo[k]), "adamw_" + k)
        delta[k], new_m[k], new_v[k] = d.reshape(shp), nm.reshape(shp), nv.reshape(shp)

    return (loss, grad_x, *[grads[k] for k in WEIGHTS], *[delta[k] for k in WEIGHTS], *[new_m[k] for k in WEIGHTS],
            *[new_v[k] for k in WEIGHTS])
```

```python
import functools
import math

import numpy as np
import jax
import jax.numpy as jnp
from jax import lax
from jax.experimental import pallas as pl
from jax.experimental.pallas import tpu as pltpu

F32 = jnp.float32
BF16 = jnp.bfloat16
MESH = pl.DeviceIdType.MESH
HI = lax.Precision.HIGHEST

D_MODEL = 2048
DEPTH = 4
SSD_INNER = 1024
SSD_HEAD_DIM = 64
SSD_GROUPS = 2
SSD_HPG = 8
SSD_STATE = 128
SSD_CONV = 4
SSD_CHUNK = 128
SSD_CONV_DIM = 1536
SSD_GW = SSD_HPG * SSD_HEAD_DIM
MLA_HEADS = 8
MLA_NOPE = 128
MLA_ROPE = 64
MLA_QK = 192
MLA_V = 128
Q_LORA = 512
KV_LORA = 512
ROPE_THETA = 10000.0
X_HEADS = 4
X_HEAD_DIM = 128
X_INNER = 512
FFN = 5632
IN_COLS = 3664
EPS = 1e-6
ADAM_LR, ADAM_B1, ADAM_B2, ADAM_EPS, ADAM_WD, ADAM_STEP = 0.001, 0.9, 0.999, 1e-08, 0.01, 10

C_Z, C_XBC, C_QA, C_KVA, C_KR, C_DT, CAT_COLS = 0, 1024, 2560, 3072, 3584, 3712, 3840

LANES = 128
VMEM_LIMIT = 56 * 1024 * 1024
MM_OPERAND_TILE_BYTES = 6 * 1024 * 1024

SMALL = ("attn_norm_g", "conv_b", "dt_bias", "a_log", "d_skip", "ssd_norm_g", "q_a_norm_g", "kv_a_norm_g",
         "mla_q_norm_g", "mla_k_norm_g", "xattn_norm_g", "mem_norm_g", "xq_norm_g", "xk_norm_g", "ffn_norm_g")
SMALL_RS = SMALL + ("conv_w",)
WEIGHTS = ("attn_norm_g", "w_in", "conv_w", "conv_b", "dt_bias", "a_log", "d_skip", "ssd_norm_g", "q_a_norm_g", "w_q_b",
           "kv_a_norm_g", "w_kv_b", "mla_q_norm_g", "mla_k_norm_g", "w_out", "xattn_norm_g", "mem_norm_g", "w_xq", "w_xk",
           "w_xv", "xq_norm_g", "xk_norm_g", "w_xo", "ffn_norm_g", "w_gate", "w_up", "w_down")


def _pcall(body, **kw):
    return pl.pallas_call(body, **kw)


def _params(sem):
    return pltpu.CompilerParams(dimension_semantics=sem, vmem_limit_bytes=VMEM_LIMIT)


def _tile(n, cands):
    for t in cands:
        if n % t == 0:
            return t
    return n


def _sigmoid(z):
    return 1.0 / (1.0 + jnp.exp(-z))


def _mtile(n, cap):
    if n % LANES:
        return n
    q = n // LANES
    t = LANES * max(d for d in range(1, q + 1) if q % d == 0 and LANES * d <= cap)
    return n if (t < 512 and n <= 2048) else t


def mm(a, b, mode, name, add=None, out_dtype=F32, wv=None, out_wv=None):
    kind, l = wv if wv is not None else (None, None)
    if kind == "row":
        b, kind = b[l].reshape(-1, b.shape[-1]), None
    if out_wv == "row":
        res = mm(a, b, mode, name, add=add, out_dtype=out_dtype, wv=wv if kind else None)
        return res.reshape(4, res.shape[0] // 4, res.shape[1])
    if kind == "col":
        _, _, r, c = b.shape
        bshape = (r, 4 * c)
    else:
        bshape = b.shape
    if mode == "nn":
        (M, K), (K2, N) = a.shape, bshape
    elif mode == "nt":
        (M, K), (N, K2) = a.shape, bshape
    else:
        (K, M), (K2, N) = a.shape, bshape
    assert K == K2, (a.shape, bshape, mode)
    m_unit = M
    n_unit = N // 4 if (out_wv == "col" or (kind == "col" and mode == "nn")) else N
    k_unit = K // 4 if (kind == "col" and mode == "nt") else K
    tm = _mtile(m_unit, 512)
    tn = _mtile(n_unit, 1024)
    tk = _mtile(k_unit, 2048)
    if tk > 1024 and tk * max(tm * a.dtype.itemsize, tn * b.dtype.itemsize) > MM_OPERAND_TILE_BYTES:
        tk = _mtile(k_unit, 1024)
    nk = K // tk
    qm, qn, qk = m_unit // tm, n_unit // tn, k_unit // tk
    dn = {"nn": (((1,), (0,)), ((), ())), "nt": (((1,), (1,)), ((), ())), "tn": (((0,), (0,)), ((), ()))}[mode]
    has_add = add is not None

    def body(a_ref, b_ref, *rest):
        rest = list(rest)
        c_ref = rest.pop(0) if has_add else None
        o_ref = rest.pop(0)
        d = lax.dot_general(a_ref[...].astype(BF16), b_ref[...].astype(BF16), dn, preferred_element_type=F32)

        def finish(res):
            if has_add:
                res = res + c_ref[...].astype(F32)
            o_ref[...] = res.astype(out_dtype)

        if nk == 1:
            finish(d)
            return
        acc = rest.pop(0)
        k = pl.program_id(2)

        @pl.when(k == 0)
        def _():
            acc[...] = d

        if nk > 2:
            @pl.when((k > 0) & (k < nk - 1))
            def _():
                acc[...] += d

        @pl.when(k == nk - 1)
        def _():
            finish(acc[...] + d)

    if mode == "tn":
        a_spec = pl.BlockSpec((tk, tm), lambda i, j, k: (k, i))
    else:
        a_spec = pl.BlockSpec((tm, tk), lambda i, j, k: (i, k))
    if kind is None:
        b_spec = (pl.BlockSpec((tn, tk), lambda i, j, k: (j, k)) if mode == "nt"
                  else pl.BlockSpec((tk, tn), lambda i, j, k: (k, j)))
    elif mode == "nn":
        b_spec = pl.BlockSpec((None, None, tk, tn), lambda i, j, k: (l, j // qn, k, j % qn))
    elif mode == "nt":
        b_spec = pl.BlockSpec((None, None, tn, tk), lambda i, j, k: (l, k // qk, j, k % qk))
    else:
        raise ValueError((mode, kind))
    if out_wv == "col":
        o_spec = pl.BlockSpec((None, tm, tn), lambda i, j, k: (j // qn, i, j % qn))
        out_shape = jax.ShapeDtypeStruct((4, M, N // 4), out_dtype)
    else:
        o_spec = pl.BlockSpec((tm, tn), lambda i, j, k: (i, j))
        out_shape = jax.ShapeDtypeStruct((M, N), out_dtype)
    in_specs = [a_spec, b_spec] + ([o_spec] if has_add else [])
    args = (a, b) + ((add,) if has_add else ())
    return _pcall(
        body, name=name, grid=(M // tm, N // tn, nk), in_specs=in_specs, out_specs=o_spec,
        out_shape=out_shape, scratch_shapes=[pltpu.VMEM((tm, tn), F32)] if nk > 1 else [],
        compiler_params=_params(("parallel", "parallel", "arbitrary")))(*args)


def norm_fwd(x, gain, W, shared, name, gate=None, out_dtype=BF16):
    T, C = x.shape
    nb = C // W
    tq = _tile(T, (256, 128))
    g2 = gain.reshape(1, -1).astype(F32)
    gated = gate is not None

    def body(*refs):
        if gated:
            x_ref, z_ref, g_ref, o_ref = refs
        else:
            x_ref, g_ref, o_ref = refs
        u = x_ref[...].astype(F32)
        if gated:
            z = z_ref[...]
            u = u * (z * _sigmoid(z))
        ms = jnp.mean(u * u, axis=-1, keepdims=True)
        o_ref[...] = (u * lax.rsqrt(ms + EPS) * g_ref[...]).astype(out_dtype)

    xs = pl.BlockSpec((tq, W), lambda j, i: (i, j))
    gs = pl.BlockSpec((1, W), (lambda j, i: (0, 0)) if shared else (lambda j, i: (0, j)))
    in_specs = [xs] + ([xs] if gated else []) + [gs]
    args = (x,) + ((gate,) if gated else ()) + (g2,)
    return _pcall(body, name=name, grid=(nb, T // tq), in_specs=in_specs, out_specs=xs,
                  out_shape=jax.ShapeDtypeStruct((T, C), out_dtype),
                  compiler_params=_params(("parallel", "parallel")))(*args)


def norm_bwd(x, gain, dy, W, shared, name, gate=None, add=None):
    T, C = x.shape
    nb = C // W
    tq = _tile(T, (256, 128))
    g2 = gain.reshape(1, -1).astype(F32)
    gated = gate is not None
    has_add = add is not None
    GC = W if shared else C

    def body(*refs):
        refs = list(refs)
        x_ref = refs.pop(0)
        z_ref = refs.pop(0) if gated else None
        g_ref = refs.pop(0)
        dy_ref = refs.pop(0)
        a_ref = refs.pop(0) if has_add else None
        dx_ref = refs.pop(0)
        dz_ref = refs.pop(0) if gated else None
        dg_ref = refs.pop(0)
        j, i = pl.program_id(0), pl.program_id(1)
        first = (i == 0) & (j == 0) if shared else (i == 0)

        @pl.when(first)
        def _():
            dg_ref[...] = jnp.zeros_like(dg_ref)

        xv = x_ref[...].astype(F32)
        if gated:
            z = z_ref[...]
            s = _sigmoid(z)
            u = xv * (z * s)
        else:
            u = xv
        rstd = lax.rsqrt(jnp.mean(u * u, axis=-1, keepdims=True) + EPS)
        uh = u * rstd
        dyv = dy_ref[...].astype(F32)
        dg_ref[...] += jnp.sum(dyv * uh, axis=0, keepdims=True)
        dyg = dyv * g_ref[...]
        du = rstd * (dyg - uh * jnp.mean(dyg * uh, axis=-1, keepdims=True))
        if gated:
            dx = du * (z * s)
            dz_ref[...] = du * xv * (s * (1.0 + z * (1.0 - s)))
        else:
            dx = du
        if has_add:
            dx = dx + a_ref[...]
        dx_ref[...] = dx

    xs = pl.BlockSpec((tq, W), lambda j, i: (i, j))
    gs = pl.BlockSpec((1, W), (lambda j, i: (0, 0)) if shared else (lambda j, i: (0, j)))
    in_specs = [xs] + ([xs] if gated else []) + [gs, xs] + ([xs] if has_add else [])
    args = (x,) + ((gate,) if gated else ()) + (g2, dy) + ((add,) if has_add else ())
    out_specs = [xs] + ([xs] if gated else []) + [gs]
    out_shape = [jax.ShapeDtypeStruct((T, C), F32)] + ([jax.ShapeDtypeStruct((T, C), F32)] if gated else []) \
        + [jax.ShapeDtypeStruct((1, GC), F32)]
    return _pcall(body, name=name, grid=(nb, T // tq), in_specs=in_specs, out_specs=out_specs, out_shape=out_shape,
                  compiler_params=_params(("arbitrary", "arbitrary")))(*args)


def _inv_freq_row():
    inv = 1.0 / (ROPE_THETA ** (np.arange(0, MLA_ROPE, 2, dtype=np.float32) / MLA_ROPE))
    row = np.zeros((1, LANES), np.float32)
    row[0, :32] = inv
    row[0, 32:64] = inv
    return jnp.asarray(row)


def rope_tables(pos):
    T = pos.shape[0]
    tq = _tile(T, (512, 256, 128))

    def body(p_ref, f_ref, c_ref, s_ref):
        ang = p_ref[...] * f_ref[...]
        lane = lax.broadcasted_iota(jnp.int32, ang.shape, 1)
        c_ref[...] = jnp.where(lane < 64, jnp.cos(ang), 0.0)
        sn = jnp.sin(ang)
        s_ref[...] = jnp.where(lane < 32, -sn, jnp.where(lane < 64, sn, 0.0))

    ts = pl.BlockSpec((tq, LANES), lambda i: (i, 0))
    return _pcall(body, name="rope_tables", grid=(T // tq,),
                  in_specs=[pl.BlockSpec((tq, 1), lambda i: (i, 0)), pl.BlockSpec((1, LANES), lambda i: (0, 0))],
                  out_specs=[ts, ts], out_shape=[jax.ShapeDtypeStruct((T, LANES), F32)] * 2,
                  compiler_params=_params(("parallel",)))(pos, _inv_freq_row())


def _swap_halves(r):
    lane = lax.broadcasted_iota(jnp.int32, r.shape, 1)
    return jnp.where(lane < 32, pltpu.roll(r, 96, 1), pltpu.roll(r, 32, 1))


def qk_fwd(nope_arr, nope_blk0, rope_arr, rope_blk0, rope_shared, gn, gr, cos, sin, name):
    T = nope_arr.shape[0]
    tq = _tile(T, (256, 128))

    col = lambda blk: slice(blk * LANES, (blk + 1) * LANES)

    def body(n_ref, r_ref, gn_ref, gr_ref, c_ref, s_ref, o_ref):
        for h in range(MLA_HEADS):
            n = n_ref[:, col(nope_blk0(h))]
            r = r_ref[...] if rope_shared else r_ref[:, col(rope_blk0(h))]
            ms = (jnp.sum(n * n, axis=-1, keepdims=True) + jnp.sum(r * r, axis=-1, keepdims=True)) * (1.0 / MLA_QK)
            rstd = lax.rsqrt(ms + EPS)
            rn = r * rstd * gr_ref[...]
            rr = rn * c_ref[...] + _swap_halves(rn) * s_ref[...]
            o_ref[:, col(2 * h)] = (n * rstd * gn_ref[...]).astype(BF16)
            o_ref[:, col(2 * h + 1)] = rr.astype(BF16)

    wide = lambda arr: pl.BlockSpec((tq, arr.shape[1]), lambda i: (i, 0))
    rspec = pl.BlockSpec((tq, LANES), lambda i: (i, rope_blk0)) if rope_shared else wide(rope_arr)
    row = pl.BlockSpec((1, LANES), lambda i: (0, 0))
    tab = pl.BlockSpec((tq, LANES), lambda i: (i, 0))
    return _pcall(body, name=name, grid=(T // tq,),
                  in_specs=[wide(nope_arr), rspec, row, row, tab, tab],
                  out_specs=pl.BlockSpec((tq, MLA_HEADS * 2 * LANES), lambda i: (i, 0)),
                  out_shape=jax.ShapeDtypeStruct((T, MLA_HEADS * 2 * LANES), BF16),
                  compiler_params=_params(("parallel",)))(nope_arr, rope_arr, gn, gr, cos, sin)


def qk_bwd(nope_arr, nope_blk0, rope_arr, rope_blk0, rope_shared, gn, gr, cos, sin, dout, name, fill=None, fill_blk=None):
    T = nope_arr.shape[0]
    tq = _tile(T, (256, 128))
    col = lambda blk: slice(blk * LANES, (blk + 1) * LANES)

    def body(*refs):
        refs = list(refs)
        n_ref, r_ref, gn_ref, gr_ref, c_ref, s_ref, d_ref = refs[:7]
        f_ref = refs[7] if rope_shared else None
        outs = refs[8:] if rope_shared else refs[7:]
        draw_ref = outs[0]
        dsh_ref = outs[1] if rope_shared else None
        dgn_ref, dgr_ref = outs[-2:]

        @pl.when(pl.program_id(0) == 0)
        def _():
            dgn_ref[...] = jnp.zeros_like(dgn_ref)
            dgr_ref[...] = jnp.zeros_like(dgr_ref)

        dgn = jnp.zeros((1, LANES), F32)
        dgr = jnp.zeros((1, LANES), F32)
        dshared = jnp.zeros((tq, LANES), F32)
        for h in range(MLA_HEADS):
            n = n_ref[:, col(nope_blk0(h))]
            r = r_ref[...] if rope_shared else r_ref[:, col(rope_blk0(h))]
            ms = (jnp.sum(n * n, axis=-1, keepdims=True) + jnp.sum(r * r, axis=-1, keepdims=True)) * (1.0 / MLA_QK)
            rstd = lax.rsqrt(ms + EPS)
            nh = n * rstd
            rh = r * rstd
            d_n = d_ref[:, col(2 * h)].astype(F32)
            d_rr = d_ref[:, col(2 * h + 1)].astype(F32)
            d_rn = d_rr * c_ref[...] - _swap_halves(d_rr) * s_ref[...]
            dgn = dgn + jnp.sum(d_n * nh, axis=0, keepdims=True)
            dgr = dgr + jnp.sum(d_rn * rh, axis=0, keepdims=True)
            dng = d_n * gn_ref[...]
            drg = d_rn * gr_ref[...]
            mean = (jnp.sum(dng * nh, axis=-1, keepdims=True) + jnp.sum(drg * rh, axis=-1, keepdims=True)) * (1.0 / MLA_QK)
            draw_ref[:, col(nope_blk0(h))] = rstd * (dng - nh * mean)
            drope = rstd * (drg - rh * mean)
            if rope_shared:
                dshared = dshared + drope
                draw_ref[:, col(fill_blk(h))] = f_ref[:, col(h)]
            else:
                draw_ref[:, col(rope_blk0(h))] = drope
        dgn_ref[...] += dgn
        dgr_ref[...] += dgr
        if rope_shared:
            dsh_ref[...] = dshared

    wide = lambda arr: pl.BlockSpec((tq, arr.shape[1]), lambda i: (i, 0))
    rspec = pl.BlockSpec((tq, LANES), lambda i: (i, rope_blk0)) if rope_shared else wide(rope_arr)
    row = pl.BlockSpec((1, LANES), lambda i: (0, 0))
    tab = pl.BlockSpec((tq, LANES), lambda i: (i, 0))
    raw = pl.BlockSpec((tq, 2 * MLA_HEADS * LANES), lambda i: (i, 0))
    in_specs = [wide(nope_arr), rspec, row, row, tab, tab, wide(dout)] + ([wide(fill)] if rope_shared else [])
    out_specs = [raw] + ([tab] if rope_shared else []) + [row, row]
    out_shape = ([jax.ShapeDtypeStruct((T, 2 * MLA_HEADS * LANES), F32)]
                 + ([jax.ShapeDtypeStruct((T, LANES), F32)] if rope_shared else [])
                 + [jax.ShapeDtypeStruct((1, LANES), F32)] * 2)
    args = (nope_arr, rope_arr, gn, gr, cos, sin, dout) + ((fill,) if rope_shared else ())
    return _pcall(body, name=name, grid=(T // tq,), in_specs=in_specs, out_specs=out_specs, out_shape=out_shape,
                  compiler_params=_params(("arbitrary",)))(*args)


def _scores(q, k, scale, causal, row0):
    s = lax.dot_general(q, k, (((1,), (1,)), ((), ())), preferred_element_type=F32) * scale
    if causal:
        qi = row0 + lax.broadcasted_iota(jnp.int32, s.shape, 0)
        ki = lax.broadcasted_iota(jnp.int32, s.shape, 1)
        s = jnp.where(qi >= ki, s, -1e30)
    return s


def attn_fwd(q, k, v, H, dqk, dv, scale, causal, name, v_idx=lambda h: h):
    B, S, _ = q.shape
    Sk = k.shape[1]
    tq = _tile(S, (256, 128))

    def tile(q_ref, k_ref, v_ref, o_ref, row0, kl):
        s = _scores(q_ref[0], k_ref[0, :kl], scale, causal, row0)
        p = jnp.exp(s - jnp.max(s, axis=-1, keepdims=True))
        l = jnp.sum(p, axis=-1, keepdims=True)
        o = lax.dot_general(p.astype(BF16), v_ref[0, :kl], (((1,), (0,)), ((), ())), preferred_element_type=F32)
        o_ref[0] = (o / l).astype(BF16)

    def body(q_ref, k_ref, v_ref, o_ref):
        if not causal:
            tile(q_ref, k_ref, v_ref, o_ref, 0, Sk)
            return
        for qi in range(S // tq):
            @pl.when(pl.program_id(2) == qi)
            def _(qi=qi):
                tile(q_ref, k_ref, v_ref, o_ref, qi * tq, (qi + 1) * tq)

    return _pcall(body, name=name, grid=(B, H, S // tq),
                  in_specs=[pl.BlockSpec((1, tq, dqk), lambda b, h, i: (b, i, h)),
                            pl.BlockSpec((1, Sk, dqk), lambda b, h, i: (b, 0, h)),
                            pl.BlockSpec((1, Sk, dv), lambda b, h, i: (b, 0, v_idx(h)))],
                  out_specs=pl.BlockSpec((1, tq, dv), lambda b, h, i: (b, i, h)),
                  out_shape=jax.ShapeDtypeStruct((B, S, H * dv), BF16),
                  compiler_params=_params(("parallel", "parallel", "parallel")))(q, k, v)


def attn_bwd(q, k, v, do, H, dqk, dv, scale, causal, name, v_idx=lambda h: h):
    B, S, _ = q.shape
    Sk = k.shape[1]
    tq = _tile(S, (256, 128))

    def body(q_ref, k_ref, v_ref, do_ref, dq_ref, dk_ref, dv_ref):
        i = pl.program_id(2)

        @pl.when(i == 0)
        def _():
            dk_ref[...] = jnp.zeros_like(dk_ref)
            dv_ref[...] = jnp.zeros_like(dv_ref)

        def tile(row0, kl):
            qv, kv, vv = q_ref[0], k_ref[0, :kl], v_ref[0, :kl]
            s = _scores(qv, kv, scale, causal, row0)
            p = jnp.exp(s - jnp.max(s, axis=-1, keepdims=True))
            p = p / jnp.sum(p, axis=-1, keepdims=True)
            dob = do_ref[0].astype(BF16)
            pb = p.astype(BF16)
            dv_ref[0, :kl] += lax.dot_general(pb, dob, (((0,), (0,)), ((), ())), preferred_element_type=F32)
            dp = lax.dot_general(dob, vv, (((1,), (1,)), ((), ())), preferred_element_type=F32)
            delta = jnp.sum(p * dp, axis=-1, keepdims=True)
            ds = (p * (dp - delta) * scale).astype(BF16)
            dq_ref[0] = lax.dot_general(ds, kv, (((1,), (0,)), ((), ())), preferred_element_type=F32)
            dk_ref[0, :kl] += lax.dot_general(ds, qv, (((0,), (0,)), ((), ())), preferred_element_type=F32)

        if not causal:
            tile(0, Sk)
            return
        for qi in range(S // tq):
            @pl.when(i == qi)
            def _(qi=qi):
                tile(qi * tq, (qi + 1) * tq)

    qs = pl.BlockSpec((1, tq, dqk), lambda b, h, i: (b, i, h))
    ks = pl.BlockSpec((1, Sk, dqk), lambda b, h, i: (b, 0, h))
    vs = pl.BlockSpec((1, Sk, dv), lambda b, h, i: (b, 0, h))
    vin = pl.BlockSpec((1, Sk, dv), lambda b, h, i: (b, 0, v_idx(h)))
    return _pcall(body, name=name, grid=(B, H, S // tq),
                  in_specs=[qs, ks, vin, pl.BlockSpec((1, tq, dv), lambda b, h, i: (b, i, h))],
                  out_specs=[qs, ks, vs],
                  out_shape=[jax.ShapeDtypeStruct((B, S, H * dqk), F32), jax.ShapeDtypeStruct((B, Sk, H * dqk), F32),
                             jax.ShapeDtypeStruct((B, Sk, H * dv), F32)],
                  compiler_params=_params(("parallel", "parallel", "arbitrary")))(q, k, v, do)


def _shift_down(u, j):
    if j == 0:
        return u
    row = lax.broadcasted_iota(jnp.int32, u.shape, 0)
    return jnp.where(row >= j, pltpu.roll(u, j, 0), 0.0)


def _shift_up(u, j):
    if j == 0:
        return u
    n = u.shape[0]
    row = lax.broadcasted_iota(jnp.int32, u.shape, 0)
    return jnp.where(row < n - j, pltpu.roll(u, n - j, 0), 0.0)


def conv_fwd(u, col0_blk, w8, b, name):
    B, S, _ = u.shape
    tc = 256
    nb = SSD_CONV_DIM // tc

    def body(u_ref, w_ref, b_ref, o_ref):
        uv = u_ref[0]
        pre = b_ref[...] + jnp.zeros_like(uv)
        for kk in range(SSD_CONV):
            pre = pre + w_ref[kk:kk + 1, :] * _shift_down(uv, SSD_CONV - 1 - kk)
        o_ref[0] = pre * _sigmoid(pre)

    return _pcall(body, name=name, grid=(B, nb),
                  in_specs=[pl.BlockSpec((1, S, tc), lambda bb, j: (bb, 0, col0_blk + j)),
                            pl.BlockSpec((8, tc), lambda bb, j: (0, j)), pl.BlockSpec((1, tc), lambda bb, j: (0, j))],
                  out_specs=pl.BlockSpec((1, S, tc), lambda bb, j: (bb, 0, j)),
                  out_shape=jax.ShapeDtypeStruct((B, S, SSD_CONV_DIM), F32),
                  compiler_params=_params(("parallel", "parallel")))(u, w8, b)


def conv_bwd(u, col0_blk, w8, b, dy, name):
    B, S, _ = u.shape
    tc = 256
    nb = SSD_CONV_DIM // tc

    def body(u_ref, w_ref, b_ref, dy_ref, du_ref, dw_ref, db_ref):
        bb = pl.program_id(1)

        @pl.when(bb == 0)
        def _():
            dw_ref[...] = jnp.zeros_like(dw_ref)
            db_ref[...] = jnp.zeros_like(db_ref)

        uv = u_ref[0]
        pre = b_ref[...] + jnp.zeros_like(uv)
        for kk in range(SSD_CONV):
            pre = pre + w_ref[kk:kk + 1, :] * _shift_down(uv, SSD_CONV - 1 - kk)
        s = _sigmoid(pre)
        dpre = dy_ref[0] * (s * (1.0 + pre * (1.0 - s)))
        du = jnp.zeros_like(uv)
        for kk in range(SSD_CONV):
            j = SSD_CONV - 1 - kk
            du = du + w_ref[kk:kk + 1, :] * _shift_up(dpre, j)
            dw_ref[kk:kk + 1, :] += jnp.sum(dpre * _shift_down(uv, j), axis=0, keepdims=True)
        db_ref[...] += jnp.sum(dpre, axis=0, keepdims=True)
        du_ref[0] = du

    return _pcall(body, name=name, grid=(nb, B),
                  in_specs=[pl.BlockSpec((1, S, tc), lambda j, bb: (bb, 0, col0_blk + j)),
                            pl.BlockSpec((8, tc), lambda j, bb: (0, j)), pl.BlockSpec((1, tc), lambda j, bb: (0, j)),
                            pl.BlockSpec((1, S, tc), lambda j, bb: (bb, 0, j))],
                  out_specs=[pl.BlockSpec((1, S, tc), lambda j, bb: (bb, 0, j)),
                             pl.BlockSpec((8, tc), lambda j, bb: (0, j)), pl.BlockSpec((1, tc), lambda j, bb: (0, j))],
                  out_shape=[jax.ShapeDtypeStruct((B, S, SSD_CONV_DIM), F32), jax.ShapeDtypeStruct((8, SSD_CONV_DIM), F32),
                             jax.ShapeDtypeStruct((1, SSD_CONV_DIM), F32)],
                  compiler_params=_params(("arbitrary", "arbitrary")))(u, w8, b, dy)


def _expand_mat():
    e = np.zeros((LANES, SSD_GW), np.float32)
    for h in range(SSD_HPG):
        e[h, h * SSD_HEAD_DIM:(h + 1) * SSD_HEAD_DIM] = 1.0
    return jnp.asarray(e)


def _tri_mat():
    return jnp.asarray(np.tril(np.ones((SSD_CHUNK, SSD_CHUNK), np.float32)))


def _dotf(a, b, dn=(((1,), (0,)), ((), ()))):
    return lax.dot_general(a, b, dn, precision=HI, preferred_element_type=F32)


def _dotb(a, b, dn=(((1,), (0,)), ((), ()))):
    return lax.dot_general(a.astype(BF16), b.astype(BF16), dn, preferred_element_type=F32)


NT = (((1,), (1,)), ((), ()))
TN = (((0,), (0,)), ((), ()))


def _ssd_common(dt_ref, par_ref, tri_ref, e_ref):
    dtr = dt_ref[0]
    xb = dtr + par_ref[0:1, :]
    dt = jnp.maximum(xb, 0.0) + jnp.log(1.0 + jnp.exp(-jnp.abs(xb)))
    A = -jnp.exp(par_ref[1:2, :])
    a = dt * A
    acs = _dotf(tri_ref[...], a)
    alast = acs[SSD_CHUNK - 1:SSD_CHUNK, :]
    E = e_ref[...]
    acsE = _dotf(acs, E)
    alastE = acsE[SSD_CHUNK - 1:SSD_CHUNK, :]
    return dict(xb=xb, dt=dt, A=A, acs=acs, acsT=acs.T, ealast=jnp.exp(alast), w=jnp.exp(alast - acs),
                dtE=_dotf(dt, E), eacsE=jnp.exp(acsE), wE=jnp.exp(alastE - acsE), ealastE=jnp.exp(alastE),
                DE=_dotf(par_ref[...], E)[2:3, :])


def _decay(c, e):
    diff = c["acs"][:, e:e + 1] - c["acsT"][e:e + 1, :]
    li = lax.broadcasted_iota(jnp.int32, diff.shape, 0)
    si = lax.broadcasted_iota(jnp.int32, diff.shape, 1)
    return jnp.where(li >= si, jnp.exp(jnp.minimum(diff, 0.0)), 0.0)


def ssd_fwd(xbc, dtp, par, name):
    B, S, _ = xbc.shape
    nc = S // SSD_CHUNK
    L, N, GW, P = SSD_CHUNK, SSD_STATE, SSD_GW, SSD_HEAD_DIM

    def body(x_ref, b_ref, c_ref, dt_ref, par_ref, tri_ref, e_ref, y_ref, st_ref, S_scr):
        @pl.when(pl.program_id(2) == 0)
        def _():
            S_scr[...] = jnp.zeros_like(S_scr)

        Sin = S_scr[...]
        st_ref[0, 0, 0] = Sin
        c = _ssd_common(dt_ref, par_ref.at[0], tri_ref, e_ref)
        X = x_ref[0]
        Bm = b_ref[0]
        Cm = c_ref[0]
        xdt = X * c["dtE"]
        CB = _dotb(Cm, Bm, NT)
        y_ref[0] = c["eacsE"] * _dotb(Cm, Sin) + c["DE"] * X
        for e in range(SSD_HPG):
            M = CB * _decay(c, e)
            y_ref[0, :, e * P:(e + 1) * P] += _dotb(M, xdt[:, e * P:(e + 1) * P])
        S_scr[...] = c["ealastE"] * Sin + _dotb(Bm, xdt * c["wE"], TN)

    return _pcall(
        body, name=name, grid=(SSD_GROUPS, B, nc),
        in_specs=[pl.BlockSpec((1, L, GW), lambda g, b, ci: (b, ci, g)),
                  pl.BlockSpec((1, L, N), lambda g, b, ci: (b, ci, 8 + g)),
                  pl.BlockSpec((1, L, N), lambda g, b, ci: (b, ci, 10 + g)),
                  pl.BlockSpec((1, L, LANES), lambda g, b, ci: (b, ci, g)),
                  pl.BlockSpec((1, 8, LANES), lambda g, b, ci: (g, 0, 0)),
                  pl.BlockSpec((L, L), lambda g, b, ci: (0, 0)),
                  pl.BlockSpec((LANES, GW), lambda g, b, ci: (0, 0))],
        out_specs=[pl.BlockSpec((1, L, GW), lambda g, b, ci: (b, ci, g)),
                   pl.BlockSpec((1, 1, 1, N, GW), lambda g, b, ci: (g, b, ci, 0, 0))],
        out_shape=[jax.ShapeDtypeStruct((B, S, SSD_INNER), F32), jax.ShapeDtypeStruct((SSD_GROUPS, B, nc, N, GW), F32)],
        scratch_shapes=[pltpu.VMEM((N, GW), F32)],
        compiler_params=_params(("arbitrary", "arbitrary", "arbitrary")))(xbc, xbc, xbc, dtp, par, _tri_mat(), _expand_mat())


def ssd_bwd(xbc, dtp, par, states, dy, name):
    B, S, _ = xbc.shape
    nc = S // SSD_CHUNK
    L, N, GW, P = SSD_CHUNK, SSD_STATE, SSD_GW, SSD_HEAD_DIM

    def body(x_ref, b_ref, c_ref, dt_ref, par_ref, tri_ref, e_ref, st_ref, dy_ref,
             dx_ref, db_ref, dc_ref, ddt_ref, dpar_ref, dS_scr, dxdt_scr):
        @pl.when(pl.program_id(2) == 0)
        def _():
            dS_scr[...] = jnp.zeros_like(dS_scr)

        @pl.when((pl.program_id(1) == 0) & (pl.program_id(2) == 0))
        def _():
            dpar_ref[...] = jnp.zeros_like(dpar_ref)

        c = _ssd_common(dt_ref, par_ref.at[0], tri_ref, e_ref)
        E = e_ref[...]
        red = lambda t: _dotf(t, E, NT)
        red_row = lambda t: red(jnp.broadcast_to(t, (8, GW)))[0:1, :]
        X, Bm, Cm = x_ref[0], b_ref[0], c_ref[0]
        Sin = st_ref[0, 0, 0]
        dY = dy_ref[0]
        dSo = dS_scr[...]
        xdt = X * c["dtE"]
        CB = _dotb(Cm, Bm, NT)
        dD = red_row(jnp.sum(dY * X, axis=0, keepdims=True))
        yoff = c["eacsE"] * _dotb(Cm, Sin)
        dacs = red(dY * yoff)
        dYe = dY * c["eacsE"]
        dC = _dotb(dYe, Sin, NT)
        dSin = _dotb(Cm, dYe, TN) + c["ealastE"] * dSo
        dB = _dotb(xdt * c["wE"], dSo, NT)
        dxw = _dotb(Bm, dSo)
        t = red(dxw * xdt) * c["w"]
        dacs = dacs - t
        dalast = jnp.sum(t, axis=0, keepdims=True) + c["ealast"] * red_row(jnp.sum(Sin * dSo, axis=0, keepdims=True))
        dxdt_scr[...] = dxw * c["wE"]
        dCB = jnp.zeros((L, L), F32)
        dacsT = jnp.zeros((LANES, L), F32)
        lane = lax.broadcasted_iota(jnp.int32, (1, LANES), 1)
        subl = lax.broadcasted_iota(jnp.int32, (LANES, 1), 0)
        for e in range(SSD_HPG):
            Lm = _decay(c, e)
            M = CB * Lm
            dYh = dY[:, e * P:(e + 1) * P]
            dM = _dotb(dYh, xdt[:, e * P:(e + 1) * P], NT)
            dxdt_scr[:, e * P:(e + 1) * P] += _dotb(M, dYh, TN)
            dCBe = dM * Lm
            dCB = dCB + dCBe
            Gm = dCBe * CB
            dacs = dacs + jnp.sum(Gm, axis=1, keepdims=True) * (lane == e).astype(F32)
            dacsT = dacsT - jnp.sum(Gm, axis=0, keepdims=True) * (subl == e).astype(F32)
        dC = dC + _dotb(dCB, Bm)
        dB = dB + _dotb(dCB, Cm, TN)
        dacs = dacs + dacsT.T
        row = lax.broadcasted_iota(jnp.int32, (L, 1), 0)
        dacs = dacs + jnp.where(row == L - 1, dalast, 0.0)
        da = _dotf(tri_ref[...], dacs, TN)
        dxdt = dxdt_scr[...]
        ddt = da * c["A"] + red(dxdt * X)
        dA = jnp.sum(da * c["dt"], axis=0, keepdims=True)
        ddtr = ddt * _sigmoid(c["xb"])
        dx_ref[0] = c["DE"] * dY + dxdt * c["dtE"]
        db_ref[0] = dB
        dc_ref[0] = dC
        ddt_ref[0] = ddtr
        dpar_ref[0, 0:1, :] += jnp.sum(ddtr, axis=0, keepdims=True)
        dpar_ref[0, 1:2, :] += dA * c["A"]
        dpar_ref[0, 2:3, :] += dD
        dS_scr[...] = dSin

    rc = lambda ci: nc - 1 - ci
    xspec = pl.BlockSpec((1, L, GW), lambda g, b, ci: (b, rc(ci), g))
    return _pcall(
        body, name=name, grid=(SSD_GROUPS, B, nc),
        in_specs=[xspec,
                  pl.BlockSpec((1, L, N), lambda g, b, ci: (b, rc(ci), 8 + g)),
                  pl.BlockSpec((1, L, N), lambda g, b, ci: (b, rc(ci), 10 + g)),
                  pl.BlockSpec((1, L, LANES), lambda g, b, ci: (b, rc(ci), g)),
                  pl.BlockSpec((1, 8, LANES), lambda g, b, ci: (g, 0, 0)),
                  pl.BlockSpec((L, L), lambda g, b, ci: (0, 0)),
                  pl.BlockSpec((LANES, GW), lambda g, b, ci: (0, 0)),
                  pl.BlockSpec((1, 1, 1, N, GW), lambda g, b, ci: (g, b, rc(ci), 0, 0)),
                  xspec],
        out_specs=[xspec,
                   pl.BlockSpec((1, L, N), lambda g, b, ci: (b, rc(ci), g)),
                   pl.BlockSpec((1, L, N), lambda g, b, ci: (b, rc(ci), g)),
                   pl.BlockSpec((1, L, LANES), lambda g, b, ci: (b, rc(ci), g)),
                   pl.BlockSpec((1, 8, LANES), lambda g, b, ci: (g, 0, 0))],
        out_shape=[jax.ShapeDtypeStruct((B, S, SSD_INNER), F32), jax.ShapeDtypeStruct((B, S, SSD_GROUPS * N), F32),
                   jax.ShapeDtypeStruct((B, S, SSD_GROUPS * N), F32), jax.ShapeDtypeStruct((B, S, SSD_GROUPS * LANES), F32),
                   jax.ShapeDtypeStruct((SSD_GROUPS, 8, LANES), F32)],
        scratch_shapes=[pltpu.VMEM((N, GW), F32), pltpu.VMEM((L, GW), F32)],
        compiler_params=_params(("arbitrary", "arbitrary", "arbitrary")))(
            xbc, xbc, xbc, dtp, par, _tri_mat(), _expand_mat(), states, dy)


def swiglu_fwd(g, u, name):
    T = g.shape[0]
    tq, tc = _tile(T, (512, 256, 128)), 512
    nf = FFN // tc

    def body(g_ref, u_ref, o_ref):
        gv = g_ref[...].astype(F32)
        o_ref[...] = (gv * _sigmoid(gv) * u_ref[...].astype(F32)).astype(BF16)

    lo = pl.BlockSpec((tq, tc), lambda i, j: (i, j))
    return _pcall(body, name=name, grid=(T // tq, nf), in_specs=[lo, lo], out_specs=lo,
                  out_shape=jax.ShapeDtypeStruct((T, FFN), BF16), compiler_params=_params(("parallel", "parallel")))(g, u)


def swiglu_bwd(gate, up, dact, name):
    T = gate.shape[0]
    tq, tc = _tile(T, (512, 256, 128)), 512
    nf = FFN // tc

    def body(g_ref, u_ref, d_ref, dg_ref, du_ref):
        g = g_ref[...].astype(F32)
        u = u_ref[...].astype(F32)
        d = d_ref[...]
        s = _sigmoid(g)
        dg_ref[...] = (d * u * (s * (1.0 + g * (1.0 - s)))).astype(BF16)
        du_ref[...] = (d * g * s).astype(BF16)

    lo = pl.BlockSpec((tq, tc), lambda i, j: (i, j))
    return _pcall(body, name=name, grid=(T // tq, nf), in_specs=[lo, lo, lo], out_specs=[lo, lo],
                  out_shape=[jax.ShapeDtypeStruct((T, FFN), BF16)] * 2,
                  compiler_params=_params(("parallel", "parallel")))(gate, up, dact)


def loss_head(y, target):
    T, D = y.shape
    tq = _tile(T, (256, 128))

    def body(y_ref, t_ref, dy_ref, l_ref):
        @pl.when(pl.program_id(0) == 0)
        def _():
            l_ref[...] = jnp.zeros_like(l_ref)

        e = y_ref[...] - t_ref[...]
        dy_ref[...] = e * (1.0 / D)
        l_ref[...] += 0.5 * jnp.sum(jnp.mean(e * e, axis=-1, keepdims=True))

    xs = pl.BlockSpec((tq, D), lambda i: (i, 0))
    return _pcall(body, name="loss_head", grid=(T // tq,), in_specs=[xs, xs],
                  out_specs=[xs, pl.BlockSpec((8, LANES), lambda i: (0, 0))],
                  out_shape=[jax.ShapeDtypeStruct((T, D), F32), jax.ShapeDtypeStruct((8, LANES), F32)],
                  compiler_params=_params(("arbitrary",)))(y, target)


def sum_arrays(arrs, out_dtype, name):
    R, C = arrs[0].shape
    tr = _tile(R, (512, 256, 128, 64, 32, 16, 8))
    n = len(arrs)

    def body(*refs):
        acc = refs[0][...].astype(F32)
        for r in refs[1:n]:
            acc = acc + r[...].astype(F32)
        refs[n][...] = acc.astype(out_dtype)

    xs = pl.BlockSpec((tr, C), lambda i: (i, 0))
    return _pcall(body, name=name, grid=(R // tr,), in_specs=[xs] * n, out_specs=xs,
                  out_shape=jax.ShapeDtypeStruct((R, C), out_dtype), compiler_params=_params(("parallel",)))(*arrs)


def adamw(w, g, m, v, name):
    R, C = w.shape
    tr = _tile(R, (256, 128, 64, 32, 16, 8))
    c1 = 1.0 - ADAM_B1 ** ADAM_STEP
    c2 = 1.0 - ADAM_B2 ** ADAM_STEP

    def body(w_ref, g_ref, m_ref, v_ref, d_ref, nm_ref, nv_ref):
        gv = g_ref[...]
        mn = ADAM_B1 * m_ref[...] + (1.0 - ADAM_B1) * gv
        vn = ADAM_B2 * v_ref[...] + (1.0 - ADAM_B2) * (gv * gv)
        nm_ref[...] = mn
        nv_ref[...] = vn
        d_ref[...] = -ADAM_LR * ((mn / c1) / (jnp.sqrt(vn / c2) + ADAM_EPS) + ADAM_WD * w_ref[...])

    xs = pl.BlockSpec((tr, C), lambda i: (i, 0))
    return _pcall(body, name=name, grid=(R // tr,), in_specs=[xs] * 4, out_specs=[xs] * 3,
                  out_shape=[jax.ShapeDtypeStruct((R, C), F32)] * 3, compiler_params=_params(("parallel",)))(w, g, m, v)


def adamw_layer(w, g, m, v, l, prev, name):
    L, r, c = w.shape
    tr = _tile(r, (256, 128, 64, 32, 16, 8))
    c1 = 1.0 - ADAM_B1 ** ADAM_STEP
    c2 = 1.0 - ADAM_B2 ** ADAM_STEP
    chained = prev is not None

    def body(w_ref, g_ref, m_ref, v_ref, *rest):
        go_ref, d_ref, nm_ref, nv_ref = rest[-4:]
        gv = g_ref[...]
        mn = ADAM_B1 * m_ref[...] + (1.0 - ADAM_B1) * gv
        vn = ADAM_B2 * v_ref[...] + (1.0 - ADAM_B2) * (gv * gv)
        go_ref[...] = gv
        nm_ref[...] = mn
        nv_ref[...] = vn
        d_ref[...] = -ADAM_LR * ((mn / c1) / (jnp.sqrt(vn / c2) + ADAM_EPS) + ADAM_WD * w_ref[...])

    ls = pl.BlockSpec((None, tr, c), lambda i: (l, i, 0))
    gs = pl.BlockSpec((tr, c), lambda i: (i, 0))
    anys = [pl.BlockSpec(memory_space=pl.ANY)] * 4 if chained else []
    return _pcall(body, name=name, grid=(r // tr,), in_specs=[ls, gs, ls, ls] + anys, out_specs=[ls] * 4,
                  out_shape=[jax.ShapeDtypeStruct((L, r, c), F32)] * 4,
                  input_output_aliases={4: 0, 5: 1, 6: 2, 7: 3} if chained else {},
                  compiler_params=_params(("parallel",)))(w, g, m, v, *(prev if chained else ()))


def add_pair(g, got, name):
    _, r, c = g.shape
    h = r // 2

    def body(c_ref, g_ref, o_ref, out_ref):
        del c_ref
        out_ref[...] = (g_ref[...].astype(F32) + o_ref[...].astype(F32)).astype(BF16)

    spec = pltpu.PrefetchScalarGridSpec(
        num_scalar_prefetch=1, grid=(4,),
        in_specs=[pl.BlockSpec((None, h, c), lambda k, cs: (k, cs[0], 0)), pl.BlockSpec((None, h, c), lambda k, cs: (k, 0, 0))],
        out_specs=pl.BlockSpec((None, h, c), lambda k, cs: (k, 0, 0)))
    return _pcall(body, name=name, grid_spec=spec, out_shape=jax.ShapeDtypeStruct((4, h, c), BF16),
                  compiler_params=_params(("arbitrary",)))(lax.axis_index("c").reshape(1).astype(jnp.int32), g, got)


def add_chips(p, b, name):
    _, h, c = p.shape
    tr = _tile(h, (256, 128, 64, 32, 16, 8))

    def body(me_ref, p_ref, b0, b1, b2, b3, out_ref):
        me = me_ref[0]
        acc = jnp.zeros(out_ref.shape, F32)
        for k, bk in enumerate((b0, b1, b2, b3)):
            acc = acc + jnp.where(me == k, p_ref[...], bk[...]).astype(F32)
        out_ref[...] = acc

    slot = lambda k: pl.BlockSpec((None, tr, c), lambda i, ms: (jnp.where(ms[0] == k, (k + 1) % 4, k), i, 0))
    spec = pltpu.PrefetchScalarGridSpec(
        num_scalar_prefetch=1, grid=(h // tr,),
        in_specs=[pl.BlockSpec((None, tr, c), lambda i, ms: (ms[0], i, 0)), slot(0), slot(1), slot(2), slot(3)],
        out_specs=pl.BlockSpec((tr, c), lambda i, ms: (i, 0)))
    me = (2 * lax.axis_index("x") + lax.axis_index("y")).reshape(1).astype(jnp.int32)
    return _pcall(body, name=name, grid_spec=spec, out_shape=jax.ShapeDtypeStruct((h, c), F32),
                  compiler_params=_params(("arbitrary",)))(me, p, b, b, b, b)


ANY = pl.BlockSpec(memory_space=pl.ANY)


def _place():
    x, y, c = lax.axis_index("x"), lax.axis_index("y"), lax.axis_index("c")
    chips = [(1 - x, y), (x, 1 - y), (1 - x, 1 - y)]
    return x, y, c, chips


def _rcopy(src, dst, ssem, rsem, dev):
    return pltpu.make_async_remote_copy(src_ref=src, dst_ref=dst, send_sem=ssem, recv_sem=rsem, device_id=dev,
                                        device_id_type=MESH)


def _half(c, h):
    return pl.ds(pl.multiple_of(c * h, 64), h)


def _me_chip():
    return 2 * lax.axis_index("x") + lax.axis_index("y")


def all_gather_weights(shards, name):
    n = len(shards)
    L = shards[0].shape[0]
    hl = L // 2

    def body(*refs):
        srcs, outs, (ssem, rsem) = refs[:n], refs[n:2 * n], refs[2 * n:]
        x, y, c, chips = _place()
        me = 2 * x + y
        sib = (x, y, 1 - c)
        mine, other = pl.ds(c * hl, hl), pl.ds((1 - c) * hl, hl)
        sends = []
        for w in range(n):
            for j, (px, py) in enumerate(chips):
                cp = _rcopy(srcs[w].at[mine], outs[w].at[mine, me], ssem.at[w, j], rsem.at[w, j], (px, py, c))
                cp.start()
                sends.append(cp)
        for j, (px, py) in enumerate(chips):
            slot = 2 * px + py
            for w in range(n):
                _rcopy(srcs[w].at[mine], outs[w].at[mine, slot], ssem.at[w, j], rsem.at[w, j], (px, py, c)).wait_recv()
                cp = _rcopy(outs[w].at[mine, slot], outs[w].at[mine, slot], ssem.at[w, 3 + j], rsem.at[w, 3 + j], sib)
                cp.start()
                sends.append(cp)
        for j, (px, py) in enumerate(chips):
            slot = 2 * px + py
            for w in range(n):
                _rcopy(outs[w].at[other, slot], outs[w].at[other, slot], ssem.at[w, 3 + j], rsem.at[w, 3 + j], sib).wait_recv()
        for cp in sends:
            cp.wait_send()

    outs = _pcall(body, name=name, in_specs=[ANY] * n, out_specs=[ANY] * n,
                  out_shape=[jax.ShapeDtypeStruct((L, 4) + s.shape[1:], s.dtype) for s in shards],
                  scratch_shapes=[pltpu.SemaphoreType.DMA((n, 6)), pltpu.SemaphoreType.DMA((n, 6))])(*shards)
    me = _me_chip()
    return [lax.dynamic_update_slice(o, s[:, None], (0, me, 0, 0)) for o, s in zip(outs, shards)]


def exchange_pair(gs, name):
    n = len(gs)

    def body(*refs):
        srcs, outs, (ssem, rsem) = refs[:n], refs[n:2 * n], refs[2 * n:]
        x, y, c, _ = _place()
        cps = []
        for w in range(n):
            h = srcs[w].shape[1] // 2
            cp = _rcopy(srcs[w].at[:, _half(1 - c, h)], outs[w], ssem.at[w], rsem.at[w], (x, y, 1 - c))
            cp.start()
            cps.append(cp)
        for cp in cps:
            cp.wait()

    return _pcall(body, name=name, in_specs=[ANY] * n, out_specs=[ANY] * n,
                  out_shape=[jax.ShapeDtypeStruct((4, g.shape[1] // 2, g.shape[2]), g.dtype) for g in gs],
                  scratch_shapes=[pltpu.SemaphoreType.DMA((n,)), pltpu.SemaphoreType.DMA((n,))])(*gs)


def scatter_chips(ps, name):
    n = len(ps)

    def body(*refs):
        srcs, outs, (ssem, rsem) = refs[:n], refs[n:2 * n], refs[2 * n:]
        x, y, c, chips = _place()
        me = 2 * x + y
        sends = []
        for w in range(n):
            for j, (px, py) in enumerate(chips):
                cp = _rcopy(srcs[w].at[2 * px + py], outs[w].at[me], ssem.at[w, j], rsem.at[w, j], (px, py, c))
                cp.start()
                sends.append(cp)
        for w in range(n):
            for j, (px, py) in enumerate(chips):
                _rcopy(srcs[w].at[me], outs[w].at[2 * px + py], ssem.at[w, j], rsem.at[w, j], (px, py, c)).wait_recv()
        for cp in sends:
            cp.wait_send()

    return _pcall(body, name=name, in_specs=[ANY] * n, out_specs=[ANY] * n,
                  out_shape=[jax.ShapeDtypeStruct(p.shape, p.dtype) for p in ps],
                  scratch_shapes=[pltpu.SemaphoreType.DMA((n, 3)), pltpu.SemaphoreType.DMA((n, 3))])(*ps)


def join_pair(rs, name):
    n = len(rs)

    def body(*refs):
        srcs, outs, (ssem, rsem) = refs[:n], refs[n:2 * n], refs[2 * n:]
        x, y, c, _ = _place()
        cps = []
        for w in range(n):
            h = srcs[w].shape[0]
            cp = _rcopy(srcs[w], outs[w].at[_half(c, h)], ssem.at[w], rsem.at[w], (x, y, 1 - c))
            cp.start()
            cps.append(cp)
        for w in range(n):
            h = srcs[w].shape[0]
            _rcopy(srcs[w], outs[w].at[_half(1 - c, h)], ssem.at[w], rsem.at[w], (x, y, 1 - c)).wait_recv()
        for cp in cps:
            cp.wait_send()

    outs = _pcall(body, name=name, in_specs=[ANY] * n, out_specs=[ANY] * n,
                  out_shape=[jax.ShapeDtypeStruct((2 * r.shape[0], r.shape[1]), r.dtype) for r in rs],
                  scratch_shapes=[pltpu.SemaphoreType.DMA((n,)), pltpu.SemaphoreType.DMA((n,))])(*rs)
    c = lax.axis_index("c")
    return [lax.dynamic_update_slice(o, r, (c * r.shape[0], 0)) for o, r in zip(outs, rs)]


def gather_all_devices(v, name):
    R, C = v.shape

    def body(src, out, ssem, rsem, lsem):
        x, y, c, _ = _place()
        me = 4 * x + 2 * y + c
        local = pltpu.make_async_copy(src, out.at[me], lsem)
        local.start()
        sends = []
        for j in range(1, 8):
            dx, dy, dc = (j >> 2) & 1, (j >> 1) & 1, j & 1
            px, py, pc = (1 - x if dx else x), (1 - y if dy else y), (1 - c if dc else c)
            cp = _rcopy(src, out.at[me], ssem.at[j - 1], rsem.at[j - 1], (px, py, pc))
            cp.start()
            sends.append(cp)
        for j in range(1, 8):
            dx, dy, dc = (j >> 2) & 1, (j >> 1) & 1, j & 1
            px, py, pc = (1 - x if dx else x), (1 - y if dy else y), (1 - c if dc else c)
            _rcopy(src, out.at[4 * px + 2 * py + pc], ssem.at[j - 1], rsem.at[j - 1], (px, py, pc)).wait_recv()
        for cp in sends:
            cp.wait_send()
        local.wait()

    return _pcall(body, name=name, in_specs=[ANY], out_specs=ANY, out_shape=jax.ShapeDtypeStruct((8, R, C), v.dtype),
                  scratch_shapes=[pltpu.SemaphoreType.DMA((7,)), pltpu.SemaphoreType.DMA((7,)), pltpu.SemaphoreType.DMA(())])(v)


def reduce_scatter_layer(gw, tag):
    names = list(gw)
    gs = [gw[k] for k in names]
    got = exchange_pair(gs, "rs_pair_" + tag)
    ps = [add_pair(g, o, "rs_add_pair_%s_%s" % (k, tag)) for k, g, o in zip(names, gs, got)]
    bs = scatter_chips(ps, "rs_chips_" + tag)
    rs = [add_chips(p, b, "rs_add_chips_%s_%s" % (k, tag)) for k, p, b in zip(names, ps, bs)]
    return dict(zip(names, join_pair(rs, "rs_join_" + tag)))


GATHERED = ("w_in", "w_q_b", "w_kv_b", "w_out", "w_xq", "w_xk", "w_xv", "w_xo", "w_gate", "w_up", "w_down")
KIND = {"w_q_b": "col", "w_kv_b": "col", "w_out": "row", "w_xq": "row", "w_xk": "row", "w_xv": "row", "w_xo": "col",
        "w_gate": "col", "w_up": "col", "w_down": "row"}


def wq_to_storage(w):
    lead = w.shape[:-1]
    w = w.reshape(lead + (2, MLA_QK))
    rope = jnp.pad(w[..., MLA_NOPE:], [(0, 0)] * (w.ndim - 1) + [(0, LANES - MLA_ROPE)])
    return jnp.concatenate([w[..., :MLA_NOPE].reshape(lead + (2 * LANES,)), rope.reshape(lead + (2 * LANES,))], axis=-1)


def wq_from_storage(g):
    lead = g.shape[:-1]
    nope = g[..., :2 * LANES].reshape(lead + (2, LANES))
    rope = g[..., 2 * LANES:].reshape(lead + (2, LANES))[..., :MLA_ROPE]
    return jnp.concatenate([nope, rope], axis=-1).reshape(lead + (2 * MLA_QK,))


def wcat_from_storage(w_in_l):
    w = w_in_l.transpose(1, 0, 2).reshape(D_MODEL, IN_COLS)
    zc = lambda n: jnp.zeros((D_MODEL, n), w.dtype)
    return jnp.concatenate([w[:, :2560], w[:, 2576:], zc(64), w[:, 2560:2576], zc(112)], axis=1)


def w_in_grad_storage(dwcat):
    g = jnp.concatenate([dwcat[:, :2560], dwcat[:, C_DT:C_DT + 16], dwcat[:, 2560:3648]], axis=1)
    return g.reshape(D_MODEL, 4, IN_COLS // 4).transpose(1, 0, 2)


def _ssd_par(sm):
    rows = jnp.stack([sm["dt_bias"], sm["a_log"], sm["d_skip"]]).reshape(3, SSD_GROUPS, SSD_HPG).transpose(1, 0, 2)
    return jnp.pad(rows, ((0, 0), (0, 5), (0, LANES - SSD_HPG)))


def _dt_pad(proj, B, S):
    dt = proj[:, C_DT:C_DT + 16].reshape(B, S, SSD_GROUPS, SSD_HPG)
    return jnp.pad(dt, ((0, 0), (0, 0), (0, 0), (0, LANES - SSD_HPG))).reshape(B, S, SSD_GROUPS * LANES)


def _mla_gains(g):
    return g[:MLA_NOPE].reshape(1, LANES), jnp.pad(g[MLA_NOPE:], (0, LANES - MLA_ROPE)).reshape(1, LANES)


_Q_NOPE = lambda h: 4 * (h // 2) + h % 2
_Q_ROPE = lambda h: 4 * (h // 2) + 2 + h % 2
_K_NOPE = lambda h: 2 * h
_V_BLK = lambda h: 2 * h + 1


def layer_fwd(x, memf, cos, sin, G, kw, sm, B, S, l):
    T = x.shape[0]
    n = lambda s: "%s_l%d" % (s, l)
    wv = lambda k: (KIND[k], l)
    sv = dict(x0=x)
    h1 = norm_fwd(x, sm["attn_norm_g"], D_MODEL, False, n("attn_norm"))
    proj = mm(h1, kw["wcat"], "nn", n("in_proj"))
    sv.update(h1=h1, proj=proj)
    proj3 = proj.reshape(B, S, CAT_COLS)
    xbc = conv_fwd(proj3, C_XBC // 256, kw["conv_w"], sm["conv_b"].reshape(1, -1), n("conv"))
    dtp = _dt_pad(proj, B, S)
    par = _ssd_par(sm)
    y_raw, states = ssd_fwd(xbc, dtp, par, n("ssd"))
    z = proj[:, C_Z:C_Z + SSD_INNER]
    y_ssd = norm_fwd(y_raw.reshape(T, SSD_INNER), sm["ssd_norm_g"], SSD_INNER // SSD_GROUPS, False, n("ssd_norm"), gate=z)
    sv.update(xbc=xbc, states=states, y_raw=y_raw)
    qn = norm_fwd(proj[:, C_QA:C_QA + Q_LORA], sm["q_a_norm_g"], Q_LORA, False, n("q_a_norm"))
    kvn = norm_fwd(proj[:, C_KVA:C_KVA + KV_LORA], sm["kv_a_norm_g"], KV_LORA, False, n("kv_a_norm"))
    q_raw = mm(qn, G["w_q_b"], "nn", n("q_b"), wv=wv("w_q_b"))
    kv_raw = mm(kvn, G["w_kv_b"], "nn", n("kv_b"), wv=wv("w_kv_b"))
    gqn, gqr = _mla_gains(sm["mla_q_norm_g"])
    gkn, gkr = _mla_gains(sm["mla_k_norm_g"])
    qh = qk_fwd(q_raw, _Q_NOPE, q_raw, _Q_ROPE, False, gqn, gqr, cos, sin, n("q_norm_rope"))
    kh = qk_fwd(kv_raw, _K_NOPE, proj, C_KR // LANES, True, gkn, gkr, cos, sin, n("k_norm_rope"))
    vv = kv_raw.astype(BF16)
    y_mla = attn_fwd(qh.reshape(B, S, -1), kh.reshape(B, S, -1), vv.reshape(B, S, -1), MLA_HEADS, 2 * LANES, MLA_V,
                     MLA_QK ** -0.5, True, n("mla_attn"), v_idx=_V_BLK)
    ycat = jnp.concatenate([y_ssd, y_mla.reshape(T, -1)], axis=1)
    x1 = mm(ycat, G["w_out"], "nn", n("out_proj"), add=x, wv=wv("w_out"))
    sv.update(qn=qn, kvn=kvn, q_raw=q_raw, kv_raw=kv_raw, qh=qh, kh=kh, vv=vv, ycat=ycat, x1=x1)
    h2 = norm_fwd(x1, sm["xattn_norm_g"], D_MODEL, False, n("xattn_norm"))
    mn = norm_fwd(memf, sm["mem_norm_g"], D_MODEL, False, n("mem_norm"))
    xq = mm(h2, G["w_xq"], "nn", n("xq"), wv=wv("w_xq"))
    xk = mm(mn, G["w_xk"], "nn", n("xk"), wv=wv("w_xk"))
    xv = mm(mn, G["w_xv"], "nn", n("xv"), wv=wv("w_xv"), out_dtype=BF16)
    xqn = norm_fwd(xq, sm["xq_norm_g"], X_HEAD_DIM, True, n("xq_norm"))
    xkn = norm_fwd(xk, sm["xk_norm_g"], X_HEAD_DIM, True, n("xk_norm"))
    ML = memf.shape[0] // B
    xo = attn_fwd(xqn.reshape(B, S, -1), xkn.reshape(B, ML, -1), xv.reshape(B, ML, -1), X_HEADS, X_HEAD_DIM, X_HEAD_DIM,
                  X_HEAD_DIM ** -0.5, False, n("x_attn"))
    x2 = mm(xo.reshape(T, -1), G["w_xo"], "nn", n("xo_proj"), add=x1, wv=wv("w_xo"))
    sv.update(h2=h2, mn=mn, xq=xq, xk=xk, xqn=xqn, xkn=xkn, xv=xv, xo=xo, x2=x2)
    h3 = norm_fwd(x2, sm["ffn_norm_g"], D_MODEL, False, n("ffn_norm"))
    gate = mm(h3, G["w_gate"], "nn", n("gate"), out_dtype=BF16, wv=wv("w_gate"))
    up = mm(h3, G["w_up"], "nn", n("up"), out_dtype=BF16, wv=wv("w_up"))
    act = swiglu_fwd(gate, up, n("swiglu"))
    x3 = mm(act, G["w_down"], "nn", n("down"), add=x2, wv=wv("w_down"))
    sv.update(h3=h3, gate=gate, up=up, act=act)
    return x3, sv


def layer_bwd(dx3, sv, memf, cos, sin, G, kw, sm, B, S, l):
    T = dx3.shape[0]
    n = lambda s: "%s_bwd_l%d" % (s, l)
    wv = lambda k: (KIND[k], l)
    dw = lambda k, a, b: mm(a, b, "tn", n(k + "_dw"), out_dtype=BF16, out_wv=KIND[k])
    gw, gs = {}, {}
    dact = mm(dx3, G["w_down"], "nt", n("down_dx"), wv=wv("w_down"))
    gw["w_down"] = dw("w_down", sv["act"], dx3)
    dgate, dup = swiglu_bwd(sv["gate"], sv["up"], dact, n("swiglu"))
    dh3 = mm(dgate, G["w_gate"], "nt", n("gate_dx"), wv=wv("w_gate"))
    dh3 = mm(dup, G["w_up"], "nt", n("up_dx"), wv=wv("w_up"), add=dh3)
    gw["w_gate"] = dw("w_gate", sv["h3"], dgate)
    gw["w_up"] = dw("w_up", sv["h3"], dup)
    dx2, gs["ffn_norm_g"] = norm_bwd(sv["x2"], sm["ffn_norm_g"], dh3, D_MODEL, False, n("ffn_norm"), add=dx3)
    dxo = mm(dx2, G["w_xo"], "nt", n("xo_dx"), wv=wv("w_xo"))
    gw["w_xo"] = dw("w_xo", sv["xo"].reshape(T, -1), dx2)
    ML = memf.shape[0] // B
    dxqn, dxkn, dxv = attn_bwd(sv["xqn"].reshape(B, S, -1), sv["xkn"].reshape(B, ML, -1), sv["xv"].reshape(B, ML, -1),
                               dxo.reshape(B, S, -1), X_HEADS, X_HEAD_DIM, X_HEAD_DIM, X_HEAD_DIM ** -0.5, False, n("x_attn"))
    dxq, gs["xq_norm_g"] = norm_bwd(sv["xq"], sm["xq_norm_g"], dxqn.reshape(T, -1), X_HEAD_DIM, True, n("xq_norm"))
    dxk, gs["xk_norm_g"] = norm_bwd(sv["xk"], sm["xk_norm_g"], dxkn.reshape(B * ML, -1), X_HEAD_DIM, True, n("xk_norm"))
    dxv = dxv.reshape(B * ML, -1)
    dmn = mm(dxk, G["w_xk"], "nt", n("xk_dx"), wv=wv("w_xk"))
    dmn = mm(dxv, G["w_xv"], "nt", n("xv_dx"), wv=wv("w_xv"), add=dmn)
    gw["w_xk"] = dw("w_xk", sv["mn"], dxk)
    gw["w_xv"] = dw("w_xv", sv["mn"], dxv)
    _, gs["mem_norm_g"] = norm_bwd(memf, sm["mem_norm_g"], dmn, D_MODEL, False, n("mem_norm"))
    dh2 = mm(dxq, G["w_xq"], "nt", n("xq_dx"), wv=wv("w_xq"))
    gw["w_xq"] = dw("w_xq", sv["h2"], dxq)
    dx1, gs["xattn_norm_g"] = norm_bwd(sv["x1"], sm["xattn_norm_g"], dh2, D_MODEL, False, n("xattn_norm"), add=dx2)
    dycat = mm(dx1, G["w_out"], "nt", n("out_dx"), wv=wv("w_out"))
    gw["w_out"] = dw("w_out", sv["ycat"], dx1)
    dqh, dkh, dvv = attn_bwd(sv["qh"].reshape(B, S, -1), sv["kh"].reshape(B, S, -1), sv["vv"].reshape(B, S, -1),
                             dycat[:, SSD_INNER:].reshape(B, S, -1), MLA_HEADS, 2 * LANES, MLA_V, MLA_QK ** -0.5, True,
                             n("mla_attn"), v_idx=_V_BLK)
    gqn, gqr = _mla_gains(sm["mla_q_norm_g"])
    gkn, gkr = _mla_gains(sm["mla_k_norm_g"])
    dq_raw, dgqn, dgqr = qk_bwd(sv["q_raw"], _Q_NOPE, sv["q_raw"], _Q_ROPE, False, gqn, gqr, cos, sin,
                                dqh.reshape(T, -1), n("q_norm_rope"))
    dkv_raw, dk_rope, dgkn, dgkr = qk_bwd(sv["kv_raw"], _K_NOPE, sv["proj"], C_KR // LANES, True, gkn, gkr, cos, sin,
                                          dkh.reshape(T, -1), n("k_norm_rope"), fill=dvv.reshape(T, -1), fill_blk=_V_BLK)
    gs["mla_q_norm_g"] = jnp.concatenate([dgqn[0], dgqr[0, :MLA_ROPE]])
    gs["mla_k_norm_g"] = jnp.concatenate([dgkn[0], dgkr[0, :MLA_ROPE]])
    dqn = mm(dq_raw, G["w_q_b"], "nt", n("q_b_dx"), wv=wv("w_q_b"))
    gw["w_q_b"] = dw("w_q_b", sv["qn"], dq_raw)
    dkvn = mm(dkv_raw, G["w_kv_b"], "nt", n("kv_b_dx"), wv=wv("w_kv_b"))
    gw["w_kv_b"] = dw("w_kv_b", sv["kvn"], dkv_raw)
    proj = sv["proj"]
    dq_a, gs["q_a_norm_g"] = norm_bwd(proj[:, C_QA:C_QA + Q_LORA], sm["q_a_norm_g"], dqn, Q_LORA, False, n("q_a_norm"))
    dkv_a, gs["kv_a_norm_g"] = norm_bwd(proj[:, C_KVA:C_KVA + KV_LORA], sm["kv_a_norm_g"], dkvn, KV_LORA, False,
                                        n("kv_a_norm"))
    z = proj[:, C_Z:C_Z + SSD_INNER]
    dy_raw, dz, gs["ssd_norm_g"] = norm_bwd(sv["y_raw"].reshape(T, -1), sm["ssd_norm_g"], dycat[:, :SSD_INNER],
                                            SSD_INNER // SSD_GROUPS, False, n("ssd_norm"), gate=z)
    dtp = _dt_pad(proj, B, S)
    par = _ssd_par(sm)
    dxs, dBm, dCm, ddt, dpar = ssd_bwd(sv["xbc"], dtp, par, sv["states"], dy_raw.reshape(B, S, -1), n("ssd"))
    gs["dt_bias"] = dpar[:, 0, :SSD_HPG].reshape(-1)
    gs["a_log"] = dpar[:, 1, :SSD_HPG].reshape(-1)
    gs["d_skip"] = dpar[:, 2, :SSD_HPG].reshape(-1)
    dxbc_act = jnp.concatenate([dxs, dBm, dCm], axis=2)
    dxbc, dconv_w, dconv_b = conv_bwd(proj.reshape(B, S, -1), C_XBC // 256, kw["conv_w"], sm["conv_b"].reshape(1, -1), dxbc_act,
                                      n("conv"))
    gs["conv_w"] = dconv_w[:SSD_CONV]
    gs["conv_b"] = dconv_b[0]
    ddt16 = ddt.reshape(T, SSD_GROUPS, LANES)[:, :, :SSD_HPG].reshape(T, 16)
    dproj = jnp.concatenate([dz, dxbc.reshape(T, -1), dq_a, dkv_a, dk_rope, ddt16, jnp.zeros((T, 112), F32)], axis=1)
    dh1 = mm(dproj, kw["wcat"], "nt", n("in_dx"))
    gw["w_in"] = w_in_grad_storage(mm(sv["h1"], dproj, "tn", n("in_dw"), out_dtype=BF16))
    dx0, gs["attn_norm_g"] = norm_bwd(sv["x0"], sm["attn_norm_g"], dh1, D_MODEL, False, n("attn_norm"), add=dx1)
    gs = {k: v.reshape(-1) for k, v in gs.items()}
    return dx0, gw, gs


def _small_pack(vals):
    flat = jnp.concatenate([vals[l][k].reshape(-1) for l in range(DEPTH) for k in SMALL_RS])
    n = flat.shape[0]
    rows = -(-n // LANES)
    rows = -(-rows // 8) * 8
    return jnp.pad(flat, (0, rows * LANES - n)).reshape(rows, LANES), n


def _small_unpack(flat, shapes):
    v = flat.reshape(-1)
    out, off = [dict() for _ in range(DEPTH)], 0
    for l in range(DEPTH):
        for k in SMALL_RS:
            sz = shapes[k]
            out[l][k] = v[off:off + sz]
            off += sz
    return out


def _adam_2d(a):
    if a.ndim == 2:
        return a
    return a.reshape(a.shape[0] * a.shape[1], a.shape[2])


def kernel(x, mem, positions, attn_norm_g, w_in, conv_w, conv_b, dt_bias, a_log, d_skip, ssd_norm_g, q_a_norm_g, w_q_b, kv_a_norm_g, w_kv_b, mla_q_norm_g, mla_k_norm_g, w_out, xattn_norm_g, mem_norm_g, w_xq, w_xk, w_xv, xq_norm_g, xk_norm_g, w_xo, ffn_norm_g, w_gate, w_up, w_down, loss_target, m_attn_norm_g, m_w_in, m_conv_w, m_conv_b, m_dt_bias, m_a_log, m_d_skip, m_ssd_norm_g, m_q_a_norm_g, m_w_q_b, m_kv_a_norm_g, m_w_kv_b, m_mla_q_norm_g, m_mla_k_norm_g, m_w_out, m_xattn_norm_g, m_mem_norm_g, m_w_xq, m_w_xk, m_w_xv, m_xq_norm_g, m_xk_norm_g, m_w_xo, m_ffn_norm_g, m_w_gate, m_w_up, m_w_down, v_attn_norm_g, v_w_in, v_conv_w, v_conv_b, v_dt_bias, v_a_log, v_d_skip, v_ssd_norm_g, v_q_a_norm_g, v_w_q_b, v_kv_a_norm_g, v_w_kv_b, v_mla_q_norm_g, v_mla_k_norm_g, v_w_out, v_xattn_norm_g, v_mem_norm_g, v_w_xq, v_w_xk, v_w_xv, v_xq_norm_g, v_xk_norm_g, v_w_xo, v_ffn_norm_g, v_w_gate, v_w_up, v_w_down):
    loc = locals()
    W = {k: loc[k] for k in WEIGHTS}
    Mo = {k: loc["m_" + k] for k in WEIGHTS}
    Vo = {k: loc["v_" + k] for k in WEIGHTS}
    B, S, D = x.shape
    T = B * S
    xf = x.reshape(T, D)
    memf = mem.reshape(-1, D)
    tf = loss_target.reshape(T, D)
    cos, sin = rope_tables(positions.reshape(T, 1).astype(F32))

    shards = [(wq_to_storage(W[k]) if k == "w_q_b" else W[k]).astype(BF16) for k in GATHERED]
    G = dict(zip(GATHERED, all_gather_weights(shards, "all_gather_weights")))
    cw = gather_all_devices(conv_w.reshape(DEPTH * SSD_CONV, -1), "gather_conv_w")[0::2]
    cw = cw.transpose(1, 0, 2).reshape(DEPTH, SSD_CONV, SSD_CONV_DIM)
    kws, sms = [], []
    for l in range(DEPTH):
        kws.append(dict(wcat=wcat_from_storage(G["w_in"][l]), conv_w=jnp.pad(cw[l], ((0, 8 - SSD_CONV), (0, 0)))))
        sms.append({k: W[k][l] for k in SMALL})

    saved = []
    h = xf
    for l in range(DEPTH):
        h, sv = layer_fwd(h, memf, cos, sin, G, kws[l], sms[l], B, S, l)
        saved.append(sv)
    dh, lpart = loss_head(h, tf)
    loss = lax.psum(lpart[0, 0], ("x", "y", "c"))

    grads, delta, new_m, new_v = {}, {}, {}, {}
    chain = {k: None for k in GATHERED}
    gsmall = [None] * DEPTH
    for l in reversed(range(DEPTH)):
        dh, gw, gs = layer_bwd(dh, saved[l], memf, cos, sin, G, kws[l], sms[l], B, S, l)
        saved[l] = None
        gsmall[l] = gs
        red = reduce_scatter_layer(gw, "l%d" % l)
        for k in GATHERED:
            g = wq_from_storage(red[k]) if k == "w_q_b" else red[k]
            chain[k] = adamw_layer(W[k], g, Mo[k], Vo[k], l, chain[k], "adamw_%s_l%d" % (k, l))
    for k in GATHERED:
        grads[k], delta[k], new_m[k], new_v[k] = chain[k]
    grad_x = dh.reshape(B, S, D)

    spack, _ = _small_pack(gsmall)
    allv = gather_all_devices(spack, "gather_small_grads")
    ssum = sum_arrays([allv[i] for i in range(8)], F32, "sum_small_grads")
    sizes = {k: int(np.prod(W[k].shape[1:])) for k in SMALL}
    sizes["conv_w"] = SSD_CONV * SSD_CONV_DIM
    gsm = _small_unpack(ssum, sizes)
    for k in SMALL_RS:
        g = jnp.stack([gsm[l][k] for l in range(DEPTH)])
        if k == "conv_w":
            cs = SSD_CONV_DIM // 4
            g = lax.dynamic_slice_in_dim(g.reshape(DEPTH, SSD_CONV, SSD_CONV_DIM), _me_chip() * cs, cs, axis=2)
        shp = W[k].shape
        grads[k] = g.reshape(shp)
        d, nm, nv = adamw(_adam_2d(W[k]), _adam_2d(grads[k]), _adam_2d(Mo[k]), _adam_2d(Vo[k]), "adamw_" + k)
        delta[k], new_m[k], new_v[k] = d.reshape(shp), nm.reshape(shp), nv.reshape(shp)

    return (loss, grad_x, *[grads[k] for k in WEIGHTS], *[delta[k] for k in WEIGHTS], *[new_m[k] for k in WEIGHTS],
            *[new_v[k] for k in WEIGHTS])
```

```python
import functools
import math

import numpy as np
import jax
import jax.numpy as jnp
from jax import lax
from jax.experimental import pallas as pl
from jax.experimental.pallas import tpu as pltpu

F32 = jnp.float32
BF16 = jnp.bfloat16
MESH = pl.DeviceIdType.MESH
HI = lax.Precision.HIGHEST

D_MODEL = 2048
DEPTH = 4
SSD_INNER = 1024
SSD_HEAD_DIM = 64
SSD_GROUPS = 2
SSD_HPG = 8
SSD_STATE = 128
SSD_CONV = 4
SSD_CHUNK = 128
SSD_CONV_DIM = 1536
SSD_GW = SSD_HPG * SSD_HEAD_DIM
MLA_HEADS = 8
MLA_NOPE = 128
MLA_ROPE = 64
MLA_QK = 192
MLA_V = 128
Q_LORA = 512
KV_LORA = 512
ROPE_THETA = 10000.0
X_HEADS = 4
X_HEAD_DIM = 128
X_INNER = 512
FFN = 5632
IN_COLS = 3664
EPS = 1e-6
ADAM_LR, ADAM_B1, ADAM_B2, ADAM_EPS, ADAM_WD, ADAM_STEP = 0.001, 0.9, 0.999, 1e-08, 0.01, 10

C_Z, C_XBC, C_QA, C_KVA, C_KR, C_DT, CAT_COLS = 0, 1024, 2560, 3072, 3584, 3712, 3840

LANES = 128
VMEM_LIMIT = 56 * 1024 * 1024
MM_VMEM_BUDGET = 40 * 1024 * 1024
MM_STEP_BYTES = 1024 * 1024

SMALL = ("attn_norm_g", "conv_b", "dt_bias", "a_log", "d_skip", "ssd_norm_g", "q_a_norm_g", "kv_a_norm_g",
         "mla_q_norm_g", "mla_k_norm_g", "xattn_norm_g", "mem_norm_g", "xq_norm_g", "xk_norm_g", "ffn_norm_g")
SMALL_RS = SMALL + ("conv_w",)
WEIGHTS = ("attn_norm_g", "w_in", "conv_w", "conv_b", "dt_bias", "a_log", "d_skip", "ssd_norm_g", "q_a_norm_g", "w_q_b",
           "kv_a_norm_g", "w_kv_b", "mla_q_norm_g", "mla_k_norm_g", "w_out", "xattn_norm_g", "mem_norm_g", "w_xq", "w_xk",
           "w_xv", "xq_norm_g", "xk_norm_g", "w_xo", "ffn_norm_g", "w_gate", "w_up", "w_down")


def _pcall(body, **kw):
    return pl.pallas_call(body, **kw)


def _params(sem):
    return pltpu.CompilerParams(dimension_semantics=sem, vmem_limit_bytes=VMEM_LIMIT)


def _tile(n, cands):
    for t in cands:
        if n % t == 0:
            return t
    return n


def _sigmoid(z):
    return 1.0 / (1.0 + jnp.exp(-z))


def _mtile(n, cap):
    if n % LANES:
        return n
    q = n // LANES
    t = LANES * max(d for d in range(1, q + 1) if q % d == 0 and LANES * d <= cap)
    return n if (t < 512 and n <= 2048) else t


def _divisor_tiles(n, cap):
    if n % LANES:
        return [n]
    q = n // LANES
    return [LANES * d for d in range(1, q + 1) if q % d == 0 and LANES * d <= cap] or [n]


def _mm_tiles(M, N, K, m_unit, n_unit, k_unit, a_item, b_item, o_item, has_add):
    best = None
    for tm in _divisor_tiles(m_unit, 2048):
        for tn in _divisor_tiles(n_unit, 2048):
            for tk in _divisor_tiles(k_unit, 2048):
                nk = K // tk
                vmem = 2 * (tm * tk * a_item + tk * tn * b_item + tm * tn * (o_item + (4 if has_add else 0)))
                vmem += tm * tn * 4 * (2 if nk > 1 else 1) + (tm * tk * 2 if a_item == 4 else 0) + (tk * tn * 2 if b_item == 4 else 0)
                if vmem > MM_VMEM_BUDGET:
                    continue
                steps = (M // tm) * (N // tn) * nk
                traffic = (M * K * a_item * (N // tn if nk > 1 else 1) + K * N * b_item * (M // tm)
                           + M * N * (o_item + (4 if has_add else 0)) + steps * MM_STEP_BYTES)
                if nk > 1:
                    traffic += steps * tm * tn * 4
                if best is None or traffic < best[0]:
                    best = (traffic, tm, tn, tk)
    assert best is not None, (M, N, K)
    return best[1:]


def mm(a, b, mode, name, add=None, out_dtype=F32, wv=None, out_wv=None):
    kind, l = wv if wv is not None else (None, None)
    if kind == "row":
        b, kind = b[l].reshape(-1, b.shape[-1]), None
    if out_wv == "row":
        res = mm(a, b, mode, name, add=add, out_dtype=out_dtype, wv=wv if kind else None)
        return res.reshape(4, res.shape[0] // 4, res.shape[1])
    if kind == "col":
        _, _, r, c = b.shape
        bshape = (r, 4 * c)
    else:
        bshape = b.shape
    if mode == "nn":
        (M, K), (K2, N) = a.shape, bshape
    elif mode == "nt":
        (M, K), (N, K2) = a.shape, bshape
    else:
        (K, M), (K2, N) = a.shape, bshape
    assert K == K2, (a.shape, bshape, mode)
    m_unit = M
    n_unit = N // 4 if (out_wv == "col" or (kind == "col" and mode == "nn")) else N
    k_unit = K // 4 if (kind == "col" and mode == "nt") else K
    tm, tn, tk = _mm_tiles(M, N, K, m_unit, n_unit, k_unit, a.dtype.itemsize, b.dtype.itemsize,
                           jnp.dtype(out_dtype).itemsize, add is not None)
    nk = K // tk
    qm, qn, qk = m_unit // tm, n_unit // tn, k_unit // tk
    dn = {"nn": (((1,), (0,)), ((), ())), "nt": (((1,), (1,)), ((), ())), "tn": (((0,), (0,)), ((), ()))}[mode]
    has_add = add is not None

    def body(a_ref, b_ref, *rest):
        rest = list(rest)
        c_ref = rest.pop(0) if has_add else None
        o_ref = rest.pop(0)
        d = lax.dot_general(a_ref[...].astype(BF16), b_ref[...].astype(BF16), dn, preferred_element_type=F32)

        def finish(res):
            if has_add:
                res = res + c_ref[...].astype(F32)
            o_ref[...] = res.astype(out_dtype)

        if nk == 1:
            finish(d)
            return
        acc = rest.pop(0)
        k = pl.program_id(2)

        @pl.when(k == 0)
        def _():
            acc[...] = d

        if nk > 2:
            @pl.when((k > 0) & (k < nk - 1))
            def _():
                acc[...] += d

        @pl.when(k == nk - 1)
        def _():
            finish(acc[...] + d)

    if mode == "tn":
        a_spec = pl.BlockSpec((tk, tm), lambda i, j, k: (k, i))
    else:
        a_spec = pl.BlockSpec((tm, tk), lambda i, j, k: (i, k))
    if kind is None:
        b_spec = (pl.BlockSpec((tn, tk), lambda i, j, k: (j, k)) if mode == "nt"
                  else pl.BlockSpec((tk, tn), lambda i, j, k: (k, j)))
    elif mode == "nn":
        b_spec = pl.BlockSpec((None, None, tk, tn), lambda i, j, k: (l, j // qn, k, j % qn))
    elif mode == "nt":
        b_spec = pl.BlockSpec((None, None, tn, tk), lambda i, j, k: (l, k // qk, j, k % qk))
    else:
        raise ValueError((mode, kind))
    if out_wv == "col":
        o_spec = pl.BlockSpec((None, tm, tn), lambda i, j, k: (j // qn, i, j % qn))
        out_shape = jax.ShapeDtypeStruct((4, M, N // 4), out_dtype)
    else:
        o_spec = pl.BlockSpec((tm, tn), lambda i, j, k: (i, j))
        out_shape = jax.ShapeDtypeStruct((M, N), out_dtype)
    in_specs = [a_spec, b_spec] + ([o_spec] if has_add else [])
    args = (a, b) + ((add,) if has_add else ())
    return _pcall(
        body, name=name, grid=(M // tm, N // tn, nk), in_specs=in_specs, out_specs=o_spec,
        out_shape=out_shape, scratch_shapes=[pltpu.VMEM((tm, tn), F32)] if nk > 1 else [],
        compiler_params=_params(("parallel", "parallel", "arbitrary")))(*args)


def norm_fwd(x, gain, W, shared, name, gate=None, out_dtype=BF16):
    T, C = x.shape
    nb = C // W
    tq = _tile(T, (256, 128))
    g2 = gain.reshape(1, -1).astype(F32)
    gated = gate is not None

    def body(*refs):
        if gated:
            x_ref, z_ref, g_ref, o_ref = refs
        else:
            x_ref, g_ref, o_ref = refs
        u = x_ref[...].astype(F32)
        if gated:
            z = z_ref[...]
            u = u * (z * _sigmoid(z))
        ms = jnp.mean(u * u, axis=-1, keepdims=True)
        o_ref[...] = (u * lax.rsqrt(ms + EPS) * g_ref[...]).astype(out_dtype)

    xs = pl.BlockSpec((tq, W), lambda j, i: (i, j))
    gs = pl.BlockSpec((1, W), (lambda j, i: (0, 0)) if shared else (lambda j, i: (0, j)))
    in_specs = [xs] + ([xs] if gated else []) + [gs]
    args = (x,) + ((gate,) if gated else ()) + (g2,)
    return _pcall(body, name=name, grid=(nb, T // tq), in_specs=in_specs, out_specs=xs,
                  out_shape=jax.ShapeDtypeStruct((T, C), out_dtype),
                  compiler_params=_params(("parallel", "parallel")))(*args)


def norm_bwd(x, gain, dy, W, shared, name, gate=None, add=None, bf16_copy=False):
    T, C = x.shape
    nb = C // W
    tq = _tile(T, (256, 128))
    g2 = gain.reshape(1, -1).astype(F32)
    gated = gate is not None
    has_add = add is not None
    GC = W if shared else C

    def body(*refs):
        refs = list(refs)
        x_ref = refs.pop(0)
        z_ref = refs.pop(0) if gated else None
        g_ref = refs.pop(0)
        dy_ref = refs.pop(0)
        a_ref = refs.pop(0) if has_add else None
        dx_ref = refs.pop(0)
        dz_ref = refs.pop(0) if gated else None
        dg_ref = refs.pop(0)
        j, i = pl.program_id(0), pl.program_id(1)
        first = (i == 0) & (j == 0) if shared else (i == 0)

        @pl.when(first)
        def _():
            dg_ref[...] = jnp.zeros_like(dg_ref)

        xv = x_ref[...].astype(F32)
        if gated:
            z = z_ref[...]
            s = _sigmoid(z)
            u = xv * (z * s)
        else:
            u = xv
        rstd = lax.rsqrt(jnp.mean(u * u, axis=-1, keepdims=True) + EPS)
        uh = u * rstd
        dyv = dy_ref[...].astype(F32)
        dg_ref[...] += jnp.sum(dyv * uh, axis=0, keepdims=True)
        dyg = dyv * g_ref[...]
        du = rstd * (dyg - uh * jnp.mean(dyg * uh, axis=-1, keepdims=True))
        if gated:
            dx = du * (z * s)
            dz_ref[...] = du * xv * (s * (1.0 + z * (1.0 - s)))
        else:
            dx = du
        if has_add:
            dx = dx + a_ref[...]
        dx_ref[...] = dx
        if bf16_copy:
            refs.pop(0)[...] = dx.astype(BF16)

    xs = pl.BlockSpec((tq, W), lambda j, i: (i, j))
    gs = pl.BlockSpec((1, W), (lambda j, i: (0, 0)) if shared else (lambda j, i: (0, j)))
    in_specs = [xs] + ([xs] if gated else []) + [gs, xs] + ([xs] if has_add else [])
    args = (x,) + ((gate,) if gated else ()) + (g2, dy) + ((add,) if has_add else ())
    out_specs = [xs] + ([xs] if gated else []) + [gs] + ([xs] if bf16_copy else [])
    out_shape = [jax.ShapeDtypeStruct((T, C), F32)] + ([jax.ShapeDtypeStruct((T, C), F32)] if gated else []) \
        + [jax.ShapeDtypeStruct((1, GC), F32)] + ([jax.ShapeDtypeStruct((T, C), BF16)] if bf16_copy else [])
    return _pcall(body, name=name, grid=(nb, T // tq), in_specs=in_specs, out_specs=out_specs, out_shape=out_shape,
                  compiler_params=_params(("arbitrary", "arbitrary")))(*args)


def _inv_freq_row():
    inv = 1.0 / (ROPE_THETA ** (np.arange(0, MLA_ROPE, 2, dtype=np.float32) / MLA_ROPE))
    row = np.zeros((1, LANES), np.float32)
    row[0, :32] = inv
    row[0, 32:64] = inv
    return jnp.asarray(row)


def rope_tables(pos):
    T = pos.shape[0]
    tq = _tile(T, (512, 256, 128))

    def body(p_ref, f_ref, c_ref, s_ref):
        ang = p_ref[...] * f_ref[...]
        lane = lax.broadcasted_iota(jnp.int32, ang.shape, 1)
        c_ref[...] = jnp.where(lane < 64, jnp.cos(ang), 0.0)
        sn = jnp.sin(ang)
        s_ref[...] = jnp.where(lane < 32, -sn, jnp.where(lane < 64, sn, 0.0))

    ts = pl.BlockSpec((tq, LANES), lambda i: (i, 0))
    return _pcall(body, name="rope_tables", grid=(T // tq,),
                  in_specs=[pl.BlockSpec((tq, 1), lambda i: (i, 0)), pl.BlockSpec((1, LANES), lambda i: (0, 0))],
                  out_specs=[ts, ts], out_shape=[jax.ShapeDtypeStruct((T, LANES), F32)] * 2,
                  compiler_params=_params(("parallel",)))(pos, _inv_freq_row())


def _swap_halves(r):
    lane = lax.broadcasted_iota(jnp.int32, r.shape, 1)
    return jnp.where(lane < 32, pltpu.roll(r, 96, 1), pltpu.roll(r, 32, 1))


def qk_fwd(nope_arr, nope_blk0, rope_arr, rope_blk0, rope_shared, gn, gr, cos, sin, name):
    T = nope_arr.shape[0]
    tq = _tile(T, (256, 128))

    col = lambda blk: slice(blk * LANES, (blk + 1) * LANES)

    def body(n_ref, r_ref, gn_ref, gr_ref, c_ref, s_ref, o_ref):
        for h in range(MLA_HEADS):
            n = n_ref[:, col(nope_blk0(h))]
            r = r_ref[...] if rope_shared else r_ref[:, col(rope_blk0(h))]
            ms = (jnp.sum(n * n, axis=-1, keepdims=True) + jnp.sum(r * r, axis=-1, keepdims=True)) * (1.0 / MLA_QK)
            rstd = lax.rsqrt(ms + EPS)
            rn = r * rstd * gr_ref[...]
            rr = rn * c_ref[...] + _swap_halves(rn) * s_ref[...]
            o_ref[:, col(2 * h)] = (n * rstd * gn_ref[...]).astype(BF16)
            o_ref[:, col(2 * h + 1)] = rr.astype(BF16)

    wide = lambda arr: pl.BlockSpec((tq, arr.shape[1]), lambda i: (i, 0))
    rspec = pl.BlockSpec((tq, LANES), lambda i: (i, rope_blk0)) if rope_shared else wide(rope_arr)
    row = pl.BlockSpec((1, LANES), lambda i: (0, 0))
    tab = pl.BlockSpec((tq, LANES), lambda i: (i, 0))
    return _pcall(body, name=name, grid=(T // tq,),
                  in_specs=[wide(nope_arr), rspec, row, row, tab, tab],
                  out_specs=pl.BlockSpec((tq, MLA_HEADS * 2 * LANES), lambda i: (i, 0)),
                  out_shape=jax.ShapeDtypeStruct((T, MLA_HEADS * 2 * LANES), BF16),
                  compiler_params=_params(("parallel",)))(nope_arr, rope_arr, gn, gr, cos, sin)


def qk_bwd(nope_arr, nope_blk0, rope_arr, rope_blk0, rope_shared, gn, gr, cos, sin, dout, name, fill=None, fill_blk=None):
    T = nope_arr.shape[0]
    tq = _tile(T, (256, 128))
    col = lambda blk: slice(blk * LANES, (blk + 1) * LANES)

    def body(*refs):
        refs = list(refs)
        n_ref, r_ref, gn_ref, gr_ref, c_ref, s_ref, d_ref = refs[:7]
        f_ref = refs[7] if rope_shared else None
        outs = refs[8:] if rope_shared else refs[7:]
        draw_ref = outs[0]
        dsh_ref = outs[1] if rope_shared else None
        dgn_ref, dgr_ref = outs[-2:]

        @pl.when(pl.program_id(0) == 0)
        def _():
            dgn_ref[...] = jnp.zeros_like(dgn_ref)
            dgr_ref[...] = jnp.zeros_like(dgr_ref)

        dgn = jnp.zeros((1, LANES), F32)
        dgr = jnp.zeros((1, LANES), F32)
        dshared = jnp.zeros((tq, LANES), F32)
        for h in range(MLA_HEADS):
            n = n_ref[:, col(nope_blk0(h))]
            r = r_ref[...] if rope_shared else r_ref[:, col(rope_blk0(h))]
            ms = (jnp.sum(n * n, axis=-1, keepdims=True) + jnp.sum(r * r, axis=-1, keepdims=True)) * (1.0 / MLA_QK)
            rstd = lax.rsqrt(ms + EPS)
            nh = n * rstd
            rh = r * rstd
            d_n = d_ref[:, col(2 * h)].astype(F32)
            d_rr = d_ref[:, col(2 * h + 1)].astype(F32)
            d_rn = d_rr * c_ref[...] - _swap_halves(d_rr) * s_ref[...]
            dgn = dgn + jnp.sum(d_n * nh, axis=0, keepdims=True)
            dgr = dgr + jnp.sum(d_rn * rh, axis=0, keepdims=True)
            dng = d_n * gn_ref[...]
            drg = d_rn * gr_ref[...]
            mean = (jnp.sum(dng * nh, axis=-1, keepdims=True) + jnp.sum(drg * rh, axis=-1, keepdims=True)) * (1.0 / MLA_QK)
            draw_ref[:, col(nope_blk0(h))] = (rstd * (dng - nh * mean)).astype(BF16)
            drope = rstd * (drg - rh * mean)
            if rope_shared:
                dshared = dshared + drope
                draw_ref[:, col(fill_blk(h))] = f_ref[:, col(h)].astype(BF16)
            else:
                draw_ref[:, col(rope_blk0(h))] = drope.astype(BF16)
        dgn_ref[...] += dgn
        dgr_ref[...] += dgr
        if rope_shared:
            dsh_ref[...] = dshared

    wide = lambda arr: pl.BlockSpec((tq, arr.shape[1]), lambda i: (i, 0))
    rspec = pl.BlockSpec((tq, LANES), lambda i: (i, rope_blk0)) if rope_shared else wide(rope_arr)
    row = pl.BlockSpec((1, LANES), lambda i: (0, 0))
    tab = pl.BlockSpec((tq, LANES), lambda i: (i, 0))
    raw = pl.BlockSpec((tq, 2 * MLA_HEADS * LANES), lambda i: (i, 0))
    in_specs = [wide(nope_arr), rspec, row, row, tab, tab, wide(dout)] + ([wide(fill)] if rope_shared else [])
    out_specs = [raw] + ([tab] if rope_shared else []) + [row, row]
    out_shape = ([jax.ShapeDtypeStruct((T, 2 * MLA_HEADS * LANES), BF16)]
                 + ([jax.ShapeDtypeStruct((T, LANES), F32)] if rope_shared else [])
                 + [jax.ShapeDtypeStruct((1, LANES), F32)] * 2)
    args = (nope_arr, rope_arr, gn, gr, cos, sin, dout) + ((fill,) if rope_shared else ())
    return _pcall(body, name=name, grid=(T // tq,), in_specs=in_specs, out_specs=out_specs, out_shape=out_shape,
                  compiler_params=_params(("arbitrary",)))(*args)


def _scores(q, k, scale, causal, row0):
    s = lax.dot_general(q, k, (((1,), (1,)), ((), ())), preferred_element_type=F32) * scale
    if causal:
        qi = row0 + lax.broadcasted_iota(jnp.int32, s.shape, 0)
        ki = lax.broadcasted_iota(jnp.int32, s.shape, 1)
        s = jnp.where(qi >= ki, s, -1e30)
    return s


def attn_fwd(q, k, v, H, dqk, dv, scale, causal, name, v_idx=lambda h: h):
    B, S, _ = q.shape
    Sk = k.shape[1]
    tq = _tile(S, (256, 128))

    def tile(q_ref, k_ref, v_ref, o_ref, row0, kl):
        s = _scores(q_ref[0], k_ref[0, :kl], scale, causal, row0)
        p = jnp.exp(s - jnp.max(s, axis=-1, keepdims=True))
        l = jnp.sum(p, axis=-1, keepdims=True)
        o = lax.dot_general(p.astype(BF16), v_ref[0, :kl], (((1,), (0,)), ((), ())), preferred_element_type=F32)
        o_ref[0] = (o / l).astype(BF16)

    def body(q_ref, k_ref, v_ref, o_ref):
        if not causal:
            tile(q_ref, k_ref, v_ref, o_ref, 0, Sk)
            return
        for qi in range(S // tq):
            @pl.when(pl.program_id(2) == qi)
            def _(qi=qi):
                tile(q_ref, k_ref, v_ref, o_ref, qi * tq, (qi + 1) * tq)

    return _pcall(body, name=name, grid=(B, H, S // tq),
                  in_specs=[pl.BlockSpec((1, tq, dqk), lambda b, h, i: (b, i, h)),
                            pl.BlockSpec((1, Sk, dqk), lambda b, h, i: (b, 0, h)),
                            pl.BlockSpec((1, Sk, dv), lambda b, h, i: (b, 0, v_idx(h)))],
                  out_specs=pl.BlockSpec((1, tq, dv), lambda b, h, i: (b, i, h)),
                  out_shape=jax.ShapeDtypeStruct((B, S, H * dv), BF16),
                  compiler_params=_params(("parallel", "parallel", "parallel")))(q, k, v)


def attn_bwd(q, k, v, do, H, dqk, dv, scale, causal, name, v_idx=lambda h: h):
    B, S, _ = q.shape
    Sk = k.shape[1]
    tq = _tile(S, (256, 128))

    def body(q_ref, k_ref, v_ref, do_ref, dq_ref, dk_ref, dv_ref):
        i = pl.program_id(2)

        @pl.when(i == 0)
        def _():
            dk_ref[...] = jnp.zeros_like(dk_ref)
            dv_ref[...] = jnp.zeros_like(dv_ref)

        def tile(row0, kl):
            qv, kv, vv = q_ref[0], k_ref[0, :kl], v_ref[0, :kl]
            s = _scores(qv, kv, scale, causal, row0)
            p = jnp.exp(s - jnp.max(s, axis=-1, keepdims=True))
            p = p / jnp.sum(p, axis=-1, keepdims=True)
            dob = do_ref[0].astype(BF16)
            pb = p.astype(BF16)
            dv_ref[0, :kl] += lax.dot_general(pb, dob, (((0,), (0,)), ((), ())), preferred_element_type=F32)
            dp = lax.dot_general(dob, vv, (((1,), (1,)), ((), ())), preferred_element_type=F32)
            delta = jnp.sum(p * dp, axis=-1, keepdims=True)
            ds = (p * (dp - delta) * scale).astype(BF16)
            dq_ref[0] = lax.dot_general(ds, kv, (((1,), (0,)), ((), ())), preferred_element_type=F32)
            dk_ref[0, :kl] += lax.dot_general(ds, qv, (((0,), (0,)), ((), ())), preferred_element_type=F32)

        if not causal:
            tile(0, Sk)
            return
        for qi in range(S // tq):
            @pl.when(i == qi)
            def _(qi=qi):
                tile(qi * tq, (qi + 1) * tq)

    qs = pl.BlockSpec((1, tq, dqk), lambda b, h, i: (b, i, h))
    ks = pl.BlockSpec((1, Sk, dqk), lambda b, h, i: (b, 0, h))
    vs = pl.BlockSpec((1, Sk, dv), lambda b, h, i: (b, 0, h))
    vin = pl.BlockSpec((1, Sk, dv), lambda b, h, i: (b, 0, v_idx(h)))
    return _pcall(body, name=name, grid=(B, H, S // tq),
                  in_specs=[qs, ks, vin, pl.BlockSpec((1, tq, dv), lambda b, h, i: (b, i, h))],
                  out_specs=[qs, ks, vs],
                  out_shape=[jax.ShapeDtypeStruct((B, S, H * dqk), F32), jax.ShapeDtypeStruct((B, Sk, H * dqk), F32),
                             jax.ShapeDtypeStruct((B, Sk, H * dv), F32)],
                  compiler_params=_params(("parallel", "parallel", "arbitrary")))(q, k, v, do)


def _shift_down(u, j):
    if j == 0:
        return u
    row = lax.broadcasted_iota(jnp.int32, u.shape, 0)
    return jnp.where(row >= j, pltpu.roll(u, j, 0), 0.0)


def _shift_up(u, j):
    if j == 0:
        return u
    n = u.shape[0]
    row = lax.broadcasted_iota(jnp.int32, u.shape, 0)
    return jnp.where(row < n - j, pltpu.roll(u, n - j, 0), 0.0)


def conv_fwd(u, col0_blk, w8, b, name):
    B, S, _ = u.shape
    tc = 256
    nb = SSD_CONV_DIM // tc

    def body(u_ref, w_ref, b_ref, o_ref):
        uv = u_ref[0]
        pre = b_ref[...] + jnp.zeros_like(uv)
        for kk in range(SSD_CONV):
            pre = pre + w_ref[kk:kk + 1, :] * _shift_down(uv, SSD_CONV - 1 - kk)
        o_ref[0] = pre * _sigmoid(pre)

    return _pcall(body, name=name, grid=(B, nb),
                  in_specs=[pl.BlockSpec((1, S, tc), lambda bb, j: (bb, 0, col0_blk + j)),
                            pl.BlockSpec((8, tc), lambda bb, j: (0, j)), pl.BlockSpec((1, tc), lambda bb, j: (0, j))],
                  out_specs=pl.BlockSpec((1, S, tc), lambda bb, j: (bb, 0, j)),
                  out_shape=jax.ShapeDtypeStruct((B, S, SSD_CONV_DIM), F32),
                  compiler_params=_params(("parallel", "parallel")))(u, w8, b)


def conv_bwd(u, col0_blk, w8, b, dy, name):
    B, S, _ = u.shape
    tc = 256
    nb = SSD_CONV_DIM // tc

    def body(u_ref, w_ref, b_ref, dy_ref, du_ref, dw_ref, db_ref):
        bb = pl.program_id(1)

        @pl.when(bb == 0)
        def _():
            dw_ref[...] = jnp.zeros_like(dw_ref)
            db_ref[...] = jnp.zeros_like(db_ref)

        uv = u_ref[0]
        pre = b_ref[...] + jnp.zeros_like(uv)
        for kk in range(SSD_CONV):
            pre = pre + w_ref[kk:kk + 1, :] * _shift_down(uv, SSD_CONV - 1 - kk)
        s = _sigmoid(pre)
        dpre = dy_ref[0] * (s * (1.0 + pre * (1.0 - s)))
        du = jnp.zeros_like(uv)
        for kk in range(SSD_CONV):
            j = SSD_CONV - 1 - kk
            du = du + w_ref[kk:kk + 1, :] * _shift_up(dpre, j)
            dw_ref[kk:kk + 1, :] += jnp.sum(dpre * _shift_down(uv, j), axis=0, keepdims=True)
        db_ref[...] += jnp.sum(dpre, axis=0, keepdims=True)
        du_ref[0] = du

    return _pcall(body, name=name, grid=(nb, B),
                  in_specs=[pl.BlockSpec((1, S, tc), lambda j, bb: (bb, 0, col0_blk + j)),
                            pl.BlockSpec((8, tc), lambda j, bb: (0, j)), pl.BlockSpec((1, tc), lambda j, bb: (0, j)),
                            pl.BlockSpec((1, S, tc), lambda j, bb: (bb, 0, j))],
                  out_specs=[pl.BlockSpec((1, S, tc), lambda j, bb: (bb, 0, j)),
                             pl.BlockSpec((8, tc), lambda j, bb: (0, j)), pl.BlockSpec((1, tc), lambda j, bb: (0, j))],
                  out_shape=[jax.ShapeDtypeStruct((B, S, SSD_CONV_DIM), F32), jax.ShapeDtypeStruct((8, SSD_CONV_DIM), F32),
                             jax.ShapeDtypeStruct((1, SSD_CONV_DIM), F32)],
                  compiler_params=_params(("arbitrary", "arbitrary")))(u, w8, b, dy)


def _expand_mat():
    e = np.zeros((LANES, SSD_GW), np.float32)
    for h in range(SSD_HPG):
        e[h, h * SSD_HEAD_DIM:(h + 1) * SSD_HEAD_DIM] = 1.0
    return jnp.asarray(e)


def _tri_mat():
    return jnp.asarray(np.tril(np.ones((SSD_CHUNK, SSD_CHUNK), np.float32)))


def _dotf(a, b, dn=(((1,), (0,)), ((), ()))):
    return lax.dot_general(a, b, dn, precision=HI, preferred_element_type=F32)


def _dotb(a, b, dn=(((1,), (0,)), ((), ()))):
    return lax.dot_general(a.astype(BF16), b.astype(BF16), dn, preferred_element_type=F32)


NT = (((1,), (1,)), ((), ()))
TN = (((0,), (0,)), ((), ()))


def _ssd_common(dt_ref, par_ref, tri_ref, e_ref):
    dtr = dt_ref[0]
    xb = dtr + par_ref[0:1, :]
    dt = jnp.maximum(xb, 0.0) + jnp.log(1.0 + jnp.exp(-jnp.abs(xb)))
    A = -jnp.exp(par_ref[1:2, :])
    a = dt * A
    acs = _dotf(tri_ref[...], a)
    alast = acs[SSD_CHUNK - 1:SSD_CHUNK, :]
    E = e_ref[...]
    acsE = _dotf(acs, E)
    alastE = acsE[SSD_CHUNK - 1:SSD_CHUNK, :]
    return dict(xb=xb, dt=dt, A=A, acs=acs, acsT=acs.T, ealast=jnp.exp(alast), w=jnp.exp(alast - acs),
                dtE=_dotf(dt, E), eacsE=jnp.exp(acsE), wE=jnp.exp(alastE - acsE), ealastE=jnp.exp(alastE),
                DE=_dotf(par_ref[...], E)[2:3, :])


def _decay(c, e):
    diff = c["acs"][:, e:e + 1] - c["acsT"][e:e + 1, :]
    li = lax.broadcasted_iota(jnp.int32, diff.shape, 0)
    si = lax.broadcasted_iota(jnp.int32, diff.shape, 1)
    return jnp.where(li >= si, jnp.exp(jnp.minimum(diff, 0.0)), 0.0)


def ssd_fwd(xbc, dtp, par, name):
    B, S, _ = xbc.shape
    nc = S // SSD_CHUNK
    L, N, GW, P = SSD_CHUNK, SSD_STATE, SSD_GW, SSD_HEAD_DIM

    def body(x_ref, b_ref, c_ref, dt_ref, par_ref, tri_ref, e_ref, y_ref, st_ref, S_scr):
        @pl.when(pl.program_id(2) == 0)
        def _():
            S_scr[...] = jnp.zeros_like(S_scr)

        Sin = S_scr[...]
        st_ref[0, 0, 0] = Sin
        c = _ssd_common(dt_ref, par_ref.at[0], tri_ref, e_ref)
        X = x_ref[0]
        Bm = b_ref[0]
        Cm = c_ref[0]
        xdt = X * c["dtE"]
        CB = _dotb(Cm, Bm, NT)
        y_ref[0] = c["eacsE"] * _dotb(Cm, Sin) + c["DE"] * X
        for e in range(SSD_HPG):
            M = CB * _decay(c, e)
            y_ref[0, :, e * P:(e + 1) * P] += _dotb(M, xdt[:, e * P:(e + 1) * P])
        S_scr[...] = c["ealastE"] * Sin + _dotb(Bm, xdt * c["wE"], TN)

    return _pcall(
        body, name=name, grid=(SSD_GROUPS, B, nc),
        in_specs=[pl.BlockSpec((1, L, GW), lambda g, b, ci: (b, ci, g)),
                  pl.BlockSpec((1, L, N), lambda g, b, ci: (b, ci, 8 + g)),
                  pl.BlockSpec((1, L, N), lambda g, b, ci: (b, ci, 10 + g)),
                  pl.BlockSpec((1, L, LANES), lambda g, b, ci: (b, ci, g)),
                  pl.BlockSpec((1, 8, LANES), lambda g, b, ci: (g, 0, 0)),
                  pl.BlockSpec((L, L), lambda g, b, ci: (0, 0)),
                  pl.BlockSpec((LANES, GW), lambda g, b, ci: (0, 0))],
        out_specs=[pl.BlockSpec((1, L, GW), lambda g, b, ci: (b, ci, g)),
                   pl.BlockSpec((1, 1, 1, N, GW), lambda g, b, ci: (g, b, ci, 0, 0))],
        out_shape=[jax.ShapeDtypeStruct((B, S, SSD_INNER), F32), jax.ShapeDtypeStruct((SSD_GROUPS, B, nc, N, GW), F32)],
        scratch_shapes=[pltpu.VMEM((N, GW), F32)],
        compiler_params=_params(("arbitrary", "arbitrary", "arbitrary")))(xbc, xbc, xbc, dtp, par, _tri_mat(), _expand_mat())


def ssd_bwd(xbc, dtp, par, states, dy, name):
    B, S, _ = xbc.shape
    nc = S // SSD_CHUNK
    L, N, GW, P = SSD_CHUNK, SSD_STATE, SSD_GW, SSD_HEAD_DIM

    def body(x_ref, b_ref, c_ref, dt_ref, par_ref, tri_ref, e_ref, st_ref, dy_ref,
             dx_ref, db_ref, dc_ref, ddt_ref, dpar_ref, dS_scr, dxdt_scr):
        @pl.when(pl.program_id(2) == 0)
        def _():
            dS_scr[...] = jnp.zeros_like(dS_scr)

        @pl.when((pl.program_id(1) == 0) & (pl.program_id(2) == 0))
        def _():
            dpar_ref[...] = jnp.zeros_like(dpar_ref)

        c = _ssd_common(dt_ref, par_ref.at[0], tri_ref, e_ref)
        E = e_ref[...]
        red = lambda t: _dotf(t, E, NT)
        red_row = lambda t: red(jnp.broadcast_to(t, (8, GW)))[0:1, :]
        X, Bm, Cm = x_ref[0], b_ref[0], c_ref[0]
        Sin = st_ref[0, 0, 0]
        dY = dy_ref[0]
        dSo = dS_scr[...]
        xdt = X * c["dtE"]
        CB = _dotb(Cm, Bm, NT)
        dD = red_row(jnp.sum(dY * X, axis=0, keepdims=True))
        yoff = c["eacsE"] * _dotb(Cm, Sin)
        dacs = red(dY * yoff)
        dYe = dY * c["eacsE"]
        dC = _dotb(dYe, Sin, NT)
        dSin = _dotb(Cm, dYe, TN) + c["ealastE"] * dSo
        dB = _dotb(xdt * c["wE"], dSo, NT)
        dxw = _dotb(Bm, dSo)
        t = red(dxw * xdt) * c["w"]
        dacs = dacs - t
        dalast = jnp.sum(t, axis=0, keepdims=True) + c["ealast"] * red_row(jnp.sum(Sin * dSo, axis=0, keepdims=True))
        dxdt_scr[...] = dxw * c["wE"]
        dCB = jnp.zeros((L, L), F32)
        dacsT = jnp.zeros((LANES, L), F32)
        lane = lax.broadcasted_iota(jnp.int32, (1, LANES), 1)
        subl = lax.broadcasted_iota(jnp.int32, (LANES, 1), 0)
        for e in range(SSD_HPG):
            Lm = _decay(c, e)
            M = CB * Lm
            dYh = dY[:, e * P:(e + 1) * P]
            dM = _dotb(dYh, xdt[:, e * P:(e + 1) * P], NT)
            dxdt_scr[:, e * P:(e + 1) * P] += _dotb(M, dYh, TN)
            dCBe = dM * Lm
            dCB = dCB + dCBe
            Gm = dCBe * CB
            dacs = dacs + jnp.sum(Gm, axis=1, keepdims=True) * (lane == e).astype(F32)
            dacsT = dacsT - jnp.sum(Gm, axis=0, keepdims=True) * (subl == e).astype(F32)
        dC = dC + _dotb(dCB, Bm)
        dB = dB + _dotb(dCB, Cm, TN)
        dacs = dacs + dacsT.T
        row = lax.broadcasted_iota(jnp.int32, (L, 1), 0)
        dacs = dacs + jnp.where(row == L - 1, dalast, 0.0)
        da = _dotf(tri_ref[...], dacs, TN)
        dxdt = dxdt_scr[...]
        ddt = da * c["A"] + red(dxdt * X)
        dA = jnp.sum(da * c["dt"], axis=0, keepdims=True)
        ddtr = ddt * _sigmoid(c["xb"])
        dx_ref[0] = c["DE"] * dY + dxdt * c["dtE"]
        db_ref[0] = dB
        dc_ref[0] = dC
        ddt_ref[0] = ddtr
        dpar_ref[0, 0:1, :] += jnp.sum(ddtr, axis=0, keepdims=True)
        dpar_ref[0, 1:2, :] += dA * c["A"]
        dpar_ref[0, 2:3, :] += dD
        dS_scr[...] = dSin

    rc = lambda ci: nc - 1 - ci
    xspec = pl.BlockSpec((1, L, GW), lambda g, b, ci: (b, rc(ci), g))
    return _pcall(
        body, name=name, grid=(SSD_GROUPS, B, nc),
        in_specs=[xspec,
                  pl.BlockSpec((1, L, N), lambda g, b, ci: (b, rc(ci), 8 + g)),
                  pl.BlockSpec((1, L, N), lambda g, b, ci: (b, rc(ci), 10 + g)),
                  pl.BlockSpec((1, L, LANES), lambda g, b, ci: (b, rc(ci), g)),
                  pl.BlockSpec((1, 8, LANES), lambda g, b, ci: (g, 0, 0)),
                  pl.BlockSpec((L, L), lambda g, b, ci: (0, 0)),
                  pl.BlockSpec((LANES, GW), lambda g, b, ci: (0, 0)),
                  pl.BlockSpec((1, 1, 1, N, GW), lambda g, b, ci: (g, b, rc(ci), 0, 0)),
                  xspec],
        out_specs=[xspec,
                   pl.BlockSpec((1, L, N), lambda g, b, ci: (b, rc(ci), g)),
                   pl.BlockSpec((1, L, N), lambda g, b, ci: (b, rc(ci), g)),
                   pl.BlockSpec((1, L, LANES), lambda g, b, ci: (b, rc(ci), g)),
                   pl.BlockSpec((1, 8, LANES), lambda g, b, ci: (g, 0, 0))],
        out_shape=[jax.ShapeDtypeStruct((B, S, SSD_INNER), F32), jax.ShapeDtypeStruct((B, S, SSD_GROUPS * N), F32),
                   jax.ShapeDtypeStruct((B, S, SSD_GROUPS * N), F32), jax.ShapeDtypeStruct((B, S, SSD_GROUPS * LANES), F32),
                   jax.ShapeDtypeStruct((SSD_GROUPS, 8, LANES), F32)],
        scratch_shapes=[pltpu.VMEM((N, GW), F32), pltpu.VMEM((L, GW), F32)],
        compiler_params=_params(("arbitrary", "arbitrary", "arbitrary")))(
            xbc, xbc, xbc, dtp, par, _tri_mat(), _expand_mat(), states, dy)


def swiglu_fwd(g, u, name):
    T = g.shape[0]
    tq, tc = _tile(T, (512, 256, 128)), 512
    nf = FFN // tc

    def body(g_ref, u_ref, o_ref):
        gv = g_ref[...].astype(F32)
        o_ref[...] = (gv * _sigmoid(gv) * u_ref[...].astype(F32)).astype(BF16)

    lo = pl.BlockSpec((tq, tc), lambda i, j: (i, j))
    return _pcall(body, name=name, grid=(T // tq, nf), in_specs=[lo, lo], out_specs=lo,
                  out_shape=jax.ShapeDtypeStruct((T, FFN), BF16), compiler_params=_params(("parallel", "parallel")))(g, u)


def swiglu_bwd(gate, up, dact, name):
    T = gate.shape[0]
    tq, tc = _tile(T, (512, 256, 128)), 512
    nf = FFN // tc

    def body(g_ref, u_ref, d_ref, dg_ref, du_ref):
        g = g_ref[...].astype(F32)
        u = u_ref[...].astype(F32)
        d = d_ref[...]
        s = _sigmoid(g)
        dg_ref[...] = (d * u * (s * (1.0 + g * (1.0 - s)))).astype(BF16)
        du_ref[...] = (d * g * s).astype(BF16)

    lo = pl.BlockSpec((tq, tc), lambda i, j: (i, j))
    return _pcall(body, name=name, grid=(T // tq, nf), in_specs=[lo, lo, lo], out_specs=[lo, lo],
                  out_shape=[jax.ShapeDtypeStruct((T, FFN), BF16)] * 2,
                  compiler_params=_params(("parallel", "parallel")))(gate, up, dact)


def loss_head(y, target):
    T, D = y.shape
    tq = _tile(T, (256, 128))

    def body(y_ref, t_ref, dy_ref, l_ref, dyb_ref):
        @pl.when(pl.program_id(0) == 0)
        def _():
            l_ref[...] = jnp.zeros_like(l_ref)

        e = y_ref[...] - t_ref[...]
        dy = e * (1.0 / D)
        dy_ref[...] = dy
        dyb_ref[...] = dy.astype(BF16)
        l_ref[...] += 0.5 * jnp.sum(jnp.mean(e * e, axis=-1, keepdims=True))

    xs = pl.BlockSpec((tq, D), lambda i: (i, 0))
    return _pcall(body, name="loss_head", grid=(T // tq,), in_specs=[xs, xs],
                  out_specs=[xs, pl.BlockSpec((8, LANES), lambda i: (0, 0)), xs],
                  out_shape=[jax.ShapeDtypeStruct((T, D), F32), jax.ShapeDtypeStruct((8, LANES), F32),
                             jax.ShapeDtypeStruct((T, D), BF16)],
                  compiler_params=_params(("arbitrary",)))(y, target)


def sum_arrays(arrs, out_dtype, name):
    R, C = arrs[0].shape
    tr = _tile(R, (512, 256, 128, 64, 32, 16, 8))
    n = len(arrs)

    def body(*refs):
        acc = refs[0][...].astype(F32)
        for r in refs[1:n]:
            acc = acc + r[...].astype(F32)
        refs[n][...] = acc.astype(out_dtype)

    xs = pl.BlockSpec((tr, C), lambda i: (i, 0))
    return _pcall(body, name=name, grid=(R // tr,), in_specs=[xs] * n, out_specs=xs,
                  out_shape=jax.ShapeDtypeStruct((R, C), out_dtype), compiler_params=_params(("parallel",)))(*arrs)


def adamw(w, g, m, v, name):
    R, C = w.shape
    tr = _tile(R, (256, 128, 64, 32, 16, 8))
    c1 = 1.0 - ADAM_B1 ** ADAM_STEP
    c2 = 1.0 - ADAM_B2 ** ADAM_STEP

    def body(w_ref, g_ref, m_ref, v_ref, d_ref, nm_ref, nv_ref):
        gv = g_ref[...]
        mn = ADAM_B1 * m_ref[...] + (1.0 - ADAM_B1) * gv
        vn = ADAM_B2 * v_ref[...] + (1.0 - ADAM_B2) * (gv * gv)
        nm_ref[...] = mn
        nv_ref[...] = vn
        d_ref[...] = -ADAM_LR * ((mn / c1) / (jnp.sqrt(vn / c2) + ADAM_EPS) + ADAM_WD * w_ref[...])

    xs = pl.BlockSpec((tr, C), lambda i: (i, 0))
    return _pcall(body, name=name, grid=(R // tr,), in_specs=[xs] * 4, out_specs=[xs] * 3,
                  out_shape=[jax.ShapeDtypeStruct((R, C), F32)] * 3, compiler_params=_params(("parallel",)))(w, g, m, v)


def adamw_layer(w, g, m, v, l, prev, name):
    L, r, c = w.shape
    tr = _tile(r, (256, 128, 64, 32, 16, 8))
    c1 = 1.0 - ADAM_B1 ** ADAM_STEP
    c2 = 1.0 - ADAM_B2 ** ADAM_STEP
    chained = prev is not None

    def body(w_ref, g_ref, m_ref, v_ref, *rest):
        go_ref, d_ref, nm_ref, nv_ref = rest[-4:]
        gv = g_ref[...]
        mn = ADAM_B1 * m_ref[...] + (1.0 - ADAM_B1) * gv
        vn = ADAM_B2 * v_ref[...] + (1.0 - ADAM_B2) * (gv * gv)
        go_ref[...] = gv
        nm_ref[...] = mn
        nv_ref[...] = vn
        d_ref[...] = -ADAM_LR * ((mn / c1) / (jnp.sqrt(vn / c2) + ADAM_EPS) + ADAM_WD * w_ref[...])

    ls = pl.BlockSpec((None, tr, c), lambda i: (l, i, 0))
    gs = pl.BlockSpec((tr, c), lambda i: (i, 0))
    anys = [pl.BlockSpec(memory_space=pl.ANY)] * 4 if chained else []
    return _pcall(body, name=name, grid=(r // tr,), in_specs=[ls, gs, ls, ls] + anys, out_specs=[ls] * 4,
                  out_shape=[jax.ShapeDtypeStruct((L, r, c), F32)] * 4,
                  input_output_aliases={4: 0, 5: 1, 6: 2, 7: 3} if chained else {},
                  compiler_params=_params(("parallel",)))(w, g, m, v, *(prev if chained else ()))


def add_pair(g, got, name):
    _, r, c = g.shape
    h = r // 2

    def body(c_ref, g_ref, o_ref, out_ref):
        del c_ref
        out_ref[...] = (g_ref[...].astype(F32) + o_ref[...].astype(F32)).astype(BF16)

    spec = pltpu.PrefetchScalarGridSpec(
        num_scalar_prefetch=1, grid=(4,),
        in_specs=[pl.BlockSpec((None, h, c), lambda k, cs: (k, cs[0], 0)), pl.BlockSpec((None, h, c), lambda k, cs: (k, 0, 0))],
        out_specs=pl.BlockSpec((None, h, c), lambda k, cs: (k, 0, 0)))
    return _pcall(body, name=name, grid_spec=spec, out_shape=jax.ShapeDtypeStruct((4, h, c), BF16),
                  compiler_params=_params(("arbitrary",)))(lax.axis_index("c").reshape(1).astype(jnp.int32), g, got)


def add_chips(p, b, name):
    _, h, c = p.shape
    tr = _tile(h, (256, 128, 64, 32, 16, 8))

    def body(me_ref, p_ref, b0, b1, b2, b3, out_ref):
        me = me_ref[0]
        acc = jnp.zeros(out_ref.shape, F32)
        for k, bk in enumerate((b0, b1, b2, b3)):
            acc = acc + jnp.where(me == k, p_ref[...], bk[...]).astype(F32)
        out_ref[...] = acc

    slot = lambda k: pl.BlockSpec((None, tr, c), lambda i, ms: (jnp.where(ms[0] == k, (k + 1) % 4, k), i, 0))
    spec = pltpu.PrefetchScalarGridSpec(
        num_scalar_prefetch=1, grid=(h // tr,),
        in_specs=[pl.BlockSpec((None, tr, c), lambda i, ms: (ms[0], i, 0)), slot(0), slot(1), slot(2), slot(3)],
        out_specs=pl.BlockSpec((tr, c), lambda i, ms: (i, 0)))
    me = (2 * lax.axis_index("x") + lax.axis_index("y")).reshape(1).astype(jnp.int32)
    return _pcall(body, name=name, grid_spec=spec, out_shape=jax.ShapeDtypeStruct((h, c), F32),
                  compiler_params=_params(("arbitrary",)))(me, p, b, b, b, b)


ANY = pl.BlockSpec(memory_space=pl.ANY)


def _place():
    x, y, c = lax.axis_index("x"), lax.axis_index("y"), lax.axis_index("c")
    chips = [(1 - x, y), (x, 1 - y), (1 - x, 1 - y)]
    return x, y, c, chips


def _rcopy(src, dst, ssem, rsem, dev):
    return pltpu.make_async_remote_copy(src_ref=src, dst_ref=dst, send_sem=ssem, recv_sem=rsem, device_id=dev,
                                        device_id_type=MESH)


def _half(c, h):
    return pl.ds(pl.multiple_of(c * h, 64), h)


def _me_chip():
    return 2 * lax.axis_index("x") + lax.axis_index("y")


def all_gather_weights(shards, name):
    n = len(shards)
    L = shards[0].shape[0]
    hl = L // 2

    def body(*refs):
        srcs, outs, (ssem, rsem) = refs[:n], refs[n:2 * n], refs[2 * n:]
        x, y, c, chips = _place()
        me = 2 * x + y
        sib = (x, y, 1 - c)
        mine, other = pl.ds(c * hl, hl), pl.ds((1 - c) * hl, hl)
        sends = []
        for w in range(n):
            for j, (px, py) in enumerate(chips):
                cp = _rcopy(srcs[w].at[mine], outs[w].at[mine, me], ssem.at[w, j], rsem.at[w, j], (px, py, c))
                cp.start()
                sends.append(cp)
        for j, (px, py) in enumerate(chips):
            slot = 2 * px + py
            for w in range(n):
                _rcopy(srcs[w].at[mine], outs[w].at[mine, slot], ssem.at[w, j], rsem.at[w, j], (px, py, c)).wait_recv()
                cp = _rcopy(outs[w].at[mine, slot], outs[w].at[mine, slot], ssem.at[w, 3 + j], rsem.at[w, 3 + j], sib)
                cp.start()
                sends.append(cp)
        for j, (px, py) in enumerate(chips):
            slot = 2 * px + py
            for w in range(n):
                _rcopy(outs[w].at[other, slot], outs[w].at[other, slot], ssem.at[w, 3 + j], rsem.at[w, 3 + j], sib).wait_recv()
        for cp in sends:
            cp.wait_send()

    outs = _pcall(body, name=name, in_specs=[ANY] * n, out_specs=[ANY] * n,
                  out_shape=[jax.ShapeDtypeStruct((L, 4) + s.shape[1:], s.dtype) for s in shards],
                  scratch_shapes=[pltpu.SemaphoreType.DMA((n, 6)), pltpu.SemaphoreType.DMA((n, 6))])(*shards)
    me = _me_chip()
    return [lax.dynamic_update_slice(o, s[:, None], (0, me, 0, 0)) for o, s in zip(outs, shards)]


def exchange_pair(gs, name):
    n = len(gs)

    def body(*refs):
        srcs, outs, (ssem, rsem) = refs[:n], refs[n:2 * n], refs[2 * n:]
        x, y, c, _ = _place()
        cps = []
        for w in range(n):
            h = srcs[w].shape[1] // 2
            cp = _rcopy(srcs[w].at[:, _half(1 - c, h)], outs[w], ssem.at[w], rsem.at[w], (x, y, 1 - c))
            cp.start()
            cps.append(cp)
        for cp in cps:
            cp.wait()

    return _pcall(body, name=name, in_specs=[ANY] * n, out_specs=[ANY] * n,
                  out_shape=[jax.ShapeDtypeStruct((4, g.shape[1] // 2, g.shape[2]), g.dtype) for g in gs],
                  scratch_shapes=[pltpu.SemaphoreType.DMA((n,)), pltpu.SemaphoreType.DMA((n,))])(*gs)


def scatter_chips(ps, name):
    n = len(ps)

    def body(*refs):
        srcs, outs, (ssem, rsem) = refs[:n], refs[n:2 * n], refs[2 * n:]
        x, y, c, chips = _place()
        me = 2 * x + y
        sends = []
        for w in range(n):
            for j, (px, py) in enumerate(chips):
                cp = _rcopy(srcs[w].at[2 * px + py], outs[w].at[me], ssem.at[w, j], rsem.at[w, j], (px, py, c))
                cp.start()
                sends.append(cp)
        for w in range(n):
            for j, (px, py) in enumerate(chips):
                _rcopy(srcs[w].at[me], outs[w].at[2 * px + py], ssem.at[w, j], rsem.at[w, j], (px, py, c)).wait_recv()
        for cp in sends:
            cp.wait_send()

    return _pcall(body, name=name, in_specs=[ANY] * n, out_specs=[ANY] * n,
                  out_shape=[jax.ShapeDtypeStruct(p.shape, p.dtype) for p in ps],
                  scratch_shapes=[pltpu.SemaphoreType.DMA((n, 3)), pltpu.SemaphoreType.DMA((n, 3))])(*ps)


def join_pair(rs, name):
    n = len(rs)

    def body(*refs):
        srcs, outs, (ssem, rsem) = refs[:n], refs[n:2 * n], refs[2 * n:]
        x, y, c, _ = _place()
        cps = []
        for w in range(n):
            h = srcs[w].shape[0]
            cp = _rcopy(srcs[w], outs[w].at[_half(c, h)], ssem.at[w], rsem.at[w], (x, y, 1 - c))
            cp.start()
            cps.append(cp)
        for w in range(n):
            h = srcs[w].shape[0]
            _rcopy(srcs[w], outs[w].at[_half(1 - c, h)], ssem.at[w], rsem.at[w], (x, y, 1 - c)).wait_recv()
        for cp in cps:
            cp.wait_send()

    outs = _pcall(body, name=name, in_specs=[ANY] * n, out_specs=[ANY] * n,
                  out_shape=[jax.ShapeDtypeStruct((2 * r.shape[0], r.shape[1]), r.dtype) for r in rs],
                  scratch_shapes=[pltpu.SemaphoreType.DMA((n,)), pltpu.SemaphoreType.DMA((n,))])(*rs)
    c = lax.axis_index("c")
    return [lax.dynamic_update_slice(o, r, (c * r.shape[0], 0)) for o, r in zip(outs, rs)]


def gather_all_devices(v, name):
    R, C = v.shape

    def body(src, out, ssem, rsem, lsem):
        x, y, c, _ = _place()
        me = 4 * x + 2 * y + c
        local = pltpu.make_async_copy(src, out.at[me], lsem)
        local.start()
        sends = []
        for j in range(1, 8):
            dx, dy, dc = (j >> 2) & 1, (j >> 1) & 1, j & 1
            px, py, pc = (1 - x if dx else x), (1 - y if dy else y), (1 - c if dc else c)
            cp = _rcopy(src, out.at[me], ssem.at[j - 1], rsem.at[j - 1], (px, py, pc))
            cp.start()
            sends.append(cp)
        for j in range(1, 8):
            dx, dy, dc = (j >> 2) & 1, (j >> 1) & 1, j & 1
            px, py, pc = (1 - x if dx else x), (1 - y if dy else y), (1 - c if dc else c)
            _rcopy(src, out.at[4 * px + 2 * py + pc], ssem.at[j - 1], rsem.at[j - 1], (px, py, pc)).wait_recv()
        for cp in sends:
            cp.wait_send()
        local.wait()

    return _pcall(body, name=name, in_specs=[ANY], out_specs=ANY, out_shape=jax.ShapeDtypeStruct((8, R, C), v.dtype),
                  scratch_shapes=[pltpu.SemaphoreType.DMA((7,)), pltpu.SemaphoreType.DMA((7,)), pltpu.SemaphoreType.DMA(())])(v)


def reduce_scatter_layer(gw, tag):
    names = list(gw)
    gs = [gw[k] for k in names]
    got = exchange_pair(gs, "rs_pair_" + tag)
    ps = [add_pair(g, o, "rs_add_pair_%s_%s" % (k, tag)) for k, g, o in zip(names, gs, got)]
    bs = scatter_chips(ps, "rs_chips_" + tag)
    rs = [add_chips(p, b, "rs_add_chips_%s_%s" % (k, tag)) for k, p, b in zip(names, ps, bs)]
    return dict(zip(names, join_pair(rs, "rs_join_" + tag)))


GATHERED = ("w_in", "w_q_b", "w_kv_b", "w_out", "w_xq", "w_xk", "w_xv", "w_xo", "w_gate", "w_up", "w_down")
KIND = {"w_q_b": "col", "w_kv_b": "col", "w_out": "row", "w_xq": "row", "w_xk": "row", "w_xv": "row", "w_xo": "col",
        "w_gate": "col", "w_up": "col", "w_down": "row"}


def wq_to_storage(w):
    lead = w.shape[:-1]
    w = w.reshape(lead + (2, MLA_QK))
    rope = jnp.pad(w[..., MLA_NOPE:], [(0, 0)] * (w.ndim - 1) + [(0, LANES - MLA_ROPE)])
    return jnp.concatenate([w[..., :MLA_NOPE].reshape(lead + (2 * LANES,)), rope.reshape(lead + (2 * LANES,))], axis=-1)


def wq_from_storage(g):
    lead = g.shape[:-1]
    nope = g[..., :2 * LANES].reshape(lead + (2, LANES))
    rope = g[..., 2 * LANES:].reshape(lead + (2, LANES))[..., :MLA_ROPE]
    return jnp.concatenate([nope, rope], axis=-1).reshape(lead + (2 * MLA_QK,))


def wcat_from_storage(w_in_l):
    w = w_in_l.transpose(1, 0, 2).reshape(D_MODEL, IN_COLS)
    zc = lambda n: jnp.zeros((D_MODEL, n), w.dtype)
    return jnp.concatenate([w[:, :2560], w[:, 2576:], zc(64), w[:, 2560:2576], zc(112)], axis=1)


def w_in_grad_storage(dwcat):
    g = jnp.concatenate([dwcat[:, :2560], dwcat[:, C_DT:C_DT + 16], dwcat[:, 2560:3648]], axis=1)
    return g.reshape(D_MODEL, 4, IN_COLS // 4).transpose(1, 0, 2)


def _ssd_par(sm):
    rows = jnp.stack([sm["dt_bias"], sm["a_log"], sm["d_skip"]]).reshape(3, SSD_GROUPS, SSD_HPG).transpose(1, 0, 2)
    return jnp.pad(rows, ((0, 0), (0, 5), (0, LANES - SSD_HPG)))


def _dt_pad(proj, B, S):
    dt = proj[:, C_DT:C_DT + 16].reshape(B, S, SSD_GROUPS, SSD_HPG)
    return jnp.pad(dt, ((0, 0), (0, 0), (0, 0), (0, LANES - SSD_HPG))).reshape(B, S, SSD_GROUPS * LANES)


def _mla_gains(g):
    return g[:MLA_NOPE].reshape(1, LANES), jnp.pad(g[MLA_NOPE:], (0, LANES - MLA_ROPE)).reshape(1, LANES)


_Q_NOPE = lambda h: 4 * (h // 2) + h % 2
_Q_ROPE = lambda h: 4 * (h // 2) + 2 + h % 2
_K_NOPE = lambda h: 2 * h
_V_BLK = lambda h: 2 * h + 1


def layer_fwd(x, memf, cos, sin, G, kw, sm, B, S, l):
    T = x.shape[0]
    n = lambda s: "%s_l%d" % (s, l)
    wv = lambda k: (KIND[k], l)
    sv = dict(x0=x)
    h1 = norm_fwd(x, sm["attn_norm_g"], D_MODEL, False, n("attn_norm"))
    proj = mm(h1, kw["wcat"], "nn", n("in_proj"))
    sv.update(h1=h1, proj=proj)
    proj3 = proj.reshape(B, S, CAT_COLS)
    xbc = conv_fwd(proj3, C_XBC // 256, kw["conv_w"], sm["conv_b"].reshape(1, -1), n("conv"))
    dtp = _dt_pad(proj, B, S)
    par = _ssd_par(sm)
    y_raw, states = ssd_fwd(xbc, dtp, par, n("ssd"))
    z = proj[:, C_Z:C_Z + SSD_INNER]
    y_ssd = norm_fwd(y_raw.reshape(T, SSD_INNER), sm["ssd_norm_g"], SSD_INNER // SSD_GROUPS, False, n("ssd_norm"), gate=z)
    sv.update(xbc=xbc, states=states, y_raw=y_raw)
    qn = norm_fwd(proj[:, C_QA:C_QA + Q_LORA], sm["q_a_norm_g"], Q_LORA, False, n("q_a_norm"))
    kvn = norm_fwd(proj[:, C_KVA:C_KVA + KV_LORA], sm["kv_a_norm_g"], KV_LORA, False, n("kv_a_norm"))
    q_raw = mm(qn, G["w_q_b"], "nn", n("q_b"), wv=wv("w_q_b"))
    kv_raw = mm(kvn, G["w_kv_b"], "nn", n("kv_b"), wv=wv("w_kv_b"))
    gqn, gqr = _mla_gains(sm["mla_q_norm_g"])
    gkn, gkr = _mla_gains(sm["mla_k_norm_g"])
    qh = qk_fwd(q_raw, _Q_NOPE, q_raw, _Q_ROPE, False, gqn, gqr, cos, sin, n("q_norm_rope"))
    kh = qk_fwd(kv_raw, _K_NOPE, proj, C_KR // LANES, True, gkn, gkr, cos, sin, n("k_norm_rope"))
    vv = kv_raw.astype(BF16)
    y_mla = attn_fwd(qh.reshape(B, S, -1), kh.reshape(B, S, -1), vv.reshape(B, S, -1), MLA_HEADS, 2 * LANES, MLA_V,
                     MLA_QK ** -0.5, True, n("mla_attn"), v_idx=_V_BLK)
    ycat = jnp.concatenate([y_ssd, y_mla.reshape(T, -1)], axis=1)
    x1 = mm(ycat, G["w_out"], "nn", n("out_proj"), add=x, wv=wv("w_out"))
    sv.update(qn=qn, kvn=kvn, q_raw=q_raw, kv_raw=kv_raw, qh=qh, kh=kh, vv=vv, ycat=ycat, x1=x1)
    h2 = norm_fwd(x1, sm["xattn_norm_g"], D_MODEL, False, n("xattn_norm"))
    mn = norm_fwd(memf, sm["mem_norm_g"], D_MODEL, False, n("mem_norm"))
    xq = mm(h2, G["w_xq"], "nn", n("xq"), wv=wv("w_xq"))
    xk = mm(mn, G["w_xk"], "nn", n("xk"), wv=wv("w_xk"))
    xv = mm(mn, G["w_xv"], "nn", n("xv"), wv=wv("w_xv"), out_dtype=BF16)
    xqn = norm_fwd(xq, sm["xq_norm_g"], X_HEAD_DIM, True, n("xq_norm"))
    xkn = norm_fwd(xk, sm["xk_norm_g"], X_HEAD_DIM, True, n("xk_norm"))
    ML = memf.shape[0] // B
    xo = attn_fwd(xqn.reshape(B, S, -1), xkn.reshape(B, ML, -1), xv.reshape(B, ML, -1), X_HEADS, X_HEAD_DIM, X_HEAD_DIM,
                  X_HEAD_DIM ** -0.5, False, n("x_attn"))
    x2 = mm(xo.reshape(T, -1), G["w_xo"], "nn", n("xo_proj"), add=x1, wv=wv("w_xo"))
    sv.update(h2=h2, mn=mn, xq=xq, xk=xk, xqn=xqn, xkn=xkn, xv=xv, xo=xo, x2=x2)
    h3 = norm_fwd(x2, sm["ffn_norm_g"], D_MODEL, False, n("ffn_norm"))
    gate = mm(h3, G["w_gate"], "nn", n("gate"), out_dtype=BF16, wv=wv("w_gate"))
    up = mm(h3, G["w_up"], "nn", n("up"), out_dtype=BF16, wv=wv("w_up"))
    act = swiglu_fwd(gate, up, n("swiglu"))
    x3 = mm(act, G["w_down"], "nn", n("down"), add=x2, wv=wv("w_down"))
    sv.update(h3=h3, gate=gate, up=up, act=act)
    return x3, sv


def layer_bwd(dx3, dx3b, sv, memf, cos, sin, G, kw, sm, B, S, l):
    T = dx3.shape[0]
    n = lambda s: "%s_bwd_l%d" % (s, l)
    wv = lambda k: (KIND[k], l)
    dw = lambda k, a, b: mm(a, b, "tn", n(k + "_dw"), out_dtype=BF16, out_wv=KIND[k])
    gw, gs = {}, {}
    dact = mm(dx3b, G["w_down"], "nt", n("down_dx"), wv=wv("w_down"))
    gw["w_down"] = dw("w_down", sv["act"], dx3b)
    dgate, dup = swiglu_bwd(sv["gate"], sv["up"], dact, n("swiglu"))
    dh3 = mm(dgate, G["w_gate"], "nt", n("gate_dx"), wv=wv("w_gate"))
    dh3 = mm(dup, G["w_up"], "nt", n("up_dx"), wv=wv("w_up"), add=dh3)
    gw["w_gate"] = dw("w_gate", sv["h3"], dgate)
    gw["w_up"] = dw("w_up", sv["h3"], dup)
    dx2, gs["ffn_norm_g"], dx2b = norm_bwd(sv["x2"], sm["ffn_norm_g"], dh3, D_MODEL, False, n("ffn_norm"), add=dx3,
                                           bf16_copy=True)
    dxo = mm(dx2b, G["w_xo"], "nt", n("xo_dx"), wv=wv("w_xo"))
    gw["w_xo"] = dw("w_xo", sv["xo"].reshape(T, -1), dx2b)
    ML = memf.shape[0] // B
    dxqn, dxkn, dxv = attn_bwd(sv["xqn"].reshape(B, S, -1), sv["xkn"].reshape(B, ML, -1), sv["xv"].reshape(B, ML, -1),
                               dxo.reshape(B, S, -1), X_HEADS, X_HEAD_DIM, X_HEAD_DIM, X_HEAD_DIM ** -0.5, False, n("x_attn"))
    dxq, gs["xq_norm_g"] = norm_bwd(sv["xq"], sm["xq_norm_g"], dxqn.reshape(T, -1), X_HEAD_DIM, True, n("xq_norm"))
    dxk, gs["xk_norm_g"] = norm_bwd(sv["xk"], sm["xk_norm_g"], dxkn.reshape(B * ML, -1), X_HEAD_DIM, True, n("xk_norm"))
    dxv = dxv.reshape(B * ML, -1)
    dmn = mm(dxk, G["w_xk"], "nt", n("xk_dx"), wv=wv("w_xk"))
    dmn = mm(dxv, G["w_xv"], "nt", n("xv_dx"), wv=wv("w_xv"), add=dmn)
    gw["w_xk"] = dw("w_xk", sv["mn"], dxk)
    gw["w_xv"] = dw("w_xv", sv["mn"], dxv)
    _, gs["mem_norm_g"] = norm_bwd(memf, sm["mem_norm_g"], dmn, D_MODEL, False, n("mem_norm"))
    dh2 = mm(dxq, G["w_xq"], "nt", n("xq_dx"), wv=wv("w_xq"))
    gw["w_xq"] = dw("w_xq", sv["h2"], dxq)
    dx1, gs["xattn_norm_g"], dx1b = norm_bwd(sv["x1"], sm["xattn_norm_g"], dh2, D_MODEL, False, n("xattn_norm"), add=dx2,
                                             bf16_copy=True)
    dycat = mm(dx1b, G["w_out"], "nt", n("out_dx"), wv=wv("w_out"))
    gw["w_out"] = dw("w_out", sv["ycat"], dx1b)
    dqh, dkh, dvv = attn_bwd(sv["qh"].reshape(B, S, -1), sv["kh"].reshape(B, S, -1), sv["vv"].reshape(B, S, -1),
                             dycat[:, SSD_INNER:].reshape(B, S, -1), MLA_HEADS, 2 * LANES, MLA_V, MLA_QK ** -0.5, True,
                             n("mla_attn"), v_idx=_V_BLK)
    gqn, gqr = _mla_gains(sm["mla_q_norm_g"])
    gkn, gkr = _mla_gains(sm["mla_k_norm_g"])
    dq_raw, dgqn, dgqr = qk_bwd(sv["q_raw"], _Q_NOPE, sv["q_raw"], _Q_ROPE, False, gqn, gqr, cos, sin,
                                dqh.reshape(T, -1), n("q_norm_rope"))
    dkv_raw, dk_rope, dgkn, dgkr = qk_bwd(sv["kv_raw"], _K_NOPE, sv["proj"], C_KR // LANES, True, gkn, gkr, cos, sin,
                                          dkh.reshape(T, -1), n("k_norm_rope"), fill=dvv.reshape(T, -1), fill_blk=_V_BLK)
    gs["mla_q_norm_g"] = jnp.concatenate([dgqn[0], dgqr[0, :MLA_ROPE]])
    gs["mla_k_norm_g"] = jnp.concatenate([dgkn[0], dgkr[0, :MLA_ROPE]])
    dqn = mm(dq_raw, G["w_q_b"], "nt", n("q_b_dx"), wv=wv("w_q_b"))
    gw["w_q_b"] = dw("w_q_b", sv["qn"], dq_raw)
    dkvn = mm(dkv_raw, G["w_kv_b"], "nt", n("kv_b_dx"), wv=wv("w_kv_b"))
    gw["w_kv_b"] = dw("w_kv_b", sv["kvn"], dkv_raw)
    proj = sv["proj"]
    dq_a, gs["q_a_norm_g"] = norm_bwd(proj[:, C_QA:C_QA + Q_LORA], sm["q_a_norm_g"], dqn, Q_LORA, False, n("q_a_norm"))
    dkv_a, gs["kv_a_norm_g"] = norm_bwd(proj[:, C_KVA:C_KVA + KV_LORA], sm["kv_a_norm_g"], dkvn, KV_LORA, False,
                                        n("kv_a_norm"))
    z = proj[:, C_Z:C_Z + SSD_INNER]
    dy_raw, dz, gs["ssd_norm_g"] = norm_bwd(sv["y_raw"].reshape(T, -1), sm["ssd_norm_g"], dycat[:, :SSD_INNER],
                                            SSD_INNER // SSD_GROUPS, False, n("ssd_norm"), gate=z)
    dtp = _dt_pad(proj, B, S)
    par = _ssd_par(sm)
    dxs, dBm, dCm, ddt, dpar = ssd_bwd(sv["xbc"], dtp, par, sv["states"], dy_raw.reshape(B, S, -1), n("ssd"))
    gs["dt_bias"] = dpar[:, 0, :SSD_HPG].reshape(-1)
    gs["a_log"] = dpar[:, 1, :SSD_HPG].reshape(-1)
    gs["d_skip"] = dpar[:, 2, :SSD_HPG].reshape(-1)
    dxbc_act = jnp.concatenate([dxs, dBm, dCm], axis=2)
    dxbc, dconv_w, dconv_b = conv_bwd(proj.reshape(B, S, -1), C_XBC // 256, kw["conv_w"], sm["conv_b"].reshape(1, -1), dxbc_act,
                                      n("conv"))
    gs["conv_w"] = dconv_w[:SSD_CONV]
    gs["conv_b"] = dconv_b[0]
    ddt16 = ddt.reshape(T, SSD_GROUPS, LANES)[:, :, :SSD_HPG].reshape(T, 16)
    dproj = jnp.concatenate([dz, dxbc.reshape(T, -1), dq_a, dkv_a, dk_rope, ddt16, jnp.zeros((T, 112), F32)],
                            axis=1).astype(BF16)
    dh1 = mm(dproj, kw["wcat"], "nt", n("in_dx"))
    gw["w_in"] = w_in_grad_storage(mm(sv["h1"], dproj, "tn", n("in_dw"), out_dtype=BF16))
    dx0, gs["attn_norm_g"], dx0b = norm_bwd(sv["x0"], sm["attn_norm_g"], dh1, D_MODEL, False, n("attn_norm"), add=dx1,
                                            bf16_copy=True)
    gs = {k: v.reshape(-1) for k, v in gs.items()}
    return dx0, dx0b, gw, gs


def _small_pack(vals):
    flat = jnp.concatenate([vals[l][k].reshape(-1) for l in range(DEPTH) for k in SMALL_RS])
    n = flat.shape[0]
    rows = -(-n // LANES)
    rows = -(-rows // 8) * 8
    return jnp.pad(flat, (0, rows * LANES - n)).reshape(rows, LANES), n


def _small_unpack(flat, shapes):
    v = flat.reshape(-1)
    out, off = [dict() for _ in range(DEPTH)], 0
    for l in range(DEPTH):
        for k in SMALL_RS:
            sz = shapes[k]
            out[l][k] = v[off:off + sz]
            off += sz
    return out


def _adam_2d(a):
    if a.ndim == 2:
        return a
    return a.reshape(a.shape[0] * a.shape[1], a.shape[2])


def kernel(x, mem, positions, attn_norm_g, w_in, conv_w, conv_b, dt_bias, a_log, d_skip, ssd_norm_g, q_a_norm_g, w_q_b, kv_a_norm_g, w_kv_b, mla_q_norm_g, mla_k_norm_g, w_out, xattn_norm_g, mem_norm_g, w_xq, w_xk, w_xv, xq_norm_g, xk_norm_g, w_xo, ffn_norm_g, w_gate, w_up, w_down, loss_target, m_attn_norm_g, m_w_in, m_conv_w, m_conv_b, m_dt_bias, m_a_log, m_d_skip, m_ssd_norm_g, m_q_a_norm_g, m_w_q_b, m_kv_a_norm_g, m_w_kv_b, m_mla_q_norm_g, m_mla_k_norm_g, m_w_out, m_xattn_norm_g, m_mem_norm_g, m_w_xq, m_w_xk, m_w_xv, m_xq_norm_g, m_xk_norm_g, m_w_xo, m_ffn_norm_g, m_w_gate, m_w_up, m_w_down, v_attn_norm_g, v_w_in, v_conv_w, v_conv_b, v_dt_bias, v_a_log, v_d_skip, v_ssd_norm_g, v_q_a_norm_g, v_w_q_b, v_kv_a_norm_g, v_w_kv_b, v_mla_q_norm_g, v_mla_k_norm_g, v_w_out, v_xattn_norm_g, v_mem_norm_g, v_w_xq, v_w_xk, v_w_xv, v_xq_norm_g, v_xk_norm_g, v_w_xo, v_ffn_norm_g, v_w_gate, v_w_up, v_w_down):
    loc = locals()
    W = {k: loc[k] for k in WEIGHTS}
    Mo = {k: loc["m_" + k] for k in WEIGHTS}
    Vo = {k: loc["v_" + k] for k in WEIGHTS}
    B, S, D = x.shape
    T = B * S
    xf = x.reshape(T, D)
    memf = mem.reshape(-1, D)
    tf = loss_target.reshape(T, D)
    cos, sin = rope_tables(positions.reshape(T, 1).astype(F32))

    shards = [(wq_to_storage(W[k]) if k == "w_q_b" else W[k]).astype(BF16) for k in GATHERED]
    G = dict(zip(GATHERED, all_gather_weights(shards, "all_gather_weights")))
    cw = gather_all_devices(conv_w.reshape(DEPTH * SSD_CONV, -1), "gather_conv_w")[0::2]
    cw = cw.transpose(1, 0, 2).reshape(DEPTH, SSD_CONV, SSD_CONV_DIM)
    kws, sms = [], []
    for l in range(DEPTH):
        kws.append(dict(wcat=wcat_from_storage(G["w_in"][l]), conv_w=jnp.pad(cw[l], ((0, 8 - SSD_CONV), (0, 0)))))
        sms.append({k: W[k][l] for k in SMALL})

    saved = []
    h = xf
    for l in range(DEPTH):
        h, sv = layer_fwd(h, memf, cos, sin, G, kws[l], sms[l], B, S, l)
        saved.append(sv)
    dh, lpart, dhb = loss_head(h, tf)
    loss = lax.psum(lpart[0, 0], ("x", "y", "c"))

    grads, delta, new_m, new_v = {}, {}, {}, {}
    chain = {k: None for k in GATHERED}
    gsmall = [None] * DEPTH
    for l in reversed(range(DEPTH)):
        dh, dhb, gw, gs = layer_bwd(dh, dhb, saved[l], memf, cos, sin, G, kws[l], sms[l], B, S, l)
        saved[l] = None
        gsmall[l] = gs
        red = reduce_scatter_layer(gw, "l%d" % l)
        for k in GATHERED:
            g = wq_from_storage(red[k]) if k == "w_q_b" else red[k]
            chain[k] = adamw_layer(W[k], g, Mo[k], Vo[k], l, chain[k], "adamw_%s_l%d" % (k, l))
    for k in GATHERED:
        grads[k], delta[k], new_m[k], new_v[k] = chain[k]
    grad_x = dh.reshape(B, S, D)

    spack, _ = _small_pack(gsmall)
    allv = gather_all_devices(spack, "gather_small_grads")
    ssum = sum_arrays([allv[i] for i in range(8)], F32, "sum_small_grads")
    sizes = {k: int(np.prod(W[k].shape[1:])) for k in SMALL}
    sizes["conv_w"] = SSD_CONV * SSD_CONV_DIM
    gsm = _small_unpack(ssum, sizes)
    for k in SMALL_RS:
        g = jnp.stack([gsm[l][k] for l in range(DEPTH)])
        if k == "conv_w":
            cs = SSD_CONV_DIM // 4
            g = lax.dynamic_slice_in_dim(g.reshape(DEPTH, SSD_CONV, SSD_CONV_DIM), _me_chip() * cs, cs, axis=2)
        shp = W[k].shape
        grads[k] = g.reshape(shp)
        d, nm, nv = adamw(_adam_2d(W[k]), _adam_2d(grads[k]), _adam_2d(Mo[k]), _adam_2d(Vo[k]), "adamw_" + k)
        delta[k], new_m[k], new_v[k] = d.reshape(shp), nm.reshape(shp), nv.reshape(shp)

    return (loss, grad_x, *[grads[k] for k in WEIGHTS], *[delta[k] for k in WEIGHTS], *[new_m[k] for k in WEIGHTS],
            *[new_v[k] for k in WEIGHTS])
```

```python
import functools
import math

import numpy as np
import jax
import jax.numpy as jnp
from jax import lax
from jax.experimental import pallas as pl
from jax.experimental.pallas import tpu as pltpu

F32 = jnp.float32
BF16 = jnp.bfloat16
MESH = pl.DeviceIdType.MESH
HI = lax.Precision.HIGHEST

D_MODEL = 2048
DEPTH = 4
SSD_INNER = 1024
SSD_HEAD_DIM = 64
SSD_GROUPS = 2
SSD_HPG = 8
SSD_STATE = 128
SSD_CONV = 4
SSD_CHUNK = 128
SSD_CONV_DIM = 1536
SSD_GW = SSD_HPG * SSD_HEAD_DIM
MLA_HEADS = 8
MLA_NOPE = 128
MLA_ROPE = 64
MLA_QK = 192
MLA_V = 128
Q_LORA = 512
KV_LORA = 512
ROPE_THETA = 10000.0
X_HEADS = 4
X_HEAD_DIM = 128
X_INNER = 512
FFN = 5632
IN_COLS = 3664
EPS = 1e-6
ADAM_LR, ADAM_B1, ADAM_B2, ADAM_EPS, ADAM_WD, ADAM_STEP = 0.001, 0.9, 0.999, 1e-08, 0.01, 10

C_Z, C_XBC, C_QA, C_KVA, C_KR, C_DT, CAT_COLS = 0, 1024, 2560, 3072, 3584, 3712, 3840

LANES = 128
VMEM_LIMIT = 56 * 1024 * 1024
MM_VMEM_BUDGET = 40 * 1024 * 1024
MM_STEP_BYTES = 1024 * 1024

SMALL = ("attn_norm_g", "conv_b", "dt_bias", "a_log", "d_skip", "ssd_norm_g", "q_a_norm_g", "kv_a_norm_g",
         "mla_q_norm_g", "mla_k_norm_g", "xattn_norm_g", "mem_norm_g", "xq_norm_g", "xk_norm_g", "ffn_norm_g")
SMALL_RS = SMALL + ("conv_w",)
WEIGHTS = ("attn_norm_g", "w_in", "conv_w", "conv_b", "dt_bias", "a_log", "d_skip", "ssd_norm_g", "q_a_norm_g", "w_q_b",
           "kv_a_norm_g", "w_kv_b", "mla_q_norm_g", "mla_k_norm_g", "w_out", "xattn_norm_g", "mem_norm_g", "w_xq", "w_xk",
           "w_xv", "xq_norm_g", "xk_norm_g", "w_xo", "ffn_norm_g", "w_gate", "w_up", "w_down")


def _pcall(body, **kw):
    return pl.pallas_call(body, **kw)


def _params(sem):
    return pltpu.CompilerParams(dimension_semantics=sem, vmem_limit_bytes=VMEM_LIMIT)


def _tile(n, cands):
    for t in cands:
        if n % t == 0:
            return t
    return n


def _sigmoid(z):
    return 1.0 / (1.0 + jnp.exp(-z))


def _mtile(n, cap):
    if n % LANES:
        return n
    q = n // LANES
    t = LANES * max(d for d in range(1, q + 1) if q % d == 0 and LANES * d <= cap)
    return n if (t < 512 and n <= 2048) else t


def _divisor_tiles(n, cap):
    if n % LANES:
        return [n]
    q = n // LANES
    return [LANES * d for d in range(1, q + 1) if q % d == 0 and LANES * d <= cap] or [n]


def _mm_tiles(M, N, K, m_unit, n_unit, k_unit, a_item, b_item, o_item, has_add):
    best = None
    for tm in _divisor_tiles(m_unit, 2048):
        for tn in _divisor_tiles(n_unit, 2048):
            for tk in _divisor_tiles(k_unit, 2048):
                nk = K // tk
                vmem = 2 * (tm * tk * a_item + tk * tn * b_item + tm * tn * (o_item + (4 if has_add else 0)))
                vmem += tm * tn * 4 * (2 if nk > 1 else 1) + (tm * tk * 2 if a_item == 4 else 0) + (tk * tn * 2 if b_item == 4 else 0)
                if vmem > MM_VMEM_BUDGET:
                    continue
                steps = (M // tm) * (N // tn) * nk
                traffic = (M * K * a_item * (N // tn if nk > 1 else 1) + K * N * b_item * (M // tm)
                           + M * N * (o_item + (4 if has_add else 0)) + steps * MM_STEP_BYTES)
                if nk > 1:
                    traffic += steps * tm * tn * 4
                if best is None or traffic < best[0]:
                    best = (traffic, tm, tn, tk)
    assert best is not None, (M, N, K)
    return best[1:]


def mm(a, b, mode, name, add=None, out_dtype=F32, wv=None, out_wv=None, dep=None):
    kind, l = wv if wv is not None else (None, None)
    if kind == "row":
        b, kind = b[l].reshape(-1, b.shape[-1]), None
    if out_wv == "row":
        res = mm(a, b, mode, name, add=add, out_dtype=out_dtype, wv=wv if kind else None, dep=dep)
        return res.reshape(4, res.shape[0] // 4, res.shape[1])
    if kind == "col":
        _, _, r, c = b.shape
        bshape = (r, 4 * c)
    else:
        bshape = b.shape
    if mode == "nn":
        (M, K), (K2, N) = a.shape, bshape
    elif mode == "nt":
        (M, K), (N, K2) = a.shape, bshape
    else:
        (K, M), (K2, N) = a.shape, bshape
    assert K == K2, (a.shape, bshape, mode)
    m_unit = M
    n_unit = N // 4 if (out_wv == "col" or (kind == "col" and mode == "nn")) else N
    k_unit = K // 4 if (kind == "col" and mode == "nt") else K
    tm, tn, tk = _mm_tiles(M, N, K, m_unit, n_unit, k_unit, a.dtype.itemsize, b.dtype.itemsize,
                           jnp.dtype(out_dtype).itemsize, add is not None)
    nk = K // tk
    qm, qn, qk = m_unit // tm, n_unit // tn, k_unit // tk
    dn = {"nn": (((1,), (0,)), ((), ())), "nt": (((1,), (1,)), ((), ())), "tn": (((0,), (0,)), ((), ()))}[mode]
    has_add = add is not None

    def body(a_ref, b_ref, *rest):
        rest = list(rest)
        c_ref = rest.pop(0) if has_add else None
        if dep is not None:
            rest.pop(0)
        o_ref = rest.pop(0)
        d = lax.dot_general(a_ref[...].astype(BF16), b_ref[...].astype(BF16), dn, preferred_element_type=F32)

        def finish(res):
            if has_add:
                res = res + c_ref[...].astype(F32)
            o_ref[...] = res.astype(out_dtype)

        if nk == 1:
            finish(d)
            return
        acc = rest.pop(0)
        k = pl.program_id(2)

        @pl.when(k == 0)
        def _():
            acc[...] = d

        if nk > 2:
            @pl.when((k > 0) & (k < nk - 1))
            def _():
                acc[...] += d

        @pl.when(k == nk - 1)
        def _():
            finish(acc[...] + d)

    if mode == "tn":
        a_spec = pl.BlockSpec((tk, tm), lambda i, j, k: (k, i))
    else:
        a_spec = pl.BlockSpec((tm, tk), lambda i, j, k: (i, k))
    if kind is None:
        b_spec = (pl.BlockSpec((tn, tk), lambda i, j, k: (j, k)) if mode == "nt"
                  else pl.BlockSpec((tk, tn), lambda i, j, k: (k, j)))
    elif mode == "nn":
        b_spec = pl.BlockSpec((None, None, tk, tn), lambda i, j, k: (l, j // qn, k, j % qn))
    elif mode == "nt":
        b_spec = pl.BlockSpec((None, None, tn, tk), lambda i, j, k: (l, k // qk, j, k % qk))
    else:
        raise ValueError((mode, kind))
    if out_wv == "col":
        o_spec = pl.BlockSpec((None, tm, tn), lambda i, j, k: (j // qn, i, j % qn))
        out_shape = jax.ShapeDtypeStruct((4, M, N // 4), out_dtype)
    else:
        o_spec = pl.BlockSpec((tm, tn), lambda i, j, k: (i, j))
        out_shape = jax.ShapeDtypeStruct((M, N), out_dtype)
    in_specs = [a_spec, b_spec] + ([o_spec] if has_add else []) + ([pl.BlockSpec(memory_space=pl.ANY)] if dep is not None else [])
    args = (a, b) + ((add,) if has_add else ()) + ((dep,) if dep is not None else ())
    return _pcall(
        body, name=name, grid=(M // tm, N // tn, nk), in_specs=in_specs, out_specs=o_spec,
        out_shape=out_shape, scratch_shapes=[pltpu.VMEM((tm, tn), F32)] if nk > 1 else [],
        compiler_params=_params(("parallel", "parallel", "arbitrary")))(*args)


def norm_fwd(x, gain, W, shared, name, gate=None, out_dtype=BF16, dep=None):
    T, C = x.shape
    nb = C // W
    tq = _tile(T, (256, 128))
    g2 = gain.reshape(1, -1).astype(F32)
    gated = gate is not None

    def body(*refs):
        o_ref = refs[-1]
        if gated:
            x_ref, z_ref, g_ref = refs[:3]
        else:
            x_ref, g_ref = refs[:2]
        u = x_ref[...].astype(F32)
        if gated:
            z = z_ref[...]
            u = u * (z * _sigmoid(z))
        ms = jnp.mean(u * u, axis=-1, keepdims=True)
        o_ref[...] = (u * lax.rsqrt(ms + EPS) * g_ref[...]).astype(out_dtype)

    xs = pl.BlockSpec((tq, W), lambda j, i: (i, j))
    gs = pl.BlockSpec((1, W), (lambda j, i: (0, 0)) if shared else (lambda j, i: (0, j)))
    in_specs = [xs] + ([xs] if gated else []) + [gs] + ([pl.BlockSpec(memory_space=pl.ANY)] if dep is not None else [])
    args = (x,) + ((gate,) if gated else ()) + (g2,) + ((dep,) if dep is not None else ())
    return _pcall(body, name=name, grid=(nb, T // tq), in_specs=in_specs, out_specs=xs,
                  out_shape=jax.ShapeDtypeStruct((T, C), out_dtype),
                  compiler_params=_params(("parallel", "parallel")))(*args)


def norm_bwd(x, gain, dy, W, shared, name, gate=None, add=None, bf16_copy=False):
    T, C = x.shape
    nb = C // W
    tq = _tile(T, (256, 128))
    g2 = gain.reshape(1, -1).astype(F32)
    gated = gate is not None
    has_add = add is not None
    GC = W if shared else C

    def body(*refs):
        refs = list(refs)
        x_ref = refs.pop(0)
        z_ref = refs.pop(0) if gated else None
        g_ref = refs.pop(0)
        dy_ref = refs.pop(0)
        a_ref = refs.pop(0) if has_add else None
        dx_ref = refs.pop(0)
        dz_ref = refs.pop(0) if gated else None
        dg_ref = refs.pop(0)
        j, i = pl.program_id(0), pl.program_id(1)
        first = (i == 0) & (j == 0) if shared else (i == 0)

        @pl.when(first)
        def _():
            dg_ref[...] = jnp.zeros_like(dg_ref)

        xv = x_ref[...].astype(F32)
        if gated:
            z = z_ref[...]
            s = _sigmoid(z)
            u = xv * (z * s)
        else:
            u = xv
        rstd = lax.rsqrt(jnp.mean(u * u, axis=-1, keepdims=True) + EPS)
        uh = u * rstd
        dyv = dy_ref[...].astype(F32)
        dg_ref[...] += jnp.sum(dyv * uh, axis=0, keepdims=True)
        dyg = dyv * g_ref[...]
        du = rstd * (dyg - uh * jnp.mean(dyg * uh, axis=-1, keepdims=True))
        if gated:
            dx = du * (z * s)
            dz_ref[...] = du * xv * (s * (1.0 + z * (1.0 - s)))
        else:
            dx = du
        if has_add:
            dx = dx + a_ref[...]
        dx_ref[...] = dx
        if bf16_copy:
            refs.pop(0)[...] = dx.astype(BF16)

    xs = pl.BlockSpec((tq, W), lambda j, i: (i, j))
    gs = pl.BlockSpec((1, W), (lambda j, i: (0, 0)) if shared else (lambda j, i: (0, j)))
    in_specs = [xs] + ([xs] if gated else []) + [gs, xs] + ([xs] if has_add else [])
    args = (x,) + ((gate,) if gated else ()) + (g2, dy) + ((add,) if has_add else ())
    out_specs = [xs] + ([xs] if gated else []) + [gs] + ([xs] if bf16_copy else [])
    out_shape = [jax.ShapeDtypeStruct((T, C), F32)] + ([jax.ShapeDtypeStruct((T, C), F32)] if gated else []) \
        + [jax.ShapeDtypeStruct((1, GC), F32)] + ([jax.ShapeDtypeStruct((T, C), BF16)] if bf16_copy else [])
    return _pcall(body, name=name, grid=(nb, T // tq), in_specs=in_specs, out_specs=out_specs, out_shape=out_shape,
                  compiler_params=_params(("arbitrary", "arbitrary")))(*args)


def _inv_freq_row():
    inv = 1.0 / (ROPE_THETA ** (np.arange(0, MLA_ROPE, 2, dtype=np.float32) / MLA_ROPE))
    row = np.zeros((1, LANES), np.float32)
    row[0, :32] = inv
    row[0, 32:64] = inv
    return jnp.asarray(row)


def rope_tables(pos):
    T = pos.shape[0]
    tq = _tile(T, (512, 256, 128))

    def body(p_ref, f_ref, c_ref, s_ref):
        ang = p_ref[...] * f_ref[...]
        lane = lax.broadcasted_iota(jnp.int32, ang.shape, 1)
        c_ref[...] = jnp.where(lane < 64, jnp.cos(ang), 0.0)
        sn = jnp.sin(ang)
        s_ref[...] = jnp.where(lane < 32, -sn, jnp.where(lane < 64, sn, 0.0))

    ts = pl.BlockSpec((tq, LANES), lambda i: (i, 0))
    return _pcall(body, name="rope_tables", grid=(T // tq,),
                  in_specs=[pl.BlockSpec((tq, 1), lambda i: (i, 0)), pl.BlockSpec((1, LANES), lambda i: (0, 0))],
                  out_specs=[ts, ts], out_shape=[jax.ShapeDtypeStruct((T, LANES), F32)] * 2,
                  compiler_params=_params(("parallel",)))(pos, _inv_freq_row())


def _swap_halves(r):
    lane = lax.broadcasted_iota(jnp.int32, r.shape, 1)
    return jnp.where(lane < 32, pltpu.roll(r, 96, 1), pltpu.roll(r, 32, 1))


def qk_fwd(nope_arr, nope_blk0, rope_arr, rope_blk0, rope_shared, gn, gr, cos, sin, name):
    T = nope_arr.shape[0]
    tq = _tile(T, (256, 128))

    col = lambda blk: slice(blk * LANES, (blk + 1) * LANES)

    def body(n_ref, r_ref, gn_ref, gr_ref, c_ref, s_ref, o_ref):
        for h in range(MLA_HEADS):
            n = n_ref[:, col(nope_blk0(h))]
            r = r_ref[...] if rope_shared else r_ref[:, col(rope_blk0(h))]
            ms = (jnp.sum(n * n, axis=-1, keepdims=True) + jnp.sum(r * r, axis=-1, keepdims=True)) * (1.0 / MLA_QK)
            rstd = lax.rsqrt(ms + EPS)
            rn = r * rstd * gr_ref[...]
            rr = rn * c_ref[...] + _swap_halves(rn) * s_ref[...]
            o_ref[:, col(2 * h)] = (n * rstd * gn_ref[...]).astype(BF16)
            o_ref[:, col(2 * h + 1)] = rr.astype(BF16)

    wide = lambda arr: pl.BlockSpec((tq, arr.shape[1]), lambda i: (i, 0))
    rspec = pl.BlockSpec((tq, LANES), lambda i: (i, rope_blk0)) if rope_shared else wide(rope_arr)
    row = pl.BlockSpec((1, LANES), lambda i: (0, 0))
    tab = pl.BlockSpec((tq, LANES), lambda i: (i, 0))
    return _pcall(body, name=name, grid=(T // tq,),
                  in_specs=[wide(nope_arr), rspec, row, row, tab, tab],
                  out_specs=pl.BlockSpec((tq, MLA_HEADS * 2 * LANES), lambda i: (i, 0)),
                  out_shape=jax.ShapeDtypeStruct((T, MLA_HEADS * 2 * LANES), BF16),
                  compiler_params=_params(("parallel",)))(nope_arr, rope_arr, gn, gr, cos, sin)


def qk_bwd(nope_arr, nope_blk0, rope_arr, rope_blk0, rope_shared, gn, gr, cos, sin, dout, name, fill=None, fill_blk=None):
    T = nope_arr.shape[0]
    tq = _tile(T, (256, 128))
    col = lambda blk: slice(blk * LANES, (blk + 1) * LANES)

    def body(*refs):
        refs = list(refs)
        n_ref, r_ref, gn_ref, gr_ref, c_ref, s_ref, d_ref = refs[:7]
        f_ref = refs[7] if rope_shared else None
        outs = refs[8:] if rope_shared else refs[7:]
        draw_ref = outs[0]
        dsh_ref = outs[1] if rope_shared else None
        dgn_ref, dgr_ref = outs[-2:]

        @pl.when(pl.program_id(0) == 0)
        def _():
            dgn_ref[...] = jnp.zeros_like(dgn_ref)
            dgr_ref[...] = jnp.zeros_like(dgr_ref)

        dgn = jnp.zeros((1, LANES), F32)
        dgr = jnp.zeros((1, LANES), F32)
        dshared = jnp.zeros((tq, LANES), F32)
        for h in range(MLA_HEADS):
            n = n_ref[:, col(nope_blk0(h))]
            r = r_ref[...] if rope_shared else r_ref[:, col(rope_blk0(h))]
            ms = (jnp.sum(n * n, axis=-1, keepdims=True) + jnp.sum(r * r, axis=-1, keepdims=True)) * (1.0 / MLA_QK)
            rstd = lax.rsqrt(ms + EPS)
            nh = n * rstd
            rh = r * rstd
            d_n = d_ref[:, col(2 * h)].astype(F32)
            d_rr = d_ref[:, col(2 * h + 1)].astype(F32)
            d_rn = d_rr * c_ref[...] - _swap_halves(d_rr) * s_ref[...]
            dgn = dgn + jnp.sum(d_n * nh, axis=0, keepdims=True)
            dgr = dgr + jnp.sum(d_rn * rh, axis=0, keepdims=True)
            dng = d_n * gn_ref[...]
            drg = d_rn * gr_ref[...]
            mean = (jnp.sum(dng * nh, axis=-1, keepdims=True) + jnp.sum(drg * rh, axis=-1, keepdims=True)) * (1.0 / MLA_QK)
            draw_ref[:, col(nope_blk0(h))] = (rstd * (dng - nh * mean)).astype(BF16)
            drope = rstd * (drg - rh * mean)
            if rope_shared:
                dshared = dshared + drope
                draw_ref[:, col(fill_blk(h))] = f_ref[:, col(h)].astype(BF16)
            else:
                draw_ref[:, col(rope_blk0(h))] = drope.astype(BF16)
        dgn_ref[...] += dgn
        dgr_ref[...] += dgr
        if rope_shared:
            dsh_ref[...] = dshared

    wide = lambda arr: pl.BlockSpec((tq, arr.shape[1]), lambda i: (i, 0))
    rspec = pl.BlockSpec((tq, LANES), lambda i: (i, rope_blk0)) if rope_shared else wide(rope_arr)
    row = pl.BlockSpec((1, LANES), lambda i: (0, 0))
    tab = pl.BlockSpec((tq, LANES), lambda i: (i, 0))
    raw = pl.BlockSpec((tq, 2 * MLA_HEADS * LANES), lambda i: (i, 0))
    in_specs = [wide(nope_arr), rspec, row, row, tab, tab, wide(dout)] + ([wide(fill)] if rope_shared else [])
    out_specs = [raw] + ([tab] if rope_shared else []) + [row, row]
    out_shape = ([jax.ShapeDtypeStruct((T, 2 * MLA_HEADS * LANES), BF16)]
                 + ([jax.ShapeDtypeStruct((T, LANES), F32)] if rope_shared else [])
                 + [jax.ShapeDtypeStruct((1, LANES), F32)] * 2)
    args = (nope_arr, rope_arr, gn, gr, cos, sin, dout) + ((fill,) if rope_shared else ())
    return _pcall(body, name=name, grid=(T // tq,), in_specs=in_specs, out_specs=out_specs, out_shape=out_shape,
                  compiler_params=_params(("arbitrary",)))(*args)


def _scores(q, k, scale, causal, row0):
    s = lax.dot_general(q, k, (((1,), (1,)), ((), ())), preferred_element_type=F32) * scale
    if causal:
        qi = row0 + lax.broadcasted_iota(jnp.int32, s.shape, 0)
        ki = lax.broadcasted_iota(jnp.int32, s.shape, 1)
        s = jnp.where(qi >= ki, s, -1e30)
    return s


def attn_fwd(q, k, v, H, dqk, dv, scale, causal, name, v_idx=lambda h: h):
    B, S, _ = q.shape
    Sk = k.shape[1]
    tq = _tile(S, (256, 128))

    def tile(q_ref, k_ref, v_ref, o_ref, row0, kl):
        s = _scores(q_ref[0], k_ref[0, :kl], scale, causal, row0)
        p = jnp.exp(s - jnp.max(s, axis=-1, keepdims=True))
        l = jnp.sum(p, axis=-1, keepdims=True)
        o = lax.dot_general(p.astype(BF16), v_ref[0, :kl], (((1,), (0,)), ((), ())), preferred_element_type=F32)
        o_ref[0] = (o / l).astype(BF16)

    def body(q_ref, k_ref, v_ref, o_ref):
        if not causal:
            tile(q_ref, k_ref, v_ref, o_ref, 0, Sk)
            return
        for qi in range(S // tq):
            @pl.when(pl.program_id(2) == qi)
            def _(qi=qi):
                tile(q_ref, k_ref, v_ref, o_ref, qi * tq, (qi + 1) * tq)

    return _pcall(body, name=name, grid=(B, H, S // tq),
                  in_specs=[pl.BlockSpec((1, tq, dqk), lambda b, h, i: (b, i, h)),
                            pl.BlockSpec((1, Sk, dqk), lambda b, h, i: (b, 0, h)),
                            pl.BlockSpec((1, Sk, dv), lambda b, h, i: (b, 0, v_idx(h)))],
                  out_specs=pl.BlockSpec((1, tq, dv), lambda b, h, i: (b, i, h)),
                  out_shape=jax.ShapeDtypeStruct((B, S, H * dv), BF16),
                  compiler_params=_params(("parallel", "parallel", "parallel")))(q, k, v)


def attn_bwd(q, k, v, do, H, dqk, dv, scale, causal, name, v_idx=lambda h: h):
    B, S, _ = q.shape
    Sk = k.shape[1]
    tq = _tile(S, (256, 128))

    def body(q_ref, k_ref, v_ref, do_ref, dq_ref, dk_ref, dv_ref):
        i = pl.program_id(2)

        @pl.when(i == 0)
        def _():
            dk_ref[...] = jnp.zeros_like(dk_ref)
            dv_ref[...] = jnp.zeros_like(dv_ref)

        def tile(row0, kl):
            qv, kv, vv = q_ref[0], k_ref[0, :kl], v_ref[0, :kl]
            s = _scores(qv, kv, scale, causal, row0)
            p = jnp.exp(s - jnp.max(s, axis=-1, keepdims=True))
            p = p / jnp.sum(p, axis=-1, keepdims=True)
            dob = do_ref[0].astype(BF16)
            pb = p.astype(BF16)
            dv_ref[0, :kl] += lax.dot_general(pb, dob, (((0,), (0,)), ((), ())), preferred_element_type=F32)
            dp = lax.dot_general(dob, vv, (((1,), (1,)), ((), ())), preferred_element_type=F32)
            delta = jnp.sum(p * dp, axis=-1, keepdims=True)
            ds = (p * (dp - delta) * scale).astype(BF16)
            dq_ref[0] = lax.dot_general(ds, kv, (((1,), (0,)), ((), ())), preferred_element_type=F32)
            dk_ref[0, :kl] += lax.dot_general(ds, qv, (((0,), (0,)), ((), ())), preferred_element_type=F32)

        if not causal:
            tile(0, Sk)
            return
        for qi in range(S // tq):
            @pl.when(i == qi)
            def _(qi=qi):
                tile(qi * tq, (qi + 1) * tq)

    qs = pl.BlockSpec((1, tq, dqk), lambda b, h, i: (b, i, h))
    ks = pl.BlockSpec((1, Sk, dqk), lambda b, h, i: (b, 0, h))
    vs = pl.BlockSpec((1, Sk, dv), lambda b, h, i: (b, 0, h))
    vin = pl.BlockSpec((1, Sk, dv), lambda b, h, i: (b, 0, v_idx(h)))
    return _pcall(body, name=name, grid=(B, H, S // tq),
                  in_specs=[qs, ks, vin, pl.BlockSpec((1, tq, dv), lambda b, h, i: (b, i, h))],
                  out_specs=[qs, ks, vs],
                  out_shape=[jax.ShapeDtypeStruct((B, S, H * dqk), F32), jax.ShapeDtypeStruct((B, Sk, H * dqk), F32),
                             jax.ShapeDtypeStruct((B, Sk, H * dv), F32)],
                  compiler_params=_params(("parallel", "parallel", "arbitrary")))(q, k, v, do)


def _shift_down(u, j):
    if j == 0:
        return u
    row = lax.broadcasted_iota(jnp.int32, u.shape, 0)
    return jnp.where(row >= j, pltpu.roll(u, j, 0), 0.0)


def _shift_up(u, j):
    if j == 0:
        return u
    n = u.shape[0]
    row = lax.broadcasted_iota(jnp.int32, u.shape, 0)
    return jnp.where(row < n - j, pltpu.roll(u, n - j, 0), 0.0)


def conv_fwd(u, col0_blk, w8, b, name):
    B, S, _ = u.shape
    tc = 256
    nb = SSD_CONV_DIM // tc

    def body(u_ref, w_ref, b_ref, o_ref):
        uv = u_ref[0]
        pre = b_ref[...] + jnp.zeros_like(uv)
        for kk in range(SSD_CONV):
            pre = pre + w_ref[kk:kk + 1, :] * _shift_down(uv, SSD_CONV - 1 - kk)
        o_ref[0] = pre * _sigmoid(pre)

    return _pcall(body, name=name, grid=(B, nb),
                  in_specs=[pl.BlockSpec((1, S, tc), lambda bb, j: (bb, 0, col0_blk + j)),
                            pl.BlockSpec((8, tc), lambda bb, j: (0, j)), pl.BlockSpec((1, tc), lambda bb, j: (0, j))],
                  out_specs=pl.BlockSpec((1, S, tc), lambda bb, j: (bb, 0, j)),
                  out_shape=jax.ShapeDtypeStruct((B, S, SSD_CONV_DIM), F32),
                  compiler_params=_params(("parallel", "parallel")))(u, w8, b)


def conv_bwd(u, col0_blk, w8, b, dy, name):
    B, S, _ = u.shape
    tc = 256
    nb = SSD_CONV_DIM // tc

    def body(u_ref, w_ref, b_ref, dy_ref, du_ref, dw_ref, db_ref):
        bb = pl.program_id(1)

        @pl.when(bb == 0)
        def _():
            dw_ref[...] = jnp.zeros_like(dw_ref)
            db_ref[...] = jnp.zeros_like(db_ref)

        uv = u_ref[0]
        pre = b_ref[...] + jnp.zeros_like(uv)
        for kk in range(SSD_CONV):
            pre = pre + w_ref[kk:kk + 1, :] * _shift_down(uv, SSD_CONV - 1 - kk)
        s = _sigmoid(pre)
        dpre = dy_ref[0] * (s * (1.0 + pre * (1.0 - s)))
        du = jnp.zeros_like(uv)
        for kk in range(SSD_CONV):
            j = SSD_CONV - 1 - kk
            du = du + w_ref[kk:kk + 1, :] * _shift_up(dpre, j)
            dw_ref[kk:kk + 1, :] += jnp.sum(dpre * _shift_down(uv, j), axis=0, keepdims=True)
        db_ref[...] += jnp.sum(dpre, axis=0, keepdims=True)
        du_ref[0] = du

    return _pcall(body, name=name, grid=(nb, B),
                  in_specs=[pl.BlockSpec((1, S, tc), lambda j, bb: (bb, 0, col0_blk + j)),
                            pl.BlockSpec((8, tc), lambda j, bb: (0, j)), pl.BlockSpec((1, tc), lambda j, bb: (0, j)),
                            pl.BlockSpec((1, S, tc), lambda j, bb: (bb, 0, j))],
                  out_specs=[pl.BlockSpec((1, S, tc), lambda j, bb: (bb, 0, j)),
                             pl.BlockSpec((8, tc), lambda j, bb: (0, j)), pl.BlockSpec((1, tc), lambda j, bb: (0, j))],
                  out_shape=[jax.ShapeDtypeStruct((B, S, SSD_CONV_DIM), F32), jax.ShapeDtypeStruct((8, SSD_CONV_DIM), F32),
                             jax.ShapeDtypeStruct((1, SSD_CONV_DIM), F32)],
                  compiler_params=_params(("arbitrary", "arbitrary")))(u, w8, b, dy)


def _expand_mat():
    e = np.zeros((LANES, SSD_GW), np.float32)
    for h in range(SSD_HPG):
        e[h, h * SSD_HEAD_DIM:(h + 1) * SSD_HEAD_DIM] = 1.0
    return jnp.asarray(e)


def _tri_mat():
    return jnp.asarray(np.tril(np.ones((SSD_CHUNK, SSD_CHUNK), np.float32)))


def _dotf(a, b, dn=(((1,), (0,)), ((), ()))):
    return lax.dot_general(a, b, dn, precision=HI, preferred_element_type=F32)


def _dotb(a, b, dn=(((1,), (0,)), ((), ()))):
    return lax.dot_general(a.astype(BF16), b.astype(BF16), dn, preferred_element_type=F32)


NT = (((1,), (1,)), ((), ()))
TN = (((0,), (0,)), ((), ()))


def _ssd_common(dt_ref, par_ref, tri_ref, e_ref):
    dtr = dt_ref[0]
    xb = dtr + par_ref[0:1, :]
    dt = jnp.maximum(xb, 0.0) + jnp.log(1.0 + jnp.exp(-jnp.abs(xb)))
    A = -jnp.exp(par_ref[1:2, :])
    a = dt * A
    acs = _dotf(tri_ref[...], a)
    alast = acs[SSD_CHUNK - 1:SSD_CHUNK, :]
    E = e_ref[...]
    acsE = _dotf(acs, E)
    alastE = acsE[SSD_CHUNK - 1:SSD_CHUNK, :]
    return dict(xb=xb, dt=dt, A=A, acs=acs, acsT=acs.T, ealast=jnp.exp(alast), w=jnp.exp(alast - acs),
                dtE=_dotf(dt, E), eacsE=jnp.exp(acsE), wE=jnp.exp(alastE - acsE), ealastE=jnp.exp(alastE),
                DE=_dotf(par_ref[...], E)[2:3, :])


def _decay(c, e):
    diff = c["acs"][:, e:e + 1] - c["acsT"][e:e + 1, :]
    li = lax.broadcasted_iota(jnp.int32, diff.shape, 0)
    si = lax.broadcasted_iota(jnp.int32, diff.shape, 1)
    return jnp.where(li >= si, jnp.exp(jnp.minimum(diff, 0.0)), 0.0)


def ssd_fwd(xbc, dtp, par, name):
    B, S, _ = xbc.shape
    nc = S // SSD_CHUNK
    L, N, GW, P = SSD_CHUNK, SSD_STATE, SSD_GW, SSD_HEAD_DIM

    def body(x_ref, b_ref, c_ref, dt_ref, par_ref, tri_ref, e_ref, y_ref, st_ref, S_scr):
        @pl.when(pl.program_id(2) == 0)
        def _():
            S_scr[...] = jnp.zeros_like(S_scr)

        Sin = S_scr[...]
        st_ref[0, 0, 0] = Sin
        c = _ssd_common(dt_ref, par_ref.at[0], tri_ref, e_ref)
        X = x_ref[0]
        Bm = b_ref[0]
        Cm = c_ref[0]
        xdt = X * c["dtE"]
        CB = _dotb(Cm, Bm, NT)
        y_ref[0] = c["eacsE"] * _dotb(Cm, Sin) + c["DE"] * X
        for e in range(SSD_HPG):
            M = CB * _decay(c, e)
            y_ref[0, :, e * P:(e + 1) * P] += _dotb(M, xdt[:, e * P:(e + 1) * P])
        S_scr[...] = c["ealastE"] * Sin + _dotb(Bm, xdt * c["wE"], TN)

    return _pcall(
        body, name=name, grid=(SSD_GROUPS, B, nc),
        in_specs=[pl.BlockSpec((1, L, GW), lambda g, b, ci: (b, ci, g)),
                  pl.BlockSpec((1, L, N), lambda g, b, ci: (b, ci, 8 + g)),
                  pl.BlockSpec((1, L, N), lambda g, b, ci: (b, ci, 10 + g)),
                  pl.BlockSpec((1, L, LANES), lambda g, b, ci: (b, ci, g)),
                  pl.BlockSpec((1, 8, LANES), lambda g, b, ci: (g, 0, 0)),
                  pl.BlockSpec((L, L), lambda g, b, ci: (0, 0)),
                  pl.BlockSpec((LANES, GW), lambda g, b, ci: (0, 0))],
        out_specs=[pl.BlockSpec((1, L, GW), lambda g, b, ci: (b, ci, g)),
                   pl.BlockSpec((1, 1, 1, N, GW), lambda g, b, ci: (g, b, ci, 0, 0))],
        out_shape=[jax.ShapeDtypeStruct((B, S, SSD_INNER), F32), jax.ShapeDtypeStruct((SSD_GROUPS, B, nc, N, GW), F32)],
        scratch_shapes=[pltpu.VMEM((N, GW), F32)],
        compiler_params=_params(("arbitrary", "arbitrary", "arbitrary")))(xbc, xbc, xbc, dtp, par, _tri_mat(), _expand_mat())


def ssd_bwd(xbc, dtp, par, states, dy, name):
    B, S, _ = xbc.shape
    nc = S // SSD_CHUNK
    L, N, GW, P = SSD_CHUNK, SSD_STATE, SSD_GW, SSD_HEAD_DIM

    def body(x_ref, b_ref, c_ref, dt_ref, par_ref, tri_ref, e_ref, st_ref, dy_ref,
             dx_ref, db_ref, dc_ref, ddt_ref, dpar_ref, dS_scr, dxdt_scr):
        @pl.when(pl.program_id(2) == 0)
        def _():
            dS_scr[...] = jnp.zeros_like(dS_scr)

        @pl.when((pl.program_id(1) == 0) & (pl.program_id(2) == 0))
        def _():
            dpar_ref[...] = jnp.zeros_like(dpar_ref)

        c = _ssd_common(dt_ref, par_ref.at[0], tri_ref, e_ref)
        E = e_ref[...]
        red = lambda t: _dotf(t, E, NT)
        red_row = lambda t: red(jnp.broadcast_to(t, (8, GW)))[0:1, :]
        X, Bm, Cm = x_ref[0], b_ref[0], c_ref[0]
        Sin = st_ref[0, 0, 0]
        dY = dy_ref[0]
        dSo = dS_scr[...]
        xdt = X * c["dtE"]
        CB = _dotb(Cm, Bm, NT)
        dD = red_row(jnp.sum(dY * X, axis=0, keepdims=True))
        yoff = c["eacsE"] * _dotb(Cm, Sin)
        dacs = red(dY * yoff)
        dYe = dY * c["eacsE"]
        dC = _dotb(dYe, Sin, NT)
        dSin = _dotb(Cm, dYe, TN) + c["ealastE"] * dSo
        dB = _dotb(xdt * c["wE"], dSo, NT)
        dxw = _dotb(Bm, dSo)
        t = red(dxw * xdt) * c["w"]
        dacs = dacs - t
        dalast = jnp.sum(t, axis=0, keepdims=True) + c["ealast"] * red_row(jnp.sum(Sin * dSo, axis=0, keepdims=True))
        dxdt_scr[...] = dxw * c["wE"]
        dCB = jnp.zeros((L, L), F32)
        dacsT = jnp.zeros((LANES, L), F32)
        lane = lax.broadcasted_iota(jnp.int32, (1, LANES), 1)
        subl = lax.broadcasted_iota(jnp.int32, (LANES, 1), 0)
        for e in range(SSD_HPG):
            Lm = _decay(c, e)
            M = CB * Lm
            dYh = dY[:, e * P:(e + 1) * P]
            dM = _dotb(dYh, xdt[:, e * P:(e + 1) * P], NT)
            dxdt_scr[:, e * P:(e + 1) * P] += _dotb(M, dYh, TN)
            dCBe = dM * Lm
            dCB = dCB + dCBe
            Gm = dCBe * CB
            dacs = dacs + jnp.sum(Gm, axis=1, keepdims=True) * (lane == e).astype(F32)
            dacsT = dacsT - jnp.sum(Gm, axis=0, keepdims=True) * (subl == e).astype(F32)
        dC = dC + _dotb(dCB, Bm)
        dB = dB + _dotb(dCB, Cm, TN)
        dacs = dacs + dacsT.T
        row = lax.broadcasted_iota(jnp.int32, (L, 1), 0)
        dacs = dacs + jnp.where(row == L - 1, dalast, 0.0)
        da = _dotf(tri_ref[...], dacs, TN)
        dxdt = dxdt_scr[...]
        ddt = da * c["A"] + red(dxdt * X)
        dA = jnp.sum(da * c["dt"], axis=0, keepdims=True)
        ddtr = ddt * _sigmoid(c["xb"])
        dx_ref[0] = c["DE"] * dY + dxdt * c["dtE"]
        db_ref[0] = dB
        dc_ref[0] = dC
        ddt_ref[0] = ddtr
        dpar_ref[0, 0:1, :] += jnp.sum(ddtr, axis=0, keepdims=True)
        dpar_ref[0, 1:2, :] += dA * c["A"]
        dpar_ref[0, 2:3, :] += dD
        dS_scr[...] = dSin

    rc = lambda ci: nc - 1 - ci
    xspec = pl.BlockSpec((1, L, GW), lambda g, b, ci: (b, rc(ci), g))
    return _pcall(
        body, name=name, grid=(SSD_GROUPS, B, nc),
        in_specs=[xspec,
                  pl.BlockSpec((1, L, N), lambda g, b, ci: (b, rc(ci), 8 + g)),
                  pl.BlockSpec((1, L, N), lambda g, b, ci: (b, rc(ci), 10 + g)),
                  pl.BlockSpec((1, L, LANES), lambda g, b, ci: (b, rc(ci), g)),
                  pl.BlockSpec((1, 8, LANES), lambda g, b, ci: (g, 0, 0)),
                  pl.BlockSpec((L, L), lambda g, b, ci: (0, 0)),
                  pl.BlockSpec((LANES, GW), lambda g, b, ci: (0, 0)),
                  pl.BlockSpec((1, 1, 1, N, GW), lambda g, b, ci: (g, b, rc(ci), 0, 0)),
                  xspec],
        out_specs=[xspec,
                   pl.BlockSpec((1, L, N), lambda g, b, ci: (b, rc(ci), g)),
                   pl.BlockSpec((1, L, N), lambda g, b, ci: (b, rc(ci), g)),
                   pl.BlockSpec((1, L, LANES), lambda g, b, ci: (b, rc(ci), g)),
                   pl.BlockSpec((1, 8, LANES), lambda g, b, ci: (g, 0, 0))],
        out_shape=[jax.ShapeDtypeStruct((B, S, SSD_INNER), F32), jax.ShapeDtypeStruct((B, S, SSD_GROUPS * N), F32),
                   jax.ShapeDtypeStruct((B, S, SSD_GROUPS * N), F32), jax.ShapeDtypeStruct((B, S, SSD_GROUPS * LANES), F32),
                   jax.ShapeDtypeStruct((SSD_GROUPS, 8, LANES), F32)],
        scratch_shapes=[pltpu.VMEM((N, GW), F32), pltpu.VMEM((L, GW), F32)],
        compiler_params=_params(("arbitrary", "arbitrary", "arbitrary")))(
            xbc, xbc, xbc, dtp, par, _tri_mat(), _expand_mat(), states, dy)


def swiglu_fwd(g, u, name):
    T = g.shape[0]
    tq, tc = _tile(T, (512, 256, 128)), 512
    nf = FFN // tc

    def body(g_ref, u_ref, o_ref):
        gv = g_ref[...].astype(F32)
        o_ref[...] = (gv * _sigmoid(gv) * u_ref[...].astype(F32)).astype(BF16)

    lo = pl.BlockSpec((tq, tc), lambda i, j: (i, j))
    return _pcall(body, name=name, grid=(T // tq, nf), in_specs=[lo, lo], out_specs=lo,
                  out_shape=jax.ShapeDtypeStruct((T, FFN), BF16), compiler_params=_params(("parallel", "parallel")))(g, u)


def swiglu_bwd(gate, up, dact, name):
    T = gate.shape[0]
    tq, tc = _tile(T, (512, 256, 128)), 512
    nf = FFN // tc

    def body(g_ref, u_ref, d_ref, dg_ref, du_ref):
        g = g_ref[...].astype(F32)
        u = u_ref[...].astype(F32)
        d = d_ref[...]
        s = _sigmoid(g)
        dg_ref[...] = (d * u * (s * (1.0 + g * (1.0 - s)))).astype(BF16)
        du_ref[...] = (d * g * s).astype(BF16)

    lo = pl.BlockSpec((tq, tc), lambda i, j: (i, j))
    return _pcall(body, name=name, grid=(T // tq, nf), in_specs=[lo, lo, lo], out_specs=[lo, lo],
                  out_shape=[jax.ShapeDtypeStruct((T, FFN), BF16)] * 2,
                  compiler_params=_params(("parallel", "parallel")))(gate, up, dact)


def loss_head(y, target):
    T, D = y.shape
    tq = _tile(T, (256, 128))

    def body(y_ref, t_ref, dy_ref, l_ref, dyb_ref):
        @pl.when(pl.program_id(0) == 0)
        def _():
            l_ref[...] = jnp.zeros_like(l_ref)

        e = y_ref[...] - t_ref[...]
        dy = e * (1.0 / D)
        dy_ref[...] = dy
        dyb_ref[...] = dy.astype(BF16)
        l_ref[...] += 0.5 * jnp.sum(jnp.mean(e * e, axis=-1, keepdims=True))

    xs = pl.BlockSpec((tq, D), lambda i: (i, 0))
    return _pcall(body, name="loss_head", grid=(T // tq,), in_specs=[xs, xs],
                  out_specs=[xs, pl.BlockSpec((8, LANES), lambda i: (0, 0)), xs],
                  out_shape=[jax.ShapeDtypeStruct((T, D), F32), jax.ShapeDtypeStruct((8, LANES), F32),
                             jax.ShapeDtypeStruct((T, D), BF16)],
                  compiler_params=_params(("arbitrary",)))(y, target)


def sum_arrays(arrs, out_dtype, name):
    R, C = arrs[0].shape
    tr = _tile(R, (512, 256, 128, 64, 32, 16, 8))
    n = len(arrs)

    def body(*refs):
        acc = refs[0][...].astype(F32)
        for r in refs[1:n]:
            acc = acc + r[...].astype(F32)
        refs[n][...] = acc.astype(out_dtype)

    xs = pl.BlockSpec((tr, C), lambda i: (i, 0))
    return _pcall(body, name=name, grid=(R // tr,), in_specs=[xs] * n, out_specs=xs,
                  out_shape=jax.ShapeDtypeStruct((R, C), out_dtype), compiler_params=_params(("parallel",)))(*arrs)


def adamw(w, g, m, v, name):
    R, C = w.shape
    tr = _tile(R, (256, 128, 64, 32, 16, 8))
    c1 = 1.0 - ADAM_B1 ** ADAM_STEP
    c2 = 1.0 - ADAM_B2 ** ADAM_STEP

    def body(w_ref, g_ref, m_ref, v_ref, d_ref, nm_ref, nv_ref):
        gv = g_ref[...]
        mn = ADAM_B1 * m_ref[...] + (1.0 - ADAM_B1) * gv
        vn = ADAM_B2 * v_ref[...] + (1.0 - ADAM_B2) * (gv * gv)
        nm_ref[...] = mn
        nv_ref[...] = vn
        d_ref[...] = -ADAM_LR * ((mn / c1) / (jnp.sqrt(vn / c2) + ADAM_EPS) + ADAM_WD * w_ref[...])

    xs = pl.BlockSpec((tr, C), lambda i: (i, 0))
    return _pcall(body, name=name, grid=(R // tr,), in_specs=[xs] * 4, out_specs=[xs] * 3,
                  out_shape=[jax.ShapeDtypeStruct((R, C), F32)] * 3, compiler_params=_params(("parallel",)))(w, g, m, v)


def adamw_layer(w, g, m, v, l, prev, name):
    L, r, c = w.shape
    tr = _tile(r, (256, 128, 64, 32, 16, 8))
    c1 = 1.0 - ADAM_B1 ** ADAM_STEP
    c2 = 1.0 - ADAM_B2 ** ADAM_STEP
    chained = prev is not None

    def body(w_ref, g_ref, m_ref, v_ref, *rest):
        go_ref, d_ref, nm_ref, nv_ref = rest[-4:]
        gv = g_ref[...]
        mn = ADAM_B1 * m_ref[...] + (1.0 - ADAM_B1) * gv
        vn = ADAM_B2 * v_ref[...] + (1.0 - ADAM_B2) * (gv * gv)
        go_ref[...] = gv
        nm_ref[...] = mn
        nv_ref[...] = vn
        d_ref[...] = -ADAM_LR * ((mn / c1) / (jnp.sqrt(vn / c2) + ADAM_EPS) + ADAM_WD * w_ref[...])

    ls = pl.BlockSpec((None, tr, c), lambda i: (l, i, 0))
    gs = pl.BlockSpec((tr, c), lambda i: (i, 0))
    anys = [pl.BlockSpec(memory_space=pl.ANY)] * 4 if chained else []
    return _pcall(body, name=name, grid=(r // tr,), in_specs=[ls, gs, ls, ls] + anys, out_specs=[ls] * 4,
                  out_shape=[jax.ShapeDtypeStruct((L, r, c), F32)] * 4,
                  input_output_aliases={4: 0, 5: 1, 6: 2, 7: 3} if chained else {},
                  compiler_params=_params(("parallel",)))(w, g, m, v, *(prev if chained else ()))


def add_pair(g, got, name):
    _, r, c = g.shape
    h = r // 2

    def body(c_ref, g_ref, o_ref, out_ref):
        del c_ref
        out_ref[...] = (g_ref[...].astype(F32) + o_ref[...].astype(F32)).astype(BF16)

    spec = pltpu.PrefetchScalarGridSpec(
        num_scalar_prefetch=1, grid=(4,),
        in_specs=[pl.BlockSpec((None, h, c), lambda k, cs: (k, cs[0], 0)), pl.BlockSpec((None, h, c), lambda k, cs: (k, 0, 0))],
        out_specs=pl.BlockSpec((None, h, c), lambda k, cs: (k, 0, 0)))
    return _pcall(body, name=name, grid_spec=spec, out_shape=jax.ShapeDtypeStruct((4, h, c), BF16),
                  compiler_params=_params(("arbitrary",)))(lax.axis_index("c").reshape(1).astype(jnp.int32), g, got)


def add_chips(p, b, name):
    _, h, c = p.shape
    tr = _tile(h, (256, 128, 64, 32, 16, 8))

    def body(me_ref, p_ref, b0, b1, b2, b3, out_ref):
        me = me_ref[0]
        acc = jnp.zeros(out_ref.shape, F32)
        for k, bk in enumerate((b0, b1, b2, b3)):
            acc = acc + jnp.where(me == k, p_ref[...], bk[...]).astype(F32)
        out_ref[...] = acc

    slot = lambda k: pl.BlockSpec((None, tr, c), lambda i, ms: (jnp.where(ms[0] == k, (k + 1) % 4, k), i, 0))
    spec = pltpu.PrefetchScalarGridSpec(
        num_scalar_prefetch=1, grid=(h // tr,),
        in_specs=[pl.BlockSpec((None, tr, c), lambda i, ms: (ms[0], i, 0)), slot(0), slot(1), slot(2), slot(3)],
        out_specs=pl.BlockSpec((tr, c), lambda i, ms: (i, 0)))
    me = (2 * lax.axis_index("x") + lax.axis_index("y")).reshape(1).astype(jnp.int32)
    return _pcall(body, name=name, grid_spec=spec, out_shape=jax.ShapeDtypeStruct((h, c), F32),
                  compiler_params=_params(("arbitrary",)))(me, p, b, b, b, b)


ANY = pl.BlockSpec(memory_space=pl.ANY)


def _place():
    x, y, c = lax.axis_index("x"), lax.axis_index("y"), lax.axis_index("c")
    chips = [(1 - x, y), (x, 1 - y), (1 - x, 1 - y)]
    return x, y, c, chips


def _rcopy(src, dst, ssem, rsem, dev):
    return pltpu.make_async_remote_copy(src_ref=src, dst_ref=dst, send_sem=ssem, recv_sem=rsem, device_id=dev,
                                        device_id_type=MESH)


def _half(c, h):
    return pl.ds(pl.multiple_of(c * h, 64), h)


def _me_chip():
    return 2 * lax.axis_index("x") + lax.axis_index("y")


def all_gather_weights(shards, name):
    n = len(shards)
    L = shards[0].shape[0]
    hl = L // 2

    def body(*refs):
        srcs, outs, (ssem, rsem) = refs[:n], refs[n:2 * n], refs[2 * n:]
        x, y, c, chips = _place()
        me = 2 * x + y
        sib = (x, y, 1 - c)
        mine, other = pl.ds(c * hl, hl), pl.ds((1 - c) * hl, hl)
        sends = []
        for w in range(n):
            for j, (px, py) in enumerate(chips):
                cp = _rcopy(srcs[w].at[mine], outs[w].at[mine, me], ssem.at[w, j], rsem.at[w, j], (px, py, c))
                cp.start()
                sends.append(cp)
        for j, (px, py) in enumerate(chips):
            slot = 2 * px + py
            for w in range(n):
                _rcopy(srcs[w].at[mine], outs[w].at[mine, slot], ssem.at[w, j], rsem.at[w, j], (px, py, c)).wait_recv()
                cp = _rcopy(outs[w].at[mine, slot], outs[w].at[mine, slot], ssem.at[w, 3 + j], rsem.at[w, 3 + j], sib)
                cp.start()
                sends.append(cp)
        for j, (px, py) in enumerate(chips):
            slot = 2 * px + py
            for w in range(n):
                _rcopy(outs[w].at[other, slot], outs[w].at[other, slot], ssem.at[w, 3 + j], rsem.at[w, 3 + j], sib).wait_recv()
        for cp in sends:
            cp.wait_send()

    outs = _pcall(body, name=name, in_specs=[ANY] * n, out_specs=[ANY] * n,
                  out_shape=[jax.ShapeDtypeStruct((L, 4) + s.shape[1:], s.dtype) for s in shards],
                  scratch_shapes=[pltpu.SemaphoreType.DMA((n, 6)), pltpu.SemaphoreType.DMA((n, 6))])(*shards)
    me = _me_chip()
    return [lax.dynamic_update_slice(o, s[:, None], (0, me, 0, 0)) for o, s in zip(outs, shards)]


HBM = pl.BlockSpec(memory_space=pltpu.HBM)
SEM = pl.BlockSpec(memory_space=pltpu.SEMAPHORE)
EFFECT = pltpu.SideEffectType.DATAFLOW_SIDE_EFFECTING


def _ici_copy(kind, src, land, x, y, c, px, py, ssem, rsem, receiving):
    me, peer = 2 * x + y, 2 * px + py
    if kind == "gather":
        rows = _half(c, src.shape[0] // 2)
        s, d = src.at[rows], land.at[peer if receiving else me, rows]
    else:
        s, d = src.at[me if receiving else peer], land.at[peer if receiving else me]
    return _rcopy(s, d, ssem, rsem, (px, py, c))


def ici_start(kind, srcs, land_shapes, name):
    n = len(srcs)

    m = 3 * n

    def body(*refs):
        src, land, token = refs[:n], refs[n:2 * n], refs[-1]
        ssem, rsem = refs[2 * n:2 * n + m], refs[2 * n + m:2 * n + 2 * m]
        x, y, c, chips = _place()
        for w in range(n):
            for j, (px, py) in enumerate(chips):
                _ici_copy(kind, src[w], land[w], x, y, c, px, py, ssem[3 * w + j], rsem[3 * w + j], False).start()
        token[...] = jnp.zeros_like(token)

    lands = [pltpu.with_memory_space_constraint(lax.empty(shp, s.dtype), pltpu.HBM) for shp, s in zip(land_shapes, srcs)]
    srcs = [pltpu.with_memory_space_constraint(s, pltpu.HBM) for s in srcs]
    outs = _pcall(
        body, name=name,
        out_shape=(*[pltpu.SemaphoreType.DMA(())] * (2 * m),
                   *[pltpu.HBM(s.shape, s.dtype) for s in srcs], *[pltpu.HBM(z.shape, z.dtype) for z in lands],
                   jax.ShapeDtypeStruct((8, LANES), F32)),
        in_specs=[HBM] * (2 * n),
        out_specs=(*([SEM] * (2 * m)), *([HBM] * (2 * n)), pl.BlockSpec(memory_space=pltpu.VMEM)),
        input_output_aliases={i: 2 * m + i for i in range(2 * n)},
        compiler_params=pltpu.CompilerParams(has_side_effects=EFFECT))(*srcs, *lands)
    return dict(kind=kind, ssem=list(outs[:m]), rsem=list(outs[m:2 * m]), srcs=list(outs[2 * m:2 * m + n]),
                lands=list(outs[2 * m + n:2 * m + 2 * n])), outs[-1]


def ici_wait(state, after, name):
    kind, n = state["kind"], len(state["srcs"])

    m = 3 * n

    def body(*refs):
        src, land = refs[:n], refs[n:2 * n]
        ssem, rsem = refs[2 * n:2 * n + m], refs[2 * n + m:2 * n + 2 * m]
        x, y, c, chips = _place()
        for w in range(n):
            for j, (px, py) in enumerate(chips):
                _ici_copy(kind, src[w], land[w], x, y, c, px, py, ssem[3 * w + j], rsem[3 * w + j], False).wait_send()
                _ici_copy(kind, src[w], land[w], x, y, c, px, py, ssem[3 * w + j], rsem[3 * w + j], True).wait_recv()

    outs = _pcall(
        body, name=name,
        out_shape=tuple(pltpu.HBM(a.shape, a.dtype) for a in state["srcs"] + state["lands"]),
        in_specs=[HBM] * (2 * n) + [SEM] * (2 * m) + [ANY], out_specs=tuple([HBM] * (2 * n)),
        input_output_aliases={i: i for i in range(2 * n)},
        compiler_params=pltpu.CompilerParams(has_side_effects=EFFECT))(
            *state["srcs"], *state["lands"], *state["ssem"], *state["rsem"], after)
    return list(outs[:n]), list(outs[n:])


def forward_pair(lands, name):
    n = len(lands)

    def body(*refs):
        land, (ssem, rsem) = refs[:n], refs[2 * n:]
        x, y, c, chips = _place()
        sib = (x, y, 1 - c)
        cps = []
        for w in range(n):
            h = land[w].shape[1] // 2
            for j, (px, py) in enumerate(chips):
                part = land[w].at[2 * px + py, _half(c, h)]
                cp = _rcopy(part, part, ssem.at[w, j], rsem.at[w, j], sib)
                cp.start()
                cps.append(cp)
        for w in range(n):
            h = land[w].shape[1] // 2
            for j, (px, py) in enumerate(chips):
                part = land[w].at[2 * px + py, _half(1 - c, h)]
                _rcopy(part, part, ssem.at[w, j], rsem.at[w, j], sib).wait_recv()
        for cp in cps:
            cp.wait_send()

    return _pcall(body, name=name, in_specs=[ANY] * n, out_specs=[ANY] * n,
                  out_shape=[jax.ShapeDtypeStruct(z.shape, z.dtype) for z in lands],
                  input_output_aliases={i: i for i in range(n)},
                  scratch_shapes=[pltpu.SemaphoreType.DMA((n, 3)), pltpu.SemaphoreType.DMA((n, 3))])(*lands)


def exchange_pair(gs, name):
    n = len(gs)

    def body(*refs):
        srcs, outs, (ssem, rsem) = refs[:n], refs[n:2 * n], refs[2 * n:]
        x, y, c, _ = _place()
        cps = []
        for w in range(n):
            h = srcs[w].shape[1] // 2
            cp = _rcopy(srcs[w].at[:, _half(1 - c, h)], outs[w], ssem.at[w], rsem.at[w], (x, y, 1 - c))
            cp.start()
            cps.append(cp)
        for cp in cps:
            cp.wait()

    return _pcall(body, name=name, in_specs=[ANY] * n, out_specs=[ANY] * n,
                  out_shape=[jax.ShapeDtypeStruct((4, g.shape[1] // 2, g.shape[2]), g.dtype) for g in gs],
                  scratch_shapes=[pltpu.SemaphoreType.DMA((n,)), pltpu.SemaphoreType.DMA((n,))])(*gs)


def scatter_chips(ps, name):
    n = len(ps)

    def body(*refs):
        srcs, outs, (ssem, rsem) = refs[:n], refs[n:2 * n], refs[2 * n:]
        x, y, c, chips = _place()
        me = 2 * x + y
        sends = []
        for w in range(n):
            for j, (px, py) in enumerate(chips):
                cp = _rcopy(srcs[w].at[2 * px + py], outs[w].at[me], ssem.at[w, j], rsem.at[w, j], (px, py, c))
                cp.start()
                sends.append(cp)
        for w in range(n):
            for j, (px, py) in enumerate(chips):
                _rcopy(srcs[w].at[me], outs[w].at[2 * px + py], ssem.at[w, j], rsem.at[w, j], (px, py, c)).wait_recv()
        for cp in sends:
            cp.wait_send()

    return _pcall(body, name=name, in_specs=[ANY] * n, out_specs=[ANY] * n,
                  out_shape=[jax.ShapeDtypeStruct(p.shape, p.dtype) for p in ps],
                  scratch_shapes=[pltpu.SemaphoreType.DMA((n, 3)), pltpu.SemaphoreType.DMA((n, 3))])(*ps)


def join_pair(rs, name):
    n = len(rs)

    def body(*refs):
        srcs, outs, (ssem, rsem) = refs[:n], refs[n:2 * n], refs[2 * n:]
        x, y, c, _ = _place()
        cps = []
        for w in range(n):
            h = srcs[w].shape[0]
            cp = _rcopy(srcs[w], outs[w].at[_half(c, h)], ssem.at[w], rsem.at[w], (x, y, 1 - c))
            cp.start()
            cps.append(cp)
        for w in range(n):
            h = srcs[w].shape[0]
            _rcopy(srcs[w], outs[w].at[_half(1 - c, h)], ssem.at[w], rsem.at[w], (x, y, 1 - c)).wait_recv()
        for cp in cps:
            cp.wait_send()

    outs = _pcall(body, name=name, in_specs=[ANY] * n, out_specs=[ANY] * n,
                  out_shape=[jax.ShapeDtypeStruct((2 * r.shape[0], r.shape[1]), r.dtype) for r in rs],
                  scratch_shapes=[pltpu.SemaphoreType.DMA((n,)), pltpu.SemaphoreType.DMA((n,))])(*rs)
    c = lax.axis_index("c")
    return [lax.dynamic_update_slice(o, r, (c * r.shape[0], 0)) for o, r in zip(outs, rs)]


def gather_all_devices(v, name):
    R, C = v.shape

    def body(src, out, ssem, rsem, lsem):
        x, y, c, _ = _place()
        me = 4 * x + 2 * y + c
        local = pltpu.make_async_copy(src, out.at[me], lsem)
        local.start()
        sends = []
        for j in range(1, 8):
            dx, dy, dc = (j >> 2) & 1, (j >> 1) & 1, j & 1
            px, py, pc = (1 - x if dx else x), (1 - y if dy else y), (1 - c if dc else c)
            cp = _rcopy(src, out.at[me], ssem.at[j - 1], rsem.at[j - 1], (px, py, pc))
            cp.start()
            sends.append(cp)
        for j in range(1, 8):
            dx, dy, dc = (j >> 2) & 1, (j >> 1) & 1, j & 1
            px, py, pc = (1 - x if dx else x), (1 - y if dy else y), (1 - c if dc else c)
            _rcopy(src, out.at[4 * px + 2 * py + pc], ssem.at[j - 1], rsem.at[j - 1], (px, py, pc)).wait_recv()
        for cp in sends:
            cp.wait_send()
        local.wait()

    return _pcall(body, name=name, in_specs=[ANY], out_specs=ANY, out_shape=jax.ShapeDtypeStruct((8, R, C), v.dtype),
                  scratch_shapes=[pltpu.SemaphoreType.DMA((7,)), pltpu.SemaphoreType.DMA((7,)), pltpu.SemaphoreType.DMA(())])(v)


def reduce_scatter_begin(gw, tag):
    names = list(gw)
    gs = [gw[k] for k in names]
    got = exchange_pair(gs, "rs_pair_" + tag)
    ps = [add_pair(g, o, "rs_add_pair_%s_%s" % (k, tag)) for k, g, o in zip(names, gs, got)]
    state, token = ici_start("scatter", ps, [p.shape for p in ps], "rs_chips_start_" + tag)
    state["names"] = names
    return state, token


def reduce_scatter_end(state, after, tag):
    names = state["names"]
    ps, bs = ici_wait(state, after, "rs_chips_wait_" + tag)
    rs = [add_chips(p, b, "rs_add_chips_%s_%s" % (k, tag)) for k, p, b in zip(names, ps, bs)]
    return dict(zip(names, join_pair(rs, "rs_join_" + tag)))


GATHERED = ("w_in", "w_q_b", "w_kv_b", "w_out", "w_xq", "w_xk", "w_xv", "w_xo", "w_gate", "w_up", "w_down")
KIND = {"w_q_b": "col", "w_kv_b": "col", "w_out": "row", "w_xq": "row", "w_xk": "row", "w_xv": "row", "w_xo": "col",
        "w_gate": "col", "w_up": "col", "w_down": "row"}


def wq_to_storage(w):
    lead = w.shape[:-1]
    w = w.reshape(lead + (2, MLA_QK))
    rope = jnp.pad(w[..., MLA_NOPE:], [(0, 0)] * (w.ndim - 1) + [(0, LANES - MLA_ROPE)])
    return jnp.concatenate([w[..., :MLA_NOPE].reshape(lead + (2 * LANES,)), rope.reshape(lead + (2 * LANES,))], axis=-1)


def wq_from_storage(g):
    lead = g.shape[:-1]
    nope = g[..., :2 * LANES].reshape(lead + (2, LANES))
    rope = g[..., 2 * LANES:].reshape(lead + (2, LANES))[..., :MLA_ROPE]
    return jnp.concatenate([nope, rope], axis=-1).reshape(lead + (2 * MLA_QK,))


def wcat_from_storage(w_in_l):
    w = w_in_l.transpose(1, 0, 2).reshape(D_MODEL, IN_COLS)
    zc = lambda n: jnp.zeros((D_MODEL, n), w.dtype)
    return jnp.concatenate([w[:, :2560], w[:, 2576:], zc(64), w[:, 2560:2576], zc(112)], axis=1)


def w_in_grad_storage(dwcat):
    g = jnp.concatenate([dwcat[:, :2560], dwcat[:, C_DT:C_DT + 16], dwcat[:, 2560:3648]], axis=1)
    return g.reshape(D_MODEL, 4, IN_COLS // 4).transpose(1, 0, 2)


def _ssd_par(sm):
    rows = jnp.stack([sm["dt_bias"], sm["a_log"], sm["d_skip"]]).reshape(3, SSD_GROUPS, SSD_HPG).transpose(1, 0, 2)
    return jnp.pad(rows, ((0, 0), (0, 5), (0, LANES - SSD_HPG)))


def _dt_pad(proj, B, S):
    dt = proj[:, C_DT:C_DT + 16].reshape(B, S, SSD_GROUPS, SSD_HPG)
    return jnp.pad(dt, ((0, 0), (0, 0), (0, 0), (0, LANES - SSD_HPG))).reshape(B, S, SSD_GROUPS * LANES)


def _mla_gains(g):
    return g[:MLA_NOPE].reshape(1, LANES), jnp.pad(g[MLA_NOPE:], (0, LANES - MLA_ROPE)).reshape(1, LANES)


_Q_NOPE = lambda h: 4 * (h // 2) + h % 2
_Q_ROPE = lambda h: 4 * (h // 2) + 2 + h % 2
_K_NOPE = lambda h: 2 * h
_V_BLK = lambda h: 2 * h + 1


def layer_fwd(x, memf, cos, sin, G, kw, sm, B, S, l, dep=None):
    T = x.shape[0]
    n = lambda s: "%s_l%d" % (s, l)
    wv = lambda k: (KIND[k], 0)
    sv = dict(x0=x)
    h1 = norm_fwd(x, sm["attn_norm_g"], D_MODEL, False, n("attn_norm"), dep=dep)
    proj = mm(h1, kw["wcat"], "nn", n("in_proj"))
    sv.update(h1=h1, proj=proj)
    proj3 = proj.reshape(B, S, CAT_COLS)
    xbc = conv_fwd(proj3, C_XBC // 256, kw["conv_w"], sm["conv_b"].reshape(1, -1), n("conv"))
    dtp = _dt_pad(proj, B, S)
    par = _ssd_par(sm)
    y_raw, states = ssd_fwd(xbc, dtp, par, n("ssd"))
    z = proj[:, C_Z:C_Z + SSD_INNER]
    y_ssd = norm_fwd(y_raw.reshape(T, SSD_INNER), sm["ssd_norm_g"], SSD_INNER // SSD_GROUPS, False, n("ssd_norm"), gate=z)
    sv.update(xbc=xbc, states=states, y_raw=y_raw)
    qn = norm_fwd(proj[:, C_QA:C_QA + Q_LORA], sm["q_a_norm_g"], Q_LORA, False, n("q_a_norm"))
    kvn = norm_fwd(proj[:, C_KVA:C_KVA + KV_LORA], sm["kv_a_norm_g"], KV_LORA, False, n("kv_a_norm"))
    q_raw = mm(qn, G["w_q_b"], "nn", n("q_b"), wv=wv("w_q_b"))
    kv_raw = mm(kvn, G["w_kv_b"], "nn", n("kv_b"), wv=wv("w_kv_b"))
    gqn, gqr = _mla_gains(sm["mla_q_norm_g"])
    gkn, gkr = _mla_gains(sm["mla_k_norm_g"])
    qh = qk_fwd(q_raw, _Q_NOPE, q_raw, _Q_ROPE, False, gqn, gqr, cos, sin, n("q_norm_rope"))
    kh = qk_fwd(kv_raw, _K_NOPE, proj, C_KR // LANES, True, gkn, gkr, cos, sin, n("k_norm_rope"))
    vv = kv_raw.astype(BF16)
    y_mla = attn_fwd(qh.reshape(B, S, -1), kh.reshape(B, S, -1), vv.reshape(B, S, -1), MLA_HEADS, 2 * LANES, MLA_V,
                     MLA_QK ** -0.5, True, n("mla_attn"), v_idx=_V_BLK)
    ycat = jnp.concatenate([y_ssd, y_mla.reshape(T, -1)], axis=1)
    x1 = mm(ycat, G["w_out"], "nn", n("out_proj"), add=x, wv=wv("w_out"))
    sv.update(qn=qn, kvn=kvn, q_raw=q_raw, kv_raw=kv_raw, qh=qh, kh=kh, vv=vv, ycat=ycat, x1=x1)
    h2 = norm_fwd(x1, sm["xattn_norm_g"], D_MODEL, False, n("xattn_norm"))
    mn = norm_fwd(memf, sm["mem_norm_g"], D_MODEL, False, n("mem_norm"))
    xq = mm(h2, G["w_xq"], "nn", n("xq"), wv=wv("w_xq"))
    xk = mm(mn, G["w_xk"], "nn", n("xk"), wv=wv("w_xk"))
    xv = mm(mn, G["w_xv"], "nn", n("xv"), wv=wv("w_xv"), out_dtype=BF16)
    xqn = norm_fwd(xq, sm["xq_norm_g"], X_HEAD_DIM, True, n("xq_norm"))
    xkn = norm_fwd(xk, sm["xk_norm_g"], X_HEAD_DIM, True, n("xk_norm"))
    ML = memf.shape[0] // B
    xo = attn_fwd(xqn.reshape(B, S, -1), xkn.reshape(B, ML, -1), xv.reshape(B, ML, -1), X_HEADS, X_HEAD_DIM, X_HEAD_DIM,
                  X_HEAD_DIM ** -0.5, False, n("x_attn"))
    x2 = mm(xo.reshape(T, -1), G["w_xo"], "nn", n("xo_proj"), add=x1, wv=wv("w_xo"))
    sv.update(h2=h2, mn=mn, xq=xq, xk=xk, xqn=xqn, xkn=xkn, xv=xv, xo=xo, x2=x2)
    h3 = norm_fwd(x2, sm["ffn_norm_g"], D_MODEL, False, n("ffn_norm"))
    gate = mm(h3, G["w_gate"], "nn", n("gate"), out_dtype=BF16, wv=wv("w_gate"))
    up = mm(h3, G["w_up"], "nn", n("up"), out_dtype=BF16, wv=wv("w_up"))
    act = swiglu_fwd(gate, up, n("swiglu"))
    x3 = mm(act, G["w_down"], "nn", n("down"), add=x2, wv=wv("w_down"))
    sv.update(h3=h3, gate=gate, up=up, act=act)
    return x3, sv


def layer_bwd(dx3, dx3b, sv, memf, cos, sin, G, kw, sm, B, S, l, dep=None):
    T = dx3.shape[0]
    n = lambda s: "%s_bwd_l%d" % (s, l)
    wv = lambda k: (KIND[k], 0)
    dw = lambda k, a, b: mm(a, b, "tn", n(k + "_dw"), out_dtype=BF16, out_wv=KIND[k])
    gw, gs = {}, {}
    dact = mm(dx3b, G["w_down"], "nt", n("down_dx"), wv=wv("w_down"), dep=dep)
    gw["w_down"] = dw("w_down", sv["act"], dx3b)
    dgate, dup = swiglu_bwd(sv["gate"], sv["up"], dact, n("swiglu"))
    dh3 = mm(dgate, G["w_gate"], "nt", n("gate_dx"), wv=wv("w_gate"))
    dh3 = mm(dup, G["w_up"], "nt", n("up_dx"), wv=wv("w_up"), add=dh3)
    gw["w_gate"] = dw("w_gate", sv["h3"], dgate)
    gw["w_up"] = dw("w_up", sv["h3"], dup)
    dx2, gs["ffn_norm_g"], dx2b = norm_bwd(sv["x2"], sm["ffn_norm_g"], dh3, D_MODEL, False, n("ffn_norm"), add=dx3,
                                           bf16_copy=True)
    dxo = mm(dx2b, G["w_xo"], "nt", n("xo_dx"), wv=wv("w_xo"))
    gw["w_xo"] = dw("w_xo", sv["xo"].reshape(T, -1), dx2b)
    ML = memf.shape[0] // B
    dxqn, dxkn, dxv = attn_bwd(sv["xqn"].reshape(B, S, -1), sv["xkn"].reshape(B, ML, -1), sv["xv"].reshape(B, ML, -1),
                               dxo.reshape(B, S, -1), X_HEADS, X_HEAD_DIM, X_HEAD_DIM, X_HEAD_DIM ** -0.5, False, n("x_attn"))
    dxq, gs["xq_norm_g"] = norm_bwd(sv["xq"], sm["xq_norm_g"], dxqn.reshape(T, -1), X_HEAD_DIM, True, n("xq_norm"))
    dxk, gs["xk_norm_g"] = norm_bwd(sv["xk"], sm["xk_norm_g"], dxkn.reshape(B * ML, -1), X_HEAD_DIM, True, n("xk_norm"))
    dxv = dxv.reshape(B * ML, -1)
    dmn = mm(dxk, G["w_xk"], "nt", n("xk_dx"), wv=wv("w_xk"))
    dmn = mm(dxv, G["w_xv"], "nt", n("xv_dx"), wv=wv("w_xv"), add=dmn)
    gw["w_xk"] = dw("w_xk", sv["mn"], dxk)
    gw["w_xv"] = dw("w_xv", sv["mn"], dxv)
    _, gs["mem_norm_g"] = norm_bwd(memf, sm["mem_norm_g"], dmn, D_MODEL, False, n("mem_norm"))
    dh2 = mm(dxq, G["w_xq"], "nt", n("xq_dx"), wv=wv("w_xq"))
    gw["w_xq"] = dw("w_xq", sv["h2"], dxq)
    dx1, gs["xattn_norm_g"], dx1b = norm_bwd(sv["x1"], sm["xattn_norm_g"], dh2, D_MODEL, False, n("xattn_norm"), add=dx2,
                                             bf16_copy=True)
    dycat = mm(dx1b, G["w_out"], "nt", n("out_dx"), wv=wv("w_out"))
    gw["w_out"] = dw("w_out", sv["ycat"], dx1b)
    dqh, dkh, dvv = attn_bwd(sv["qh"].reshape(B, S, -1), sv["kh"].reshape(B, S, -1), sv["vv"].reshape(B, S, -1),
                             dycat[:, SSD_INNER:].reshape(B, S, -1), MLA_HEADS, 2 * LANES, MLA_V, MLA_QK ** -0.5, True,
                             n("mla_attn"), v_idx=_V_BLK)
    gqn, gqr = _mla_gains(sm["mla_q_norm_g"])
    gkn, gkr = _mla_gains(sm["mla_k_norm_g"])
    dq_raw, dgqn, dgqr = qk_bwd(sv["q_raw"], _Q_NOPE, sv["q_raw"], _Q_ROPE, False, gqn, gqr, cos, sin,
                                dqh.reshape(T, -1), n("q_norm_rope"))
    dkv_raw, dk_rope, dgkn, dgkr = qk_bwd(sv["kv_raw"], _K_NOPE, sv["proj"], C_KR // LANES, True, gkn, gkr, cos, sin,
                                          dkh.reshape(T, -1), n("k_norm_rope"), fill=dvv.reshape(T, -1), fill_blk=_V_BLK)
    gs["mla_q_norm_g"] = jnp.concatenate([dgqn[0], dgqr[0, :MLA_ROPE]])
    gs["mla_k_norm_g"] = jnp.concatenate([dgkn[0], dgkr[0, :MLA_ROPE]])
    dqn = mm(dq_raw, G["w_q_b"], "nt", n("q_b_dx"), wv=wv("w_q_b"))
    gw["w_q_b"] = dw("w_q_b", sv["qn"], dq_raw)
    dkvn = mm(dkv_raw, G["w_kv_b"], "nt", n("kv_b_dx"), wv=wv("w_kv_b"))
    gw["w_kv_b"] = dw("w_kv_b", sv["kvn"], dkv_raw)
    proj = sv["proj"]
    dq_a, gs["q_a_norm_g"] = norm_bwd(proj[:, C_QA:C_QA + Q_LORA], sm["q_a_norm_g"], dqn, Q_LORA, False, n("q_a_norm"))
    dkv_a, gs["kv_a_norm_g"] = norm_bwd(proj[:, C_KVA:C_KVA + KV_LORA], sm["kv_a_norm_g"], dkvn, KV_LORA, False,
                                        n("kv_a_norm"))
    z = proj[:, C_Z:C_Z + SSD_INNER]
    dy_raw, dz, gs["ssd_norm_g"] = norm_bwd(sv["y_raw"].reshape(T, -1), sm["ssd_norm_g"], dycat[:, :SSD_INNER],
                                            SSD_INNER // SSD_GROUPS, False, n("ssd_norm"), gate=z)
    dtp = _dt_pad(proj, B, S)
    par = _ssd_par(sm)
    dxs, dBm, dCm, ddt, dpar = ssd_bwd(sv["xbc"], dtp, par, sv["states"], dy_raw.reshape(B, S, -1), n("ssd"))
    gs["dt_bias"] = dpar[:, 0, :SSD_HPG].reshape(-1)
    gs["a_log"] = dpar[:, 1, :SSD_HPG].reshape(-1)
    gs["d_skip"] = dpar[:, 2, :SSD_HPG].reshape(-1)
    dxbc_act = jnp.concatenate([dxs, dBm, dCm], axis=2)
    dxbc, dconv_w, dconv_b = conv_bwd(proj.reshape(B, S, -1), C_XBC // 256, kw["conv_w"], sm["conv_b"].reshape(1, -1), dxbc_act,
                                      n("conv"))
    gs["conv_w"] = dconv_w[:SSD_CONV]
    gs["conv_b"] = dconv_b[0]
    ddt16 = ddt.reshape(T, SSD_GROUPS, LANES)[:, :, :SSD_HPG].reshape(T, 16)
    dproj = jnp.concatenate([dz, dxbc.reshape(T, -1), dq_a, dkv_a, dk_rope, ddt16, jnp.zeros((T, 112), F32)],
                            axis=1).astype(BF16)
    dh1 = mm(dproj, kw["wcat"], "nt", n("in_dx"))
    gw["w_in"] = w_in_grad_storage(mm(sv["h1"], dproj, "tn", n("in_dw"), out_dtype=BF16))
    dx0, gs["attn_norm_g"], dx0b = norm_bwd(sv["x0"], sm["attn_norm_g"], dh1, D_MODEL, False, n("attn_norm"), add=dx1,
                                            bf16_copy=True)
    gs = {k: v.reshape(-1) for k, v in gs.items()}
    return dx0, dx0b, gw, gs


def _small_pack(vals):
    flat = jnp.concatenate([vals[l][k].reshape(-1) for l in range(DEPTH) for k in SMALL_RS])
    n = flat.shape[0]
    rows = -(-n // LANES)
    rows = -(-rows // 8) * 8
    return jnp.pad(flat, (0, rows * LANES - n)).reshape(rows, LANES), n


def _small_unpack(flat, shapes):
    v = flat.reshape(-1)
    out, off = [dict() for _ in range(DEPTH)], 0
    for l in range(DEPTH):
        for k in SMALL_RS:
            sz = shapes[k]
            out[l][k] = v[off:off + sz]
            off += sz
    return out


def _adam_2d(a):
    if a.ndim == 2:
        return a
    return a.reshape(a.shape[0] * a.shape[1], a.shape[2])


def kernel(x, mem, positions, attn_norm_g, w_in, conv_w, conv_b, dt_bias, a_log, d_skip, ssd_norm_g, q_a_norm_g, w_q_b, kv_a_norm_g, w_kv_b, mla_q_norm_g, mla_k_norm_g, w_out, xattn_norm_g, mem_norm_g, w_xq, w_xk, w_xv, xq_norm_g, xk_norm_g, w_xo, ffn_norm_g, w_gate, w_up, w_down, loss_target, m_attn_norm_g, m_w_in, m_conv_w, m_conv_b, m_dt_bias, m_a_log, m_d_skip, m_ssd_norm_g, m_q_a_norm_g, m_w_q_b, m_kv_a_norm_g, m_w_kv_b, m_mla_q_norm_g, m_mla_k_norm_g, m_w_out, m_xattn_norm_g, m_mem_norm_g, m_w_xq, m_w_xk, m_w_xv, m_xq_norm_g, m_xk_norm_g, m_w_xo, m_ffn_norm_g, m_w_gate, m_w_up, m_w_down, v_attn_norm_g, v_w_in, v_conv_w, v_conv_b, v_dt_bias, v_a_log, v_d_skip, v_ssd_norm_g, v_q_a_norm_g, v_w_q_b, v_kv_a_norm_g, v_w_kv_b, v_mla_q_norm_g, v_mla_k_norm_g, v_w_out, v_xattn_norm_g, v_mem_norm_g, v_w_xq, v_w_xk, v_w_xv, v_xq_norm_g, v_xk_norm_g, v_w_xo, v_ffn_norm_g, v_w_gate, v_w_up, v_w_down):
    loc = locals()
    W = {k: loc[k] for k in WEIGHTS}
    Mo = {k: loc["m_" + k] for k in WEIGHTS}
    Vo = {k: loc["v_" + k] for k in WEIGHTS}
    B, S, D = x.shape
    T = B * S
    xf = x.reshape(T, D)
    memf = mem.reshape(-1, D)
    tf = loss_target.reshape(T, D)
    cos, sin = rope_tables(positions.reshape(T, 1).astype(F32))

    cw = gather_all_devices(conv_w.reshape(DEPTH * SSD_CONV, -1), "gather_conv_w")[0::2]
    cw = cw.transpose(1, 0, 2).reshape(DEPTH, SSD_CONV, SSD_CONV_DIM)
    sms = [{k: W[k][l] for k in SMALL} for l in range(DEPTH)]
    me = _me_chip()

    def gather_start(l):
        shards = [(wq_to_storage(W[k][l]) if k == "w_q_b" else W[k][l]).astype(BF16) for k in GATHERED]
        return ici_start("gather", shards, [(4,) + s.shape for s in shards], "gather_start_l%d" % l)

    def gather_end(state, after, l):
        shards, lands = ici_wait(state, after, "gather_wait_l%d" % l)
        lands = forward_pair(lands, "gather_forward_l%d" % l)
        Gl = {k: lax.dynamic_update_slice(z, s[None], (me, 0, 0))[None] for k, z, s in zip(GATHERED, lands, shards)}
        return Gl, dict(wcat=wcat_from_storage(Gl["w_in"][0]), conv_w=jnp.pad(cw[l], ((0, 8 - SSD_CONV), (0, 0))))

    saved, Gs, kws = [], [], []
    h = xf
    state, token = gather_start(0)
    for l in range(DEPTH):
        Gl, kwl = gather_end(state, h, l)
        Gs.append(Gl)
        kws.append(kwl)
        if l + 1 < DEPTH:
            state, token = gather_start(l + 1)
        h, sv = layer_fwd(h, memf, cos, sin, Gl, kwl, sms[l], B, S, l, dep=token if l + 1 < DEPTH else None)
        saved.append(sv)
    dh, lpart, dhb = loss_head(h, tf)
    loss = lax.psum(lpart[0, 0], ("x", "y", "c"))

    grads, delta, new_m, new_v = {}, {}, {}, {}
    chain = {k: None for k in GATHERED}
    gsmall = [None] * DEPTH

    def finish(rs_state, after, l):
        red = reduce_scatter_end(rs_state, after, "l%d" % l)
        for k in GATHERED:
            g = wq_from_storage(red[k]) if k == "w_q_b" else red[k]
            chain[k] = adamw_layer(W[k], g, Mo[k], Vo[k], l, chain[k], "adamw_%s_l%d" % (k, l))

    rs_state, token = None, None
    for l in reversed(range(DEPTH)):
        dh, dhb, gw, gs = layer_bwd(dh, dhb, saved[l], memf, cos, sin, Gs[l], kws[l], sms[l], B, S, l, dep=token)
        saved[l] = None
        gsmall[l] = gs
        if rs_state is not None:
            finish(rs_state, dh, l + 1)
        rs_state, token = reduce_scatter_begin(gw, "l%d" % l)
    finish(rs_state, token, 0)
    for k in GATHERED:
        grads[k], delta[k], new_m[k], new_v[k] = chain[k]
    grad_x = dh.reshape(B, S, D)

    spack, _ = _small_pack(gsmall)
    allv = gather_all_devices(spack, "gather_small_grads")
    ssum = sum_arrays([allv[i] for i in range(8)], F32, "sum_small_grads")
    sizes = {k: int(np.prod(W[k].shape[1:])) for k in SMALL}
    sizes["conv_w"] = SSD_CONV * SSD_CONV_DIM
    gsm = _small_unpack(ssum, sizes)
    for k in SMALL_RS:
        g = jnp.stack([gsm[l][k] for l in range(DEPTH)])
        if k == "conv_w":
            cs = SSD_CONV_DIM // 4
            g = lax.dynamic_slice_in_dim(g.reshape(DEPTH, SSD_CONV, SSD_CONV_DIM), _me_chip() * cs, cs, axis=2)
        shp = W[k].shape
        grads[k] = g.reshape(shp)
        d, nm, nv = adamw(_adam_2d(W[k]), _adam_2d(grads[k]), _adam_2d(Mo[k]), _adam_2d(Vo[k]), "adamw_" + k)
        delta[k], new_m[k], new_v[k] = d.reshape(shp), nm.reshape(shp), nv.reshape(shp)

    return (loss, grad_x, *[grads[k] for k in WEIGHTS], *[delta[k] for k in WEIGHTS], *[new_m[k] for k in WEIGHTS],
            *[new_v[k] for k in WEIGHTS])
```

```python
import functools
import math

import numpy as np
import jax
import jax.numpy as jnp
from jax import lax
from jax.experimental import pallas as pl
from jax.experimental.pallas import tpu as pltpu

F32 = jnp.float32
BF16 = jnp.bfloat16
MESH = pl.DeviceIdType.MESH
HI = lax.Precision.HIGHEST

D_MODEL = 2048
DEPTH = 4
SSD_INNER = 1024
SSD_HEAD_DIM = 64
SSD_GROUPS = 2
SSD_HPG = 8
SSD_STATE = 128
SSD_CONV = 4
SSD_CHUNK = 128
SSD_CONV_DIM = 1536
SSD_GW = SSD_HPG * SSD_HEAD_DIM
MLA_HEADS = 8
MLA_NOPE = 128
MLA_ROPE = 64
MLA_QK = 192
MLA_V = 128
Q_LORA = 512
KV_LORA = 512
ROPE_THETA = 10000.0
X_HEADS = 4
X_HEAD_DIM = 128
X_INNER = 512
FFN = 5632
IN_COLS = 3664
EPS = 1e-6
ADAM_LR, ADAM_B1, ADAM_B2, ADAM_EPS, ADAM_WD, ADAM_STEP = 0.001, 0.9, 0.999, 1e-08, 0.01, 10

C_Z, C_XBC, C_QA, C_KVA, C_KR, C_DT, CAT_COLS = 0, 1024, 2560, 3072, 3584, 3712, 3840

LANES = 128
VMEM_LIMIT = 56 * 1024 * 1024
MM_VMEM_BUDGET = 40 * 1024 * 1024
MM_STEP_BYTES = 1024 * 1024

SMALL = ("attn_norm_g", "conv_b", "dt_bias", "a_log", "d_skip", "ssd_norm_g", "q_a_norm_g", "kv_a_norm_g",
         "mla_q_norm_g", "mla_k_norm_g", "xattn_norm_g", "mem_norm_g", "xq_norm_g", "xk_norm_g", "ffn_norm_g")
SMALL_RS = SMALL + ("conv_w",)
WEIGHTS = ("attn_norm_g", "w_in", "conv_w", "conv_b", "dt_bias", "a_log", "d_skip", "ssd_norm_g", "q_a_norm_g", "w_q_b",
           "kv_a_norm_g", "w_kv_b", "mla_q_norm_g", "mla_k_norm_g", "w_out", "xattn_norm_g", "mem_norm_g", "w_xq", "w_xk",
           "w_xv", "xq_norm_g", "xk_norm_g", "w_xo", "ffn_norm_g", "w_gate", "w_up", "w_down")


def _pcall(body, **kw):
    return pl.pallas_call(body, **kw)


def _params(sem):
    return pltpu.CompilerParams(dimension_semantics=sem, vmem_limit_bytes=VMEM_LIMIT)


def _tile(n, cands):
    for t in cands:
        if n % t == 0:
            return t
    return n


def _sigmoid(z):
    return 1.0 / (1.0 + jnp.exp(-z))


def _mtile(n, cap):
    if n % LANES:
        return n
    q = n // LANES
    t = LANES * max(d for d in range(1, q + 1) if q % d == 0 and LANES * d <= cap)
    return n if (t < 512 and n <= 2048) else t


def _divisor_tiles(n, cap):
    if n % LANES:
        return [n]
    q = n // LANES
    return [LANES * d for d in range(1, q + 1) if q % d == 0 and LANES * d <= cap] or [n]


def _mm_tiles(M, N, K, m_unit, n_unit, k_unit, a_item, b_item, o_item, has_add):
    best = None
    for tm in _divisor_tiles(m_unit, 2048):
        for tn in _divisor_tiles(n_unit, 2048):
            for tk in _divisor_tiles(k_unit, 2048):
                nk = K // tk
                vmem = 2 * (tm * tk * a_item + tk * tn * b_item + tm * tn * (o_item + (4 if has_add else 0)))
                vmem += tm * tn * 4 * (2 if nk > 1 else 1) + (tm * tk * 2 if a_item == 4 else 0) + (tk * tn * 2 if b_item == 4 else 0)
                if vmem > MM_VMEM_BUDGET:
                    continue
                steps = (M // tm) * (N // tn) * nk
                traffic = (M * K * a_item * (N // tn if nk > 1 else 1) + K * N * b_item * (M // tm)
                           + M * N * (o_item + (4 if has_add else 0)) + steps * MM_STEP_BYTES)
                if nk > 1:
                    traffic += steps * tm * tn * 4
                if best is None or traffic < best[0]:
                    best = (traffic, tm, tn, tk)
    assert best is not None, (M, N, K)
    return best[1:]


def mm(a, b, mode, name, add=None, out_dtype=F32, wv=None, out_wv=None, dep=None):
    kind, l = wv if wv is not None else (None, None)
    if kind == "row":
        b, kind = b[l].reshape(-1, b.shape[-1]), None
    if out_wv == "row":
        res = mm(a, b, mode, name, add=add, out_dtype=out_dtype, wv=wv if kind else None, dep=dep)
        return res.reshape(4, res.shape[0] // 4, res.shape[1])
    if kind == "col":
        _, _, r, c = b.shape
        bshape = (r, 4 * c)
    else:
        bshape = b.shape
    if mode == "nn":
        (M, K), (K2, N) = a.shape, bshape
    elif mode == "nt":
        (M, K), (N, K2) = a.shape, bshape
    else:
        (K, M), (K2, N) = a.shape, bshape
    assert K == K2, (a.shape, bshape, mode)
    m_unit = M
    n_unit = N // 4 if (out_wv == "col" or (kind == "col" and mode == "nn")) else N
    k_unit = K // 4 if (kind == "col" and mode == "nt") else K
    tm, tn, tk = _mm_tiles(M, N, K, m_unit, n_unit, k_unit, a.dtype.itemsize, b.dtype.itemsize,
                           jnp.dtype(out_dtype).itemsize, add is not None)
    nk = K // tk
    qm, qn, qk = m_unit // tm, n_unit // tn, k_unit // tk
    dn = {"nn": (((1,), (0,)), ((), ())), "nt": (((1,), (1,)), ((), ())), "tn": (((0,), (0,)), ((), ()))}[mode]
    has_add = add is not None

    def body(a_ref, b_ref, *rest):
        rest = list(rest)
        c_ref = rest.pop(0) if has_add else None
        if dep is not None:
            rest.pop(0)
        o_ref = rest.pop(0)
        d = lax.dot_general(a_ref[...].astype(BF16), b_ref[...].astype(BF16), dn, preferred_element_type=F32)

        def finish(res):
            if has_add:
                res = res + c_ref[...].astype(F32)
            o_ref[...] = res.astype(out_dtype)

        if nk == 1:
            finish(d)
            return
        acc = rest.pop(0)
        k = pl.program_id(2)

        @pl.when(k == 0)
        def _():
            acc[...] = d

        if nk > 2:
            @pl.when((k > 0) & (k < nk - 1))
            def _():
                acc[...] += d

        @pl.when(k == nk - 1)
        def _():
            finish(acc[...] + d)

    if mode == "tn":
        a_spec = pl.BlockSpec((tk, tm), lambda i, j, k: (k, i))
    else:
        a_spec = pl.BlockSpec((tm, tk), lambda i, j, k: (i, k))
    if kind is None:
        b_spec = (pl.BlockSpec((tn, tk), lambda i, j, k: (j, k)) if mode == "nt"
                  else pl.BlockSpec((tk, tn), lambda i, j, k: (k, j)))
    elif mode == "nn":
        b_spec = pl.BlockSpec((None, None, tk, tn), lambda i, j, k: (l, j // qn, k, j % qn))
    elif mode == "nt":
        b_spec = pl.BlockSpec((None, None, tn, tk), lambda i, j, k: (l, k // qk, j, k % qk))
    else:
        raise ValueError((mode, kind))
    if out_wv == "col":
        o_spec = pl.BlockSpec((None, tm, tn), lambda i, j, k: (j // qn, i, j % qn))
        out_shape = jax.ShapeDtypeStruct((4, M, N // 4), out_dtype)
    else:
        o_spec = pl.BlockSpec((tm, tn), lambda i, j, k: (i, j))
        out_shape = jax.ShapeDtypeStruct((M, N), out_dtype)
    in_specs = [a_spec, b_spec] + ([o_spec] if has_add else []) + ([pl.BlockSpec(memory_space=pl.ANY)] if dep is not None else [])
    args = (a, b) + ((add,) if has_add else ()) + ((dep,) if dep is not None else ())
    return _pcall(
        body, name=name, grid=(M // tm, N // tn, nk), in_specs=in_specs, out_specs=o_spec,
        out_shape=out_shape, scratch_shapes=[pltpu.VMEM((tm, tn), F32)] if nk > 1 else [],
        compiler_params=_params(("parallel", "parallel", "arbitrary")))(*args)


def norm_fwd(x, gain, W, shared, name, gate=None, out_dtype=BF16, dep=None):
    T, C = x.shape
    nb = C // W
    tq = _tile(T, (256, 128))
    g2 = gain.reshape(1, -1).astype(F32)
    gated = gate is not None

    def body(*refs):
        o_ref = refs[-1]
        if gated:
            x_ref, z_ref, g_ref = refs[:3]
        else:
            x_ref, g_ref = refs[:2]
        u = x_ref[...].astype(F32)
        if gated:
            z = z_ref[...]
            u = u * (z * _sigmoid(z))
        ms = jnp.mean(u * u, axis=-1, keepdims=True)
        o_ref[...] = (u * lax.rsqrt(ms + EPS) * g_ref[...]).astype(out_dtype)

    xs = pl.BlockSpec((tq, W), lambda j, i: (i, j))
    gs = pl.BlockSpec((1, W), (lambda j, i: (0, 0)) if shared else (lambda j, i: (0, j)))
    in_specs = [xs] + ([xs] if gated else []) + [gs] + ([pl.BlockSpec(memory_space=pl.ANY)] if dep is not None else [])
    args = (x,) + ((gate,) if gated else ()) + (g2,) + ((dep,) if dep is not None else ())
    return _pcall(body, name=name, grid=(nb, T // tq), in_specs=in_specs, out_specs=xs,
                  out_shape=jax.ShapeDtypeStruct((T, C), out_dtype),
                  compiler_params=_params(("parallel", "parallel")))(*args)


def norm_bwd(x, gain, dy, W, shared, name, gate=None, add=None, bf16_copy=False):
    T, C = x.shape
    nb = C // W
    tq = _tile(T, (256, 128))
    g2 = gain.reshape(1, -1).astype(F32)
    gated = gate is not None
    has_add = add is not None
    GC = W if shared else C

    def body(*refs):
        refs = list(refs)
        x_ref = refs.pop(0)
        z_ref = refs.pop(0) if gated else None
        g_ref = refs.pop(0)
        dy_ref = refs.pop(0)
        a_ref = refs.pop(0) if has_add else None
        dx_ref = refs.pop(0)
        dz_ref = refs.pop(0) if gated else None
        dg_ref = refs.pop(0)
        j, i = pl.program_id(0), pl.program_id(1)
        first = (i == 0) & (j == 0) if shared else (i == 0)

        @pl.when(first)
        def _():
            dg_ref[...] = jnp.zeros_like(dg_ref)

        xv = x_ref[...].astype(F32)
        if gated:
            z = z_ref[...]
            s = _sigmoid(z)
            u = xv * (z * s)
        else:
            u = xv
        rstd = lax.rsqrt(jnp.mean(u * u, axis=-1, keepdims=True) + EPS)
        uh = u * rstd
        dyv = dy_ref[...].astype(F32)
        dg_ref[...] += jnp.sum(dyv * uh, axis=0, keepdims=True)
        dyg = dyv * g_ref[...]
        du = rstd * (dyg - uh * jnp.mean(dyg * uh, axis=-1, keepdims=True))
        if gated:
            dx = du * (z * s)
            dz_ref[...] = du * xv * (s * (1.0 + z * (1.0 - s)))
        else:
            dx = du
        if has_add:
            dx = dx + a_ref[...]
        dx_ref[...] = dx
        if bf16_copy:
            refs.pop(0)[...] = dx.astype(BF16)

    xs = pl.BlockSpec((tq, W), lambda j, i: (i, j))
    gs = pl.BlockSpec((1, W), (lambda j, i: (0, 0)) if shared else (lambda j, i: (0, j)))
    in_specs = [xs] + ([xs] if gated else []) + [gs, xs] + ([xs] if has_add else [])
    args = (x,) + ((gate,) if gated else ()) + (g2, dy) + ((add,) if has_add else ())
    out_specs = [xs] + ([xs] if gated else []) + [gs] + ([xs] if bf16_copy else [])
    out_shape = [jax.ShapeDtypeStruct((T, C), F32)] + ([jax.ShapeDtypeStruct((T, C), F32)] if gated else []) \
        + [jax.ShapeDtypeStruct((1, GC), F32)] + ([jax.ShapeDtypeStruct((T, C), BF16)] if bf16_copy else [])
    return _pcall(body, name=name, grid=(nb, T // tq), in_specs=in_specs, out_specs=out_specs, out_shape=out_shape,
                  compiler_params=_params(("arbitrary", "arbitrary")))(*args)


def _inv_freq_row():
    inv = 1.0 / (ROPE_THETA ** (np.arange(0, MLA_ROPE, 2, dtype=np.float32) / MLA_ROPE))
    row = np.zeros((1, LANES), np.float32)
    row[0, :32] = inv
    row[0, 32:64] = inv
    return jnp.asarray(row)


def rope_tables(pos):
    T = pos.shape[0]
    tq = _tile(T, (512, 256, 128))

    def body(p_ref, f_ref, c_ref, s_ref):
        ang = p_ref[...] * f_ref[...]
        lane = lax.broadcasted_iota(jnp.int32, ang.shape, 1)
        c_ref[...] = jnp.where(lane < 64, jnp.cos(ang), 0.0)
        sn = jnp.sin(ang)
        s_ref[...] = jnp.where(lane < 32, -sn, jnp.where(lane < 64, sn, 0.0))

    ts = pl.BlockSpec((tq, LANES), lambda i: (i, 0))
    return _pcall(body, name="rope_tables", grid=(T // tq,),
                  in_specs=[pl.BlockSpec((tq, 1), lambda i: (i, 0)), pl.BlockSpec((1, LANES), lambda i: (0, 0))],
                  out_specs=[ts, ts], out_shape=[jax.ShapeDtypeStruct((T, LANES), F32)] * 2,
                  compiler_params=_params(("parallel",)))(pos, _inv_freq_row())


def _swap_halves(r):
    lane = lax.broadcasted_iota(jnp.int32, r.shape, 1)
    return jnp.where(lane < 32, pltpu.roll(r, 96, 1), pltpu.roll(r, 32, 1))


def qk_fwd(nope_arr, nope_blk0, rope_arr, rope_blk0, rope_shared, gn, gr, cos, sin, name):
    T = nope_arr.shape[0]
    tq = _tile(T, (256, 128))

    col = lambda blk: slice(blk * LANES, (blk + 1) * LANES)

    def body(n_ref, r_ref, gn_ref, gr_ref, c_ref, s_ref, o_ref):
        for h in range(MLA_HEADS):
            n = n_ref[:, col(nope_blk0(h))]
            r = r_ref[...] if rope_shared else r_ref[:, col(rope_blk0(h))]
            ms = (jnp.sum(n * n, axis=-1, keepdims=True) + jnp.sum(r * r, axis=-1, keepdims=True)) * (1.0 / MLA_QK)
            rstd = lax.rsqrt(ms + EPS)
            rn = r * rstd * gr_ref[...]
            rr = rn * c_ref[...] + _swap_halves(rn) * s_ref[...]
            o_ref[:, col(2 * h)] = (n * rstd * gn_ref[...]).astype(BF16)
            o_ref[:, col(2 * h + 1)] = rr.astype(BF16)

    wide = lambda arr: pl.BlockSpec((tq, arr.shape[1]), lambda i: (i, 0))
    rspec = pl.BlockSpec((tq, LANES), lambda i: (i, rope_blk0)) if rope_shared else wide(rope_arr)
    row = pl.BlockSpec((1, LANES), lambda i: (0, 0))
    tab = pl.BlockSpec((tq, LANES), lambda i: (i, 0))
    return _pcall(body, name=name, grid=(T // tq,),
                  in_specs=[wide(nope_arr), rspec, row, row, tab, tab],
                  out_specs=pl.BlockSpec((tq, MLA_HEADS * 2 * LANES), lambda i: (i, 0)),
                  out_shape=jax.ShapeDtypeStruct((T, MLA_HEADS * 2 * LANES), BF16),
                  compiler_params=_params(("parallel",)))(nope_arr, rope_arr, gn, gr, cos, sin)


def qk_bwd(nope_arr, nope_blk0, rope_arr, rope_blk0, rope_shared, gn, gr, cos, sin, dout, name, fill=None, fill_blk=None):
    T = nope_arr.shape[0]
    tq = _tile(T, (256, 128))
    col = lambda blk: slice(blk * LANES, (blk + 1) * LANES)

    def body(*refs):
        refs = list(refs)
        n_ref, r_ref, gn_ref, gr_ref, c_ref, s_ref, d_ref = refs[:7]
        f_ref = refs[7] if rope_shared else None
        outs = refs[8:] if rope_shared else refs[7:]
        draw_ref = outs[0]
        dsh_ref = outs[1] if rope_shared else None
        dgn_ref, dgr_ref = outs[-2:]

        @pl.when(pl.program_id(0) == 0)
        def _():
            dgn_ref[...] = jnp.zeros_like(dgn_ref)
            dgr_ref[...] = jnp.zeros_like(dgr_ref)

        dgn = jnp.zeros((1, LANES), F32)
        dgr = jnp.zeros((1, LANES), F32)
        dshared = jnp.zeros((tq, LANES), F32)
        for h in range(MLA_HEADS):
            n = n_ref[:, col(nope_blk0(h))]
            r = r_ref[...] if rope_shared else r_ref[:, col(rope_blk0(h))]
            ms = (jnp.sum(n * n, axis=-1, keepdims=True) + jnp.sum(r * r, axis=-1, keepdims=True)) * (1.0 / MLA_QK)
            rstd = lax.rsqrt(ms + EPS)
            nh = n * rstd
            rh = r * rstd
            d_n = d_ref[:, col(2 * h)].astype(F32)
            d_rr = d_ref[:, col(2 * h + 1)].astype(F32)
            d_rn = d_rr * c_ref[...] - _swap_halves(d_rr) * s_ref[...]
            dgn = dgn + jnp.sum(d_n * nh, axis=0, keepdims=True)
            dgr = dgr + jnp.sum(d_rn * rh, axis=0, keepdims=True)
            dng = d_n * gn_ref[...]
            drg = d_rn * gr_ref[...]
            mean = (jnp.sum(dng * nh, axis=-1, keepdims=True) + jnp.sum(drg * rh, axis=-1, keepdims=True)) * (1.0 / MLA_QK)
            draw_ref[:, col(nope_blk0(h))] = (rstd * (dng - nh * mean)).astype(BF16)
            drope = rstd * (drg - rh * mean)
            if rope_shared:
                dshared = dshared + drope
                draw_ref[:, col(fill_blk(h))] = f_ref[:, col(h)].astype(BF16)
            else:
                draw_ref[:, col(rope_blk0(h))] = drope.astype(BF16)
        dgn_ref[...] += dgn
        dgr_ref[...] += dgr
        if rope_shared:
            dsh_ref[...] = dshared

    wide = lambda arr: pl.BlockSpec((tq, arr.shape[1]), lambda i: (i, 0))
    rspec = pl.BlockSpec((tq, LANES), lambda i: (i, rope_blk0)) if rope_shared else wide(rope_arr)
    row = pl.BlockSpec((1, LANES), lambda i: (0, 0))
    tab = pl.BlockSpec((tq, LANES), lambda i: (i, 0))
    raw = pl.BlockSpec((tq, 2 * MLA_HEADS * LANES), lambda i: (i, 0))
    in_specs = [wide(nope_arr), rspec, row, row, tab, tab, wide(dout)] + ([wide(fill)] if rope_shared else [])
    out_specs = [raw] + ([tab] if rope_shared else []) + [row, row]
    out_shape = ([jax.ShapeDtypeStruct((T, 2 * MLA_HEADS * LANES), BF16)]
                 + ([jax.ShapeDtypeStruct((T, LANES), F32)] if rope_shared else [])
                 + [jax.ShapeDtypeStruct((1, LANES), F32)] * 2)
    args = (nope_arr, rope_arr, gn, gr, cos, sin, dout) + ((fill,) if rope_shared else ())
    return _pcall(body, name=name, grid=(T // tq,), in_specs=in_specs, out_specs=out_specs, out_shape=out_shape,
                  compiler_params=_params(("arbitrary",)))(*args)


def _scores(q, k, scale, causal, row0):
    s = lax.dot_general(q, k, (((1,), (1,)), ((), ())), preferred_element_type=F32) * scale
    if causal:
        qi = row0 + lax.broadcasted_iota(jnp.int32, s.shape, 0)
        ki = lax.broadcasted_iota(jnp.int32, s.shape, 1)
        s = jnp.where(qi >= ki, s, -1e30)
    return s


def attn_fwd(q, k, v, H, dqk, dv, scale, causal, name, v_idx=lambda h: h):
    B, S, _ = q.shape
    Sk = k.shape[1]
    tq = _tile(S, (256, 128))

    def tile(q_ref, k_ref, v_ref, o_ref, row0, kl):
        s = _scores(q_ref[0], k_ref[0, :kl], scale, causal, row0)
        p = jnp.exp(s - jnp.max(s, axis=-1, keepdims=True))
        l = jnp.sum(p, axis=-1, keepdims=True)
        o = lax.dot_general(p.astype(BF16), v_ref[0, :kl], (((1,), (0,)), ((), ())), preferred_element_type=F32)
        o_ref[0] = (o / l).astype(BF16)

    def body(q_ref, k_ref, v_ref, o_ref):
        if not causal:
            tile(q_ref, k_ref, v_ref, o_ref, 0, Sk)
            return
        for qi in range(S // tq):
            @pl.when(pl.program_id(2) == qi)
            def _(qi=qi):
                tile(q_ref, k_ref, v_ref, o_ref, qi * tq, (qi + 1) * tq)

    return _pcall(body, name=name, grid=(B, H, S // tq),
                  in_specs=[pl.BlockSpec((1, tq, dqk), lambda b, h, i: (b, i, h)),
                            pl.BlockSpec((1, Sk, dqk), lambda b, h, i: (b, 0, h)),
                            pl.BlockSpec((1, Sk, dv), lambda b, h, i: (b, 0, v_idx(h)))],
                  out_specs=pl.BlockSpec((1, tq, dv), lambda b, h, i: (b, i, h)),
                  out_shape=jax.ShapeDtypeStruct((B, S, H * dv), BF16),
                  compiler_params=_params(("parallel", "parallel", "parallel")))(q, k, v)


def attn_bwd(q, k, v, do, H, dqk, dv, scale, causal, name, v_idx=lambda h: h):
    B, S, _ = q.shape
    Sk = k.shape[1]
    tq = _tile(S, (256, 128))

    def body(q_ref, k_ref, v_ref, do_ref, dq_ref, dk_ref, dv_ref):
        i = pl.program_id(2)

        @pl.when(i == 0)
        def _():
            dk_ref[...] = jnp.zeros_like(dk_ref)
            dv_ref[...] = jnp.zeros_like(dv_ref)

        def tile(row0, kl):
            qv, kv, vv = q_ref[0], k_ref[0, :kl], v_ref[0, :kl]
            s = _scores(qv, kv, scale, causal, row0)
            p = jnp.exp(s - jnp.max(s, axis=-1, keepdims=True))
            p = p / jnp.sum(p, axis=-1, keepdims=True)
            dob = do_ref[0].astype(BF16)
            pb = p.astype(BF16)
            dv_ref[0, :kl] += lax.dot_general(pb, dob, (((0,), (0,)), ((), ())), preferred_element_type=F32)
            dp = lax.dot_general(dob, vv, (((1,), (1,)), ((), ())), preferred_element_type=F32)
            delta = jnp.sum(p * dp, axis=-1, keepdims=True)
            ds = (p * (dp - delta) * scale).astype(BF16)
            dq_ref[0] = lax.dot_general(ds, kv, (((1,), (0,)), ((), ())), preferred_element_type=F32)
            dk_ref[0, :kl] += lax.dot_general(ds, qv, (((0,), (0,)), ((), ())), preferred_element_type=F32)

        if not causal:
            tile(0, Sk)
            return
        for qi in range(S // tq):
            @pl.when(i == qi)
            def _(qi=qi):
                tile(qi * tq, (qi + 1) * tq)

    qs = pl.BlockSpec((1, tq, dqk), lambda b, h, i: (b, i, h))
    ks = pl.BlockSpec((1, Sk, dqk), lambda b, h, i: (b, 0, h))
    vs = pl.BlockSpec((1, Sk, dv), lambda b, h, i: (b, 0, h))
    vin = pl.BlockSpec((1, Sk, dv), lambda b, h, i: (b, 0, v_idx(h)))
    return _pcall(body, name=name, grid=(B, H, S // tq),
                  in_specs=[qs, ks, vin, pl.BlockSpec((1, tq, dv), lambda b, h, i: (b, i, h))],
                  out_specs=[qs, ks, vs],
                  out_shape=[jax.ShapeDtypeStruct((B, S, H * dqk), F32), jax.ShapeDtypeStruct((B, Sk, H * dqk), F32),
                             jax.ShapeDtypeStruct((B, Sk, H * dv), F32)],
                  compiler_params=_params(("parallel", "parallel", "arbitrary")))(q, k, v, do)


def _shift_down(u, j):
    if j == 0:
        return u
    row = lax.broadcasted_iota(jnp.int32, u.shape, 0)
    return jnp.where(row >= j, pltpu.roll(u, j, 0), 0.0)


def _shift_up(u, j):
    if j == 0:
        return u
    n = u.shape[0]
    row = lax.broadcasted_iota(jnp.int32, u.shape, 0)
    return jnp.where(row < n - j, pltpu.roll(u, n - j, 0), 0.0)


def conv_fwd(u, col0_blk, w8, b, name):
    B, S, _ = u.shape
    tc = 256
    nb = SSD_CONV_DIM // tc

    def body(u_ref, w_ref, b_ref, o_ref):
        uv = u_ref[0]
        pre = b_ref[...] + jnp.zeros_like(uv)
        for kk in range(SSD_CONV):
            pre = pre + w_ref[kk:kk + 1, :] * _shift_down(uv, SSD_CONV - 1 - kk)
        o_ref[0] = pre * _sigmoid(pre)

    return _pcall(body, name=name, grid=(B, nb),
                  in_specs=[pl.BlockSpec((1, S, tc), lambda bb, j: (bb, 0, col0_blk + j)),
                            pl.BlockSpec((8, tc), lambda bb, j: (0, j)), pl.BlockSpec((1, tc), lambda bb, j: (0, j))],
                  out_specs=pl.BlockSpec((1, S, tc), lambda bb, j: (bb, 0, j)),
                  out_shape=jax.ShapeDtypeStruct((B, S, SSD_CONV_DIM), F32),
                  compiler_params=_params(("parallel", "parallel")))(u, w8, b)


def conv_bwd(u, col0_blk, w8, b, dy, name):
    B, S, _ = u.shape
    tc = 256
    nb = SSD_CONV_DIM // tc

    def body(u_ref, w_ref, b_ref, dy_ref, du_ref, dw_ref, db_ref):
        bb = pl.program_id(1)

        @pl.when(bb == 0)
        def _():
            dw_ref[...] = jnp.zeros_like(dw_ref)
            db_ref[...] = jnp.zeros_like(db_ref)

        uv = u_ref[0]
        pre = b_ref[...] + jnp.zeros_like(uv)
        for kk in range(SSD_CONV):
            pre = pre + w_ref[kk:kk + 1, :] * _shift_down(uv, SSD_CONV - 1 - kk)
        s = _sigmoid(pre)
        dpre = dy_ref[0] * (s * (1.0 + pre * (1.0 - s)))
        du = jnp.zeros_like(uv)
        for kk in range(SSD_CONV):
            j = SSD_CONV - 1 - kk
            du = du + w_ref[kk:kk + 1, :] * _shift_up(dpre, j)
            dw_ref[kk:kk + 1, :] += jnp.sum(dpre * _shift_down(uv, j), axis=0, keepdims=True)
        db_ref[...] += jnp.sum(dpre, axis=0, keepdims=True)
        du_ref[0] = du

    return _pcall(body, name=name, grid=(nb, B),
                  in_specs=[pl.BlockSpec((1, S, tc), lambda j, bb: (bb, 0, col0_blk + j)),
                            pl.BlockSpec((8, tc), lambda j, bb: (0, j)), pl.BlockSpec((1, tc), lambda j, bb: (0, j)),
                            pl.BlockSpec((1, S, tc), lambda j, bb: (bb, 0, j))],
                  out_specs=[pl.BlockSpec((1, S, tc), lambda j, bb: (bb, 0, j)),
                             pl.BlockSpec((8, tc), lambda j, bb: (0, j)), pl.BlockSpec((1, tc), lambda j, bb: (0, j))],
                  out_shape=[jax.ShapeDtypeStruct((B, S, SSD_CONV_DIM), F32), jax.ShapeDtypeStruct((8, SSD_CONV_DIM), F32),
                             jax.ShapeDtypeStruct((1, SSD_CONV_DIM), F32)],
                  compiler_params=_params(("arbitrary", "arbitrary")))(u, w8, b, dy)


def _expand_mat():
    e = np.zeros((LANES, SSD_GW), np.float32)
    for h in range(SSD_HPG):
        e[h, h * SSD_HEAD_DIM:(h + 1) * SSD_HEAD_DIM] = 1.0
    return jnp.asarray(e)


def _tri_mat():
    return jnp.asarray(np.tril(np.ones((SSD_CHUNK, SSD_CHUNK), np.float32)))


def _dotf(a, b, dn=(((1,), (0,)), ((), ()))):
    return lax.dot_general(a, b, dn, precision=HI, preferred_element_type=F32)


def _dotb(a, b, dn=(((1,), (0,)), ((), ()))):
    return lax.dot_general(a.astype(BF16), b.astype(BF16), dn, preferred_element_type=F32)


NT = (((1,), (1,)), ((), ()))
TN = (((0,), (0,)), ((), ()))


def _ssd_common(dt_ref, par_ref, tri_ref, e_ref):
    dtr = dt_ref[0]
    xb = dtr + par_ref[0:1, :]
    dt = jnp.maximum(xb, 0.0) + jnp.log(1.0 + jnp.exp(-jnp.abs(xb)))
    A = -jnp.exp(par_ref[1:2, :])
    a = dt * A
    acs = _dotf(tri_ref[...], a)
    alast = acs[SSD_CHUNK - 1:SSD_CHUNK, :]
    E = e_ref[...]
    acsE = _dotf(acs, E)
    alastE = acsE[SSD_CHUNK - 1:SSD_CHUNK, :]
    return dict(xb=xb, dt=dt, A=A, acs=acs, acsT=acs.T, ealast=jnp.exp(alast), w=jnp.exp(alast - acs),
                dtE=_dotf(dt, E), eacsE=jnp.exp(acsE), wE=jnp.exp(alastE - acsE), ealastE=jnp.exp(alastE),
                DE=_dotf(par_ref[...], E)[2:3, :])


def _decay(c, e):
    diff = c["acs"][:, e:e + 1] - c["acsT"][e:e + 1, :]
    li = lax.broadcasted_iota(jnp.int32, diff.shape, 0)
    si = lax.broadcasted_iota(jnp.int32, diff.shape, 1)
    return jnp.where(li >= si, jnp.exp(jnp.minimum(diff, 0.0)), 0.0)


def ssd_fwd(xbc, dtp, par, name):
    B, S, _ = xbc.shape
    nc = S // SSD_CHUNK
    L, N, GW, P = SSD_CHUNK, SSD_STATE, SSD_GW, SSD_HEAD_DIM

    def body(x_ref, b_ref, c_ref, dt_ref, par_ref, tri_ref, e_ref, y_ref, st_ref, S_scr):
        @pl.when(pl.program_id(2) == 0)
        def _():
            S_scr[...] = jnp.zeros_like(S_scr)

        Sin = S_scr[...]
        st_ref[0, 0, 0] = Sin
        c = _ssd_common(dt_ref, par_ref.at[0], tri_ref, e_ref)
        X = x_ref[0]
        Bm = b_ref[0]
        Cm = c_ref[0]
        xdt = X * c["dtE"]
        CB = _dotb(Cm, Bm, NT)
        y_ref[0] = c["eacsE"] * _dotb(Cm, Sin) + c["DE"] * X
        for e in range(SSD_HPG):
            M = CB * _decay(c, e)
            y_ref[0, :, e * P:(e + 1) * P] += _dotb(M, xdt[:, e * P:(e + 1) * P])
        S_scr[...] = c["ealastE"] * Sin + _dotb(Bm, xdt * c["wE"], TN)

    return _pcall(
        body, name=name, grid=(SSD_GROUPS, B, nc),
        in_specs=[pl.BlockSpec((1, L, GW), lambda g, b, ci: (b, ci, g)),
                  pl.BlockSpec((1, L, N), lambda g, b, ci: (b, ci, 8 + g)),
                  pl.BlockSpec((1, L, N), lambda g, b, ci: (b, ci, 10 + g)),
                  pl.BlockSpec((1, L, LANES), lambda g, b, ci: (b, ci, g)),
                  pl.BlockSpec((1, 8, LANES), lambda g, b, ci: (g, 0, 0)),
                  pl.BlockSpec((L, L), lambda g, b, ci: (0, 0)),
                  pl.BlockSpec((LANES, GW), lambda g, b, ci: (0, 0))],
        out_specs=[pl.BlockSpec((1, L, GW), lambda g, b, ci: (b, ci, g)),
                   pl.BlockSpec((1, 1, 1, N, GW), lambda g, b, ci: (g, b, ci, 0, 0))],
        out_shape=[jax.ShapeDtypeStruct((B, S, SSD_INNER), F32), jax.ShapeDtypeStruct((SSD_GROUPS, B, nc, N, GW), F32)],
        scratch_shapes=[pltpu.VMEM((N, GW), F32)],
        compiler_params=_params(("arbitrary", "arbitrary", "arbitrary")))(xbc, xbc, xbc, dtp, par, _tri_mat(), _expand_mat())


def ssd_bwd(xbc, dtp, par, states, dy, name):
    B, S, _ = xbc.shape
    nc = S // SSD_CHUNK
    L, N, GW, P = SSD_CHUNK, SSD_STATE, SSD_GW, SSD_HEAD_DIM

    def body(x_ref, b_ref, c_ref, dt_ref, par_ref, tri_ref, e_ref, st_ref, dy_ref,
             dx_ref, db_ref, dc_ref, ddt_ref, dpar_ref, dS_scr, dxdt_scr):
        @pl.when(pl.program_id(2) == 0)
        def _():
            dS_scr[...] = jnp.zeros_like(dS_scr)

        @pl.when((pl.program_id(1) == 0) & (pl.program_id(2) == 0))
        def _():
            dpar_ref[...] = jnp.zeros_like(dpar_ref)

        c = _ssd_common(dt_ref, par_ref.at[0], tri_ref, e_ref)
        E = e_ref[...]
        red = lambda t: _dotf(t, E, NT)
        red_row = lambda t: red(jnp.broadcast_to(t, (8, GW)))[0:1, :]
        X, Bm, Cm = x_ref[0], b_ref[0], c_ref[0]
        Sin = st_ref[0, 0, 0]
        dY = dy_ref[0]
        dSo = dS_scr[...]
        xdt = X * c["dtE"]
        CB = _dotb(Cm, Bm, NT)
        dD = red_row(jnp.sum(dY * X, axis=0, keepdims=True))
        yoff = c["eacsE"] * _dotb(Cm, Sin)
        dacs = red(dY * yoff)
        dYe = dY * c["eacsE"]
        dC = _dotb(dYe, Sin, NT)
        dSin = _dotb(Cm, dYe, TN) + c["ealastE"] * dSo
        dB = _dotb(xdt * c["wE"], dSo, NT)
        dxw = _dotb(Bm, dSo)
        t = red(dxw * xdt) * c["w"]
        dacs = dacs - t
        dalast = jnp.sum(t, axis=0, keepdims=True) + c["ealast"] * red_row(jnp.sum(Sin * dSo, axis=0, keepdims=True))
        dxdt_scr[...] = dxw * c["wE"]
        dCB = jnp.zeros((L, L), F32)
        dacsT = jnp.zeros((LANES, L), F32)
        lane = lax.broadcasted_iota(jnp.int32, (1, LANES), 1)
        subl = lax.broadcasted_iota(jnp.int32, (LANES, 1), 0)
        for e in range(SSD_HPG):
            Lm = _decay(c, e)
            M = CB * Lm
            dYh = dY[:, e * P:(e + 1) * P]
            dM = _dotb(dYh, xdt[:, e * P:(e + 1) * P], NT)
            dxdt_scr[:, e * P:(e + 1) * P] += _dotb(M, dYh, TN)
            dCBe = dM * Lm
            dCB = dCB + dCBe
            Gm = dCBe * CB
            dacs = dacs + jnp.sum(Gm, axis=1, keepdims=True) * (lane == e).astype(F32)
            dacsT = dacsT - jnp.sum(Gm, axis=0, keepdims=True) * (subl == e).astype(F32)
        dC = dC + _dotb(dCB, Bm)
        dB = dB + _dotb(dCB, Cm, TN)
        dacs = dacs + dacsT.T
        row = lax.broadcasted_iota(jnp.int32, (L, 1), 0)
        dacs = dacs + jnp.where(row == L - 1, dalast, 0.0)
        da = _dotf(tri_ref[...], dacs, TN)
        dxdt = dxdt_scr[...]
        ddt = da * c["A"] + red(dxdt * X)
        dA = jnp.sum(da * c["dt"], axis=0, keepdims=True)
        ddtr = ddt * _sigmoid(c["xb"])
        dx_ref[0] = c["DE"] * dY + dxdt * c["dtE"]
        db_ref[0] = dB
        dc_ref[0] = dC
        ddt_ref[0] = ddtr
        dpar_ref[0, 0:1, :] += jnp.sum(ddtr, axis=0, keepdims=True)
        dpar_ref[0, 1:2, :] += dA * c["A"]
        dpar_ref[0, 2:3, :] += dD
        dS_scr[...] = dSin

    rc = lambda ci: nc - 1 - ci
    xspec = pl.BlockSpec((1, L, GW), lambda g, b, ci: (b, rc(ci), g))
    return _pcall(
        body, name=name, grid=(SSD_GROUPS, B, nc),
        in_specs=[xspec,
                  pl.BlockSpec((1, L, N), lambda g, b, ci: (b, rc(ci), 8 + g)),
                  pl.BlockSpec((1, L, N), lambda g, b, ci: (b, rc(ci), 10 + g)),
                  pl.BlockSpec((1, L, LANES), lambda g, b, ci: (b, rc(ci), g)),
                  pl.BlockSpec((1, 8, LANES), lambda g, b, ci: (g, 0, 0)),
                  pl.BlockSpec((L, L), lambda g, b, ci: (0, 0)),
                  pl.BlockSpec((LANES, GW), lambda g, b, ci: (0, 0)),
                  pl.BlockSpec((1, 1, 1, N, GW), lambda g, b, ci: (g, b, rc(ci), 0, 0)),
                  xspec],
        out_specs=[xspec,
                   pl.BlockSpec((1, L, N), lambda g, b, ci: (b, rc(ci), g)),
                   pl.BlockSpec((1, L, N), lambda g, b, ci: (b, rc(ci), g)),
                   pl.BlockSpec((1, L, LANES), lambda g, b, ci: (b, rc(ci), g)),
                   pl.BlockSpec((1, 8, LANES), lambda g, b, ci: (g, 0, 0))],
        out_shape=[jax.ShapeDtypeStruct((B, S, SSD_INNER), F32), jax.ShapeDtypeStruct((B, S, SSD_GROUPS * N), F32),
                   jax.ShapeDtypeStruct((B, S, SSD_GROUPS * N), F32), jax.ShapeDtypeStruct((B, S, SSD_GROUPS * LANES), F32),
                   jax.ShapeDtypeStruct((SSD_GROUPS, 8, LANES), F32)],
        scratch_shapes=[pltpu.VMEM((N, GW), F32), pltpu.VMEM((L, GW), F32)],
        compiler_params=_params(("arbitrary", "arbitrary", "arbitrary")))(
            xbc, xbc, xbc, dtp, par, _tri_mat(), _expand_mat(), states, dy)


def gate_up_act(h, wg, wu, name):
    T, D = h.shape
    c = wg.shape[-1]
    tm = _tile(T, (512, 256, 128))

    def body(h_ref, g_w, u_w, g_ref, u_ref, a_ref):
        hv = h_ref[...]
        g = jnp.dot(hv, g_w[...], preferred_element_type=F32)
        u = jnp.dot(hv, u_w[...], preferred_element_type=F32)
        g_ref[...] = g.astype(BF16)
        u_ref[...] = u.astype(BF16)
        a_ref[...] = (g * _sigmoid(g) * u).astype(BF16)

    ws = pl.BlockSpec((None, None, D, c), lambda i, j: (0, j, 0, 0))
    os_ = pl.BlockSpec((tm, c), lambda i, j: (i, j))
    return _pcall(body, name=name, grid=(T // tm, 4), in_specs=[pl.BlockSpec((tm, D), lambda i, j: (i, 0)), ws, ws],
                  out_specs=[os_, os_, os_], out_shape=[jax.ShapeDtypeStruct((T, 4 * c), BF16)] * 3,
                  compiler_params=_params(("parallel", "parallel")))(h, wg, wu)


def down_dx_swiglu(dy, w_down, gate, up, name, dep=None):
    T, D = dy.shape
    Fh = w_down.shape[0]
    tm, tn = _tile(T, (1024, 512, 256, 128)), 512

    def body(dy_ref, w_ref, g_ref, u_ref, *rest):
        dg_ref, du_ref = rest[-2:]
        d = lax.dot_general(dy_ref[...], w_ref[...], NT, preferred_element_type=F32)
        g = g_ref[...].astype(F32)
        u = u_ref[...].astype(F32)
        s = _sigmoid(g)
        dg_ref[...] = (d * u * (s * (1.0 + g * (1.0 - s)))).astype(BF16)
        du_ref[...] = (d * g * s).astype(BF16)

    ts = pl.BlockSpec((tm, tn), lambda i, j: (i, j))
    in_specs = [pl.BlockSpec((tm, D), lambda i, j: (i, 0)), pl.BlockSpec((tn, D), lambda i, j: (j, 0)), ts, ts]
    args = (dy, w_down, gate, up)
    if dep is not None:
        in_specs, args = in_specs + [pl.BlockSpec(memory_space=pl.ANY)], args + (dep,)
    return _pcall(body, name=name, grid=(T // tm, Fh // tn), in_specs=in_specs, out_specs=[ts, ts],
                  out_shape=[jax.ShapeDtypeStruct((T, Fh), BF16)] * 2,
                  compiler_params=_params(("parallel", "parallel")))(*args)


def swiglu_fwd(g, u, name):
    T = g.shape[0]
    tq, tc = _tile(T, (512, 256, 128)), 512
    nf = FFN // tc

    def body(g_ref, u_ref, o_ref):
        gv = g_ref[...].astype(F32)
        o_ref[...] = (gv * _sigmoid(gv) * u_ref[...].astype(F32)).astype(BF16)

    lo = pl.BlockSpec((tq, tc), lambda i, j: (i, j))
    return _pcall(body, name=name, grid=(T // tq, nf), in_specs=[lo, lo], out_specs=lo,
                  out_shape=jax.ShapeDtypeStruct((T, FFN), BF16), compiler_params=_params(("parallel", "parallel")))(g, u)


def swiglu_bwd(gate, up, dact, name):
    T = gate.shape[0]
    tq, tc = _tile(T, (512, 256, 128)), 512
    nf = FFN // tc

    def body(g_ref, u_ref, d_ref, dg_ref, du_ref):
        g = g_ref[...].astype(F32)
        u = u_ref[...].astype(F32)
        d = d_ref[...]
        s = _sigmoid(g)
        dg_ref[...] = (d * u * (s * (1.0 + g * (1.0 - s)))).astype(BF16)
        du_ref[...] = (d * g * s).astype(BF16)

    lo = pl.BlockSpec((tq, tc), lambda i, j: (i, j))
    return _pcall(body, name=name, grid=(T // tq, nf), in_specs=[lo, lo, lo], out_specs=[lo, lo],
                  out_shape=[jax.ShapeDtypeStruct((T, FFN), BF16)] * 2,
                  compiler_params=_params(("parallel", "parallel")))(gate, up, dact)


def loss_head(y, target):
    T, D = y.shape
    tq = _tile(T, (256, 128))

    def body(y_ref, t_ref, dy_ref, l_ref, dyb_ref):
        @pl.when(pl.program_id(0) == 0)
        def _():
            l_ref[...] = jnp.zeros_like(l_ref)

        e = y_ref[...] - t_ref[...]
        dy = e * (1.0 / D)
        dy_ref[...] = dy
        dyb_ref[...] = dy.astype(BF16)
        l_ref[...] += 0.5 * jnp.sum(jnp.mean(e * e, axis=-1, keepdims=True))

    xs = pl.BlockSpec((tq, D), lambda i: (i, 0))
    return _pcall(body, name="loss_head", grid=(T // tq,), in_specs=[xs, xs],
                  out_specs=[xs, pl.BlockSpec((8, LANES), lambda i: (0, 0)), xs],
                  out_shape=[jax.ShapeDtypeStruct((T, D), F32), jax.ShapeDtypeStruct((8, LANES), F32),
                             jax.ShapeDtypeStruct((T, D), BF16)],
                  compiler_params=_params(("arbitrary",)))(y, target)


def sum_arrays(arrs, out_dtype, name):
    R, C = arrs[0].shape
    tr = _tile(R, (512, 256, 128, 64, 32, 16, 8))
    n = len(arrs)

    def body(*refs):
        acc = refs[0][...].astype(F32)
        for r in refs[1:n]:
            acc = acc + r[...].astype(F32)
        refs[n][...] = acc.astype(out_dtype)

    xs = pl.BlockSpec((tr, C), lambda i: (i, 0))
    return _pcall(body, name=name, grid=(R // tr,), in_specs=[xs] * n, out_specs=xs,
                  out_shape=jax.ShapeDtypeStruct((R, C), out_dtype), compiler_params=_params(("parallel",)))(*arrs)


def adamw(w, g, m, v, name):
    R, C = w.shape
    tr = _tile(R, (256, 128, 64, 32, 16, 8))
    c1 = 1.0 - ADAM_B1 ** ADAM_STEP
    c2 = 1.0 - ADAM_B2 ** ADAM_STEP

    def body(w_ref, g_ref, m_ref, v_ref, d_ref, nm_ref, nv_ref):
        gv = g_ref[...]
        mn = ADAM_B1 * m_ref[...] + (1.0 - ADAM_B1) * gv
        vn = ADAM_B2 * v_ref[...] + (1.0 - ADAM_B2) * (gv * gv)
        nm_ref[...] = mn
        nv_ref[...] = vn
        d_ref[...] = -ADAM_LR * ((mn / c1) / (jnp.sqrt(vn / c2) + ADAM_EPS) + ADAM_WD * w_ref[...])

    xs = pl.BlockSpec((tr, C), lambda i: (i, 0))
    return _pcall(body, name=name, grid=(R // tr,), in_specs=[xs] * 4, out_specs=[xs] * 3,
                  out_shape=[jax.ShapeDtypeStruct((R, C), F32)] * 3, compiler_params=_params(("parallel",)))(w, g, m, v)


def adamw_layer(w, g, m, v, l, prev, name):
    L, r, c = w.shape
    tr = _tile(r, (256, 128, 64, 32, 16, 8))
    c1 = 1.0 - ADAM_B1 ** ADAM_STEP
    c2 = 1.0 - ADAM_B2 ** ADAM_STEP
    chained = prev is not None

    def body(w_ref, g_ref, m_ref, v_ref, *rest):
        go_ref, d_ref, nm_ref, nv_ref = rest[-4:]
        gv = g_ref[...]
        mn = ADAM_B1 * m_ref[...] + (1.0 - ADAM_B1) * gv
        vn = ADAM_B2 * v_ref[...] + (1.0 - ADAM_B2) * (gv * gv)
        go_ref[...] = gv
        nm_ref[...] = mn
        nv_ref[...] = vn
        d_ref[...] = -ADAM_LR * ((mn / c1) / (jnp.sqrt(vn / c2) + ADAM_EPS) + ADAM_WD * w_ref[...])

    ls = pl.BlockSpec((None, tr, c), lambda i: (l, i, 0))
    gs = pl.BlockSpec((tr, c), lambda i: (i, 0))
    anys = [pl.BlockSpec(memory_space=pl.ANY)] * 4 if chained else []
    return _pcall(body, name=name, grid=(r // tr,), in_specs=[ls, gs, ls, ls] + anys, out_specs=[ls] * 4,
                  out_shape=[jax.ShapeDtypeStruct((L, r, c), F32)] * 4,
                  input_output_aliases={4: 0, 5: 1, 6: 2, 7: 3} if chained else {},
                  compiler_params=_params(("parallel",)))(w, g, m, v, *(prev if chained else ()))


def add_pair(g, got, name):
    _, r, c = g.shape
    h = r // 2

    def body(c_ref, g_ref, o_ref, out_ref):
        del c_ref
        out_ref[...] = (g_ref[...].astype(F32) + o_ref[...].astype(F32)).astype(BF16)

    spec = pltpu.PrefetchScalarGridSpec(
        num_scalar_prefetch=1, grid=(4,),
        in_specs=[pl.BlockSpec((None, h, c), lambda k, cs: (k, cs[0], 0)), pl.BlockSpec((None, h, c), lambda k, cs: (k, 0, 0))],
        out_specs=pl.BlockSpec((None, h, c), lambda k, cs: (k, 0, 0)))
    return _pcall(body, name=name, grid_spec=spec, out_shape=jax.ShapeDtypeStruct((4, h, c), BF16),
                  compiler_params=_params(("arbitrary",)))(lax.axis_index("c").reshape(1).astype(jnp.int32), g, got)


def add_chips(p, b, name):
    _, h, c = p.shape
    tr = _tile(h, (256, 128, 64, 32, 16, 8))

    def body(me_ref, p_ref, b0, b1, b2, b3, out_ref):
        me = me_ref[0]
        acc = jnp.zeros(out_ref.shape, F32)
        for k, bk in enumerate((b0, b1, b2, b3)):
            acc = acc + jnp.where(me == k, p_ref[...], bk[...]).astype(F32)
        out_ref[...] = acc

    slot = lambda k: pl.BlockSpec((None, tr, c), lambda i, ms: (jnp.where(ms[0] == k, (k + 1) % 4, k), i, 0))
    spec = pltpu.PrefetchScalarGridSpec(
        num_scalar_prefetch=1, grid=(h // tr,),
        in_specs=[pl.BlockSpec((None, tr, c), lambda i, ms: (ms[0], i, 0)), slot(0), slot(1), slot(2), slot(3)],
        out_specs=pl.BlockSpec((tr, c), lambda i, ms: (i, 0)))
    me = (2 * lax.axis_index("x") + lax.axis_index("y")).reshape(1).astype(jnp.int32)
    return _pcall(body, name=name, grid_spec=spec, out_shape=jax.ShapeDtypeStruct((h, c), F32),
                  compiler_params=_params(("arbitrary",)))(me, p, b, b, b, b)


ANY = pl.BlockSpec(memory_space=pl.ANY)


def _place():
    x, y, c = lax.axis_index("x"), lax.axis_index("y"), lax.axis_index("c")
    chips = [(1 - x, y), (x, 1 - y), (1 - x, 1 - y)]
    return x, y, c, chips


def _rcopy(src, dst, ssem, rsem, dev):
    return pltpu.make_async_remote_copy(src_ref=src, dst_ref=dst, send_sem=ssem, recv_sem=rsem, device_id=dev,
                                        device_id_type=MESH)


def _half(c, h):
    return pl.ds(pl.multiple_of(c * h, 64), h)


def _me_chip():
    return 2 * lax.axis_index("x") + lax.axis_index("y")


def all_gather_weights(shards, name):
    n = len(shards)
    L = shards[0].shape[0]
    hl = L // 2

    def body(*refs):
        srcs, outs, (ssem, rsem) = refs[:n], refs[n:2 * n], refs[2 * n:]
        x, y, c, chips = _place()
        me = 2 * x + y
        sib = (x, y, 1 - c)
        mine, other = pl.ds(c * hl, hl), pl.ds((1 - c) * hl, hl)
        sends = []
        for w in range(n):
            for j, (px, py) in enumerate(chips):
                cp = _rcopy(srcs[w].at[mine], outs[w].at[mine, me], ssem.at[w, j], rsem.at[w, j], (px, py, c))
                cp.start()
                sends.append(cp)
        for j, (px, py) in enumerate(chips):
            slot = 2 * px + py
            for w in range(n):
                _rcopy(srcs[w].at[mine], outs[w].at[mine, slot], ssem.at[w, j], rsem.at[w, j], (px, py, c)).wait_recv()
                cp = _rcopy(outs[w].at[mine, slot], outs[w].at[mine, slot], ssem.at[w, 3 + j], rsem.at[w, 3 + j], sib)
                cp.start()
                sends.append(cp)
        for j, (px, py) in enumerate(chips):
            slot = 2 * px + py
            for w in range(n):
                _rcopy(outs[w].at[other, slot], outs[w].at[other, slot], ssem.at[w, 3 + j], rsem.at[w, 3 + j], sib).wait_recv()
        for cp in sends:
            cp.wait_send()

    outs = _pcall(body, name=name, in_specs=[ANY] * n, out_specs=[ANY] * n,
                  out_shape=[jax.ShapeDtypeStruct((L, 4) + s.shape[1:], s.dtype) for s in shards],
                  scratch_shapes=[pltpu.SemaphoreType.DMA((n, 6)), pltpu.SemaphoreType.DMA((n, 6))])(*shards)
    me = _me_chip()
    return [lax.dynamic_update_slice(o, s[:, None], (0, me, 0, 0)) for o, s in zip(outs, shards)]


HBM = pl.BlockSpec(memory_space=pltpu.HBM)
SEM = pl.BlockSpec(memory_space=pltpu.SEMAPHORE)
EFFECT = pltpu.SideEffectType.DATAFLOW_SIDE_EFFECTING


def _ici_copy(kind, src, land, x, y, c, px, py, ssem, rsem, receiving):
    me, peer = 2 * x + y, 2 * px + py
    if kind == "gather":
        rows = _half(c, src.shape[0] // 2)
        s, d = src.at[rows], land.at[peer if receiving else me, rows]
    else:
        s, d = src.at[me if receiving else peer], land.at[peer if receiving else me]
    return _rcopy(s, d, ssem, rsem, (px, py, c))


def ici_start(kind, srcs, land_shapes, name):
    n = len(srcs)

    m = 3 * n

    def body(*refs):
        src, land, token = refs[:n], refs[n:2 * n], refs[-1]
        ssem, rsem = refs[2 * n:2 * n + m], refs[2 * n + m:2 * n + 2 * m]
        x, y, c, chips = _place()
        for w in range(n):
            for j, (px, py) in enumerate(chips):
                _ici_copy(kind, src[w], land[w], x, y, c, px, py, ssem[3 * w + j], rsem[3 * w + j], False).start()
        token[...] = jnp.zeros_like(token)

    lands = [pltpu.with_memory_space_constraint(lax.empty(shp, s.dtype), pltpu.HBM) for shp, s in zip(land_shapes, srcs)]
    srcs = [pltpu.with_memory_space_constraint(s, pltpu.HBM) for s in srcs]
    outs = _pcall(
        body, name=name,
        out_shape=(*[pltpu.SemaphoreType.DMA(())] * (2 * m),
                   *[pltpu.HBM(s.shape, s.dtype) for s in srcs], *[pltpu.HBM(z.shape, z.dtype) for z in lands],
                   jax.ShapeDtypeStruct((8, LANES), F32)),
        in_specs=[HBM] * (2 * n),
        out_specs=(*([SEM] * (2 * m)), *([HBM] * (2 * n)), pl.BlockSpec(memory_space=pltpu.VMEM)),
        input_output_aliases={i: 2 * m + i for i in range(2 * n)},
        compiler_params=pltpu.CompilerParams(has_side_effects=EFFECT))(*srcs, *lands)
    return dict(kind=kind, ssem=list(outs[:m]), rsem=list(outs[m:2 * m]), srcs=list(outs[2 * m:2 * m + n]),
                lands=list(outs[2 * m + n:2 * m + 2 * n])), outs[-1]


def ici_wait(state, after, name):
    kind, n = state["kind"], len(state["srcs"])

    m = 3 * n

    def body(*refs):
        src, land = refs[:n], refs[n:2 * n]
        ssem, rsem = refs[2 * n:2 * n + m], refs[2 * n + m:2 * n + 2 * m]
        x, y, c, chips = _place()
        for w in range(n):
            for j, (px, py) in enumerate(chips):
                _ici_copy(kind, src[w], land[w], x, y, c, px, py, ssem[3 * w + j], rsem[3 * w + j], False).wait_send()
                _ici_copy(kind, src[w], land[w], x, y, c, px, py, ssem[3 * w + j], rsem[3 * w + j], True).wait_recv()

    outs = _pcall(
        body, name=name,
        out_shape=tuple(pltpu.HBM(a.shape, a.dtype) for a in state["srcs"] + state["lands"]),
        in_specs=[HBM] * (2 * n) + [SEM] * (2 * m) + [ANY], out_specs=tuple([HBM] * (2 * n)),
        input_output_aliases={i: i for i in range(2 * n)},
        compiler_params=pltpu.CompilerParams(has_side_effects=EFFECT))(
            *state["srcs"], *state["lands"], *state["ssem"], *state["rsem"], after)
    return list(outs[:n]), list(outs[n:])


def forward_pair(lands, name):
    n = len(lands)

    def body(*refs):
        land, (ssem, rsem) = refs[:n], refs[2 * n:]
        x, y, c, chips = _place()
        sib = (x, y, 1 - c)
        cps = []
        for w in range(n):
            h = land[w].shape[1] // 2
            for j, (px, py) in enumerate(chips):
                part = land[w].at[2 * px + py, _half(c, h)]
                cp = _rcopy(part, part, ssem.at[w, j], rsem.at[w, j], sib)
                cp.start()
                cps.append(cp)
        for w in range(n):
            h = land[w].shape[1] // 2
            for j, (px, py) in enumerate(chips):
                part = land[w].at[2 * px + py, _half(1 - c, h)]
                _rcopy(part, part, ssem.at[w, j], rsem.at[w, j], sib).wait_recv()
        for cp in cps:
            cp.wait_send()

    return _pcall(body, name=name, in_specs=[ANY] * n, out_specs=[ANY] * n,
                  out_shape=[jax.ShapeDtypeStruct(z.shape, z.dtype) for z in lands],
                  input_output_aliases={i: i for i in range(n)},
                  scratch_shapes=[pltpu.SemaphoreType.DMA((n, 3)), pltpu.SemaphoreType.DMA((n, 3))])(*lands)


def exchange_pair(gs, name):
    n = len(gs)

    def body(*refs):
        srcs, outs, (ssem, rsem) = refs[:n], refs[n:2 * n], refs[2 * n:]
        x, y, c, _ = _place()
        cps = []
        for w in range(n):
            h = srcs[w].shape[1] // 2
            cp = _rcopy(srcs[w].at[:, _half(1 - c, h)], outs[w], ssem.at[w], rsem.at[w], (x, y, 1 - c))
            cp.start()
            cps.append(cp)
        for cp in cps:
            cp.wait()

    return _pcall(body, name=name, in_specs=[ANY] * n, out_specs=[ANY] * n,
                  out_shape=[jax.ShapeDtypeStruct((4, g.shape[1] // 2, g.shape[2]), g.dtype) for g in gs],
                  scratch_shapes=[pltpu.SemaphoreType.DMA((n,)), pltpu.SemaphoreType.DMA((n,))])(*gs)


def scatter_chips(ps, name):
    n = len(ps)

    def body(*refs):
        srcs, outs, (ssem, rsem) = refs[:n], refs[n:2 * n], refs[2 * n:]
        x, y, c, chips = _place()
        me = 2 * x + y
        sends = []
        for w in range(n):
            for j, (px, py) in enumerate(chips):
                cp = _rcopy(srcs[w].at[2 * px + py], outs[w].at[me], ssem.at[w, j], rsem.at[w, j], (px, py, c))
                cp.start()
                sends.append(cp)
        for w in range(n):
            for j, (px, py) in enumerate(chips):
                _rcopy(srcs[w].at[me], outs[w].at[2 * px + py], ssem.at[w, j], rsem.at[w, j], (px, py, c)).wait_recv()
        for cp in sends:
            cp.wait_send()

    return _pcall(body, name=name, in_specs=[ANY] * n, out_specs=[ANY] * n,
                  out_shape=[jax.ShapeDtypeStruct(p.shape, p.dtype) for p in ps],
                  scratch_shapes=[pltpu.SemaphoreType.DMA((n, 3)), pltpu.SemaphoreType.DMA((n, 3))])(*ps)


def join_pair(rs, name):
    n = len(rs)

    def body(*refs):
        srcs, outs, (ssem, rsem) = refs[:n], refs[n:2 * n], refs[2 * n:]
        x, y, c, _ = _place()
        cps = []
        for w in range(n):
            h = srcs[w].shape[0]
            cp = _rcopy(srcs[w], outs[w].at[_half(c, h)], ssem.at[w], rsem.at[w], (x, y, 1 - c))
            cp.start()
            cps.append(cp)
        for w in range(n):
            h = srcs[w].shape[0]
            _rcopy(srcs[w], outs[w].at[_half(1 - c, h)], ssem.at[w], rsem.at[w], (x, y, 1 - c)).wait_recv()
        for cp in cps:
            cp.wait_send()

    outs = _pcall(body, name=name, in_specs=[ANY] * n, out_specs=[ANY] * n,
                  out_shape=[jax.ShapeDtypeStruct((2 * r.shape[0], r.shape[1]), r.dtype) for r in rs],
                  scratch_shapes=[pltpu.SemaphoreType.DMA((n,)), pltpu.SemaphoreType.DMA((n,))])(*rs)
    c = lax.axis_index("c")
    return [lax.dynamic_update_slice(o, r, (c * r.shape[0], 0)) for o, r in zip(outs, rs)]


def gather_all_devices(v, name):
    R, C = v.shape

    def body(src, out, ssem, rsem, lsem):
        x, y, c, _ = _place()
        me = 4 * x + 2 * y + c
        local = pltpu.make_async_copy(src, out.at[me], lsem)
        local.start()
        sends = []
        for j in range(1, 8):
            dx, dy, dc = (j >> 2) & 1, (j >> 1) & 1, j & 1
            px, py, pc = (1 - x if dx else x), (1 - y if dy else y), (1 - c if dc else c)
            cp = _rcopy(src, out.at[me], ssem.at[j - 1], rsem.at[j - 1], (px, py, pc))
            cp.start()
            sends.append(cp)
        for j in range(1, 8):
            dx, dy, dc = (j >> 2) & 1, (j >> 1) & 1, j & 1
            px, py, pc = (1 - x if dx else x), (1 - y if dy else y), (1 - c if dc else c)
            _rcopy(src, out.at[4 * px + 2 * py + pc], ssem.at[j - 1], rsem.at[j - 1], (px, py, pc)).wait_recv()
        for cp in sends:
            cp.wait_send()
        local.wait()

    return _pcall(body, name=name, in_specs=[ANY], out_specs=ANY, out_shape=jax.ShapeDtypeStruct((8, R, C), v.dtype),
                  scratch_shapes=[pltpu.SemaphoreType.DMA((7,)), pltpu.SemaphoreType.DMA((7,)), pltpu.SemaphoreType.DMA(())])(v)


def reduce_scatter_begin(gw, tag):
    names = list(gw)
    gs = [gw[k] for k in names]
    got = exchange_pair(gs, "rs_pair_" + tag)
    ps = [add_pair(g, o, "rs_add_pair_%s_%s" % (k, tag)) for k, g, o in zip(names, gs, got)]
    state, token = ici_start("scatter", ps, [p.shape for p in ps], "rs_chips_start_" + tag)
    state["names"] = names
    return state, token


def reduce_scatter_end(state, after, tag):
    names = state["names"]
    ps, bs = ici_wait(state, after, "rs_chips_wait_" + tag)
    rs = [add_chips(p, b, "rs_add_chips_%s_%s" % (k, tag)) for k, p, b in zip(names, ps, bs)]
    return dict(zip(names, join_pair(rs, "rs_join_" + tag)))


GATHERED = ("w_in", "w_q_b", "w_kv_b", "w_out", "w_xq", "w_xk", "w_xv", "w_xo", "w_gate", "w_up", "w_down")
KIND = {"w_q_b": "col", "w_kv_b": "col", "w_out": "row", "w_xq": "row", "w_xk": "row", "w_xv": "row", "w_xo": "col",
        "w_gate": "col", "w_up": "col", "w_down": "row"}


def wq_to_storage(w):
    lead = w.shape[:-1]
    w = w.reshape(lead + (2, MLA_QK))
    rope = jnp.pad(w[..., MLA_NOPE:], [(0, 0)] * (w.ndim - 1) + [(0, LANES - MLA_ROPE)])
    return jnp.concatenate([w[..., :MLA_NOPE].reshape(lead + (2 * LANES,)), rope.reshape(lead + (2 * LANES,))], axis=-1)


def wq_from_storage(g):
    lead = g.shape[:-1]
    nope = g[..., :2 * LANES].reshape(lead + (2, LANES))
    rope = g[..., 2 * LANES:].reshape(lead + (2, LANES))[..., :MLA_ROPE]
    return jnp.concatenate([nope, rope], axis=-1).reshape(lead + (2 * MLA_QK,))


def wcat_from_storage(w_in_l):
    w = w_in_l.transpose(1, 0, 2).reshape(D_MODEL, IN_COLS)
    zc = lambda n: jnp.zeros((D_MODEL, n), w.dtype)
    return jnp.concatenate([w[:, :2560], w[:, 2576:], zc(64), w[:, 2560:2576], zc(112)], axis=1)


def w_in_grad_storage(dwcat):
    g = jnp.concatenate([dwcat[:, :2560], dwcat[:, C_DT:C_DT + 16], dwcat[:, 2560:3648]], axis=1)
    return g.reshape(D_MODEL, 4, IN_COLS // 4).transpose(1, 0, 2)


def _ssd_par(sm):
    rows = jnp.stack([sm["dt_bias"], sm["a_log"], sm["d_skip"]]).reshape(3, SSD_GROUPS, SSD_HPG).transpose(1, 0, 2)
    return jnp.pad(rows, ((0, 0), (0, 5), (0, LANES - SSD_HPG)))


def _dt_pad(proj, B, S):
    dt = proj[:, C_DT:C_DT + 16].reshape(B, S, SSD_GROUPS, SSD_HPG)
    return jnp.pad(dt, ((0, 0), (0, 0), (0, 0), (0, LANES - SSD_HPG))).reshape(B, S, SSD_GROUPS * LANES)


def _mla_gains(g):
    return g[:MLA_NOPE].reshape(1, LANES), jnp.pad(g[MLA_NOPE:], (0, LANES - MLA_ROPE)).reshape(1, LANES)


_Q_NOPE = lambda h: 4 * (h // 2) + h % 2
_Q_ROPE = lambda h: 4 * (h // 2) + 2 + h % 2
_K_NOPE = lambda h: 2 * h
_V_BLK = lambda h: 2 * h + 1


def layer_fwd(x, memf, cos, sin, G, kw, sm, B, S, l, dep=None):
    T = x.shape[0]
    n = lambda s: "%s_l%d" % (s, l)
    wv = lambda k: (KIND[k], 0)
    sv = dict(x0=x)
    h1 = norm_fwd(x, sm["attn_norm_g"], D_MODEL, False, n("attn_norm"), dep=dep)
    proj = mm(h1, kw["wcat"], "nn", n("in_proj"))
    mark = getattr(G, "mark", lambda v: None)
    mark(proj)
    sv.update(h1=h1, proj=proj)
    proj3 = proj.reshape(B, S, CAT_COLS)
    xbc = conv_fwd(proj3, C_XBC // 256, kw["conv_w"], sm["conv_b"].reshape(1, -1), n("conv"))
    dtp = _dt_pad(proj, B, S)
    par = _ssd_par(sm)
    y_raw, states = ssd_fwd(xbc, dtp, par, n("ssd"))
    z = proj[:, C_Z:C_Z + SSD_INNER]
    y_ssd = norm_fwd(y_raw.reshape(T, SSD_INNER), sm["ssd_norm_g"], SSD_INNER // SSD_GROUPS, False, n("ssd_norm"), gate=z)
    sv.update(xbc=xbc, states=states, y_raw=y_raw)
    qn = norm_fwd(proj[:, C_QA:C_QA + Q_LORA], sm["q_a_norm_g"], Q_LORA, False, n("q_a_norm"))
    kvn = norm_fwd(proj[:, C_KVA:C_KVA + KV_LORA], sm["kv_a_norm_g"], KV_LORA, False, n("kv_a_norm"))
    q_raw = mm(qn, G["w_q_b"], "nn", n("q_b"), wv=wv("w_q_b"))
    kv_raw = mm(kvn, G["w_kv_b"], "nn", n("kv_b"), wv=wv("w_kv_b"))
    gqn, gqr = _mla_gains(sm["mla_q_norm_g"])
    gkn, gkr = _mla_gains(sm["mla_k_norm_g"])
    qh = qk_fwd(q_raw, _Q_NOPE, q_raw, _Q_ROPE, False, gqn, gqr, cos, sin, n("q_norm_rope"))
    kh = qk_fwd(kv_raw, _K_NOPE, proj, C_KR // LANES, True, gkn, gkr, cos, sin, n("k_norm_rope"))
    vv = kv_raw.astype(BF16)
    y_mla = attn_fwd(qh.reshape(B, S, -1), kh.reshape(B, S, -1), vv.reshape(B, S, -1), MLA_HEADS, 2 * LANES, MLA_V,
                     MLA_QK ** -0.5, True, n("mla_attn"), v_idx=_V_BLK)
    ycat = jnp.concatenate([y_ssd, y_mla.reshape(T, -1)], axis=1)
    x1 = mm(ycat, G["w_out"], "nn", n("out_proj"), add=x, wv=wv("w_out"))
    sv.update(qn=qn, kvn=kvn, q_raw=q_raw, kv_raw=kv_raw, qh=qh, kh=kh, vv=vv, ycat=ycat, x1=x1)
    h2 = norm_fwd(x1, sm["xattn_norm_g"], D_MODEL, False, n("xattn_norm"))
    mn = norm_fwd(memf, sm["mem_norm_g"], D_MODEL, False, n("mem_norm"))
    xq = mm(h2, G["w_xq"], "nn", n("xq"), wv=wv("w_xq"))
    xk = mm(mn, G["w_xk"], "nn", n("xk"), wv=wv("w_xk"))
    xv = mm(mn, G["w_xv"], "nn", n("xv"), wv=wv("w_xv"), out_dtype=BF16)
    xqn = norm_fwd(xq, sm["xq_norm_g"], X_HEAD_DIM, True, n("xq_norm"))
    xkn = norm_fwd(xk, sm["xk_norm_g"], X_HEAD_DIM, True, n("xk_norm"))
    ML = memf.shape[0] // B
    xo = attn_fwd(xqn.reshape(B, S, -1), xkn.reshape(B, ML, -1), xv.reshape(B, ML, -1), X_HEADS, X_HEAD_DIM, X_HEAD_DIM,
                  X_HEAD_DIM ** -0.5, False, n("x_attn"))
    x2 = mm(xo.reshape(T, -1), G["w_xo"], "nn", n("xo_proj"), add=x1, wv=wv("w_xo"))
    mark(x2)
    sv.update(h2=h2, mn=mn, xq=xq, xk=xk, xqn=xqn, xkn=xkn, xv=xv, xo=xo, x2=x2)
    h3 = norm_fwd(x2, sm["ffn_norm_g"], D_MODEL, False, n("ffn_norm"))
    gate, up, act = gate_up_act(h3, G["w_gate"], G["w_up"], n("gate_up"))
    x3 = mm(act, G["w_down"], "nn", n("down"), add=x2, wv=wv("w_down"))
    sv.update(h3=h3, gate=gate, up=up, act=act)
    return x3, sv


def layer_bwd(dx3, dx3b, sv, memf, cos, sin, G, kw, sm, B, S, l, dep=None, rs_hook=None):
    T = dx3.shape[0]
    n = lambda s: "%s_bwd_l%d" % (s, l)
    wv = lambda k: (KIND[k], 0)
    dw = lambda k, a, b: mm(a, b, "tn", n(k + "_dw"), out_dtype=BF16, out_wv=KIND[k])
    gw, gs = {}, {}
    dgate, dup = down_dx_swiglu(dx3b, G["w_down"][0].reshape(FFN, D_MODEL), sv["gate"], sv["up"], n("down_dx"), dep=dep)
    gw["w_down"] = dw("w_down", sv["act"], dx3b)
    dh3 = mm(dgate, G["w_gate"], "nt", n("gate_dx"), wv=wv("w_gate"))
    dh3 = mm(dup, G["w_up"], "nt", n("up_dx"), wv=wv("w_up"), add=dh3)
    gw["w_gate"] = dw("w_gate", sv["h3"], dgate)
    gw["w_up"] = dw("w_up", sv["h3"], dup)
    dx2, gs["ffn_norm_g"], dx2b = norm_bwd(sv["x2"], sm["ffn_norm_g"], dh3, D_MODEL, False, n("ffn_norm"), add=dx3,
                                           bf16_copy=True)
    ffn_token = rs_hook({k: gw.pop(k) for k in ("w_down", "w_gate", "w_up")}) if rs_hook is not None else None
    dxo = mm(dx2b, G["w_xo"], "nt", n("xo_dx"), wv=wv("w_xo"), dep=ffn_token)
    gw["w_xo"] = dw("w_xo", sv["xo"].reshape(T, -1), dx2b)
    ML = memf.shape[0] // B
    dxqn, dxkn, dxv = attn_bwd(sv["xqn"].reshape(B, S, -1), sv["xkn"].reshape(B, ML, -1), sv["xv"].reshape(B, ML, -1),
                               dxo.reshape(B, S, -1), X_HEADS, X_HEAD_DIM, X_HEAD_DIM, X_HEAD_DIM ** -0.5, False, n("x_attn"))
    dxq, gs["xq_norm_g"] = norm_bwd(sv["xq"], sm["xq_norm_g"], dxqn.reshape(T, -1), X_HEAD_DIM, True, n("xq_norm"))
    dxk, gs["xk_norm_g"] = norm_bwd(sv["xk"], sm["xk_norm_g"], dxkn.reshape(B * ML, -1), X_HEAD_DIM, True, n("xk_norm"))
    dxv = dxv.reshape(B * ML, -1)
    dmn = mm(dxk, G["w_xk"], "nt", n("xk_dx"), wv=wv("w_xk"))
    dmn = mm(dxv, G["w_xv"], "nt", n("xv_dx"), wv=wv("w_xv"), add=dmn)
    gw["w_xk"] = dw("w_xk", sv["mn"], dxk)
    gw["w_xv"] = dw("w_xv", sv["mn"], dxv)
    _, gs["mem_norm_g"] = norm_bwd(memf, sm["mem_norm_g"], dmn, D_MODEL, False, n("mem_norm"))
    dh2 = mm(dxq, G["w_xq"], "nt", n("xq_dx"), wv=wv("w_xq"))
    gw["w_xq"] = dw("w_xq", sv["h2"], dxq)
    dx1, gs["xattn_norm_g"], dx1b = norm_bwd(sv["x1"], sm["xattn_norm_g"], dh2, D_MODEL, False, n("xattn_norm"), add=dx2,
                                             bf16_copy=True)
    dycat = mm(dx1b, G["w_out"], "nt", n("out_dx"), wv=wv("w_out"))
    gw["w_out"] = dw("w_out", sv["ycat"], dx1b)
    dqh, dkh, dvv = attn_bwd(sv["qh"].reshape(B, S, -1), sv["kh"].reshape(B, S, -1), sv["vv"].reshape(B, S, -1),
                             dycat[:, SSD_INNER:].reshape(B, S, -1), MLA_HEADS, 2 * LANES, MLA_V, MLA_QK ** -0.5, True,
                             n("mla_attn"), v_idx=_V_BLK)
    gqn, gqr = _mla_gains(sm["mla_q_norm_g"])
    gkn, gkr = _mla_gains(sm["mla_k_norm_g"])
    dq_raw, dgqn, dgqr = qk_bwd(sv["q_raw"], _Q_NOPE, sv["q_raw"], _Q_ROPE, False, gqn, gqr, cos, sin,
                                dqh.reshape(T, -1), n("q_norm_rope"))
    dkv_raw, dk_rope, dgkn, dgkr = qk_bwd(sv["kv_raw"], _K_NOPE, sv["proj"], C_KR // LANES, True, gkn, gkr, cos, sin,
                                          dkh.reshape(T, -1), n("k_norm_rope"), fill=dvv.reshape(T, -1), fill_blk=_V_BLK)
    gs["mla_q_norm_g"] = jnp.concatenate([dgqn[0], dgqr[0, :MLA_ROPE]])
    gs["mla_k_norm_g"] = jnp.concatenate([dgkn[0], dgkr[0, :MLA_ROPE]])
    dqn = mm(dq_raw, G["w_q_b"], "nt", n("q_b_dx"), wv=wv("w_q_b"))
    gw["w_q_b"] = dw("w_q_b", sv["qn"], dq_raw)
    dkvn = mm(dkv_raw, G["w_kv_b"], "nt", n("kv_b_dx"), wv=wv("w_kv_b"))
    gw["w_kv_b"] = dw("w_kv_b", sv["kvn"], dkv_raw)
    proj = sv["proj"]
    dq_a, gs["q_a_norm_g"] = norm_bwd(proj[:, C_QA:C_QA + Q_LORA], sm["q_a_norm_g"], dqn, Q_LORA, False, n("q_a_norm"))
    dkv_a, gs["kv_a_norm_g"] = norm_bwd(proj[:, C_KVA:C_KVA + KV_LORA], sm["kv_a_norm_g"], dkvn, KV_LORA, False,
                                        n("kv_a_norm"))
    z = proj[:, C_Z:C_Z + SSD_INNER]
    dy_raw, dz, gs["ssd_norm_g"] = norm_bwd(sv["y_raw"].reshape(T, -1), sm["ssd_norm_g"], dycat[:, :SSD_INNER],
                                            SSD_INNER // SSD_GROUPS, False, n("ssd_norm"), gate=z)
    dtp = _dt_pad(proj, B, S)
    par = _ssd_par(sm)
    dxs, dBm, dCm, ddt, dpar = ssd_bwd(sv["xbc"], dtp, par, sv["states"], dy_raw.reshape(B, S, -1), n("ssd"))
    gs["dt_bias"] = dpar[:, 0, :SSD_HPG].reshape(-1)
    gs["a_log"] = dpar[:, 1, :SSD_HPG].reshape(-1)
    gs["d_skip"] = dpar[:, 2, :SSD_HPG].reshape(-1)
    dxbc_act = jnp.concatenate([dxs, dBm, dCm], axis=2)
    dxbc, dconv_w, dconv_b = conv_bwd(proj.reshape(B, S, -1), C_XBC // 256, kw["conv_w"], sm["conv_b"].reshape(1, -1), dxbc_act,
                                      n("conv"))
    gs["conv_w"] = dconv_w[:SSD_CONV]
    gs["conv_b"] = dconv_b[0]
    ddt16 = ddt.reshape(T, SSD_GROUPS, LANES)[:, :, :SSD_HPG].reshape(T, 16)
    dproj = jnp.concatenate([dz, dxbc.reshape(T, -1), dq_a, dkv_a, dk_rope, ddt16, jnp.zeros((T, 112), F32)],
                            axis=1).astype(BF16)
    dh1 = mm(dproj, kw["wcat"], "nt", n("in_dx"))
    gw["w_in"] = w_in_grad_storage(mm(sv["h1"], dproj, "tn", n("in_dw"), out_dtype=BF16))
    dx0, gs["attn_norm_g"], dx0b = norm_bwd(sv["x0"], sm["attn_norm_g"], dh1, D_MODEL, False, n("attn_norm"), add=dx1,
                                            bf16_copy=True)
    gs = {k: v.reshape(-1) for k, v in gs.items()}
    return dx0, dx0b, gw, gs


def _small_pack(vals):
    flat = jnp.concatenate([vals[l][k].reshape(-1) for l in range(DEPTH) for k in SMALL_RS])
    n = flat.shape[0]
    rows = -(-n // LANES)
    rows = -(-rows // 8) * 8
    return jnp.pad(flat, (0, rows * LANES - n)).reshape(rows, LANES), n


def _small_unpack(flat, shapes):
    v = flat.reshape(-1)
    out, off = [dict() for _ in range(DEPTH)], 0
    for l in range(DEPTH):
        for k in SMALL_RS:
            sz = shapes[k]
            out[l][k] = v[off:off + sz]
            off += sz
    return out


def _adam_2d(a):
    if a.ndim == 2:
        return a
    return a.reshape(a.shape[0] * a.shape[1], a.shape[2])


def kernel(x, mem, positions, attn_norm_g, w_in, conv_w, conv_b, dt_bias, a_log, d_skip, ssd_norm_g, q_a_norm_g, w_q_b, kv_a_norm_g, w_kv_b, mla_q_norm_g, mla_k_norm_g, w_out, xattn_norm_g, mem_norm_g, w_xq, w_xk, w_xv, xq_norm_g, xk_norm_g, w_xo, ffn_norm_g, w_gate, w_up, w_down, loss_target, m_attn_norm_g, m_w_in, m_conv_w, m_conv_b, m_dt_bias, m_a_log, m_d_skip, m_ssd_norm_g, m_q_a_norm_g, m_w_q_b, m_kv_a_norm_g, m_w_kv_b, m_mla_q_norm_g, m_mla_k_norm_g, m_w_out, m_xattn_norm_g, m_mem_norm_g, m_w_xq, m_w_xk, m_w_xv, m_xq_norm_g, m_xk_norm_g, m_w_xo, m_ffn_norm_g, m_w_gate, m_w_up, m_w_down, v_attn_norm_g, v_w_in, v_conv_w, v_conv_b, v_dt_bias, v_a_log, v_d_skip, v_ssd_norm_g, v_q_a_norm_g, v_w_q_b, v_kv_a_norm_g, v_w_kv_b, v_mla_q_norm_g, v_mla_k_norm_g, v_w_out, v_xattn_norm_g, v_mem_norm_g, v_w_xq, v_w_xk, v_w_xv, v_xq_norm_g, v_xk_norm_g, v_w_xo, v_ffn_norm_g, v_w_gate, v_w_up, v_w_down):
    loc = locals()
    W = {k: loc[k] for k in WEIGHTS}
    Mo = {k: loc["m_" + k] for k in WEIGHTS}
    Vo = {k: loc["v_" + k] for k in WEIGHTS}
    B, S, D = x.shape
    T = B * S
    xf = x.reshape(T, D)
    memf = mem.reshape(-1, D)
    tf = loss_target.reshape(T, D)
    cos, sin = rope_tables(positions.reshape(T, 1).astype(F32))

    cw = gather_all_devices(conv_w.reshape(DEPTH * SSD_CONV, -1), "gather_conv_w")[0::2]
    cw = cw.transpose(1, 0, 2).reshape(DEPTH, SSD_CONV, SSD_CONV_DIM)
    sms = [{k: W[k][l] for k in SMALL} for l in range(DEPTH)]
    me = _me_chip()

    def gather_start(l):
        shards = [(wq_to_storage(W[k][l]) if k == "w_q_b" else W[k][l]).astype(BF16) for k in GATHERED]
        return ici_start("gather", shards, [(4,) + s.shape for s in shards], "gather_start_l%d" % l)

    def gather_end(state, names, after, tag):
        idx = [GATHERED.index(k) for k in names]
        pick3 = lambda sems: [sems[3 * w + j] for w in idx for j in range(3)]
        sub = dict(kind=state["kind"], srcs=[state["srcs"][w] for w in idx], lands=[state["lands"][w] for w in idx],
                   ssem=pick3(state["ssem"]), rsem=pick3(state["rsem"]))
        shards, lands = ici_wait(sub, after, "gather_wait_" + tag)
        lands = forward_pair(lands, "gather_forward_" + tag)
        return {k: lax.dynamic_update_slice(z, s[None], (me, 0, 0))[None] for k, z, s in zip(names, lands, shards)}

    class Weights(dict):
        def __init__(self, state, groups, after, tag):
            super().__init__(gather_end(state, groups[0], after, tag + "a"))
            self.state, self.later, self.after, self.tag = state, list(groups[1:]), after, tag

        def mark(self, value):
            self.after = value

        def __missing__(self, k):
            names = self.later.pop(0)
            self.update(gather_end(self.state, names, self.after, self.tag + "bcdef"[len(self.later)]))
            return self[k]

    FIRST_GROUPS = (("w_in",), ("w_q_b", "w_kv_b", "w_out", "w_xq", "w_xk", "w_xv", "w_xo"), ("w_gate", "w_up", "w_down"))

    saved, Gs, kws = [], [], []
    h = xf
    state, token = gather_start(0)
    for l in range(DEPTH):
        Gl = Weights(state, FIRST_GROUPS if l == 0 else (GATHERED,), h, "l%d" % l)
        kwl = dict(wcat=wcat_from_storage(Gl["w_in"][0]), conv_w=jnp.pad(cw[l], ((0, 8 - SSD_CONV), (0, 0))))
        Gs.append(Gl)
        kws.append(kwl)
        if l + 1 < DEPTH:
            state, token = gather_start(l + 1)
        h, sv = layer_fwd(h, memf, cos, sin, Gl, kwl, sms[l], B, S, l, dep=token if l + 1 < DEPTH else None)
        saved.append(sv)
    dh, lpart, dhb = loss_head(h, tf)
    loss = lax.psum(lpart[0, 0], ("x", "y", "c"))

    grads, delta, new_m, new_v = {}, {}, {}, {}
    chain = {k: None for k in GATHERED}
    gsmall = [None] * DEPTH
    pending = [[] for _ in range(DEPTH)]

    def begin(gw, l, tag):
        rs_state, tok = reduce_scatter_begin(gw, "l%d%s" % (l, tag))
        pending[l].append((rs_state, "l%d%s" % (l, tag)))
        return tok

    def finish(l, after):
        for rs_state, tag in pending[l]:
            red = reduce_scatter_end(rs_state, after, tag)
            for k, g in red.items():
                g = wq_from_storage(g) if k == "w_q_b" else g
                chain[k] = adamw_layer(W[k], g, Mo[k], Vo[k], l, chain[k], "adamw_%s_l%d" % (k, l))

    token = None
    for l in reversed(range(DEPTH)):
        dh, dhb, gw, gs = layer_bwd(dh, dhb, saved[l], memf, cos, sin, Gs[l], kws[l], sms[l], B, S, l, dep=token,
                                    rs_hook=functools.partial(begin, l=l, tag="_ffn"))
        saved[l] = None
        gsmall[l] = gs
        token = begin(gw, l, "_mix")
        if l + 1 < DEPTH:
            finish(l + 1, token)
    grad_x = dh.reshape(B, S, D)

    spack, _ = _small_pack(gsmall)
    allv = gather_all_devices(spack, "gather_small_grads")
    ssum = sum_arrays([allv[i] for i in range(8)], F32, "sum_small_grads")
    sizes = {k: int(np.prod(W[k].shape[1:])) for k in SMALL}
    sizes["conv_w"] = SSD_CONV * SSD_CONV_DIM
    gsm = _small_unpack(ssum, sizes)
    for k in SMALL_RS:
        g = jnp.stack([gsm[l][k] for l in range(DEPTH)])
        if k == "conv_w":
            cs = SSD_CONV_DIM // 4
            g = lax.dynamic_slice_in_dim(g.reshape(DEPTH, SSD_CONV, SSD_CONV_DIM), _me_chip() * cs, cs, axis=2)
        shp = W[k].shape
        grads[k] = g.reshape(shp)
        d, nm, nv = adamw(_adam_2d(W[k]), _adam_2d(grads[k]), _adam_2d(Mo[k]), _adam_2d(Vo[k]), "adamw_" + k)
        delta[k], new_m[k], new_v[k] = d.reshape(shp), nm.reshape(shp), nv.reshape(shp)

    finish(0, delta["ffn_norm_g"])
    for k in GATHERED:
        grads[k], delta[k], new_m[k], new_v[k] = chain[k]

    return (loss, grad_x, *[grads[k] for k in WEIGHTS], *[delta[k] for k in WEIGHTS], *[new_m[k] for k in WEIGHTS],
            *[new_v[k] for k in WEIGHTS])
```

```python
import functools
import math

import numpy as np
import jax
import jax.numpy as jnp
from jax import lax
from jax.experimental import pallas as pl
from jax.experimental.pallas import tpu as pltpu

F32 = jnp.float32
BF16 = jnp.bfloat16
MESH = pl.DeviceIdType.MESH
HI = lax.Precision.HIGHEST

D_MODEL = 2048
DEPTH = 4
SSD_INNER = 1024
SSD_HEAD_DIM = 64
SSD_GROUPS = 2
SSD_HPG = 8
SSD_STATE = 128
SSD_CONV = 4
SSD_CHUNK = 128
SSD_CONV_DIM = 1536
SSD_GW = SSD_HPG * SSD_HEAD_DIM
MLA_HEADS = 8
MLA_NOPE = 128
MLA_ROPE = 64
MLA_QK = 192
MLA_V = 128
Q_LORA = 512
KV_LORA = 512
ROPE_THETA = 10000.0
X_HEADS = 4
X_HEAD_DIM = 128
X_INNER = 512
FFN = 5632
IN_COLS = 3664
EPS = 1e-6
ADAM_LR, ADAM_B1, ADAM_B2, ADAM_EPS, ADAM_WD, ADAM_STEP = 0.001, 0.9, 0.999, 1e-08, 0.01, 10

C_Z, C_XBC, C_QA, C_KVA, C_KR, C_DT, CAT_COLS = 0, 1024, 2560, 3072, 3584, 3712, 3840

LANES = 128
VMEM_LIMIT = 56 * 1024 * 1024
MM_VMEM_BUDGET = 40 * 1024 * 1024
MM_STEP_BYTES = 1024 * 1024
NORM_BLOCK_ELEMS = 512 * 1024
ATTN_SCORE_ELEMS = 512 * 1024

SMALL = ("attn_norm_g", "conv_b", "dt_bias", "a_log", "d_skip", "ssd_norm_g", "q_a_norm_g", "kv_a_norm_g",
         "mla_q_norm_g", "mla_k_norm_g", "xattn_norm_g", "mem_norm_g", "xq_norm_g", "xk_norm_g", "ffn_norm_g")
SMALL_RS = SMALL + ("conv_w",)
WEIGHTS = ("attn_norm_g", "w_in", "conv_w", "conv_b", "dt_bias", "a_log", "d_skip", "ssd_norm_g", "q_a_norm_g", "w_q_b",
           "kv_a_norm_g", "w_kv_b", "mla_q_norm_g", "mla_k_norm_g", "w_out", "xattn_norm_g", "mem_norm_g", "w_xq", "w_xk",
           "w_xv", "xq_norm_g", "xk_norm_g", "w_xo", "ffn_norm_g", "w_gate", "w_up", "w_down")


def _pcall(body, **kw):
    return pl.pallas_call(body, **kw)


def _params(sem):
    return pltpu.CompilerParams(dimension_semantics=sem, vmem_limit_bytes=VMEM_LIMIT)


def _tile(n, cands):
    for t in cands:
        if n % t == 0:
            return t
    return n


def _sigmoid(z):
    return 1.0 / (1.0 + jnp.exp(-z))


def _mtile(n, cap):
    if n % LANES:
        return n
    q = n // LANES
    t = LANES * max(d for d in range(1, q + 1) if q % d == 0 and LANES * d <= cap)
    return n if (t < 512 and n <= 2048) else t


def _divisor_tiles(n, cap):
    if n % LANES:
        return [n]
    q = n // LANES
    return [LANES * d for d in range(1, q + 1) if q % d == 0 and LANES * d <= cap] or [n]


def _mm_tiles(M, N, K, m_unit, n_unit, k_unit, a_item, b_item, o_item, has_add):
    best = None
    for tm in _divisor_tiles(m_unit, 2048):
        for tn in _divisor_tiles(n_unit, 2048):
            for tk in _divisor_tiles(k_unit, 2048):
                nk = K // tk
                vmem = 2 * (tm * tk * a_item + tk * tn * b_item + tm * tn * (o_item + (4 if has_add else 0)))
                vmem += tm * tn * 4 * (2 if nk > 1 else 1) + (tm * tk * 2 if a_item == 4 else 0) + (tk * tn * 2 if b_item == 4 else 0)
                if vmem > MM_VMEM_BUDGET:
                    continue
                steps = (M // tm) * (N // tn) * nk
                traffic = (M * K * a_item * (N // tn if nk > 1 else 1) + K * N * b_item * (M // tm)
                           + M * N * (o_item + (4 if has_add else 0)) + steps * MM_STEP_BYTES)
                if nk > 1:
                    traffic += steps * tm * tn * 4
                if best is None or traffic < best[0]:
                    best = (traffic, tm, tn, tk)
    assert best is not None, (M, N, K)
    return best[1:]


def mm(a, b, mode, name, add=None, out_dtype=F32, wv=None, out_wv=None, dep=None):
    kind, l = wv if wv is not None else (None, None)
    if kind == "row":
        b, kind = b[l].reshape(-1, b.shape[-1]), None
    if out_wv == "row":
        res = mm(a, b, mode, name, add=add, out_dtype=out_dtype, wv=wv if kind else None, dep=dep)
        return res.reshape(4, res.shape[0] // 4, res.shape[1])
    if kind == "col":
        _, _, r, c = b.shape
        bshape = (r, 4 * c)
    else:
        bshape = b.shape
    if mode == "nn":
        (M, K), (K2, N) = a.shape, bshape
    elif mode == "nt":
        (M, K), (N, K2) = a.shape, bshape
    else:
        (K, M), (K2, N) = a.shape, bshape
    assert K == K2, (a.shape, bshape, mode)
    m_unit = M
    n_unit = N // 4 if (out_wv == "col" or (kind == "col" and mode == "nn")) else N
    k_unit = K // 4 if (kind == "col" and mode == "nt") else K
    tm, tn, tk = _mm_tiles(M, N, K, m_unit, n_unit, k_unit, a.dtype.itemsize, b.dtype.itemsize,
                           jnp.dtype(out_dtype).itemsize, add is not None)
    nk = K // tk
    qm, qn, qk = m_unit // tm, n_unit // tn, k_unit // tk
    dn = {"nn": (((1,), (0,)), ((), ())), "nt": (((1,), (1,)), ((), ())), "tn": (((0,), (0,)), ((), ()))}[mode]
    has_add = add is not None

    def body(a_ref, b_ref, *rest):
        rest = list(rest)
        c_ref = rest.pop(0) if has_add else None
        if dep is not None:
            rest.pop(0)
        o_ref = rest.pop(0)
        d = lax.dot_general(a_ref[...].astype(BF16), b_ref[...].astype(BF16), dn, preferred_element_type=F32)

        def finish(res):
            if has_add:
                res = res + c_ref[...].astype(F32)
            o_ref[...] = res.astype(out_dtype)

        if nk == 1:
            finish(d)
            return
        acc = rest.pop(0)
        k = pl.program_id(2)

        @pl.when(k == 0)
        def _():
            acc[...] = d

        if nk > 2:
            @pl.when((k > 0) & (k < nk - 1))
            def _():
                acc[...] += d

        @pl.when(k == nk - 1)
        def _():
            finish(acc[...] + d)

    if mode == "tn":
        a_spec = pl.BlockSpec((tk, tm), lambda i, j, k: (k, i))
    else:
        a_spec = pl.BlockSpec((tm, tk), lambda i, j, k: (i, k))
    if kind is None:
        b_spec = (pl.BlockSpec((tn, tk), lambda i, j, k: (j, k)) if mode == "nt"
                  else pl.BlockSpec((tk, tn), lambda i, j, k: (k, j)))
    elif mode == "nn":
        b_spec = pl.BlockSpec((None, None, tk, tn), lambda i, j, k: (l, j // qn, k, j % qn))
    elif mode == "nt":
        b_spec = pl.BlockSpec((None, None, tn, tk), lambda i, j, k: (l, k // qk, j, k % qk))
    else:
        raise ValueError((mode, kind))
    if out_wv == "col":
        o_spec = pl.BlockSpec((None, tm, tn), lambda i, j, k: (j // qn, i, j % qn))
        out_shape = jax.ShapeDtypeStruct((4, M, N // 4), out_dtype)
    else:
        o_spec = pl.BlockSpec((tm, tn), lambda i, j, k: (i, j))
        out_shape = jax.ShapeDtypeStruct((M, N), out_dtype)
    in_specs = [a_spec, b_spec] + ([o_spec] if has_add else []) + ([pl.BlockSpec(memory_space=pl.ANY)] if dep is not None else [])
    args = (a, b) + ((add,) if has_add else ()) + ((dep,) if dep is not None else ())
    return _pcall(
        body, name=name, grid=(M // tm, N // tn, nk), in_specs=in_specs, out_specs=o_spec,
        out_shape=out_shape, scratch_shapes=[pltpu.VMEM((tm, tn), F32)] if nk > 1 else [],
        compiler_params=_params(("parallel", "parallel", "arbitrary")))(*args)


def norm_fwd(x, gain, W, shared, name, gate=None, out_dtype=BF16, dep=None):
    T, C = x.shape
    nb = C // W
    tq = _tile(T, tuple(t for t in (4096, 2048, 1024, 512, 256, 128) if t * W <= NORM_BLOCK_ELEMS))
    g2 = gain.reshape(1, -1).astype(F32)
    gated = gate is not None

    def body(*refs):
        o_ref = refs[-1]
        if gated:
            x_ref, z_ref, g_ref = refs[:3]
        else:
            x_ref, g_ref = refs[:2]
        u = x_ref[...].astype(F32)
        if gated:
            z = z_ref[...]
            u = u * (z * _sigmoid(z))
        ms = jnp.mean(u * u, axis=-1, keepdims=True)
        o_ref[...] = (u * lax.rsqrt(ms + EPS) * g_ref[...]).astype(out_dtype)

    xs = pl.BlockSpec((tq, W), lambda j, i: (i, j))
    gs = pl.BlockSpec((1, W), (lambda j, i: (0, 0)) if shared else (lambda j, i: (0, j)))
    in_specs = [xs] + ([xs] if gated else []) + [gs] + ([pl.BlockSpec(memory_space=pl.ANY)] if dep is not None else [])
    args = (x,) + ((gate,) if gated else ()) + (g2,) + ((dep,) if dep is not None else ())
    return _pcall(body, name=name, grid=(nb, T // tq), in_specs=in_specs, out_specs=xs,
                  out_shape=jax.ShapeDtypeStruct((T, C), out_dtype),
                  compiler_params=_params(("parallel", "parallel")))(*args)


def norm_bwd(x, gain, dy, W, shared, name, gate=None, add=None, bf16_copy=False):
    T, C = x.shape
    nb = C // W
    tq = _tile(T, tuple(t for t in (4096, 2048, 1024, 512, 256, 128) if t * W <= NORM_BLOCK_ELEMS))
    g2 = gain.reshape(1, -1).astype(F32)
    gated = gate is not None
    has_add = add is not None
    GC = W if shared else C

    def body(*refs):
        refs = list(refs)
        x_ref = refs.pop(0)
        z_ref = refs.pop(0) if gated else None
        g_ref = refs.pop(0)
        dy_ref = refs.pop(0)
        a_ref = refs.pop(0) if has_add else None
        dx_ref = refs.pop(0)
        dz_ref = refs.pop(0) if gated else None
        dg_ref = refs.pop(0)
        j, i = pl.program_id(0), pl.program_id(1)
        first = (i == 0) & (j == 0) if shared else (i == 0)

        @pl.when(first)
        def _():
            dg_ref[...] = jnp.zeros_like(dg_ref)

        xv = x_ref[...].astype(F32)
        if gated:
            z = z_ref[...]
            s = _sigmoid(z)
            u = xv * (z * s)
        else:
            u = xv
        rstd = lax.rsqrt(jnp.mean(u * u, axis=-1, keepdims=True) + EPS)
        uh = u * rstd
        dyv = dy_ref[...].astype(F32)
        dg_ref[...] += jnp.sum(dyv * uh, axis=0, keepdims=True)
        dyg = dyv * g_ref[...]
        du = rstd * (dyg - uh * jnp.mean(dyg * uh, axis=-1, keepdims=True))
        if gated:
            dx = du * (z * s)
            dz_ref[...] = du * xv * (s * (1.0 + z * (1.0 - s)))
        else:
            dx = du
        if has_add:
            dx = dx + a_ref[...]
        dx_ref[...] = dx
        if bf16_copy:
            refs.pop(0)[...] = dx.astype(BF16)

    xs = pl.BlockSpec((tq, W), lambda j, i: (i, j))
    gs = pl.BlockSpec((1, W), (lambda j, i: (0, 0)) if shared else (lambda j, i: (0, j)))
    in_specs = [xs] + ([xs] if gated else []) + [gs, xs] + ([xs] if has_add else [])
    args = (x,) + ((gate,) if gated else ()) + (g2, dy) + ((add,) if has_add else ())
    out_specs = [xs] + ([xs] if gated else []) + [gs] + ([xs] if bf16_copy else [])
    out_shape = [jax.ShapeDtypeStruct((T, C), F32)] + ([jax.ShapeDtypeStruct((T, C), F32)] if gated else []) \
        + [jax.ShapeDtypeStruct((1, GC), F32)] + ([jax.ShapeDtypeStruct((T, C), BF16)] if bf16_copy else [])
    return _pcall(body, name=name, grid=(nb, T // tq), in_specs=in_specs, out_specs=out_specs, out_shape=out_shape,
                  compiler_params=_params(("arbitrary", "arbitrary")))(*args)


def _inv_freq_row():
    inv = 1.0 / (ROPE_THETA ** (np.arange(0, MLA_ROPE, 2, dtype=np.float32) / MLA_ROPE))
    row = np.zeros((1, LANES), np.float32)
    row[0, :32] = inv
    row[0, 32:64] = inv
    return jnp.asarray(row)


def rope_tables(pos):
    T = pos.shape[0]
    tq = _tile(T, (512, 256, 128))

    def body(p_ref, f_ref, c_ref, s_ref):
        ang = p_ref[...] * f_ref[...]
        lane = lax.broadcasted_iota(jnp.int32, ang.shape, 1)
        c_ref[...] = jnp.where(lane < 64, jnp.cos(ang), 0.0)
        sn = jnp.sin(ang)
        s_ref[...] = jnp.where(lane < 32, -sn, jnp.where(lane < 64, sn, 0.0))

    ts = pl.BlockSpec((tq, LANES), lambda i: (i, 0))
    return _pcall(body, name="rope_tables", grid=(T // tq,),
                  in_specs=[pl.BlockSpec((tq, 1), lambda i: (i, 0)), pl.BlockSpec((1, LANES), lambda i: (0, 0))],
                  out_specs=[ts, ts], out_shape=[jax.ShapeDtypeStruct((T, LANES), F32)] * 2,
                  compiler_params=_params(("parallel",)))(pos, _inv_freq_row())


def _swap_halves(r):
    lane = lax.broadcasted_iota(jnp.int32, r.shape, 1)
    return jnp.where(lane < 32, pltpu.roll(r, 96, 1), pltpu.roll(r, 32, 1))


def qk_fwd(nope_arr, nope_blk0, rope_arr, rope_blk0, rope_shared, gn, gr, cos, sin, name):
    T = nope_arr.shape[0]
    tq = _tile(T, (256, 128))

    col = lambda blk: slice(blk * LANES, (blk + 1) * LANES)

    def body(n_ref, r_ref, gn_ref, gr_ref, c_ref, s_ref, o_ref):
        for h in range(MLA_HEADS):
            n = n_ref[:, col(nope_blk0(h))]
            r = r_ref[...] if rope_shared else r_ref[:, col(rope_blk0(h))]
            ms = (jnp.sum(n * n, axis=-1, keepdims=True) + jnp.sum(r * r, axis=-1, keepdims=True)) * (1.0 / MLA_QK)
            rstd = lax.rsqrt(ms + EPS)
            rn = r * rstd * gr_ref[...]
            rr = rn * c_ref[...] + _swap_halves(rn) * s_ref[...]
            o_ref[:, col(2 * h)] = (n * rstd * gn_ref[...]).astype(BF16)
            o_ref[:, col(2 * h + 1)] = rr.astype(BF16)

    wide = lambda arr: pl.BlockSpec((tq, arr.shape[1]), lambda i: (i, 0))
    rspec = pl.BlockSpec((tq, LANES), lambda i: (i, rope_blk0)) if rope_shared else wide(rope_arr)
    row = pl.BlockSpec((1, LANES), lambda i: (0, 0))
    tab = pl.BlockSpec((tq, LANES), lambda i: (i, 0))
    return _pcall(body, name=name, grid=(T // tq,),
                  in_specs=[wide(nope_arr), rspec, row, row, tab, tab],
                  out_specs=pl.BlockSpec((tq, MLA_HEADS * 2 * LANES), lambda i: (i, 0)),
                  out_shape=jax.ShapeDtypeStruct((T, MLA_HEADS * 2 * LANES), BF16),
                  compiler_params=_params(("parallel",)))(nope_arr, rope_arr, gn, gr, cos, sin)


def qk_bwd(nope_arr, nope_blk0, rope_arr, rope_blk0, rope_shared, gn, gr, cos, sin, dout, name, fill=None, fill_blk=None):
    T = nope_arr.shape[0]
    tq = _tile(T, (256, 128))
    col = lambda blk: slice(blk * LANES, (blk + 1) * LANES)

    def body(*refs):
        refs = list(refs)
        n_ref, r_ref, gn_ref, gr_ref, c_ref, s_ref, d_ref = refs[:7]
        f_ref = refs[7] if rope_shared else None
        outs = refs[8:] if rope_shared else refs[7:]
        draw_ref = outs[0]
        dsh_ref = outs[1] if rope_shared else None
        dgn_ref, dgr_ref = outs[-2:]

        @pl.when(pl.program_id(0) == 0)
        def _():
            dgn_ref[...] = jnp.zeros_like(dgn_ref)
            dgr_ref[...] = jnp.zeros_like(dgr_ref)

        dgn = jnp.zeros((1, LANES), F32)
        dgr = jnp.zeros((1, LANES), F32)
        dshared = jnp.zeros((tq, LANES), F32)
        for h in range(MLA_HEADS):
            n = n_ref[:, col(nope_blk0(h))]
            r = r_ref[...] if rope_shared else r_ref[:, col(rope_blk0(h))]
            ms = (jnp.sum(n * n, axis=-1, keepdims=True) + jnp.sum(r * r, axis=-1, keepdims=True)) * (1.0 / MLA_QK)
            rstd = lax.rsqrt(ms + EPS)
            nh = n * rstd
            rh = r * rstd
            d_n = d_ref[:, col(2 * h)].astype(F32)
            d_rr = d_ref[:, col(2 * h + 1)].astype(F32)
            d_rn = d_rr * c_ref[...] - _swap_halves(d_rr) * s_ref[...]
            dgn = dgn + jnp.sum(d_n * nh, axis=0, keepdims=True)
            dgr = dgr + jnp.sum(d_rn * rh, axis=0, keepdims=True)
            dng = d_n * gn_ref[...]
            drg = d_rn * gr_ref[...]
            mean = (jnp.sum(dng * nh, axis=-1, keepdims=True) + jnp.sum(drg * rh, axis=-1, keepdims=True)) * (1.0 / MLA_QK)
            draw_ref[:, col(nope_blk0(h))] = (rstd * (dng - nh * mean)).astype(BF16)
            drope = rstd * (drg - rh * mean)
            if rope_shared:
                dshared = dshared + drope
                draw_ref[:, col(fill_blk(h))] = f_ref[:, col(h)].astype(BF16)
            else:
                draw_ref[:, col(rope_blk0(h))] = drope.astype(BF16)
        dgn_ref[...] += dgn
        dgr_ref[...] += dgr
        if rope_shared:
            dsh_ref[...] = dshared

    wide = lambda arr: pl.BlockSpec((tq, arr.shape[1]), lambda i: (i, 0))
    rspec = pl.BlockSpec((tq, LANES), lambda i: (i, rope_blk0)) if rope_shared else wide(rope_arr)
    row = pl.BlockSpec((1, LANES), lambda i: (0, 0))
    tab = pl.BlockSpec((tq, LANES), lambda i: (i, 0))
    raw = pl.BlockSpec((tq, 2 * MLA_HEADS * LANES), lambda i: (i, 0))
    in_specs = [wide(nope_arr), rspec, row, row, tab, tab, wide(dout)] + ([wide(fill)] if rope_shared else [])
    out_specs = [raw] + ([tab] if rope_shared else []) + [row, row]
    out_shape = ([jax.ShapeDtypeStruct((T, 2 * MLA_HEADS * LANES), BF16)]
                 + ([jax.ShapeDtypeStruct((T, LANES), F32)] if rope_shared else [])
                 + [jax.ShapeDtypeStruct((1, LANES), F32)] * 2)
    args = (nope_arr, rope_arr, gn, gr, cos, sin, dout) + ((fill,) if rope_shared else ())
    return _pcall(body, name=name, grid=(T // tq,), in_specs=in_specs, out_specs=out_specs, out_shape=out_shape,
                  compiler_params=_params(("arbitrary",)))(*args)


def _scores(q, k, scale, causal, row0):
    s = lax.dot_general(q, k, (((1,), (1,)), ((), ())), preferred_element_type=F32) * scale
    if causal:
        qi = row0 + lax.broadcasted_iota(jnp.int32, s.shape, 0)
        ki = lax.broadcasted_iota(jnp.int32, s.shape, 1)
        s = jnp.where(qi >= ki, s, -1e30)
    return s


def attn_fwd(q, k, v, H, dqk, dv, scale, causal, name, v_idx=lambda h: h):
    B, S, _ = q.shape
    Sk = k.shape[1]
    tq = _tile(S, tuple(t for t in (2048, 1024, 512, 256, 128) if t * Sk <= ATTN_SCORE_ELEMS))

    def tile(q_ref, k_ref, v_ref, o_ref, row0, kl):
        s = _scores(q_ref[0], k_ref[0, :kl], scale, causal, row0)
        p = jnp.exp(s - jnp.max(s, axis=-1, keepdims=True))
        l = jnp.sum(p, axis=-1, keepdims=True)
        o = lax.dot_general(p.astype(BF16), v_ref[0, :kl], (((1,), (0,)), ((), ())), preferred_element_type=F32)
        o_ref[0] = (o / l).astype(BF16)

    def body(q_ref, k_ref, v_ref, o_ref):
        if not causal:
            tile(q_ref, k_ref, v_ref, o_ref, 0, Sk)
            return
        for qi in range(S // tq):
            @pl.when(pl.program_id(2) == qi)
            def _(qi=qi):
                tile(q_ref, k_ref, v_ref, o_ref, qi * tq, (qi + 1) * tq)

    return _pcall(body, name=name, grid=(B, H, S // tq),
                  in_specs=[pl.BlockSpec((1, tq, dqk), lambda b, h, i: (b, i, h)),
                            pl.BlockSpec((1, Sk, dqk), lambda b, h, i: (b, 0, h)),
                            pl.BlockSpec((1, Sk, dv), lambda b, h, i: (b, 0, v_idx(h)))],
                  out_specs=pl.BlockSpec((1, tq, dv), lambda b, h, i: (b, i, h)),
                  out_shape=jax.ShapeDtypeStruct((B, S, H * dv), BF16),
                  compiler_params=_params(("parallel", "parallel", "parallel")))(q, k, v)


def attn_bwd(q, k, v, do, H, dqk, dv, scale, causal, name, v_idx=lambda h: h):
    B, S, _ = q.shape
    Sk = k.shape[1]
    tq = _tile(S, tuple(t for t in (2048, 1024, 512, 256, 128) if t * Sk <= ATTN_SCORE_ELEMS))

    def body(q_ref, k_ref, v_ref, do_ref, dq_ref, dk_ref, dv_ref):
        i = pl.program_id(2)

        @pl.when(i == 0)
        def _():
            dk_ref[...] = jnp.zeros_like(dk_ref)
            dv_ref[...] = jnp.zeros_like(dv_ref)

        def tile(row0, kl):
            qv, kv, vv = q_ref[0], k_ref[0, :kl], v_ref[0, :kl]
            s = _scores(qv, kv, scale, causal, row0)
            p = jnp.exp(s - jnp.max(s, axis=-1, keepdims=True))
            p = p / jnp.sum(p, axis=-1, keepdims=True)
            dob = do_ref[0].astype(BF16)
            pb = p.astype(BF16)
            dv_ref[0, :kl] += lax.dot_general(pb, dob, (((0,), (0,)), ((), ())), preferred_element_type=F32)
            dp = lax.dot_general(dob, vv, (((1,), (1,)), ((), ())), preferred_element_type=F32)
            delta = jnp.sum(p * dp, axis=-1, keepdims=True)
            ds = (p * (dp - delta) * scale).astype(BF16)
            dq_ref[0] = lax.dot_general(ds, kv, (((1,), (0,)), ((), ())), preferred_element_type=F32)
            dk_ref[0, :kl] += lax.dot_general(ds, qv, (((0,), (0,)), ((), ())), preferred_element_type=F32)

        if not causal:
            tile(0, Sk)
            return
        for qi in range(S // tq):
            @pl.when(i == qi)
            def _(qi=qi):
                tile(qi * tq, (qi + 1) * tq)

    qs = pl.BlockSpec((1, tq, dqk), lambda b, h, i: (b, i, h))
    ks = pl.BlockSpec((1, Sk, dqk), lambda b, h, i: (b, 0, h))
    vs = pl.BlockSpec((1, Sk, dv), lambda b, h, i: (b, 0, h))
    vin = pl.BlockSpec((1, Sk, dv), lambda b, h, i: (b, 0, v_idx(h)))
    return _pcall(body, name=name, grid=(B, H, S // tq),
                  in_specs=[qs, ks, vin, pl.BlockSpec((1, tq, dv), lambda b, h, i: (b, i, h))],
                  out_specs=[qs, ks, vs],
                  out_shape=[jax.ShapeDtypeStruct((B, S, H * dqk), F32), jax.ShapeDtypeStruct((B, Sk, H * dqk), F32),
                             jax.ShapeDtypeStruct((B, Sk, H * dv), F32)],
                  compiler_params=_params(("parallel", "parallel", "arbitrary")))(q, k, v, do)


def _shift_down(u, j):
    if j == 0:
        return u
    row = lax.broadcasted_iota(jnp.int32, u.shape, 0)
    return jnp.where(row >= j, pltpu.roll(u, j, 0), 0.0)


def _shift_up(u, j):
    if j == 0:
        return u
    n = u.shape[0]
    row = lax.broadcasted_iota(jnp.int32, u.shape, 0)
    return jnp.where(row < n - j, pltpu.roll(u, n - j, 0), 0.0)


def conv_fwd(u, col0_blk, w8, b, name):
    B, S, _ = u.shape
    tc = 256
    nb = SSD_CONV_DIM // tc

    def body(u_ref, w_ref, b_ref, o_ref):
        uv = u_ref[0]
        pre = b_ref[...] + jnp.zeros_like(uv)
        for kk in range(SSD_CONV):
            pre = pre + w_ref[kk:kk + 1, :] * _shift_down(uv, SSD_CONV - 1 - kk)
        o_ref[0] = pre * _sigmoid(pre)

    return _pcall(body, name=name, grid=(B, nb),
                  in_specs=[pl.BlockSpec((1, S, tc), lambda bb, j: (bb, 0, col0_blk + j)),
                            pl.BlockSpec((8, tc), lambda bb, j: (0, j)), pl.BlockSpec((1, tc), lambda bb, j: (0, j))],
                  out_specs=pl.BlockSpec((1, S, tc), lambda bb, j: (bb, 0, j)),
                  out_shape=jax.ShapeDtypeStruct((B, S, SSD_CONV_DIM), F32),
                  compiler_params=_params(("parallel", "parallel")))(u, w8, b)


def conv_bwd(u, col0_blk, w8, b, dy, name):
    B, S, _ = u.shape
    tc = 256
    nb = SSD_CONV_DIM // tc

    def body(u_ref, w_ref, b_ref, dy_ref, du_ref, dw_ref, db_ref):
        bb = pl.program_id(1)

        @pl.when(bb == 0)
        def _():
            dw_ref[...] = jnp.zeros_like(dw_ref)
            db_ref[...] = jnp.zeros_like(db_ref)

        uv = u_ref[0]
        pre = b_ref[...] + jnp.zeros_like(uv)
        for kk in range(SSD_CONV):
            pre = pre + w_ref[kk:kk + 1, :] * _shift_down(uv, SSD_CONV - 1 - kk)
        s = _sigmoid(pre)
        dpre = dy_ref[0] * (s * (1.0 + pre * (1.0 - s)))
        du = jnp.zeros_like(uv)
        for kk in range(SSD_CONV):
            j = SSD_CONV - 1 - kk
            du = du + w_ref[kk:kk + 1, :] * _shift_up(dpre, j)
            dw_ref[kk:kk + 1, :] += jnp.sum(dpre * _shift_down(uv, j), axis=0, keepdims=True)
        db_ref[...] += jnp.sum(dpre, axis=0, keepdims=True)
        du_ref[0] = du

    return _pcall(body, name=name, grid=(nb, B),
                  in_specs=[pl.BlockSpec((1, S, tc), lambda j, bb: (bb, 0, col0_blk + j)),
                            pl.BlockSpec((8, tc), lambda j, bb: (0, j)), pl.BlockSpec((1, tc), lambda j, bb: (0, j)),
                            pl.BlockSpec((1, S, tc), lambda j, bb: (bb, 0, j))],
                  out_specs=[pl.BlockSpec((1, S, tc), lambda j, bb: (bb, 0, j)),
                             pl.BlockSpec((8, tc), lambda j, bb: (0, j)), pl.BlockSpec((1, tc), lambda j, bb: (0, j))],
                  out_shape=[jax.ShapeDtypeStruct((B, S, SSD_CONV_DIM), F32), jax.ShapeDtypeStruct((8, SSD_CONV_DIM), F32),
                             jax.ShapeDtypeStruct((1, SSD_CONV_DIM), F32)],
                  compiler_params=_params(("arbitrary", "arbitrary")))(u, w8, b, dy)


def _expand_mat():
    e = np.zeros((LANES, SSD_GW), np.float32)
    for h in range(SSD_HPG):
        e[h, h * SSD_HEAD_DIM:(h + 1) * SSD_HEAD_DIM] = 1.0
    return jnp.asarray(e)


def _tri_mat():
    return jnp.asarray(np.tril(np.ones((SSD_CHUNK, SSD_CHUNK), np.float32)))


def _dotf(a, b, dn=(((1,), (0,)), ((), ()))):
    return lax.dot_general(a, b, dn, precision=HI, preferred_element_type=F32)


def _dotb(a, b, dn=(((1,), (0,)), ((), ()))):
    return lax.dot_general(a.astype(BF16), b.astype(BF16), dn, preferred_element_type=F32)


NT = (((1,), (1,)), ((), ()))
TN = (((0,), (0,)), ((), ()))


def _ssd_common(dt_ref, par_ref, tri_ref, e_ref):
    dtr = dt_ref[0]
    xb = dtr + par_ref[0:1, :]
    dt = jnp.maximum(xb, 0.0) + jnp.log(1.0 + jnp.exp(-jnp.abs(xb)))
    A = -jnp.exp(par_ref[1:2, :])
    a = dt * A
    acs = _dotf(tri_ref[...], a)
    alast = acs[SSD_CHUNK - 1:SSD_CHUNK, :]
    E = e_ref[...]
    acsE = _dotf(acs, E)
    alastE = acsE[SSD_CHUNK - 1:SSD_CHUNK, :]
    return dict(xb=xb, dt=dt, A=A, acs=acs, acsT=acs.T, ealast=jnp.exp(alast), w=jnp.exp(alast - acs),
                dtE=_dotf(dt, E), eacsE=jnp.exp(acsE), wE=jnp.exp(alastE - acsE), ealastE=jnp.exp(alastE),
                DE=_dotf(par_ref[...], E)[2:3, :])


def _decay(c, e):
    diff = c["acs"][:, e:e + 1] - c["acsT"][e:e + 1, :]
    li = lax.broadcasted_iota(jnp.int32, diff.shape, 0)
    si = lax.broadcasted_iota(jnp.int32, diff.shape, 1)
    return jnp.where(li >= si, jnp.exp(jnp.minimum(diff, 0.0)), 0.0)


def ssd_fwd(xbc, dtp, par, name):
    B, S, _ = xbc.shape
    nc = S // SSD_CHUNK
    L, N, GW, P = SSD_CHUNK, SSD_STATE, SSD_GW, SSD_HEAD_DIM

    def body(x_ref, b_ref, c_ref, dt_ref, par_ref, tri_ref, e_ref, y_ref, st_ref, S_scr):
        @pl.when(pl.program_id(2) == 0)
        def _():
            S_scr[...] = jnp.zeros_like(S_scr)

        Sin = S_scr[...]
        st_ref[0, 0, 0] = Sin
        c = _ssd_common(dt_ref, par_ref.at[0], tri_ref, e_ref)
        X = x_ref[0]
        Bm = b_ref[0]
        Cm = c_ref[0]
        xdt = X * c["dtE"]
        CB = _dotb(Cm, Bm, NT)
        y_ref[0] = c["eacsE"] * _dotb(Cm, Sin) + c["DE"] * X
        for e in range(SSD_HPG):
            M = CB * _decay(c, e)
            y_ref[0, :, e * P:(e + 1) * P] += _dotb(M, xdt[:, e * P:(e + 1) * P])
        S_scr[...] = c["ealastE"] * Sin + _dotb(Bm, xdt * c["wE"], TN)

    return _pcall(
        body, name=name, grid=(SSD_GROUPS, B, nc),
        in_specs=[pl.BlockSpec((1, L, GW), lambda g, b, ci: (b, ci, g)),
                  pl.BlockSpec((1, L, N), lambda g, b, ci: (b, ci, 8 + g)),
                  pl.BlockSpec((1, L, N), lambda g, b, ci: (b, ci, 10 + g)),
                  pl.BlockSpec((1, L, LANES), lambda g, b, ci: (b, ci, g)),
                  pl.BlockSpec((1, 8, LANES), lambda g, b, ci: (g, 0, 0)),
                  pl.BlockSpec((L, L), lambda g, b, ci: (0, 0)),
                  pl.BlockSpec((LANES, GW), lambda g, b, ci: (0, 0))],
        out_specs=[pl.BlockSpec((1, L, GW), lambda g, b, ci: (b, ci, g)),
                   pl.BlockSpec((1, 1, 1, N, GW), lambda g, b, ci: (g, b, ci, 0, 0))],
        out_shape=[jax.ShapeDtypeStruct((B, S, SSD_INNER), F32), jax.ShapeDtypeStruct((SSD_GROUPS, B, nc, N, GW), F32)],
        scratch_shapes=[pltpu.VMEM((N, GW), F32)],
        compiler_params=_params(("arbitrary", "arbitrary", "arbitrary")))(xbc, xbc, xbc, dtp, par, _tri_mat(), _expand_mat())


def ssd_bwd(xbc, dtp, par, states, dy, name):
    B, S, _ = xbc.shape
    nc = S // SSD_CHUNK
    L, N, GW, P = SSD_CHUNK, SSD_STATE, SSD_GW, SSD_HEAD_DIM

    def body(x_ref, b_ref, c_ref, dt_ref, par_ref, tri_ref, e_ref, st_ref, dy_ref,
             dx_ref, db_ref, dc_ref, ddt_ref, dpar_ref, dS_scr, dxdt_scr):
        @pl.when(pl.program_id(2) == 0)
        def _():
            dS_scr[...] = jnp.zeros_like(dS_scr)

        @pl.when((pl.program_id(1) == 0) & (pl.program_id(2) == 0))
        def _():
            dpar_ref[...] = jnp.zeros_like(dpar_ref)

        c = _ssd_common(dt_ref, par_ref.at[0], tri_ref, e_ref)
        E = e_ref[...]
        red = lambda t: _dotf(t, E, NT)
        red_row = lambda t: red(jnp.broadcast_to(t, (8, GW)))[0:1, :]
        X, Bm, Cm = x_ref[0], b_ref[0], c_ref[0]
        Sin = st_ref[0, 0, 0]
        dY = dy_ref[0]
        dSo = dS_scr[...]
        xdt = X * c["dtE"]
        CB = _dotb(Cm, Bm, NT)
        dD = red_row(jnp.sum(dY * X, axis=0, keepdims=True))
        yoff = c["eacsE"] * _dotb(Cm, Sin)
        dacs = red(dY * yoff)
        dYe = dY * c["eacsE"]
        dC = _dotb(dYe, Sin, NT)
        dSin = _dotb(Cm, dYe, TN) + c["ealastE"] * dSo
        dB = _dotb(xdt * c["wE"], dSo, NT)
        dxw = _dotb(Bm, dSo)
        t = red(dxw * xdt) * c["w"]
        dacs = dacs - t
        dalast = jnp.sum(t, axis=0, keepdims=True) + c["ealast"] * red_row(jnp.sum(Sin * dSo, axis=0, keepdims=True))
        dxdt_scr[...] = dxw * c["wE"]
        dCB = jnp.zeros((L, L), F32)
        dacsT = jnp.zeros((LANES, L), F32)
        lane = lax.broadcasted_iota(jnp.int32, (1, LANES), 1)
        subl = lax.broadcasted_iota(jnp.int32, (LANES, 1), 0)
        for e in range(SSD_HPG):
            Lm = _decay(c, e)
            M = CB * Lm
            dYh = dY[:, e * P:(e + 1) * P]
            dM = _dotb(dYh, xdt[:, e * P:(e + 1) * P], NT)
            dxdt_scr[:, e * P:(e + 1) * P] += _dotb(M, dYh, TN)
            dCBe = dM * Lm
            dCB = dCB + dCBe
            Gm = dCBe * CB
            dacs = dacs + jnp.sum(Gm, axis=1, keepdims=True) * (lane == e).astype(F32)
            dacsT = dacsT - jnp.sum(Gm, axis=0, keepdims=True) * (subl == e).astype(F32)
        dC = dC + _dotb(dCB, Bm)
        dB = dB + _dotb(dCB, Cm, TN)
        dacs = dacs + dacsT.T
        row = lax.broadcasted_iota(jnp.int32, (L, 1), 0)
        dacs = dacs + jnp.where(row == L - 1, dalast, 0.0)
        da = _dotf(tri_ref[...], dacs, TN)
        dxdt = dxdt_scr[...]
        ddt = da * c["A"] + red(dxdt * X)
        dA = jnp.sum(da * c["dt"], axis=0, keepdims=True)
        ddtr = ddt * _sigmoid(c["xb"])
        dx_ref[0] = c["DE"] * dY + dxdt * c["dtE"]
        db_ref[0] = dB
        dc_ref[0] = dC
        ddt_ref[0] = ddtr
        dpar_ref[0, 0:1, :] += jnp.sum(ddtr, axis=0, keepdims=True)
        dpar_ref[0, 1:2, :] += dA * c["A"]
        dpar_ref[0, 2:3, :] += dD
        dS_scr[...] = dSin

    rc = lambda ci: nc - 1 - ci
    xspec = pl.BlockSpec((1, L, GW), lambda g, b, ci: (b, rc(ci), g))
    return _pcall(
        body, name=name, grid=(SSD_GROUPS, B, nc),
        in_specs=[xspec,
                  pl.BlockSpec((1, L, N), lambda g, b, ci: (b, rc(ci), 8 + g)),
                  pl.BlockSpec((1, L, N), lambda g, b, ci: (b, rc(ci), 10 + g)),
                  pl.BlockSpec((1, L, LANES), lambda g, b, ci: (b, rc(ci), g)),
                  pl.BlockSpec((1, 8, LANES), lambda g, b, ci: (g, 0, 0)),
                  pl.BlockSpec((L, L), lambda g, b, ci: (0, 0)),
                  pl.BlockSpec((LANES, GW), lambda g, b, ci: (0, 0)),
                  pl.BlockSpec((1, 1, 1, N, GW), lambda g, b, ci: (g, b, rc(ci), 0, 0)),
                  xspec],
        out_specs=[xspec,
                   pl.BlockSpec((1, L, N), lambda g, b, ci: (b, rc(ci), g)),
                   pl.BlockSpec((1, L, N), lambda g, b, ci: (b, rc(ci), g)),
                   pl.BlockSpec((1, L, LANES), lambda g, b, ci: (b, rc(ci), g)),
                   pl.BlockSpec((1, 8, LANES), lambda g, b, ci: (g, 0, 0))],
        out_shape=[jax.ShapeDtypeStruct((B, S, SSD_INNER), F32), jax.ShapeDtypeStruct((B, S, SSD_GROUPS * N), F32),
                   jax.ShapeDtypeStruct((B, S, SSD_GROUPS * N), F32), jax.ShapeDtypeStruct((B, S, SSD_GROUPS * LANES), F32),
                   jax.ShapeDtypeStruct((SSD_GROUPS, 8, LANES), F32)],
        scratch_shapes=[pltpu.VMEM((N, GW), F32), pltpu.VMEM((L, GW), F32)],
        compiler_params=_params(("arbitrary", "arbitrary", "arbitrary")))(
            xbc, xbc, xbc, dtp, par, _tri_mat(), _expand_mat(), states, dy)


def gate_up_act(h, wg, wu, name):
    T, D = h.shape
    c = wg.shape[-1]
    tm = _tile(T, (512, 256, 128))

    def body(h_ref, g_w, u_w, g_ref, u_ref, a_ref):
        hv = h_ref[...]
        g = jnp.dot(hv, g_w[...], preferred_element_type=F32)
        u = jnp.dot(hv, u_w[...], preferred_element_type=F32)
        g_ref[...] = g.astype(BF16)
        u_ref[...] = u.astype(BF16)
        a_ref[...] = (g * _sigmoid(g) * u).astype(BF16)

    ws = pl.BlockSpec((None, None, D, c), lambda i, j: (0, j, 0, 0))
    os_ = pl.BlockSpec((tm, c), lambda i, j: (i, j))
    return _pcall(body, name=name, grid=(T // tm, 4), in_specs=[pl.BlockSpec((tm, D), lambda i, j: (i, 0)), ws, ws],
                  out_specs=[os_, os_, os_], out_shape=[jax.ShapeDtypeStruct((T, 4 * c), BF16)] * 3,
                  compiler_params=_params(("parallel", "parallel")))(h, wg, wu)


def down_dx_swiglu(dy, w_down, gate, up, name, dep=None):
    T, D = dy.shape
    Fh = w_down.shape[0]
    tm, tn = _tile(T, (1024, 512, 256, 128)), 512

    def body(dy_ref, w_ref, g_ref, u_ref, *rest):
        dg_ref, du_ref = rest[-2:]
        d = lax.dot_general(dy_ref[...], w_ref[...], NT, preferred_element_type=F32)
        g = g_ref[...].astype(F32)
        u = u_ref[...].astype(F32)
        s = _sigmoid(g)
        dg_ref[...] = (d * u * (s * (1.0 + g * (1.0 - s)))).astype(BF16)
        du_ref[...] = (d * g * s).astype(BF16)

    ts = pl.BlockSpec((tm, tn), lambda i, j: (i, j))
    in_specs = [pl.BlockSpec((tm, D), lambda i, j: (i, 0)), pl.BlockSpec((tn, D), lambda i, j: (j, 0)), ts, ts]
    args = (dy, w_down, gate, up)
    if dep is not None:
        in_specs, args = in_specs + [pl.BlockSpec(memory_space=pl.ANY)], args + (dep,)
    return _pcall(body, name=name, grid=(T // tm, Fh // tn), in_specs=in_specs, out_specs=[ts, ts],
                  out_shape=[jax.ShapeDtypeStruct((T, Fh), BF16)] * 2,
                  compiler_params=_params(("parallel", "parallel")))(*args)


def swiglu_fwd(g, u, name):
    T = g.shape[0]
    tq, tc = _tile(T, (512, 256, 128)), 512
    nf = FFN // tc

    def body(g_ref, u_ref, o_ref):
        gv = g_ref[...].astype(F32)
        o_ref[...] = (gv * _sigmoid(gv) * u_ref[...].astype(F32)).astype(BF16)

    lo = pl.BlockSpec((tq, tc), lambda i, j: (i, j))
    return _pcall(body, name=name, grid=(T // tq, nf), in_specs=[lo, lo], out_specs=lo,
                  out_shape=jax.ShapeDtypeStruct((T, FFN), BF16), compiler_params=_params(("parallel", "parallel")))(g, u)


def swiglu_bwd(gate, up, dact, name):
    T = gate.shape[0]
    tq, tc = _tile(T, (512, 256, 128)), 512
    nf = FFN // tc

    def body(g_ref, u_ref, d_ref, dg_ref, du_ref):
        g = g_ref[...].astype(F32)
        u = u_ref[...].astype(F32)
        d = d_ref[...]
        s = _sigmoid(g)
        dg_ref[...] = (d * u * (s * (1.0 + g * (1.0 - s)))).astype(BF16)
        du_ref[...] = (d * g * s).astype(BF16)

    lo = pl.BlockSpec((tq, tc), lambda i, j: (i, j))
    return _pcall(body, name=name, grid=(T // tq, nf), in_specs=[lo, lo, lo], out_specs=[lo, lo],
                  out_shape=[jax.ShapeDtypeStruct((T, FFN), BF16)] * 2,
                  compiler_params=_params(("parallel", "parallel")))(gate, up, dact)


def loss_head(y, target):
    T, D = y.shape
    tq = _tile(T, (256, 128))

    def body(y_ref, t_ref, dy_ref, l_ref, dyb_ref):
        @pl.when(pl.program_id(0) == 0)
        def _():
            l_ref[...] = jnp.zeros_like(l_ref)

        e = y_ref[...] - t_ref[...]
        dy = e * (1.0 / D)
        dy_ref[...] = dy
        dyb_ref[...] = dy.astype(BF16)
        l_ref[...] += 0.5 * jnp.sum(jnp.mean(e * e, axis=-1, keepdims=True))

    xs = pl.BlockSpec((tq, D), lambda i: (i, 0))
    return _pcall(body, name="loss_head", grid=(T // tq,), in_specs=[xs, xs],
                  out_specs=[xs, pl.BlockSpec((8, LANES), lambda i: (0, 0)), xs],
                  out_shape=[jax.ShapeDtypeStruct((T, D), F32), jax.ShapeDtypeStruct((8, LANES), F32),
                             jax.ShapeDtypeStruct((T, D), BF16)],
                  compiler_params=_params(("arbitrary",)))(y, target)


def sum_arrays(arrs, out_dtype, name):
    R, C = arrs[0].shape
    tr = _tile(R, (512, 256, 128, 64, 32, 16, 8))
    n = len(arrs)

    def body(*refs):
        acc = refs[0][...].astype(F32)
        for r in refs[1:n]:
            acc = acc + r[...].astype(F32)
        refs[n][...] = acc.astype(out_dtype)

    xs = pl.BlockSpec((tr, C), lambda i: (i, 0))
    return _pcall(body, name=name, grid=(R // tr,), in_specs=[xs] * n, out_specs=xs,
                  out_shape=jax.ShapeDtypeStruct((R, C), out_dtype), compiler_params=_params(("parallel",)))(*arrs)


def adamw(w, g, m, v, name):
    R, C = w.shape
    tr = _tile(R, (256, 128, 64, 32, 16, 8))
    c1 = 1.0 - ADAM_B1 ** ADAM_STEP
    c2 = 1.0 - ADAM_B2 ** ADAM_STEP

    def body(w_ref, g_ref, m_ref, v_ref, d_ref, nm_ref, nv_ref):
        gv = g_ref[...]
        mn = ADAM_B1 * m_ref[...] + (1.0 - ADAM_B1) * gv
        vn = ADAM_B2 * v_ref[...] + (1.0 - ADAM_B2) * (gv * gv)
        nm_ref[...] = mn
        nv_ref[...] = vn
        d_ref[...] = -ADAM_LR * ((mn / c1) / (jnp.sqrt(vn / c2) + ADAM_EPS) + ADAM_WD * w_ref[...])

    xs = pl.BlockSpec((tr, C), lambda i: (i, 0))
    return _pcall(body, name=name, grid=(R // tr,), in_specs=[xs] * 4, out_specs=[xs] * 3,
                  out_shape=[jax.ShapeDtypeStruct((R, C), F32)] * 3, compiler_params=_params(("parallel",)))(w, g, m, v)


def adamw_layer(w, g, m, v, l, prev, name):
    L, r, c = w.shape
    tr = _tile(r, (256, 128, 64, 32, 16, 8))
    c1 = 1.0 - ADAM_B1 ** ADAM_STEP
    c2 = 1.0 - ADAM_B2 ** ADAM_STEP
    chained = prev is not None

    def body(w_ref, g_ref, m_ref, v_ref, *rest):
        go_ref, d_ref, nm_ref, nv_ref = rest[-4:]
        gv = g_ref[...]
        mn = ADAM_B1 * m_ref[...] + (1.0 - ADAM_B1) * gv
        vn = ADAM_B2 * v_ref[...] + (1.0 - ADAM_B2) * (gv * gv)
        go_ref[...] = gv
        nm_ref[...] = mn
        nv_ref[...] = vn
        d_ref[...] = -ADAM_LR * ((mn / c1) / (jnp.sqrt(vn / c2) + ADAM_EPS) + ADAM_WD * w_ref[...])

    ls = pl.BlockSpec((None, tr, c), lambda i: (l, i, 0))
    gs = pl.BlockSpec((tr, c), lambda i: (i, 0))
    anys = [pl.BlockSpec(memory_space=pl.ANY)] * 4 if chained else []
    return _pcall(body, name=name, grid=(r // tr,), in_specs=[ls, gs, ls, ls] + anys, out_specs=[ls] * 4,
                  out_shape=[jax.ShapeDtypeStruct((L, r, c), F32)] * 4,
                  input_output_aliases={4: 0, 5: 1, 6: 2, 7: 3} if chained else {},
                  compiler_params=_params(("parallel",)))(w, g, m, v, *(prev if chained else ()))


def add_pair(g, got, name):
    _, r, c = g.shape
    h = r // 2

    def body(c_ref, g_ref, o_ref, out_ref):
        del c_ref
        out_ref[...] = (g_ref[...].astype(F32) + o_ref[...].astype(F32)).astype(BF16)

    spec = pltpu.PrefetchScalarGridSpec(
        num_scalar_prefetch=1, grid=(4,),
        in_specs=[pl.BlockSpec((None, h, c), lambda k, cs: (k, cs[0], 0)), pl.BlockSpec((None, h, c), lambda k, cs: (k, 0, 0))],
        out_specs=pl.BlockSpec((None, h, c), lambda k, cs: (k, 0, 0)))
    return _pcall(body, name=name, grid_spec=spec, out_shape=jax.ShapeDtypeStruct((4, h, c), BF16),
                  compiler_params=_params(("arbitrary",)))(lax.axis_index("c").reshape(1).astype(jnp.int32), g, got)


def add_chips(p, b, name):
    _, h, c = p.shape
    tr = _tile(h, (256, 128, 64, 32, 16, 8))

    def body(me_ref, p_ref, b0, b1, b2, b3, out_ref):
        me = me_ref[0]
        acc = jnp.zeros(out_ref.shape, F32)
        for k, bk in enumerate((b0, b1, b2, b3)):
            acc = acc + jnp.where(me == k, p_ref[...], bk[...]).astype(F32)
        out_ref[...] = acc

    slot = lambda k: pl.BlockSpec((None, tr, c), lambda i, ms: (jnp.where(ms[0] == k, (k + 1) % 4, k), i, 0))
    spec = pltpu.PrefetchScalarGridSpec(
        num_scalar_prefetch=1, grid=(h // tr,),
        in_specs=[pl.BlockSpec((None, tr, c), lambda i, ms: (ms[0], i, 0)), slot(0), slot(1), slot(2), slot(3)],
        out_specs=pl.BlockSpec((tr, c), lambda i, ms: (i, 0)))
    me = (2 * lax.axis_index("x") + lax.axis_index("y")).reshape(1).astype(jnp.int32)
    return _pcall(body, name=name, grid_spec=spec, out_shape=jax.ShapeDtypeStruct((h, c), F32),
                  compiler_params=_params(("arbitrary",)))(me, p, b, b, b, b)


ANY = pl.BlockSpec(memory_space=pl.ANY)


def _place():
    x, y, c = lax.axis_index("x"), lax.axis_index("y"), lax.axis_index("c")
    chips = [(1 - x, y), (x, 1 - y), (1 - x, 1 - y)]
    return x, y, c, chips


def _rcopy(src, dst, ssem, rsem, dev):
    return pltpu.make_async_remote_copy(src_ref=src, dst_ref=dst, send_sem=ssem, recv_sem=rsem, device_id=dev,
                                        device_id_type=MESH)


def _half(c, h):
    return pl.ds(pl.multiple_of(c * h, 64), h)


def _me_chip():
    return 2 * lax.axis_index("x") + lax.axis_index("y")


def all_gather_weights(shards, name):
    n = len(shards)
    L = shards[0].shape[0]
    hl = L // 2

    def body(*refs):
        srcs, outs, (ssem, rsem) = refs[:n], refs[n:2 * n], refs[2 * n:]
        x, y, c, chips = _place()
        me = 2 * x + y
        sib = (x, y, 1 - c)
        mine, other = pl.ds(c * hl, hl), pl.ds((1 - c) * hl, hl)
        sends = []
        for w in range(n):
            for j, (px, py) in enumerate(chips):
                cp = _rcopy(srcs[w].at[mine], outs[w].at[mine, me], ssem.at[w, j], rsem.at[w, j], (px, py, c))
                cp.start()
                sends.append(cp)
        for j, (px, py) in enumerate(chips):
            slot = 2 * px + py
            for w in range(n):
                _rcopy(srcs[w].at[mine], outs[w].at[mine, slot], ssem.at[w, j], rsem.at[w, j], (px, py, c)).wait_recv()
                cp = _rcopy(outs[w].at[mine, slot], outs[w].at[mine, slot], ssem.at[w, 3 + j], rsem.at[w, 3 + j], sib)
                cp.start()
                sends.append(cp)
        for j, (px, py) in enumerate(chips):
            slot = 2 * px + py
            for w in range(n):
                _rcopy(outs[w].at[other, slot], outs[w].at[other, slot], ssem.at[w, 3 + j], rsem.at[w, 3 + j], sib).wait_recv()
        for cp in sends:
            cp.wait_send()

    outs = _pcall(body, name=name, in_specs=[ANY] * n, out_specs=[ANY] * n,
                  out_shape=[jax.ShapeDtypeStruct((L, 4) + s.shape[1:], s.dtype) for s in shards],
                  scratch_shapes=[pltpu.SemaphoreType.DMA((n, 6)), pltpu.SemaphoreType.DMA((n, 6))])(*shards)
    me = _me_chip()
    return [lax.dynamic_update_slice(o, s[:, None], (0, me, 0, 0)) for o, s in zip(outs, shards)]


HBM = pl.BlockSpec(memory_space=pltpu.HBM)
SEM = pl.BlockSpec(memory_space=pltpu.SEMAPHORE)
EFFECT = pltpu.SideEffectType.DATAFLOW_SIDE_EFFECTING


def _ici_copy(kind, src, land, x, y, c, px, py, ssem, rsem, receiving):
    me, peer = 2 * x + y, 2 * px + py
    if kind == "gather":
        rows = _half(c, src.shape[0] // 2)
        s, d = src.at[rows], land.at[peer if receiving else me, rows]
    else:
        s, d = src.at[me if receiving else peer], land.at[peer if receiving else me]
    return _rcopy(s, d, ssem, rsem, (px, py, c))


def ici_start(kind, srcs, land_shapes, name):
    n = len(srcs)

    m = 3 * n

    def body(*refs):
        src, land, token = refs[:n], refs[n:2 * n], refs[-1]
        ssem, rsem = refs[2 * n:2 * n + m], refs[2 * n + m:2 * n + 2 * m]
        x, y, c, chips = _place()
        for w in range(n):
            for j, (px, py) in enumerate(chips):
                _ici_copy(kind, src[w], land[w], x, y, c, px, py, ssem[3 * w + j], rsem[3 * w + j], False).start()
        token[...] = jnp.zeros_like(token)

    lands = [pltpu.with_memory_space_constraint(lax.empty(shp, s.dtype), pltpu.HBM) for shp, s in zip(land_shapes, srcs)]
    srcs = [pltpu.with_memory_space_constraint(s, pltpu.HBM) for s in srcs]
    outs = _pcall(
        body, name=name,
        out_shape=(*[pltpu.SemaphoreType.DMA(())] * (2 * m),
                   *[pltpu.HBM(s.shape, s.dtype) for s in srcs], *[pltpu.HBM(z.shape, z.dtype) for z in lands],
                   jax.ShapeDtypeStruct((8, LANES), F32)),
        in_specs=[HBM] * (2 * n),
        out_specs=(*([SEM] * (2 * m)), *([HBM] * (2 * n)), pl.BlockSpec(memory_space=pltpu.VMEM)),
        input_output_aliases={i: 2 * m + i for i in range(2 * n)},
        compiler_params=pltpu.CompilerParams(has_side_effects=EFFECT))(*srcs, *lands)
    return dict(kind=kind, ssem=list(outs[:m]), rsem=list(outs[m:2 * m]), srcs=list(outs[2 * m:2 * m + n]),
                lands=list(outs[2 * m + n:2 * m + 2 * n])), outs[-1]


def ici_wait(state, after, name):
    kind, n = state["kind"], len(state["srcs"])

    m = 3 * n

    def body(*refs):
        src, land = refs[:n], refs[n:2 * n]
        ssem, rsem = refs[2 * n:2 * n + m], refs[2 * n + m:2 * n + 2 * m]
        x, y, c, chips = _place()
        for w in range(n):
            for j, (px, py) in enumerate(chips):
                _ici_copy(kind, src[w], land[w], x, y, c, px, py, ssem[3 * w + j], rsem[3 * w + j], False).wait_send()
                _ici_copy(kind, src[w], land[w], x, y, c, px, py, ssem[3 * w + j], rsem[3 * w + j], True).wait_recv()

    outs = _pcall(
        body, name=name,
        out_shape=tuple(pltpu.HBM(a.shape, a.dtype) for a in state["srcs"] + state["lands"]),
        in_specs=[HBM] * (2 * n) + [SEM] * (2 * m) + [ANY], out_specs=tuple([HBM] * (2 * n)),
        input_output_aliases={i: i for i in range(2 * n)},
        compiler_params=pltpu.CompilerParams(has_side_effects=EFFECT))(
            *state["srcs"], *state["lands"], *state["ssem"], *state["rsem"], after)
    return list(outs[:n]), list(outs[n:])


def forward_pair(lands, name):
    n = len(lands)

    def body(*refs):
        land, (ssem, rsem) = refs[:n], refs[2 * n:]
        x, y, c, chips = _place()
        sib = (x, y, 1 - c)
        cps = []
        for w in range(n):
            h = land[w].shape[1] // 2
            for j, (px, py) in enumerate(chips):
                part = land[w].at[2 * px + py, _half(c, h)]
                cp = _rcopy(part, part, ssem.at[w, j], rsem.at[w, j], sib)
                cp.start()
                cps.append(cp)
        for w in range(n):
            h = land[w].shape[1] // 2
            for j, (px, py) in enumerate(chips):
                part = land[w].at[2 * px + py, _half(1 - c, h)]
                _rcopy(part, part, ssem.at[w, j], rsem.at[w, j], sib).wait_recv()
        for cp in cps:
            cp.wait_send()

    return _pcall(body, name=name, in_specs=[ANY] * n, out_specs=[ANY] * n,
                  out_shape=[jax.ShapeDtypeStruct(z.shape, z.dtype) for z in lands],
                  input_output_aliases={i: i for i in range(n)},
                  scratch_shapes=[pltpu.SemaphoreType.DMA((n, 3)), pltpu.SemaphoreType.DMA((n, 3))])(*lands)


def exchange_pair(gs, name):
    n = len(gs)

    def body(*refs):
        srcs, outs, (ssem, rsem) = refs[:n], refs[n:2 * n], refs[2 * n:]
        x, y, c, _ = _place()
        cps = []
        for w in range(n):
            h = srcs[w].shape[1] // 2
            cp = _rcopy(srcs[w].at[:, _half(1 - c, h)], outs[w], ssem.at[w], rsem.at[w], (x, y, 1 - c))
            cp.start()
            cps.append(cp)
        for cp in cps:
            cp.wait()

    return _pcall(body, name=name, in_specs=[ANY] * n, out_specs=[ANY] * n,
                  out_shape=[jax.ShapeDtypeStruct((4, g.shape[1] // 2, g.shape[2]), g.dtype) for g in gs],
                  scratch_shapes=[pltpu.SemaphoreType.DMA((n,)), pltpu.SemaphoreType.DMA((n,))])(*gs)


def scatter_chips(ps, name):
    n = len(ps)

    def body(*refs):
        srcs, outs, (ssem, rsem) = refs[:n], refs[n:2 * n], refs[2 * n:]
        x, y, c, chips = _place()
        me = 2 * x + y
        sends = []
        for w in range(n):
            for j, (px, py) in enumerate(chips):
                cp = _rcopy(srcs[w].at[2 * px + py], outs[w].at[me], ssem.at[w, j], rsem.at[w, j], (px, py, c))
                cp.start()
                sends.append(cp)
        for w in range(n):
            for j, (px, py) in enumerate(chips):
                _rcopy(srcs[w].at[me], outs[w].at[2 * px + py], ssem.at[w, j], rsem.at[w, j], (px, py, c)).wait_recv()
        for cp in sends:
            cp.wait_send()

    return _pcall(body, name=name, in_specs=[ANY] * n, out_specs=[ANY] * n,
                  out_shape=[jax.ShapeDtypeStruct(p.shape, p.dtype) for p in ps],
                  scratch_shapes=[pltpu.SemaphoreType.DMA((n, 3)), pltpu.SemaphoreType.DMA((n, 3))])(*ps)


def join_pair(rs, name):
    n = len(rs)

    def body(*refs):
        srcs, outs, (ssem, rsem) = refs[:n], refs[n:2 * n], refs[2 * n:]
        x, y, c, _ = _place()
        cps = []
        for w in range(n):
            h = srcs[w].shape[0]
            cp = _rcopy(srcs[w], outs[w].at[_half(c, h)], ssem.at[w], rsem.at[w], (x, y, 1 - c))
            cp.start()
            cps.append(cp)
        for w in range(n):
            h = srcs[w].shape[0]
            _rcopy(srcs[w], outs[w].at[_half(1 - c, h)], ssem.at[w], rsem.at[w], (x, y, 1 - c)).wait_recv()
        for cp in cps:
            cp.wait_send()

    outs = _pcall(body, name=name, in_specs=[ANY] * n, out_specs=[ANY] * n,
                  out_shape=[jax.ShapeDtypeStruct((2 * r.shape[0], r.shape[1]), r.dtype) for r in rs],
                  scratch_shapes=[pltpu.SemaphoreType.DMA((n,)), pltpu.SemaphoreType.DMA((n,))])(*rs)
    c = lax.axis_index("c")
    return [lax.dynamic_update_slice(o, r, (c * r.shape[0], 0)) for o, r in zip(outs, rs)]


def gather_all_devices(v, name):
    R, C = v.shape

    def body(src, out, ssem, rsem, lsem):
        x, y, c, _ = _place()
        me = 4 * x + 2 * y + c
        local = pltpu.make_async_copy(src, out.at[me], lsem)
        local.start()
        sends = []
        for j in range(1, 8):
            dx, dy, dc = (j >> 2) & 1, (j >> 1) & 1, j & 1
            px, py, pc = (1 - x if dx else x), (1 - y if dy else y), (1 - c if dc else c)
            cp = _rcopy(src, out.at[me], ssem.at[j - 1], rsem.at[j - 1], (px, py, pc))
            cp.start()
            sends.append(cp)
        for j in range(1, 8):
            dx, dy, dc = (j >> 2) & 1, (j >> 1) & 1, j & 1
            px, py, pc = (1 - x if dx else x), (1 - y if dy else y), (1 - c if dc else c)
            _rcopy(src, out.at[4 * px + 2 * py + pc], ssem.at[j - 1], rsem.at[j - 1], (px, py, pc)).wait_recv()
        for cp in sends:
            cp.wait_send()
        local.wait()

    return _pcall(body, name=name, in_specs=[ANY], out_specs=ANY, out_shape=jax.ShapeDtypeStruct((8, R, C), v.dtype),
                  scratch_shapes=[pltpu.SemaphoreType.DMA((7,)), pltpu.SemaphoreType.DMA((7,)), pltpu.SemaphoreType.DMA(())])(v)


def reduce_scatter_begin(gw, tag):
    names = list(gw)
    gs = [gw[k] for k in names]
    got = exchange_pair(gs, "rs_pair_" + tag)
    ps = [add_pair(g, o, "rs_add_pair_%s_%s" % (k, tag)) for k, g, o in zip(names, gs, got)]
    state, token = ici_start("scatter", ps, [p.shape for p in ps], "rs_chips_start_" + tag)
    state["names"] = names
    return state, token


def reduce_scatter_end(state, after, tag):
    names = state["names"]
    ps, bs = ici_wait(state, after, "rs_chips_wait_" + tag)
    rs = [add_chips(p, b, "rs_add_chips_%s_%s" % (k, tag)) for k, p, b in zip(names, ps, bs)]
    return dict(zip(names, join_pair(rs, "rs_join_" + tag)))


GATHERED = ("w_in", "w_q_b", "w_kv_b", "w_out", "w_xq", "w_xk", "w_xv", "w_xo", "w_gate", "w_up", "w_down")
KIND = {"w_q_b": "col", "w_kv_b": "col", "w_out": "row", "w_xq": "row", "w_xk": "row", "w_xv": "row", "w_xo": "col",
        "w_gate": "col", "w_up": "col", "w_down": "row"}


def wq_to_storage(w):
    lead = w.shape[:-1]
    w = w.reshape(lead + (2, MLA_QK))
    rope = jnp.pad(w[..., MLA_NOPE:], [(0, 0)] * (w.ndim - 1) + [(0, LANES - MLA_ROPE)])
    return jnp.concatenate([w[..., :MLA_NOPE].reshape(lead + (2 * LANES,)), rope.reshape(lead + (2 * LANES,))], axis=-1)


def wq_from_storage(g):
    lead = g.shape[:-1]
    nope = g[..., :2 * LANES].reshape(lead + (2, LANES))
    rope = g[..., 2 * LANES:].reshape(lead + (2, LANES))[..., :MLA_ROPE]
    return jnp.concatenate([nope, rope], axis=-1).reshape(lead + (2 * MLA_QK,))


def wcat_from_storage(w_in_l):
    w = w_in_l.transpose(1, 0, 2).reshape(D_MODEL, IN_COLS)
    zc = lambda n: jnp.zeros((D_MODEL, n), w.dtype)
    return jnp.concatenate([w[:, :2560], w[:, 2576:], zc(64), w[:, 2560:2576], zc(112)], axis=1)


def w_in_grad_storage(dwcat):
    g = jnp.concatenate([dwcat[:, :2560], dwcat[:, C_DT:C_DT + 16], dwcat[:, 2560:3648]], axis=1)
    return g.reshape(D_MODEL, 4, IN_COLS // 4).transpose(1, 0, 2)


def _ssd_par(sm):
    rows = jnp.stack([sm["dt_bias"], sm["a_log"], sm["d_skip"]]).reshape(3, SSD_GROUPS, SSD_HPG).transpose(1, 0, 2)
    return jnp.pad(rows, ((0, 0), (0, 5), (0, LANES - SSD_HPG)))


def _dt_pad(proj, B, S):
    dt = proj[:, C_DT:C_DT + 16].reshape(B, S, SSD_GROUPS, SSD_HPG)
    return jnp.pad(dt, ((0, 0), (0, 0), (0, 0), (0, LANES - SSD_HPG))).reshape(B, S, SSD_GROUPS * LANES)


def _mla_gains(g):
    return g[:MLA_NOPE].reshape(1, LANES), jnp.pad(g[MLA_NOPE:], (0, LANES - MLA_ROPE)).reshape(1, LANES)


_Q_NOPE = lambda h: 4 * (h // 2) + h % 2
_Q_ROPE = lambda h: 4 * (h // 2) + 2 + h % 2
_K_NOPE = lambda h: 2 * h
_V_BLK = lambda h: 2 * h + 1


def layer_fwd(x, memf, cos, sin, G, kw, sm, B, S, l, dep=None):
    T = x.shape[0]
    n = lambda s: "%s_l%d" % (s, l)
    wv = lambda k: (KIND[k], 0)
    sv = dict(x0=x)
    h1 = norm_fwd(x, sm["attn_norm_g"], D_MODEL, False, n("attn_norm"), dep=dep)
    proj = mm(h1, kw["wcat"], "nn", n("in_proj"))
    mark = getattr(G, "mark", lambda v: None)
    mark(proj)
    sv.update(h1=h1, proj=proj)
    proj3 = proj.reshape(B, S, CAT_COLS)
    xbc = conv_fwd(proj3, C_XBC // 256, kw["conv_w"], sm["conv_b"].reshape(1, -1), n("conv"))
    dtp = _dt_pad(proj, B, S)
    par = _ssd_par(sm)
    y_raw, states = ssd_fwd(xbc, dtp, par, n("ssd"))
    z = proj[:, C_Z:C_Z + SSD_INNER]
    y_ssd = norm_fwd(y_raw.reshape(T, SSD_INNER), sm["ssd_norm_g"], SSD_INNER // SSD_GROUPS, False, n("ssd_norm"), gate=z)
    sv.update(xbc=xbc, states=states, y_raw=y_raw)
    qn = norm_fwd(proj[:, C_QA:C_QA + Q_LORA], sm["q_a_norm_g"], Q_LORA, False, n("q_a_norm"))
    kvn = norm_fwd(proj[:, C_KVA:C_KVA + KV_LORA], sm["kv_a_norm_g"], KV_LORA, False, n("kv_a_norm"))
    q_raw = mm(qn, G["w_q_b"], "nn", n("q_b"), wv=wv("w_q_b"))
    kv_raw = mm(kvn, G["w_kv_b"], "nn", n("kv_b"), wv=wv("w_kv_b"))
    gqn, gqr = _mla_gains(sm["mla_q_norm_g"])
    gkn, gkr = _mla_gains(sm["mla_k_norm_g"])
    qh = qk_fwd(q_raw, _Q_NOPE, q_raw, _Q_ROPE, False, gqn, gqr, cos, sin, n("q_norm_rope"))
    kh = qk_fwd(kv_raw, _K_NOPE, proj, C_KR // LANES, True, gkn, gkr, cos, sin, n("k_norm_rope"))
    vv = kv_raw.astype(BF16)
    y_mla = attn_fwd(qh.reshape(B, S, -1), kh.reshape(B, S, -1), vv.reshape(B, S, -1), MLA_HEADS, 2 * LANES, MLA_V,
                     MLA_QK ** -0.5, True, n("mla_attn"), v_idx=_V_BLK)
    ycat = jnp.concatenate([y_ssd, y_mla.reshape(T, -1)], axis=1)
    x1 = mm(ycat, G["w_out"], "nn", n("out_proj"), add=x, wv=wv("w_out"))
    sv.update(qn=qn, kvn=kvn, q_raw=q_raw, kv_raw=kv_raw, qh=qh, kh=kh, vv=vv, ycat=ycat, x1=x1)
    h2 = norm_fwd(x1, sm["xattn_norm_g"], D_MODEL, False, n("xattn_norm"))
    mn = norm_fwd(memf, sm["mem_norm_g"], D_MODEL, False, n("mem_norm"))
    xq = mm(h2, G["w_xq"], "nn", n("xq"), wv=wv("w_xq"))
    xk = mm(mn, G["w_xk"], "nn", n("xk"), wv=wv("w_xk"))
    xv = mm(mn, G["w_xv"], "nn", n("xv"), wv=wv("w_xv"), out_dtype=BF16)
    xqn = norm_fwd(xq, sm["xq_norm_g"], X_HEAD_DIM, True, n("xq_norm"))
    xkn = norm_fwd(xk, sm["xk_norm_g"], X_HEAD_DIM, True, n("xk_norm"))
    ML = memf.shape[0] // B
    xo = attn_fwd(xqn.reshape(B, S, -1), xkn.reshape(B, ML, -1), xv.reshape(B, ML, -1), X_HEADS, X_HEAD_DIM, X_HEAD_DIM,
                  X_HEAD_DIM ** -0.5, False, n("x_attn"))
    x2 = mm(xo.reshape(T, -1), G["w_xo"], "nn", n("xo_proj"), add=x1, wv=wv("w_xo"))
    mark(x2)
    sv.update(h2=h2, mn=mn, xq=xq, xk=xk, xqn=xqn, xkn=xkn, xv=xv, xo=xo, x2=x2)
    h3 = norm_fwd(x2, sm["ffn_norm_g"], D_MODEL, False, n("ffn_norm"))
    gate, up, act = gate_up_act(h3, G["w_gate"], G["w_up"], n("gate_up"))
    x3 = mm(act, G["w_down"], "nn", n("down"), add=x2, wv=wv("w_down"))
    sv.update(h3=h3, gate=gate, up=up, act=act)
    return x3, sv


def layer_bwd(dx3, dx3b, sv, memf, cos, sin, G, kw, sm, B, S, l, dep=None, rs_hook=None):
    T = dx3.shape[0]
    n = lambda s: "%s_bwd_l%d" % (s, l)
    wv = lambda k: (KIND[k], 0)
    dw = lambda k, a, b: mm(a, b, "tn", n(k + "_dw"), out_dtype=BF16, out_wv=KIND[k])
    gw, gs = {}, {}
    dgate, dup = down_dx_swiglu(dx3b, G["w_down"][0].reshape(FFN, D_MODEL), sv["gate"], sv["up"], n("down_dx"), dep=dep)
    gw["w_down"] = dw("w_down", sv["act"], dx3b)
    dh3 = mm(dgate, G["w_gate"], "nt", n("gate_dx"), wv=wv("w_gate"))
    dh3 = mm(dup, G["w_up"], "nt", n("up_dx"), wv=wv("w_up"), add=dh3)
    gw["w_gate"] = dw("w_gate", sv["h3"], dgate)
    gw["w_up"] = dw("w_up", sv["h3"], dup)
    dx2, gs["ffn_norm_g"], dx2b = norm_bwd(sv["x2"], sm["ffn_norm_g"], dh3, D_MODEL, False, n("ffn_norm"), add=dx3,
                                           bf16_copy=True)
    ffn_token = rs_hook({k: gw.pop(k) for k in ("w_down", "w_gate", "w_up")}) if rs_hook is not None else None
    dxo = mm(dx2b, G["w_xo"], "nt", n("xo_dx"), wv=wv("w_xo"), dep=ffn_token)
    gw["w_xo"] = dw("w_xo", sv["xo"].reshape(T, -1), dx2b)
    ML = memf.shape[0] // B
    dxqn, dxkn, dxv = attn_bwd(sv["xqn"].reshape(B, S, -1), sv["xkn"].reshape(B, ML, -1), sv["xv"].reshape(B, ML, -1),
                               dxo.reshape(B, S, -1), X_HEADS, X_HEAD_DIM, X_HEAD_DIM, X_HEAD_DIM ** -0.5, False, n("x_attn"))
    dxq, gs["xq_norm_g"] = norm_bwd(sv["xq"], sm["xq_norm_g"], dxqn.reshape(T, -1), X_HEAD_DIM, True, n("xq_norm"))
    dxk, gs["xk_norm_g"] = norm_bwd(sv["xk"], sm["xk_norm_g"], dxkn.reshape(B * ML, -1), X_HEAD_DIM, True, n("xk_norm"))
    dxv = dxv.reshape(B * ML, -1)
    dmn = mm(dxk, G["w_xk"], "nt", n("xk_dx"), wv=wv("w_xk"))
    dmn = mm(dxv, G["w_xv"], "nt", n("xv_dx"), wv=wv("w_xv"), add=dmn)
    gw["w_xk"] = dw("w_xk", sv["mn"], dxk)
    gw["w_xv"] = dw("w_xv", sv["mn"], dxv)
    _, gs["mem_norm_g"] = norm_bwd(memf, sm["mem_norm_g"], dmn, D_MODEL, False, n("mem_norm"))
    dh2 = mm(dxq, G["w_xq"], "nt", n("xq_dx"), wv=wv("w_xq"))
    gw["w_xq"] = dw("w_xq", sv["h2"], dxq)
    dx1, gs["xattn_norm_g"], dx1b = norm_bwd(sv["x1"], sm["xattn_norm_g"], dh2, D_MODEL, False, n("xattn_norm"), add=dx2,
                                             bf16_copy=True)
    dycat = mm(dx1b, G["w_out"], "nt", n("out_dx"), wv=wv("w_out"))
    gw["w_out"] = dw("w_out", sv["ycat"], dx1b)
    dqh, dkh, dvv = attn_bwd(sv["qh"].reshape(B, S, -1), sv["kh"].reshape(B, S, -1), sv["vv"].reshape(B, S, -1),
                             dycat[:, SSD_INNER:].reshape(B, S, -1), MLA_HEADS, 2 * LANES, MLA_V, MLA_QK ** -0.5, True,
                             n("mla_attn"), v_idx=_V_BLK)
    gqn, gqr = _mla_gains(sm["mla_q_norm_g"])
    gkn, gkr = _mla_gains(sm["mla_k_norm_g"])
    dq_raw, dgqn, dgqr = qk_bwd(sv["q_raw"], _Q_NOPE, sv["q_raw"], _Q_ROPE, False, gqn, gqr, cos, sin,
                                dqh.reshape(T, -1), n("q_norm_rope"))
    dkv_raw, dk_rope, dgkn, dgkr = qk_bwd(sv["kv_raw"], _K_NOPE, sv["proj"], C_KR // LANES, True, gkn, gkr, cos, sin,
                                          dkh.reshape(T, -1), n("k_norm_rope"), fill=dvv.reshape(T, -1), fill_blk=_V_BLK)
    gs["mla_q_norm_g"] = jnp.concatenate([dgqn[0], dgqr[0, :MLA_ROPE]])
    gs["mla_k_norm_g"] = jnp.concatenate([dgkn[0], dgkr[0, :MLA_ROPE]])
    dqn = mm(dq_raw, G["w_q_b"], "nt", n("q_b_dx"), wv=wv("w_q_b"))
    gw["w_q_b"] = dw("w_q_b", sv["qn"], dq_raw)
    dkvn = mm(dkv_raw, G["w_kv_b"], "nt", n("kv_b_dx"), wv=wv("w_kv_b"))
    gw["w_kv_b"] = dw("w_kv_b", sv["kvn"], dkv_raw)
    proj = sv["proj"]
    dq_a, gs["q_a_norm_g"] = norm_bwd(proj[:, C_QA:C_QA + Q_LORA], sm["q_a_norm_g"], dqn, Q_LORA, False, n("q_a_norm"))
    dkv_a, gs["kv_a_norm_g"] = norm_bwd(proj[:, C_KVA:C_KVA + KV_LORA], sm["kv_a_norm_g"], dkvn, KV_LORA, False,
                                        n("kv_a_norm"))
    z = proj[:, C_Z:C_Z + SSD_INNER]
    dy_raw, dz, gs["ssd_norm_g"] = norm_bwd(sv["y_raw"].reshape(T, -1), sm["ssd_norm_g"], dycat[:, :SSD_INNER],
                                            SSD_INNER // SSD_GROUPS, False, n("ssd_norm"), gate=z)
    dtp = _dt_pad(proj, B, S)
    par = _ssd_par(sm)
    dxs, dBm, dCm, ddt, dpar = ssd_bwd(sv["xbc"], dtp, par, sv["states"], dy_raw.reshape(B, S, -1), n("ssd"))
    gs["dt_bias"] = dpar[:, 0, :SSD_HPG].reshape(-1)
    gs["a_log"] = dpar[:, 1, :SSD_HPG].reshape(-1)
    gs["d_skip"] = dpar[:, 2, :SSD_HPG].reshape(-1)
    dxbc_act = jnp.concatenate([dxs, dBm, dCm], axis=2)
    dxbc, dconv_w, dconv_b = conv_bwd(proj.reshape(B, S, -1), C_XBC // 256, kw["conv_w"], sm["conv_b"].reshape(1, -1), dxbc_act,
                                      n("conv"))
    gs["conv_w"] = dconv_w[:SSD_CONV]
    gs["conv_b"] = dconv_b[0]
    ddt16 = ddt.reshape(T, SSD_GROUPS, LANES)[:, :, :SSD_HPG].reshape(T, 16)
    dproj = jnp.concatenate([dz, dxbc.reshape(T, -1), dq_a, dkv_a, dk_rope, ddt16, jnp.zeros((T, 112), F32)],
                            axis=1).astype(BF16)
    dh1 = mm(dproj, kw["wcat"], "nt", n("in_dx"))
    gw["w_in"] = w_in_grad_storage(mm(sv["h1"], dproj, "tn", n("in_dw"), out_dtype=BF16))
    dx0, gs["attn_norm_g"], dx0b = norm_bwd(sv["x0"], sm["attn_norm_g"], dh1, D_MODEL, False, n("attn_norm"), add=dx1,
                                            bf16_copy=True)
    gs = {k: v.reshape(-1) for k, v in gs.items()}
    return dx0, dx0b, gw, gs


def _small_pack(vals):
    flat = jnp.concatenate([vals[l][k].reshape(-1) for l in range(DEPTH) for k in SMALL_RS])
    n = flat.shape[0]
    rows = -(-n // LANES)
    rows = -(-rows // 8) * 8
    return jnp.pad(flat, (0, rows * LANES - n)).reshape(rows, LANES), n


def _small_unpack(flat, shapes):
    v = flat.reshape(-1)
    out, off = [dict() for _ in range(DEPTH)], 0
    for l in range(DEPTH):
        for k in SMALL_RS:
            sz = shapes[k]
            out[l][k] = v[off:off + sz]
            off += sz
    return out


def _adam_2d(a):
    if a.ndim == 2:
        return a
    return a.reshape(a.shape[0] * a.shape[1], a.shape[2])


def kernel(x, mem, positions, attn_norm_g, w_in, conv_w, conv_b, dt_bias, a_log, d_skip, ssd_norm_g, q_a_norm_g, w_q_b, kv_a_norm_g, w_kv_b, mla_q_norm_g, mla_k_norm_g, w_out, xattn_norm_g, mem_norm_g, w_xq, w_xk, w_xv, xq_norm_g, xk_norm_g, w_xo, ffn_norm_g, w_gate, w_up, w_down, loss_target, m_attn_norm_g, m_w_in, m_conv_w, m_conv_b, m_dt_bias, m_a_log, m_d_skip, m_ssd_norm_g, m_q_a_norm_g, m_w_q_b, m_kv_a_norm_g, m_w_kv_b, m_mla_q_norm_g, m_mla_k_norm_g, m_w_out, m_xattn_norm_g, m_mem_norm_g, m_w_xq, m_w_xk, m_w_xv, m_xq_norm_g, m_xk_norm_g, m_w_xo, m_ffn_norm_g, m_w_gate, m_w_up, m_w_down, v_attn_norm_g, v_w_in, v_conv_w, v_conv_b, v_dt_bias, v_a_log, v_d_skip, v_ssd_norm_g, v_q_a_norm_g, v_w_q_b, v_kv_a_norm_g, v_w_kv_b, v_mla_q_norm_g, v_mla_k_norm_g, v_w_out, v_xattn_norm_g, v_mem_norm_g, v_w_xq, v_w_xk, v_w_xv, v_xq_norm_g, v_xk_norm_g, v_w_xo, v_ffn_norm_g, v_w_gate, v_w_up, v_w_down):
    loc = locals()
    W = {k: loc[k] for k in WEIGHTS}
    Mo = {k: loc["m_" + k] for k in WEIGHTS}
    Vo = {k: loc["v_" + k] for k in WEIGHTS}
    B, S, D = x.shape
    T = B * S
    xf = x.reshape(T, D)
    memf = mem.reshape(-1, D)
    tf = loss_target.reshape(T, D)
    cos, sin = rope_tables(positions.reshape(T, 1).astype(F32))

    cw = gather_all_devices(conv_w.reshape(DEPTH * SSD_CONV, -1), "gather_conv_w")[0::2]
    cw = cw.transpose(1, 0, 2).reshape(DEPTH, SSD_CONV, SSD_CONV_DIM)
    sms = [{k: W[k][l] for k in SMALL} for l in range(DEPTH)]
    me = _me_chip()

    cw, w_in_first = lax.optimization_barrier((cw, W["w_in"][0]))

    def gather_start(l):
        shards = [(wq_to_storage(W[k][l]) if k == "w_q_b" else W[k][l]).astype(BF16) for k in GATHERED]
        if l == 0:
            shards[0] = w_in_first.astype(BF16)
        return ici_start("gather", shards, [(4,) + s.shape for s in shards], "gather_start_l%d" % l)

    def gather_end(state, names, after, tag):
        idx = [GATHERED.index(k) for k in names]
        pick3 = lambda sems: [sems[3 * w + j] for w in idx for j in range(3)]
        sub = dict(kind=state["kind"], srcs=[state["srcs"][w] for w in idx], lands=[state["lands"][w] for w in idx],
                   ssem=pick3(state["ssem"]), rsem=pick3(state["rsem"]))
        shards, lands = ici_wait(sub, after, "gather_wait_" + tag)
        lands = forward_pair(lands, "gather_forward_" + tag)
        return {k: lax.dynamic_update_slice(z, s[None], (me, 0, 0))[None] for k, z, s in zip(names, lands, shards)}

    class Weights(dict):
        def __init__(self, state, groups, after, tag):
            super().__init__(gather_end(state, groups[0], after, tag + "a"))
            self.state, self.later, self.after, self.tag = state, list(groups[1:]), after, tag

        def mark(self, value):
            self.after = value

        def __missing__(self, k):
            names = self.later.pop(0)
            self.update(gather_end(self.state, names, self.after, self.tag + "bcdef"[len(self.later)]))
            return self[k]

    FIRST_GROUPS = (("w_in",), ("w_q_b", "w_kv_b", "w_out", "w_xq", "w_xk", "w_xv", "w_xo"), ("w_gate", "w_up", "w_down"))

    saved, Gs, kws = [], [], []
    h = xf
    state, token = gather_start(0)
    for l in range(DEPTH):
        Gl = Weights(state, FIRST_GROUPS if l == 0 else (GATHERED,), h, "l%d" % l)
        kwl = dict(wcat=wcat_from_storage(Gl["w_in"][0]), conv_w=jnp.pad(cw[l], ((0, 8 - SSD_CONV), (0, 0))))
        Gs.append(Gl)
        kws.append(kwl)
        if l + 1 < DEPTH:
            state, token = gather_start(l + 1)
        h, sv = layer_fwd(h, memf, cos, sin, Gl, kwl, sms[l], B, S, l, dep=token if l + 1 < DEPTH else None)
        saved.append(sv)
    dh, lpart, dhb = loss_head(h, tf)
    loss = lax.psum(lpart[0, 0], ("x", "y", "c"))

    grads, delta, new_m, new_v = {}, {}, {}, {}
    chain = {k: None for k in GATHERED}
    gsmall = [None] * DEPTH
    pending = [[] for _ in range(DEPTH)]

    def begin(gw, l, tag):
        rs_state, tok = reduce_scatter_begin(gw, "l%d%s" % (l, tag))
        pending[l].append((rs_state, "l%d%s" % (l, tag)))
        return tok

    def finish(l, after):
        for rs_state, tag in pending[l]:
            red = reduce_scatter_end(rs_state, after, tag)
            for k, g in red.items():
                g = wq_from_storage(g) if k == "w_q_b" else g
                chain[k] = adamw_layer(W[k], g, Mo[k], Vo[k], l, chain[k], "adamw_%s_l%d" % (k, l))

    token = None
    for l in reversed(range(DEPTH)):
        dh, dhb, gw, gs = layer_bwd(dh, dhb, saved[l], memf, cos, sin, Gs[l], kws[l], sms[l], B, S, l, dep=token,
                                    rs_hook=functools.partial(begin, l=l, tag="_ffn"))
        saved[l] = None
        gsmall[l] = gs
        token = begin(gw, l, "_mix")
        if l + 1 < DEPTH:
            finish(l + 1, token)
    grad_x = dh.reshape(B, S, D)

    spack, _ = _small_pack(gsmall)
    allv = gather_all_devices(spack, "gather_small_grads")
    ssum = sum_arrays([allv[i] for i in range(8)], F32, "sum_small_grads")
    sizes = {k: int(np.prod(W[k].shape[1:])) for k in SMALL}
    sizes["conv_w"] = SSD_CONV * SSD_CONV_DIM
    gsm = _small_unpack(ssum, sizes)
    for k in SMALL_RS:
        g = jnp.stack([gsm[l][k] for l in range(DEPTH)])
        if k == "conv_w":
            cs = SSD_CONV_DIM // 4
            g = lax.dynamic_slice_in_dim(g.reshape(DEPTH, SSD_CONV, SSD_CONV_DIM), _me_chip() * cs, cs, axis=2)
        shp = W[k].shape
        grads[k] = g.reshape(shp)
        d, nm, nv = adamw(_adam_2d(W[k]), _adam_2d(grads[k]), _adam_2d(Mo[k]), _adam_2d(Vo[k]), "adamw_" + k)
        delta[k], new_m[k], new_v[k] = d.reshape(shp), nm.reshape(shp), nv.reshape(shp)

    finish(0, lax.optimization_barrier((delta["ffn_norm_g"], chain["w_down"][1], chain["w_gate"][1], chain["w_up"][1]))[0])
    for k in GATHERED:
        grads[k], delta[k], new_m[k], new_v[k] = chain[k]

    return (loss, grad_x, *[grads[k] for k in WEIGHTS], *[delta[k] for k in WEIGHTS], *[new_m[k] for k in WEIGHTS],
            *[new_v[k] for k in WEIGHTS])
```

```python
import functools
import math

import numpy as np
import jax
import jax.numpy as jnp
from jax import lax
from jax.experimental import pallas as pl
from jax.experimental.pallas import tpu as pltpu

F32 = jnp.float32
BF16 = jnp.bfloat16
MESH = pl.DeviceIdType.MESH
HI = lax.Precision.HIGHEST

D_MODEL = 2048
DEPTH = 4
SSD_INNER = 1024
SSD_HEAD_DIM = 64
SSD_GROUPS = 2
SSD_HPG = 8
SSD_STATE = 128
SSD_CONV = 4
SSD_CHUNK = 128
SSD_CONV_DIM = 1536
SSD_GW = SSD_HPG * SSD_HEAD_DIM
MLA_HEADS = 8
MLA_NOPE = 128
MLA_ROPE = 64
MLA_QK = 192
MLA_V = 128
Q_LORA = 512
KV_LORA = 512
ROPE_THETA = 10000.0
X_HEADS = 4
X_HEAD_DIM = 128
X_INNER = 512
FFN = 5632
IN_COLS = 3664
EPS = 1e-6
ADAM_LR, ADAM_B1, ADAM_B2, ADAM_EPS, ADAM_WD, ADAM_STEP = 0.001, 0.9, 0.999, 1e-08, 0.01, 10

C_Z, C_XBC, C_QA, C_KVA, C_KR, C_DT, CAT_COLS = 0, 1024, 2560, 3072, 3584, 3712, 3840

LANES = 128
VMEM_LIMIT = 56 * 1024 * 1024
MM_VMEM_BUDGET = 40 * 1024 * 1024
MM_STEP_BYTES = 1024 * 1024
NORM_BLOCK_ELEMS = 512 * 1024
ATTN_SCORE_ELEMS = 512 * 1024

SMALL = ("attn_norm_g", "conv_b", "dt_bias", "a_log", "d_skip", "ssd_norm_g", "q_a_norm_g", "kv_a_norm_g",
         "mla_q_norm_g", "mla_k_norm_g", "xattn_norm_g", "mem_norm_g", "xq_norm_g", "xk_norm_g", "ffn_norm_g")
SMALL_RS = SMALL + ("conv_w",)
WEIGHTS = ("attn_norm_g", "w_in", "conv_w", "conv_b", "dt_bias", "a_log", "d_skip", "ssd_norm_g", "q_a_norm_g", "w_q_b",
           "kv_a_norm_g", "w_kv_b", "mla_q_norm_g", "mla_k_norm_g", "w_out", "xattn_norm_g", "mem_norm_g", "w_xq", "w_xk",
           "w_xv", "xq_norm_g", "xk_norm_g", "w_xo", "ffn_norm_g", "w_gate", "w_up", "w_down")


def _pcall(body, **kw):
    return pl.pallas_call(body, **kw)


def _params(sem):
    return pltpu.CompilerParams(dimension_semantics=sem, vmem_limit_bytes=VMEM_LIMIT)


def _tile(n, cands):
    for t in cands:
        if n % t == 0:
            return t
    return n


def _sigmoid(z):
    return 1.0 / (1.0 + jnp.exp(-z))


def _mtile(n, cap):
    if n % LANES:
        return n
    q = n // LANES
    t = LANES * max(d for d in range(1, q + 1) if q % d == 0 and LANES * d <= cap)
    return n if (t < 512 and n <= 2048) else t


def _divisor_tiles(n, cap):
    if n % LANES:
        return [n]
    q = n // LANES
    return [LANES * d for d in range(1, q + 1) if q % d == 0 and LANES * d <= cap] or [n]


def _mm_tiles(M, N, K, m_unit, n_unit, k_unit, a_item, b_item, o_item, has_add):
    best = None
    for tm in _divisor_tiles(m_unit, 2048):
        for tn in _divisor_tiles(n_unit, 2048):
            for tk in _divisor_tiles(k_unit, 2048):
                nk = K // tk
                vmem = 2 * (tm * tk * a_item + tk * tn * b_item + tm * tn * (o_item + (4 if has_add else 0)))
                vmem += tm * tn * 4 * (2 if nk > 1 else 1) + (tm * tk * 2 if a_item == 4 else 0) + (tk * tn * 2 if b_item == 4 else 0)
                if vmem > MM_VMEM_BUDGET:
                    continue
                steps = (M // tm) * (N // tn) * nk
                traffic = (M * K * a_item * (N // tn if nk > 1 else 1) + K * N * b_item * (M // tm)
                           + M * N * (o_item + (4 if has_add else 0)) + steps * MM_STEP_BYTES)
                if nk > 1:
                    traffic += steps * tm * tn * 4
                if best is None or traffic < best[0]:
                    best = (traffic, tm, tn, tk)
    assert best is not None, (M, N, K)
    return best[1:]


def mm(a, b, mode, name, add=None, out_dtype=F32, wv=None, out_wv=None, dep=None):
    kind, l = wv if wv is not None else (None, None)
    if kind == "row":
        b, kind = b[l].reshape(-1, b.shape[-1]), None
    if out_wv == "row":
        res = mm(a, b, mode, name, add=add, out_dtype=out_dtype, wv=wv if kind else None, dep=dep)
        return res.reshape(4, res.shape[0] // 4, res.shape[1])
    if kind == "col":
        _, _, r, c = b.shape
        bshape = (r, 4 * c)
    else:
        bshape = b.shape
    if mode == "nn":
        (M, K), (K2, N) = a.shape, bshape
    elif mode == "nt":
        (M, K), (N, K2) = a.shape, bshape
    else:
        (K, M), (K2, N) = a.shape, bshape
    assert K == K2, (a.shape, bshape, mode)
    m_unit = M
    n_unit = N // 4 if (out_wv == "col" or (kind == "col" and mode == "nn")) else N
    k_unit = K // 4 if (kind == "col" and mode == "nt") else K
    tm, tn, tk = _mm_tiles(M, N, K, m_unit, n_unit, k_unit, a.dtype.itemsize, b.dtype.itemsize,
                           jnp.dtype(out_dtype).itemsize, add is not None)
    nk = K // tk
    qm, qn, qk = m_unit // tm, n_unit // tn, k_unit // tk
    dn = {"nn": (((1,), (0,)), ((), ())), "nt": (((1,), (1,)), ((), ())), "tn": (((0,), (0,)), ((), ()))}[mode]
    has_add = add is not None

    def body(a_ref, b_ref, *rest):
        rest = list(rest)
        c_ref = rest.pop(0) if has_add else None
        if dep is not None:
            rest.pop(0)
        o_ref = rest.pop(0)
        d = lax.dot_general(a_ref[...].astype(BF16), b_ref[...].astype(BF16), dn, preferred_element_type=F32)

        def finish(res):
            if has_add:
                res = res + c_ref[...].astype(F32)
            o_ref[...] = res.astype(out_dtype)

        if nk == 1:
            finish(d)
            return
        acc = rest.pop(0)
        k = pl.program_id(2)

        @pl.when(k == 0)
        def _():
            acc[...] = d

        if nk > 2:
            @pl.when((k > 0) & (k < nk - 1))
            def _():
                acc[...] += d

        @pl.when(k == nk - 1)
        def _():
            finish(acc[...] + d)

    if mode == "tn":
        a_spec = pl.BlockSpec((tk, tm), lambda i, j, k: (k, i))
    else:
        a_spec = pl.BlockSpec((tm, tk), lambda i, j, k: (i, k))
    if kind is None:
        b_spec = (pl.BlockSpec((tn, tk), lambda i, j, k: (j, k)) if mode == "nt"
                  else pl.BlockSpec((tk, tn), lambda i, j, k: (k, j)))
    elif mode == "nn":
        b_spec = pl.BlockSpec((None, None, tk, tn), lambda i, j, k: (l, j // qn, k, j % qn))
    elif mode == "nt":
        b_spec = pl.BlockSpec((None, None, tn, tk), lambda i, j, k: (l, k // qk, j, k % qk))
    else:
        raise ValueError((mode, kind))
    if out_wv == "col":
        o_spec = pl.BlockSpec((None, tm, tn), lambda i, j, k: (j // qn, i, j % qn))
        out_shape = jax.ShapeDtypeStruct((4, M, N // 4), out_dtype)
    else:
        o_spec = pl.BlockSpec((tm, tn), lambda i, j, k: (i, j))
        out_shape = jax.ShapeDtypeStruct((M, N), out_dtype)
    in_specs = [a_spec, b_spec] + ([o_spec] if has_add else []) + ([pl.BlockSpec(memory_space=pl.ANY)] if dep is not None else [])
    args = (a, b) + ((add,) if has_add else ()) + ((dep,) if dep is not None else ())
    return _pcall(
        body, name=name, grid=(M // tm, N // tn, nk), in_specs=in_specs, out_specs=o_spec,
        out_shape=out_shape, scratch_shapes=[pltpu.VMEM((tm, tn), F32)] if nk > 1 else [],
        compiler_params=_params(("parallel", "parallel", "arbitrary")))(*args)


def norm_fwd(x, gain, W, shared, name, gate=None, out_dtype=BF16, dep=None):
    T, C = x.shape
    nb = C // W
    tq = _tile(T, tuple(t for t in (4096, 2048, 1024, 512, 256, 128) if t * W <= NORM_BLOCK_ELEMS))
    g2 = gain.reshape(1, -1).astype(F32)
    gated = gate is not None

    def body(*refs):
        o_ref = refs[-1]
        if gated:
            x_ref, z_ref, g_ref = refs[:3]
        else:
            x_ref, g_ref = refs[:2]
        u = x_ref[...].astype(F32)
        if gated:
            z = z_ref[...]
            u = u * (z * _sigmoid(z))
        ms = jnp.mean(u * u, axis=-1, keepdims=True)
        o_ref[...] = (u * lax.rsqrt(ms + EPS) * g_ref[...]).astype(out_dtype)

    xs = pl.BlockSpec((tq, W), lambda j, i: (i, j))
    gs = pl.BlockSpec((1, W), (lambda j, i: (0, 0)) if shared else (lambda j, i: (0, j)))
    in_specs = [xs] + ([xs] if gated else []) + [gs] + ([pl.BlockSpec(memory_space=pl.ANY)] if dep is not None else [])
    args = (x,) + ((gate,) if gated else ()) + (g2,) + ((dep,) if dep is not None else ())
    return _pcall(body, name=name, grid=(nb, T // tq), in_specs=in_specs, out_specs=xs,
                  out_shape=jax.ShapeDtypeStruct((T, C), out_dtype),
                  compiler_params=_params(("parallel", "parallel")))(*args)


def norm_bwd(x, gain, dy, W, shared, name, gate=None, add=None, bf16_copy=False):
    T, C = x.shape
    nb = C // W
    tq = _tile(T, tuple(t for t in (4096, 2048, 1024, 512, 256, 128) if t * W <= NORM_BLOCK_ELEMS))
    g2 = gain.reshape(1, -1).astype(F32)
    gated = gate is not None
    has_add = add is not None
    GC = W if shared else C

    def body(*refs):
        refs = list(refs)
        x_ref = refs.pop(0)
        z_ref = refs.pop(0) if gated else None
        g_ref = refs.pop(0)
        dy_ref = refs.pop(0)
        a_ref = refs.pop(0) if has_add else None
        dx_ref = refs.pop(0)
        dz_ref = refs.pop(0) if gated else None
        dg_ref = refs.pop(0)
        j, i = pl.program_id(0), pl.program_id(1)
        first = (i == 0) & (j == 0) if shared else (i == 0)

        @pl.when(first)
        def _():
            dg_ref[...] = jnp.zeros_like(dg_ref)

        xv = x_ref[...].astype(F32)
        if gated:
            z = z_ref[...]
            s = _sigmoid(z)
            u = xv * (z * s)
        else:
            u = xv
        rstd = lax.rsqrt(jnp.mean(u * u, axis=-1, keepdims=True) + EPS)
        uh = u * rstd
        dyv = dy_ref[...].astype(F32)
        dg_ref[...] += jnp.sum(dyv * uh, axis=0, keepdims=True)
        dyg = dyv * g_ref[...]
        du = rstd * (dyg - uh * jnp.mean(dyg * uh, axis=-1, keepdims=True))
        if gated:
            dx = du * (z * s)
            dz_ref[...] = du * xv * (s * (1.0 + z * (1.0 - s)))
        else:
            dx = du
        if has_add:
            dx = dx + a_ref[...]
        dx_ref[...] = dx
        if bf16_copy:
            refs.pop(0)[...] = dx.astype(BF16)

    xs = pl.BlockSpec((tq, W), lambda j, i: (i, j))
    gs = pl.BlockSpec((1, W), (lambda j, i: (0, 0)) if shared else (lambda j, i: (0, j)))
    in_specs = [xs] + ([xs] if gated else []) + [gs, xs] + ([xs] if has_add else [])
    args = (x,) + ((gate,) if gated else ()) + (g2, dy) + ((add,) if has_add else ())
    out_specs = [xs] + ([xs] if gated else []) + [gs] + ([xs] if bf16_copy else [])
    out_shape = [jax.ShapeDtypeStruct((T, C), F32)] + ([jax.ShapeDtypeStruct((T, C), F32)] if gated else []) \
        + [jax.ShapeDtypeStruct((1, GC), F32)] + ([jax.ShapeDtypeStruct((T, C), BF16)] if bf16_copy else [])
    return _pcall(body, name=name, grid=(nb, T // tq), in_specs=in_specs, out_specs=out_specs, out_shape=out_shape,
                  compiler_params=_params(("arbitrary", "arbitrary")))(*args)


def _inv_freq_row():
    inv = 1.0 / (ROPE_THETA ** (np.arange(0, MLA_ROPE, 2, dtype=np.float32) / MLA_ROPE))
    row = np.zeros((1, LANES), np.float32)
    row[0, :32] = inv
    row[0, 32:64] = inv
    return jnp.asarray(row)


def rope_tables(pos):
    T = pos.shape[0]
    tq = _tile(T, (512, 256, 128))

    def body(p_ref, f_ref, c_ref, s_ref):
        ang = p_ref[...] * f_ref[...]
        lane = lax.broadcasted_iota(jnp.int32, ang.shape, 1)
        c_ref[...] = jnp.where(lane < 64, jnp.cos(ang), 0.0)
        sn = jnp.sin(ang)
        s_ref[...] = jnp.where(lane < 32, -sn, jnp.where(lane < 64, sn, 0.0))

    ts = pl.BlockSpec((tq, LANES), lambda i: (i, 0))
    return _pcall(body, name="rope_tables", grid=(T // tq,),
                  in_specs=[pl.BlockSpec((tq, 1), lambda i: (i, 0)), pl.BlockSpec((1, LANES), lambda i: (0, 0))],
                  out_specs=[ts, ts], out_shape=[jax.ShapeDtypeStruct((T, LANES), F32)] * 2,
                  compiler_params=_params(("parallel",)))(pos, _inv_freq_row())


def _swap_halves(r):
    lane = lax.broadcasted_iota(jnp.int32, r.shape, 1)
    return jnp.where(lane < 32, pltpu.roll(r, 96, 1), pltpu.roll(r, 32, 1))


def qk_fwd(nope_arr, nope_blk0, rope_arr, rope_blk0, rope_shared, gn, gr, cos, sin, name):
    T = nope_arr.shape[0]
    tq = _tile(T, (256, 128))

    col = lambda blk: slice(blk * LANES, (blk + 1) * LANES)

    def body(n_ref, r_ref, gn_ref, gr_ref, c_ref, s_ref, o_ref):
        for h in range(MLA_HEADS):
            n = n_ref[:, col(nope_blk0(h))]
            r = r_ref[...] if rope_shared else n_ref[:, col(rope_blk0(h))]
            ms = (jnp.sum(n * n, axis=-1, keepdims=True) + jnp.sum(r * r, axis=-1, keepdims=True)) * (1.0 / MLA_QK)
            rstd = lax.rsqrt(ms + EPS)
            rn = r * rstd * gr_ref[...]
            rr = rn * c_ref[...] + _swap_halves(rn) * s_ref[...]
            o_ref[:, col(2 * h)] = (n * rstd * gn_ref[...]).astype(BF16)
            o_ref[:, col(2 * h + 1)] = rr.astype(BF16)

    wide = lambda arr: pl.BlockSpec((tq, arr.shape[1]), lambda i: (i, 0))
    assert rope_shared or rope_arr is nope_arr
    rspec = pl.BlockSpec((tq, LANES), lambda i: (i, rope_blk0)) if rope_shared else pl.BlockSpec((8, LANES), lambda i: (0, 0))
    row = pl.BlockSpec((1, LANES), lambda i: (0, 0))
    tab = pl.BlockSpec((tq, LANES), lambda i: (i, 0))
    return _pcall(body, name=name, grid=(T // tq,),
                  in_specs=[wide(nope_arr), rspec, row, row, tab, tab],
                  out_specs=pl.BlockSpec((tq, MLA_HEADS * 2 * LANES), lambda i: (i, 0)),
                  out_shape=jax.ShapeDtypeStruct((T, MLA_HEADS * 2 * LANES), BF16),
                  compiler_params=_params(("parallel",)))(nope_arr, rope_arr, gn, gr, cos, sin)


def qk_bwd(nope_arr, nope_blk0, rope_arr, rope_blk0, rope_shared, gn, gr, cos, sin, dout, name, fill=None, fill_blk=None):
    T = nope_arr.shape[0]
    tq = _tile(T, (256, 128))
    col = lambda blk: slice(blk * LANES, (blk + 1) * LANES)

    def body(*refs):
        refs = list(refs)
        n_ref, r_ref, gn_ref, gr_ref, c_ref, s_ref, d_ref = refs[:7]
        f_ref = refs[7] if rope_shared else None
        outs = refs[8:] if rope_shared else refs[7:]
        draw_ref = outs[0]
        dsh_ref = outs[1] if rope_shared else None
        dgn_ref, dgr_ref = outs[-2:]

        @pl.when(pl.program_id(0) == 0)
        def _():
            dgn_ref[...] = jnp.zeros_like(dgn_ref)
            dgr_ref[...] = jnp.zeros_like(dgr_ref)

        dgn = jnp.zeros((1, LANES), F32)
        dgr = jnp.zeros((1, LANES), F32)
        dshared = jnp.zeros((tq, LANES), F32)
        for h in range(MLA_HEADS):
            n = n_ref[:, col(nope_blk0(h))]
            r = r_ref[...] if rope_shared else n_ref[:, col(rope_blk0(h))]
            ms = (jnp.sum(n * n, axis=-1, keepdims=True) + jnp.sum(r * r, axis=-1, keepdims=True)) * (1.0 / MLA_QK)
            rstd = lax.rsqrt(ms + EPS)
            nh = n * rstd
            rh = r * rstd
            d_n = d_ref[:, col(2 * h)].astype(F32)
            d_rr = d_ref[:, col(2 * h + 1)].astype(F32)
            d_rn = d_rr * c_ref[...] - _swap_halves(d_rr) * s_ref[...]
            dgn = dgn + jnp.sum(d_n * nh, axis=0, keepdims=True)
            dgr = dgr + jnp.sum(d_rn * rh, axis=0, keepdims=True)
            dng = d_n * gn_ref[...]
            drg = d_rn * gr_ref[...]
            mean = (jnp.sum(dng * nh, axis=-1, keepdims=True) + jnp.sum(drg * rh, axis=-1, keepdims=True)) * (1.0 / MLA_QK)
            draw_ref[:, col(nope_blk0(h))] = (rstd * (dng - nh * mean)).astype(BF16)
            drope = rstd * (drg - rh * mean)
            if rope_shared:
                dshared = dshared + drope
                draw_ref[:, col(fill_blk(h))] = f_ref[:, col(h)].astype(BF16)
            else:
                draw_ref[:, col(rope_blk0(h))] = drope.astype(BF16)
        dgn_ref[...] += dgn
        dgr_ref[...] += dgr
        if rope_shared:
            dsh_ref[...] = dshared

    wide = lambda arr: pl.BlockSpec((tq, arr.shape[1]), lambda i: (i, 0))
    assert rope_shared or rope_arr is nope_arr
    rspec = pl.BlockSpec((tq, LANES), lambda i: (i, rope_blk0)) if rope_shared else pl.BlockSpec((8, LANES), lambda i: (0, 0))
    row = pl.BlockSpec((1, LANES), lambda i: (0, 0))
    tab = pl.BlockSpec((tq, LANES), lambda i: (i, 0))
    raw = pl.BlockSpec((tq, 2 * MLA_HEADS * LANES), lambda i: (i, 0))
    in_specs = [wide(nope_arr), rspec, row, row, tab, tab, wide(dout)] + ([wide(fill)] if rope_shared else [])
    out_specs = [raw] + ([tab] if rope_shared else []) + [row, row]
    out_shape = ([jax.ShapeDtypeStruct((T, 2 * MLA_HEADS * LANES), BF16)]
                 + ([jax.ShapeDtypeStruct((T, LANES), F32)] if rope_shared else [])
                 + [jax.ShapeDtypeStruct((1, LANES), F32)] * 2)
    args = (nope_arr, rope_arr, gn, gr, cos, sin, dout) + ((fill,) if rope_shared else ())
    return _pcall(body, name=name, grid=(T // tq,), in_specs=in_specs, out_specs=out_specs, out_shape=out_shape,
                  compiler_params=_params(("arbitrary",)))(*args)


def _scores(q, k, scale, causal, row0):
    s = lax.dot_general(q, k, (((1,), (1,)), ((), ())), preferred_element_type=F32) * scale
    if causal:
        qi = row0 + lax.broadcasted_iota(jnp.int32, s.shape, 0)
        ki = lax.broadcasted_iota(jnp.int32, s.shape, 1)
        s = jnp.where(qi >= ki, s, -1e30)
    return s


def attn_fwd(q, k, v, H, dqk, dv, scale, causal, name, v_idx=lambda h: h):
    B, S, _ = q.shape
    Sk = k.shape[1]
    tq = _tile(S, tuple(t for t in (2048, 1024, 512, 256, 128) if t * Sk <= ATTN_SCORE_ELEMS))

    def tile(q_ref, k_ref, v_ref, o_ref, row0, kl):
        s = _scores(q_ref[0], k_ref[0, :kl], scale, causal, row0)
        p = jnp.exp(s - jnp.max(s, axis=-1, keepdims=True))
        l = jnp.sum(p, axis=-1, keepdims=True)
        o = lax.dot_general(p.astype(BF16), v_ref[0, :kl], (((1,), (0,)), ((), ())), preferred_element_type=F32)
        o_ref[0] = (o / l).astype(BF16)

    def body(q_ref, k_ref, v_ref, o_ref):
        if not causal:
            tile(q_ref, k_ref, v_ref, o_ref, 0, Sk)
            return
        for qi in range(S // tq):
            @pl.when(pl.program_id(2) == qi)
            def _(qi=qi):
                tile(q_ref, k_ref, v_ref, o_ref, qi * tq, (qi + 1) * tq)

    return _pcall(body, name=name, grid=(B, H, S // tq),
                  in_specs=[pl.BlockSpec((1, tq, dqk), lambda b, h, i: (b, i, h)),
                            pl.BlockSpec((1, Sk, dqk), lambda b, h, i: (b, 0, h)),
                            pl.BlockSpec((1, Sk, dv), lambda b, h, i: (b, 0, v_idx(h)))],
                  out_specs=pl.BlockSpec((1, tq, dv), lambda b, h, i: (b, i, h)),
                  out_shape=jax.ShapeDtypeStruct((B, S, H * dv), BF16),
                  compiler_params=_params(("parallel", "parallel", "parallel")))(q, k, v)


def attn_bwd(q, k, v, do, H, dqk, dv, scale, causal, name, v_idx=lambda h: h):
    B, S, _ = q.shape
    Sk = k.shape[1]
    tq = _tile(S, tuple(t for t in (2048, 1024, 512, 256, 128) if t * Sk <= ATTN_SCORE_ELEMS))

    def body(q_ref, k_ref, v_ref, do_ref, dq_ref, dk_ref, dv_ref):
        i = pl.program_id(2)

        @pl.when(i == 0)
        def _():
            dk_ref[...] = jnp.zeros_like(dk_ref)
            dv_ref[...] = jnp.zeros_like(dv_ref)

        def tile(row0, kl):
            qv, kv, vv = q_ref[0], k_ref[0, :kl], v_ref[0, :kl]
            s = _scores(qv, kv, scale, causal, row0)
            p = jnp.exp(s - jnp.max(s, axis=-1, keepdims=True))
            p = p / jnp.sum(p, axis=-1, keepdims=True)
            dob = do_ref[0].astype(BF16)
            pb = p.astype(BF16)
            dv_ref[0, :kl] += lax.dot_general(pb, dob, (((0,), (0,)), ((), ())), preferred_element_type=F32)
            dp = lax.dot_general(dob, vv, (((1,), (1,)), ((), ())), preferred_element_type=F32)
            delta = jnp.sum(p * dp, axis=-1, keepdims=True)
            ds = (p * (dp - delta) * scale).astype(BF16)
            dq_ref[0] = lax.dot_general(ds, kv, (((1,), (0,)), ((), ())), preferred_element_type=F32)
            dk_ref[0, :kl] += lax.dot_general(ds, qv, (((0,), (0,)), ((), ())), preferred_element_type=F32)

        if not causal:
            tile(0, Sk)
            return
        for qi in range(S // tq):
            @pl.when(i == qi)
            def _(qi=qi):
                tile(qi * tq, (qi + 1) * tq)

    qs = pl.BlockSpec((1, tq, dqk), lambda b, h, i: (b, i, h))
    ks = pl.BlockSpec((1, Sk, dqk), lambda b, h, i: (b, 0, h))
    vs = pl.BlockSpec((1, Sk, dv), lambda b, h, i: (b, 0, h))
    vin = pl.BlockSpec((1, Sk, dv), lambda b, h, i: (b, 0, v_idx(h)))
    return _pcall(body, name=name, grid=(B, H, S // tq),
                  in_specs=[qs, ks, vin, pl.BlockSpec((1, tq, dv), lambda b, h, i: (b, i, h))],
                  out_specs=[qs, ks, vs],
                  out_shape=[jax.ShapeDtypeStruct((B, S, H * dqk), F32), jax.ShapeDtypeStruct((B, Sk, H * dqk), F32),
                             jax.ShapeDtypeStruct((B, Sk, H * dv), F32)],
                  compiler_params=_params(("parallel", "parallel", "arbitrary")))(q, k, v, do)


def _shift_down(u, j):
    if j == 0:
        return u
    row = lax.broadcasted_iota(jnp.int32, u.shape, 0)
    return jnp.where(row >= j, pltpu.roll(u, j, 0), 0.0)


def _shift_up(u, j):
    if j == 0:
        return u
    n = u.shape[0]
    row = lax.broadcasted_iota(jnp.int32, u.shape, 0)
    return jnp.where(row < n - j, pltpu.roll(u, n - j, 0), 0.0)


def conv_fwd(u, col0_blk, w8, b, name):
    B, S, _ = u.shape
    tc = 256
    nb = SSD_CONV_DIM // tc

    def body(u_ref, w_ref, b_ref, o_ref):
        uv = u_ref[0]
        pre = b_ref[...] + jnp.zeros_like(uv)
        for kk in range(SSD_CONV):
            pre = pre + w_ref[kk:kk + 1, :] * _shift_down(uv, SSD_CONV - 1 - kk)
        o_ref[0] = pre * _sigmoid(pre)

    return _pcall(body, name=name, grid=(B, nb),
                  in_specs=[pl.BlockSpec((1, S, tc), lambda bb, j: (bb, 0, col0_blk + j)),
                            pl.BlockSpec((8, tc), lambda bb, j: (0, j)), pl.BlockSpec((1, tc), lambda bb, j: (0, j))],
                  out_specs=pl.BlockSpec((1, S, tc), lambda bb, j: (bb, 0, j)),
                  out_shape=jax.ShapeDtypeStruct((B, S, SSD_CONV_DIM), F32),
                  compiler_params=_params(("parallel", "parallel")))(u, w8, b)


def conv_bwd(u, col0_blk, w8, b, dy, name):
    B, S, _ = u.shape
    tc = 256
    nb = SSD_CONV_DIM // tc

    def body(u_ref, w_ref, b_ref, dy_ref, du_ref, dw_ref, db_ref):
        bb = pl.program_id(1)

        @pl.when(bb == 0)
        def _():
            dw_ref[...] = jnp.zeros_like(dw_ref)
            db_ref[...] = jnp.zeros_like(db_ref)

        uv = u_ref[0]
        pre = b_ref[...] + jnp.zeros_like(uv)
        for kk in range(SSD_CONV):
            pre = pre + w_ref[kk:kk + 1, :] * _shift_down(uv, SSD_CONV - 1 - kk)
        s = _sigmoid(pre)
        dpre = dy_ref[0] * (s * (1.0 + pre * (1.0 - s)))
        du = jnp.zeros_like(uv)
        for kk in range(SSD_CONV):
            j = SSD_CONV - 1 - kk
            du = du + w_ref[kk:kk + 1, :] * _shift_up(dpre, j)
            dw_ref[kk:kk + 1, :] += jnp.sum(dpre * _shift_down(uv, j), axis=0, keepdims=True)
        db_ref[...] += jnp.sum(dpre, axis=0, keepdims=True)
        du_ref[0] = du

    return _pcall(body, name=name, grid=(nb, B),
                  in_specs=[pl.BlockSpec((1, S, tc), lambda j, bb: (bb, 0, col0_blk + j)),
                            pl.BlockSpec((8, tc), lambda j, bb: (0, j)), pl.BlockSpec((1, tc), lambda j, bb: (0, j)),
                            pl.BlockSpec((1, S, tc), lambda j, bb: (bb, 0, j))],
                  out_specs=[pl.BlockSpec((1, S, tc), lambda j, bb: (bb, 0, j)),
                             pl.BlockSpec((8, tc), lambda j, bb: (0, j)), pl.BlockSpec((1, tc), lambda j, bb: (0, j))],
                  out_shape=[jax.ShapeDtypeStruct((B, S, SSD_CONV_DIM), F32), jax.ShapeDtypeStruct((8, SSD_CONV_DIM), F32),
                             jax.ShapeDtypeStruct((1, SSD_CONV_DIM), F32)],
                  compiler_params=_params(("arbitrary", "arbitrary")))(u, w8, b, dy)


def _expand_mat():
    e = np.zeros((LANES, SSD_GW), np.float32)
    for h in range(SSD_HPG):
        e[h, h * SSD_HEAD_DIM:(h + 1) * SSD_HEAD_DIM] = 1.0
    return jnp.asarray(e)


def _tri_mat():
    return jnp.asarray(np.tril(np.ones((SSD_CHUNK, SSD_CHUNK), np.float32)))


def _dotf(a, b, dn=(((1,), (0,)), ((), ()))):
    return lax.dot_general(a, b, dn, precision=HI, preferred_element_type=F32)


def _dotb(a, b, dn=(((1,), (0,)), ((), ()))):
    return lax.dot_general(a.astype(BF16), b.astype(BF16), dn, preferred_element_type=F32)


NT = (((1,), (1,)), ((), ()))
TN = (((0,), (0,)), ((), ()))


def _ssd_common(dt_ref, par_ref, tri_ref, e_ref):
    dtr = dt_ref[0]
    xb = dtr + par_ref[0:1, :]
    dt = jnp.maximum(xb, 0.0) + jnp.log(1.0 + jnp.exp(-jnp.abs(xb)))
    A = -jnp.exp(par_ref[1:2, :])
    a = dt * A
    acs = _dotf(tri_ref[...], a)
    alast = acs[SSD_CHUNK - 1:SSD_CHUNK, :]
    E = e_ref[...]
    acsE = _dotf(acs, E)
    alastE = acsE[SSD_CHUNK - 1:SSD_CHUNK, :]
    return dict(xb=xb, dt=dt, A=A, acs=acs, acsT=acs.T, ealast=jnp.exp(alast), w=jnp.exp(alast - acs),
                dtE=_dotf(dt, E), eacsE=jnp.exp(acsE), wE=jnp.exp(alastE - acsE), ealastE=jnp.exp(alastE),
                DE=_dotf(par_ref[...], E)[2:3, :])


def _decay(c, e):
    diff = c["acs"][:, e:e + 1] - c["acsT"][e:e + 1, :]
    li = lax.broadcasted_iota(jnp.int32, diff.shape, 0)
    si = lax.broadcasted_iota(jnp.int32, diff.shape, 1)
    return jnp.where(li >= si, jnp.exp(jnp.minimum(diff, 0.0)), 0.0)


def ssd_fwd(xbc, dtp, par, name):
    B, S, _ = xbc.shape
    nc = S // SSD_CHUNK
    L, N, GW, P = SSD_CHUNK, SSD_STATE, SSD_GW, SSD_HEAD_DIM

    def body(x_ref, b_ref, c_ref, dt_ref, par_ref, tri_ref, e_ref, y_ref, st_ref, S_scr):
        @pl.when(pl.program_id(2) == 0)
        def _():
            S_scr[...] = jnp.zeros_like(S_scr)

        Sin = S_scr[...]
        st_ref[0, 0, 0] = Sin
        c = _ssd_common(dt_ref, par_ref.at[0], tri_ref, e_ref)
        X = x_ref[0]
        Bm = b_ref[0]
        Cm = c_ref[0]
        xdt = X * c["dtE"]
        CB = _dotb(Cm, Bm, NT)
        y_ref[0] = c["eacsE"] * _dotb(Cm, Sin) + c["DE"] * X
        for e in range(SSD_HPG):
            M = CB * _decay(c, e)
            y_ref[0, :, e * P:(e + 1) * P] += _dotb(M, xdt[:, e * P:(e + 1) * P])
        S_scr[...] = c["ealastE"] * Sin + _dotb(Bm, xdt * c["wE"], TN)

    return _pcall(
        body, name=name, grid=(SSD_GROUPS, B, nc),
        in_specs=[pl.BlockSpec((1, L, GW), lambda g, b, ci: (b, ci, g)),
                  pl.BlockSpec((1, L, N), lambda g, b, ci: (b, ci, 8 + g)),
                  pl.BlockSpec((1, L, N), lambda g, b, ci: (b, ci, 10 + g)),
                  pl.BlockSpec((1, L, LANES), lambda g, b, ci: (b, ci, g)),
                  pl.BlockSpec((1, 8, LANES), lambda g, b, ci: (g, 0, 0)),
                  pl.BlockSpec((L, L), lambda g, b, ci: (0, 0)),
                  pl.BlockSpec((LANES, GW), lambda g, b, ci: (0, 0))],
        out_specs=[pl.BlockSpec((1, L, GW), lambda g, b, ci: (b, ci, g)),
                   pl.BlockSpec((1, 1, 1, N, GW), lambda g, b, ci: (g, b, ci, 0, 0))],
        out_shape=[jax.ShapeDtypeStruct((B, S, SSD_INNER), F32), jax.ShapeDtypeStruct((SSD_GROUPS, B, nc, N, GW), F32)],
        scratch_shapes=[pltpu.VMEM((N, GW), F32)],
        compiler_params=_params(("arbitrary", "arbitrary", "arbitrary")))(xbc, xbc, xbc, dtp, par, _tri_mat(), _expand_mat())


def ssd_bwd(xbc, dtp, par, states, dy, name):
    B, S, _ = xbc.shape
    nc = S // SSD_CHUNK
    L, N, GW, P = SSD_CHUNK, SSD_STATE, SSD_GW, SSD_HEAD_DIM

    def body(x_ref, b_ref, c_ref, dt_ref, par_ref, tri_ref, e_ref, st_ref, dy_ref,
             dx_ref, db_ref, dc_ref, ddt_ref, dpar_ref, dS_scr, dxdt_scr):
        @pl.when(pl.program_id(2) == 0)
        def _():
            dS_scr[...] = jnp.zeros_like(dS_scr)

        @pl.when((pl.program_id(1) == 0) & (pl.program_id(2) == 0))
        def _():
            dpar_ref[...] = jnp.zeros_like(dpar_ref)

        c = _ssd_common(dt_ref, par_ref.at[0], tri_ref, e_ref)
        E = e_ref[...]
        red = lambda t: _dotf(t, E, NT)
        red_row = lambda t: red(jnp.broadcast_to(t, (8, GW)))[0:1, :]
        X, Bm, Cm = x_ref[0], b_ref[0], c_ref[0]
        Sin = st_ref[0, 0, 0]
        dY = dy_ref[0]
        dSo = dS_scr[...]
        xdt = X * c["dtE"]
        CB = _dotb(Cm, Bm, NT)
        dD = red_row(jnp.sum(dY * X, axis=0, keepdims=True))
        yoff = c["eacsE"] * _dotb(Cm, Sin)
        dacs = red(dY * yoff)
        dYe = dY * c["eacsE"]
        dC = _dotb(dYe, Sin, NT)
        dSin = _dotb(Cm, dYe, TN) + c["ealastE"] * dSo
        dB = _dotb(xdt * c["wE"], dSo, NT)
        dxw = _dotb(Bm, dSo)
        t = red(dxw * xdt) * c["w"]
        dacs = dacs - t
        dalast = jnp.sum(t, axis=0, keepdims=True) + c["ealast"] * red_row(jnp.sum(Sin * dSo, axis=0, keepdims=True))
        dxdt_scr[...] = dxw * c["wE"]
        dCB = jnp.zeros((L, L), F32)
        dacsT = jnp.zeros((LANES, L), F32)
        lane = lax.broadcasted_iota(jnp.int32, (1, LANES), 1)
        subl = lax.broadcasted_iota(jnp.int32, (LANES, 1), 0)
        for e in range(SSD_HPG):
            Lm = _decay(c, e)
            M = CB * Lm
            dYh = dY[:, e * P:(e + 1) * P]
            dM = _dotb(dYh, xdt[:, e * P:(e + 1) * P], NT)
            dxdt_scr[:, e * P:(e + 1) * P] += _dotb(M, dYh, TN)
            dCBe = dM * Lm
            dCB = dCB + dCBe
            Gm = dCBe * CB
            dacs = dacs + jnp.sum(Gm, axis=1, keepdims=True) * (lane == e).astype(F32)
            dacsT = dacsT - jnp.sum(Gm, axis=0, keepdims=True) * (subl == e).astype(F32)
        dC = dC + _dotb(dCB, Bm)
        dB = dB + _dotb(dCB, Cm, TN)
        dacs = dacs + dacsT.T
        row = lax.broadcasted_iota(jnp.int32, (L, 1), 0)
        dacs = dacs + jnp.where(row == L - 1, dalast, 0.0)
        da = _dotf(tri_ref[...], dacs, TN)
        dxdt = dxdt_scr[...]
        ddt = da * c["A"] + red(dxdt * X)
        dA = jnp.sum(da * c["dt"], axis=0, keepdims=True)
        ddtr = ddt * _sigmoid(c["xb"])
        dx_ref[0] = c["DE"] * dY + dxdt * c["dtE"]
        db_ref[0] = dB
        dc_ref[0] = dC
        ddt_ref[0] = ddtr
        dpar_ref[0, 0:1, :] += jnp.sum(ddtr, axis=0, keepdims=True)
        dpar_ref[0, 1:2, :] += dA * c["A"]
        dpar_ref[0, 2:3, :] += dD
        dS_scr[...] = dSin

    rc = lambda ci: nc - 1 - ci
    xspec = pl.BlockSpec((1, L, GW), lambda g, b, ci: (b, rc(ci), g))
    return _pcall(
        body, name=name, grid=(SSD_GROUPS, B, nc),
        in_specs=[xspec,
                  pl.BlockSpec((1, L, N), lambda g, b, ci: (b, rc(ci), 8 + g)),
                  pl.BlockSpec((1, L, N), lambda g, b, ci: (b, rc(ci), 10 + g)),
                  pl.BlockSpec((1, L, LANES), lambda g, b, ci: (b, rc(ci), g)),
                  pl.BlockSpec((1, 8, LANES), lambda g, b, ci: (g, 0, 0)),
                  pl.BlockSpec((L, L), lambda g, b, ci: (0, 0)),
                  pl.BlockSpec((LANES, GW), lambda g, b, ci: (0, 0)),
                  pl.BlockSpec((1, 1, 1, N, GW), lambda g, b, ci: (g, b, rc(ci), 0, 0)),
                  xspec],
        out_specs=[xspec,
                   pl.BlockSpec((1, L, N), lambda g, b, ci: (b, rc(ci), g)),
                   pl.BlockSpec((1, L, N), lambda g, b, ci: (b, rc(ci), g)),
                   pl.BlockSpec((1, L, LANES), lambda g, b, ci: (b, rc(ci), g)),
                   pl.BlockSpec((1, 8, LANES), lambda g, b, ci: (g, 0, 0))],
        out_shape=[jax.ShapeDtypeStruct((B, S, SSD_INNER), F32), jax.ShapeDtypeStruct((B, S, SSD_GROUPS * N), F32),
                   jax.ShapeDtypeStruct((B, S, SSD_GROUPS * N), F32), jax.ShapeDtypeStruct((B, S, SSD_GROUPS * LANES), F32),
                   jax.ShapeDtypeStruct((SSD_GROUPS, 8, LANES), F32)],
        scratch_shapes=[pltpu.VMEM((N, GW), F32), pltpu.VMEM((L, GW), F32)],
        compiler_params=_params(("arbitrary", "arbitrary", "arbitrary")))(
            xbc, xbc, xbc, dtp, par, _tri_mat(), _expand_mat(), states, dy)


def gate_up_act(h, wg, wu, name):
    T, D = h.shape
    c = wg.shape[-1]
    tm = _tile(T, (512, 256, 128))

    def body(h_ref, g_w, u_w, g_ref, u_ref, a_ref):
        hv = h_ref[...]
        g = jnp.dot(hv, g_w[...], preferred_element_type=F32)
        u = jnp.dot(hv, u_w[...], preferred_element_type=F32)
        g_ref[...] = g.astype(BF16)
        u_ref[...] = u.astype(BF16)
        a_ref[...] = (g * _sigmoid(g) * u).astype(BF16)

    ws = pl.BlockSpec((None, None, D, c), lambda i, j: (0, j, 0, 0))
    os_ = pl.BlockSpec((tm, c), lambda i, j: (i, j))
    return _pcall(body, name=name, grid=(T // tm, 4), in_specs=[pl.BlockSpec((tm, D), lambda i, j: (i, 0)), ws, ws],
                  out_specs=[os_, os_, os_], out_shape=[jax.ShapeDtypeStruct((T, 4 * c), BF16)] * 3,
                  compiler_params=_params(("parallel", "parallel")))(h, wg, wu)


def gate_up_dx(dg, du, wg, wu, name):
    T = dg.shape[0]
    _, _, D, c = wg.shape
    tm, tn = _tile(T, (1024, 512, 256, 128)), _tile(D, (1024, 512, 256, 128))

    def body(g_ref, u_ref, gw_ref, uw_ref, o_ref, acc):
        k = pl.program_id(2)
        d = (lax.dot_general(g_ref[...], gw_ref[...], NT, preferred_element_type=F32)
             + lax.dot_general(u_ref[...], uw_ref[...], NT, preferred_element_type=F32))

        @pl.when(k == 0)
        def _():
            acc[...] = d

        @pl.when((k > 0) & (k < 3))
        def _():
            acc[...] += d

        @pl.when(k == 3)
        def _():
            o_ref[...] = acc[...] + d

    a_spec = pl.BlockSpec((tm, c), lambda i, j, k: (i, k))
    w_spec = pl.BlockSpec((None, None, tn, c), lambda i, j, k: (0, k, j, 0))
    return _pcall(body, name=name, grid=(T // tm, D // tn, 4), in_specs=[a_spec, a_spec, w_spec, w_spec],
                  out_specs=pl.BlockSpec((tm, tn), lambda i, j, k: (i, j)), out_shape=jax.ShapeDtypeStruct((T, D), F32),
                  scratch_shapes=[pltpu.VMEM((tm, tn), F32)],
                  compiler_params=_params(("parallel", "parallel", "arbitrary")))(dg, du, wg, wu)


def down_dx_swiglu(dy, w_down, gate, up, name, dep=None):
    T, D = dy.shape
    Fh = w_down.shape[0]
    tm, tn = _tile(T, (1024, 512, 256, 128)), 512

    def body(dy_ref, w_ref, g_ref, u_ref, *rest):
        dg_ref, du_ref = rest[-2:]
        d = lax.dot_general(dy_ref[...], w_ref[...], NT, preferred_element_type=F32)
        g = g_ref[...].astype(F32)
        u = u_ref[...].astype(F32)
        s = _sigmoid(g)
        dg_ref[...] = (d * u * (s * (1.0 + g * (1.0 - s)))).astype(BF16)
        du_ref[...] = (d * g * s).astype(BF16)

    ts = pl.BlockSpec((tm, tn), lambda i, j: (i, j))
    in_specs = [pl.BlockSpec((tm, D), lambda i, j: (i, 0)), pl.BlockSpec((tn, D), lambda i, j: (j, 0)), ts, ts]
    args = (dy, w_down, gate, up)
    if dep is not None:
        in_specs, args = in_specs + [pl.BlockSpec(memory_space=pl.ANY)], args + (dep,)
    return _pcall(body, name=name, grid=(T // tm, Fh // tn), in_specs=in_specs, out_specs=[ts, ts],
                  out_shape=[jax.ShapeDtypeStruct((T, Fh), BF16)] * 2,
                  compiler_params=_params(("parallel", "parallel")))(*args)


def swiglu_fwd(g, u, name):
    T = g.shape[0]
    tq, tc = _tile(T, (512, 256, 128)), 512
    nf = FFN // tc

    def body(g_ref, u_ref, o_ref):
        gv = g_ref[...].astype(F32)
        o_ref[...] = (gv * _sigmoid(gv) * u_ref[...].astype(F32)).astype(BF16)

    lo = pl.BlockSpec((tq, tc), lambda i, j: (i, j))
    return _pcall(body, name=name, grid=(T // tq, nf), in_specs=[lo, lo], out_specs=lo,
                  out_shape=jax.ShapeDtypeStruct((T, FFN), BF16), compiler_params=_params(("parallel", "parallel")))(g, u)


def swiglu_bwd(gate, up, dact, name):
    T = gate.shape[0]
    tq, tc = _tile(T, (512, 256, 128)), 512
    nf = FFN // tc

    def body(g_ref, u_ref, d_ref, dg_ref, du_ref):
        g = g_ref[...].astype(F32)
        u = u_ref[...].astype(F32)
        d = d_ref[...]
        s = _sigmoid(g)
        dg_ref[...] = (d * u * (s * (1.0 + g * (1.0 - s)))).astype(BF16)
        du_ref[...] = (d * g * s).astype(BF16)

    lo = pl.BlockSpec((tq, tc), lambda i, j: (i, j))
    return _pcall(body, name=name, grid=(T // tq, nf), in_specs=[lo, lo, lo], out_specs=[lo, lo],
                  out_shape=[jax.ShapeDtypeStruct((T, FFN), BF16)] * 2,
                  compiler_params=_params(("parallel", "parallel")))(gate, up, dact)


def loss_head(y, target):
    T, D = y.shape
    tq = _tile(T, (256, 128))

    def body(y_ref, t_ref, dy_ref, l_ref, dyb_ref):
        @pl.when(pl.program_id(0) == 0)
        def _():
            l_ref[...] = jnp.zeros_like(l_ref)

        e = y_ref[...] - t_ref[...]
        dy = e * (1.0 / D)
        dy_ref[...] = dy
        dyb_ref[...] = dy.astype(BF16)
        l_ref[...] += 0.5 * jnp.sum(jnp.mean(e * e, axis=-1, keepdims=True))

    xs = pl.BlockSpec((tq, D), lambda i: (i, 0))
    return _pcall(body, name="loss_head", grid=(T // tq,), in_specs=[xs, xs],
                  out_specs=[xs, pl.BlockSpec((8, LANES), lambda i: (0, 0)), xs],
                  out_shape=[jax.ShapeDtypeStruct((T, D), F32), jax.ShapeDtypeStruct((8, LANES), F32),
                             jax.ShapeDtypeStruct((T, D), BF16)],
                  compiler_params=_params(("arbitrary",)))(y, target)


def sum_arrays(arrs, out_dtype, name):
    R, C = arrs[0].shape
    tr = _tile(R, (512, 256, 128, 64, 32, 16, 8))
    n = len(arrs)

    def body(*refs):
        acc = refs[0][...].astype(F32)
        for r in refs[1:n]:
            acc = acc + r[...].astype(F32)
        refs[n][...] = acc.astype(out_dtype)

    xs = pl.BlockSpec((tr, C), lambda i: (i, 0))
    return _pcall(body, name=name, grid=(R // tr,), in_specs=[xs] * n, out_specs=xs,
                  out_shape=jax.ShapeDtypeStruct((R, C), out_dtype), compiler_params=_params(("parallel",)))(*arrs)


def adamw(w, g, m, v, name):
    R, C = w.shape
    tr = _tile(R, (256, 128, 64, 32, 16, 8))
    c1 = 1.0 - ADAM_B1 ** ADAM_STEP
    c2 = 1.0 - ADAM_B2 ** ADAM_STEP

    def body(w_ref, g_ref, m_ref, v_ref, d_ref, nm_ref, nv_ref):
        gv = g_ref[...]
        mn = ADAM_B1 * m_ref[...] + (1.0 - ADAM_B1) * gv
        vn = ADAM_B2 * v_ref[...] + (1.0 - ADAM_B2) * (gv * gv)
        nm_ref[...] = mn
        nv_ref[...] = vn
        d_ref[...] = -ADAM_LR * ((mn / c1) / (jnp.sqrt(vn / c2) + ADAM_EPS) + ADAM_WD * w_ref[...])

    xs = pl.BlockSpec((tr, C), lambda i: (i, 0))
    return _pcall(body, name=name, grid=(R // tr,), in_specs=[xs] * 4, out_specs=[xs] * 3,
                  out_shape=[jax.ShapeDtypeStruct((R, C), F32)] * 3, compiler_params=_params(("parallel",)))(w, g, m, v)


def adamw_layer(w, g, m, v, l, prev, name):
    L, r, c = w.shape
    tr = _tile(r, (256, 128, 64, 32, 16, 8))
    c1 = 1.0 - ADAM_B1 ** ADAM_STEP
    c2 = 1.0 - ADAM_B2 ** ADAM_STEP
    chained = prev is not None

    def body(w_ref, g_ref, m_ref, v_ref, *rest):
        go_ref, d_ref, nm_ref, nv_ref = rest[-4:]
        gv = g_ref[...]
        mn = ADAM_B1 * m_ref[...] + (1.0 - ADAM_B1) * gv
        vn = ADAM_B2 * v_ref[...] + (1.0 - ADAM_B2) * (gv * gv)
        go_ref[...] = gv
        nm_ref[...] = mn
        nv_ref[...] = vn
        d_ref[...] = -ADAM_LR * ((mn / c1) / (jnp.sqrt(vn / c2) + ADAM_EPS) + ADAM_WD * w_ref[...])

    ls = pl.BlockSpec((None, tr, c), lambda i: (l, i, 0))
    gs = pl.BlockSpec((tr, c), lambda i: (i, 0))
    anys = [pl.BlockSpec(memory_space=pl.ANY)] * 4 if chained else []
    return _pcall(body, name=name, grid=(r // tr,), in_specs=[ls, gs, ls, ls] + anys, out_specs=[ls] * 4,
                  out_shape=[jax.ShapeDtypeStruct((L, r, c), F32)] * 4,
                  input_output_aliases={4: 0, 5: 1, 6: 2, 7: 3} if chained else {},
                  compiler_params=_params(("parallel",)))(w, g, m, v, *(prev if chained else ()))


def add_pair(g, got, name):
    _, r, c = g.shape
    h = r // 2

    def body(c_ref, g_ref, o_ref, out_ref):
        del c_ref
        out_ref[...] = (g_ref[...].astype(F32) + o_ref[...].astype(F32)).astype(BF16)

    spec = pltpu.PrefetchScalarGridSpec(
        num_scalar_prefetch=1, grid=(4,),
        in_specs=[pl.BlockSpec((None, h, c), lambda k, cs: (k, cs[0], 0)), pl.BlockSpec((None, h, c), lambda k, cs: (k, 0, 0))],
        out_specs=pl.BlockSpec((None, h, c), lambda k, cs: (k, 0, 0)))
    return _pcall(body, name=name, grid_spec=spec, out_shape=jax.ShapeDtypeStruct((4, h, c), BF16),
                  compiler_params=_params(("arbitrary",)))(lax.axis_index("c").reshape(1).astype(jnp.int32), g, got)


def add_chips(p, b, name):
    _, h, c = p.shape
    tr = _tile(h, (256, 128, 64, 32, 16, 8))

    def body(me_ref, p_ref, b0, b1, b2, b3, out_ref):
        me = me_ref[0]
        acc = jnp.zeros(out_ref.shape, F32)
        for k, bk in enumerate((b0, b1, b2, b3)):
            acc = acc + jnp.where(me == k, p_ref[...], bk[...]).astype(F32)
        out_ref[...] = acc

    slot = lambda k: pl.BlockSpec((None, tr, c), lambda i, ms: (jnp.where(ms[0] == k, (k + 1) % 4, k), i, 0))
    spec = pltpu.PrefetchScalarGridSpec(
        num_scalar_prefetch=1, grid=(h // tr,),
        in_specs=[pl.BlockSpec((None, tr, c), lambda i, ms: (ms[0], i, 0)), slot(0), slot(1), slot(2), slot(3)],
        out_specs=pl.BlockSpec((tr, c), lambda i, ms: (i, 0)))
    me = (2 * lax.axis_index("x") + lax.axis_index("y")).reshape(1).astype(jnp.int32)
    return _pcall(body, name=name, grid_spec=spec, out_shape=jax.ShapeDtypeStruct((h, c), F32),
                  compiler_params=_params(("arbitrary",)))(me, p, b, b, b, b)


ANY = pl.BlockSpec(memory_space=pl.ANY)


def _place():
    x, y, c = lax.axis_index("x"), lax.axis_index("y"), lax.axis_index("c")
    chips = [(1 - x, y), (x, 1 - y), (1 - x, 1 - y)]
    return x, y, c, chips


def _rcopy(src, dst, ssem, rsem, dev):
    return pltpu.make_async_remote_copy(src_ref=src, dst_ref=dst, send_sem=ssem, recv_sem=rsem, device_id=dev,
                                        device_id_type=MESH)


def _half(c, h):
    return pl.ds(pl.multiple_of(c * h, 64), h)


def _me_chip():
    return 2 * lax.axis_index("x") + lax.axis_index("y")


def all_gather_weights(shards, name):
    n = len(shards)
    L = shards[0].shape[0]
    hl = L // 2

    def body(*refs):
        srcs, outs, (ssem, rsem) = refs[:n], refs[n:2 * n], refs[2 * n:]
        x, y, c, chips = _place()
        me = 2 * x + y
        sib = (x, y, 1 - c)
        mine, other = pl.ds(c * hl, hl), pl.ds((1 - c) * hl, hl)
        sends = []
        for w in range(n):
            for j, (px, py) in enumerate(chips):
                cp = _rcopy(srcs[w].at[mine], outs[w].at[mine, me], ssem.at[w, j], rsem.at[w, j], (px, py, c))
                cp.start()
                sends.append(cp)
        for j, (px, py) in enumerate(chips):
            slot = 2 * px + py
            for w in range(n):
                _rcopy(srcs[w].at[mine], outs[w].at[mine, slot], ssem.at[w, j], rsem.at[w, j], (px, py, c)).wait_recv()
                cp = _rcopy(outs[w].at[mine, slot], outs[w].at[mine, slot], ssem.at[w, 3 + j], rsem.at[w, 3 + j], sib)
                cp.start()
                sends.append(cp)
        for j, (px, py) in enumerate(chips):
            slot = 2 * px + py
            for w in range(n):
                _rcopy(outs[w].at[other, slot], outs[w].at[other, slot], ssem.at[w, 3 + j], rsem.at[w, 3 + j], sib).wait_recv()
        for cp in sends:
            cp.wait_send()

    outs = _pcall(body, name=name, in_specs=[ANY] * n, out_specs=[ANY] * n,
                  out_shape=[jax.ShapeDtypeStruct((L, 4) + s.shape[1:], s.dtype) for s in shards],
                  scratch_shapes=[pltpu.SemaphoreType.DMA((n, 6)), pltpu.SemaphoreType.DMA((n, 6))])(*shards)
    me = _me_chip()
    return [lax.dynamic_update_slice(o, s[:, None], (0, me, 0, 0)) for o, s in zip(outs, shards)]


HBM = pl.BlockSpec(memory_space=pltpu.HBM)
SEM = pl.BlockSpec(memory_space=pltpu.SEMAPHORE)
EFFECT = pltpu.SideEffectType.DATAFLOW_SIDE_EFFECTING


def _ici_copy(kind, src, land, x, y, c, px, py, ssem, rsem, receiving):
    me, peer = 2 * x + y, 2 * px + py
    if kind == "gather":
        rows = _half(c, land.shape[1] // 2)
        s, d = land.at[me, rows], land.at[peer if receiving else me, rows]
    else:
        s, d = src.at[me if receiving else peer], land.at[peer if receiving else me]
    return _rcopy(s, d, ssem, rsem, (px, py, c))


def ici_start(kind, srcs, lands, name):
    n, ns = len(lands), len(srcs)
    m = 3 * n
    na = ns + n

    def body(*refs):
        src, land, token = (refs[:ns] if ns else [None] * n), refs[ns:na], refs[-1]
        ssem, rsem = refs[na:na + m], refs[na + m:na + 2 * m]
        x, y, c, chips = _place()
        for w in range(n):
            for j, (px, py) in enumerate(chips):
                _ici_copy(kind, src[w], land[w], x, y, c, px, py, ssem[3 * w + j], rsem[3 * w + j], False).start()
        token[...] = jnp.zeros_like(token)

    arrs = [pltpu.with_memory_space_constraint(a, pltpu.HBM) for a in list(srcs) + list(lands)]
    outs = _pcall(
        body, name=name,
        out_shape=(*[pltpu.SemaphoreType.DMA(())] * (2 * m), *[pltpu.HBM(a.shape, a.dtype) for a in arrs],
                   jax.ShapeDtypeStruct((8, LANES), F32)),
        in_specs=[HBM] * na,
        out_specs=(*([SEM] * (2 * m)), *([HBM] * na), pl.BlockSpec(memory_space=pltpu.VMEM)),
        input_output_aliases={i: 2 * m + i for i in range(na)},
        compiler_params=pltpu.CompilerParams(has_side_effects=EFFECT))(*arrs)
    return dict(kind=kind, ssem=list(outs[:m]), rsem=list(outs[m:2 * m]), srcs=list(outs[2 * m:2 * m + ns]),
                lands=list(outs[2 * m + ns:2 * m + na])), outs[-1]


def ici_wait(state, after, name):
    kind, n, ns = state["kind"], len(state["lands"]), len(state["srcs"])
    m = 3 * n
    na = ns + n

    def body(*refs):
        src, land = (refs[:ns] if ns else [None] * n), refs[ns:na]
        ssem, rsem = refs[na:na + m], refs[na + m:na + 2 * m]
        x, y, c, chips = _place()
        for w in range(n):
            for j, (px, py) in enumerate(chips):
                _ici_copy(kind, src[w], land[w], x, y, c, px, py, ssem[3 * w + j], rsem[3 * w + j], False).wait_send()
                _ici_copy(kind, src[w], land[w], x, y, c, px, py, ssem[3 * w + j], rsem[3 * w + j], True).wait_recv()

    outs = _pcall(
        body, name=name,
        out_shape=tuple(pltpu.HBM(a.shape, a.dtype) for a in state["srcs"] + state["lands"]),
        in_specs=[HBM] * na + [SEM] * (2 * m) + [ANY], out_specs=tuple([HBM] * na),
        input_output_aliases={i: i for i in range(na)},
        compiler_params=pltpu.CompilerParams(has_side_effects=EFFECT))(
            *state["srcs"], *state["lands"], *state["ssem"], *state["rsem"], after)
    return list(outs[:ns]), list(outs[ns:])


def forward_pair(lands, name):
    n = len(lands)

    def body(*refs):
        land, (ssem, rsem) = refs[:n], refs[2 * n:]
        x, y, c, chips = _place()
        sib = (x, y, 1 - c)
        cps = []
        for w in range(n):
            h = land[w].shape[1] // 2
            for j, (px, py) in enumerate(chips):
                part = land[w].at[2 * px + py, _half(c, h)]
                cp = _rcopy(part, part, ssem.at[w, j], rsem.at[w, j], sib)
                cp.start()
                cps.append(cp)
        for w in range(n):
            h = land[w].shape[1] // 2
            for j, (px, py) in enumerate(chips):
                part = land[w].at[2 * px + py, _half(1 - c, h)]
                _rcopy(part, part, ssem.at[w, j], rsem.at[w, j], sib).wait_recv()
        for cp in cps:
            cp.wait_send()

    return _pcall(body, name=name, in_specs=[ANY] * n, out_specs=[ANY] * n,
                  out_shape=[jax.ShapeDtypeStruct(z.shape, z.dtype) for z in lands],
                  input_output_aliases={i: i for i in range(n)},
                  scratch_shapes=[pltpu.SemaphoreType.DMA((n, 3)), pltpu.SemaphoreType.DMA((n, 3))])(*lands)


def exchange_pair(gs, name):
    n = len(gs)

    def body(*refs):
        srcs, outs, (ssem, rsem) = refs[:n], refs[n:2 * n], refs[2 * n:]
        x, y, c, _ = _place()
        cps = []
        for w in range(n):
            h = srcs[w].shape[1] // 2
            cp = _rcopy(srcs[w].at[:, _half(1 - c, h)], outs[w], ssem.at[w], rsem.at[w], (x, y, 1 - c))
            cp.start()
            cps.append(cp)
        for cp in cps:
            cp.wait()

    return _pcall(body, name=name, in_specs=[ANY] * n, out_specs=[ANY] * n,
                  out_shape=[jax.ShapeDtypeStruct((4, g.shape[1] // 2, g.shape[2]), g.dtype) for g in gs],
                  scratch_shapes=[pltpu.SemaphoreType.DMA((n,)), pltpu.SemaphoreType.DMA((n,))])(*gs)


def scatter_chips(ps, name):
    n = len(ps)

    def body(*refs):
        srcs, outs, (ssem, rsem) = refs[:n], refs[n:2 * n], refs[2 * n:]
        x, y, c, chips = _place()
        me = 2 * x + y
        sends = []
        for w in range(n):
            for j, (px, py) in enumerate(chips):
                cp = _rcopy(srcs[w].at[2 * px + py], outs[w].at[me], ssem.at[w, j], rsem.at[w, j], (px, py, c))
                cp.start()
                sends.append(cp)
        for w in range(n):
            for j, (px, py) in enumerate(chips):
                _rcopy(srcs[w].at[me], outs[w].at[2 * px + py], ssem.at[w, j], rsem.at[w, j], (px, py, c)).wait_recv()
        for cp in sends:
            cp.wait_send()

    return _pcall(body, name=name, in_specs=[ANY] * n, out_specs=[ANY] * n,
                  out_shape=[jax.ShapeDtypeStruct(p.shape, p.dtype) for p in ps],
                  scratch_shapes=[pltpu.SemaphoreType.DMA((n, 3)), pltpu.SemaphoreType.DMA((n, 3))])(*ps)


def join_pair(rs, name):
    n = len(rs)

    def body(*refs):
        srcs, outs, (ssem, rsem) = refs[:n], refs[n:2 * n], refs[2 * n:]
        x, y, c, _ = _place()
        cps = []
        for w in range(n):
            h = srcs[w].shape[0]
            cp = _rcopy(srcs[w], outs[w].at[_half(c, h)], ssem.at[w], rsem.at[w], (x, y, 1 - c))
            cp.start()
            cps.append(cp)
        for w in range(n):
            h = srcs[w].shape[0]
            _rcopy(srcs[w], outs[w].at[_half(1 - c, h)], ssem.at[w], rsem.at[w], (x, y, 1 - c)).wait_recv()
        for cp in cps:
            cp.wait_send()

    outs = _pcall(body, name=name, in_specs=[ANY] * n, out_specs=[ANY] * n,
                  out_shape=[jax.ShapeDtypeStruct((2 * r.shape[0], r.shape[1]), r.dtype) for r in rs],
                  scratch_shapes=[pltpu.SemaphoreType.DMA((n,)), pltpu.SemaphoreType.DMA((n,))])(*rs)
    c = lax.axis_index("c")
    return [lax.dynamic_update_slice(o, r, (c * r.shape[0], 0)) for o, r in zip(outs, rs)]


def gather_all_devices(v, name):
    R, C = v.shape

    def body(src, out, ssem, rsem, lsem):
        x, y, c, _ = _place()
        me = 4 * x + 2 * y + c
        local = pltpu.make_async_copy(src, out.at[me], lsem)
        local.start()
        sends = []
        for j in range(1, 8):
            dx, dy, dc = (j >> 2) & 1, (j >> 1) & 1, j & 1
            px, py, pc = (1 - x if dx else x), (1 - y if dy else y), (1 - c if dc else c)
            cp = _rcopy(src, out.at[me], ssem.at[j - 1], rsem.at[j - 1], (px, py, pc))
            cp.start()
            sends.append(cp)
        for j in range(1, 8):
            dx, dy, dc = (j >> 2) & 1, (j >> 1) & 1, j & 1
            px, py, pc = (1 - x if dx else x), (1 - y if dy else y), (1 - c if dc else c)
            _rcopy(src, out.at[4 * px + 2 * py + pc], ssem.at[j - 1], rsem.at[j - 1], (px, py, pc)).wait_recv()
        for cp in sends:
            cp.wait_send()
        local.wait()

    return _pcall(body, name=name, in_specs=[ANY], out_specs=ANY, out_shape=jax.ShapeDtypeStruct((8, R, C), v.dtype),
                  scratch_shapes=[pltpu.SemaphoreType.DMA((7,)), pltpu.SemaphoreType.DMA((7,)), pltpu.SemaphoreType.DMA(())])(v)


def reduce_scatter_begin(gw, tag):
    names = list(gw)
    gs = [gw[k] for k in names]
    got = exchange_pair(gs, "rs_pair_" + tag)
    ps = [add_pair(g, o, "rs_add_pair_%s_%s" % (k, tag)) for k, g, o in zip(names, gs, got)]
    state, token = ici_start("scatter", ps, [lax.empty(p.shape, p.dtype) for p in ps], "rs_chips_start_" + tag)
    state["names"] = names
    return state, token


def reduce_scatter_end(state, after, tag):
    names = state["names"]
    ps, bs = ici_wait(state, after, "rs_chips_wait_" + tag)
    rs = [add_chips(p, b, "rs_add_chips_%s_%s" % (k, tag)) for k, p, b in zip(names, ps, bs)]
    return dict(zip(names, join_pair(rs, "rs_join_" + tag)))


GATHERED = ("w_in", "w_q_b", "w_kv_b", "w_out", "w_xq", "w_xk", "w_xv", "w_xo", "w_gate", "w_up", "w_down")
KIND = {"w_q_b": "col", "w_kv_b": "col", "w_out": "row", "w_xq": "row", "w_xk": "row", "w_xv": "row", "w_xo": "col",
        "w_gate": "col", "w_up": "col", "w_down": "row"}


def wq_to_storage(w):
    lead = w.shape[:-1]
    w = w.reshape(lead + (2, MLA_QK))
    rope = jnp.pad(w[..., MLA_NOPE:], [(0, 0)] * (w.ndim - 1) + [(0, LANES - MLA_ROPE)])
    return jnp.concatenate([w[..., :MLA_NOPE].reshape(lead + (2 * LANES,)), rope.reshape(lead + (2 * LANES,))], axis=-1)


def wq_from_storage(g):
    lead = g.shape[:-1]
    nope = g[..., :2 * LANES].reshape(lead + (2, LANES))
    rope = g[..., 2 * LANES:].reshape(lead + (2, LANES))[..., :MLA_ROPE]
    return jnp.concatenate([nope, rope], axis=-1).reshape(lead + (2 * MLA_QK,))


def wcat_from_storage(w_in_l):
    w = w_in_l.transpose(1, 0, 2).reshape(D_MODEL, IN_COLS)
    zc = lambda n: jnp.zeros((D_MODEL, n), w.dtype)
    return jnp.concatenate([w[:, :2560], w[:, 2576:], zc(64), w[:, 2560:2576], zc(112)], axis=1)


def w_in_grad_storage(dwcat):
    g = jnp.concatenate([dwcat[:, :2560], dwcat[:, C_DT:C_DT + 16], dwcat[:, 2560:3648]], axis=1)
    return g.reshape(D_MODEL, 4, IN_COLS // 4).transpose(1, 0, 2)


def _ssd_par(sm):
    rows = jnp.stack([sm["dt_bias"], sm["a_log"], sm["d_skip"]]).reshape(3, SSD_GROUPS, SSD_HPG).transpose(1, 0, 2)
    return jnp.pad(rows, ((0, 0), (0, 5), (0, LANES - SSD_HPG)))


def _dt_pad(proj, B, S):
    dt = proj[:, C_DT:C_DT + 16].reshape(B, S, SSD_GROUPS, SSD_HPG)
    return jnp.pad(dt, ((0, 0), (0, 0), (0, 0), (0, LANES - SSD_HPG))).reshape(B, S, SSD_GROUPS * LANES)


def _mla_gains(g):
    return g[:MLA_NOPE].reshape(1, LANES), jnp.pad(g[MLA_NOPE:], (0, LANES - MLA_ROPE)).reshape(1, LANES)


_Q_NOPE = lambda h: 4 * (h // 2) + h % 2
_Q_ROPE = lambda h: 4 * (h // 2) + 2 + h % 2
_K_NOPE = lambda h: 2 * h
_V_BLK = lambda h: 2 * h + 1


def layer_fwd(x, memf, cos, sin, G, kw, sm, B, S, l, dep=None):
    T = x.shape[0]
    n = lambda s: "%s_l%d" % (s, l)
    wv = lambda k: (KIND[k], 0)
    sv = dict(x0=x)
    h1 = norm_fwd(x, sm["attn_norm_g"], D_MODEL, False, n("attn_norm"), dep=dep)
    proj = mm(h1, kw["wcat"], "nn", n("in_proj"))
    mark = getattr(G, "mark", lambda v: None)
    mark(proj)
    sv.update(h1=h1, proj=proj)
    proj3 = proj.reshape(B, S, CAT_COLS)
    xbc = conv_fwd(proj3, C_XBC // 256, kw["conv_w"], sm["conv_b"].reshape(1, -1), n("conv"))
    dtp = _dt_pad(proj, B, S)
    par = _ssd_par(sm)
    y_raw, states = ssd_fwd(xbc, dtp, par, n("ssd"))
    z = proj[:, C_Z:C_Z + SSD_INNER]
    y_ssd = norm_fwd(y_raw.reshape(T, SSD_INNER), sm["ssd_norm_g"], SSD_INNER // SSD_GROUPS, False, n("ssd_norm"), gate=z)
    sv.update(xbc=xbc, states=states, y_raw=y_raw)
    qn = norm_fwd(proj[:, C_QA:C_QA + Q_LORA], sm["q_a_norm_g"], Q_LORA, False, n("q_a_norm"))
    kvn = norm_fwd(proj[:, C_KVA:C_KVA + KV_LORA], sm["kv_a_norm_g"], KV_LORA, False, n("kv_a_norm"))
    q_raw = mm(qn, G["w_q_b"], "nn", n("q_b"), wv=wv("w_q_b"))
    kv_raw = mm(kvn, G["w_kv_b"], "nn", n("kv_b"), wv=wv("w_kv_b"))
    gqn, gqr = _mla_gains(sm["mla_q_norm_g"])
    gkn, gkr = _mla_gains(sm["mla_k_norm_g"])
    qh = qk_fwd(q_raw, _Q_NOPE, q_raw, _Q_ROPE, False, gqn, gqr, cos, sin, n("q_norm_rope"))
    kh = qk_fwd(kv_raw, _K_NOPE, proj, C_KR // LANES, True, gkn, gkr, cos, sin, n("k_norm_rope"))
    vv = kv_raw.astype(BF16)
    y_mla = attn_fwd(qh.reshape(B, S, -1), kh.reshape(B, S, -1), vv.reshape(B, S, -1), MLA_HEADS, 2 * LANES, MLA_V,
                     MLA_QK ** -0.5, True, n("mla_attn"), v_idx=_V_BLK)
    ycat = jnp.concatenate([y_ssd, y_mla.reshape(T, -1)], axis=1)
    x1 = mm(ycat, G["w_out"], "nn", n("out_proj"), add=x, wv=wv("w_out"))
    sv.update(qn=qn, kvn=kvn, q_raw=q_raw, kv_raw=kv_raw, qh=qh, kh=kh, vv=vv, ycat=ycat, x1=x1)
    h2 = norm_fwd(x1, sm["xattn_norm_g"], D_MODEL, False, n("xattn_norm"))
    mn = norm_fwd(memf, sm["mem_norm_g"], D_MODEL, False, n("mem_norm"))
    xq = mm(h2, G["w_xq"], "nn", n("xq"), wv=wv("w_xq"))
    xk = mm(mn, G["w_xk"], "nn", n("xk"), wv=wv("w_xk"))
    xv = mm(mn, G["w_xv"], "nn", n("xv"), wv=wv("w_xv"), out_dtype=BF16)
    xqn = norm_fwd(xq, sm["xq_norm_g"], X_HEAD_DIM, True, n("xq_norm"))
    xkn = norm_fwd(xk, sm["xk_norm_g"], X_HEAD_DIM, True, n("xk_norm"))
    ML = memf.shape[0] // B
    xo = attn_fwd(xqn.reshape(B, S, -1), xkn.reshape(B, ML, -1), xv.reshape(B, ML, -1), X_HEADS, X_HEAD_DIM, X_HEAD_DIM,
                  X_HEAD_DIM ** -0.5, False, n("x_attn"))
    x2 = mm(xo.reshape(T, -1), G["w_xo"], "nn", n("xo_proj"), add=x1, wv=wv("w_xo"))
    mark(x2)
    sv.update(h2=h2, mn=mn, xq=xq, xk=xk, xqn=xqn, xkn=xkn, xv=xv, xo=xo, x2=x2)
    h3 = norm_fwd(x2, sm["ffn_norm_g"], D_MODEL, False, n("ffn_norm"))
    gate, up, act = gate_up_act(h3, G["w_gate"], G["w_up"], n("gate_up"))
    x3 = mm(act, G["w_down"], "nn", n("down"), add=x2, wv=wv("w_down"))
    sv.update(h3=h3, gate=gate, up=up, act=act)
    return x3, sv


def layer_bwd(dx3, dx3b, sv, memf, cos, sin, G, kw, sm, B, S, l, dep=None, rs_hook=None):
    T = dx3.shape[0]
    n = lambda s: "%s_bwd_l%d" % (s, l)
    wv = lambda k: (KIND[k], 0)
    dw = lambda k, a, b: mm(a, b, "tn", n(k + "_dw"), out_dtype=BF16, out_wv=KIND[k])
    gw, gs = {}, {}
    dgate, dup = down_dx_swiglu(dx3b, G["w_down"][0].reshape(FFN, D_MODEL), sv["gate"], sv["up"], n("down_dx"), dep=dep)
    gw["w_down"] = dw("w_down", sv["act"], dx3b)
    dh3 = gate_up_dx(dgate, dup, G["w_gate"], G["w_up"], n("gate_up_dx"))
    gw["w_gate"] = dw("w_gate", sv["h3"], dgate)
    gw["w_up"] = dw("w_up", sv["h3"], dup)
    dx2, gs["ffn_norm_g"], dx2b = norm_bwd(sv["x2"], sm["ffn_norm_g"], dh3, D_MODEL, False, n("ffn_norm"), add=dx3,
                                           bf16_copy=True)
    ffn_token = rs_hook({k: gw.pop(k) for k in ("w_down", "w_gate", "w_up")}) if rs_hook is not None else None
    dxo = mm(dx2b, G["w_xo"], "nt", n("xo_dx"), wv=wv("w_xo"), dep=ffn_token)
    gw["w_xo"] = dw("w_xo", sv["xo"].reshape(T, -1), dx2b)
    ML = memf.shape[0] // B
    dxqn, dxkn, dxv = attn_bwd(sv["xqn"].reshape(B, S, -1), sv["xkn"].reshape(B, ML, -1), sv["xv"].reshape(B, ML, -1),
                               dxo.reshape(B, S, -1), X_HEADS, X_HEAD_DIM, X_HEAD_DIM, X_HEAD_DIM ** -0.5, False, n("x_attn"))
    dxq, gs["xq_norm_g"] = norm_bwd(sv["xq"], sm["xq_norm_g"], dxqn.reshape(T, -1), X_HEAD_DIM, True, n("xq_norm"))
    dxk, gs["xk_norm_g"] = norm_bwd(sv["xk"], sm["xk_norm_g"], dxkn.reshape(B * ML, -1), X_HEAD_DIM, True, n("xk_norm"))
    dxv = dxv.reshape(B * ML, -1)
    dmn = mm(dxk, G["w_xk"], "nt", n("xk_dx"), wv=wv("w_xk"))
    dmn = mm(dxv, G["w_xv"], "nt", n("xv_dx"), wv=wv("w_xv"), add=dmn)
    gw["w_xk"] = dw("w_xk", sv["mn"], dxk)
    gw["w_xv"] = dw("w_xv", sv["mn"], dxv)
    _, gs["mem_norm_g"] = norm_bwd(memf, sm["mem_norm_g"], dmn, D_MODEL, False, n("mem_norm"))
    dh2 = mm(dxq, G["w_xq"], "nt", n("xq_dx"), wv=wv("w_xq"))
    gw["w_xq"] = dw("w_xq", sv["h2"], dxq)
    dx1, gs["xattn_norm_g"], dx1b = norm_bwd(sv["x1"], sm["xattn_norm_g"], dh2, D_MODEL, False, n("xattn_norm"), add=dx2,
                                             bf16_copy=True)
    dycat = mm(dx1b, G["w_out"], "nt", n("out_dx"), wv=wv("w_out"))
    gw["w_out"] = dw("w_out", sv["ycat"], dx1b)
    dqh, dkh, dvv = attn_bwd(sv["qh"].reshape(B, S, -1), sv["kh"].reshape(B, S, -1), sv["vv"].reshape(B, S, -1),
                             dycat[:, SSD_INNER:].reshape(B, S, -1), MLA_HEADS, 2 * LANES, MLA_V, MLA_QK ** -0.5, True,
                             n("mla_attn"), v_idx=_V_BLK)
    gqn, gqr = _mla_gains(sm["mla_q_norm_g"])
    gkn, gkr = _mla_gains(sm["mla_k_norm_g"])
    dq_raw, dgqn, dgqr = qk_bwd(sv["q_raw"], _Q_NOPE, sv["q_raw"], _Q_ROPE, False, gqn, gqr, cos, sin,
                                dqh.reshape(T, -1), n("q_norm_rope"))
    dkv_raw, dk_rope, dgkn, dgkr = qk_bwd(sv["kv_raw"], _K_NOPE, sv["proj"], C_KR // LANES, True, gkn, gkr, cos, sin,
                                          dkh.reshape(T, -1), n("k_norm_rope"), fill=dvv.reshape(T, -1), fill_blk=_V_BLK)
    gs["mla_q_norm_g"] = jnp.concatenate([dgqn[0], dgqr[0, :MLA_ROPE]])
    gs["mla_k_norm_g"] = jnp.concatenate([dgkn[0], dgkr[0, :MLA_ROPE]])
    dqn = mm(dq_raw, G["w_q_b"], "nt", n("q_b_dx"), wv=wv("w_q_b"))
    gw["w_q_b"] = dw("w_q_b", sv["qn"], dq_raw)
    dkvn = mm(dkv_raw, G["w_kv_b"], "nt", n("kv_b_dx"), wv=wv("w_kv_b"))
    gw["w_kv_b"] = dw("w_kv_b", sv["kvn"], dkv_raw)
    proj = sv["proj"]
    dq_a, gs["q_a_norm_g"] = norm_bwd(proj[:, C_QA:C_QA + Q_LORA], sm["q_a_norm_g"], dqn, Q_LORA, False, n("q_a_norm"))
    dkv_a, gs["kv_a_norm_g"] = norm_bwd(proj[:, C_KVA:C_KVA + KV_LORA], sm["kv_a_norm_g"], dkvn, KV_LORA, False,
                                        n("kv_a_norm"))
    z = proj[:, C_Z:C_Z + SSD_INNER]
    dy_raw, dz, gs["ssd_norm_g"] = norm_bwd(sv["y_raw"].reshape(T, -1), sm["ssd_norm_g"], dycat[:, :SSD_INNER],
                                            SSD_INNER // SSD_GROUPS, False, n("ssd_norm"), gate=z)
    dtp = _dt_pad(proj, B, S)
    par = _ssd_par(sm)
    dxs, dBm, dCm, ddt, dpar = ssd_bwd(sv["xbc"], dtp, par, sv["states"], dy_raw.reshape(B, S, -1), n("ssd"))
    gs["dt_bias"] = dpar[:, 0, :SSD_HPG].reshape(-1)
    gs["a_log"] = dpar[:, 1, :SSD_HPG].reshape(-1)
    gs["d_skip"] = dpar[:, 2, :SSD_HPG].reshape(-1)
    dxbc_act = jnp.concatenate([dxs, dBm, dCm], axis=2)
    dxbc, dconv_w, dconv_b = conv_bwd(proj.reshape(B, S, -1), C_XBC // 256, kw["conv_w"], sm["conv_b"].reshape(1, -1), dxbc_act,
                                      n("conv"))
    gs["conv_w"] = dconv_w[:SSD_CONV]
    gs["conv_b"] = dconv_b[0]
    ddt16 = ddt.reshape(T, SSD_GROUPS, LANES)[:, :, :SSD_HPG].reshape(T, 16)
    dproj = jnp.concatenate([dz, dxbc.reshape(T, -1), dq_a, dkv_a, dk_rope, ddt16, jnp.zeros((T, 112), F32)],
                            axis=1).astype(BF16)
    dh1 = mm(dproj, kw["wcat"], "nt", n("in_dx"))
    gw["w_in"] = w_in_grad_storage(mm(sv["h1"], dproj, "tn", n("in_dw"), out_dtype=BF16))
    dx0, gs["attn_norm_g"], dx0b = norm_bwd(sv["x0"], sm["attn_norm_g"], dh1, D_MODEL, False, n("attn_norm"), add=dx1,
                                            bf16_copy=True)
    gs = {k: v.reshape(-1) for k, v in gs.items()}
    return dx0, dx0b, gw, gs


def _small_pack(vals):
    flat = jnp.concatenate([vals[l][k].reshape(-1) for l in range(DEPTH) for k in SMALL_RS])
    n = flat.shape[0]
    rows = -(-n // LANES)
    rows = -(-rows // 8) * 8
    return jnp.pad(flat, (0, rows * LANES - n)).reshape(rows, LANES), n


def _small_unpack(flat, shapes):
    v = flat.reshape(-1)
    out, off = [dict() for _ in range(DEPTH)], 0
    for l in range(DEPTH):
        for k in SMALL_RS:
            sz = shapes[k]
            out[l][k] = v[off:off + sz]
            off += sz
    return out


def _adam_2d(a):
    if a.ndim == 2:
        return a
    return a.reshape(a.shape[0] * a.shape[1], a.shape[2])


def kernel(x, mem, positions, attn_norm_g, w_in, conv_w, conv_b, dt_bias, a_log, d_skip, ssd_norm_g, q_a_norm_g, w_q_b, kv_a_norm_g, w_kv_b, mla_q_norm_g, mla_k_norm_g, w_out, xattn_norm_g, mem_norm_g, w_xq, w_xk, w_xv, xq_norm_g, xk_norm_g, w_xo, ffn_norm_g, w_gate, w_up, w_down, loss_target, m_attn_norm_g, m_w_in, m_conv_w, m_conv_b, m_dt_bias, m_a_log, m_d_skip, m_ssd_norm_g, m_q_a_norm_g, m_w_q_b, m_kv_a_norm_g, m_w_kv_b, m_mla_q_norm_g, m_mla_k_norm_g, m_w_out, m_xattn_norm_g, m_mem_norm_g, m_w_xq, m_w_xk, m_w_xv, m_xq_norm_g, m_xk_norm_g, m_w_xo, m_ffn_norm_g, m_w_gate, m_w_up, m_w_down, v_attn_norm_g, v_w_in, v_conv_w, v_conv_b, v_dt_bias, v_a_log, v_d_skip, v_ssd_norm_g, v_q_a_norm_g, v_w_q_b, v_kv_a_norm_g, v_w_kv_b, v_mla_q_norm_g, v_mla_k_norm_g, v_w_out, v_xattn_norm_g, v_mem_norm_g, v_w_xq, v_w_xk, v_w_xv, v_xq_norm_g, v_xk_norm_g, v_w_xo, v_ffn_norm_g, v_w_gate, v_w_up, v_w_down):
    loc = locals()
    W = {k: loc[k] for k in WEIGHTS}
    Mo = {k: loc["m_" + k] for k in WEIGHTS}
    Vo = {k: loc["v_" + k] for k in WEIGHTS}
    B, S, D = x.shape
    T = B * S
    xf = x.reshape(T, D)
    memf = mem.reshape(-1, D)
    tf = loss_target.reshape(T, D)
    cos, sin = rope_tables(positions.reshape(T, 1).astype(F32))

    cw = gather_all_devices(conv_w.reshape(DEPTH * SSD_CONV, -1), "gather_conv_w")[0::2]
    cw = cw.transpose(1, 0, 2).reshape(DEPTH, SSD_CONV, SSD_CONV_DIM)
    sms = [{k: W[k][l] for k in SMALL} for l in range(DEPTH)]
    me = _me_chip()

    cw, w_in_first = lax.optimization_barrier((cw, W["w_in"][0]))

    def gather_start(l):
        shards = [(wq_to_storage(W[k][l]) if k == "w_q_b" else W[k][l]).astype(BF16) for k in GATHERED]
        if l == 0:
            shards[0] = w_in_first.astype(BF16)
        lands = [lax.dynamic_update_slice(lax.empty((4,) + s.shape, BF16), s[None], (me, 0, 0)) for s in shards]
        return ici_start("gather", [], lands, "gather_start_l%d" % l)

    def gather_end(state, names, after, tag):
        idx = [GATHERED.index(k) for k in names]
        pick3 = lambda sems: [sems[3 * w + j] for w in idx for j in range(3)]
        sub = dict(kind=state["kind"], srcs=[], lands=[state["lands"][w] for w in idx],
                   ssem=pick3(state["ssem"]), rsem=pick3(state["rsem"]))
        _, lands = ici_wait(sub, after, "gather_wait_" + tag)
        return {k: z[None] for k, z in zip(names, forward_pair(lands, "gather_forward_" + tag))}

    class Weights(dict):
        def __init__(self, state, groups, after, tag):
            super().__init__(gather_end(state, groups[0], after, tag + "a"))
            self.state, self.later, self.after, self.tag = state, list(groups[1:]), after, tag

        def mark(self, value):
            self.after = value

        def __missing__(self, k):
            names = self.later.pop(0)
            self.update(gather_end(self.state, names, self.after, self.tag + "bcdef"[len(self.later)]))
            return self[k]

    FIRST_GROUPS = (("w_in",), ("w_q_b", "w_kv_b", "w_out", "w_xq", "w_xk", "w_xv", "w_xo"), ("w_gate", "w_up", "w_down"))

    saved, Gs, kws = [], [], []
    h = xf
    state, token = gather_start(0)
    for l in range(DEPTH):
        Gl = Weights(state, FIRST_GROUPS if l == 0 else (GATHERED,), h, "l%d" % l)
        kwl = dict(wcat=wcat_from_storage(Gl["w_in"][0]), conv_w=jnp.pad(cw[l], ((0, 8 - SSD_CONV), (0, 0))))
        Gs.append(Gl)
        kws.append(kwl)
        if l + 1 < DEPTH:
            state, token = gather_start(l + 1)
        h, sv = layer_fwd(h, memf, cos, sin, Gl, kwl, sms[l], B, S, l, dep=token if l + 1 < DEPTH else None)
        saved.append(sv)
    dh, lpart, dhb = loss_head(h, tf)
    loss = lax.psum(lpart[0, 0], ("x", "y", "c"))

    grads, delta, new_m, new_v = {}, {}, {}, {}
    chain = {k: None for k in GATHERED}
    gsmall = [None] * DEPTH
    pending = [[] for _ in range(DEPTH)]

    def begin(gw, l, tag):
        rs_state, tok = reduce_scatter_begin(gw, "l%d%s" % (l, tag))
        pending[l].append((rs_state, "l%d%s" % (l, tag)))
        return tok

    def finish(l, after):
        for rs_state, tag in pending[l]:
            red = reduce_scatter_end(rs_state, after, tag)
            for k, g in red.items():
                g = wq_from_storage(g) if k == "w_q_b" else g
                chain[k] = adamw_layer(W[k], g, Mo[k], Vo[k], l, chain[k], "adamw_%s_l%d" % (k, l))

    token = None
    for l in reversed(range(DEPTH)):
        dh, dhb, gw, gs = layer_bwd(dh, dhb, saved[l], memf, cos, sin, Gs[l], kws[l], sms[l], B, S, l, dep=token,
                                    rs_hook=functools.partial(begin, l=l, tag="_ffn"))
        saved[l] = None
        gsmall[l] = gs
        token = begin(gw, l, "_mix")
        if l + 1 < DEPTH:
            finish(l + 1, token)
    grad_x = dh.reshape(B, S, D)

    spack, _ = _small_pack(gsmall)
    allv = gather_all_devices(spack, "gather_small_grads")
    ssum = sum_arrays([allv[i] for i in range(8)], F32, "sum_small_grads")
    sizes = {k: int(np.prod(W[k].shape[1:])) for k in SMALL}
    sizes["conv_w"] = SSD_CONV * SSD_CONV_DIM
    gsm = _small_unpack(ssum, sizes)
    for k in SMALL_RS:
        g = jnp.stack([gsm[l][k] for l in range(DEPTH)])
        if k == "conv_w":
            cs = SSD_CONV_DIM // 4
            g = lax.dynamic_slice_in_dim(g.reshape(DEPTH, SSD_CONV, SSD_CONV_DIM), _me_chip() * cs, cs, axis=2)
        shp = W[k].shape
        grads[k] = g.reshape(shp)
        d, nm, nv = adamw(_adam_2d(W[k]), _adam_2d(grads[k]), _adam_2d(Mo[k]), _adam_2d(Vo[k]), "adamw_" + k)
        delta[k], new_m[k], new_v[k] = d.reshape(shp), nm.reshape(shp), nv.reshape(shp)

    finish(0, lax.optimization_barrier((delta["ffn_norm_g"], chain["w_down"][1], chain["w_gate"][1], chain["w_up"][1]))[0])
    for k in GATHERED:
        grads[k], delta[k], new_m[k], new_v[k] = chain[k]

    return (loss, grad_x, *[grads[k] for k in WEIGHTS], *[delta[k] for k in WEIGHTS], *[new_m[k] for k in WEIGHTS],
            *[new_v[k] for k in WEIGHTS])
```

```python
import functools
import math

import numpy as np
import jax
import jax.numpy as jnp
from jax import lax
from jax.experimental import pallas as pl
from jax.experimental.pallas import tpu as pltpu

F32 = jnp.float32
BF16 = jnp.bfloat16
MESH = pl.DeviceIdType.MESH
HI = lax.Precision.HIGHEST

D_MODEL = 2048
DEPTH = 4
SSD_INNER = 1024
SSD_HEAD_DIM = 64
SSD_GROUPS = 2
SSD_HPG = 8
SSD_STATE = 128
SSD_CONV = 4
SSD_CHUNK = 128
SSD_CONV_DIM = 1536
SSD_GW = SSD_HPG * SSD_HEAD_DIM
MLA_HEADS = 8
MLA_NOPE = 128
MLA_ROPE = 64
MLA_QK = 192
MLA_V = 128
Q_LORA = 512
KV_LORA = 512
ROPE_THETA = 10000.0
X_HEADS = 4
X_HEAD_DIM = 128
X_INNER = 512
FFN = 5632
IN_COLS = 3664
EPS = 1e-6
ADAM_LR, ADAM_B1, ADAM_B2, ADAM_EPS, ADAM_WD, ADAM_STEP = 0.001, 0.9, 0.999, 1e-08, 0.01, 10

C_Z, C_XBC, C_QA, C_KVA, C_KR, C_DT, CAT_COLS = 0, 1024, 2560, 3072, 3584, 3712, 3840

LANES = 128
VMEM_LIMIT = 56 * 1024 * 1024
MM_VMEM_BUDGET = 40 * 1024 * 1024
MM_STEP_BYTES = 1024 * 1024
NORM_BLOCK_ELEMS = 512 * 1024
ATTN_SCORE_ELEMS = 1024 * 1024

SMALL = ("attn_norm_g", "conv_b", "dt_bias", "a_log", "d_skip", "ssd_norm_g", "q_a_norm_g", "kv_a_norm_g",
         "mla_q_norm_g", "mla_k_norm_g", "xattn_norm_g", "mem_norm_g", "xq_norm_g", "xk_norm_g", "ffn_norm_g")
SMALL_RS = SMALL + ("conv_w",)
WEIGHTS = ("attn_norm_g", "w_in", "conv_w", "conv_b", "dt_bias", "a_log", "d_skip", "ssd_norm_g", "q_a_norm_g", "w_q_b",
           "kv_a_norm_g", "w_kv_b", "mla_q_norm_g", "mla_k_norm_g", "w_out", "xattn_norm_g", "mem_norm_g", "w_xq", "w_xk",
           "w_xv", "xq_norm_g", "xk_norm_g", "w_xo", "ffn_norm_g", "w_gate", "w_up", "w_down")


def _pcall(body, **kw):
    return pl.pallas_call(body, **kw)


def _params(sem):
    return pltpu.CompilerParams(dimension_semantics=sem, vmem_limit_bytes=VMEM_LIMIT)


def _tile(n, cands):
    for t in cands:
        if n % t == 0:
            return t
    return n


def _sigmoid(z):
    return 1.0 / (1.0 + jnp.exp(-z))


def _mtile(n, cap):
    if n % LANES:
        return n
    q = n // LANES
    t = LANES * max(d for d in range(1, q + 1) if q % d == 0 and LANES * d <= cap)
    return n if (t < 512 and n <= 2048) else t


def _divisor_tiles(n, cap):
    if n % LANES:
        return [n]
    q = n // LANES
    return [LANES * d for d in range(1, q + 1) if q % d == 0 and LANES * d <= cap] or [n]


def _mm_tiles(M, N, K, m_unit, n_unit, k_unit, a_item, b_item, o_item, has_add):
    best = None
    for tm in _divisor_tiles(m_unit, 2048):
        for tn in _divisor_tiles(n_unit, 2048):
            for tk in _divisor_tiles(k_unit, 2048):
                nk = K // tk
                vmem = 2 * (tm * tk * a_item + tk * tn * b_item + tm * tn * (o_item + (4 if has_add else 0)))
                vmem += tm * tn * 4 * (2 if nk > 1 else 1) + (tm * tk * 2 if a_item == 4 else 0) + (tk * tn * 2 if b_item == 4 else 0)
                if vmem > MM_VMEM_BUDGET:
                    continue
                steps = (M // tm) * (N // tn) * nk
                traffic = (M * K * a_item * (N // tn if nk > 1 else 1) + K * N * b_item * (M // tm)
                           + M * N * (o_item + (4 if has_add else 0)) + steps * MM_STEP_BYTES)
                if nk > 1:
                    traffic += steps * tm * tn * 4
                if best is None or traffic < best[0]:
                    best = (traffic, tm, tn, tk)
    assert best is not None, (M, N, K)
    return best[1:]


def mm(a, b, mode, name, add=None, out_dtype=F32, wv=None, out_wv=None, dep=None):
    kind, l = wv if wv is not None else (None, None)
    if kind == "row":
        b, kind = b[l].reshape(-1, b.shape[-1]), None
    if out_wv == "row":
        res = mm(a, b, mode, name, add=add, out_dtype=out_dtype, wv=wv if kind else None, dep=dep)
        return res.reshape(4, res.shape[0] // 4, res.shape[1])
    if kind == "col":
        _, _, r, c = b.shape
        bshape = (r, 4 * c)
    else:
        bshape = b.shape
    if mode == "nn":
        (M, K), (K2, N) = a.shape, bshape
    elif mode == "nt":
        (M, K), (N, K2) = a.shape, bshape
    else:
        (K, M), (K2, N) = a.shape, bshape
    assert K == K2, (a.shape, bshape, mode)
    m_unit = M
    n_unit = N // 4 if (out_wv == "col" or (kind == "col" and mode == "nn")) else N
    k_unit = K // 4 if (kind == "col" and mode == "nt") else K
    tm, tn, tk = _mm_tiles(M, N, K, m_unit, n_unit, k_unit, a.dtype.itemsize, b.dtype.itemsize,
                           jnp.dtype(out_dtype).itemsize, add is not None)
    nk = K // tk
    qm, qn, qk = m_unit // tm, n_unit // tn, k_unit // tk
    dn = {"nn": (((1,), (0,)), ((), ())), "nt": (((1,), (1,)), ((), ())), "tn": (((0,), (0,)), ((), ()))}[mode]
    has_add = add is not None

    def body(a_ref, b_ref, *rest):
        rest = list(rest)
        c_ref = rest.pop(0) if has_add else None
        if dep is not None:
            rest.pop(0)
        o_ref = rest.pop(0)
        d = lax.dot_general(a_ref[...].astype(BF16), b_ref[...].astype(BF16), dn, preferred_element_type=F32)

        def finish(res):
            if has_add:
                res = res + c_ref[...].astype(F32)
            o_ref[...] = res.astype(out_dtype)

        if nk == 1:
            finish(d)
            return
        acc = rest.pop(0)
        k = pl.program_id(2)

        @pl.when(k == 0)
        def _():
            acc[...] = d

        if nk > 2:
            @pl.when((k > 0) & (k < nk - 1))
            def _():
                acc[...] += d

        @pl.when(k == nk - 1)
        def _():
            finish(acc[...] + d)

    if mode == "tn":
        a_spec = pl.BlockSpec((tk, tm), lambda i, j, k: (k, i))
    else:
        a_spec = pl.BlockSpec((tm, tk), lambda i, j, k: (i, k))
    if kind is None:
        b_spec = (pl.BlockSpec((tn, tk), lambda i, j, k: (j, k)) if mode == "nt"
                  else pl.BlockSpec((tk, tn), lambda i, j, k: (k, j)))
    elif mode == "nn":
        b_spec = pl.BlockSpec((None, None, tk, tn), lambda i, j, k: (l, j // qn, k, j % qn))
    elif mode == "nt":
        b_spec = pl.BlockSpec((None, None, tn, tk), lambda i, j, k: (l, k // qk, j, k % qk))
    else:
        raise ValueError((mode, kind))
    if out_wv == "col":
        o_spec = pl.BlockSpec((None, tm, tn), lambda i, j, k: (j // qn, i, j % qn))
        out_shape = jax.ShapeDtypeStruct((4, M, N // 4), out_dtype)
    else:
        o_spec = pl.BlockSpec((tm, tn), lambda i, j, k: (i, j))
        out_shape = jax.ShapeDtypeStruct((M, N), out_dtype)
    in_specs = [a_spec, b_spec] + ([o_spec] if has_add else []) + ([pl.BlockSpec(memory_space=pl.ANY)] if dep is not None else [])
    args = (a, b) + ((add,) if has_add else ()) + ((dep,) if dep is not None else ())
    return _pcall(
        body, name=name, grid=(M // tm, N // tn, nk), in_specs=in_specs, out_specs=o_spec,
        out_shape=out_shape, scratch_shapes=[pltpu.VMEM((tm, tn), F32)] if nk > 1 else [],
        compiler_params=_params(("parallel", "parallel", "arbitrary")))(*args)


def norm_fwd(x, gain, W, shared, name, gate=None, out_dtype=BF16, dep=None):
    T, C = x.shape
    nb = C // W
    tq = _tile(T, tuple(t for t in (4096, 2048, 1024, 512, 256, 128) if t * W <= NORM_BLOCK_ELEMS))
    g2 = gain.reshape(1, -1).astype(F32)
    gated = gate is not None

    def body(*refs):
        o_ref = refs[-1]
        if gated:
            x_ref, z_ref, g_ref = refs[:3]
        else:
            x_ref, g_ref = refs[:2]
        u = x_ref[...].astype(F32)
        if gated:
            z = z_ref[...]
            u = u * (z * _sigmoid(z))
        ms = jnp.mean(u * u, axis=-1, keepdims=True)
        o_ref[...] = (u * lax.rsqrt(ms + EPS) * g_ref[...]).astype(out_dtype)

    xs = pl.BlockSpec((tq, W), lambda j, i: (i, j))
    gs = pl.BlockSpec((1, W), (lambda j, i: (0, 0)) if shared else (lambda j, i: (0, j)))
    in_specs = [xs] + ([xs] if gated else []) + [gs] + ([pl.BlockSpec(memory_space=pl.ANY)] if dep is not None else [])
    args = (x,) + ((gate,) if gated else ()) + (g2,) + ((dep,) if dep is not None else ())
    return _pcall(body, name=name, grid=(nb, T // tq), in_specs=in_specs, out_specs=xs,
                  out_shape=jax.ShapeDtypeStruct((T, C), out_dtype),
                  compiler_params=_params(("parallel", "parallel")))(*args)


def norm_bwd(x, gain, dy, W, shared, name, gate=None, add=None, bf16_copy=False):
    T, C = x.shape
    nb = C // W
    tq = _tile(T, tuple(t for t in (4096, 2048, 1024, 512, 256, 128) if t * W <= NORM_BLOCK_ELEMS))
    g2 = gain.reshape(1, -1).astype(F32)
    gated = gate is not None
    has_add = add is not None
    GC = W if shared else C

    def body(*refs):
        refs = list(refs)
        x_ref = refs.pop(0)
        z_ref = refs.pop(0) if gated else None
        g_ref = refs.pop(0)
        dy_ref = refs.pop(0)
        a_ref = refs.pop(0) if has_add else None
        dx_ref = refs.pop(0)
        dz_ref = refs.pop(0) if gated else None
        dg_ref = refs.pop(0)
        j, i = pl.program_id(0), pl.program_id(1)
        first = (i == 0) & (j == 0) if shared else (i == 0)

        @pl.when(first)
        def _():
            dg_ref[...] = jnp.zeros_like(dg_ref)

        xv = x_ref[...].astype(F32)
        if gated:
            z = z_ref[...]
            s = _sigmoid(z)
            u = xv * (z * s)
        else:
            u = xv
        rstd = lax.rsqrt(jnp.mean(u * u, axis=-1, keepdims=True) + EPS)
        uh = u * rstd
        dyv = dy_ref[...].astype(F32)
        dg_ref[...] += jnp.sum(dyv * uh, axis=0, keepdims=True)
        dyg = dyv * g_ref[...]
        du = rstd * (dyg - uh * jnp.mean(dyg * uh, axis=-1, keepdims=True))
        if gated:
            dx = du * (z * s)
            dz_ref[...] = du * xv * (s * (1.0 + z * (1.0 - s)))
        else:
            dx = du
        if has_add:
            dx = dx + a_ref[...]
        dx_ref[...] = dx
        if bf16_copy:
            refs.pop(0)[...] = dx.astype(BF16)

    xs = pl.BlockSpec((tq, W), lambda j, i: (i, j))
    gs = pl.BlockSpec((1, W), (lambda j, i: (0, 0)) if shared else (lambda j, i: (0, j)))
    in_specs = [xs] + ([xs] if gated else []) + [gs, xs] + ([xs] if has_add else [])
    args = (x,) + ((gate,) if gated else ()) + (g2, dy) + ((add,) if has_add else ())
    out_specs = [xs] + ([xs] if gated else []) + [gs] + ([xs] if bf16_copy else [])
    out_shape = [jax.ShapeDtypeStruct((T, C), F32)] + ([jax.ShapeDtypeStruct((T, C), F32)] if gated else []) \
        + [jax.ShapeDtypeStruct((1, GC), F32)] + ([jax.ShapeDtypeStruct((T, C), BF16)] if bf16_copy else [])
    return _pcall(body, name=name, grid=(nb, T // tq), in_specs=in_specs, out_specs=out_specs, out_shape=out_shape,
                  compiler_params=_params(("arbitrary", "arbitrary")))(*args)


def _inv_freq_row():
    inv = 1.0 / (ROPE_THETA ** (np.arange(0, MLA_ROPE, 2, dtype=np.float32) / MLA_ROPE))
    row = np.zeros((1, LANES), np.float32)
    row[0, :32] = inv
    row[0, 32:64] = inv
    return jnp.asarray(row)


def rope_tables(pos):
    T = pos.shape[0]
    tq = _tile(T, (512, 256, 128))

    def body(p_ref, f_ref, c_ref, s_ref):
        ang = p_ref[...] * f_ref[...]
        lane = lax.broadcasted_iota(jnp.int32, ang.shape, 1)
        c_ref[...] = jnp.where(lane < 64, jnp.cos(ang), 0.0)
        sn = jnp.sin(ang)
        s_ref[...] = jnp.where(lane < 32, -sn, jnp.where(lane < 64, sn, 0.0))

    ts = pl.BlockSpec((tq, LANES), lambda i: (i, 0))
    return _pcall(body, name="rope_tables", grid=(T // tq,),
                  in_specs=[pl.BlockSpec((tq, 1), lambda i: (i, 0)), pl.BlockSpec((1, LANES), lambda i: (0, 0))],
                  out_specs=[ts, ts], out_shape=[jax.ShapeDtypeStruct((T, LANES), F32)] * 2,
                  compiler_params=_params(("parallel",)))(pos, _inv_freq_row())


def _swap_halves(r):
    lane = lax.broadcasted_iota(jnp.int32, r.shape, 1)
    return jnp.where(lane < 32, pltpu.roll(r, 96, 1), pltpu.roll(r, 32, 1))


def qk_fwd(nope_arr, nope_blk0, rope_arr, rope_blk0, rope_shared, gn, gr, cos, sin, name):
    T = nope_arr.shape[0]
    tq = _tile(T, (256, 128))

    col = lambda blk: slice(blk * LANES, (blk + 1) * LANES)

    def body(n_ref, r_ref, gn_ref, gr_ref, c_ref, s_ref, o_ref):
        for h in range(MLA_HEADS):
            n = n_ref[:, col(nope_blk0(h))]
            r = r_ref[...] if rope_shared else n_ref[:, col(rope_blk0(h))]
            ms = (jnp.sum(n * n, axis=-1, keepdims=True) + jnp.sum(r * r, axis=-1, keepdims=True)) * (1.0 / MLA_QK)
            rstd = lax.rsqrt(ms + EPS)
            rn = r * rstd * gr_ref[...]
            rr = rn * c_ref[...] + _swap_halves(rn) * s_ref[...]
            o_ref[:, col(2 * h)] = (n * rstd * gn_ref[...]).astype(BF16)
            o_ref[:, col(2 * h + 1)] = rr.astype(BF16)

    wide = lambda arr: pl.BlockSpec((tq, arr.shape[1]), lambda i: (i, 0))
    assert rope_shared or rope_arr is nope_arr
    rspec = pl.BlockSpec((tq, LANES), lambda i: (i, rope_blk0)) if rope_shared else pl.BlockSpec((8, LANES), lambda i: (0, 0))
    row = pl.BlockSpec((1, LANES), lambda i: (0, 0))
    tab = pl.BlockSpec((tq, LANES), lambda i: (i, 0))
    return _pcall(body, name=name, grid=(T // tq,),
                  in_specs=[wide(nope_arr), rspec, row, row, tab, tab],
                  out_specs=pl.BlockSpec((tq, MLA_HEADS * 2 * LANES), lambda i: (i, 0)),
                  out_shape=jax.ShapeDtypeStruct((T, MLA_HEADS * 2 * LANES), BF16),
                  compiler_params=_params(("parallel",)))(nope_arr, rope_arr, gn, gr, cos, sin)


def qk_bwd(nope_arr, nope_blk0, rope_arr, rope_blk0, rope_shared, gn, gr, cos, sin, dout, name, fill=None, fill_blk=None):
    T = nope_arr.shape[0]
    tq = _tile(T, (256, 128))
    col = lambda blk: slice(blk * LANES, (blk + 1) * LANES)

    def body(*refs):
        refs = list(refs)
        n_ref, r_ref, gn_ref, gr_ref, c_ref, s_ref, d_ref = refs[:7]
        f_ref = refs[7] if rope_shared else None
        outs = refs[8:] if rope_shared else refs[7:]
        draw_ref = outs[0]
        dsh_ref = outs[1] if rope_shared else None
        dgn_ref, dgr_ref = outs[-2:]

        @pl.when(pl.program_id(0) == 0)
        def _():
            dgn_ref[...] = jnp.zeros_like(dgn_ref)
            dgr_ref[...] = jnp.zeros_like(dgr_ref)

        dgn = jnp.zeros((1, LANES), F32)
        dgr = jnp.zeros((1, LANES), F32)
        dshared = jnp.zeros((tq, LANES), F32)
        for h in range(MLA_HEADS):
            n = n_ref[:, col(nope_blk0(h))]
            r = r_ref[...] if rope_shared else n_ref[:, col(rope_blk0(h))]
            ms = (jnp.sum(n * n, axis=-1, keepdims=True) + jnp.sum(r * r, axis=-1, keepdims=True)) * (1.0 / MLA_QK)
            rstd = lax.rsqrt(ms + EPS)
            nh = n * rstd
            rh = r * rstd
            d_n = d_ref[:, col(2 * h)].astype(F32)
            d_rr = d_ref[:, col(2 * h + 1)].astype(F32)
            d_rn = d_rr * c_ref[...] - _swap_halves(d_rr) * s_ref[...]
            dgn = dgn + jnp.sum(d_n * nh, axis=0, keepdims=True)
            dgr = dgr + jnp.sum(d_rn * rh, axis=0, keepdims=True)
            dng = d_n * gn_ref[...]
            drg = d_rn * gr_ref[...]
            mean = (jnp.sum(dng * nh, axis=-1, keepdims=True) + jnp.sum(drg * rh, axis=-1, keepdims=True)) * (1.0 / MLA_QK)
            draw_ref[:, col(nope_blk0(h))] = (rstd * (dng - nh * mean)).astype(BF16)
            drope = rstd * (drg - rh * mean)
            if rope_shared:
                dshared = dshared + drope
                draw_ref[:, col(fill_blk(h))] = f_ref[:, col(h)].astype(BF16)
            else:
                draw_ref[:, col(rope_blk0(h))] = drope.astype(BF16)
        dgn_ref[...] += dgn
        dgr_ref[...] += dgr
        if rope_shared:
            dsh_ref[...] = dshared

    wide = lambda arr: pl.BlockSpec((tq, arr.shape[1]), lambda i: (i, 0))
    assert rope_shared or rope_arr is nope_arr
    rspec = pl.BlockSpec((tq, LANES), lambda i: (i, rope_blk0)) if rope_shared else pl.BlockSpec((8, LANES), lambda i: (0, 0))
    row = pl.BlockSpec((1, LANES), lambda i: (0, 0))
    tab = pl.BlockSpec((tq, LANES), lambda i: (i, 0))
    raw = pl.BlockSpec((tq, 2 * MLA_HEADS * LANES), lambda i: (i, 0))
    in_specs = [wide(nope_arr), rspec, row, row, tab, tab, wide(dout)] + ([wide(fill)] if rope_shared else [])
    out_specs = [raw] + ([tab] if rope_shared else []) + [row, row]
    out_shape = ([jax.ShapeDtypeStruct((T, 2 * MLA_HEADS * LANES), BF16)]
                 + ([jax.ShapeDtypeStruct((T, LANES), F32)] if rope_shared else [])
                 + [jax.ShapeDtypeStruct((1, LANES), F32)] * 2)
    args = (nope_arr, rope_arr, gn, gr, cos, sin, dout) + ((fill,) if rope_shared else ())
    return _pcall(body, name=name, grid=(T // tq,), in_specs=in_specs, out_specs=out_specs, out_shape=out_shape,
                  compiler_params=_params(("arbitrary",)))(*args)


def _scores(q, k, scale, causal, row0):
    s = lax.dot_general(q, k, (((1,), (1,)), ((), ())), preferred_element_type=F32) * scale
    if causal:
        qi = row0 + lax.broadcasted_iota(jnp.int32, s.shape, 0)
        ki = lax.broadcasted_iota(jnp.int32, s.shape, 1)
        s = jnp.where(qi >= ki, s, -1e30)
    return s


def attn_fwd(q, k, v, H, dqk, dv, scale, causal, name, v_idx=lambda h: h):
    B, S, _ = q.shape
    Sk = k.shape[1]
    tq = _tile(S, tuple(t for t in (2048, 1024, 512, 256, 128) if t * Sk <= ATTN_SCORE_ELEMS))

    def tile(q_ref, k_ref, v_ref, o_ref, row0, kl):
        s = _scores(q_ref[0], k_ref[0, :kl], scale, causal, row0)
        p = jnp.exp(s - jnp.max(s, axis=-1, keepdims=True))
        l = jnp.sum(p, axis=-1, keepdims=True)
        o = lax.dot_general(p.astype(BF16), v_ref[0, :kl], (((1,), (0,)), ((), ())), preferred_element_type=F32)
        o_ref[0] = (o / l).astype(BF16)

    def body(q_ref, k_ref, v_ref, o_ref):
        if not causal:
            tile(q_ref, k_ref, v_ref, o_ref, 0, Sk)
            return
        for qi in range(S // tq):
            @pl.when(pl.program_id(2) == qi)
            def _(qi=qi):
                tile(q_ref, k_ref, v_ref, o_ref, qi * tq, (qi + 1) * tq)

    return _pcall(body, name=name, grid=(B, H, S // tq),
                  in_specs=[pl.BlockSpec((1, tq, dqk), lambda b, h, i: (b, i, h)),
                            pl.BlockSpec((1, Sk, dqk), lambda b, h, i: (b, 0, h)),
                            pl.BlockSpec((1, Sk, dv), lambda b, h, i: (b, 0, v_idx(h)))],
                  out_specs=pl.BlockSpec((1, tq, dv), lambda b, h, i: (b, i, h)),
                  out_shape=jax.ShapeDtypeStruct((B, S, H * dv), BF16),
                  compiler_params=_params(("parallel", "parallel", "parallel")))(q, k, v)


def attn_bwd(q, k, v, do, H, dqk, dv, scale, causal, name, v_idx=lambda h: h):
    B, S, _ = q.shape
    Sk = k.shape[1]
    tq = _tile(S, tuple(t for t in (2048, 1024, 512, 256, 128) if t * Sk <= ATTN_SCORE_ELEMS))

    def body(q_ref, k_ref, v_ref, do_ref, dq_ref, dk_ref, dv_ref):
        i = pl.program_id(2)

        @pl.when(i == 0)
        def _():
            dk_ref[...] = jnp.zeros_like(dk_ref)
            dv_ref[...] = jnp.zeros_like(dv_ref)

        def tile(row0, kl):
            qv, kv, vv = q_ref[0], k_ref[0, :kl], v_ref[0, :kl]
            s = _scores(qv, kv, scale, causal, row0)
            p = jnp.exp(s - jnp.max(s, axis=-1, keepdims=True))
            p = p / jnp.sum(p, axis=-1, keepdims=True)
            dob = do_ref[0].astype(BF16)
            pb = p.astype(BF16)
            dv_ref[0, :kl] += lax.dot_general(pb, dob, (((0,), (0,)), ((), ())), preferred_element_type=F32)
            dp = lax.dot_general(dob, vv, (((1,), (1,)), ((), ())), preferred_element_type=F32)
            delta = jnp.sum(p * dp, axis=-1, keepdims=True)
            ds = (p * (dp - delta) * scale).astype(BF16)
            dq_ref[0] = lax.dot_general(ds, kv, (((1,), (0,)), ((), ())), preferred_element_type=F32)
            dk_ref[0, :kl] += lax.dot_general(ds, qv, (((0,), (0,)), ((), ())), preferred_element_type=F32)

        if not causal:
            tile(0, Sk)
            return
        for qi in range(S // tq):
            @pl.when(i == qi)
            def _(qi=qi):
                tile(qi * tq, (qi + 1) * tq)

    qs = pl.BlockSpec((1, tq, dqk), lambda b, h, i: (b, i, h))
    ks = pl.BlockSpec((1, Sk, dqk), lambda b, h, i: (b, 0, h))
    vs = pl.BlockSpec((1, Sk, dv), lambda b, h, i: (b, 0, h))
    vin = pl.BlockSpec((1, Sk, dv), lambda b, h, i: (b, 0, v_idx(h)))
    return _pcall(body, name=name, grid=(B, H, S // tq),
                  in_specs=[qs, ks, vin, pl.BlockSpec((1, tq, dv), lambda b, h, i: (b, i, h))],
                  out_specs=[qs, ks, vs],
                  out_shape=[jax.ShapeDtypeStruct((B, S, H * dqk), F32), jax.ShapeDtypeStruct((B, Sk, H * dqk), F32),
                             jax.ShapeDtypeStruct((B, Sk, H * dv), F32)],
                  compiler_params=_params(("parallel", "parallel", "arbitrary")))(q, k, v, do)


def _shift_down(u, j):
    if j == 0:
        return u
    row = lax.broadcasted_iota(jnp.int32, u.shape, 0)
    return jnp.where(row >= j, pltpu.roll(u, j, 0), 0.0)


def _shift_up(u, j):
    if j == 0:
        return u
    n = u.shape[0]
    row = lax.broadcasted_iota(jnp.int32, u.shape, 0)
    return jnp.where(row < n - j, pltpu.roll(u, n - j, 0), 0.0)


def conv_fwd(u, col0_blk, w8, b, name):
    B, S, _ = u.shape
    tc = 256
    nb = SSD_CONV_DIM // tc

    def body(u_ref, w_ref, b_ref, o_ref):
        uv = u_ref[0]
        pre = b_ref[...] + jnp.zeros_like(uv)
        for kk in range(SSD_CONV):
            pre = pre + w_ref[kk:kk + 1, :] * _shift_down(uv, SSD_CONV - 1 - kk)
        o_ref[0] = pre * _sigmoid(pre)

    return _pcall(body, name=name, grid=(B, nb),
                  in_specs=[pl.BlockSpec((1, S, tc), lambda bb, j: (bb, 0, col0_blk + j)),
                            pl.BlockSpec((8, tc), lambda bb, j: (0, j)), pl.BlockSpec((1, tc), lambda bb, j: (0, j))],
                  out_specs=pl.BlockSpec((1, S, tc), lambda bb, j: (bb, 0, j)),
                  out_shape=jax.ShapeDtypeStruct((B, S, SSD_CONV_DIM), F32),
                  compiler_params=_params(("parallel", "parallel")))(u, w8, b)


def conv_bwd(u, col0_blk, w8, b, dy, name):
    B, S, _ = u.shape
    tc = 256
    nb = SSD_CONV_DIM // tc

    def body(u_ref, w_ref, b_ref, dy_ref, du_ref, dw_ref, db_ref):
        bb = pl.program_id(1)

        @pl.when(bb == 0)
        def _():
            dw_ref[...] = jnp.zeros_like(dw_ref)
            db_ref[...] = jnp.zeros_like(db_ref)

        uv = u_ref[0]
        pre = b_ref[...] + jnp.zeros_like(uv)
        for kk in range(SSD_CONV):
            pre = pre + w_ref[kk:kk + 1, :] * _shift_down(uv, SSD_CONV - 1 - kk)
        s = _sigmoid(pre)
        dpre = dy_ref[0] * (s * (1.0 + pre * (1.0 - s)))
        du = jnp.zeros_like(uv)
        for kk in range(SSD_CONV):
            j = SSD_CONV - 1 - kk
            du = du + w_ref[kk:kk + 1, :] * _shift_up(dpre, j)
            dw_ref[kk:kk + 1, :] += jnp.sum(dpre * _shift_down(uv, j), axis=0, keepdims=True)
        db_ref[...] += jnp.sum(dpre, axis=0, keepdims=True)
        du_ref[0] = du

    return _pcall(body, name=name, grid=(nb, B),
                  in_specs=[pl.BlockSpec((1, S, tc), lambda j, bb: (bb, 0, col0_blk + j)),
                            pl.BlockSpec((8, tc), lambda j, bb: (0, j)), pl.BlockSpec((1, tc), lambda j, bb: (0, j)),
                            pl.BlockSpec((1, S, tc), lambda j, bb: (bb, 0, j))],
                  out_specs=[pl.BlockSpec((1, S, tc), lambda j, bb: (bb, 0, j)),
                             pl.BlockSpec((8, tc), lambda j, bb: (0, j)), pl.BlockSpec((1, tc), lambda j, bb: (0, j))],
                  out_shape=[jax.ShapeDtypeStruct((B, S, SSD_CONV_DIM), F32), jax.ShapeDtypeStruct((8, SSD_CONV_DIM), F32),
                             jax.ShapeDtypeStruct((1, SSD_CONV_DIM), F32)],
                  compiler_params=_params(("arbitrary", "arbitrary")))(u, w8, b, dy)


def _expand_mat():
    e = np.zeros((LANES, SSD_GW), np.float32)
    for h in range(SSD_HPG):
        e[h, h * SSD_HEAD_DIM:(h + 1) * SSD_HEAD_DIM] = 1.0
    return jnp.asarray(e)


def _tri_mat():
    return jnp.asarray(np.tril(np.ones((SSD_CHUNK, SSD_CHUNK), np.float32)))


def _split3(x):
    hi = x.astype(BF16)
    r = x - hi.astype(F32)
    mid = r.astype(BF16)
    return hi, mid, (r - mid.astype(F32)).astype(BF16)


def _dotf(a, b, dn=(((1,), (0,)), ((), ())), sel="b"):
    if sel == "b":
        s = b.astype(BF16)
        parts = [lax.dot_general(p, s, dn, preferred_element_type=F32) for p in _split3(a)]
    else:
        s = a.astype(BF16)
        parts = [lax.dot_general(s, p, dn, preferred_element_type=F32) for p in _split3(b)]
    return parts[0] + parts[1] + parts[2]


def _dotb(a, b, dn=(((1,), (0,)), ((), ()))):
    return lax.dot_general(a.astype(BF16), b.astype(BF16), dn, preferred_element_type=F32)


NT = (((1,), (1,)), ((), ()))
TN = (((0,), (0,)), ((), ()))


def _ssd_common(dt_ref, par_ref, tri_ref, e_ref):
    dtr = dt_ref[0]
    xb = dtr + par_ref[0:1, :]
    dt = jnp.maximum(xb, 0.0) + jnp.log(1.0 + jnp.exp(-jnp.abs(xb)))
    A = -jnp.exp(par_ref[1:2, :])
    a = dt * A
    acs = _dotf(tri_ref[...], a, sel="a")
    alast = acs[SSD_CHUNK - 1:SSD_CHUNK, :]
    E = e_ref[...]
    acsE = _dotf(acs, E)
    alastE = acsE[SSD_CHUNK - 1:SSD_CHUNK, :]
    return dict(xb=xb, dt=dt, A=A, acs=acs, acsT=acs.T, ealast=jnp.exp(alast), w=jnp.exp(alast - acs),
                dtE=_dotf(dt, E), eacsE=jnp.exp(acsE), wE=jnp.exp(alastE - acsE), ealastE=jnp.exp(alastE),
                DE=_dotf(par_ref[...], E)[2:3, :])


def _decay(c, e):
    diff = c["acs"][:, e:e + 1] - c["acsT"][e:e + 1, :]
    li = lax.broadcasted_iota(jnp.int32, diff.shape, 0)
    si = lax.broadcasted_iota(jnp.int32, diff.shape, 1)
    return jnp.where(li >= si, jnp.exp(jnp.minimum(diff, 0.0)), 0.0)


def ssd_fwd(xbc, dtp, par, name):
    B, S, _ = xbc.shape
    nc = S // SSD_CHUNK
    L, N, GW, P = SSD_CHUNK, SSD_STATE, SSD_GW, SSD_HEAD_DIM

    def body(x_ref, b_ref, c_ref, dt_ref, par_ref, tri_ref, e_ref, y_ref, st_ref, S_scr):
        @pl.when(pl.program_id(2) == 0)
        def _():
            S_scr[...] = jnp.zeros_like(S_scr)

        Sin = S_scr[...]
        st_ref[0, 0, 0] = Sin
        c = _ssd_common(dt_ref, par_ref.at[0], tri_ref, e_ref)
        X = x_ref[0]
        Bm = b_ref[0]
        Cm = c_ref[0]
        xdt = X * c["dtE"]
        CB = _dotb(Cm, Bm, NT)
        y_ref[0] = c["eacsE"] * _dotb(Cm, Sin) + c["DE"] * X
        for e in range(SSD_HPG):
            M = CB * _decay(c, e)
            y_ref[0, :, e * P:(e + 1) * P] += _dotb(M, xdt[:, e * P:(e + 1) * P])
        S_scr[...] = c["ealastE"] * Sin + _dotb(Bm, xdt * c["wE"], TN)

    return _pcall(
        body, name=name, grid=(SSD_GROUPS, B, nc),
        in_specs=[pl.BlockSpec((1, L, GW), lambda g, b, ci: (b, ci, g)),
                  pl.BlockSpec((1, L, N), lambda g, b, ci: (b, ci, 8 + g)),
                  pl.BlockSpec((1, L, N), lambda g, b, ci: (b, ci, 10 + g)),
                  pl.BlockSpec((1, L, LANES), lambda g, b, ci: (b, ci, g)),
                  pl.BlockSpec((1, 8, LANES), lambda g, b, ci: (g, 0, 0)),
                  pl.BlockSpec((L, L), lambda g, b, ci: (0, 0)),
                  pl.BlockSpec((LANES, GW), lambda g, b, ci: (0, 0))],
        out_specs=[pl.BlockSpec((1, L, GW), lambda g, b, ci: (b, ci, g)),
                   pl.BlockSpec((1, 1, 1, N, GW), lambda g, b, ci: (g, b, ci, 0, 0))],
        out_shape=[jax.ShapeDtypeStruct((B, S, SSD_INNER), F32), jax.ShapeDtypeStruct((SSD_GROUPS, B, nc, N, GW), F32)],
        scratch_shapes=[pltpu.VMEM((N, GW), F32)],
        compiler_params=_params(("arbitrary", "arbitrary", "arbitrary")))(xbc, xbc, xbc, dtp, par, _tri_mat(), _expand_mat())


def ssd_bwd(xbc, dtp, par, states, dy, name):
    B, S, _ = xbc.shape
    nc = S // SSD_CHUNK
    L, N, GW, P = SSD_CHUNK, SSD_STATE, SSD_GW, SSD_HEAD_DIM

    def body(x_ref, b_ref, c_ref, dt_ref, par_ref, tri_ref, e_ref, st_ref, dy_ref,
             dx_ref, db_ref, dc_ref, ddt_ref, dpar_ref, dS_scr, dxdt_scr):
        @pl.when(pl.program_id(2) == 0)
        def _():
            dS_scr[...] = jnp.zeros_like(dS_scr)

        @pl.when((pl.program_id(1) == 0) & (pl.program_id(2) == 0))
        def _():
            dpar_ref[...] = jnp.zeros_like(dpar_ref)

        c = _ssd_common(dt_ref, par_ref.at[0], tri_ref, e_ref)
        E = e_ref[...]
        red = lambda t: _dotf(t, E, NT)
        red_row = lambda t: red(jnp.broadcast_to(t, (8, GW)))[0:1, :]
        X, Bm, Cm = x_ref[0], b_ref[0], c_ref[0]
        Sin = st_ref[0, 0, 0]
        dY = dy_ref[0]
        dSo = dS_scr[...]
        xdt = X * c["dtE"]
        CB = _dotb(Cm, Bm, NT)
        dD = red_row(jnp.sum(dY * X, axis=0, keepdims=True))
        yoff = c["eacsE"] * _dotb(Cm, Sin)
        dacs = red(dY * yoff)
        dYe = dY * c["eacsE"]
        dC = _dotb(dYe, Sin, NT)
        dSin = _dotb(Cm, dYe, TN) + c["ealastE"] * dSo
        dB = _dotb(xdt * c["wE"], dSo, NT)
        dxw = _dotb(Bm, dSo)
        t = red(dxw * xdt) * c["w"]
        dacs = dacs - t
        dalast = jnp.sum(t, axis=0, keepdims=True) + c["ealast"] * red_row(jnp.sum(Sin * dSo, axis=0, keepdims=True))
        dxdt_scr[...] = dxw * c["wE"]
        dCB = jnp.zeros((L, L), F32)
        dacsT = jnp.zeros((LANES, L), F32)
        lane = lax.broadcasted_iota(jnp.int32, (1, LANES), 1)
        subl = lax.broadcasted_iota(jnp.int32, (LANES, 1), 0)
        for e in range(SSD_HPG):
            Lm = _decay(c, e)
            M = CB * Lm
            dYh = dY[:, e * P:(e + 1) * P]
            dM = _dotb(dYh, xdt[:, e * P:(e + 1) * P], NT)
            dxdt_scr[:, e * P:(e + 1) * P] += _dotb(M, dYh, TN)
            dCBe = dM * Lm
            dCB = dCB + dCBe
            Gm = dCBe * CB
            dacs = dacs + jnp.sum(Gm, axis=1, keepdims=True) * (lane == e).astype(F32)
            dacsT = dacsT - jnp.sum(Gm, axis=0, keepdims=True) * (subl == e).astype(F32)
        dC = dC + _dotb(dCB, Bm)
        dB = dB + _dotb(dCB, Cm, TN)
        dacs = dacs + dacsT.T
        row = lax.broadcasted_iota(jnp.int32, (L, 1), 0)
        dacs = dacs + jnp.where(row == L - 1, dalast, 0.0)
        da = _dotf(tri_ref[...], dacs, TN, sel="a")
        dxdt = dxdt_scr[...]
        ddt = da * c["A"] + red(dxdt * X)
        dA = jnp.sum(da * c["dt"], axis=0, keepdims=True)
        ddtr = ddt * _sigmoid(c["xb"])
        dx_ref[0] = c["DE"] * dY + dxdt * c["dtE"]
        db_ref[0] = dB
        dc_ref[0] = dC
        ddt_ref[0] = ddtr
        dpar_ref[0, 0:1, :] += jnp.sum(ddtr, axis=0, keepdims=True)
        dpar_ref[0, 1:2, :] += dA * c["A"]
        dpar_ref[0, 2:3, :] += dD
        dS_scr[...] = dSin

    rc = lambda ci: nc - 1 - ci
    xspec = pl.BlockSpec((1, L, GW), lambda g, b, ci: (b, rc(ci), g))
    return _pcall(
        body, name=name, grid=(SSD_GROUPS, B, nc),
        in_specs=[xspec,
                  pl.BlockSpec((1, L, N), lambda g, b, ci: (b, rc(ci), 8 + g)),
                  pl.BlockSpec((1, L, N), lambda g, b, ci: (b, rc(ci), 10 + g)),
                  pl.BlockSpec((1, L, LANES), lambda g, b, ci: (b, rc(ci), g)),
                  pl.BlockSpec((1, 8, LANES), lambda g, b, ci: (g, 0, 0)),
                  pl.BlockSpec((L, L), lambda g, b, ci: (0, 0)),
                  pl.BlockSpec((LANES, GW), lambda g, b, ci: (0, 0)),
                  pl.BlockSpec((1, 1, 1, N, GW), lambda g, b, ci: (g, b, rc(ci), 0, 0)),
                  xspec],
        out_specs=[xspec,
                   pl.BlockSpec((1, L, N), lambda g, b, ci: (b, rc(ci), g)),
                   pl.BlockSpec((1, L, N), lambda g, b, ci: (b, rc(ci), g)),
                   pl.BlockSpec((1, L, LANES), lambda g, b, ci: (b, rc(ci), g)),
                   pl.BlockSpec((1, 8, LANES), lambda g, b, ci: (g, 0, 0))],
        out_shape=[jax.ShapeDtypeStruct((B, S, SSD_INNER), F32), jax.ShapeDtypeStruct((B, S, SSD_GROUPS * N), F32),
                   jax.ShapeDtypeStruct((B, S, SSD_GROUPS * N), F32), jax.ShapeDtypeStruct((B, S, SSD_GROUPS * LANES), F32),
                   jax.ShapeDtypeStruct((SSD_GROUPS, 8, LANES), F32)],
        scratch_shapes=[pltpu.VMEM((N, GW), F32), pltpu.VMEM((L, GW), F32)],
        compiler_params=_params(("arbitrary", "arbitrary", "arbitrary")))(
            xbc, xbc, xbc, dtp, par, _tri_mat(), _expand_mat(), states, dy)


def gate_up_act(h, wg, wu, name):
    T, D = h.shape
    c = wg.shape[-1]
    tm = _tile(T, (512, 256, 128))

    def body(h_ref, g_w, u_w, g_ref, u_ref, a_ref):
        hv = h_ref[...]
        g = jnp.dot(hv, g_w[...], preferred_element_type=F32)
        u = jnp.dot(hv, u_w[...], preferred_element_type=F32)
        g_ref[...] = g.astype(BF16)
        u_ref[...] = u.astype(BF16)
        a_ref[...] = (g * _sigmoid(g) * u).astype(BF16)

    ws = pl.BlockSpec((None, None, D, c), lambda i, j: (0, j, 0, 0))
    os_ = pl.BlockSpec((tm, c), lambda i, j: (i, j))
    return _pcall(body, name=name, grid=(T // tm, 4), in_specs=[pl.BlockSpec((tm, D), lambda i, j: (i, 0)), ws, ws],
                  out_specs=[os_, os_, os_], out_shape=[jax.ShapeDtypeStruct((T, 4 * c), BF16)] * 3,
                  compiler_params=_params(("parallel", "parallel")))(h, wg, wu)


def gate_up_dx(dg, du, wg, wu, name):
    T = dg.shape[0]
    _, _, D, c = wg.shape
    tm, tn = _tile(T, (1024, 512, 256, 128)), _tile(D, (1024, 512, 256, 128))

    def body(g_ref, u_ref, gw_ref, uw_ref, o_ref, acc):
        k = pl.program_id(2)
        d = (lax.dot_general(g_ref[...], gw_ref[...], NT, preferred_element_type=F32)
             + lax.dot_general(u_ref[...], uw_ref[...], NT, preferred_element_type=F32))

        @pl.when(k == 0)
        def _():
            acc[...] = d

        @pl.when((k > 0) & (k < 3))
        def _():
            acc[...] += d

        @pl.when(k == 3)
        def _():
            o_ref[...] = acc[...] + d

    a_spec = pl.BlockSpec((tm, c), lambda i, j, k: (i, k))
    w_spec = pl.BlockSpec((None, None, tn, c), lambda i, j, k: (0, k, j, 0))
    return _pcall(body, name=name, grid=(T // tm, D // tn, 4), in_specs=[a_spec, a_spec, w_spec, w_spec],
                  out_specs=pl.BlockSpec((tm, tn), lambda i, j, k: (i, j)), out_shape=jax.ShapeDtypeStruct((T, D), F32),
                  scratch_shapes=[pltpu.VMEM((tm, tn), F32)],
                  compiler_params=_params(("parallel", "parallel", "arbitrary")))(dg, du, wg, wu)


def down_dx_swiglu(dy, w_down, gate, up, name, dep=None):
    T, D = dy.shape
    Fh = w_down.shape[0]
    tm, tn = _tile(T, (1024, 512, 256, 128)), 512

    def body(dy_ref, w_ref, g_ref, u_ref, *rest):
        dg_ref, du_ref = rest[-2:]
        d = lax.dot_general(dy_ref[...], w_ref[...], NT, preferred_element_type=F32)
        g = g_ref[...].astype(F32)
        u = u_ref[...].astype(F32)
        s = _sigmoid(g)
        dg_ref[...] = (d * u * (s * (1.0 + g * (1.0 - s)))).astype(BF16)
        du_ref[...] = (d * g * s).astype(BF16)

    ts = pl.BlockSpec((tm, tn), lambda i, j: (i, j))
    in_specs = [pl.BlockSpec((tm, D), lambda i, j: (i, 0)), pl.BlockSpec((tn, D), lambda i, j: (j, 0)), ts, ts]
    args = (dy, w_down, gate, up)
    if dep is not None:
        in_specs, args = in_specs + [pl.BlockSpec(memory_space=pl.ANY)], args + (dep,)
    return _pcall(body, name=name, grid=(T // tm, Fh // tn), in_specs=in_specs, out_specs=[ts, ts],
                  out_shape=[jax.ShapeDtypeStruct((T, Fh), BF16)] * 2,
                  compiler_params=_params(("parallel", "parallel")))(*args)


def swiglu_fwd(g, u, name):
    T = g.shape[0]
    tq, tc = _tile(T, (512, 256, 128)), 512
    nf = FFN // tc

    def body(g_ref, u_ref, o_ref):
        gv = g_ref[...].astype(F32)
        o_ref[...] = (gv * _sigmoid(gv) * u_ref[...].astype(F32)).astype(BF16)

    lo = pl.BlockSpec((tq, tc), lambda i, j: (i, j))
    return _pcall(body, name=name, grid=(T // tq, nf), in_specs=[lo, lo], out_specs=lo,
                  out_shape=jax.ShapeDtypeStruct((T, FFN), BF16), compiler_params=_params(("parallel", "parallel")))(g, u)


def swiglu_bwd(gate, up, dact, name):
    T = gate.shape[0]
    tq, tc = _tile(T, (512, 256, 128)), 512
    nf = FFN // tc

    def body(g_ref, u_ref, d_ref, dg_ref, du_ref):
        g = g_ref[...].astype(F32)
        u = u_ref[...].astype(F32)
        d = d_ref[...]
        s = _sigmoid(g)
        dg_ref[...] = (d * u * (s * (1.0 + g * (1.0 - s)))).astype(BF16)
        du_ref[...] = (d * g * s).astype(BF16)

    lo = pl.BlockSpec((tq, tc), lambda i, j: (i, j))
    return _pcall(body, name=name, grid=(T // tq, nf), in_specs=[lo, lo, lo], out_specs=[lo, lo],
                  out_shape=[jax.ShapeDtypeStruct((T, FFN), BF16)] * 2,
                  compiler_params=_params(("parallel", "parallel")))(gate, up, dact)


def loss_head(y, target):
    T, D = y.shape
    tq = _tile(T, (256, 128))

    def body(y_ref, t_ref, dy_ref, l_ref, dyb_ref):
        @pl.when(pl.program_id(0) == 0)
        def _():
            l_ref[...] = jnp.zeros_like(l_ref)

        e = y_ref[...] - t_ref[...]
        dy = e * (1.0 / D)
        dy_ref[...] = dy
        dyb_ref[...] = dy.astype(BF16)
        l_ref[...] += 0.5 * jnp.sum(jnp.mean(e * e, axis=-1, keepdims=True))

    xs = pl.BlockSpec((tq, D), lambda i: (i, 0))
    return _pcall(body, name="loss_head", grid=(T // tq,), in_specs=[xs, xs],
                  out_specs=[xs, pl.BlockSpec((8, LANES), lambda i: (0, 0)), xs],
                  out_shape=[jax.ShapeDtypeStruct((T, D), F32), jax.ShapeDtypeStruct((8, LANES), F32),
                             jax.ShapeDtypeStruct((T, D), BF16)],
                  compiler_params=_params(("arbitrary",)))(y, target)


def sum_arrays(arrs, out_dtype, name):
    R, C = arrs[0].shape
    tr = _tile(R, (512, 256, 128, 64, 32, 16, 8))
    n = len(arrs)

    def body(*refs):
        acc = refs[0][...].astype(F32)
        for r in refs[1:n]:
            acc = acc + r[...].astype(F32)
        refs[n][...] = acc.astype(out_dtype)

    xs = pl.BlockSpec((tr, C), lambda i: (i, 0))
    return _pcall(body, name=name, grid=(R // tr,), in_specs=[xs] * n, out_specs=xs,
                  out_shape=jax.ShapeDtypeStruct((R, C), out_dtype), compiler_params=_params(("parallel",)))(*arrs)


def adamw(w, g, m, v, name):
    R, C = w.shape
    tr = _tile(R, (256, 128, 64, 32, 16, 8))
    c1 = 1.0 - ADAM_B1 ** ADAM_STEP
    c2 = 1.0 - ADAM_B2 ** ADAM_STEP

    def body(w_ref, g_ref, m_ref, v_ref, d_ref, nm_ref, nv_ref):
        gv = g_ref[...]
        mn = ADAM_B1 * m_ref[...] + (1.0 - ADAM_B1) * gv
        vn = ADAM_B2 * v_ref[...] + (1.0 - ADAM_B2) * (gv * gv)
        nm_ref[...] = mn
        nv_ref[...] = vn
        d_ref[...] = -ADAM_LR * ((mn / c1) / (jnp.sqrt(vn / c2) + ADAM_EPS) + ADAM_WD * w_ref[...])

    xs = pl.BlockSpec((tr, C), lambda i: (i, 0))
    return _pcall(body, name=name, grid=(R // tr,), in_specs=[xs] * 4, out_specs=[xs] * 3,
                  out_shape=[jax.ShapeDtypeStruct((R, C), F32)] * 3, compiler_params=_params(("parallel",)))(w, g, m, v)


def adamw_layer(w, g, m, v, l, prev, name):
    L, r, c = w.shape
    tr = _tile(r, (256, 128, 64, 32, 16, 8))
    c1 = 1.0 - ADAM_B1 ** ADAM_STEP
    c2 = 1.0 - ADAM_B2 ** ADAM_STEP
    chained = prev is not None

    def body(w_ref, g_ref, m_ref, v_ref, *rest):
        go_ref, d_ref, nm_ref, nv_ref = rest[-4:]
        gv = g_ref[...]
        mn = ADAM_B1 * m_ref[...] + (1.0 - ADAM_B1) * gv
        vn = ADAM_B2 * v_ref[...] + (1.0 - ADAM_B2) * (gv * gv)
        go_ref[...] = gv
        nm_ref[...] = mn
        nv_ref[...] = vn
        d_ref[...] = -ADAM_LR * ((mn / c1) / (jnp.sqrt(vn / c2) + ADAM_EPS) + ADAM_WD * w_ref[...])

    ls = pl.BlockSpec((None, tr, c), lambda i: (l, i, 0))
    gs = pl.BlockSpec((tr, c), lambda i: (i, 0))
    anys = [pl.BlockSpec(memory_space=pl.ANY)] * 4 if chained else []
    return _pcall(body, name=name, grid=(r // tr,), in_specs=[ls, gs, ls, ls] + anys, out_specs=[ls] * 4,
                  out_shape=[jax.ShapeDtypeStruct((L, r, c), F32)] * 4,
                  input_output_aliases={4: 0, 5: 1, 6: 2, 7: 3} if chained else {},
                  compiler_params=_params(("parallel",)))(w, g, m, v, *(prev if chained else ()))


def add_pair(g, got, name):
    _, r, c = g.shape
    h = r // 2

    def body(c_ref, g_ref, o_ref, out_ref):
        del c_ref
        out_ref[...] = (g_ref[...].astype(F32) + o_ref[...].astype(F32)).astype(BF16)

    spec = pltpu.PrefetchScalarGridSpec(
        num_scalar_prefetch=1, grid=(4,),
        in_specs=[pl.BlockSpec((None, h, c), lambda k, cs: (k, cs[0], 0)), pl.BlockSpec((None, h, c), lambda k, cs: (k, 0, 0))],
        out_specs=pl.BlockSpec((None, h, c), lambda k, cs: (k, 0, 0)))
    return _pcall(body, name=name, grid_spec=spec, out_shape=jax.ShapeDtypeStruct((4, h, c), BF16),
                  compiler_params=_params(("arbitrary",)))(lax.axis_index("c").reshape(1).astype(jnp.int32), g, got)


def add_chips(p, b, name):
    _, h, c = p.shape
    tr = _tile(h, (256, 128, 64, 32, 16, 8))

    def body(me_ref, p_ref, b0, b1, b2, b3, out_ref):
        me = me_ref[0]
        acc = jnp.zeros(out_ref.shape, F32)
        for k, bk in enumerate((b0, b1, b2, b3)):
            acc = acc + jnp.where(me == k, p_ref[...], bk[...]).astype(F32)
        out_ref[...] = acc

    slot = lambda k: pl.BlockSpec((None, tr, c), lambda i, ms: (jnp.where(ms[0] == k, (k + 1) % 4, k), i, 0))
    spec = pltpu.PrefetchScalarGridSpec(
        num_scalar_prefetch=1, grid=(h // tr,),
        in_specs=[pl.BlockSpec((None, tr, c), lambda i, ms: (ms[0], i, 0)), slot(0), slot(1), slot(2), slot(3)],
        out_specs=pl.BlockSpec((tr, c), lambda i, ms: (i, 0)))
    me = (2 * lax.axis_index("x") + lax.axis_index("y")).reshape(1).astype(jnp.int32)
    return _pcall(body, name=name, grid_spec=spec, out_shape=jax.ShapeDtypeStruct((h, c), F32),
                  compiler_params=_params(("arbitrary",)))(me, p, b, b, b, b)


ANY = pl.BlockSpec(memory_space=pl.ANY)


def _place():
    x, y, c = lax.axis_index("x"), lax.axis_index("y"), lax.axis_index("c")
    chips = [(1 - x, y), (x, 1 - y), (1 - x, 1 - y)]
    return x, y, c, chips


def _rcopy(src, dst, ssem, rsem, dev):
    return pltpu.make_async_remote_copy(src_ref=src, dst_ref=dst, send_sem=ssem, recv_sem=rsem, device_id=dev,
                                        device_id_type=MESH)


def _half(c, h):
    return pl.ds(pl.multiple_of(c * h, 64), h)


def _me_chip():
    return 2 * lax.axis_index("x") + lax.axis_index("y")


def all_gather_weights(shards, name):
    n = len(shards)
    L = shards[0].shape[0]
    hl = L // 2

    def body(*refs):
        srcs, outs, (ssem, rsem) = refs[:n], refs[n:2 * n], refs[2 * n:]
        x, y, c, chips = _place()
        me = 2 * x + y
        sib = (x, y, 1 - c)
        mine, other = pl.ds(c * hl, hl), pl.ds((1 - c) * hl, hl)
        sends = []
        for w in range(n):
            for j, (px, py) in enumerate(chips):
                cp = _rcopy(srcs[w].at[mine], outs[w].at[mine, me], ssem.at[w, j], rsem.at[w, j], (px, py, c))
                cp.start()
                sends.append(cp)
        for j, (px, py) in enumerate(chips):
            slot = 2 * px + py
            for w in range(n):
                _rcopy(srcs[w].at[mine], outs[w].at[mine, slot], ssem.at[w, j], rsem.at[w, j], (px, py, c)).wait_recv()
                cp = _rcopy(outs[w].at[mine, slot], outs[w].at[mine, slot], ssem.at[w, 3 + j], rsem.at[w, 3 + j], sib)
                cp.start()
                sends.append(cp)
        for j, (px, py) in enumerate(chips):
            slot = 2 * px + py
            for w in range(n):
                _rcopy(outs[w].at[other, slot], outs[w].at[other, slot], ssem.at[w, 3 + j], rsem.at[w, 3 + j], sib).wait_recv()
        for cp in sends:
            cp.wait_send()

    outs = _pcall(body, name=name, in_specs=[ANY] * n, out_specs=[ANY] * n,
                  out_shape=[jax.ShapeDtypeStruct((L, 4) + s.shape[1:], s.dtype) for s in shards],
                  scratch_shapes=[pltpu.SemaphoreType.DMA((n, 6)), pltpu.SemaphoreType.DMA((n, 6))])(*shards)
    me = _me_chip()
    return [lax.dynamic_update_slice(o, s[:, None], (0, me, 0, 0)) for o, s in zip(outs, shards)]


HBM = pl.BlockSpec(memory_space=pltpu.HBM)
SEM = pl.BlockSpec(memory_space=pltpu.SEMAPHORE)
EFFECT = pltpu.SideEffectType.DATAFLOW_SIDE_EFFECTING


def _ici_copy(kind, src, land, x, y, c, px, py, ssem, rsem, receiving):
    me, peer = 2 * x + y, 2 * px + py
    if kind == "gather":
        rows = _half(c, land.shape[1] // 2)
        s, d = land.at[me, rows], land.at[peer if receiving else me, rows]
    else:
        s, d = src.at[me if receiving else peer], land.at[peer if receiving else me]
    return _rcopy(s, d, ssem, rsem, (px, py, c))


def ici_start(kind, srcs, lands, name):
    n, ns = len(lands), len(srcs)
    m = 3 * n
    na = ns + n

    def body(*refs):
        src, land, token = (refs[:ns] if ns else [None] * n), refs[ns:na], refs[-1]
        ssem, rsem = refs[na:na + m], refs[na + m:na + 2 * m]
        x, y, c, chips = _place()
        for w in range(n):
            for j, (px, py) in enumerate(chips):
                _ici_copy(kind, src[w], land[w], x, y, c, px, py, ssem[3 * w + j], rsem[3 * w + j], False).start()
        token[...] = jnp.zeros_like(token)

    arrs = [pltpu.with_memory_space_constraint(a, pltpu.HBM) for a in list(srcs) + list(lands)]
    outs = _pcall(
        body, name=name,
        out_shape=(*[pltpu.SemaphoreType.DMA(())] * (2 * m), *[pltpu.HBM(a.shape, a.dtype) for a in arrs],
                   jax.ShapeDtypeStruct((8, LANES), F32)),
        in_specs=[HBM] * na,
        out_specs=(*([SEM] * (2 * m)), *([HBM] * na), pl.BlockSpec(memory_space=pltpu.VMEM)),
        input_output_aliases={i: 2 * m + i for i in range(na)},
        compiler_params=pltpu.CompilerParams(has_side_effects=EFFECT))(*arrs)
    return dict(kind=kind, ssem=list(outs[:m]), rsem=list(outs[m:2 * m]), srcs=list(outs[2 * m:2 * m + ns]),
                lands=list(outs[2 * m + ns:2 * m + na])), outs[-1]


def ici_wait(state, after, name):
    kind, n, ns = state["kind"], len(state["lands"]), len(state["srcs"])
    m = 3 * n
    na = ns + n

    def body(*refs):
        src, land = (refs[:ns] if ns else [None] * n), refs[ns:na]
        ssem, rsem = refs[na:na + m], refs[na + m:na + 2 * m]
        x, y, c, chips = _place()
        for w in range(n):
            for j, (px, py) in enumerate(chips):
                _ici_copy(kind, src[w], land[w], x, y, c, px, py, ssem[3 * w + j], rsem[3 * w + j], False).wait_send()
                _ici_copy(kind, src[w], land[w], x, y, c, px, py, ssem[3 * w + j], rsem[3 * w + j], True).wait_recv()

    outs = _pcall(
        body, name=name,
        out_shape=tuple(pltpu.HBM(a.shape, a.dtype) for a in state["srcs"] + state["lands"]),
        in_specs=[HBM] * na + [SEM] * (2 * m) + [ANY], out_specs=tuple([HBM] * na),
        input_output_aliases={i: i for i in range(na)},
        compiler_params=pltpu.CompilerParams(has_side_effects=EFFECT))(
            *state["srcs"], *state["lands"], *state["ssem"], *state["rsem"], after)
    return list(outs[:ns]), list(outs[ns:])


def forward_pair(lands, name):
    n = len(lands)

    def body(*refs):
        land, (ssem, rsem) = refs[:n], refs[2 * n:]
        x, y, c, chips = _place()
        sib = (x, y, 1 - c)
        cps = []
        for w in range(n):
            h = land[w].shape[1] // 2
            for j, (px, py) in enumerate(chips):
                part = land[w].at[2 * px + py, _half(c, h)]
                cp = _rcopy(part, part, ssem.at[w, j], rsem.at[w, j], sib)
                cp.start()
                cps.append(cp)
        for w in range(n):
            h = land[w].shape[1] // 2
            for j, (px, py) in enumerate(chips):
                part = land[w].at[2 * px + py, _half(1 - c, h)]
                _rcopy(part, part, ssem.at[w, j], rsem.at[w, j], sib).wait_recv()
        for cp in cps:
            cp.wait_send()

    return _pcall(body, name=name, in_specs=[ANY] * n, out_specs=[ANY] * n,
                  out_shape=[jax.ShapeDtypeStruct(z.shape, z.dtype) for z in lands],
                  input_output_aliases={i: i for i in range(n)},
                  scratch_shapes=[pltpu.SemaphoreType.DMA((n, 3)), pltpu.SemaphoreType.DMA((n, 3))])(*lands)


def exchange_pair(gs, name):
    n = len(gs)

    def body(*refs):
        srcs, outs, (ssem, rsem) = refs[:n], refs[n:2 * n], refs[2 * n:]
        x, y, c, _ = _place()
        cps = []
        for w in range(n):
            h = srcs[w].shape[1] // 2
            cp = _rcopy(srcs[w].at[:, _half(1 - c, h)], outs[w], ssem.at[w], rsem.at[w], (x, y, 1 - c))
            cp.start()
            cps.append(cp)
        for cp in cps:
            cp.wait()

    return _pcall(body, name=name, in_specs=[ANY] * n, out_specs=[ANY] * n,
                  out_shape=[jax.ShapeDtypeStruct((4, g.shape[1] // 2, g.shape[2]), g.dtype) for g in gs],
                  scratch_shapes=[pltpu.SemaphoreType.DMA((n,)), pltpu.SemaphoreType.DMA((n,))])(*gs)


def scatter_chips(ps, name):
    n = len(ps)

    def body(*refs):
        srcs, outs, (ssem, rsem) = refs[:n], refs[n:2 * n], refs[2 * n:]
        x, y, c, chips = _place()
        me = 2 * x + y
        sends = []
        for w in range(n):
            for j, (px, py) in enumerate(chips):
                cp = _rcopy(srcs[w].at[2 * px + py], outs[w].at[me], ssem.at[w, j], rsem.at[w, j], (px, py, c))
                cp.start()
                sends.append(cp)
        for w in range(n):
            for j, (px, py) in enumerate(chips):
                _rcopy(srcs[w].at[me], outs[w].at[2 * px + py], ssem.at[w, j], rsem.at[w, j], (px, py, c)).wait_recv()
        for cp in sends:
            cp.wait_send()

    return _pcall(body, name=name, in_specs=[ANY] * n, out_specs=[ANY] * n,
                  out_shape=[jax.ShapeDtypeStruct(p.shape, p.dtype) for p in ps],
                  scratch_shapes=[pltpu.SemaphoreType.DMA((n, 3)), pltpu.SemaphoreType.DMA((n, 3))])(*ps)


def join_pair(rs, name):
    n = len(rs)

    def body(*refs):
        srcs, outs, (ssem, rsem) = refs[:n], refs[n:2 * n], refs[2 * n:]
        x, y, c, _ = _place()
        cps = []
        for w in range(n):
            h = srcs[w].shape[0]
            cp = _rcopy(srcs[w], outs[w].at[_half(c, h)], ssem.at[w], rsem.at[w], (x, y, 1 - c))
            cp.start()
            cps.append(cp)
        for w in range(n):
            h = srcs[w].shape[0]
            _rcopy(srcs[w], outs[w].at[_half(1 - c, h)], ssem.at[w], rsem.at[w], (x, y, 1 - c)).wait_recv()
        for cp in cps:
            cp.wait_send()

    outs = _pcall(body, name=name, in_specs=[ANY] * n, out_specs=[ANY] * n,
                  out_shape=[jax.ShapeDtypeStruct((2 * r.shape[0], r.shape[1]), r.dtype) for r in rs],
                  scratch_shapes=[pltpu.SemaphoreType.DMA((n,)), pltpu.SemaphoreType.DMA((n,))])(*rs)
    c = lax.axis_index("c")
    return [lax.dynamic_update_slice(o, r, (c * r.shape[0], 0)) for o, r in zip(outs, rs)]


def gather_all_devices(v, name):
    R, C = v.shape

    def body(src, out, ssem, rsem, lsem):
        x, y, c, _ = _place()
        me = 4 * x + 2 * y + c
        local = pltpu.make_async_copy(src, out.at[me], lsem)
        local.start()
        sends = []
        for j in range(1, 8):
            dx, dy, dc = (j >> 2) & 1, (j >> 1) & 1, j & 1
            px, py, pc = (1 - x if dx else x), (1 - y if dy else y), (1 - c if dc else c)
            cp = _rcopy(src, out.at[me], ssem.at[j - 1], rsem.at[j - 1], (px, py, pc))
            cp.start()
            sends.append(cp)
        for j in range(1, 8):
            dx, dy, dc = (j >> 2) & 1, (j >> 1) & 1, j & 1
            px, py, pc = (1 - x if dx else x), (1 - y if dy else y), (1 - c if dc else c)
            _rcopy(src, out.at[4 * px + 2 * py + pc], ssem.at[j - 1], rsem.at[j - 1], (px, py, pc)).wait_recv()
        for cp in sends:
            cp.wait_send()
        local.wait()

    return _pcall(body, name=name, in_specs=[ANY], out_specs=ANY, out_shape=jax.ShapeDtypeStruct((8, R, C), v.dtype),
                  scratch_shapes=[pltpu.SemaphoreType.DMA((7,)), pltpu.SemaphoreType.DMA((7,)), pltpu.SemaphoreType.DMA(())])(v)


def reduce_scatter_begin(gw, tag):
    names = list(gw)
    gs = [gw[k] for k in names]
    got = exchange_pair(gs, "rs_pair_" + tag)
    ps = [add_pair(g, o, "rs_add_pair_%s_%s" % (k, tag)) for k, g, o in zip(names, gs, got)]
    state, token = ici_start("scatter", ps, [lax.empty(p.shape, p.dtype) for p in ps], "rs_chips_start_" + tag)
    state["names"] = names
    return state, token


def reduce_scatter_end(state, after, tag):
    names = state["names"]
    ps, bs = ici_wait(state, after, "rs_chips_wait_" + tag)
    rs = [add_chips(p, b, "rs_add_chips_%s_%s" % (k, tag)) for k, p, b in zip(names, ps, bs)]
    return dict(zip(names, join_pair(rs, "rs_join_" + tag)))


GATHERED = ("w_in", "w_q_b", "w_kv_b", "w_out", "w_xq", "w_xk", "w_xv", "w_xo", "w_gate", "w_up", "w_down")
KIND = {"w_q_b": "col", "w_kv_b": "col", "w_out": "row", "w_xq": "row", "w_xk": "row", "w_xv": "row", "w_xo": "col",
        "w_gate": "col", "w_up": "col", "w_down": "row"}


def wq_to_storage(w):
    lead = w.shape[:-1]
    w = w.reshape(lead + (2, MLA_QK))
    rope = jnp.pad(w[..., MLA_NOPE:], [(0, 0)] * (w.ndim - 1) + [(0, LANES - MLA_ROPE)])
    return jnp.concatenate([w[..., :MLA_NOPE].reshape(lead + (2 * LANES,)), rope.reshape(lead + (2 * LANES,))], axis=-1)


def wq_from_storage(g):
    lead = g.shape[:-1]
    nope = g[..., :2 * LANES].reshape(lead + (2, LANES))
    rope = g[..., 2 * LANES:].reshape(lead + (2, LANES))[..., :MLA_ROPE]
    return jnp.concatenate([nope, rope], axis=-1).reshape(lead + (2 * MLA_QK,))


def wcat_from_storage(w_in_l):
    w = w_in_l.transpose(1, 0, 2).reshape(D_MODEL, IN_COLS)
    zc = lambda n: jnp.zeros((D_MODEL, n), w.dtype)
    return jnp.concatenate([w[:, :2560], w[:, 2576:], zc(64), w[:, 2560:2576], zc(112)], axis=1)


def w_in_grad_storage(dwcat):
    g = jnp.concatenate([dwcat[:, :2560], dwcat[:, C_DT:C_DT + 16], dwcat[:, 2560:3648]], axis=1)
    return g.reshape(D_MODEL, 4, IN_COLS // 4).transpose(1, 0, 2)


def _ssd_par(sm):
    rows = jnp.stack([sm["dt_bias"], sm["a_log"], sm["d_skip"]]).reshape(3, SSD_GROUPS, SSD_HPG).transpose(1, 0, 2)
    return jnp.pad(rows, ((0, 0), (0, 5), (0, LANES - SSD_HPG)))


def _dt_pad(proj, B, S):
    dt = proj[:, C_DT:C_DT + 16].reshape(B, S, SSD_GROUPS, SSD_HPG)
    return jnp.pad(dt, ((0, 0), (0, 0), (0, 0), (0, LANES - SSD_HPG))).reshape(B, S, SSD_GROUPS * LANES)


def _mla_gains(g):
    return g[:MLA_NOPE].reshape(1, LANES), jnp.pad(g[MLA_NOPE:], (0, LANES - MLA_ROPE)).reshape(1, LANES)


_Q_NOPE = lambda h: 4 * (h // 2) + h % 2
_Q_ROPE = lambda h: 4 * (h // 2) + 2 + h % 2
_K_NOPE = lambda h: 2 * h
_V_BLK = lambda h: 2 * h + 1


def layer_fwd(x, memf, cos, sin, G, kw, sm, B, S, l, dep=None):
    T = x.shape[0]
    n = lambda s: "%s_l%d" % (s, l)
    wv = lambda k: (KIND[k], 0)
    sv = dict(x0=x)
    h1 = norm_fwd(x, sm["attn_norm_g"], D_MODEL, False, n("attn_norm"), dep=dep)
    proj = mm(h1, kw["wcat"], "nn", n("in_proj"))
    mark = getattr(G, "mark", lambda v: None)
    mark(proj)
    sv.update(h1=h1, proj=proj)
    proj3 = proj.reshape(B, S, CAT_COLS)
    xbc = conv_fwd(proj3, C_XBC // 256, kw["conv_w"], sm["conv_b"].reshape(1, -1), n("conv"))
    dtp = _dt_pad(proj, B, S)
    par = _ssd_par(sm)
    y_raw, states = ssd_fwd(xbc, dtp, par, n("ssd"))
    z = proj[:, C_Z:C_Z + SSD_INNER]
    y_ssd = norm_fwd(y_raw.reshape(T, SSD_INNER), sm["ssd_norm_g"], SSD_INNER // SSD_GROUPS, False, n("ssd_norm"), gate=z)
    sv.update(xbc=xbc, states=states, y_raw=y_raw)
    qn = norm_fwd(proj[:, C_QA:C_QA + Q_LORA], sm["q_a_norm_g"], Q_LORA, False, n("q_a_norm"))
    kvn = norm_fwd(proj[:, C_KVA:C_KVA + KV_LORA], sm["kv_a_norm_g"], KV_LORA, False, n("kv_a_norm"))
    q_raw = mm(qn, G["w_q_b"], "nn", n("q_b"), wv=wv("w_q_b"))
    kv_raw = mm(kvn, G["w_kv_b"], "nn", n("kv_b"), wv=wv("w_kv_b"))
    gqn, gqr = _mla_gains(sm["mla_q_norm_g"])
    gkn, gkr = _mla_gains(sm["mla_k_norm_g"])
    qh = qk_fwd(q_raw, _Q_NOPE, q_raw, _Q_ROPE, False, gqn, gqr, cos, sin, n("q_norm_rope"))
    kh = qk_fwd(kv_raw, _K_NOPE, proj, C_KR // LANES, True, gkn, gkr, cos, sin, n("k_norm_rope"))
    vv = kv_raw.astype(BF16)
    y_mla = attn_fwd(qh.reshape(B, S, -1), kh.reshape(B, S, -1), vv.reshape(B, S, -1), MLA_HEADS, 2 * LANES, MLA_V,
                     MLA_QK ** -0.5, True, n("mla_attn"), v_idx=_V_BLK)
    ycat = jnp.concatenate([y_ssd, y_mla.reshape(T, -1)], axis=1)
    x1 = mm(ycat, G["w_out"], "nn", n("out_proj"), add=x, wv=wv("w_out"))
    sv.update(qn=qn, kvn=kvn, q_raw=q_raw, kv_raw=kv_raw, qh=qh, kh=kh, vv=vv, ycat=ycat, x1=x1)
    h2 = norm_fwd(x1, sm["xattn_norm_g"], D_MODEL, False, n("xattn_norm"))
    mn = norm_fwd(memf, sm["mem_norm_g"], D_MODEL, False, n("mem_norm"))
    xq = mm(h2, G["w_xq"], "nn", n("xq"), wv=wv("w_xq"))
    xk = mm(mn, G["w_xk"], "nn", n("xk"), wv=wv("w_xk"))
    xv = mm(mn, G["w_xv"], "nn", n("xv"), wv=wv("w_xv"), out_dtype=BF16)
    xqn = norm_fwd(xq, sm["xq_norm_g"], X_HEAD_DIM, True, n("xq_norm"))
    xkn = norm_fwd(xk, sm["xk_norm_g"], X_HEAD_DIM, True, n("xk_norm"))
    ML = memf.shape[0] // B
    xo = attn_fwd(xqn.reshape(B, S, -1), xkn.reshape(B, ML, -1), xv.reshape(B, ML, -1), X_HEADS, X_HEAD_DIM, X_HEAD_DIM,
                  X_HEAD_DIM ** -0.5, False, n("x_attn"))
    x2 = mm(xo.reshape(T, -1), G["w_xo"], "nn", n("xo_proj"), add=x1, wv=wv("w_xo"))
    mark(x2)
    sv.update(h2=h2, mn=mn, xq=xq, xk=xk, xqn=xqn, xkn=xkn, xv=xv, xo=xo, x2=x2)
    h3 = norm_fwd(x2, sm["ffn_norm_g"], D_MODEL, False, n("ffn_norm"))
    gate, up, act = gate_up_act(h3, G["w_gate"], G["w_up"], n("gate_up"))
    x3 = mm(act, G["w_down"], "nn", n("down"), add=x2, wv=wv("w_down"))
    sv.update(h3=h3, gate=gate, up=up, act=act)
    return x3, sv


def layer_bwd(dx3, dx3b, sv, memf, cos, sin, G, kw, sm, B, S, l, dep=None, rs_hook=None):
    T = dx3.shape[0]
    n = lambda s: "%s_bwd_l%d" % (s, l)
    wv = lambda k: (KIND[k], 0)
    dw = lambda k, a, b: mm(a, b, "tn", n(k + "_dw"), out_dtype=BF16, out_wv=KIND[k])
    gw, gs = {}, {}
    dgate, dup = down_dx_swiglu(dx3b, G["w_down"][0].reshape(FFN, D_MODEL), sv["gate"], sv["up"], n("down_dx"), dep=dep)
    gw["w_down"] = dw("w_down", sv["act"], dx3b)
    dh3 = gate_up_dx(dgate, dup, G["w_gate"], G["w_up"], n("gate_up_dx"))
    gw["w_gate"] = dw("w_gate", sv["h3"], dgate)
    gw["w_up"] = dw("w_up", sv["h3"], dup)
    dx2, gs["ffn_norm_g"], dx2b = norm_bwd(sv["x2"], sm["ffn_norm_g"], dh3, D_MODEL, False, n("ffn_norm"), add=dx3,
                                           bf16_copy=True)
    ffn_token = rs_hook({k: gw.pop(k) for k in ("w_down", "w_gate", "w_up")}) if rs_hook is not None else None
    dxo = mm(dx2b, G["w_xo"], "nt", n("xo_dx"), wv=wv("w_xo"), dep=ffn_token)
    gw["w_xo"] = dw("w_xo", sv["xo"].reshape(T, -1), dx2b)
    ML = memf.shape[0] // B
    dxqn, dxkn, dxv = attn_bwd(sv["xqn"].reshape(B, S, -1), sv["xkn"].reshape(B, ML, -1), sv["xv"].reshape(B, ML, -1),
                               dxo.reshape(B, S, -1), X_HEADS, X_HEAD_DIM, X_HEAD_DIM, X_HEAD_DIM ** -0.5, False, n("x_attn"))
    dxq, gs["xq_norm_g"] = norm_bwd(sv["xq"], sm["xq_norm_g"], dxqn.reshape(T, -1), X_HEAD_DIM, True, n("xq_norm"))
    dxk, gs["xk_norm_g"] = norm_bwd(sv["xk"], sm["xk_norm_g"], dxkn.reshape(B * ML, -1), X_HEAD_DIM, True, n("xk_norm"))
    dxv = dxv.reshape(B * ML, -1)
    dmn = mm(dxk, G["w_xk"], "nt", n("xk_dx"), wv=wv("w_xk"))
    dmn = mm(dxv, G["w_xv"], "nt", n("xv_dx"), wv=wv("w_xv"), add=dmn)
    gw["w_xk"] = dw("w_xk", sv["mn"], dxk)
    gw["w_xv"] = dw("w_xv", sv["mn"], dxv)
    _, gs["mem_norm_g"] = norm_bwd(memf, sm["mem_norm_g"], dmn, D_MODEL, False, n("mem_norm"))
    dh2 = mm(dxq, G["w_xq"], "nt", n("xq_dx"), wv=wv("w_xq"))
    gw["w_xq"] = dw("w_xq", sv["h2"], dxq)
    dx1, gs["xattn_norm_g"], dx1b = norm_bwd(sv["x1"], sm["xattn_norm_g"], dh2, D_MODEL, False, n("xattn_norm"), add=dx2,
                                             bf16_copy=True)
    dycat = mm(dx1b, G["w_out"], "nt", n("out_dx"), wv=wv("w_out"))
    gw["w_out"] = dw("w_out", sv["ycat"], dx1b)
    dqh, dkh, dvv = attn_bwd(sv["qh"].reshape(B, S, -1), sv["kh"].reshape(B, S, -1), sv["vv"].reshape(B, S, -1),
                             dycat[:, SSD_INNER:].reshape(B, S, -1), MLA_HEADS, 2 * LANES, MLA_V, MLA_QK ** -0.5, True,
                             n("mla_attn"), v_idx=_V_BLK)
    gqn, gqr = _mla_gains(sm["mla_q_norm_g"])
    gkn, gkr = _mla_gains(sm["mla_k_norm_g"])
    dq_raw, dgqn, dgqr = qk_bwd(sv["q_raw"], _Q_NOPE, sv["q_raw"], _Q_ROPE, False, gqn, gqr, cos, sin,
                                dqh.reshape(T, -1), n("q_norm_rope"))
    dkv_raw, dk_rope, dgkn, dgkr = qk_bwd(sv["kv_raw"], _K_NOPE, sv["proj"], C_KR // LANES, True, gkn, gkr, cos, sin,
                                          dkh.reshape(T, -1), n("k_norm_rope"), fill=dvv.reshape(T, -1), fill_blk=_V_BLK)
    gs["mla_q_norm_g"] = jnp.concatenate([dgqn[0], dgqr[0, :MLA_ROPE]])
    gs["mla_k_norm_g"] = jnp.concatenate([dgkn[0], dgkr[0, :MLA_ROPE]])
    dqn = mm(dq_raw, G["w_q_b"], "nt", n("q_b_dx"), wv=wv("w_q_b"))
    gw["w_q_b"] = dw("w_q_b", sv["qn"], dq_raw)
    dkvn = mm(dkv_raw, G["w_kv_b"], "nt", n("kv_b_dx"), wv=wv("w_kv_b"))
    gw["w_kv_b"] = dw("w_kv_b", sv["kvn"], dkv_raw)
    proj = sv["proj"]
    dq_a, gs["q_a_norm_g"] = norm_bwd(proj[:, C_QA:C_QA + Q_LORA], sm["q_a_norm_g"], dqn, Q_LORA, False, n("q_a_norm"))
    dkv_a, gs["kv_a_norm_g"] = norm_bwd(proj[:, C_KVA:C_KVA + KV_LORA], sm["kv_a_norm_g"], dkvn, KV_LORA, False,
                                        n("kv_a_norm"))
    z = proj[:, C_Z:C_Z + SSD_INNER]
    dy_raw, dz, gs["ssd_norm_g"] = norm_bwd(sv["y_raw"].reshape(T, -1), sm["ssd_norm_g"], dycat[:, :SSD_INNER],
                                            SSD_INNER // SSD_GROUPS, False, n("ssd_norm"), gate=z)
    dtp = _dt_pad(proj, B, S)
    par = _ssd_par(sm)
    dxs, dBm, dCm, ddt, dpar = ssd_bwd(sv["xbc"], dtp, par, sv["states"], dy_raw.reshape(B, S, -1), n("ssd"))
    gs["dt_bias"] = dpar[:, 0, :SSD_HPG].reshape(-1)
    gs["a_log"] = dpar[:, 1, :SSD_HPG].reshape(-1)
    gs["d_skip"] = dpar[:, 2, :SSD_HPG].reshape(-1)
    dxbc_act = jnp.concatenate([dxs, dBm, dCm], axis=2)
    dxbc, dconv_w, dconv_b = conv_bwd(proj.reshape(B, S, -1), C_XBC // 256, kw["conv_w"], sm["conv_b"].reshape(1, -1), dxbc_act,
                                      n("conv"))
    gs["conv_w"] = dconv_w[:SSD_CONV]
    gs["conv_b"] = dconv_b[0]
    ddt16 = ddt.reshape(T, SSD_GROUPS, LANES)[:, :, :SSD_HPG].reshape(T, 16)
    dproj = jnp.concatenate([dz, dxbc.reshape(T, -1), dq_a, dkv_a, dk_rope, ddt16, jnp.zeros((T, 112), F32)],
                            axis=1).astype(BF16)
    dh1 = mm(dproj, kw["wcat"], "nt", n("in_dx"))
    gw["w_in"] = w_in_grad_storage(mm(sv["h1"], dproj, "tn", n("in_dw"), out_dtype=BF16))
    dx0, gs["attn_norm_g"], dx0b = norm_bwd(sv["x0"], sm["attn_norm_g"], dh1, D_MODEL, False, n("attn_norm"), add=dx1,
                                            bf16_copy=True)
    gs = {k: v.reshape(-1) for k, v in gs.items()}
    return dx0, dx0b, gw, gs


def _small_pack(vals):
    flat = jnp.concatenate([vals[l][k].reshape(-1) for l in range(DEPTH) for k in SMALL_RS])
    n = flat.shape[0]
    rows = -(-n // LANES)
    rows = -(-rows // 8) * 8
    return jnp.pad(flat, (0, rows * LANES - n)).reshape(rows, LANES), n


def _small_unpack(flat, shapes):
    v = flat.reshape(-1)
    out, off = [dict() for _ in range(DEPTH)], 0
    for l in range(DEPTH):
        for k in SMALL_RS:
            sz = shapes[k]
            out[l][k] = v[off:off + sz]
            off += sz
    return out


def _adam_2d(a):
    if a.ndim == 2:
        return a
    return a.reshape(a.shape[0] * a.shape[1], a.shape[2])


def kernel(x, mem, positions, attn_norm_g, w_in, conv_w, conv_b, dt_bias, a_log, d_skip, ssd_norm_g, q_a_norm_g, w_q_b, kv_a_norm_g, w_kv_b, mla_q_norm_g, mla_k_norm_g, w_out, xattn_norm_g, mem_norm_g, w_xq, w_xk, w_xv, xq_norm_g, xk_norm_g, w_xo, ffn_norm_g, w_gate, w_up, w_down, loss_target, m_attn_norm_g, m_w_in, m_conv_w, m_conv_b, m_dt_bias, m_a_log, m_d_skip, m_ssd_norm_g, m_q_a_norm_g, m_w_q_b, m_kv_a_norm_g, m_w_kv_b, m_mla_q_norm_g, m_mla_k_norm_g, m_w_out, m_xattn_norm_g, m_mem_norm_g, m_w_xq, m_w_xk, m_w_xv, m_xq_norm_g, m_xk_norm_g, m_w_xo, m_ffn_norm_g, m_w_gate, m_w_up, m_w_down, v_attn_norm_g, v_w_in, v_conv_w, v_conv_b, v_dt_bias, v_a_log, v_d_skip, v_ssd_norm_g, v_q_a_norm_g, v_w_q_b, v_kv_a_norm_g, v_w_kv_b, v_mla_q_norm_g, v_mla_k_norm_g, v_w_out, v_xattn_norm_g, v_mem_norm_g, v_w_xq, v_w_xk, v_w_xv, v_xq_norm_g, v_xk_norm_g, v_w_xo, v_ffn_norm_g, v_w_gate, v_w_up, v_w_down):
    loc = locals()
    W = {k: loc[k] for k in WEIGHTS}
    Mo = {k: loc["m_" + k] for k in WEIGHTS}
    Vo = {k: loc["v_" + k] for k in WEIGHTS}
    B, S, D = x.shape
    T = B * S
    xf = x.reshape(T, D)
    memf = mem.reshape(-1, D)
    tf = loss_target.reshape(T, D)
    cos, sin = rope_tables(positions.reshape(T, 1).astype(F32))

    cw = gather_all_devices(conv_w.reshape(DEPTH * SSD_CONV, -1), "gather_conv_w")[0::2]
    cw = cw.transpose(1, 0, 2).reshape(DEPTH, SSD_CONV, SSD_CONV_DIM)
    sms = [{k: W[k][l] for k in SMALL} for l in range(DEPTH)]
    me = _me_chip()

    cw, w_in_first = lax.optimization_barrier((cw, W["w_in"][0]))

    def gather_start(l):
        shards = [(wq_to_storage(W[k][l]) if k == "w_q_b" else W[k][l]).astype(BF16) for k in GATHERED]
        if l == 0:
            shards[0] = w_in_first.astype(BF16)
        lands = [lax.dynamic_update_slice(lax.empty((4,) + s.shape, BF16), s[None], (me, 0, 0)) for s in shards]
        return ici_start("gather", [], lands, "gather_start_l%d" % l)

    def gather_end(state, names, after, tag):
        idx = [GATHERED.index(k) for k in names]
        pick3 = lambda sems: [sems[3 * w + j] for w in idx for j in range(3)]
        sub = dict(kind=state["kind"], srcs=[], lands=[state["lands"][w] for w in idx],
                   ssem=pick3(state["ssem"]), rsem=pick3(state["rsem"]))
        _, lands = ici_wait(sub, after, "gather_wait_" + tag)
        return {k: z[None] for k, z in zip(names, forward_pair(lands, "gather_forward_" + tag))}

    class Weights(dict):
        def __init__(self, state, groups, after, tag):
            super().__init__(gather_end(state, groups[0], after, tag + "a"))
            self.state, self.later, self.after, self.tag = state, list(groups[1:]), after, tag

        def mark(self, value):
            self.after = value

        def __missing__(self, k):
            names = self.later.pop(0)
            self.update(gather_end(self.state, names, self.after, self.tag + "bcdef"[len(self.later)]))
            return self[k]

    FIRST_GROUPS = (("w_in",), ("w_q_b", "w_kv_b", "w_out", "w_xq", "w_xk", "w_xv", "w_xo"), ("w_gate", "w_up", "w_down"))

    saved, Gs, kws = [], [], []
    h = xf
    state, token = gather_start(0)
    for l in range(DEPTH):
        Gl = Weights(state, FIRST_GROUPS if l == 0 else (GATHERED,), h, "l%d" % l)
        kwl = dict(wcat=wcat_from_storage(Gl["w_in"][0]), conv_w=jnp.pad(cw[l], ((0, 8 - SSD_CONV), (0, 0))))
        Gs.append(Gl)
        kws.append(kwl)
        if l + 1 < DEPTH:
            state, token = gather_start(l + 1)
        h, sv = layer_fwd(h, memf, cos, sin, Gl, kwl, sms[l], B, S, l, dep=token if l + 1 < DEPTH else None)
        saved.append(sv)
    dh, lpart, dhb = loss_head(h, tf)
    loss = lax.psum(lpart[0, 0], ("x", "y", "c"))

    grads, delta, new_m, new_v = {}, {}, {}, {}
    chain = {k: None for k in GATHERED}
    gsmall = [None] * DEPTH
    pending = [[] for _ in range(DEPTH)]

    def begin(gw, l, tag):
        rs_state, tok = reduce_scatter_begin(gw, "l%d%s" % (l, tag))
        pending[l].append((rs_state, "l%d%s" % (l, tag)))
        return tok

    def finish(l, after):
        for rs_state, tag in pending[l]:
            red = reduce_scatter_end(rs_state, after, tag)
            for k, g in red.items():
                g = wq_from_storage(g) if k == "w_q_b" else g
                chain[k] = adamw_layer(W[k], g, Mo[k], Vo[k], l, chain[k], "adamw_%s_l%d" % (k, l))

    token = None
    for l in reversed(range(DEPTH)):
        dh, dhb, gw, gs = layer_bwd(dh, dhb, saved[l], memf, cos, sin, Gs[l], kws[l], sms[l], B, S, l, dep=token,
                                    rs_hook=functools.partial(begin, l=l, tag="_ffn"))
        saved[l] = None
        gsmall[l] = gs
        token = begin(gw, l, "_mix")
        if l + 1 < DEPTH:
            finish(l + 1, token)
    grad_x = dh.reshape(B, S, D)

    spack, _ = _small_pack(gsmall)
    allv = gather_all_devices(spack, "gather_small_grads")
    ssum = sum_arrays([allv[i] for i in range(8)], F32, "sum_small_grads")
    sizes = {k: int(np.prod(W[k].shape[1:])) for k in SMALL}
    sizes["conv_w"] = SSD_CONV * SSD_CONV_DIM
    gsm = _small_unpack(ssum, sizes)
    for k in SMALL_RS:
        g = jnp.stack([gsm[l][k] for l in range(DEPTH)])
        if k == "conv_w":
            cs = SSD_CONV_DIM // 4
            g = lax.dynamic_slice_in_dim(g.reshape(DEPTH, SSD_CONV, SSD_CONV_DIM), _me_chip() * cs, cs, axis=2)
        shp = W[k].shape
        grads[k] = g.reshape(shp)
        d, nm, nv = adamw(_adam_2d(W[k]), _adam_2d(grads[k]), _adam_2d(Mo[k]), _adam_2d(Vo[k]), "adamw_" + k)
        delta[k], new_m[k], new_v[k] = d.reshape(shp), nm.reshape(shp), nv.reshape(shp)

    finish(0, lax.optimization_barrier((delta["ffn_norm_g"], chain["w_down"][1], chain["w_gate"][1], chain["w_up"][1]))[0])
    for k in GATHERED:
        grads[k], delta[k], new_m[k], new_v[k] = chain[k]

    return (loss, grad_x, *[grads[k] for k in WEIGHTS], *[delta[k] for k in WEIGHTS], *[new_m[k] for k in WEIGHTS],
            *[new_v[k] for k in WEIGHTS])
```

```python
import functools
import math

import numpy as np
import jax
import jax.numpy as jnp
from jax import lax
from jax.experimental import pallas as pl
from jax.experimental.pallas import tpu as pltpu

F32 = jnp.float32
BF16 = jnp.bfloat16
MESH = pl.DeviceIdType.MESH
HI = lax.Precision.HIGHEST

D_MODEL = 2048
DEPTH = 4
SSD_INNER = 1024
SSD_HEAD_DIM = 64
SSD_GROUPS = 2
SSD_HPG = 8
SSD_STATE = 128
SSD_CONV = 4
SSD_CHUNK = 128
SSD_CONV_DIM = 1536
SSD_GW = SSD_HPG * SSD_HEAD_DIM
MLA_HEADS = 8
MLA_NOPE = 128
MLA_ROPE = 64
MLA_QK = 192
MLA_V = 128
Q_LORA = 512
KV_LORA = 512
ROPE_THETA = 10000.0
X_HEADS = 4
X_HEAD_DIM = 128
X_INNER = 512
FFN = 5632
IN_COLS = 3664
EPS = 1e-6
ADAM_LR, ADAM_B1, ADAM_B2, ADAM_EPS, ADAM_WD, ADAM_STEP = 0.001, 0.9, 0.999, 1e-08, 0.01, 10

C_Z, C_XBC, C_QA, C_KVA, C_KR, C_DT, CAT_COLS = 0, 1024, 2560, 3072, 3584, 3712, 3840

LANES = 128
VMEM_LIMIT = 56 * 1024 * 1024
MM_VMEM_BUDGET = 40 * 1024 * 1024
MM_STEP_BYTES = 1024 * 1024
NORM_BLOCK_ELEMS = 512 * 1024
ATTN_SCORE_ELEMS = 1024 * 1024

SMALL = ("attn_norm_g", "conv_b", "dt_bias", "a_log", "d_skip", "ssd_norm_g", "q_a_norm_g", "kv_a_norm_g",
         "mla_q_norm_g", "mla_k_norm_g", "xattn_norm_g", "mem_norm_g", "xq_norm_g", "xk_norm_g", "ffn_norm_g")
SMALL_RS = SMALL + ("conv_w",)
WEIGHTS = ("attn_norm_g", "w_in", "conv_w", "conv_b", "dt_bias", "a_log", "d_skip", "ssd_norm_g", "q_a_norm_g", "w_q_b",
           "kv_a_norm_g", "w_kv_b", "mla_q_norm_g", "mla_k_norm_g", "w_out", "xattn_norm_g", "mem_norm_g", "w_xq", "w_xk",
           "w_xv", "xq_norm_g", "xk_norm_g", "w_xo", "ffn_norm_g", "w_gate", "w_up", "w_down")


def _pcall(body, **kw):
    return pl.pallas_call(body, **kw)


def _params(sem):
    return pltpu.CompilerParams(dimension_semantics=sem, vmem_limit_bytes=VMEM_LIMIT)


def _tile(n, cands):
    for t in cands:
        if n % t == 0:
            return t
    return n


def _sigmoid(z):
    return 1.0 / (1.0 + jnp.exp(-z))


def _mtile(n, cap):
    if n % LANES:
        return n
    q = n // LANES
    t = LANES * max(d for d in range(1, q + 1) if q % d == 0 and LANES * d <= cap)
    return n if (t < 512 and n <= 2048) else t


def _divisor_tiles(n, cap):
    if n % LANES:
        return [n]
    q = n // LANES
    return [LANES * d for d in range(1, q + 1) if q % d == 0 and LANES * d <= cap] or [n]


def _mm_tiles(M, N, K, m_unit, n_unit, k_unit, a_item, b_item, o_item, has_add):
    best = None
    for tm in _divisor_tiles(m_unit, 2048):
        for tn in _divisor_tiles(n_unit, 2048):
            for tk in _divisor_tiles(k_unit, 2048):
                nk = K // tk
                vmem = 2 * (tm * tk * a_item + tk * tn * b_item + tm * tn * (o_item + (4 if has_add else 0)))
                vmem += tm * tn * 4 * (2 if nk > 1 else 1) + (tm * tk * 2 if a_item == 4 else 0) + (tk * tn * 2 if b_item == 4 else 0)
                if vmem > MM_VMEM_BUDGET:
                    continue
                steps = (M // tm) * (N // tn) * nk
                traffic = (M * K * a_item * (N // tn if nk > 1 else 1) + K * N * b_item * (M // tm)
                           + M * N * (o_item + (4 if has_add else 0)) + steps * MM_STEP_BYTES)
                if nk > 1:
                    traffic += steps * tm * tn * 4
                if best is None or traffic < best[0]:
                    best = (traffic, tm, tn, tk)
    assert best is not None, (M, N, K)
    return best[1:]


def mm(a, b, mode, name, add=None, out_dtype=F32, wv=None, out_wv=None, dep=None):
    kind, l = wv if wv is not None else (None, None)
    if kind == "row":
        b, kind = b[l].reshape(-1, b.shape[-1]), None
    if out_wv == "row":
        res = mm(a, b, mode, name, add=add, out_dtype=out_dtype, wv=wv if kind else None, dep=dep)
        return res.reshape(4, res.shape[0] // 4, res.shape[1])
    if kind == "col":
        _, _, r, c = b.shape
        bshape = (r, 4 * c)
    else:
        bshape = b.shape
    if mode == "nn":
        (M, K), (K2, N) = a.shape, bshape
    elif mode == "nt":
        (M, K), (N, K2) = a.shape, bshape
    else:
        (K, M), (K2, N) = a.shape, bshape
    assert K == K2, (a.shape, bshape, mode)
    m_unit = M
    n_unit = N // 4 if (out_wv == "col" or (kind == "col" and mode == "nn")) else N
    k_unit = K // 4 if (kind == "col" and mode == "nt") else K
    tm, tn, tk = _mm_tiles(M, N, K, m_unit, n_unit, k_unit, a.dtype.itemsize, b.dtype.itemsize,
                           jnp.dtype(out_dtype).itemsize, add is not None)
    nk = K // tk
    qm, qn, qk = m_unit // tm, n_unit // tn, k_unit // tk
    dn = {"nn": (((1,), (0,)), ((), ())), "nt": (((1,), (1,)), ((), ())), "tn": (((0,), (0,)), ((), ()))}[mode]
    has_add = add is not None

    def body(a_ref, b_ref, *rest):
        rest = list(rest)
        c_ref = rest.pop(0) if has_add else None
        if dep is not None:
            rest.pop(0)
        o_ref = rest.pop(0)
        d = lax.dot_general(a_ref[...].astype(BF16), b_ref[...].astype(BF16), dn, preferred_element_type=F32)

        def finish(res):
            if has_add:
                res = res + c_ref[...].astype(F32)
            o_ref[...] = res.astype(out_dtype)

        if nk == 1:
            finish(d)
            return
        acc = rest.pop(0)
        k = pl.program_id(2)

        @pl.when(k == 0)
        def _():
            acc[...] = d

        if nk > 2:
            @pl.when((k > 0) & (k < nk - 1))
            def _():
                acc[...] += d

        @pl.when(k == nk - 1)
        def _():
            finish(acc[...] + d)

    if mode == "tn":
        a_spec = pl.BlockSpec((tk, tm), lambda i, j, k: (k, i))
    else:
        a_spec = pl.BlockSpec((tm, tk), lambda i, j, k: (i, k))
    if kind is None:
        b_spec = (pl.BlockSpec((tn, tk), lambda i, j, k: (j, k)) if mode == "nt"
                  else pl.BlockSpec((tk, tn), lambda i, j, k: (k, j)))
    elif mode == "nn":
        b_spec = pl.BlockSpec((None, None, tk, tn), lambda i, j, k: (l, j // qn, k, j % qn))
    elif mode == "nt":
        b_spec = pl.BlockSpec((None, None, tn, tk), lambda i, j, k: (l, k // qk, j, k % qk))
    else:
        raise ValueError((mode, kind))
    if out_wv == "col":
        o_spec = pl.BlockSpec((None, tm, tn), lambda i, j, k: (j // qn, i, j % qn))
        out_shape = jax.ShapeDtypeStruct((4, M, N // 4), out_dtype)
    else:
        o_spec = pl.BlockSpec((tm, tn), lambda i, j, k: (i, j))
        out_shape = jax.ShapeDtypeStruct((M, N), out_dtype)
    in_specs = [a_spec, b_spec] + ([o_spec] if has_add else []) + ([pl.BlockSpec(memory_space=pl.ANY)] if dep is not None else [])
    args = (a, b) + ((add,) if has_add else ()) + ((dep,) if dep is not None else ())
    return _pcall(
        body, name=name, grid=(M // tm, N // tn, nk), in_specs=in_specs, out_specs=o_spec,
        out_shape=out_shape, scratch_shapes=[pltpu.VMEM((tm, tn), F32)] if nk > 1 else [],
        compiler_params=_params(("parallel", "parallel", "arbitrary")))(*args)


def norm_fwd(x, gain, W, shared, name, gate=None, out_dtype=BF16, dep=None):
    T, C = x.shape
    nb = C // W
    tq = _tile(T, tuple(t for t in (4096, 2048, 1024, 512, 256, 128) if t * W <= NORM_BLOCK_ELEMS))
    g2 = gain.reshape(1, -1).astype(F32)
    gated = gate is not None

    def body(*refs):
        o_ref = refs[-1]
        if gated:
            x_ref, z_ref, g_ref = refs[:3]
        else:
            x_ref, g_ref = refs[:2]
        u = x_ref[...].astype(F32)
        if gated:
            z = z_ref[...]
            u = u * (z * _sigmoid(z))
        ms = jnp.mean(u * u, axis=-1, keepdims=True)
        o_ref[...] = (u * lax.rsqrt(ms + EPS) * g_ref[...]).astype(out_dtype)

    xs = pl.BlockSpec((tq, W), lambda j, i: (i, j))
    gs = pl.BlockSpec((1, W), (lambda j, i: (0, 0)) if shared else (lambda j, i: (0, j)))
    in_specs = [xs] + ([xs] if gated else []) + [gs] + ([pl.BlockSpec(memory_space=pl.ANY)] if dep is not None else [])
    args = (x,) + ((gate,) if gated else ()) + (g2,) + ((dep,) if dep is not None else ())
    return _pcall(body, name=name, grid=(nb, T // tq), in_specs=in_specs, out_specs=xs,
                  out_shape=jax.ShapeDtypeStruct((T, C), out_dtype),
                  compiler_params=_params(("parallel", "parallel")))(*args)


def norm_bwd(x, gain, dy, W, shared, name, gate=None, add=None, bf16_copy=False):
    T, C = x.shape
    nb = C // W
    tq = _tile(T, tuple(t for t in (4096, 2048, 1024, 512, 256, 128) if t * W <= NORM_BLOCK_ELEMS))
    g2 = gain.reshape(1, -1).astype(F32)
    gated = gate is not None
    has_add = add is not None
    GC = W if shared else C

    def body(*refs):
        refs = list(refs)
        x_ref = refs.pop(0)
        z_ref = refs.pop(0) if gated else None
        g_ref = refs.pop(0)
        dy_ref = refs.pop(0)
        a_ref = refs.pop(0) if has_add else None
        dx_ref = refs.pop(0)
        dz_ref = refs.pop(0) if gated else None
        dg_ref = refs.pop(0)
        j, i = pl.program_id(0), pl.program_id(1)
        first = (i == 0) & (j == 0) if shared else (i == 0)

        @pl.when(first)
        def _():
            dg_ref[...] = jnp.zeros_like(dg_ref)

        xv = x_ref[...].astype(F32)
        if gated:
            z = z_ref[...]
            s = _sigmoid(z)
            u = xv * (z * s)
        else:
            u = xv
        rstd = lax.rsqrt(jnp.mean(u * u, axis=-1, keepdims=True) + EPS)
        uh = u * rstd
        dyv = dy_ref[...].astype(F32)
        dg_ref[...] += jnp.sum(dyv * uh, axis=0, keepdims=True)
        dyg = dyv * g_ref[...]
        du = rstd * (dyg - uh * jnp.mean(dyg * uh, axis=-1, keepdims=True))
        if gated:
            dx = du * (z * s)
            dz_ref[...] = du * xv * (s * (1.0 + z * (1.0 - s)))
        else:
            dx = du
        if has_add:
            dx = dx + a_ref[...]
        dx_ref[...] = dx
        if bf16_copy:
            refs.pop(0)[...] = dx.astype(BF16)

    xs = pl.BlockSpec((tq, W), lambda j, i: (i, j))
    gs = pl.BlockSpec((1, W), (lambda j, i: (0, 0)) if shared else (lambda j, i: (0, j)))
    in_specs = [xs] + ([xs] if gated else []) + [gs, xs] + ([xs] if has_add else [])
    args = (x,) + ((gate,) if gated else ()) + (g2, dy) + ((add,) if has_add else ())
    out_specs = [xs] + ([xs] if gated else []) + [gs] + ([xs] if bf16_copy else [])
    out_shape = [jax.ShapeDtypeStruct((T, C), F32)] + ([jax.ShapeDtypeStruct((T, C), F32)] if gated else []) \
        + [jax.ShapeDtypeStruct((1, GC), F32)] + ([jax.ShapeDtypeStruct((T, C), BF16)] if bf16_copy else [])
    return _pcall(body, name=name, grid=(nb, T // tq), in_specs=in_specs, out_specs=out_specs, out_shape=out_shape,
                  compiler_params=_params(("arbitrary", "arbitrary")))(*args)


def _inv_freq_row():
    inv = 1.0 / (ROPE_THETA ** (np.arange(0, MLA_ROPE, 2, dtype=np.float32) / MLA_ROPE))
    row = np.zeros((1, LANES), np.float32)
    row[0, :32] = inv
    row[0, 32:64] = inv
    return jnp.asarray(row)


def rope_tables(pos):
    T = pos.shape[0]
    tq = _tile(T, (512, 256, 128))

    def body(p_ref, f_ref, c_ref, s_ref):
        ang = p_ref[...] * f_ref[...]
        lane = lax.broadcasted_iota(jnp.int32, ang.shape, 1)
        c_ref[...] = jnp.where(lane < 64, jnp.cos(ang), 0.0)
        sn = jnp.sin(ang)
        s_ref[...] = jnp.where(lane < 32, -sn, jnp.where(lane < 64, sn, 0.0))

    ts = pl.BlockSpec((tq, LANES), lambda i: (i, 0))
    return _pcall(body, name="rope_tables", grid=(T // tq,),
                  in_specs=[pl.BlockSpec((tq, 1), lambda i: (i, 0)), pl.BlockSpec((1, LANES), lambda i: (0, 0))],
                  out_specs=[ts, ts], out_shape=[jax.ShapeDtypeStruct((T, LANES), F32)] * 2,
                  compiler_params=_params(("parallel",)))(pos, _inv_freq_row())


def _swap_halves(r):
    lane = lax.broadcasted_iota(jnp.int32, r.shape, 1)
    return jnp.where(lane < 32, pltpu.roll(r, 96, 1), pltpu.roll(r, 32, 1))


def qk_fwd(nope_arr, nope_blk0, rope_arr, rope_blk0, rope_shared, gn, gr, cos, sin, name):
    T = nope_arr.shape[0]
    tq = _tile(T, (256, 128))

    col = lambda blk: slice(blk * LANES, (blk + 1) * LANES)

    def body(n_ref, r_ref, gn_ref, gr_ref, c_ref, s_ref, o_ref):
        for h in range(MLA_HEADS):
            n = n_ref[:, col(nope_blk0(h))]
            r = r_ref[...] if rope_shared else n_ref[:, col(rope_blk0(h))]
            ms = (jnp.sum(n * n, axis=-1, keepdims=True) + jnp.sum(r * r, axis=-1, keepdims=True)) * (1.0 / MLA_QK)
            rstd = lax.rsqrt(ms + EPS)
            rn = r * rstd * gr_ref[...]
            rr = rn * c_ref[...] + _swap_halves(rn) * s_ref[...]
            o_ref[:, col(2 * h)] = (n * rstd * gn_ref[...]).astype(BF16)
            o_ref[:, col(2 * h + 1)] = rr.astype(BF16)

    wide = lambda arr: pl.BlockSpec((tq, arr.shape[1]), lambda i: (i, 0))
    assert rope_shared or rope_arr is nope_arr
    rspec = pl.BlockSpec((tq, LANES), lambda i: (i, rope_blk0)) if rope_shared else pl.BlockSpec((8, LANES), lambda i: (0, 0))
    row = pl.BlockSpec((1, LANES), lambda i: (0, 0))
    tab = pl.BlockSpec((tq, LANES), lambda i: (i, 0))
    return _pcall(body, name=name, grid=(T // tq,),
                  in_specs=[wide(nope_arr), rspec, row, row, tab, tab],
                  out_specs=pl.BlockSpec((tq, MLA_HEADS * 2 * LANES), lambda i: (i, 0)),
                  out_shape=jax.ShapeDtypeStruct((T, MLA_HEADS * 2 * LANES), BF16),
                  compiler_params=_params(("parallel",)))(nope_arr, rope_arr, gn, gr, cos, sin)


def qk_bwd(nope_arr, nope_blk0, rope_arr, rope_blk0, rope_shared, gn, gr, cos, sin, dout, name, fill=None, fill_blk=None):
    T = nope_arr.shape[0]
    tq = _tile(T, (256, 128))
    col = lambda blk: slice(blk * LANES, (blk + 1) * LANES)

    def body(*refs):
        refs = list(refs)
        n_ref, r_ref, gn_ref, gr_ref, c_ref, s_ref, d_ref = refs[:7]
        f_ref = refs[7] if rope_shared else None
        outs = refs[8:] if rope_shared else refs[7:]
        draw_ref = outs[0]
        dsh_ref = outs[1] if rope_shared else None
        dgn_ref, dgr_ref = outs[-2:]

        @pl.when(pl.program_id(0) == 0)
        def _():
            dgn_ref[...] = jnp.zeros_like(dgn_ref)
            dgr_ref[...] = jnp.zeros_like(dgr_ref)

        dgn = jnp.zeros((1, LANES), F32)
        dgr = jnp.zeros((1, LANES), F32)
        dshared = jnp.zeros((tq, LANES), F32)
        for h in range(MLA_HEADS):
            n = n_ref[:, col(nope_blk0(h))]
            r = r_ref[...] if rope_shared else n_ref[:, col(rope_blk0(h))]
            ms = (jnp.sum(n * n, axis=-1, keepdims=True) + jnp.sum(r * r, axis=-1, keepdims=True)) * (1.0 / MLA_QK)
            rstd = lax.rsqrt(ms + EPS)
            nh = n * rstd
            rh = r * rstd
            d_n = d_ref[:, col(2 * h)].astype(F32)
            d_rr = d_ref[:, col(2 * h + 1)].astype(F32)
            d_rn = d_rr * c_ref[...] - _swap_halves(d_rr) * s_ref[...]
            dgn = dgn + jnp.sum(d_n * nh, axis=0, keepdims=True)
            dgr = dgr + jnp.sum(d_rn * rh, axis=0, keepdims=True)
            dng = d_n * gn_ref[...]
            drg = d_rn * gr_ref[...]
            mean = (jnp.sum(dng * nh, axis=-1, keepdims=True) + jnp.sum(drg * rh, axis=-1, keepdims=True)) * (1.0 / MLA_QK)
            draw_ref[:, col(nope_blk0(h))] = (rstd * (dng - nh * mean)).astype(BF16)
            drope = rstd * (drg - rh * mean)
            if rope_shared:
                dshared = dshared + drope
                draw_ref[:, col(fill_blk(h))] = f_ref[:, col(h)].astype(BF16)
            else:
                draw_ref[:, col(rope_blk0(h))] = drope.astype(BF16)
        dgn_ref[...] += dgn
        dgr_ref[...] += dgr
        if rope_shared:
            dsh_ref[...] = dshared

    wide = lambda arr: pl.BlockSpec((tq, arr.shape[1]), lambda i: (i, 0))
    assert rope_shared or rope_arr is nope_arr
    rspec = pl.BlockSpec((tq, LANES), lambda i: (i, rope_blk0)) if rope_shared else pl.BlockSpec((8, LANES), lambda i: (0, 0))
    row = pl.BlockSpec((1, LANES), lambda i: (0, 0))
    tab = pl.BlockSpec((tq, LANES), lambda i: (i, 0))
    raw = pl.BlockSpec((tq, 2 * MLA_HEADS * LANES), lambda i: (i, 0))
    in_specs = [wide(nope_arr), rspec, row, row, tab, tab, wide(dout)] + ([wide(fill)] if rope_shared else [])
    out_specs = [raw] + ([tab] if rope_shared else []) + [row, row]
    out_shape = ([jax.ShapeDtypeStruct((T, 2 * MLA_HEADS * LANES), BF16)]
                 + ([jax.ShapeDtypeStruct((T, LANES), F32)] if rope_shared else [])
                 + [jax.ShapeDtypeStruct((1, LANES), F32)] * 2)
    args = (nope_arr, rope_arr, gn, gr, cos, sin, dout) + ((fill,) if rope_shared else ())
    return _pcall(body, name=name, grid=(T // tq,), in_specs=in_specs, out_specs=out_specs, out_shape=out_shape,
                  compiler_params=_params(("arbitrary",)))(*args)


def _scores(q, k, scale, causal, row0):
    s = lax.dot_general(q, k, (((1,), (1,)), ((), ())), preferred_element_type=F32) * scale
    if causal:
        qi = row0 + lax.broadcasted_iota(jnp.int32, s.shape, 0)
        ki = lax.broadcasted_iota(jnp.int32, s.shape, 1)
        s = jnp.where(qi >= ki, s, -1e30)
    return s


def attn_fwd(q, k, v, H, dqk, dv, scale, causal, name, v_idx=lambda h: h):
    B, S, _ = q.shape
    Sk = k.shape[1]
    tq = _tile(S, tuple(t for t in (2048, 1024, 512, 256, 128) if t * Sk <= ATTN_SCORE_ELEMS))

    def tile(q_ref, k_ref, v_ref, o_ref, row0, kl):
        s = _scores(q_ref[0], k_ref[0, :kl], scale, causal, row0)
        p = jnp.exp(s - jnp.max(s, axis=-1, keepdims=True))
        l = jnp.sum(p, axis=-1, keepdims=True)
        o = lax.dot_general(p.astype(BF16), v_ref[0, :kl], (((1,), (0,)), ((), ())), preferred_element_type=F32)
        o_ref[0] = (o / l).astype(BF16)

    def body(q_ref, k_ref, v_ref, o_ref):
        if not causal:
            tile(q_ref, k_ref, v_ref, o_ref, 0, Sk)
            return
        for qi in range(S // tq):
            @pl.when(pl.program_id(2) == qi)
            def _(qi=qi):
                tile(q_ref, k_ref, v_ref, o_ref, qi * tq, (qi + 1) * tq)

    return _pcall(body, name=name, grid=(B, H, S // tq),
                  in_specs=[pl.BlockSpec((1, tq, dqk), lambda b, h, i: (b, i, h)),
                            pl.BlockSpec((1, Sk, dqk), lambda b, h, i: (b, 0, h)),
                            pl.BlockSpec((1, Sk, dv), lambda b, h, i: (b, 0, v_idx(h)))],
                  out_specs=pl.BlockSpec((1, tq, dv), lambda b, h, i: (b, i, h)),
                  out_shape=jax.ShapeDtypeStruct((B, S, H * dv), BF16),
                  compiler_params=_params(("parallel", "parallel", "parallel")))(q, k, v)


def attn_bwd(q, k, v, do, H, dqk, dv, scale, causal, name, v_idx=lambda h: h):
    B, S, _ = q.shape
    Sk = k.shape[1]
    tq = _tile(S, tuple(t for t in (2048, 1024, 512, 256, 128) if t * Sk <= ATTN_SCORE_ELEMS))

    def body(q_ref, k_ref, v_ref, do_ref, dq_ref, dk_ref, dv_ref):
        i = pl.program_id(2)

        @pl.when(i == 0)
        def _():
            dk_ref[...] = jnp.zeros_like(dk_ref)
            dv_ref[...] = jnp.zeros_like(dv_ref)

        def tile(row0, kl):
            qv, kv, vv = q_ref[0], k_ref[0, :kl], v_ref[0, :kl]
            s = _scores(qv, kv, scale, causal, row0)
            p = jnp.exp(s - jnp.max(s, axis=-1, keepdims=True))
            p = p / jnp.sum(p, axis=-1, keepdims=True)
            dob = do_ref[0].astype(BF16)
            pb = p.astype(BF16)
            dv_ref[0, :kl] += lax.dot_general(pb, dob, (((0,), (0,)), ((), ())), preferred_element_type=F32)
            dp = lax.dot_general(dob, vv, (((1,), (1,)), ((), ())), preferred_element_type=F32)
            delta = jnp.sum(p * dp, axis=-1, keepdims=True)
            ds = (p * (dp - delta) * scale).astype(BF16)
            dq_ref[0] = lax.dot_general(ds, kv, (((1,), (0,)), ((), ())), preferred_element_type=F32)
            dk_ref[0, :kl] += lax.dot_general(ds, qv, (((0,), (0,)), ((), ())), preferred_element_type=F32)

        if not causal:
            tile(0, Sk)
            return
        for qi in range(S // tq):
            @pl.when(i == qi)
            def _(qi=qi):
                tile(qi * tq, (qi + 1) * tq)

    qs = pl.BlockSpec((1, tq, dqk), lambda b, h, i: (b, i, h))
    ks = pl.BlockSpec((1, Sk, dqk), lambda b, h, i: (b, 0, h))
    vs = pl.BlockSpec((1, Sk, dv), lambda b, h, i: (b, 0, h))
    vin = pl.BlockSpec((1, Sk, dv), lambda b, h, i: (b, 0, v_idx(h)))
    return _pcall(body, name=name, grid=(B, H, S // tq),
                  in_specs=[qs, ks, vin, pl.BlockSpec((1, tq, dv), lambda b, h, i: (b, i, h))],
                  out_specs=[qs, ks, vs],
                  out_shape=[jax.ShapeDtypeStruct((B, S, H * dqk), F32), jax.ShapeDtypeStruct((B, Sk, H * dqk), F32),
                             jax.ShapeDtypeStruct((B, Sk, H * dv), F32)],
                  compiler_params=_params(("parallel", "parallel", "arbitrary")))(q, k, v, do)


def _shift_down(u, j):
    if j == 0:
        return u
    row = lax.broadcasted_iota(jnp.int32, u.shape, 0)
    return jnp.where(row >= j, pltpu.roll(u, j, 0), 0.0)


def _shift_up(u, j):
    if j == 0:
        return u
    n = u.shape[0]
    row = lax.broadcasted_iota(jnp.int32, u.shape, 0)
    return jnp.where(row < n - j, pltpu.roll(u, n - j, 0), 0.0)


def conv_fwd(u, col0_blk, w8, b, name):
    B, S, _ = u.shape
    tc = 256
    nb = SSD_CONV_DIM // tc

    def body(u_ref, w_ref, b_ref, o_ref):
        uv = u_ref[0]
        pre = b_ref[...] + jnp.zeros_like(uv)
        for kk in range(SSD_CONV):
            pre = pre + w_ref[kk:kk + 1, :] * _shift_down(uv, SSD_CONV - 1 - kk)
        o_ref[0] = pre * _sigmoid(pre)

    return _pcall(body, name=name, grid=(B, nb),
                  in_specs=[pl.BlockSpec((1, S, tc), lambda bb, j: (bb, 0, col0_blk + j)),
                            pl.BlockSpec((8, tc), lambda bb, j: (0, j)), pl.BlockSpec((1, tc), lambda bb, j: (0, j))],
                  out_specs=pl.BlockSpec((1, S, tc), lambda bb, j: (bb, 0, j)),
                  out_shape=jax.ShapeDtypeStruct((B, S, SSD_CONV_DIM), F32),
                  compiler_params=_params(("parallel", "parallel")))(u, w8, b)


def conv_bwd(u, col0_blk, w8, b, dy, name):
    B, S, _ = u.shape
    tc = 256
    nb = SSD_CONV_DIM // tc

    def body(u_ref, w_ref, b_ref, dy_ref, du_ref, dw_ref, db_ref):
        bb = pl.program_id(1)

        @pl.when(bb == 0)
        def _():
            dw_ref[...] = jnp.zeros_like(dw_ref)
            db_ref[...] = jnp.zeros_like(db_ref)

        uv = u_ref[0]
        pre = b_ref[...] + jnp.zeros_like(uv)
        for kk in range(SSD_CONV):
            pre = pre + w_ref[kk:kk + 1, :] * _shift_down(uv, SSD_CONV - 1 - kk)
        s = _sigmoid(pre)
        dpre = dy_ref[0] * (s * (1.0 + pre * (1.0 - s)))
        du = jnp.zeros_like(uv)
        for kk in range(SSD_CONV):
            j = SSD_CONV - 1 - kk
            du = du + w_ref[kk:kk + 1, :] * _shift_up(dpre, j)
            dw_ref[kk:kk + 1, :] += jnp.sum(dpre * _shift_down(uv, j), axis=0, keepdims=True)
        db_ref[...] += jnp.sum(dpre, axis=0, keepdims=True)
        du_ref[0] = du

    return _pcall(body, name=name, grid=(nb, B),
                  in_specs=[pl.BlockSpec((1, S, tc), lambda j, bb: (bb, 0, col0_blk + j)),
                            pl.BlockSpec((8, tc), lambda j, bb: (0, j)), pl.BlockSpec((1, tc), lambda j, bb: (0, j)),
                            pl.BlockSpec((1, S, tc), lambda j, bb: (bb, 0, j))],
                  out_specs=[pl.BlockSpec((1, S, tc), lambda j, bb: (bb, 0, j)),
                             pl.BlockSpec((8, tc), lambda j, bb: (0, j)), pl.BlockSpec((1, tc), lambda j, bb: (0, j))],
                  out_shape=[jax.ShapeDtypeStruct((B, S, SSD_CONV_DIM), F32), jax.ShapeDtypeStruct((8, SSD_CONV_DIM), F32),
                             jax.ShapeDtypeStruct((1, SSD_CONV_DIM), F32)],
                  compiler_params=_params(("arbitrary", "arbitrary")))(u, w8, b, dy)


def _expand_mat():
    e = np.zeros((LANES, SSD_GW), np.float32)
    for h in range(SSD_HPG):
        e[h, h * SSD_HEAD_DIM:(h + 1) * SSD_HEAD_DIM] = 1.0
    return jnp.asarray(e)


def _tri_mat():
    return jnp.asarray(np.tril(np.ones((SSD_CHUNK, SSD_CHUNK), np.float32)))


def _split3(x):
    hi = x.astype(BF16)
    r = x - hi.astype(F32)
    mid = r.astype(BF16)
    return hi, mid, (r - mid.astype(F32)).astype(BF16)


def _dotf(a, b, dn=(((1,), (0,)), ((), ())), sel="b"):
    if sel == "b":
        s = b.astype(BF16)
        parts = [lax.dot_general(p, s, dn, preferred_element_type=F32) for p in _split3(a)]
    else:
        s = a.astype(BF16)
        parts = [lax.dot_general(s, p, dn, preferred_element_type=F32) for p in _split3(b)]
    return parts[0] + parts[1] + parts[2]


def _dotb(a, b, dn=(((1,), (0,)), ((), ()))):
    return lax.dot_general(a.astype(BF16), b.astype(BF16), dn, preferred_element_type=F32)


NT = (((1,), (1,)), ((), ()))
TN = (((0,), (0,)), ((), ()))


def _ssd_common(dt_ref, par_ref, tri_ref, e_ref):
    dtr = dt_ref[0]
    xb = dtr + par_ref[0:1, :]
    dt = jnp.maximum(xb, 0.0) + jnp.log(1.0 + jnp.exp(-jnp.abs(xb)))
    A = -jnp.exp(par_ref[1:2, :])
    a = dt * A
    acs = _dotf(tri_ref[...], a, sel="a")
    alast = acs[SSD_CHUNK - 1:SSD_CHUNK, :]
    E = e_ref[...]
    acsE = _dotf(acs, E)
    alastE = acsE[SSD_CHUNK - 1:SSD_CHUNK, :]
    return dict(xb=xb, dt=dt, A=A, acs=acs, acsT=acs.T, ealast=jnp.exp(alast), w=jnp.exp(alast - acs),
                dtE=_dotf(dt, E), eacsE=jnp.exp(acsE), wE=jnp.exp(alastE - acsE), ealastE=jnp.exp(alastE),
                DE=_dotf(par_ref[...], E)[2:3, :])


def _decay(c, e):
    diff = c["acs"][:, e:e + 1] - c["acsT"][e:e + 1, :]
    li = lax.broadcasted_iota(jnp.int32, diff.shape, 0)
    si = lax.broadcasted_iota(jnp.int32, diff.shape, 1)
    return jnp.where(li >= si, jnp.exp(jnp.minimum(diff, 0.0)), 0.0)


def ssd_fwd(xbc, dtp, par, name):
    B, S, _ = xbc.shape
    nc = S // SSD_CHUNK
    L, N, GW, P = SSD_CHUNK, SSD_STATE, SSD_GW, SSD_HEAD_DIM

    def body(x_ref, b_ref, c_ref, dt_ref, par_ref, tri_ref, e_ref, y_ref, st_ref, S_scr):
        @pl.when(pl.program_id(2) == 0)
        def _():
            S_scr[...] = jnp.zeros_like(S_scr)

        Sin = S_scr[...]
        st_ref[0, 0, 0] = Sin
        c = _ssd_common(dt_ref, par_ref.at[0], tri_ref, e_ref)
        X = x_ref[0]
        Bm = b_ref[0]
        Cm = c_ref[0]
        xdt = X * c["dtE"]
        CB = _dotb(Cm, Bm, NT)
        y_ref[0] = c["eacsE"] * _dotb(Cm, Sin) + c["DE"] * X
        for e in range(SSD_HPG):
            M = CB * _decay(c, e)
            y_ref[0, :, e * P:(e + 1) * P] += _dotb(M, xdt[:, e * P:(e + 1) * P])
        S_scr[...] = c["ealastE"] * Sin + _dotb(Bm, xdt * c["wE"], TN)

    return _pcall(
        body, name=name, grid=(SSD_GROUPS, B, nc),
        in_specs=[pl.BlockSpec((1, L, GW), lambda g, b, ci: (b, ci, g)),
                  pl.BlockSpec((1, L, N), lambda g, b, ci: (b, ci, 8 + g)),
                  pl.BlockSpec((1, L, N), lambda g, b, ci: (b, ci, 10 + g)),
                  pl.BlockSpec((1, L, LANES), lambda g, b, ci: (b, ci, g)),
                  pl.BlockSpec((1, 8, LANES), lambda g, b, ci: (g, 0, 0)),
                  pl.BlockSpec((L, L), lambda g, b, ci: (0, 0)),
                  pl.BlockSpec((LANES, GW), lambda g, b, ci: (0, 0))],
        out_specs=[pl.BlockSpec((1, L, GW), lambda g, b, ci: (b, ci, g)),
                   pl.BlockSpec((1, 1, 1, N, GW), lambda g, b, ci: (g, b, ci, 0, 0))],
        out_shape=[jax.ShapeDtypeStruct((B, S, SSD_INNER), F32), jax.ShapeDtypeStruct((SSD_GROUPS, B, nc, N, GW), F32)],
        scratch_shapes=[pltpu.VMEM((N, GW), F32)],
        compiler_params=_params(("arbitrary", "arbitrary", "arbitrary")))(xbc, xbc, xbc, dtp, par, _tri_mat(), _expand_mat())


def ssd_bwd(xbc, dtp, par, states, dy, name):
    B, S, _ = xbc.shape
    nc = S // SSD_CHUNK
    L, N, GW, P = SSD_CHUNK, SSD_STATE, SSD_GW, SSD_HEAD_DIM

    def body(x_ref, b_ref, c_ref, dt_ref, par_ref, tri_ref, e_ref, st_ref, dy_ref,
             dx_ref, db_ref, dc_ref, ddt_ref, dpar_ref, dS_scr, dxdt_scr):
        @pl.when(pl.program_id(2) == 0)
        def _():
            dS_scr[...] = jnp.zeros_like(dS_scr)

        @pl.when((pl.program_id(1) == 0) & (pl.program_id(2) == 0))
        def _():
            dpar_ref[...] = jnp.zeros_like(dpar_ref)

        c = _ssd_common(dt_ref, par_ref.at[0], tri_ref, e_ref)
        E = e_ref[...]
        red = lambda t: _dotf(t, E, NT)
        red_row = lambda t: red(jnp.broadcast_to(t, (8, GW)))[0:1, :]
        X, Bm, Cm = x_ref[0], b_ref[0], c_ref[0]
        Sin = st_ref[0, 0, 0]
        dY = dy_ref[0]
        dSo = dS_scr[...]
        xdt = X * c["dtE"]
        CB = _dotb(Cm, Bm, NT)
        dD = red_row(jnp.sum(dY * X, axis=0, keepdims=True))
        yoff = c["eacsE"] * _dotb(Cm, Sin)
        dacs = red(dY * yoff)
        dYe = dY * c["eacsE"]
        dC = _dotb(dYe, Sin, NT)
        dSin = _dotb(Cm, dYe, TN) + c["ealastE"] * dSo
        dB = _dotb(xdt * c["wE"], dSo, NT)
        dxw = _dotb(Bm, dSo)
        t = red(dxw * xdt) * c["w"]
        dacs = dacs - t
        dalast = jnp.sum(t, axis=0, keepdims=True) + c["ealast"] * red_row(jnp.sum(Sin * dSo, axis=0, keepdims=True))
        dxdt_scr[...] = dxw * c["wE"]
        dCB = jnp.zeros((L, L), F32)
        dacsT = jnp.zeros((LANES, L), F32)
        lane = lax.broadcasted_iota(jnp.int32, (1, LANES), 1)
        subl = lax.broadcasted_iota(jnp.int32, (LANES, 1), 0)
        for e in range(SSD_HPG):
            Lm = _decay(c, e)
            M = CB * Lm
            dYh = dY[:, e * P:(e + 1) * P]
            dM = _dotb(dYh, xdt[:, e * P:(e + 1) * P], NT)
            dxdt_scr[:, e * P:(e + 1) * P] += _dotb(M, dYh, TN)
            dCBe = dM * Lm
            dCB = dCB + dCBe
            Gm = dCBe * CB
            dacs = dacs + jnp.sum(Gm, axis=1, keepdims=True) * (lane == e).astype(F32)
            dacsT = dacsT - jnp.sum(Gm, axis=0, keepdims=True) * (subl == e).astype(F32)
        dC = dC + _dotb(dCB, Bm)
        dB = dB + _dotb(dCB, Cm, TN)
        dacs = dacs + dacsT.T
        row = lax.broadcasted_iota(jnp.int32, (L, 1), 0)
        dacs = dacs + jnp.where(row == L - 1, dalast, 0.0)
        da = _dotf(tri_ref[...], dacs, TN, sel="a")
        dxdt = dxdt_scr[...]
        ddt = da * c["A"] + red(dxdt * X)
        dA = jnp.sum(da * c["dt"], axis=0, keepdims=True)
        ddtr = ddt * _sigmoid(c["xb"])
        dx_ref[0] = c["DE"] * dY + dxdt * c["dtE"]
        db_ref[0] = dB
        dc_ref[0] = dC
        ddt_ref[0] = ddtr
        dpar_ref[0, 0:1, :] += jnp.sum(ddtr, axis=0, keepdims=True)
        dpar_ref[0, 1:2, :] += dA * c["A"]
        dpar_ref[0, 2:3, :] += dD
        dS_scr[...] = dSin

    rc = lambda ci: nc - 1 - ci
    xspec = pl.BlockSpec((1, L, GW), lambda g, b, ci: (b, rc(ci), g))
    return _pcall(
        body, name=name, grid=(SSD_GROUPS, B, nc),
        in_specs=[xspec,
                  pl.BlockSpec((1, L, N), lambda g, b, ci: (b, rc(ci), 8 + g)),
                  pl.BlockSpec((1, L, N), lambda g, b, ci: (b, rc(ci), 10 + g)),
                  pl.BlockSpec((1, L, LANES), lambda g, b, ci: (b, rc(ci), g)),
                  pl.BlockSpec((1, 8, LANES), lambda g, b, ci: (g, 0, 0)),
                  pl.BlockSpec((L, L), lambda g, b, ci: (0, 0)),
                  pl.BlockSpec((LANES, GW), lambda g, b, ci: (0, 0)),
                  pl.BlockSpec((1, 1, 1, N, GW), lambda g, b, ci: (g, b, rc(ci), 0, 0)),
                  xspec],
        out_specs=[xspec,
                   pl.BlockSpec((1, L, N), lambda g, b, ci: (b, rc(ci), g)),
                   pl.BlockSpec((1, L, N), lambda g, b, ci: (b, rc(ci), g)),
                   pl.BlockSpec((1, L, LANES), lambda g, b, ci: (b, rc(ci), g)),
                   pl.BlockSpec((1, 8, LANES), lambda g, b, ci: (g, 0, 0))],
        out_shape=[jax.ShapeDtypeStruct((B, S, SSD_INNER), F32), jax.ShapeDtypeStruct((B, S, SSD_GROUPS * N), F32),
                   jax.ShapeDtypeStruct((B, S, SSD_GROUPS * N), F32), jax.ShapeDtypeStruct((B, S, SSD_GROUPS * LANES), F32),
                   jax.ShapeDtypeStruct((SSD_GROUPS, 8, LANES), F32)],
        scratch_shapes=[pltpu.VMEM((N, GW), F32), pltpu.VMEM((L, GW), F32)],
        compiler_params=_params(("arbitrary", "arbitrary", "arbitrary")))(
            xbc, xbc, xbc, dtp, par, _tri_mat(), _expand_mat(), states, dy)


def gate_up_act(h, wg, wu, name):
    T, D = h.shape
    c = wg.shape[-1]
    tm = _tile(T, (512, 256, 128))

    def body(h_ref, g_w, u_w, g_ref, u_ref, a_ref):
        hv = h_ref[...]
        g = jnp.dot(hv, g_w[...], preferred_element_type=F32)
        u = jnp.dot(hv, u_w[...], preferred_element_type=F32)
        g_ref[...] = g.astype(BF16)
        u_ref[...] = u.astype(BF16)
        a_ref[...] = (g * _sigmoid(g) * u).astype(BF16)

    ws = pl.BlockSpec((None, None, D, c), lambda i, j: (0, j, 0, 0))
    os_ = pl.BlockSpec((tm, c), lambda i, j: (i, j))
    return _pcall(body, name=name, grid=(T // tm, 4), in_specs=[pl.BlockSpec((tm, D), lambda i, j: (i, 0)), ws, ws],
                  out_specs=[os_, os_, os_], out_shape=[jax.ShapeDtypeStruct((T, 4 * c), BF16)] * 3,
                  compiler_params=_params(("parallel", "parallel")))(h, wg, wu)


def gate_up_dx(dg, du, wg, wu, name):
    T = dg.shape[0]
    _, _, D, c = wg.shape
    tm, tn = _tile(T, (1024, 512, 256, 128)), _tile(D, (1024, 512, 256, 128))

    def body(g_ref, u_ref, gw_ref, uw_ref, o_ref, acc):
        k = pl.program_id(2)
        d = (lax.dot_general(g_ref[...], gw_ref[...], NT, preferred_element_type=F32)
             + lax.dot_general(u_ref[...], uw_ref[...], NT, preferred_element_type=F32))

        @pl.when(k == 0)
        def _():
            acc[...] = d

        @pl.when((k > 0) & (k < 3))
        def _():
            acc[...] += d

        @pl.when(k == 3)
        def _():
            o_ref[...] = acc[...] + d

    a_spec = pl.BlockSpec((tm, c), lambda i, j, k: (i, k))
    w_spec = pl.BlockSpec((None, None, tn, c), lambda i, j, k: (0, k, j, 0))
    return _pcall(body, name=name, grid=(T // tm, D // tn, 4), in_specs=[a_spec, a_spec, w_spec, w_spec],
                  out_specs=pl.BlockSpec((tm, tn), lambda i, j, k: (i, j)), out_shape=jax.ShapeDtypeStruct((T, D), F32),
                  scratch_shapes=[pltpu.VMEM((tm, tn), F32)],
                  compiler_params=_params(("parallel", "parallel", "arbitrary")))(dg, du, wg, wu)


def down_dx_swiglu(dy, w_down, gate, up, name, dep=None):
    T, D = dy.shape
    Fh = w_down.shape[0]
    tm, tn = _tile(T, (1024, 512, 256, 128)), 512

    def body(dy_ref, w_ref, g_ref, u_ref, *rest):
        dg_ref, du_ref = rest[-2:]
        d = lax.dot_general(dy_ref[...], w_ref[...], NT, preferred_element_type=F32)
        g = g_ref[...].astype(F32)
        u = u_ref[...].astype(F32)
        s = _sigmoid(g)
        dg_ref[...] = (d * u * (s * (1.0 + g * (1.0 - s)))).astype(BF16)
        du_ref[...] = (d * g * s).astype(BF16)

    ts = pl.BlockSpec((tm, tn), lambda i, j: (i, j))
    in_specs = [pl.BlockSpec((tm, D), lambda i, j: (i, 0)), pl.BlockSpec((tn, D), lambda i, j: (j, 0)), ts, ts]
    args = (dy, w_down, gate, up)
    if dep is not None:
        in_specs, args = in_specs + [pl.BlockSpec(memory_space=pl.ANY)], args + (dep,)
    return _pcall(body, name=name, grid=(T // tm, Fh // tn), in_specs=in_specs, out_specs=[ts, ts],
                  out_shape=[jax.ShapeDtypeStruct((T, Fh), BF16)] * 2,
                  compiler_params=_params(("parallel", "parallel")))(*args)


def swiglu_fwd(g, u, name):
    T = g.shape[0]
    tq, tc = _tile(T, (512, 256, 128)), 512
    nf = FFN // tc

    def body(g_ref, u_ref, o_ref):
        gv = g_ref[...].astype(F32)
        o_ref[...] = (gv * _sigmoid(gv) * u_ref[...].astype(F32)).astype(BF16)

    lo = pl.BlockSpec((tq, tc), lambda i, j: (i, j))
    return _pcall(body, name=name, grid=(T // tq, nf), in_specs=[lo, lo], out_specs=lo,
                  out_shape=jax.ShapeDtypeStruct((T, FFN), BF16), compiler_params=_params(("parallel", "parallel")))(g, u)


def swiglu_bwd(gate, up, dact, name):
    T = gate.shape[0]
    tq, tc = _tile(T, (512, 256, 128)), 512
    nf = FFN // tc

    def body(g_ref, u_ref, d_ref, dg_ref, du_ref):
        g = g_ref[...].astype(F32)
        u = u_ref[...].astype(F32)
        d = d_ref[...]
        s = _sigmoid(g)
        dg_ref[...] = (d * u * (s * (1.0 + g * (1.0 - s)))).astype(BF16)
        du_ref[...] = (d * g * s).astype(BF16)

    lo = pl.BlockSpec((tq, tc), lambda i, j: (i, j))
    return _pcall(body, name=name, grid=(T // tq, nf), in_specs=[lo, lo, lo], out_specs=[lo, lo],
                  out_shape=[jax.ShapeDtypeStruct((T, FFN), BF16)] * 2,
                  compiler_params=_params(("parallel", "parallel")))(gate, up, dact)


def loss_head(y, target):
    T, D = y.shape
    tq = _tile(T, (256, 128))

    def body(y_ref, t_ref, dy_ref, l_ref, dyb_ref):
        @pl.when(pl.program_id(0) == 0)
        def _():
            l_ref[...] = jnp.zeros_like(l_ref)

        e = y_ref[...] - t_ref[...]
        dy = e * (1.0 / D)
        dy_ref[...] = dy
        dyb_ref[...] = dy.astype(BF16)
        l_ref[...] += 0.5 * jnp.sum(jnp.mean(e * e, axis=-1, keepdims=True))

    xs = pl.BlockSpec((tq, D), lambda i: (i, 0))
    return _pcall(body, name="loss_head", grid=(T // tq,), in_specs=[xs, xs],
                  out_specs=[xs, pl.BlockSpec((8, LANES), lambda i: (0, 0)), xs],
                  out_shape=[jax.ShapeDtypeStruct((T, D), F32), jax.ShapeDtypeStruct((8, LANES), F32),
                             jax.ShapeDtypeStruct((T, D), BF16)],
                  compiler_params=_params(("arbitrary",)))(y, target)


def sum_arrays(arrs, out_dtype, name):
    R, C = arrs[0].shape
    tr = _tile(R, (512, 256, 128, 64, 32, 16, 8))
    n = len(arrs)

    def body(*refs):
        acc = refs[0][...].astype(F32)
        for r in refs[1:n]:
            acc = acc + r[...].astype(F32)
        refs[n][...] = acc.astype(out_dtype)

    xs = pl.BlockSpec((tr, C), lambda i: (i, 0))
    return _pcall(body, name=name, grid=(R // tr,), in_specs=[xs] * n, out_specs=xs,
                  out_shape=jax.ShapeDtypeStruct((R, C), out_dtype), compiler_params=_params(("parallel",)))(*arrs)


def adamw(w, g, m, v, name):
    R, C = w.shape
    tr = _tile(R, (256, 128, 64, 32, 16, 8))
    c1 = 1.0 - ADAM_B1 ** ADAM_STEP
    c2 = 1.0 - ADAM_B2 ** ADAM_STEP

    def body(w_ref, g_ref, m_ref, v_ref, d_ref, nm_ref, nv_ref):
        gv = g_ref[...]
        mn = ADAM_B1 * m_ref[...] + (1.0 - ADAM_B1) * gv
        vn = ADAM_B2 * v_ref[...] + (1.0 - ADAM_B2) * (gv * gv)
        nm_ref[...] = mn
        nv_ref[...] = vn
        d_ref[...] = -ADAM_LR * ((mn / c1) / (jnp.sqrt(vn / c2) + ADAM_EPS) + ADAM_WD * w_ref[...])

    xs = pl.BlockSpec((tr, C), lambda i: (i, 0))
    return _pcall(body, name=name, grid=(R // tr,), in_specs=[xs] * 4, out_specs=[xs] * 3,
                  out_shape=[jax.ShapeDtypeStruct((R, C), F32)] * 3, compiler_params=_params(("parallel",)))(w, g, m, v)


def adamw_layer(w, g, m, v, l, prev, name, g_own=None):
    if g_own is not None:
        return _adamw_layer_halves(w, g, g_own, m, v, l, prev, name)
    L, r, c = w.shape
    tr = _tile(r, (256, 128, 64, 32, 16, 8))
    c1 = 1.0 - ADAM_B1 ** ADAM_STEP
    c2 = 1.0 - ADAM_B2 ** ADAM_STEP
    chained = prev is not None

    def body(w_ref, g_ref, m_ref, v_ref, *rest):
        go_ref, d_ref, nm_ref, nv_ref = rest[-4:]
        gv = g_ref[...]
        mn = ADAM_B1 * m_ref[...] + (1.0 - ADAM_B1) * gv
        vn = ADAM_B2 * v_ref[...] + (1.0 - ADAM_B2) * (gv * gv)
        go_ref[...] = gv
        nm_ref[...] = mn
        nv_ref[...] = vn
        d_ref[...] = -ADAM_LR * ((mn / c1) / (jnp.sqrt(vn / c2) + ADAM_EPS) + ADAM_WD * w_ref[...])

    ls = pl.BlockSpec((None, tr, c), lambda i: (l, i, 0))
    gs = pl.BlockSpec((tr, c), lambda i: (i, 0))
    anys = [pl.BlockSpec(memory_space=pl.ANY)] * 4 if chained else []
    return _pcall(body, name=name, grid=(r // tr,), in_specs=[ls, gs, ls, ls] + anys, out_specs=[ls] * 4,
                  out_shape=[jax.ShapeDtypeStruct((L, r, c), F32)] * 4,
                  input_output_aliases={4: 0, 5: 1, 6: 2, 7: 3} if chained else {},
                  compiler_params=_params(("parallel",)))(w, g, m, v, *(prev if chained else ()))


def _adamw_layer_halves(w, g_join, g_own, m, v, l, prev, name):
    L, r, c = w.shape
    h = r // 2
    tr = _tile(h, (256, 128, 64, 32, 16, 8))
    nh = h // tr
    c1 = 1.0 - ADAM_B1 ** ADAM_STEP
    c2 = 1.0 - ADAM_B2 ** ADAM_STEP
    chained = prev is not None

    def body(c_ref, w_ref, gj_ref, go_ref, m_ref, v_ref, *rest):
        gout_ref, d_ref, nm_ref, nv_ref = rest[-4:]
        i = pl.program_id(0)
        mine = (i >= c_ref[0] * nh) & (i < (c_ref[0] + 1) * nh)
        gv = jnp.where(mine, go_ref[...], gj_ref[...])
        mn = ADAM_B1 * m_ref[...] + (1.0 - ADAM_B1) * gv
        vn = ADAM_B2 * v_ref[...] + (1.0 - ADAM_B2) * (gv * gv)
        gout_ref[...] = gv
        nm_ref[...] = mn
        nv_ref[...] = vn
        d_ref[...] = -ADAM_LR * ((mn / c1) / (jnp.sqrt(vn / c2) + ADAM_EPS) + ADAM_WD * w_ref[...])

    ls = pl.BlockSpec((None, tr, c), lambda i, cs: (l, i, 0))
    gj = pl.BlockSpec((tr, c), lambda i, cs: (i, 0))
    go = pl.BlockSpec((tr, c), lambda i, cs: (jnp.clip(i - cs[0] * nh, 0, nh - 1), 0))
    anys = [pl.BlockSpec(memory_space=pl.ANY)] * 4 if chained else []
    spec = pltpu.PrefetchScalarGridSpec(num_scalar_prefetch=1, grid=(r // tr,), in_specs=[ls, gj, go, ls, ls] + anys,
                                        out_specs=[ls] * 4)
    return _pcall(body, name=name, grid_spec=spec, out_shape=[jax.ShapeDtypeStruct((L, r, c), F32)] * 4,
                  input_output_aliases={6: 0, 7: 1, 8: 2, 9: 3} if chained else {},
                  compiler_params=_params(("arbitrary",)))(
                      lax.axis_index("c").reshape(1).astype(jnp.int32), w, g_join, g_own, m, v, *(prev if chained else ()))


def add_pair(g, got, name):
    _, r, c = g.shape
    h = r // 2

    def body(c_ref, g_ref, o_ref, out_ref):
        del c_ref
        out_ref[...] = (g_ref[...].astype(F32) + o_ref[...].astype(F32)).astype(BF16)

    spec = pltpu.PrefetchScalarGridSpec(
        num_scalar_prefetch=1, grid=(4,),
        in_specs=[pl.BlockSpec((None, h, c), lambda k, cs: (k, cs[0], 0)), pl.BlockSpec((None, h, c), lambda k, cs: (k, 0, 0))],
        out_specs=pl.BlockSpec((None, h, c), lambda k, cs: (k, 0, 0)))
    return _pcall(body, name=name, grid_spec=spec, out_shape=jax.ShapeDtypeStruct((4, h, c), BF16),
                  compiler_params=_params(("arbitrary",)))(lax.axis_index("c").reshape(1).astype(jnp.int32), g, got)


def add_chips(p, b, name):
    _, h, c = p.shape
    tr = _tile(h, (256, 128, 64, 32, 16, 8))

    def body(me_ref, p_ref, b0, b1, b2, b3, out_ref):
        me = me_ref[0]
        acc = jnp.zeros(out_ref.shape, F32)
        for k, bk in enumerate((b0, b1, b2, b3)):
            acc = acc + jnp.where(me == k, p_ref[...], bk[...]).astype(F32)
        out_ref[...] = acc

    slot = lambda k: pl.BlockSpec((None, tr, c), lambda i, ms: (jnp.where(ms[0] == k, (k + 1) % 4, k), i, 0))
    spec = pltpu.PrefetchScalarGridSpec(
        num_scalar_prefetch=1, grid=(h // tr,),
        in_specs=[pl.BlockSpec((None, tr, c), lambda i, ms: (ms[0], i, 0)), slot(0), slot(1), slot(2), slot(3)],
        out_specs=pl.BlockSpec((tr, c), lambda i, ms: (i, 0)))
    me = (2 * lax.axis_index("x") + lax.axis_index("y")).reshape(1).astype(jnp.int32)
    return _pcall(body, name=name, grid_spec=spec, out_shape=jax.ShapeDtypeStruct((h, c), F32),
                  compiler_params=_params(("arbitrary",)))(me, p, b, b, b, b)


ANY = pl.BlockSpec(memory_space=pl.ANY)


def _place():
    x, y, c = lax.axis_index("x"), lax.axis_index("y"), lax.axis_index("c")
    chips = [(1 - x, y), (x, 1 - y), (1 - x, 1 - y)]
    return x, y, c, chips


def _rcopy(src, dst, ssem, rsem, dev):
    return pltpu.make_async_remote_copy(src_ref=src, dst_ref=dst, send_sem=ssem, recv_sem=rsem, device_id=dev,
                                        device_id_type=MESH)


def _half(c, h):
    return pl.ds(pl.multiple_of(c * h, 64), h)


def _me_chip():
    return 2 * lax.axis_index("x") + lax.axis_index("y")


def all_gather_weights(shards, name):
    n = len(shards)
    L = shards[0].shape[0]
    hl = L // 2

    def body(*refs):
        srcs, outs, (ssem, rsem) = refs[:n], refs[n:2 * n], refs[2 * n:]
        x, y, c, chips = _place()
        me = 2 * x + y
        sib = (x, y, 1 - c)
        mine, other = pl.ds(c * hl, hl), pl.ds((1 - c) * hl, hl)
        sends = []
        for w in range(n):
            for j, (px, py) in enumerate(chips):
                cp = _rcopy(srcs[w].at[mine], outs[w].at[mine, me], ssem.at[w, j], rsem.at[w, j], (px, py, c))
                cp.start()
                sends.append(cp)
        for j, (px, py) in enumerate(chips):
            slot = 2 * px + py
            for w in range(n):
                _rcopy(srcs[w].at[mine], outs[w].at[mine, slot], ssem.at[w, j], rsem.at[w, j], (px, py, c)).wait_recv()
                cp = _rcopy(outs[w].at[mine, slot], outs[w].at[mine, slot], ssem.at[w, 3 + j], rsem.at[w, 3 + j], sib)
                cp.start()
                sends.append(cp)
        for j, (px, py) in enumerate(chips):
            slot = 2 * px + py
            for w in range(n):
                _rcopy(outs[w].at[other, slot], outs[w].at[other, slot], ssem.at[w, 3 + j], rsem.at[w, 3 + j], sib).wait_recv()
        for cp in sends:
            cp.wait_send()

    outs = _pcall(body, name=name, in_specs=[ANY] * n, out_specs=[ANY] * n,
                  out_shape=[jax.ShapeDtypeStruct((L, 4) + s.shape[1:], s.dtype) for s in shards],
                  scratch_shapes=[pltpu.SemaphoreType.DMA((n, 6)), pltpu.SemaphoreType.DMA((n, 6))])(*shards)
    me = _me_chip()
    return [lax.dynamic_update_slice(o, s[:, None], (0, me, 0, 0)) for o, s in zip(outs, shards)]


HBM = pl.BlockSpec(memory_space=pltpu.HBM)
SEM = pl.BlockSpec(memory_space=pltpu.SEMAPHORE)
EFFECT = pltpu.SideEffectType.DATAFLOW_SIDE_EFFECTING


def _ici_copy(kind, src, land, x, y, c, px, py, ssem, rsem, receiving):
    me, peer = 2 * x + y, 2 * px + py
    if kind == "pair":
        return _rcopy(src.at[:, _half(1 - c, land.shape[1])], land, ssem, rsem, (x, y, 1 - c))
    if kind == "gather":
        rows = _half(c, land.shape[1] // 2)
        s, d = land.at[me, rows], land.at[peer if receiving else me, rows]
    else:
        s, d = src.at[me if receiving else peer], land.at[peer if receiving else me]
    return _rcopy(s, d, ssem, rsem, (px, py, c))


def ici_start(kind, srcs, lands, name):
    n, ns = len(lands), len(srcs)
    P = 1 if kind == "pair" else 3
    m = P * n
    na = ns + n

    def body(*refs):
        src, land, token = (refs[:ns] if ns else [None] * n), refs[ns:na], refs[-1]
        ssem, rsem = refs[na:na + m], refs[na + m:na + 2 * m]
        x, y, c, chips = _place()
        for w in range(n):
            for j, (px, py) in enumerate(chips[:P]):
                _ici_copy(kind, src[w], land[w], x, y, c, px, py, ssem[P * w + j], rsem[P * w + j], False).start()
        token[...] = jnp.zeros_like(token)

    arrs = [pltpu.with_memory_space_constraint(a, pltpu.HBM) for a in list(srcs) + list(lands)]
    outs = _pcall(
        body, name=name,
        out_shape=(*[pltpu.SemaphoreType.DMA(())] * (2 * m), *[pltpu.HBM(a.shape, a.dtype) for a in arrs],
                   jax.ShapeDtypeStruct((8, LANES), F32)),
        in_specs=[HBM] * na,
        out_specs=(*([SEM] * (2 * m)), *([HBM] * na), pl.BlockSpec(memory_space=pltpu.VMEM)),
        input_output_aliases={i: 2 * m + i for i in range(na)},
        compiler_params=pltpu.CompilerParams(has_side_effects=EFFECT))(*arrs)
    return dict(kind=kind, ssem=list(outs[:m]), rsem=list(outs[m:2 * m]), srcs=list(outs[2 * m:2 * m + ns]),
                lands=list(outs[2 * m + ns:2 * m + na])), outs[-1]


def ici_wait(state, after, name):
    kind, n, ns = state["kind"], len(state["lands"]), len(state["srcs"])
    P = 1 if kind == "pair" else 3
    m = P * n
    na = ns + n

    def body(*refs):
        src, land = (refs[:ns] if ns else [None] * n), refs[ns:na]
        ssem, rsem = refs[na:na + m], refs[na + m:na + 2 * m]
        x, y, c, chips = _place()
        for w in range(n):
            for j, (px, py) in enumerate(chips[:P]):
                _ici_copy(kind, src[w], land[w], x, y, c, px, py, ssem[P * w + j], rsem[P * w + j], False).wait_send()
                _ici_copy(kind, src[w], land[w], x, y, c, px, py, ssem[P * w + j], rsem[P * w + j], True).wait_recv()

    outs = _pcall(
        body, name=name,
        out_shape=tuple(pltpu.HBM(a.shape, a.dtype) for a in state["srcs"] + state["lands"]),
        in_specs=[HBM] * na + [SEM] * (2 * m) + [ANY], out_specs=tuple([HBM] * na),
        input_output_aliases={i: i for i in range(na)},
        compiler_params=pltpu.CompilerParams(has_side_effects=EFFECT))(
            *state["srcs"], *state["lands"], *state["ssem"], *state["rsem"], after)
    return list(outs[:ns]), list(outs[ns:])


def forward_pair(lands, name):
    n = len(lands)

    def body(*refs):
        land, (ssem, rsem) = refs[:n], refs[2 * n:]
        x, y, c, chips = _place()
        sib = (x, y, 1 - c)
        cps = []
        for w in range(n):
            h = land[w].shape[1] // 2
            for j, (px, py) in enumerate(chips):
                part = land[w].at[2 * px + py, _half(c, h)]
                cp = _rcopy(part, part, ssem.at[w, j], rsem.at[w, j], sib)
                cp.start()
                cps.append(cp)
        for w in range(n):
            h = land[w].shape[1] // 2
            for j, (px, py) in enumerate(chips):
                part = land[w].at[2 * px + py, _half(1 - c, h)]
                _rcopy(part, part, ssem.at[w, j], rsem.at[w, j], sib).wait_recv()
        for cp in cps:
            cp.wait_send()

    return _pcall(body, name=name, in_specs=[ANY] * n, out_specs=[ANY] * n,
                  out_shape=[jax.ShapeDtypeStruct(z.shape, z.dtype) for z in lands],
                  input_output_aliases={i: i for i in range(n)},
                  scratch_shapes=[pltpu.SemaphoreType.DMA((n, 3)), pltpu.SemaphoreType.DMA((n, 3))])(*lands)


def exchange_pair(gs, name):
    n = len(gs)

    def body(*refs):
        srcs, outs, (ssem, rsem) = refs[:n], refs[n:2 * n], refs[2 * n:]
        x, y, c, _ = _place()
        cps = []
        for w in range(n):
            h = srcs[w].shape[1] // 2
            cp = _rcopy(srcs[w].at[:, _half(1 - c, h)], outs[w], ssem.at[w], rsem.at[w], (x, y, 1 - c))
            cp.start()
            cps.append(cp)
        for cp in cps:
            cp.wait()

    return _pcall(body, name=name, in_specs=[ANY] * n, out_specs=[ANY] * n,
                  out_shape=[jax.ShapeDtypeStruct((4, g.shape[1] // 2, g.shape[2]), g.dtype) for g in gs],
                  scratch_shapes=[pltpu.SemaphoreType.DMA((n,)), pltpu.SemaphoreType.DMA((n,))])(*gs)


def scatter_chips(ps, name):
    n = len(ps)

    def body(*refs):
        srcs, outs, (ssem, rsem) = refs[:n], refs[n:2 * n], refs[2 * n:]
        x, y, c, chips = _place()
        me = 2 * x + y
        sends = []
        for w in range(n):
            for j, (px, py) in enumerate(chips):
                cp = _rcopy(srcs[w].at[2 * px + py], outs[w].at[me], ssem.at[w, j], rsem.at[w, j], (px, py, c))
                cp.start()
                sends.append(cp)
        for w in range(n):
            for j, (px, py) in enumerate(chips):
                _rcopy(srcs[w].at[me], outs[w].at[2 * px + py], ssem.at[w, j], rsem.at[w, j], (px, py, c)).wait_recv()
        for cp in sends:
            cp.wait_send()

    return _pcall(body, name=name, in_specs=[ANY] * n, out_specs=[ANY] * n,
                  out_shape=[jax.ShapeDtypeStruct(p.shape, p.dtype) for p in ps],
                  scratch_shapes=[pltpu.SemaphoreType.DMA((n, 3)), pltpu.SemaphoreType.DMA((n, 3))])(*ps)


def join_pair(rs, name):
    n = len(rs)

    def body(*refs):
        srcs, outs, (ssem, rsem) = refs[:n], refs[n:2 * n], refs[2 * n:]
        x, y, c, _ = _place()
        cps = []
        for w in range(n):
            h = srcs[w].shape[0]
            cp = _rcopy(srcs[w], outs[w].at[_half(c, h)], ssem.at[w], rsem.at[w], (x, y, 1 - c))
            cp.start()
            cps.append(cp)
        for w in range(n):
            h = srcs[w].shape[0]
            _rcopy(srcs[w], outs[w].at[_half(1 - c, h)], ssem.at[w], rsem.at[w], (x, y, 1 - c)).wait_recv()
        for cp in cps:
            cp.wait_send()

    outs = _pcall(body, name=name, in_specs=[ANY] * n, out_specs=[ANY] * n,
                  out_shape=[jax.ShapeDtypeStruct((2 * r.shape[0], r.shape[1]), r.dtype) for r in rs],
                  scratch_shapes=[pltpu.SemaphoreType.DMA((n,)), pltpu.SemaphoreType.DMA((n,))])(*rs)
    return outs


def fill_own_half(joined, own):
    return lax.dynamic_update_slice(joined, own, (lax.axis_index("c") * own.shape[0], 0))


def gather_all_devices(v, name):
    R, C = v.shape

    def body(src, out, ssem, rsem, lsem):
        x, y, c, _ = _place()
        me = 4 * x + 2 * y + c
        local = pltpu.make_async_copy(src, out.at[me], lsem)
        local.start()
        sends = []
        for j in range(1, 8):
            dx, dy, dc = (j >> 2) & 1, (j >> 1) & 1, j & 1
            px, py, pc = (1 - x if dx else x), (1 - y if dy else y), (1 - c if dc else c)
            cp = _rcopy(src, out.at[me], ssem.at[j - 1], rsem.at[j - 1], (px, py, pc))
            cp.start()
            sends.append(cp)
        for j in range(1, 8):
            dx, dy, dc = (j >> 2) & 1, (j >> 1) & 1, j & 1
            px, py, pc = (1 - x if dx else x), (1 - y if dy else y), (1 - c if dc else c)
            _rcopy(src, out.at[4 * px + 2 * py + pc], ssem.at[j - 1], rsem.at[j - 1], (px, py, pc)).wait_recv()
        for cp in sends:
            cp.wait_send()
        local.wait()

    return _pcall(body, name=name, in_specs=[ANY], out_specs=ANY, out_shape=jax.ShapeDtypeStruct((8, R, C), v.dtype),
                  scratch_shapes=[pltpu.SemaphoreType.DMA((7,)), pltpu.SemaphoreType.DMA((7,)), pltpu.SemaphoreType.DMA(())])(v)


def reduce_scatter_pair(gw, tag):
    names = list(gw)
    gs = [gw[k] for k in names]
    lands = [lax.empty((4, g.shape[1] // 2, g.shape[2]), g.dtype) for g in gs]
    state, token = ici_start("pair", gs, lands, "rs_pair_start_" + tag)
    state["names"] = names
    return state, token


def reduce_scatter_begin(pair_state, after, tag):
    names = pair_state["names"]
    gs, got = ici_wait(pair_state, after, "rs_pair_wait_" + tag)
    ps = [add_pair(g, o, "rs_add_pair_%s_%s" % (k, tag)) for k, g, o in zip(names, gs, got)]
    state, token = ici_start("scatter", ps, [lax.empty(p.shape, p.dtype) for p in ps], "rs_chips_start_" + tag)
    state["names"] = names
    return state, token


def reduce_scatter_end(state, after, tag):
    names = state["names"]
    ps, bs = ici_wait(state, after, "rs_chips_wait_" + tag)
    rs = [add_chips(p, b, "rs_add_chips_%s_%s" % (k, tag)) for k, p, b in zip(names, ps, bs)]
    return dict(zip(names, zip(join_pair(rs, "rs_join_" + tag), rs)))


GATHERED = ("w_in", "w_q_b", "w_kv_b", "w_out", "w_xq", "w_xk", "w_xv", "w_xo", "w_gate", "w_up", "w_down")
KIND = {"w_q_b": "col", "w_kv_b": "col", "w_out": "row", "w_xq": "row", "w_xk": "row", "w_xv": "row", "w_xo": "col",
        "w_gate": "col", "w_up": "col", "w_down": "row"}


def wq_to_storage(w):
    lead = w.shape[:-1]
    w = w.reshape(lead + (2, MLA_QK))
    rope = jnp.pad(w[..., MLA_NOPE:], [(0, 0)] * (w.ndim - 1) + [(0, LANES - MLA_ROPE)])
    return jnp.concatenate([w[..., :MLA_NOPE].reshape(lead + (2 * LANES,)), rope.reshape(lead + (2 * LANES,))], axis=-1)


def wq_from_storage(g):
    lead = g.shape[:-1]
    nope = g[..., :2 * LANES].reshape(lead + (2, LANES))
    rope = g[..., 2 * LANES:].reshape(lead + (2, LANES))[..., :MLA_ROPE]
    return jnp.concatenate([nope, rope], axis=-1).reshape(lead + (2 * MLA_QK,))


def wcat_from_storage(w_in_l):
    w = w_in_l.transpose(1, 0, 2).reshape(D_MODEL, IN_COLS)
    zc = lambda n: jnp.zeros((D_MODEL, n), w.dtype)
    return jnp.concatenate([w[:, :2560], w[:, 2576:], zc(64), w[:, 2560:2576], zc(112)], axis=1)


def w_in_grad_storage(dwcat):
    g = jnp.concatenate([dwcat[:, :2560], dwcat[:, C_DT:C_DT + 16], dwcat[:, 2560:3648]], axis=1)
    return g.reshape(D_MODEL, 4, IN_COLS // 4).transpose(1, 0, 2)


def _ssd_par(sm):
    rows = jnp.stack([sm["dt_bias"], sm["a_log"], sm["d_skip"]]).reshape(3, SSD_GROUPS, SSD_HPG).transpose(1, 0, 2)
    return jnp.pad(rows, ((0, 0), (0, 5), (0, LANES - SSD_HPG)))


def _dt_pad(proj, B, S):
    dt = proj[:, C_DT:C_DT + 16].reshape(B, S, SSD_GROUPS, SSD_HPG)
    return jnp.pad(dt, ((0, 0), (0, 0), (0, 0), (0, LANES - SSD_HPG))).reshape(B, S, SSD_GROUPS * LANES)


def _mla_gains(g):
    return g[:MLA_NOPE].reshape(1, LANES), jnp.pad(g[MLA_NOPE:], (0, LANES - MLA_ROPE)).reshape(1, LANES)


_Q_NOPE = lambda h: 4 * (h // 2) + h % 2
_Q_ROPE = lambda h: 4 * (h // 2) + 2 + h % 2
_K_NOPE = lambda h: 2 * h
_V_BLK = lambda h: 2 * h + 1


def layer_fwd(x, memf, cos, sin, G, kw, sm, B, S, l, dep=None):
    T = x.shape[0]
    n = lambda s: "%s_l%d" % (s, l)
    wv = lambda k: (KIND[k], 0)
    sv = dict(x0=x)
    h1 = norm_fwd(x, sm["attn_norm_g"], D_MODEL, False, n("attn_norm"), dep=dep)
    proj = mm(h1, kw["wcat"], "nn", n("in_proj"))
    mark = getattr(G, "mark", lambda v: None)
    mark(proj)
    sv.update(h1=h1, proj=proj)
    proj3 = proj.reshape(B, S, CAT_COLS)
    xbc = conv_fwd(proj3, C_XBC // 256, kw["conv_w"], sm["conv_b"].reshape(1, -1), n("conv"))
    dtp = _dt_pad(proj, B, S)
    par = _ssd_par(sm)
    y_raw, states = ssd_fwd(xbc, dtp, par, n("ssd"))
    z = proj[:, C_Z:C_Z + SSD_INNER]
    y_ssd = norm_fwd(y_raw.reshape(T, SSD_INNER), sm["ssd_norm_g"], SSD_INNER // SSD_GROUPS, False, n("ssd_norm"), gate=z)
    sv.update(xbc=xbc, states=states, y_raw=y_raw)
    qn = norm_fwd(proj[:, C_QA:C_QA + Q_LORA], sm["q_a_norm_g"], Q_LORA, False, n("q_a_norm"))
    kvn = norm_fwd(proj[:, C_KVA:C_KVA + KV_LORA], sm["kv_a_norm_g"], KV_LORA, False, n("kv_a_norm"))
    q_raw = mm(qn, G["w_q_b"], "nn", n("q_b"), wv=wv("w_q_b"))
    kv_raw = mm(kvn, G["w_kv_b"], "nn", n("kv_b"), wv=wv("w_kv_b"))
    gqn, gqr = _mla_gains(sm["mla_q_norm_g"])
    gkn, gkr = _mla_gains(sm["mla_k_norm_g"])
    qh = qk_fwd(q_raw, _Q_NOPE, q_raw, _Q_ROPE, False, gqn, gqr, cos, sin, n("q_norm_rope"))
    kh = qk_fwd(kv_raw, _K_NOPE, proj, C_KR // LANES, True, gkn, gkr, cos, sin, n("k_norm_rope"))
    vv = kv_raw.astype(BF16)
    y_mla = attn_fwd(qh.reshape(B, S, -1), kh.reshape(B, S, -1), vv.reshape(B, S, -1), MLA_HEADS, 2 * LANES, MLA_V,
                     MLA_QK ** -0.5, True, n("mla_attn"), v_idx=_V_BLK)
    ycat = jnp.concatenate([y_ssd, y_mla.reshape(T, -1)], axis=1)
    x1 = mm(ycat, G["w_out"], "nn", n("out_proj"), add=x, wv=wv("w_out"))
    sv.update(qn=qn, kvn=kvn, q_raw=q_raw, kv_raw=kv_raw, qh=qh, kh=kh, vv=vv, ycat=ycat, x1=x1)
    h2 = norm_fwd(x1, sm["xattn_norm_g"], D_MODEL, False, n("xattn_norm"))
    mn = norm_fwd(memf, sm["mem_norm_g"], D_MODEL, False, n("mem_norm"))
    xq = mm(h2, G["w_xq"], "nn", n("xq"), wv=wv("w_xq"))
    xk = mm(mn, G["w_xk"], "nn", n("xk"), wv=wv("w_xk"))
    xv = mm(mn, G["w_xv"], "nn", n("xv"), wv=wv("w_xv"), out_dtype=BF16)
    xqn = norm_fwd(xq, sm["xq_norm_g"], X_HEAD_DIM, True, n("xq_norm"))
    xkn = norm_fwd(xk, sm["xk_norm_g"], X_HEAD_DIM, True, n("xk_norm"))
    ML = memf.shape[0] // B
    xo = attn_fwd(xqn.reshape(B, S, -1), xkn.reshape(B, ML, -1), xv.reshape(B, ML, -1), X_HEADS, X_HEAD_DIM, X_HEAD_DIM,
                  X_HEAD_DIM ** -0.5, False, n("x_attn"))
    x2 = mm(xo.reshape(T, -1), G["w_xo"], "nn", n("xo_proj"), add=x1, wv=wv("w_xo"))
    mark(x2)
    sv.update(h2=h2, mn=mn, xq=xq, xk=xk, xqn=xqn, xkn=xkn, xv=xv, xo=xo, x2=x2)
    h3 = norm_fwd(x2, sm["ffn_norm_g"], D_MODEL, False, n("ffn_norm"))
    gate, up, act = gate_up_act(h3, G["w_gate"], G["w_up"], n("gate_up"))
    x3 = mm(act, G["w_down"], "nn", n("down"), add=x2, wv=wv("w_down"))
    sv.update(h3=h3, gate=gate, up=up, act=act)
    return x3, sv


def layer_bwd(dx3, dx3b, sv, memf, cos, sin, G, kw, sm, B, S, l, dep=None, rs_hook=None):
    T = dx3.shape[0]
    n = lambda s: "%s_bwd_l%d" % (s, l)
    wv = lambda k: (KIND[k], 0)
    dw = lambda k, a, b: mm(a, b, "tn", n(k + "_dw"), out_dtype=BF16, out_wv=KIND[k])
    gw, gs = {}, {}
    dgate, dup = down_dx_swiglu(dx3b, G["w_down"][0].reshape(FFN, D_MODEL), sv["gate"], sv["up"], n("down_dx"), dep=dep)
    gw["w_down"] = dw("w_down", sv["act"], dx3b)
    dh3 = gate_up_dx(dgate, dup, G["w_gate"], G["w_up"], n("gate_up_dx"))
    gw["w_gate"] = dw("w_gate", sv["h3"], dgate)
    gw["w_up"] = dw("w_up", sv["h3"], dup)
    dx2, gs["ffn_norm_g"], dx2b = norm_bwd(sv["x2"], sm["ffn_norm_g"], dh3, D_MODEL, False, n("ffn_norm"), add=dx3,
                                           bf16_copy=True)
    ffn_token = rs_hook({k: gw.pop(k) for k in ("w_down", "w_gate", "w_up")}) if rs_hook is not None else None
    dxo = mm(dx2b, G["w_xo"], "nt", n("xo_dx"), wv=wv("w_xo"), dep=ffn_token)
    gw["w_xo"] = dw("w_xo", sv["xo"].reshape(T, -1), dx2b)
    ML = memf.shape[0] // B
    dxqn, dxkn, dxv = attn_bwd(sv["xqn"].reshape(B, S, -1), sv["xkn"].reshape(B, ML, -1), sv["xv"].reshape(B, ML, -1),
                               dxo.reshape(B, S, -1), X_HEADS, X_HEAD_DIM, X_HEAD_DIM, X_HEAD_DIM ** -0.5, False, n("x_attn"))
    dxq, gs["xq_norm_g"] = norm_bwd(sv["xq"], sm["xq_norm_g"], dxqn.reshape(T, -1), X_HEAD_DIM, True, n("xq_norm"))
    dxk, gs["xk_norm_g"] = norm_bwd(sv["xk"], sm["xk_norm_g"], dxkn.reshape(B * ML, -1), X_HEAD_DIM, True, n("xk_norm"))
    dxv = dxv.reshape(B * ML, -1)
    dmn = mm(dxk, G["w_xk"], "nt", n("xk_dx"), wv=wv("w_xk"))
    dmn = mm(dxv, G["w_xv"], "nt", n("xv_dx"), wv=wv("w_xv"), add=dmn)
    gw["w_xk"] = dw("w_xk", sv["mn"], dxk)
    gw["w_xv"] = dw("w_xv", sv["mn"], dxv)
    _, gs["mem_norm_g"] = norm_bwd(memf, sm["mem_norm_g"], dmn, D_MODEL, False, n("mem_norm"))
    dh2 = mm(dxq, G["w_xq"], "nt", n("xq_dx"), wv=wv("w_xq"))
    gw["w_xq"] = dw("w_xq", sv["h2"], dxq)
    dx1, gs["xattn_norm_g"], dx1b = norm_bwd(sv["x1"], sm["xattn_norm_g"], dh2, D_MODEL, False, n("xattn_norm"), add=dx2,
                                             bf16_copy=True)
    dycat = mm(dx1b, G["w_out"], "nt", n("out_dx"), wv=wv("w_out"))
    gw["w_out"] = dw("w_out", sv["ycat"], dx1b)
    dqh, dkh, dvv = attn_bwd(sv["qh"].reshape(B, S, -1), sv["kh"].reshape(B, S, -1), sv["vv"].reshape(B, S, -1),
                             dycat[:, SSD_INNER:].reshape(B, S, -1), MLA_HEADS, 2 * LANES, MLA_V, MLA_QK ** -0.5, True,
                             n("mla_attn"), v_idx=_V_BLK)
    gqn, gqr = _mla_gains(sm["mla_q_norm_g"])
    gkn, gkr = _mla_gains(sm["mla_k_norm_g"])
    dq_raw, dgqn, dgqr = qk_bwd(sv["q_raw"], _Q_NOPE, sv["q_raw"], _Q_ROPE, False, gqn, gqr, cos, sin,
                                dqh.reshape(T, -1), n("q_norm_rope"))
    dkv_raw, dk_rope, dgkn, dgkr = qk_bwd(sv["kv_raw"], _K_NOPE, sv["proj"], C_KR // LANES, True, gkn, gkr, cos, sin,
                                          dkh.reshape(T, -1), n("k_norm_rope"), fill=dvv.reshape(T, -1), fill_blk=_V_BLK)
    gs["mla_q_norm_g"] = jnp.concatenate([dgqn[0], dgqr[0, :MLA_ROPE]])
    gs["mla_k_norm_g"] = jnp.concatenate([dgkn[0], dgkr[0, :MLA_ROPE]])
    dqn = mm(dq_raw, G["w_q_b"], "nt", n("q_b_dx"), wv=wv("w_q_b"))
    gw["w_q_b"] = dw("w_q_b", sv["qn"], dq_raw)
    dkvn = mm(dkv_raw, G["w_kv_b"], "nt", n("kv_b_dx"), wv=wv("w_kv_b"))
    gw["w_kv_b"] = dw("w_kv_b", sv["kvn"], dkv_raw)
    proj = sv["proj"]
    dq_a, gs["q_a_norm_g"] = norm_bwd(proj[:, C_QA:C_QA + Q_LORA], sm["q_a_norm_g"], dqn, Q_LORA, False, n("q_a_norm"))
    dkv_a, gs["kv_a_norm_g"] = norm_bwd(proj[:, C_KVA:C_KVA + KV_LORA], sm["kv_a_norm_g"], dkvn, KV_LORA, False,
                                        n("kv_a_norm"))
    z = proj[:, C_Z:C_Z + SSD_INNER]
    dy_raw, dz, gs["ssd_norm_g"] = norm_bwd(sv["y_raw"].reshape(T, -1), sm["ssd_norm_g"], dycat[:, :SSD_INNER],
                                            SSD_INNER // SSD_GROUPS, False, n("ssd_norm"), gate=z)
    dtp = _dt_pad(proj, B, S)
    par = _ssd_par(sm)
    dxs, dBm, dCm, ddt, dpar = ssd_bwd(sv["xbc"], dtp, par, sv["states"], dy_raw.reshape(B, S, -1), n("ssd"))
    gs["dt_bias"] = dpar[:, 0, :SSD_HPG].reshape(-1)
    gs["a_log"] = dpar[:, 1, :SSD_HPG].reshape(-1)
    gs["d_skip"] = dpar[:, 2, :SSD_HPG].reshape(-1)
    dxbc_act = jnp.concatenate([dxs, dBm, dCm], axis=2)
    dxbc, dconv_w, dconv_b = conv_bwd(proj.reshape(B, S, -1), C_XBC // 256, kw["conv_w"], sm["conv_b"].reshape(1, -1), dxbc_act,
                                      n("conv"))
    gs["conv_w"] = dconv_w[:SSD_CONV]
    gs["conv_b"] = dconv_b[0]
    ddt16 = ddt.reshape(T, SSD_GROUPS, LANES)[:, :, :SSD_HPG].reshape(T, 16)
    dproj = jnp.concatenate([dz, dxbc.reshape(T, -1), dq_a, dkv_a, dk_rope, ddt16, jnp.zeros((T, 112), F32)],
                            axis=1).astype(BF16)
    dh1 = mm(dproj, kw["wcat"], "nt", n("in_dx"))
    gw["w_in"] = w_in_grad_storage(mm(sv["h1"], dproj, "tn", n("in_dw"), out_dtype=BF16))
    dx0, gs["attn_norm_g"], dx0b = norm_bwd(sv["x0"], sm["attn_norm_g"], dh1, D_MODEL, False, n("attn_norm"), add=dx1,
                                            bf16_copy=True)
    gs = {k: v.reshape(-1) for k, v in gs.items()}
    return dx0, dx0b, gw, gs


def _small_pack(vals):
    flat = jnp.concatenate([vals[l][k].reshape(-1) for l in range(DEPTH) for k in SMALL_RS])
    n = flat.shape[0]
    rows = -(-n // LANES)
    rows = -(-rows // 8) * 8
    return jnp.pad(flat, (0, rows * LANES - n)).reshape(rows, LANES), n


def _small_unpack(flat, shapes):
    v = flat.reshape(-1)
    out, off = [dict() for _ in range(DEPTH)], 0
    for l in range(DEPTH):
        for k in SMALL_RS:
            sz = shapes[k]
            out[l][k] = v[off:off + sz]
            off += sz
    return out


def _adam_2d(a):
    if a.ndim == 2:
        return a
    return a.reshape(a.shape[0] * a.shape[1], a.shape[2])


def kernel(x, mem, positions, attn_norm_g, w_in, conv_w, conv_b, dt_bias, a_log, d_skip, ssd_norm_g, q_a_norm_g, w_q_b, kv_a_norm_g, w_kv_b, mla_q_norm_g, mla_k_norm_g, w_out, xattn_norm_g, mem_norm_g, w_xq, w_xk, w_xv, xq_norm_g, xk_norm_g, w_xo, ffn_norm_g, w_gate, w_up, w_down, loss_target, m_attn_norm_g, m_w_in, m_conv_w, m_conv_b, m_dt_bias, m_a_log, m_d_skip, m_ssd_norm_g, m_q_a_norm_g, m_w_q_b, m_kv_a_norm_g, m_w_kv_b, m_mla_q_norm_g, m_mla_k_norm_g, m_w_out, m_xattn_norm_g, m_mem_norm_g, m_w_xq, m_w_xk, m_w_xv, m_xq_norm_g, m_xk_norm_g, m_w_xo, m_ffn_norm_g, m_w_gate, m_w_up, m_w_down, v_attn_norm_g, v_w_in, v_conv_w, v_conv_b, v_dt_bias, v_a_log, v_d_skip, v_ssd_norm_g, v_q_a_norm_g, v_w_q_b, v_kv_a_norm_g, v_w_kv_b, v_mla_q_norm_g, v_mla_k_norm_g, v_w_out, v_xattn_norm_g, v_mem_norm_g, v_w_xq, v_w_xk, v_w_xv, v_xq_norm_g, v_xk_norm_g, v_w_xo, v_ffn_norm_g, v_w_gate, v_w_up, v_w_down):
    loc = locals()
    W = {k: loc[k] for k in WEIGHTS}
    Mo = {k: loc["m_" + k] for k in WEIGHTS}
    Vo = {k: loc["v_" + k] for k in WEIGHTS}
    B, S, D = x.shape
    T = B * S
    xf = x.reshape(T, D)
    memf = mem.reshape(-1, D)
    tf = loss_target.reshape(T, D)
    cos, sin = rope_tables(positions.reshape(T, 1).astype(F32))

    cw = gather_all_devices(conv_w.reshape(DEPTH * SSD_CONV, -1), "gather_conv_w")[0::2]
    cw = cw.transpose(1, 0, 2).reshape(DEPTH, SSD_CONV, SSD_CONV_DIM)
    sms = [{k: W[k][l] for k in SMALL} for l in range(DEPTH)]
    me = _me_chip()

    cw, w_in_first = lax.optimization_barrier((cw, W["w_in"][0]))

    def gather_start(l):
        shards = [(wq_to_storage(W[k][l]) if k == "w_q_b" else W[k][l]).astype(BF16) for k in GATHERED]
        if l == 0:
            shards[0] = w_in_first.astype(BF16)
        lands = [lax.dynamic_update_slice(lax.empty((4,) + s.shape, BF16), s[None], (me, 0, 0)) for s in shards]
        return ici_start("gather", [], lands, "gather_start_l%d" % l)

    def gather_end(state, names, after, tag):
        idx = [GATHERED.index(k) for k in names]
        pick3 = lambda sems: [sems[3 * w + j] for w in idx for j in range(3)]
        sub = dict(kind=state["kind"], srcs=[], lands=[state["lands"][w] for w in idx],
                   ssem=pick3(state["ssem"]), rsem=pick3(state["rsem"]))
        _, lands = ici_wait(sub, after, "gather_wait_" + tag)
        return {k: z[None] for k, z in zip(names, forward_pair(lands, "gather_forward_" + tag))}

    class Weights(dict):
        def __init__(self, state, groups, after, tag):
            super().__init__(gather_end(state, groups[0], after, tag + "a"))
            self.state, self.later, self.after, self.tag = state, list(groups[1:]), after, tag

        def mark(self, value):
            self.after = value

        def __missing__(self, k):
            names = self.later.pop(0)
            self.update(gather_end(self.state, names, self.after, self.tag + "bcdef"[len(self.later)]))
            return self[k]

    FIRST_GROUPS = (("w_in",), ("w_q_b", "w_kv_b", "w_out", "w_xq", "w_xk", "w_xv", "w_xo"), ("w_gate", "w_up", "w_down"))

    saved, Gs, kws = [], [], []
    h = xf
    state, token = gather_start(0)
    for l in range(DEPTH):
        Gl = Weights(state, FIRST_GROUPS if l == 0 else (GATHERED,), h, "l%d" % l)
        kwl = dict(wcat=wcat_from_storage(Gl["w_in"][0]), conv_w=jnp.pad(cw[l], ((0, 8 - SSD_CONV), (0, 0))))
        Gs.append(Gl)
        kws.append(kwl)
        if l + 1 < DEPTH:
            state, token = gather_start(l + 1)
        h, sv = layer_fwd(h, memf, cos, sin, Gl, kwl, sms[l], B, S, l, dep=token if l + 1 < DEPTH else None)
        saved.append(sv)
    dh, lpart, dhb = loss_head(h, tf)
    loss = lax.psum(lpart[0, 0], ("x", "y", "c"))

    grads, delta, new_m, new_v = {}, {}, {}, {}
    chain = {k: None for k in GATHERED}
    gsmall = [None] * DEPTH
    pending = [[] for _ in range(DEPTH)]

    in_pair = []

    def flush(after):
        if not in_pair:
            return None
        pair_state, l, tag = in_pair.pop()
        rs_state, tok = reduce_scatter_begin(pair_state, after, tag)
        pending[l].append((rs_state, tag))
        return tok

    def begin(gw, l, tag):
        tok_chips = flush(next(iter(gw.values())))
        pair_state, tok = reduce_scatter_pair(gw, "l%d%s" % (l, tag))
        in_pair.append((pair_state, l, "l%d%s" % (l, tag)))
        return tok if tok_chips is None else tok + tok_chips

    def finish(l, after):
        for rs_state, tag in pending[l]:
            red = reduce_scatter_end(rs_state, after, tag)
            for k, (joined, own) in red.items():
                if k == "w_q_b":
                    joined, own = wq_from_storage(fill_own_half(joined, own)), None
                chain[k] = adamw_layer(W[k], joined, Mo[k], Vo[k], l, chain[k], "adamw_%s_l%d" % (k, l), g_own=own)

    token = None
    for l in reversed(range(DEPTH)):
        dh, dhb, gw, gs = layer_bwd(dh, dhb, saved[l], memf, cos, sin, Gs[l], kws[l], sms[l], B, S, l, dep=token,
                                    rs_hook=functools.partial(begin, l=l, tag="_ffn"))
        saved[l] = None
        gsmall[l] = gs
        token = begin(gw, l, "_mix")
        if l + 1 < DEPTH:
            finish(l + 1, token)
    grad_x = dh.reshape(B, S, D)
    flush(dh)

    spack, _ = _small_pack(gsmall)
    allv = gather_all_devices(spack, "gather_small_grads")
    ssum = sum_arrays([allv[i] for i in range(8)], F32, "sum_small_grads")
    sizes = {k: int(np.prod(W[k].shape[1:])) for k in SMALL}
    sizes["conv_w"] = SSD_CONV * SSD_CONV_DIM
    gsm = _small_unpack(ssum, sizes)
    for k in SMALL_RS:
        g = jnp.stack([gsm[l][k] for l in range(DEPTH)])
        if k == "conv_w":
            cs = SSD_CONV_DIM // 4
            g = lax.dynamic_slice_in_dim(g.reshape(DEPTH, SSD_CONV, SSD_CONV_DIM), _me_chip() * cs, cs, axis=2)
        shp = W[k].shape
        grads[k] = g.reshape(shp)
        d, nm, nv = adamw(_adam_2d(W[k]), _adam_2d(grads[k]), _adam_2d(Mo[k]), _adam_2d(Vo[k]), "adamw_" + k)
        delta[k], new_m[k], new_v[k] = d.reshape(shp), nm.reshape(shp), nv.reshape(shp)

    finish(0, lax.optimization_barrier((delta["ffn_norm_g"], chain["w_down"][1], chain["w_gate"][1], chain["w_up"][1]))[0])
    for k in GATHERED:
        grads[k], delta[k], new_m[k], new_v[k] = chain[k]

    return (loss, grad_x, *[grads[k] for k in WEIGHTS], *[delta[k] for k in WEIGHTS], *[new_m[k] for k in WEIGHTS],
            *[new_v[k] for k in WEIGHTS])
```

```python
import functools
import math

import numpy as np
import jax
import jax.numpy as jnp
from jax import lax
from jax.experimental import pallas as pl
from jax.experimental.pallas import tpu as pltpu

F32 = jnp.float32
BF16 = jnp.bfloat16
MESH = pl.DeviceIdType.MESH
HI = lax.Precision.HIGHEST

D_MODEL = 2048
DEPTH = 4
SSD_INNER = 1024
SSD_HEAD_DIM = 64
SSD_GROUPS = 2
SSD_HPG = 8
SSD_STATE = 128
SSD_CONV = 4
SSD_CHUNK = 128
SSD_CONV_DIM = 1536
SSD_GW = SSD_HPG * SSD_HEAD_DIM
MLA_HEADS = 8
MLA_NOPE = 128
MLA_ROPE = 64
MLA_QK = 192
MLA_V = 128
Q_LORA = 512
KV_LORA = 512
ROPE_THETA = 10000.0
X_HEADS = 4
X_HEAD_DIM = 128
X_INNER = 512
FFN = 5632
IN_COLS = 3664
EPS = 1e-6
ADAM_LR, ADAM_B1, ADAM_B2, ADAM_EPS, ADAM_WD, ADAM_STEP = 0.001, 0.9, 0.999, 1e-08, 0.01, 10

C_Z, C_XBC, C_QA, C_KVA, C_KR, C_DT, CAT_COLS = 0, 1024, 2560, 3072, 3584, 3712, 3840

LANES = 128
VMEM_LIMIT = 56 * 1024 * 1024
MM_VMEM_BUDGET = 40 * 1024 * 1024
MM_STEP_BYTES = 1024 * 1024
NORM_BLOCK_ELEMS = 512 * 1024
ATTN_SCORE_ELEMS = 1024 * 1024

SMALL = ("attn_norm_g", "conv_b", "dt_bias", "a_log", "d_skip", "ssd_norm_g", "q_a_norm_g", "kv_a_norm_g",
         "mla_q_norm_g", "mla_k_norm_g", "xattn_norm_g", "mem_norm_g", "xq_norm_g", "xk_norm_g", "ffn_norm_g")
SMALL_RS = SMALL + ("conv_w",)
WEIGHTS = ("attn_norm_g", "w_in", "conv_w", "conv_b", "dt_bias", "a_log", "d_skip", "ssd_norm_g", "q_a_norm_g", "w_q_b",
           "kv_a_norm_g", "w_kv_b", "mla_q_norm_g", "mla_k_norm_g", "w_out", "xattn_norm_g", "mem_norm_g", "w_xq", "w_xk",
           "w_xv", "xq_norm_g", "xk_norm_g", "w_xo", "ffn_norm_g", "w_gate", "w_up", "w_down")


def _pcall(body, **kw):
    return pl.pallas_call(body, **kw)


def _params(sem):
    return pltpu.CompilerParams(dimension_semantics=sem, vmem_limit_bytes=VMEM_LIMIT)


def _tile(n, cands):
    for t in cands:
        if n % t == 0:
            return t
    return n


def _sigmoid(z):
    return 1.0 / (1.0 + jnp.exp(-z))


def _mtile(n, cap):
    if n % LANES:
        return n
    q = n // LANES
    t = LANES * max(d for d in range(1, q + 1) if q % d == 0 and LANES * d <= cap)
    return n if (t < 512 and n <= 2048) else t


def _divisor_tiles(n, cap):
    if n % LANES:
        return [n]
    q = n // LANES
    return [LANES * d for d in range(1, q + 1) if q % d == 0 and LANES * d <= cap] or [n]


def _mm_tiles(M, N, K, m_unit, n_unit, k_unit, a_item, b_item, o_item, has_add):
    best = None
    for tm in _divisor_tiles(m_unit, 2048):
        for tn in _divisor_tiles(n_unit, 2048):
            for tk in _divisor_tiles(k_unit, 2048):
                nk = K // tk
                vmem = 2 * (tm * tk * a_item + tk * tn * b_item + tm * tn * (o_item + (4 if has_add else 0)))
                vmem += tm * tn * 4 * (2 if nk > 1 else 1) + (tm * tk * 2 if a_item == 4 else 0) + (tk * tn * 2 if b_item == 4 else 0)
                if vmem > MM_VMEM_BUDGET:
                    continue
                steps = (M // tm) * (N // tn) * nk
                traffic = (M * K * a_item * (N // tn if nk > 1 else 1) + K * N * b_item * (M // tm)
                           + M * N * (o_item + (4 if has_add else 0)) + steps * MM_STEP_BYTES)
                if nk > 1:
                    traffic += steps * tm * tn * 4
                if best is None or traffic < best[0]:
                    best = (traffic, tm, tn, tk)
    assert best is not None, (M, N, K)
    return best[1:]


def mm(a, b, mode, name, add=None, out_dtype=F32, wv=None, out_wv=None, dep=None):
    kind, l = wv if wv is not None else (None, None)
    if kind == "row":
        b, kind = b[l].reshape(-1, b.shape[-1]), None
    if out_wv == "row":
        res = mm(a, b, mode, name, add=add, out_dtype=out_dtype, wv=wv if kind else None, dep=dep)
        return res.reshape(4, res.shape[0] // 4, res.shape[1])
    if kind == "col":
        _, _, r, c = b.shape
        bshape = (r, 4 * c)
    else:
        bshape = b.shape
    if mode == "nn":
        (M, K), (K2, N) = a.shape, bshape
    elif mode == "nt":
        (M, K), (N, K2) = a.shape, bshape
    else:
        (K, M), (K2, N) = a.shape, bshape
    assert K == K2, (a.shape, bshape, mode)
    m_unit = M
    n_unit = N // 4 if (out_wv == "col" or (kind == "col" and mode == "nn")) else N
    k_unit = K // 4 if (kind == "col" and mode == "nt") else K
    tm, tn, tk = _mm_tiles(M, N, K, m_unit, n_unit, k_unit, a.dtype.itemsize, b.dtype.itemsize,
                           jnp.dtype(out_dtype).itemsize, add is not None)
    nk = K // tk
    qm, qn, qk = m_unit // tm, n_unit // tn, k_unit // tk
    dn = {"nn": (((1,), (0,)), ((), ())), "nt": (((1,), (1,)), ((), ())), "tn": (((0,), (0,)), ((), ()))}[mode]
    has_add = add is not None

    def body(a_ref, b_ref, *rest):
        rest = list(rest)
        c_ref = rest.pop(0) if has_add else None
        if dep is not None:
            rest.pop(0)
        o_ref = rest.pop(0)
        d = lax.dot_general(a_ref[...].astype(BF16), b_ref[...].astype(BF16), dn, preferred_element_type=F32)

        def finish(res):
            if has_add:
                res = res + c_ref[...].astype(F32)
            o_ref[...] = res.astype(out_dtype)

        if nk == 1:
            finish(d)
            return
        acc = rest.pop(0)
        k = pl.program_id(2)

        @pl.when(k == 0)
        def _():
            acc[...] = d

        if nk > 2:
            @pl.when((k > 0) & (k < nk - 1))
            def _():
                acc[...] += d

        @pl.when(k == nk - 1)
        def _():
            finish(acc[...] + d)

    if mode == "tn":
        a_spec = pl.BlockSpec((tk, tm), lambda i, j, k: (k, i))
    else:
        a_spec = pl.BlockSpec((tm, tk), lambda i, j, k: (i, k))
    if kind is None:
        b_spec = (pl.BlockSpec((tn, tk), lambda i, j, k: (j, k)) if mode == "nt"
                  else pl.BlockSpec((tk, tn), lambda i, j, k: (k, j)))
    elif mode == "nn":
        b_spec = pl.BlockSpec((None, None, tk, tn), lambda i, j, k: (l, j // qn, k, j % qn))
    elif mode == "nt":
        b_spec = pl.BlockSpec((None, None, tn, tk), lambda i, j, k: (l, k // qk, j, k % qk))
    else:
        raise ValueError((mode, kind))
    if out_wv == "col":
        o_spec = pl.BlockSpec((None, tm, tn), lambda i, j, k: (j // qn, i, j % qn))
        out_shape = jax.ShapeDtypeStruct((4, M, N // 4), out_dtype)
    else:
        o_spec = pl.BlockSpec((tm, tn), lambda i, j, k: (i, j))
        out_shape = jax.ShapeDtypeStruct((M, N), out_dtype)
    in_specs = [a_spec, b_spec] + ([o_spec] if has_add else []) + ([pl.BlockSpec(memory_space=pl.ANY)] if dep is not None else [])
    args = (a, b) + ((add,) if has_add else ()) + ((dep,) if dep is not None else ())
    return _pcall(
        body, name=name, grid=(M // tm, N // tn, nk), in_specs=in_specs, out_specs=o_spec,
        out_shape=out_shape, scratch_shapes=[pltpu.VMEM((tm, tn), F32)] if nk > 1 else [],
        compiler_params=_params(("parallel", "parallel", "arbitrary")))(*args)


def norm_fwd(x, gain, W, shared, name, gate=None, out_dtype=BF16, dep=None):
    T, C = x.shape
    nb = C // W
    tq = _tile(T, tuple(t for t in (4096, 2048, 1024, 512, 256, 128) if t * W <= NORM_BLOCK_ELEMS))
    g2 = gain.reshape(1, -1).astype(F32)
    gated = gate is not None

    def body(*refs):
        o_ref = refs[-1]
        if gated:
            x_ref, z_ref, g_ref = refs[:3]
        else:
            x_ref, g_ref = refs[:2]
        u = x_ref[...].astype(F32)
        if gated:
            z = z_ref[...]
            u = u * (z * _sigmoid(z))
        ms = jnp.mean(u * u, axis=-1, keepdims=True)
        o_ref[...] = (u * lax.rsqrt(ms + EPS) * g_ref[...]).astype(out_dtype)

    xs = pl.BlockSpec((tq, W), lambda j, i: (i, j))
    gs = pl.BlockSpec((1, W), (lambda j, i: (0, 0)) if shared else (lambda j, i: (0, j)))
    in_specs = [xs] + ([xs] if gated else []) + [gs] + ([pl.BlockSpec(memory_space=pl.ANY)] if dep is not None else [])
    args = (x,) + ((gate,) if gated else ()) + (g2,) + ((dep,) if dep is not None else ())
    return _pcall(body, name=name, grid=(nb, T // tq), in_specs=in_specs, out_specs=xs,
                  out_shape=jax.ShapeDtypeStruct((T, C), out_dtype),
                  compiler_params=_params(("parallel", "parallel")))(*args)


def norm_bwd(x, gain, dy, W, shared, name, gate=None, add=None, bf16_copy=False):
    T, C = x.shape
    nb = C // W
    tq = _tile(T, tuple(t for t in (4096, 2048, 1024, 512, 256, 128) if t * W <= NORM_BLOCK_ELEMS))
    g2 = gain.reshape(1, -1).astype(F32)
    gated = gate is not None
    has_add = add is not None
    GC = W if shared else C

    def body(*refs):
        refs = list(refs)
        x_ref = refs.pop(0)
        z_ref = refs.pop(0) if gated else None
        g_ref = refs.pop(0)
        dy_ref = refs.pop(0)
        a_ref = refs.pop(0) if has_add else None
        dx_ref = refs.pop(0)
        dz_ref = refs.pop(0) if gated else None
        dg_ref = refs.pop(0)
        j, i = pl.program_id(0), pl.program_id(1)
        first = (i == 0) & (j == 0) if shared else (i == 0)

        @pl.when(first)
        def _():
            dg_ref[...] = jnp.zeros_like(dg_ref)

        xv = x_ref[...].astype(F32)
        if gated:
            z = z_ref[...]
            s = _sigmoid(z)
            u = xv * (z * s)
        else:
            u = xv
        rstd = lax.rsqrt(jnp.mean(u * u, axis=-1, keepdims=True) + EPS)
        uh = u * rstd
        dyv = dy_ref[...].astype(F32)
        dg_ref[...] += jnp.sum(dyv * uh, axis=0, keepdims=True)
        dyg = dyv * g_ref[...]
        du = rstd * (dyg - uh * jnp.mean(dyg * uh, axis=-1, keepdims=True))
        if gated:
            dx = du * (z * s)
            dz_ref[...] = du * xv * (s * (1.0 + z * (1.0 - s)))
        else:
            dx = du
        if has_add:
            dx = dx + a_ref[...]
        dx_ref[...] = dx
        if bf16_copy:
            refs.pop(0)[...] = dx.astype(BF16)

    xs = pl.BlockSpec((tq, W), lambda j, i: (i, j))
    gs = pl.BlockSpec((1, W), (lambda j, i: (0, 0)) if shared else (lambda j, i: (0, j)))
    in_specs = [xs] + ([xs] if gated else []) + [gs, xs] + ([xs] if has_add else [])
    args = (x,) + ((gate,) if gated else ()) + (g2, dy) + ((add,) if has_add else ())
    out_specs = [xs] + ([xs] if gated else []) + [gs] + ([xs] if bf16_copy else [])
    out_shape = [jax.ShapeDtypeStruct((T, C), F32)] + ([jax.ShapeDtypeStruct((T, C), F32)] if gated else []) \
        + [jax.ShapeDtypeStruct((1, GC), F32)] + ([jax.ShapeDtypeStruct((T, C), BF16)] if bf16_copy else [])
    return _pcall(body, name=name, grid=(nb, T // tq), in_specs=in_specs, out_specs=out_specs, out_shape=out_shape,
                  compiler_params=_params(("arbitrary", "arbitrary")))(*args)


def _inv_freq_row():
    inv = 1.0 / (ROPE_THETA ** (np.arange(0, MLA_ROPE, 2, dtype=np.float32) / MLA_ROPE))
    row = np.zeros((1, LANES), np.float32)
    row[0, :32] = inv
    row[0, 32:64] = inv
    return jnp.asarray(row)


def rope_tables(pos):
    T = pos.shape[0]
    tq = _tile(T, (512, 256, 128))

    def body(p_ref, f_ref, c_ref, s_ref):
        ang = p_ref[...] * f_ref[...]
        lane = lax.broadcasted_iota(jnp.int32, ang.shape, 1)
        c_ref[...] = jnp.where(lane < 64, jnp.cos(ang), 0.0)
        sn = jnp.sin(ang)
        s_ref[...] = jnp.where(lane < 32, -sn, jnp.where(lane < 64, sn, 0.0))

    ts = pl.BlockSpec((tq, LANES), lambda i: (i, 0))
    return _pcall(body, name="rope_tables", grid=(T // tq,),
                  in_specs=[pl.BlockSpec((tq, 1), lambda i: (i, 0)), pl.BlockSpec((1, LANES), lambda i: (0, 0))],
                  out_specs=[ts, ts], out_shape=[jax.ShapeDtypeStruct((T, LANES), F32)] * 2,
                  compiler_params=_params(("parallel",)))(pos, _inv_freq_row())


def _swap_halves(r):
    lane = lax.broadcasted_iota(jnp.int32, r.shape, 1)
    return jnp.where(lane < 32, pltpu.roll(r, 96, 1), pltpu.roll(r, 32, 1))


def qk_fwd(nope_arr, nope_blk0, rope_arr, rope_blk0, rope_shared, gn, gr, cos, sin, name):
    T = nope_arr.shape[0]
    tq = _tile(T, (256, 128))

    col = lambda blk: slice(blk * LANES, (blk + 1) * LANES)

    def body(n_ref, r_ref, gn_ref, gr_ref, c_ref, s_ref, o_ref):
        for h in range(MLA_HEADS):
            n = n_ref[:, col(nope_blk0(h))]
            r = r_ref[...] if rope_shared else n_ref[:, col(rope_blk0(h))]
            ms = (jnp.sum(n * n, axis=-1, keepdims=True) + jnp.sum(r * r, axis=-1, keepdims=True)) * (1.0 / MLA_QK)
            rstd = lax.rsqrt(ms + EPS)
            rn = r * rstd * gr_ref[...]
            rr = rn * c_ref[...] + _swap_halves(rn) * s_ref[...]
            o_ref[:, col(2 * h)] = (n * rstd * gn_ref[...]).astype(BF16)
            o_ref[:, col(2 * h + 1)] = rr.astype(BF16)

    wide = lambda arr: pl.BlockSpec((tq, arr.shape[1]), lambda i: (i, 0))
    assert rope_shared or rope_arr is nope_arr
    rspec = pl.BlockSpec((tq, LANES), lambda i: (i, rope_blk0)) if rope_shared else pl.BlockSpec((8, LANES), lambda i: (0, 0))
    row = pl.BlockSpec((1, LANES), lambda i: (0, 0))
    tab = pl.BlockSpec((tq, LANES), lambda i: (i, 0))
    return _pcall(body, name=name, grid=(T // tq,),
                  in_specs=[wide(nope_arr), rspec, row, row, tab, tab],
                  out_specs=pl.BlockSpec((tq, MLA_HEADS * 2 * LANES), lambda i: (i, 0)),
                  out_shape=jax.ShapeDtypeStruct((T, MLA_HEADS * 2 * LANES), BF16),
                  compiler_params=_params(("parallel",)))(nope_arr, rope_arr, gn, gr, cos, sin)


def qk_bwd(nope_arr, nope_blk0, rope_arr, rope_blk0, rope_shared, gn, gr, cos, sin, dout, name, fill=None, fill_blk=None):
    T = nope_arr.shape[0]
    tq = _tile(T, (256, 128))
    col = lambda blk: slice(blk * LANES, (blk + 1) * LANES)

    def body(*refs):
        refs = list(refs)
        n_ref, r_ref, gn_ref, gr_ref, c_ref, s_ref, d_ref = refs[:7]
        f_ref = refs[7] if rope_shared else None
        outs = refs[8:] if rope_shared else refs[7:]
        draw_ref = outs[0]
        dsh_ref = outs[1] if rope_shared else None
        dgn_ref, dgr_ref = outs[-2:]

        @pl.when(pl.program_id(0) == 0)
        def _():
            dgn_ref[...] = jnp.zeros_like(dgn_ref)
            dgr_ref[...] = jnp.zeros_like(dgr_ref)

        dgn = jnp.zeros((1, LANES), F32)
        dgr = jnp.zeros((1, LANES), F32)
        dshared = jnp.zeros((tq, LANES), F32)
        for h in range(MLA_HEADS):
            n = n_ref[:, col(nope_blk0(h))]
            r = r_ref[...] if rope_shared else n_ref[:, col(rope_blk0(h))]
            ms = (jnp.sum(n * n, axis=-1, keepdims=True) + jnp.sum(r * r, axis=-1, keepdims=True)) * (1.0 / MLA_QK)
            rstd = lax.rsqrt(ms + EPS)
            nh = n * rstd
            rh = r * rstd
            d_n = d_ref[:, col(2 * h)].astype(F32)
            d_rr = d_ref[:, col(2 * h + 1)].astype(F32)
            d_rn = d_rr * c_ref[...] - _swap_halves(d_rr) * s_ref[...]
            dgn = dgn + jnp.sum(d_n * nh, axis=0, keepdims=True)
            dgr = dgr + jnp.sum(d_rn * rh, axis=0, keepdims=True)
            dng = d_n * gn_ref[...]
            drg = d_rn * gr_ref[...]
            mean = (jnp.sum(dng * nh, axis=-1, keepdims=True) + jnp.sum(drg * rh, axis=-1, keepdims=True)) * (1.0 / MLA_QK)
            draw_ref[:, col(nope_blk0(h))] = (rstd * (dng - nh * mean)).astype(BF16)
            drope = rstd * (drg - rh * mean)
            if rope_shared:
                dshared = dshared + drope
                draw_ref[:, col(fill_blk(h))] = f_ref[:, col(h)].astype(BF16)
            else:
                draw_ref[:, col(rope_blk0(h))] = drope.astype(BF16)
        dgn_ref[...] += dgn
        dgr_ref[...] += dgr
        if rope_shared:
            dsh_ref[...] = dshared

    wide = lambda arr: pl.BlockSpec((tq, arr.shape[1]), lambda i: (i, 0))
    assert rope_shared or rope_arr is nope_arr
    rspec = pl.BlockSpec((tq, LANES), lambda i: (i, rope_blk0)) if rope_shared else pl.BlockSpec((8, LANES), lambda i: (0, 0))
    row = pl.BlockSpec((1, LANES), lambda i: (0, 0))
    tab = pl.BlockSpec((tq, LANES), lambda i: (i, 0))
    raw = pl.BlockSpec((tq, 2 * MLA_HEADS * LANES), lambda i: (i, 0))
    in_specs = [wide(nope_arr), rspec, row, row, tab, tab, wide(dout)] + ([wide(fill)] if rope_shared else [])
    out_specs = [raw] + ([tab] if rope_shared else []) + [row, row]
    out_shape = ([jax.ShapeDtypeStruct((T, 2 * MLA_HEADS * LANES), BF16)]
                 + ([jax.ShapeDtypeStruct((T, LANES), F32)] if rope_shared else [])
                 + [jax.ShapeDtypeStruct((1, LANES), F32)] * 2)
    args = (nope_arr, rope_arr, gn, gr, cos, sin, dout) + ((fill,) if rope_shared else ())
    return _pcall(body, name=name, grid=(T // tq,), in_specs=in_specs, out_specs=out_specs, out_shape=out_shape,
                  compiler_params=_params(("arbitrary",)))(*args)


def _scores(q, k, scale, causal, row0):
    s = lax.dot_general(q, k, (((1,), (1,)), ((), ())), preferred_element_type=F32) * scale
    if causal:
        qi = row0 + lax.broadcasted_iota(jnp.int32, s.shape, 0)
        ki = lax.broadcasted_iota(jnp.int32, s.shape, 1)
        s = jnp.where(qi >= ki, s, -1e30)
    return s


def attn_fwd(q, k, v, H, dqk, dv, scale, causal, name, v_idx=lambda h: h):
    B, S, _ = q.shape
    Sk = k.shape[1]
    tq = _tile(S, tuple(t for t in (2048, 1024, 512, 256, 128) if t * Sk <= ATTN_SCORE_ELEMS))

    def tile(q_ref, k_ref, v_ref, o_ref, row0, kl):
        s = _scores(q_ref[0], k_ref[0, :kl], scale, causal, row0)
        p = jnp.exp(s - jnp.max(s, axis=-1, keepdims=True))
        l = jnp.sum(p, axis=-1, keepdims=True)
        o = lax.dot_general(p.astype(BF16), v_ref[0, :kl], (((1,), (0,)), ((), ())), preferred_element_type=F32)
        o_ref[0] = (o / l).astype(BF16)

    def body(q_ref, k_ref, v_ref, o_ref):
        if not causal:
            tile(q_ref, k_ref, v_ref, o_ref, 0, Sk)
            return
        for qi in range(S // tq):
            @pl.when(pl.program_id(2) == qi)
            def _(qi=qi):
                tile(q_ref, k_ref, v_ref, o_ref, qi * tq, (qi + 1) * tq)

    return _pcall(body, name=name, grid=(B, H, S // tq),
                  in_specs=[pl.BlockSpec((1, tq, dqk), lambda b, h, i: (b, i, h)),
                            pl.BlockSpec((1, Sk, dqk), lambda b, h, i: (b, 0, h)),
                            pl.BlockSpec((1, Sk, dv), lambda b, h, i: (b, 0, v_idx(h)))],
                  out_specs=pl.BlockSpec((1, tq, dv), lambda b, h, i: (b, i, h)),
                  out_shape=jax.ShapeDtypeStruct((B, S, H * dv), BF16),
                  compiler_params=_params(("parallel", "parallel", "parallel")))(q, k, v)


def attn_bwd(q, k, v, do, H, dqk, dv, scale, causal, name, v_idx=lambda h: h):
    B, S, _ = q.shape
    Sk = k.shape[1]
    tq = _tile(S, tuple(t for t in (2048, 1024, 512, 256, 128) if t * Sk <= ATTN_SCORE_ELEMS))

    def body(q_ref, k_ref, v_ref, do_ref, dq_ref, dk_ref, dv_ref):
        i = pl.program_id(2)

        @pl.when(i == 0)
        def _():
            dk_ref[...] = jnp.zeros_like(dk_ref)
            dv_ref[...] = jnp.zeros_like(dv_ref)

        def tile(row0, kl):
            qv, kv, vv = q_ref[0], k_ref[0, :kl], v_ref[0, :kl]
            s = _scores(qv, kv, scale, causal, row0)
            p = jnp.exp(s - jnp.max(s, axis=-1, keepdims=True))
            p = p / jnp.sum(p, axis=-1, keepdims=True)
            dob = do_ref[0].astype(BF16)
            pb = p.astype(BF16)
            dv_ref[0, :kl] += lax.dot_general(pb, dob, (((0,), (0,)), ((), ())), preferred_element_type=F32)
            dp = lax.dot_general(dob, vv, (((1,), (1,)), ((), ())), preferred_element_type=F32)
            delta = jnp.sum(p * dp, axis=-1, keepdims=True)
            ds = (p * (dp - delta) * scale).astype(BF16)
            dq_ref[0] = lax.dot_general(ds, kv, (((1,), (0,)), ((), ())), preferred_element_type=F32)
            dk_ref[0, :kl] += lax.dot_general(ds, qv, (((0,), (0,)), ((), ())), preferred_element_type=F32)

        if not causal:
            tile(0, Sk)
            return
        for qi in range(S // tq):
            @pl.when(i == qi)
            def _(qi=qi):
                tile(qi * tq, (qi + 1) * tq)

    qs = pl.BlockSpec((1, tq, dqk), lambda b, h, i: (b, i, h))
    ks = pl.BlockSpec((1, Sk, dqk), lambda b, h, i: (b, 0, h))
    vs = pl.BlockSpec((1, Sk, dv), lambda b, h, i: (b, 0, h))
    vin = pl.BlockSpec((1, Sk, dv), lambda b, h, i: (b, 0, v_idx(h)))
    return _pcall(body, name=name, grid=(B, H, S // tq),
                  in_specs=[qs, ks, vin, pl.BlockSpec((1, tq, dv), lambda b, h, i: (b, i, h))],
                  out_specs=[qs, ks, vs],
                  out_shape=[jax.ShapeDtypeStruct((B, S, H * dqk), F32), jax.ShapeDtypeStruct((B, Sk, H * dqk), F32),
                             jax.ShapeDtypeStruct((B, Sk, H * dv), F32)],
                  compiler_params=_params(("parallel", "parallel", "arbitrary")))(q, k, v, do)


def _shift_down(u, j):
    if j == 0:
        return u
    row = lax.broadcasted_iota(jnp.int32, u.shape, 0)
    return jnp.where(row >= j, pltpu.roll(u, j, 0), 0.0)


def _shift_up(u, j):
    if j == 0:
        return u
    n = u.shape[0]
    row = lax.broadcasted_iota(jnp.int32, u.shape, 0)
    return jnp.where(row < n - j, pltpu.roll(u, n - j, 0), 0.0)


def conv_fwd(u, col0_blk, w8, b, name):
    B, S, _ = u.shape
    tc = 256
    nb = SSD_CONV_DIM // tc

    def body(u_ref, w_ref, b_ref, o_ref):
        uv = u_ref[0]
        pre = b_ref[...] + jnp.zeros_like(uv)
        for kk in range(SSD_CONV):
            pre = pre + w_ref[kk:kk + 1, :] * _shift_down(uv, SSD_CONV - 1 - kk)
        o_ref[0] = pre * _sigmoid(pre)

    return _pcall(body, name=name, grid=(B, nb),
                  in_specs=[pl.BlockSpec((1, S, tc), lambda bb, j: (bb, 0, col0_blk + j)),
                            pl.BlockSpec((8, tc), lambda bb, j: (0, j)), pl.BlockSpec((1, tc), lambda bb, j: (0, j))],
                  out_specs=pl.BlockSpec((1, S, tc), lambda bb, j: (bb, 0, j)),
                  out_shape=jax.ShapeDtypeStruct((B, S, SSD_CONV_DIM), F32),
                  compiler_params=_params(("parallel", "parallel")))(u, w8, b)


def conv_bwd(u, col0_blk, w8, b, dy, name):
    B, S, _ = u.shape
    tc = 256
    nb = SSD_CONV_DIM // tc

    def body(u_ref, w_ref, b_ref, dy_ref, du_ref, dw_ref, db_ref):
        bb = pl.program_id(1)

        @pl.when(bb == 0)
        def _():
            dw_ref[...] = jnp.zeros_like(dw_ref)
            db_ref[...] = jnp.zeros_like(db_ref)

        uv = u_ref[0]
        pre = b_ref[...] + jnp.zeros_like(uv)
        for kk in range(SSD_CONV):
            pre = pre + w_ref[kk:kk + 1, :] * _shift_down(uv, SSD_CONV - 1 - kk)
        s = _sigmoid(pre)
        dpre = dy_ref[0] * (s * (1.0 + pre * (1.0 - s)))
        du = jnp.zeros_like(uv)
        for kk in range(SSD_CONV):
            j = SSD_CONV - 1 - kk
            du = du + w_ref[kk:kk + 1, :] * _shift_up(dpre, j)
            dw_ref[kk:kk + 1, :] += jnp.sum(dpre * _shift_down(uv, j), axis=0, keepdims=True)
        db_ref[...] += jnp.sum(dpre, axis=0, keepdims=True)
        du_ref[0] = du

    return _pcall(body, name=name, grid=(nb, B),
                  in_specs=[pl.BlockSpec((1, S, tc), lambda j, bb: (bb, 0, col0_blk + j)),
                            pl.BlockSpec((8, tc), lambda j, bb: (0, j)), pl.BlockSpec((1, tc), lambda j, bb: (0, j)),
                            pl.BlockSpec((1, S, tc), lambda j, bb: (bb, 0, j))],
                  out_specs=[pl.BlockSpec((1, S, tc), lambda j, bb: (bb, 0, j)),
                             pl.BlockSpec((8, tc), lambda j, bb: (0, j)), pl.BlockSpec((1, tc), lambda j, bb: (0, j))],
                  out_shape=[jax.ShapeDtypeStruct((B, S, SSD_CONV_DIM), F32), jax.ShapeDtypeStruct((8, SSD_CONV_DIM), F32),
                             jax.ShapeDtypeStruct((1, SSD_CONV_DIM), F32)],
                  compiler_params=_params(("arbitrary", "arbitrary")))(u, w8, b, dy)


def _expand_mat():
    e = np.zeros((LANES, SSD_GW), np.float32)
    for h in range(SSD_HPG):
        e[h, h * SSD_HEAD_DIM:(h + 1) * SSD_HEAD_DIM] = 1.0
    return jnp.asarray(e)


def _tri_mat():
    return jnp.asarray(np.tril(np.ones((SSD_CHUNK, SSD_CHUNK), np.float32)))


def _split3(x):
    hi = x.astype(BF16)
    r = x - hi.astype(F32)
    mid = r.astype(BF16)
    return hi, mid, (r - mid.astype(F32)).astype(BF16)


def _dotf(a, b, dn=(((1,), (0,)), ((), ())), sel="b"):
    if sel == "b":
        s = b.astype(BF16)
        parts = [lax.dot_general(p, s, dn, preferred_element_type=F32) for p in _split3(a)]
    else:
        s = a.astype(BF16)
        parts = [lax.dot_general(s, p, dn, preferred_element_type=F32) for p in _split3(b)]
    return parts[0] + parts[1] + parts[2]


def _dotb(a, b, dn=(((1,), (0,)), ((), ()))):
    return lax.dot_general(a.astype(BF16), b.astype(BF16), dn, preferred_element_type=F32)


NT = (((1,), (1,)), ((), ()))
TN = (((0,), (0,)), ((), ()))


def _ssd_common(dt_ref, par_ref, tri_ref, e_ref):
    dtr = dt_ref[0]
    xb = dtr + par_ref[0:1, :]
    dt = jnp.maximum(xb, 0.0) + jnp.log(1.0 + jnp.exp(-jnp.abs(xb)))
    A = -jnp.exp(par_ref[1:2, :])
    a = dt * A
    acs = _dotf(tri_ref[...], a, sel="a")
    alast = acs[SSD_CHUNK - 1:SSD_CHUNK, :]
    E = e_ref[...]
    acsE = _dotf(acs, E)
    alastE = acsE[SSD_CHUNK - 1:SSD_CHUNK, :]
    return dict(xb=xb, dt=dt, A=A, acs=acs, acsT=acs.T, ealast=jnp.exp(alast), w=jnp.exp(alast - acs),
                dtE=_dotf(dt, E), eacsE=jnp.exp(acsE), wE=jnp.exp(alastE - acsE), ealastE=jnp.exp(alastE),
                DE=_dotf(par_ref[...], E)[2:3, :])


def _decay(c, e):
    diff = c["acs"][:, e:e + 1] - c["acsT"][e:e + 1, :]
    li = lax.broadcasted_iota(jnp.int32, diff.shape, 0)
    si = lax.broadcasted_iota(jnp.int32, diff.shape, 1)
    return jnp.where(li >= si, jnp.exp(jnp.minimum(diff, 0.0)), 0.0)


def ssd_fwd(xbc, dtp, par, name):
    B, S, _ = xbc.shape
    nc = S // SSD_CHUNK
    L, N, GW, P = SSD_CHUNK, SSD_STATE, SSD_GW, SSD_HEAD_DIM

    def body(x_ref, b_ref, c_ref, dt_ref, par_ref, tri_ref, e_ref, y_ref, st_ref, S_scr):
        @pl.when(pl.program_id(2) == 0)
        def _():
            S_scr[...] = jnp.zeros_like(S_scr)

        Sin = S_scr[...]
        st_ref[0, 0, 0] = Sin
        c = _ssd_common(dt_ref, par_ref.at[0], tri_ref, e_ref)
        X = x_ref[0]
        Bm = b_ref[0]
        Cm = c_ref[0]
        xdt = X * c["dtE"]
        CB = _dotb(Cm, Bm, NT)
        y_ref[0] = c["eacsE"] * _dotb(Cm, Sin) + c["DE"] * X
        for e in range(SSD_HPG):
            M = CB * _decay(c, e)
            y_ref[0, :, e * P:(e + 1) * P] += _dotb(M, xdt[:, e * P:(e + 1) * P])
        S_scr[...] = c["ealastE"] * Sin + _dotb(Bm, xdt * c["wE"], TN)

    return _pcall(
        body, name=name, grid=(SSD_GROUPS, B, nc),
        in_specs=[pl.BlockSpec((1, L, GW), lambda g, b, ci: (b, ci, g)),
                  pl.BlockSpec((1, L, N), lambda g, b, ci: (b, ci, 8 + g)),
                  pl.BlockSpec((1, L, N), lambda g, b, ci: (b, ci, 10 + g)),
                  pl.BlockSpec((1, L, LANES), lambda g, b, ci: (b, ci, g)),
                  pl.BlockSpec((1, 8, LANES), lambda g, b, ci: (g, 0, 0)),
                  pl.BlockSpec((L, L), lambda g, b, ci: (0, 0)),
                  pl.BlockSpec((LANES, GW), lambda g, b, ci: (0, 0))],
        out_specs=[pl.BlockSpec((1, L, GW), lambda g, b, ci: (b, ci, g)),
                   pl.BlockSpec((1, 1, 1, N, GW), lambda g, b, ci: (g, b, ci, 0, 0))],
        out_shape=[jax.ShapeDtypeStruct((B, S, SSD_INNER), F32), jax.ShapeDtypeStruct((SSD_GROUPS, B, nc, N, GW), F32)],
        scratch_shapes=[pltpu.VMEM((N, GW), F32)],
        compiler_params=_params(("arbitrary", "arbitrary", "arbitrary")))(xbc, xbc, xbc, dtp, par, _tri_mat(), _expand_mat())


def ssd_bwd(xbc, dtp, par, states, dy, name):
    B, S, _ = xbc.shape
    nc = S // SSD_CHUNK
    L, N, GW, P = SSD_CHUNK, SSD_STATE, SSD_GW, SSD_HEAD_DIM

    def body(x_ref, b_ref, c_ref, dt_ref, par_ref, tri_ref, e_ref, st_ref, dy_ref,
             dx_ref, db_ref, dc_ref, ddt_ref, dpar_ref, dS_scr, dxdt_scr):
        @pl.when(pl.program_id(2) == 0)
        def _():
            dS_scr[...] = jnp.zeros_like(dS_scr)

        @pl.when((pl.program_id(1) == 0) & (pl.program_id(2) == 0))
        def _():
            dpar_ref[...] = jnp.zeros_like(dpar_ref)

        c = _ssd_common(dt_ref, par_ref.at[0], tri_ref, e_ref)
        E = e_ref[...]
        red = lambda t: _dotf(t, E, NT)
        red_row = lambda t: red(jnp.broadcast_to(t, (8, GW)))[0:1, :]
        X, Bm, Cm = x_ref[0], b_ref[0], c_ref[0]
        Sin = st_ref[0, 0, 0]
        dY = dy_ref[0]
        dSo = dS_scr[...]
        xdt = X * c["dtE"]
        CB = _dotb(Cm, Bm, NT)
        dD = red_row(jnp.sum(dY * X, axis=0, keepdims=True))
        yoff = c["eacsE"] * _dotb(Cm, Sin)
        dacs = red(dY * yoff)
        dYe = dY * c["eacsE"]
        dC = _dotb(dYe, Sin, NT)
        dSin = _dotb(Cm, dYe, TN) + c["ealastE"] * dSo
        dB = _dotb(xdt * c["wE"], dSo, NT)
        dxw = _dotb(Bm, dSo)
        t = red(dxw * xdt) * c["w"]
        dacs = dacs - t
        dalast = jnp.sum(t, axis=0, keepdims=True) + c["ealast"] * red_row(jnp.sum(Sin * dSo, axis=0, keepdims=True))
        dxdt_scr[...] = dxw * c["wE"]
        dCB = jnp.zeros((L, L), F32)
        dacsT = jnp.zeros((LANES, L), F32)
        lane = lax.broadcasted_iota(jnp.int32, (1, LANES), 1)
        subl = lax.broadcasted_iota(jnp.int32, (LANES, 1), 0)
        for e in range(SSD_HPG):
            Lm = _decay(c, e)
            M = CB * Lm
            dYh = dY[:, e * P:(e + 1) * P]
            dM = _dotb(dYh, xdt[:, e * P:(e + 1) * P], NT)
            dxdt_scr[:, e * P:(e + 1) * P] += _dotb(M, dYh, TN)
            dCBe = dM * Lm
            dCB = dCB + dCBe
            Gm = dCBe * CB
            dacs = dacs + jnp.sum(Gm, axis=1, keepdims=True) * (lane == e).astype(F32)
            dacsT = dacsT - jnp.sum(Gm, axis=0, keepdims=True) * (subl == e).astype(F32)
        dC = dC + _dotb(dCB, Bm)
        dB = dB + _dotb(dCB, Cm, TN)
        dacs = dacs + dacsT.T
        row = lax.broadcasted_iota(jnp.int32, (L, 1), 0)
        dacs = dacs + jnp.where(row == L - 1, dalast, 0.0)
        da = _dotf(tri_ref[...], dacs, TN, sel="a")
        dxdt = dxdt_scr[...]
        ddt = da * c["A"] + red(dxdt * X)
        dA = jnp.sum(da * c["dt"], axis=0, keepdims=True)
        ddtr = ddt * _sigmoid(c["xb"])
        dx_ref[0] = c["DE"] * dY + dxdt * c["dtE"]
        db_ref[0] = dB
        dc_ref[0] = dC
        ddt_ref[0] = ddtr
        dpar_ref[0, 0:1, :] += jnp.sum(ddtr, axis=0, keepdims=True)
        dpar_ref[0, 1:2, :] += dA * c["A"]
        dpar_ref[0, 2:3, :] += dD
        dS_scr[...] = dSin

    rc = lambda ci: nc - 1 - ci
    xspec = pl.BlockSpec((1, L, GW), lambda g, b, ci: (b, rc(ci), g))
    return _pcall(
        body, name=name, grid=(SSD_GROUPS, B, nc),
        in_specs=[xspec,
                  pl.BlockSpec((1, L, N), lambda g, b, ci: (b, rc(ci), 8 + g)),
                  pl.BlockSpec((1, L, N), lambda g, b, ci: (b, rc(ci), 10 + g)),
                  pl.BlockSpec((1, L, LANES), lambda g, b, ci: (b, rc(ci), g)),
                  pl.BlockSpec((1, 8, LANES), lambda g, b, ci: (g, 0, 0)),
                  pl.BlockSpec((L, L), lambda g, b, ci: (0, 0)),
                  pl.BlockSpec((LANES, GW), lambda g, b, ci: (0, 0)),
                  pl.BlockSpec((1, 1, 1, N, GW), lambda g, b, ci: (g, b, rc(ci), 0, 0)),
                  xspec],
        out_specs=[xspec,
                   pl.BlockSpec((1, L, N), lambda g, b, ci: (b, rc(ci), g)),
                   pl.BlockSpec((1, L, N), lambda g, b, ci: (b, rc(ci), g)),
                   pl.BlockSpec((1, L, LANES), lambda g, b, ci: (b, rc(ci), g)),
                   pl.BlockSpec((1, 8, LANES), lambda g, b, ci: (g, 0, 0))],
        out_shape=[jax.ShapeDtypeStruct((B, S, SSD_INNER), F32), jax.ShapeDtypeStruct((B, S, SSD_GROUPS * N), F32),
                   jax.ShapeDtypeStruct((B, S, SSD_GROUPS * N), F32), jax.ShapeDtypeStruct((B, S, SSD_GROUPS * LANES), F32),
                   jax.ShapeDtypeStruct((SSD_GROUPS, 8, LANES), F32)],
        scratch_shapes=[pltpu.VMEM((N, GW), F32), pltpu.VMEM((L, GW), F32)],
        compiler_params=_params(("arbitrary", "arbitrary", "arbitrary")))(
            xbc, xbc, xbc, dtp, par, _tri_mat(), _expand_mat(), states, dy)


def gate_up_act(h, wg, wu, name):
    T, D = h.shape
    c = wg.shape[-1]
    tm = _tile(T, (512, 256, 128))

    def body(h_ref, g_w, u_w, g_ref, u_ref, a_ref):
        hv = h_ref[...]
        g = jnp.dot(hv, g_w[...], preferred_element_type=F32)
        u = jnp.dot(hv, u_w[...], preferred_element_type=F32)
        g_ref[...] = g.astype(BF16)
        u_ref[...] = u.astype(BF16)
        a_ref[...] = (g * _sigmoid(g) * u).astype(BF16)

    ws = pl.BlockSpec((None, None, D, c), lambda i, j: (0, j, 0, 0))
    os_ = pl.BlockSpec((tm, c), lambda i, j: (i, j))
    return _pcall(body, name=name, grid=(T // tm, 4), in_specs=[pl.BlockSpec((tm, D), lambda i, j: (i, 0)), ws, ws],
                  out_specs=[os_, os_, os_], out_shape=[jax.ShapeDtypeStruct((T, 4 * c), BF16)] * 3,
                  compiler_params=_params(("parallel", "parallel")))(h, wg, wu)


def gate_up_dx(dg, du, wg, wu, name):
    T = dg.shape[0]
    _, _, D, c = wg.shape
    tm, tn = _tile(T, (1024, 512, 256, 128)), _tile(D, (1024, 512, 256, 128))

    def body(g_ref, u_ref, gw_ref, uw_ref, o_ref, acc):
        k = pl.program_id(2)
        d = (lax.dot_general(g_ref[...], gw_ref[...], NT, preferred_element_type=F32)
             + lax.dot_general(u_ref[...], uw_ref[...], NT, preferred_element_type=F32))

        @pl.when(k == 0)
        def _():
            acc[...] = d

        @pl.when((k > 0) & (k < 3))
        def _():
            acc[...] += d

        @pl.when(k == 3)
        def _():
            o_ref[...] = acc[...] + d

    a_spec = pl.BlockSpec((tm, c), lambda i, j, k: (i, k))
    w_spec = pl.BlockSpec((None, None, tn, c), lambda i, j, k: (0, k, j, 0))
    return _pcall(body, name=name, grid=(T // tm, D // tn, 4), in_specs=[a_spec, a_spec, w_spec, w_spec],
                  out_specs=pl.BlockSpec((tm, tn), lambda i, j, k: (i, j)), out_shape=jax.ShapeDtypeStruct((T, D), F32),
                  scratch_shapes=[pltpu.VMEM((tm, tn), F32)],
                  compiler_params=_params(("parallel", "parallel", "arbitrary")))(dg, du, wg, wu)


def down_dx_swiglu(dy, w_down, gate, up, name, dep=None):
    T, D = dy.shape
    Fh = w_down.shape[0]
    tm, tn = _tile(T, (1024, 512, 256, 128)), 512

    def body(dy_ref, w_ref, g_ref, u_ref, *rest):
        dg_ref, du_ref = rest[-2:]
        d = lax.dot_general(dy_ref[...], w_ref[...], NT, preferred_element_type=F32)
        g = g_ref[...].astype(F32)
        u = u_ref[...].astype(F32)
        s = _sigmoid(g)
        dg_ref[...] = (d * u * (s * (1.0 + g * (1.0 - s)))).astype(BF16)
        du_ref[...] = (d * g * s).astype(BF16)

    ts = pl.BlockSpec((tm, tn), lambda i, j: (i, j))
    in_specs = [pl.BlockSpec((tm, D), lambda i, j: (i, 0)), pl.BlockSpec((tn, D), lambda i, j: (j, 0)), ts, ts]
    args = (dy, w_down, gate, up)
    if dep is not None:
        in_specs, args = in_specs + [pl.BlockSpec(memory_space=pl.ANY)], args + (dep,)
    return _pcall(body, name=name, grid=(T // tm, Fh // tn), in_specs=in_specs, out_specs=[ts, ts],
                  out_shape=[jax.ShapeDtypeStruct((T, Fh), BF16)] * 2,
                  compiler_params=_params(("parallel", "parallel")))(*args)


def swiglu_fwd(g, u, name):
    T = g.shape[0]
    tq, tc = _tile(T, (512, 256, 128)), 512
    nf = FFN // tc

    def body(g_ref, u_ref, o_ref):
        gv = g_ref[...].astype(F32)
        o_ref[...] = (gv * _sigmoid(gv) * u_ref[...].astype(F32)).astype(BF16)

    lo = pl.BlockSpec((tq, tc), lambda i, j: (i, j))
    return _pcall(body, name=name, grid=(T // tq, nf), in_specs=[lo, lo], out_specs=lo,
                  out_shape=jax.ShapeDtypeStruct((T, FFN), BF16), compiler_params=_params(("parallel", "parallel")))(g, u)


def swiglu_bwd(gate, up, dact, name):
    T = gate.shape[0]
    tq, tc = _tile(T, (512, 256, 128)), 512
    nf = FFN // tc

    def body(g_ref, u_ref, d_ref, dg_ref, du_ref):
        g = g_ref[...].astype(F32)
        u = u_ref[...].astype(F32)
        d = d_ref[...]
        s = _sigmoid(g)
        dg_ref[...] = (d * u * (s * (1.0 + g * (1.0 - s)))).astype(BF16)
        du_ref[...] = (d * g * s).astype(BF16)

    lo = pl.BlockSpec((tq, tc), lambda i, j: (i, j))
    return _pcall(body, name=name, grid=(T // tq, nf), in_specs=[lo, lo, lo], out_specs=[lo, lo],
                  out_shape=[jax.ShapeDtypeStruct((T, FFN), BF16)] * 2,
                  compiler_params=_params(("parallel", "parallel")))(gate, up, dact)


def loss_head(y, target):
    T, D = y.shape
    tq = _tile(T, (256, 128))

    def body(y_ref, t_ref, dy_ref, l_ref, dyb_ref):
        @pl.when(pl.program_id(0) == 0)
        def _():
            l_ref[...] = jnp.zeros_like(l_ref)

        e = y_ref[...] - t_ref[...]
        dy = e * (1.0 / D)
        dy_ref[...] = dy
        dyb_ref[...] = dy.astype(BF16)
        l_ref[...] += 0.5 * jnp.sum(jnp.mean(e * e, axis=-1, keepdims=True))

    xs = pl.BlockSpec((tq, D), lambda i: (i, 0))
    return _pcall(body, name="loss_head", grid=(T // tq,), in_specs=[xs, xs],
                  out_specs=[xs, pl.BlockSpec((8, LANES), lambda i: (0, 0)), xs],
                  out_shape=[jax.ShapeDtypeStruct((T, D), F32), jax.ShapeDtypeStruct((8, LANES), F32),
                             jax.ShapeDtypeStruct((T, D), BF16)],
                  compiler_params=_params(("arbitrary",)))(y, target)


def sum_arrays(arrs, out_dtype, name):
    R, C = arrs[0].shape
    tr = _tile(R, (512, 256, 128, 64, 32, 16, 8))
    n = len(arrs)

    def body(*refs):
        acc = refs[0][...].astype(F32)
        for r in refs[1:n]:
            acc = acc + r[...].astype(F32)
        refs[n][...] = acc.astype(out_dtype)

    xs = pl.BlockSpec((tr, C), lambda i: (i, 0))
    return _pcall(body, name=name, grid=(R // tr,), in_specs=[xs] * n, out_specs=xs,
                  out_shape=jax.ShapeDtypeStruct((R, C), out_dtype), compiler_params=_params(("parallel",)))(*arrs)


def adamw(w, g, m, v, name):
    R, C = w.shape
    tr = _tile(R, (256, 128, 64, 32, 16, 8))
    c1 = 1.0 - ADAM_B1 ** ADAM_STEP
    c2 = 1.0 - ADAM_B2 ** ADAM_STEP

    def body(w_ref, g_ref, m_ref, v_ref, d_ref, nm_ref, nv_ref):
        gv = g_ref[...]
        mn = ADAM_B1 * m_ref[...] + (1.0 - ADAM_B1) * gv
        vn = ADAM_B2 * v_ref[...] + (1.0 - ADAM_B2) * (gv * gv)
        nm_ref[...] = mn
        nv_ref[...] = vn
        d_ref[...] = -ADAM_LR * ((mn / c1) / (jnp.sqrt(vn / c2) + ADAM_EPS) + ADAM_WD * w_ref[...])

    xs = pl.BlockSpec((tr, C), lambda i: (i, 0))
    return _pcall(body, name=name, grid=(R // tr,), in_specs=[xs] * 4, out_specs=[xs] * 3,
                  out_shape=[jax.ShapeDtypeStruct((R, C), F32)] * 3, compiler_params=_params(("parallel",)))(w, g, m, v)


def adamw_layer(w, g, m, v, l, prev, name, g_own=None):
    if g_own is not None:
        return _adamw_layer_halves(w, g, g_own, m, v, l, prev, name)
    L, r, c = w.shape
    tr = _tile(r, (256, 128, 64, 32, 16, 8))
    c1 = 1.0 - ADAM_B1 ** ADAM_STEP
    c2 = 1.0 - ADAM_B2 ** ADAM_STEP
    chained = prev is not None

    def body(w_ref, g_ref, m_ref, v_ref, *rest):
        go_ref, d_ref, nm_ref, nv_ref = rest[-4:]
        gv = g_ref[...]
        mn = ADAM_B1 * m_ref[...] + (1.0 - ADAM_B1) * gv
        vn = ADAM_B2 * v_ref[...] + (1.0 - ADAM_B2) * (gv * gv)
        go_ref[...] = gv
        nm_ref[...] = mn
        nv_ref[...] = vn
        d_ref[...] = -ADAM_LR * ((mn / c1) / (jnp.sqrt(vn / c2) + ADAM_EPS) + ADAM_WD * w_ref[...])

    ls = pl.BlockSpec((None, tr, c), lambda i: (l, i, 0))
    gs = pl.BlockSpec((tr, c), lambda i: (i, 0))
    anys = [pl.BlockSpec(memory_space=pl.ANY)] * 4 if chained else []
    return _pcall(body, name=name, grid=(r // tr,), in_specs=[ls, gs, ls, ls] + anys, out_specs=[ls] * 4,
                  out_shape=[jax.ShapeDtypeStruct((L, r, c), F32)] * 4,
                  input_output_aliases={4: 0, 5: 1, 6: 2, 7: 3} if chained else {},
                  compiler_params=_params(("parallel",)))(w, g, m, v, *(prev if chained else ()))


def _adamw_layer_halves(w, g_join, g_own, m, v, l, prev, name):
    L, r, c = w.shape
    h = r // 2
    tr = _tile(h, (256, 128, 64, 32, 16, 8))
    nh = h // tr
    c1 = 1.0 - ADAM_B1 ** ADAM_STEP
    c2 = 1.0 - ADAM_B2 ** ADAM_STEP
    chained = prev is not None

    def body(c_ref, w_ref, gj_ref, go_ref, m_ref, v_ref, *rest):
        gout_ref, d_ref, nm_ref, nv_ref = rest[-4:]
        i = pl.program_id(0)
        mine = (i >= c_ref[0] * nh) & (i < (c_ref[0] + 1) * nh)
        gv = jnp.where(mine, go_ref[...], gj_ref[...])
        mn = ADAM_B1 * m_ref[...] + (1.0 - ADAM_B1) * gv
        vn = ADAM_B2 * v_ref[...] + (1.0 - ADAM_B2) * (gv * gv)
        gout_ref[...] = gv
        nm_ref[...] = mn
        nv_ref[...] = vn
        d_ref[...] = -ADAM_LR * ((mn / c1) / (jnp.sqrt(vn / c2) + ADAM_EPS) + ADAM_WD * w_ref[...])

    ls = pl.BlockSpec((None, tr, c), lambda i, cs: (l, i, 0))
    gj = pl.BlockSpec((tr, c), lambda i, cs: (i, 0))
    go = pl.BlockSpec((tr, c), lambda i, cs: (jnp.clip(i - cs[0] * nh, 0, nh - 1), 0))
    anys = [pl.BlockSpec(memory_space=pl.ANY)] * 4 if chained else []
    spec = pltpu.PrefetchScalarGridSpec(num_scalar_prefetch=1, grid=(r // tr,), in_specs=[ls, gj, go, ls, ls] + anys,
                                        out_specs=[ls] * 4)
    return _pcall(body, name=name, grid_spec=spec, out_shape=[jax.ShapeDtypeStruct((L, r, c), F32)] * 4,
                  input_output_aliases={6: 0, 7: 1, 8: 2, 9: 3} if chained else {},
                  compiler_params=_params(("arbitrary",)))(
                      lax.axis_index("c").reshape(1).astype(jnp.int32), w, g_join, g_own, m, v, *(prev if chained else ()))


def add_pair(g, got, name):
    _, r, c = g.shape
    h = r // 2

    def body(c_ref, g_ref, o_ref, out_ref):
        del c_ref
        out_ref[...] = (g_ref[...].astype(F32) + o_ref[...].astype(F32)).astype(BF16)

    spec = pltpu.PrefetchScalarGridSpec(
        num_scalar_prefetch=1, grid=(4,),
        in_specs=[pl.BlockSpec((None, h, c), lambda k, cs: (k, cs[0], 0)), pl.BlockSpec((None, h, c), lambda k, cs: (k, 0, 0))],
        out_specs=pl.BlockSpec((None, h, c), lambda k, cs: (k, 0, 0)))
    return _pcall(body, name=name, grid_spec=spec, out_shape=jax.ShapeDtypeStruct((4, h, c), BF16),
                  compiler_params=_params(("arbitrary",)))(lax.axis_index("c").reshape(1).astype(jnp.int32), g, got)


def add_chips(p, b, name):
    _, h, c = p.shape
    tr = _tile(h, (256, 128, 64, 32, 16, 8))

    def body(me_ref, p_ref, b0, b1, b2, b3, out_ref):
        me = me_ref[0]
        acc = jnp.zeros(out_ref.shape, F32)
        for k, bk in enumerate((b0, b1, b2, b3)):
            acc = acc + jnp.where(me == k, p_ref[...], bk[...]).astype(F32)
        out_ref[...] = acc

    slot = lambda k: pl.BlockSpec((None, tr, c), lambda i, ms: (jnp.where(ms[0] == k, (k + 1) % 4, k), i, 0))
    spec = pltpu.PrefetchScalarGridSpec(
        num_scalar_prefetch=1, grid=(h // tr,),
        in_specs=[pl.BlockSpec((None, tr, c), lambda i, ms: (ms[0], i, 0)), slot(0), slot(1), slot(2), slot(3)],
        out_specs=pl.BlockSpec((tr, c), lambda i, ms: (i, 0)))
    me = (2 * lax.axis_index("x") + lax.axis_index("y")).reshape(1).astype(jnp.int32)
    return _pcall(body, name=name, grid_spec=spec, out_shape=jax.ShapeDtypeStruct((h, c), F32),
                  compiler_params=_params(("arbitrary",)))(me, p, b, b, b, b)


ANY = pl.BlockSpec(memory_space=pl.ANY)


def _place():
    x, y, c = lax.axis_index("x"), lax.axis_index("y"), lax.axis_index("c")
    chips = [(1 - x, y), (x, 1 - y), (1 - x, 1 - y)]
    return x, y, c, chips


def _rcopy(src, dst, ssem, rsem, dev):
    return pltpu.make_async_remote_copy(src_ref=src, dst_ref=dst, send_sem=ssem, recv_sem=rsem, device_id=dev,
                                        device_id_type=MESH)


def _half(c, h):
    return pl.ds(pl.multiple_of(c * h, 64), h)


def _me_chip():
    return 2 * lax.axis_index("x") + lax.axis_index("y")


def all_gather_weights(shards, name):
    n = len(shards)
    L = shards[0].shape[0]
    hl = L // 2

    def body(*refs):
        srcs, outs, (ssem, rsem) = refs[:n], refs[n:2 * n], refs[2 * n:]
        x, y, c, chips = _place()
        me = 2 * x + y
        sib = (x, y, 1 - c)
        mine, other = pl.ds(c * hl, hl), pl.ds((1 - c) * hl, hl)
        sends = []
        for w in range(n):
            for j, (px, py) in enumerate(chips):
                cp = _rcopy(srcs[w].at[mine], outs[w].at[mine, me], ssem.at[w, j], rsem.at[w, j], (px, py, c))
                cp.start()
                sends.append(cp)
        for j, (px, py) in enumerate(chips):
            slot = 2 * px + py
            for w in range(n):
                _rcopy(srcs[w].at[mine], outs[w].at[mine, slot], ssem.at[w, j], rsem.at[w, j], (px, py, c)).wait_recv()
                cp = _rcopy(outs[w].at[mine, slot], outs[w].at[mine, slot], ssem.at[w, 3 + j], rsem.at[w, 3 + j], sib)
                cp.start()
                sends.append(cp)
        for j, (px, py) in enumerate(chips):
            slot = 2 * px + py
            for w in range(n):
                _rcopy(outs[w].at[other, slot], outs[w].at[other, slot], ssem.at[w, 3 + j], rsem.at[w, 3 + j], sib).wait_recv()
        for cp in sends:
            cp.wait_send()

    outs = _pcall(body, name=name, in_specs=[ANY] * n, out_specs=[ANY] * n,
                  out_shape=[jax.ShapeDtypeStruct((L, 4) + s.shape[1:], s.dtype) for s in shards],
                  scratch_shapes=[pltpu.SemaphoreType.DMA((n, 6)), pltpu.SemaphoreType.DMA((n, 6))])(*shards)
    me = _me_chip()
    return [lax.dynamic_update_slice(o, s[:, None], (0, me, 0, 0)) for o, s in zip(outs, shards)]


HBM = pl.BlockSpec(memory_space=pltpu.HBM)
SEM = pl.BlockSpec(memory_space=pltpu.SEMAPHORE)
EFFECT = pltpu.SideEffectType.DATAFLOW_SIDE_EFFECTING


def _ici_copy(kind, src, land, x, y, c, px, py, ssem, rsem, receiving):
    me, peer = 2 * x + y, 2 * px + py
    if kind == "pair":
        return _rcopy(src.at[:, _half(1 - c, land.shape[1])], land, ssem, rsem, (x, y, 1 - c))
    if kind == "gather":
        rows = _half(c, land.shape[1] // 2)
        s, d = land.at[me, rows], land.at[peer if receiving else me, rows]
    else:
        s, d = src.at[me if receiving else peer], land.at[peer if receiving else me]
    return _rcopy(s, d, ssem, rsem, (px, py, c))


def ici_start(kind, srcs, lands, name):
    n, ns = len(lands), len(srcs)
    P = 1 if kind == "pair" else 3
    m = P * n
    na = ns + n

    def body(*refs):
        src, land, token = (refs[:ns] if ns else [None] * n), refs[ns:na], refs[-1]
        ssem, rsem = refs[na:na + m], refs[na + m:na + 2 * m]
        x, y, c, chips = _place()
        for w in range(n):
            for j, (px, py) in enumerate(chips[:P]):
                _ici_copy(kind, src[w], land[w], x, y, c, px, py, ssem[P * w + j], rsem[P * w + j], False).start()
        token[...] = jnp.zeros_like(token)

    arrs = [pltpu.with_memory_space_constraint(a, pltpu.HBM) for a in list(srcs) + list(lands)]
    outs = _pcall(
        body, name=name,
        out_shape=(*[pltpu.SemaphoreType.DMA(())] * (2 * m), *[pltpu.HBM(a.shape, a.dtype) for a in arrs],
                   jax.ShapeDtypeStruct((8, LANES), F32)),
        in_specs=[HBM] * na,
        out_specs=(*([SEM] * (2 * m)), *([HBM] * na), pl.BlockSpec(memory_space=pltpu.VMEM)),
        input_output_aliases={i: 2 * m + i for i in range(na)},
        compiler_params=pltpu.CompilerParams(has_side_effects=EFFECT))(*arrs)
    return dict(kind=kind, ssem=list(outs[:m]), rsem=list(outs[m:2 * m]), srcs=list(outs[2 * m:2 * m + ns]),
                lands=list(outs[2 * m + ns:2 * m + na])), outs[-1]


def ici_wait(state, after, name):
    kind, n, ns = state["kind"], len(state["lands"]), len(state["srcs"])
    P = 1 if kind == "pair" else 3
    m = P * n
    na = ns + n

    def body(*refs):
        src, land = (refs[:ns] if ns else [None] * n), refs[ns:na]
        ssem, rsem = refs[na:na + m], refs[na + m:na + 2 * m]
        x, y, c, chips = _place()
        for w in range(n):
            for j, (px, py) in enumerate(chips[:P]):
                _ici_copy(kind, src[w], land[w], x, y, c, px, py, ssem[P * w + j], rsem[P * w + j], False).wait_send()
                _ici_copy(kind, src[w], land[w], x, y, c, px, py, ssem[P * w + j], rsem[P * w + j], True).wait_recv()

    outs = _pcall(
        body, name=name,
        out_shape=tuple(pltpu.HBM(a.shape, a.dtype) for a in state["srcs"] + state["lands"]),
        in_specs=[HBM] * na + [SEM] * (2 * m) + [ANY], out_specs=tuple([HBM] * na),
        input_output_aliases={i: i for i in range(na)},
        compiler_params=pltpu.CompilerParams(has_side_effects=EFFECT))(
            *state["srcs"], *state["lands"], *state["ssem"], *state["rsem"], after)
    return list(outs[:ns]), list(outs[ns:])


def forward_pair(lands, name):
    n = len(lands)

    def body(*refs):
        land, (ssem, rsem) = refs[:n], refs[2 * n:]
        x, y, c, chips = _place()
        sib = (x, y, 1 - c)
        cps = []
        for w in range(n):
            h = land[w].shape[1] // 2
            for j, (px, py) in enumerate(chips):
                part = land[w].at[2 * px + py, _half(c, h)]
                cp = _rcopy(part, part, ssem.at[w, j], rsem.at[w, j], sib)
                cp.start()
                cps.append(cp)
        for w in range(n):
            h = land[w].shape[1] // 2
            for j, (px, py) in enumerate(chips):
                part = land[w].at[2 * px + py, _half(1 - c, h)]
                _rcopy(part, part, ssem.at[w, j], rsem.at[w, j], sib).wait_recv()
        for cp in cps:
            cp.wait_send()

    return _pcall(body, name=name, in_specs=[ANY] * n, out_specs=[ANY] * n,
                  out_shape=[jax.ShapeDtypeStruct(z.shape, z.dtype) for z in lands],
                  input_output_aliases={i: i for i in range(n)},
                  scratch_shapes=[pltpu.SemaphoreType.DMA((n, 3)), pltpu.SemaphoreType.DMA((n, 3))])(*lands)


def exchange_pair(gs, name):
    n = len(gs)

    def body(*refs):
        srcs, outs, (ssem, rsem) = refs[:n], refs[n:2 * n], refs[2 * n:]
        x, y, c, _ = _place()
        cps = []
        for w in range(n):
            h = srcs[w].shape[1] // 2
            cp = _rcopy(srcs[w].at[:, _half(1 - c, h)], outs[w], ssem.at[w], rsem.at[w], (x, y, 1 - c))
            cp.start()
            cps.append(cp)
        for cp in cps:
            cp.wait()

    return _pcall(body, name=name, in_specs=[ANY] * n, out_specs=[ANY] * n,
                  out_shape=[jax.ShapeDtypeStruct((4, g.shape[1] // 2, g.shape[2]), g.dtype) for g in gs],
                  scratch_shapes=[pltpu.SemaphoreType.DMA((n,)), pltpu.SemaphoreType.DMA((n,))])(*gs)


def scatter_chips(ps, name):
    n = len(ps)

    def body(*refs):
        srcs, outs, (ssem, rsem) = refs[:n], refs[n:2 * n], refs[2 * n:]
        x, y, c, chips = _place()
        me = 2 * x + y
        sends = []
        for w in range(n):
            for j, (px, py) in enumerate(chips):
                cp = _rcopy(srcs[w].at[2 * px + py], outs[w].at[me], ssem.at[w, j], rsem.at[w, j], (px, py, c))
                cp.start()
                sends.append(cp)
        for w in range(n):
            for j, (px, py) in enumerate(chips):
                _rcopy(srcs[w].at[me], outs[w].at[2 * px + py], ssem.at[w, j], rsem.at[w, j], (px, py, c)).wait_recv()
        for cp in sends:
            cp.wait_send()

    return _pcall(body, name=name, in_specs=[ANY] * n, out_specs=[ANY] * n,
                  out_shape=[jax.ShapeDtypeStruct(p.shape, p.dtype) for p in ps],
                  scratch_shapes=[pltpu.SemaphoreType.DMA((n, 3)), pltpu.SemaphoreType.DMA((n, 3))])(*ps)


def join_pair(rs, name):
    n = len(rs)

    def body(*refs):
        srcs, outs, (ssem, rsem) = refs[:n], refs[n:2 * n], refs[2 * n:]
        x, y, c, _ = _place()
        cps = []
        for w in range(n):
            h = srcs[w].shape[0]
            cp = _rcopy(srcs[w], outs[w].at[_half(c, h)], ssem.at[w], rsem.at[w], (x, y, 1 - c))
            cp.start()
            cps.append(cp)
        for w in range(n):
            h = srcs[w].shape[0]
            _rcopy(srcs[w], outs[w].at[_half(1 - c, h)], ssem.at[w], rsem.at[w], (x, y, 1 - c)).wait_recv()
        for cp in cps:
            cp.wait_send()

    outs = _pcall(body, name=name, in_specs=[ANY] * n, out_specs=[ANY] * n,
                  out_shape=[jax.ShapeDtypeStruct((2 * r.shape[0], r.shape[1]), r.dtype) for r in rs],
                  scratch_shapes=[pltpu.SemaphoreType.DMA((n,)), pltpu.SemaphoreType.DMA((n,))])(*rs)
    return outs


def fill_own_half(joined, own):
    return lax.dynamic_update_slice(joined, own, (lax.axis_index("c") * own.shape[0], 0))


def gather_all_devices(v, name):
    R, C = v.shape

    def body(src, out, ssem, rsem, lsem):
        x, y, c, _ = _place()
        me = 4 * x + 2 * y + c
        local = pltpu.make_async_copy(src, out.at[me], lsem)
        local.start()
        sends = []
        for j in range(1, 8):
            dx, dy, dc = (j >> 2) & 1, (j >> 1) & 1, j & 1
            px, py, pc = (1 - x if dx else x), (1 - y if dy else y), (1 - c if dc else c)
            cp = _rcopy(src, out.at[me], ssem.at[j - 1], rsem.at[j - 1], (px, py, pc))
            cp.start()
            sends.append(cp)
        for j in range(1, 8):
            dx, dy, dc = (j >> 2) & 1, (j >> 1) & 1, j & 1
            px, py, pc = (1 - x if dx else x), (1 - y if dy else y), (1 - c if dc else c)
            _rcopy(src, out.at[4 * px + 2 * py + pc], ssem.at[j - 1], rsem.at[j - 1], (px, py, pc)).wait_recv()
        for cp in sends:
            cp.wait_send()
        local.wait()

    return _pcall(body, name=name, in_specs=[ANY], out_specs=ANY, out_shape=jax.ShapeDtypeStruct((8, R, C), v.dtype),
                  scratch_shapes=[pltpu.SemaphoreType.DMA((7,)), pltpu.SemaphoreType.DMA((7,)), pltpu.SemaphoreType.DMA(())])(v)


def reduce_scatter_pair(gw, tag):
    names = list(gw)
    gs = [gw[k] for k in names]
    lands = [lax.empty((4, g.shape[1] // 2, g.shape[2]), g.dtype) for g in gs]
    state, token = ici_start("pair", gs, lands, "rs_pair_start_" + tag)
    state["names"] = names
    return state, token


def reduce_scatter_begin(pair_state, after, tag):
    names = pair_state["names"]
    gs, got = ici_wait(pair_state, after, "rs_pair_wait_" + tag)
    ps = [add_pair(g, o, "rs_add_pair_%s_%s" % (k, tag)) for k, g, o in zip(names, gs, got)]
    state, token = ici_start("scatter", ps, [lax.empty(p.shape, p.dtype) for p in ps], "rs_chips_start_" + tag)
    state["names"] = names
    return state, token


def reduce_scatter_end(state, after, tag):
    names = state["names"]
    ps, bs = ici_wait(state, after, "rs_chips_wait_" + tag)
    rs = [add_chips(p, b, "rs_add_chips_%s_%s" % (k, tag)) for k, p, b in zip(names, ps, bs)]
    return dict(zip(names, zip(join_pair(rs, "rs_join_" + tag), rs)))


GATHERED = ("w_in", "w_q_b", "w_kv_b", "w_out", "w_xq", "w_xk", "w_xv", "w_xo", "w_gate", "w_up", "w_down")
KIND = {"w_q_b": "col", "w_kv_b": "col", "w_out": "row", "w_xq": "row", "w_xk": "row", "w_xv": "row", "w_xo": "col",
        "w_gate": "col", "w_up": "col", "w_down": "row"}


def wq_to_storage(w):
    lead = w.shape[:-1]
    w = w.reshape(lead + (2, MLA_QK))
    rope = jnp.pad(w[..., MLA_NOPE:], [(0, 0)] * (w.ndim - 1) + [(0, LANES - MLA_ROPE)])
    return jnp.concatenate([w[..., :MLA_NOPE].reshape(lead + (2 * LANES,)), rope.reshape(lead + (2 * LANES,))], axis=-1)


def wq_from_storage(g):
    lead = g.shape[:-1]
    nope = g[..., :2 * LANES].reshape(lead + (2, LANES))
    rope = g[..., 2 * LANES:].reshape(lead + (2, LANES))[..., :MLA_ROPE]
    return jnp.concatenate([nope, rope], axis=-1).reshape(lead + (2 * MLA_QK,))


def wcat_from_storage(w_in_l):
    w = w_in_l.transpose(1, 0, 2).reshape(D_MODEL, IN_COLS)
    zc = lambda n: jnp.zeros((D_MODEL, n), w.dtype)
    return jnp.concatenate([w[:, :2560], w[:, 2576:], zc(64), w[:, 2560:2576], zc(112)], axis=1)


def w_in_grad_storage(dwcat):
    g = jnp.concatenate([dwcat[:, :2560], dwcat[:, C_DT:C_DT + 16], dwcat[:, 2560:3648]], axis=1)
    return g.reshape(D_MODEL, 4, IN_COLS // 4).transpose(1, 0, 2)


def _ssd_par(sm):
    rows = jnp.stack([sm["dt_bias"], sm["a_log"], sm["d_skip"]]).reshape(3, SSD_GROUPS, SSD_HPG).transpose(1, 0, 2)
    return jnp.pad(rows, ((0, 0), (0, 5), (0, LANES - SSD_HPG)))


def _dt_pad(proj, B, S):
    dt = proj[:, C_DT:C_DT + 16].reshape(B, S, SSD_GROUPS, SSD_HPG)
    return jnp.pad(dt, ((0, 0), (0, 0), (0, 0), (0, LANES - SSD_HPG))).reshape(B, S, SSD_GROUPS * LANES)


def _mla_gains(g):
    return g[:MLA_NOPE].reshape(1, LANES), jnp.pad(g[MLA_NOPE:], (0, LANES - MLA_ROPE)).reshape(1, LANES)


_Q_NOPE = lambda h: 4 * (h // 2) + h % 2
_Q_ROPE = lambda h: 4 * (h // 2) + 2 + h % 2
_K_NOPE = lambda h: 2 * h
_V_BLK = lambda h: 2 * h + 1


def layer_fwd(x, memf, cos, sin, G, kw, sm, B, S, l, dep=None):
    T = x.shape[0]
    n = lambda s: "%s_l%d" % (s, l)
    wv = lambda k: (KIND[k], 0)
    sv = dict(x0=x)
    h1 = norm_fwd(x, sm["attn_norm_g"], D_MODEL, False, n("attn_norm"), dep=dep)
    proj = mm(h1, kw["wcat"], "nn", n("in_proj"))
    mark = getattr(G, "mark", lambda v: None)
    mark(proj)
    sv.update(h1=h1, proj=proj)
    proj3 = proj.reshape(B, S, CAT_COLS)
    xbc = conv_fwd(proj3, C_XBC // 256, kw["conv_w"], sm["conv_b"].reshape(1, -1), n("conv"))
    dtp = _dt_pad(proj, B, S)
    par = _ssd_par(sm)
    y_raw, states = ssd_fwd(xbc, dtp, par, n("ssd"))
    z = proj[:, C_Z:C_Z + SSD_INNER]
    y_ssd = norm_fwd(y_raw.reshape(T, SSD_INNER), sm["ssd_norm_g"], SSD_INNER // SSD_GROUPS, False, n("ssd_norm"), gate=z)
    sv.update(xbc=xbc, states=states, y_raw=y_raw)
    qn = norm_fwd(proj[:, C_QA:C_QA + Q_LORA], sm["q_a_norm_g"], Q_LORA, False, n("q_a_norm"))
    kvn = norm_fwd(proj[:, C_KVA:C_KVA + KV_LORA], sm["kv_a_norm_g"], KV_LORA, False, n("kv_a_norm"))
    q_raw = mm(qn, G["w_q_b"], "nn", n("q_b"), wv=wv("w_q_b"))
    kv_raw = mm(kvn, G["w_kv_b"], "nn", n("kv_b"), wv=wv("w_kv_b"))
    gqn, gqr = _mla_gains(sm["mla_q_norm_g"])
    gkn, gkr = _mla_gains(sm["mla_k_norm_g"])
    qh = qk_fwd(q_raw, _Q_NOPE, q_raw, _Q_ROPE, False, gqn, gqr, cos, sin, n("q_norm_rope"))
    kh = qk_fwd(kv_raw, _K_NOPE, proj, C_KR // LANES, True, gkn, gkr, cos, sin, n("k_norm_rope"))
    vv = kv_raw.astype(BF16)
    y_mla = attn_fwd(qh.reshape(B, S, -1), kh.reshape(B, S, -1), vv.reshape(B, S, -1), MLA_HEADS, 2 * LANES, MLA_V,
                     MLA_QK ** -0.5, True, n("mla_attn"), v_idx=_V_BLK)
    ycat = jnp.concatenate([y_ssd, y_mla.reshape(T, -1)], axis=1)
    x1 = mm(ycat, G["w_out"], "nn", n("out_proj"), add=x, wv=wv("w_out"))
    sv.update(qn=qn, kvn=kvn, q_raw=q_raw, kv_raw=kv_raw, qh=qh, kh=kh, vv=vv, ycat=ycat, x1=x1)
    h2 = norm_fwd(x1, sm["xattn_norm_g"], D_MODEL, False, n("xattn_norm"))
    mn = norm_fwd(memf, sm["mem_norm_g"], D_MODEL, False, n("mem_norm"))
    xq = mm(h2, G["w_xq"], "nn", n("xq"), wv=wv("w_xq"))
    xk = mm(mn, G["w_xk"], "nn", n("xk"), wv=wv("w_xk"))
    xv = mm(mn, G["w_xv"], "nn", n("xv"), wv=wv("w_xv"), out_dtype=BF16)
    xqn = norm_fwd(xq, sm["xq_norm_g"], X_HEAD_DIM, True, n("xq_norm"))
    xkn = norm_fwd(xk, sm["xk_norm_g"], X_HEAD_DIM, True, n("xk_norm"))
    ML = memf.shape[0] // B
    xo = attn_fwd(xqn.reshape(B, S, -1), xkn.reshape(B, ML, -1), xv.reshape(B, ML, -1), X_HEADS, X_HEAD_DIM, X_HEAD_DIM,
                  X_HEAD_DIM ** -0.5, False, n("x_attn"))
    x2 = mm(xo.reshape(T, -1), G["w_xo"], "nn", n("xo_proj"), add=x1, wv=wv("w_xo"))
    mark(x2)
    sv.update(h2=h2, mn=mn, xq=xq, xk=xk, xqn=xqn, xkn=xkn, xv=xv, xo=xo, x2=x2)
    h3 = norm_fwd(x2, sm["ffn_norm_g"], D_MODEL, False, n("ffn_norm"))
    gate, up, act = gate_up_act(h3, G["w_gate"], G["w_up"], n("gate_up"))
    x3 = mm(act, G["w_down"], "nn", n("down"), add=x2, wv=wv("w_down"))
    sv.update(h3=h3, gate=gate, up=up, act=act)
    return x3, sv


def layer_bwd(dx3, dx3b, sv, memf, cos, sin, G, kw, sm, B, S, l, dep=None, rs_hook=None):
    T = dx3.shape[0]
    n = lambda s: "%s_bwd_l%d" % (s, l)
    wv = lambda k: (KIND[k], 0)
    dw = lambda k, a, b: mm(a, b, "tn", n(k + "_dw"), out_dtype=BF16, out_wv=KIND[k])
    gw, gs = {}, {}
    dgate, dup = down_dx_swiglu(dx3b, G["w_down"][0].reshape(FFN, D_MODEL), sv["gate"], sv["up"], n("down_dx"), dep=dep)
    gw["w_down"] = dw("w_down", sv["act"], dx3b)
    dh3 = gate_up_dx(dgate, dup, G["w_gate"], G["w_up"], n("gate_up_dx"))
    gw["w_gate"] = dw("w_gate", sv["h3"], dgate)
    gw["w_up"] = dw("w_up", sv["h3"], dup)
    dx2, gs["ffn_norm_g"], dx2b = norm_bwd(sv["x2"], sm["ffn_norm_g"], dh3, D_MODEL, False, n("ffn_norm"), add=dx3,
                                           bf16_copy=True)
    ffn_token = rs_hook({k: gw.pop(k) for k in ("w_down", "w_gate", "w_up")}) if rs_hook is not None else None
    dxo = mm(dx2b, G["w_xo"], "nt", n("xo_dx"), wv=wv("w_xo"), dep=ffn_token)
    gw["w_xo"] = dw("w_xo", sv["xo"].reshape(T, -1), dx2b)
    ML = memf.shape[0] // B
    dxqn, dxkn, dxv = attn_bwd(sv["xqn"].reshape(B, S, -1), sv["xkn"].reshape(B, ML, -1), sv["xv"].reshape(B, ML, -1),
                               dxo.reshape(B, S, -1), X_HEADS, X_HEAD_DIM, X_HEAD_DIM, X_HEAD_DIM ** -0.5, False, n("x_attn"))
    dxq, gs["xq_norm_g"] = norm_bwd(sv["xq"], sm["xq_norm_g"], dxqn.reshape(T, -1), X_HEAD_DIM, True, n("xq_norm"))
    dxk, gs["xk_norm_g"] = norm_bwd(sv["xk"], sm["xk_norm_g"], dxkn.reshape(B * ML, -1), X_HEAD_DIM, True, n("xk_norm"))
    dxv = dxv.reshape(B * ML, -1)
    dmn = mm(dxk, G["w_xk"], "nt", n("xk_dx"), wv=wv("w_xk"))
    dmn = mm(dxv, G["w_xv"], "nt", n("xv_dx"), wv=wv("w_xv"), add=dmn)
    gw["w_xk"] = dw("w_xk", sv["mn"], dxk)
    gw["w_xv"] = dw("w_xv", sv["mn"], dxv)
    _, gs["mem_norm_g"] = norm_bwd(memf, sm["mem_norm_g"], dmn, D_MODEL, False, n("mem_norm"))
    dh2 = mm(dxq, G["w_xq"], "nt", n("xq_dx"), wv=wv("w_xq"))
    gw["w_xq"] = dw("w_xq", sv["h2"], dxq)
    dx1, gs["xattn_norm_g"], dx1b = norm_bwd(sv["x1"], sm["xattn_norm_g"], dh2, D_MODEL, False, n("xattn_norm"), add=dx2,
                                             bf16_copy=True)
    dycat = mm(dx1b, G["w_out"], "nt", n("out_dx"), wv=wv("w_out"))
    gw["w_out"] = dw("w_out", sv["ycat"], dx1b)
    mid_token = rs_hook(None, after=dycat) if rs_hook is not None else None
    dqh, dkh, dvv = attn_bwd(sv["qh"].reshape(B, S, -1), sv["kh"].reshape(B, S, -1), sv["vv"].reshape(B, S, -1),
                             dycat[:, SSD_INNER:].reshape(B, S, -1), MLA_HEADS, 2 * LANES, MLA_V, MLA_QK ** -0.5, True,
                             n("mla_attn"), v_idx=_V_BLK)
    gqn, gqr = _mla_gains(sm["mla_q_norm_g"])
    gkn, gkr = _mla_gains(sm["mla_k_norm_g"])
    dq_raw, dgqn, dgqr = qk_bwd(sv["q_raw"], _Q_NOPE, sv["q_raw"], _Q_ROPE, False, gqn, gqr, cos, sin,
                                dqh.reshape(T, -1), n("q_norm_rope"))
    dkv_raw, dk_rope, dgkn, dgkr = qk_bwd(sv["kv_raw"], _K_NOPE, sv["proj"], C_KR // LANES, True, gkn, gkr, cos, sin,
                                          dkh.reshape(T, -1), n("k_norm_rope"), fill=dvv.reshape(T, -1), fill_blk=_V_BLK)
    gs["mla_q_norm_g"] = jnp.concatenate([dgqn[0], dgqr[0, :MLA_ROPE]])
    gs["mla_k_norm_g"] = jnp.concatenate([dgkn[0], dgkr[0, :MLA_ROPE]])
    dqn = mm(dq_raw, G["w_q_b"], "nt", n("q_b_dx"), wv=wv("w_q_b"), dep=mid_token)
    gw["w_q_b"] = dw("w_q_b", sv["qn"], dq_raw)
    dkvn = mm(dkv_raw, G["w_kv_b"], "nt", n("kv_b_dx"), wv=wv("w_kv_b"))
    gw["w_kv_b"] = dw("w_kv_b", sv["kvn"], dkv_raw)
    proj = sv["proj"]
    dq_a, gs["q_a_norm_g"] = norm_bwd(proj[:, C_QA:C_QA + Q_LORA], sm["q_a_norm_g"], dqn, Q_LORA, False, n("q_a_norm"))
    dkv_a, gs["kv_a_norm_g"] = norm_bwd(proj[:, C_KVA:C_KVA + KV_LORA], sm["kv_a_norm_g"], dkvn, KV_LORA, False,
                                        n("kv_a_norm"))
    z = proj[:, C_Z:C_Z + SSD_INNER]
    dy_raw, dz, gs["ssd_norm_g"] = norm_bwd(sv["y_raw"].reshape(T, -1), sm["ssd_norm_g"], dycat[:, :SSD_INNER],
                                            SSD_INNER // SSD_GROUPS, False, n("ssd_norm"), gate=z)
    dtp = _dt_pad(proj, B, S)
    par = _ssd_par(sm)
    dxs, dBm, dCm, ddt, dpar = ssd_bwd(sv["xbc"], dtp, par, sv["states"], dy_raw.reshape(B, S, -1), n("ssd"))
    gs["dt_bias"] = dpar[:, 0, :SSD_HPG].reshape(-1)
    gs["a_log"] = dpar[:, 1, :SSD_HPG].reshape(-1)
    gs["d_skip"] = dpar[:, 2, :SSD_HPG].reshape(-1)
    dxbc_act = jnp.concatenate([dxs, dBm, dCm], axis=2)
    dxbc, dconv_w, dconv_b = conv_bwd(proj.reshape(B, S, -1), C_XBC // 256, kw["conv_w"], sm["conv_b"].reshape(1, -1), dxbc_act,
                                      n("conv"))
    gs["conv_w"] = dconv_w[:SSD_CONV]
    gs["conv_b"] = dconv_b[0]
    ddt16 = ddt.reshape(T, SSD_GROUPS, LANES)[:, :, :SSD_HPG].reshape(T, 16)
    dproj = jnp.concatenate([dz, dxbc.reshape(T, -1), dq_a, dkv_a, dk_rope, ddt16, jnp.zeros((T, 112), F32)],
                            axis=1).astype(BF16)
    dh1 = mm(dproj, kw["wcat"], "nt", n("in_dx"))
    gw["w_in"] = w_in_grad_storage(mm(sv["h1"], dproj, "tn", n("in_dw"), out_dtype=BF16))
    dx0, gs["attn_norm_g"], dx0b = norm_bwd(sv["x0"], sm["attn_norm_g"], dh1, D_MODEL, False, n("attn_norm"), add=dx1,
                                            bf16_copy=True)
    gs = {k: v.reshape(-1) for k, v in gs.items()}
    return dx0, dx0b, gw, gs


def _small_pack(vals):
    flat = jnp.concatenate([vals[l][k].reshape(-1) for l in range(DEPTH) for k in SMALL_RS])
    n = flat.shape[0]
    rows = -(-n // LANES)
    rows = -(-rows // 8) * 8
    return jnp.pad(flat, (0, rows * LANES - n)).reshape(rows, LANES), n


def _small_unpack(flat, shapes):
    v = flat.reshape(-1)
    out, off = [dict() for _ in range(DEPTH)], 0
    for l in range(DEPTH):
        for k in SMALL_RS:
            sz = shapes[k]
            out[l][k] = v[off:off + sz]
            off += sz
    return out


def _adam_2d(a):
    if a.ndim == 2:
        return a
    return a.reshape(a.shape[0] * a.shape[1], a.shape[2])


def kernel(x, mem, positions, attn_norm_g, w_in, conv_w, conv_b, dt_bias, a_log, d_skip, ssd_norm_g, q_a_norm_g, w_q_b, kv_a_norm_g, w_kv_b, mla_q_norm_g, mla_k_norm_g, w_out, xattn_norm_g, mem_norm_g, w_xq, w_xk, w_xv, xq_norm_g, xk_norm_g, w_xo, ffn_norm_g, w_gate, w_up, w_down, loss_target, m_attn_norm_g, m_w_in, m_conv_w, m_conv_b, m_dt_bias, m_a_log, m_d_skip, m_ssd_norm_g, m_q_a_norm_g, m_w_q_b, m_kv_a_norm_g, m_w_kv_b, m_mla_q_norm_g, m_mla_k_norm_g, m_w_out, m_xattn_norm_g, m_mem_norm_g, m_w_xq, m_w_xk, m_w_xv, m_xq_norm_g, m_xk_norm_g, m_w_xo, m_ffn_norm_g, m_w_gate, m_w_up, m_w_down, v_attn_norm_g, v_w_in, v_conv_w, v_conv_b, v_dt_bias, v_a_log, v_d_skip, v_ssd_norm_g, v_q_a_norm_g, v_w_q_b, v_kv_a_norm_g, v_w_kv_b, v_mla_q_norm_g, v_mla_k_norm_g, v_w_out, v_xattn_norm_g, v_mem_norm_g, v_w_xq, v_w_xk, v_w_xv, v_xq_norm_g, v_xk_norm_g, v_w_xo, v_ffn_norm_g, v_w_gate, v_w_up, v_w_down):
    loc = locals()
    W = {k: loc[k] for k in WEIGHTS}
    Mo = {k: loc["m_" + k] for k in WEIGHTS}
    Vo = {k: loc["v_" + k] for k in WEIGHTS}
    B, S, D = x.shape
    T = B * S
    xf = x.reshape(T, D)
    memf = mem.reshape(-1, D)
    tf = loss_target.reshape(T, D)
    cos, sin = rope_tables(positions.reshape(T, 1).astype(F32))

    cw = gather_all_devices(conv_w.reshape(DEPTH * SSD_CONV, -1), "gather_conv_w")[0::2]
    cw = cw.transpose(1, 0, 2).reshape(DEPTH, SSD_CONV, SSD_CONV_DIM)
    sms = [{k: W[k][l] for k in SMALL} for l in range(DEPTH)]
    me = _me_chip()

    cw, w_in_first = lax.optimization_barrier((cw, W["w_in"][0]))

    def gather_start(l):
        shards = [(wq_to_storage(W[k][l]) if k == "w_q_b" else W[k][l]).astype(BF16) for k in GATHERED]
        if l == 0:
            shards[0] = w_in_first.astype(BF16)
        lands = [lax.dynamic_update_slice(lax.empty((4,) + s.shape, BF16), s[None], (me, 0, 0)) for s in shards]
        return ici_start("gather", [], lands, "gather_start_l%d" % l)

    def gather_end(state, names, after, tag):
        idx = [GATHERED.index(k) for k in names]
        pick3 = lambda sems: [sems[3 * w + j] for w in idx for j in range(3)]
        sub = dict(kind=state["kind"], srcs=[], lands=[state["lands"][w] for w in idx],
                   ssem=pick3(state["ssem"]), rsem=pick3(state["rsem"]))
        _, lands = ici_wait(sub, after, "gather_wait_" + tag)
        return {k: z[None] for k, z in zip(names, forward_pair(lands, "gather_forward_" + tag))}

    class Weights(dict):
        def __init__(self, state, groups, after, tag):
            super().__init__(gather_end(state, groups[0], after, tag + "a"))
            self.state, self.later, self.after, self.tag = state, list(groups[1:]), after, tag

        def mark(self, value):
            self.after = value

        def __missing__(self, k):
            names = self.later.pop(0)
            self.update(gather_end(self.state, names, self.after, self.tag + "bcdef"[len(self.later)]))
            return self[k]

    FIRST_GROUPS = (("w_in",), ("w_q_b", "w_kv_b", "w_out", "w_xq", "w_xk", "w_xv", "w_xo"), ("w_gate", "w_up", "w_down"))

    saved, Gs, kws = [], [], []
    h = xf
    state, token = gather_start(0)
    for l in range(DEPTH):
        Gl = Weights(state, FIRST_GROUPS if l == 0 else (GATHERED,), h, "l%d" % l)
        kwl = dict(wcat=wcat_from_storage(Gl["w_in"][0]), conv_w=jnp.pad(cw[l], ((0, 8 - SSD_CONV), (0, 0))))
        Gs.append(Gl)
        kws.append(kwl)
        if l + 1 < DEPTH:
            state, token = gather_start(l + 1)
        h, sv = layer_fwd(h, memf, cos, sin, Gl, kwl, sms[l], B, S, l, dep=token if l + 1 < DEPTH else None)
        saved.append(sv)
    dh, lpart, dhb = loss_head(h, tf)
    loss = lax.psum(lpart[0, 0], ("x", "y", "c"))

    grads, delta, new_m, new_v = {}, {}, {}, {}
    chain = {k: None for k in GATHERED}
    gsmall = [None] * DEPTH
    pending = [[] for _ in range(DEPTH)]

    in_pair = []

    def flush(after):
        if not in_pair:
            return None
        pair_state, l, tag = in_pair.pop()
        rs_state, tok = reduce_scatter_begin(pair_state, after, tag)
        pending[l].append((rs_state, tag))
        return tok

    def begin(gw, l, tag, after=None):
        if gw is None:
            return flush(after)
        tok_chips = flush(next(iter(gw.values())))
        pair_state, tok = reduce_scatter_pair(gw, "l%d%s" % (l, tag))
        in_pair.append((pair_state, l, "l%d%s" % (l, tag)))
        return tok if tok_chips is None else tok + tok_chips

    def finish(l, after):
        for rs_state, tag in pending[l]:
            red = reduce_scatter_end(rs_state, after, tag)
            for k, (joined, own) in red.items():
                if k == "w_q_b":
                    joined, own = wq_from_storage(fill_own_half(joined, own)), None
                chain[k] = adamw_layer(W[k], joined, Mo[k], Vo[k], l, chain[k], "adamw_%s_l%d" % (k, l), g_own=own)

    token = None
    for l in reversed(range(DEPTH)):
        dh, dhb, gw, gs = layer_bwd(dh, dhb, saved[l], memf, cos, sin, Gs[l], kws[l], sms[l], B, S, l, dep=token,
                                    rs_hook=functools.partial(begin, l=l, tag="_ffn"))
        saved[l] = None
        gsmall[l] = gs
        token = begin(gw, l, "_mix")
        if l == 0:
            token = token + flush(dh)
        if l + 1 < DEPTH:
            finish(l + 1, token)
    grad_x = dh.reshape(B, S, D)

    spack, _ = _small_pack(gsmall)
    allv = gather_all_devices(spack, "gather_small_grads")
    ssum = sum_arrays([allv[i] for i in range(8)], F32, "sum_small_grads")
    sizes = {k: int(np.prod(W[k].shape[1:])) for k in SMALL}
    sizes["conv_w"] = SSD_CONV * SSD_CONV_DIM
    gsm = _small_unpack(ssum, sizes)
    for k in SMALL_RS:
        g = jnp.stack([gsm[l][k] for l in range(DEPTH)])
        if k == "conv_w":
            cs = SSD_CONV_DIM // 4
            g = lax.dynamic_slice_in_dim(g.reshape(DEPTH, SSD_CONV, SSD_CONV_DIM), _me_chip() * cs, cs, axis=2)
        shp = W[k].shape
        grads[k] = g.reshape(shp)
        d, nm, nv = adamw(_adam_2d(W[k]), _adam_2d(grads[k]), _adam_2d(Mo[k]), _adam_2d(Vo[k]), "adamw_" + k)
        delta[k], new_m[k], new_v[k] = d.reshape(shp), nm.reshape(shp), nv.reshape(shp)

    finish(0, lax.optimization_barrier((delta["ffn_norm_g"], chain["w_down"][1], chain["w_gate"][1], chain["w_up"][1]))[0])
    for k in GATHERED:
        grads[k], delta[k], new_m[k], new_v[k] = chain[k]

    return (loss, grad_x, *[grads[k] for k in WEIGHTS], *[delta[k] for k in WEIGHTS], *[new_m[k] for k in WEIGHTS],
            *[new_v[k] for k in WEIGHTS])
```

```python
import functools

import numpy as np
import jax
import jax.numpy as jnp
from jax import lax
from jax.experimental import pallas as pl
from jax.experimental.pallas import tpu as pltpu

F32 = jnp.float32
BF16 = jnp.bfloat16
MESH = pl.DeviceIdType.MESH

D_MODEL = 2048
DEPTH = 4
SSD_INNER = 1024
SSD_HEAD_DIM = 64
SSD_GROUPS = 2
SSD_HPG = 8
SSD_STATE = 128
SSD_CONV = 4
SSD_CHUNK = 128
SSD_CONV_DIM = 1536
SSD_GW = SSD_HPG * SSD_HEAD_DIM
MLA_HEADS = 8
MLA_NOPE = 128
MLA_ROPE = 64
MLA_QK = 192
MLA_V = 128
Q_LORA = 512
KV_LORA = 512
ROPE_THETA = 10000.0
X_HEADS = 4
X_HEAD_DIM = 128
X_INNER = 512
FFN = 5632
IN_COLS = 3664
EPS = 1e-6
ADAM_LR, ADAM_B1, ADAM_B2, ADAM_EPS, ADAM_WD, ADAM_STEP = 0.001, 0.9, 0.999, 1e-08, 0.01, 10

C_Z, C_XBC, C_QA, C_KVA, C_KR, C_DT, CAT_COLS = 0, 1024, 2560, 3072, 3584, 3712, 3840

LANES = 128
VMEM_LIMIT = 56 * 1024 * 1024
MM_VMEM_BUDGET = 40 * 1024 * 1024
MM_STEP_BYTES = 1024 * 1024
NORM_BLOCK_ELEMS = 512 * 1024
ATTN_SCORE_ELEMS = 1024 * 1024

SMALL = ("attn_norm_g", "conv_b", "dt_bias", "a_log", "d_skip", "ssd_norm_g", "q_a_norm_g", "kv_a_norm_g",
         "mla_q_norm_g", "mla_k_norm_g", "xattn_norm_g", "mem_norm_g", "xq_norm_g", "xk_norm_g", "ffn_norm_g")
SMALL_RS = SMALL + ("conv_w",)
WEIGHTS = ("attn_norm_g", "w_in", "conv_w", "conv_b", "dt_bias", "a_log", "d_skip", "ssd_norm_g", "q_a_norm_g", "w_q_b",
           "kv_a_norm_g", "w_kv_b", "mla_q_norm_g", "mla_k_norm_g", "w_out", "xattn_norm_g", "mem_norm_g", "w_xq", "w_xk",
           "w_xv", "xq_norm_g", "xk_norm_g", "w_xo", "ffn_norm_g", "w_gate", "w_up", "w_down")


def _pcall(body, **kw):
    return pl.pallas_call(body, **kw)


def _params(sem):
    return pltpu.CompilerParams(dimension_semantics=sem, vmem_limit_bytes=VMEM_LIMIT)


def _tile(n, cands):
    for t in cands:
        if n % t == 0:
            return t
    return n


def _sigmoid(z):
    return 1.0 / (1.0 + jnp.exp(-z))


def _divisor_tiles(n, cap):
    if n % LANES:
        return [n]
    q = n // LANES
    return [LANES * d for d in range(1, q + 1) if q % d == 0 and LANES * d <= cap] or [n]


def _mm_tiles(M, N, K, m_unit, n_unit, k_unit, a_item, b_item, o_item, has_add):
    best = None
    for tm in _divisor_tiles(m_unit, 2048):
        for tn in _divisor_tiles(n_unit, 2048):
            for tk in _divisor_tiles(k_unit, 2048):
                nk = K // tk
                vmem = 2 * (tm * tk * a_item + tk * tn * b_item + tm * tn * (o_item + (4 if has_add else 0)))
                vmem += tm * tn * 4 * (2 if nk > 1 else 1) + (tm * tk * 2 if a_item == 4 else 0) + (tk * tn * 2 if b_item == 4 else 0)
                if vmem > MM_VMEM_BUDGET:
                    continue
                steps = (M // tm) * (N // tn) * nk
                traffic = (M * K * a_item * (N // tn if nk > 1 else 1) + K * N * b_item * (M // tm)
                           + M * N * (o_item + (4 if has_add else 0)) + steps * MM_STEP_BYTES)
                if nk > 1:
                    traffic += steps * tm * tn * 4
                if best is None or traffic < best[0]:
                    best = (traffic, tm, tn, tk)
    assert best is not None, (M, N, K)
    return best[1:]


def mm(a, b, mode, name, add=None, out_dtype=F32, wv=None, out_wv=None, dep=None):
    kind, l = wv if wv is not None else (None, None)
    if kind == "row":
        b, kind = b[l].reshape(-1, b.shape[-1]), None
    if out_wv == "row":
        res = mm(a, b, mode, name, add=add, out_dtype=out_dtype, wv=wv if kind else None, dep=dep)
        return res.reshape(4, res.shape[0] // 4, res.shape[1])
    if kind == "col":
        _, _, r, c = b.shape
        bshape = (r, 4 * c)
    else:
        bshape = b.shape
    if mode == "nn":
        (M, K), (K2, N) = a.shape, bshape
    elif mode == "nt":
        (M, K), (N, K2) = a.shape, bshape
    else:
        (K, M), (K2, N) = a.shape, bshape
    assert K == K2, (a.shape, bshape, mode)
    m_unit = M
    n_unit = N // 4 if (out_wv == "col" or (kind == "col" and mode == "nn")) else N
    k_unit = K // 4 if (kind == "col" and mode == "nt") else K
    tm, tn, tk = _mm_tiles(M, N, K, m_unit, n_unit, k_unit, a.dtype.itemsize, b.dtype.itemsize,
                           jnp.dtype(out_dtype).itemsize, add is not None)
    nk = K // tk
    qm, qn, qk = m_unit // tm, n_unit // tn, k_unit // tk
    dn = {"nn": (((1,), (0,)), ((), ())), "nt": (((1,), (1,)), ((), ())), "tn": (((0,), (0,)), ((), ()))}[mode]
    has_add = add is not None

    def body(a_ref, b_ref, *rest):
        rest = list(rest)
        c_ref = rest.pop(0) if has_add else None
        if dep is not None:
            rest.pop(0)
        o_ref = rest.pop(0)
        d = lax.dot_general(a_ref[...].astype(BF16), b_ref[...].astype(BF16), dn, preferred_element_type=F32)

        def finish(res):
            if has_add:
                res = res + c_ref[...].astype(F32)
            o_ref[...] = res.astype(out_dtype)

        if nk == 1:
            finish(d)
            return
        acc = rest.pop(0)
        k = pl.program_id(2)

        @pl.when(k == 0)
        def _():
            acc[...] = d

        if nk > 2:
            @pl.when((k > 0) & (k < nk - 1))
            def _():
                acc[...] += d

        @pl.when(k == nk - 1)
        def _():
            finish(acc[...] + d)

    if mode == "tn":
        a_spec = pl.BlockSpec((tk, tm), lambda i, j, k: (k, i))
    else:
        a_spec = pl.BlockSpec((tm, tk), lambda i, j, k: (i, k))
    if kind is None:
        b_spec = (pl.BlockSpec((tn, tk), lambda i, j, k: (j, k)) if mode == "nt"
                  else pl.BlockSpec((tk, tn), lambda i, j, k: (k, j)))
    elif mode == "nn":
        b_spec = pl.BlockSpec((None, None, tk, tn), lambda i, j, k: (l, j // qn, k, j % qn))
    elif mode == "nt":
        b_spec = pl.BlockSpec((None, None, tn, tk), lambda i, j, k: (l, k // qk, j, k % qk))
    else:
        raise ValueError((mode, kind))
    if out_wv == "col":
        o_spec = pl.BlockSpec((None, tm, tn), lambda i, j, k: (j // qn, i, j % qn))
        out_shape = jax.ShapeDtypeStruct((4, M, N // 4), out_dtype)
    else:
        o_spec = pl.BlockSpec((tm, tn), lambda i, j, k: (i, j))
        out_shape = jax.ShapeDtypeStruct((M, N), out_dtype)
    in_specs = [a_spec, b_spec] + ([o_spec] if has_add else []) + ([pl.BlockSpec(memory_space=pl.ANY)] if dep is not None else [])
    args = (a, b) + ((add,) if has_add else ()) + ((dep,) if dep is not None else ())
    return _pcall(
        body, name=name, grid=(M // tm, N // tn, nk), in_specs=in_specs, out_specs=o_spec,
        out_shape=out_shape, scratch_shapes=[pltpu.VMEM((tm, tn), F32)] if nk > 1 else [],
        compiler_params=_params(("parallel", "parallel", "arbitrary")))(*args)


def norm_fwd(x, gain, W, shared, name, gate=None, out_dtype=BF16, dep=None):
    T, C = x.shape
    nb = C // W
    tq = _tile(T, tuple(t for t in (4096, 2048, 1024, 512, 256, 128) if t * W <= NORM_BLOCK_ELEMS))
    g2 = gain.reshape(1, -1).astype(F32)
    gated = gate is not None

    def body(*refs):
        o_ref = refs[-1]
        if gated:
            x_ref, z_ref, g_ref = refs[:3]
        else:
            x_ref, g_ref = refs[:2]
        u = x_ref[...].astype(F32)
        if gated:
            z = z_ref[...]
            u = u * (z * _sigmoid(z))
        ms = jnp.mean(u * u, axis=-1, keepdims=True)
        o_ref[...] = (u * lax.rsqrt(ms + EPS) * g_ref[...]).astype(out_dtype)

    xs = pl.BlockSpec((tq, W), lambda j, i: (i, j))
    gs = pl.BlockSpec((1, W), (lambda j, i: (0, 0)) if shared else (lambda j, i: (0, j)))
    in_specs = [xs] + ([xs] if gated else []) + [gs] + ([pl.BlockSpec(memory_space=pl.ANY)] if dep is not None else [])
    args = (x,) + ((gate,) if gated else ()) + (g2,) + ((dep,) if dep is not None else ())
    return _pcall(body, name=name, grid=(nb, T // tq), in_specs=in_specs, out_specs=xs,
                  out_shape=jax.ShapeDtypeStruct((T, C), out_dtype),
                  compiler_params=_params(("parallel", "parallel")))(*args)


def norm_bwd(x, gain, dy, W, shared, name, gate=None, add=None, bf16_copy=False):
    T, C = x.shape
    nb = C // W
    tq = _tile(T, tuple(t for t in (4096, 2048, 1024, 512, 256, 128) if t * W <= NORM_BLOCK_ELEMS))
    g2 = gain.reshape(1, -1).astype(F32)
    gated = gate is not None
    has_add = add is not None
    GC = W if shared else C

    def body(*refs):
        refs = list(refs)
        x_ref = refs.pop(0)
        z_ref = refs.pop(0) if gated else None
        g_ref = refs.pop(0)
        dy_ref = refs.pop(0)
        a_ref = refs.pop(0) if has_add else None
        dx_ref = refs.pop(0)
        dz_ref = refs.pop(0) if gated else None
        dg_ref = refs.pop(0)
        j, i = pl.program_id(0), pl.program_id(1)
        first = (i == 0) & (j == 0) if shared else (i == 0)

        @pl.when(first)
        def _():
            dg_ref[...] = jnp.zeros_like(dg_ref)

        xv = x_ref[...].astype(F32)
        if gated:
            z = z_ref[...]
            s = _sigmoid(z)
            u = xv * (z * s)
        else:
            u = xv
        rstd = lax.rsqrt(jnp.mean(u * u, axis=-1, keepdims=True) + EPS)
        uh = u * rstd
        dyv = dy_ref[...].astype(F32)
        dg_ref[...] += jnp.sum(dyv * uh, axis=0, keepdims=True)
        dyg = dyv * g_ref[...]
        du = rstd * (dyg - uh * jnp.mean(dyg * uh, axis=-1, keepdims=True))
        if gated:
            dx = du * (z * s)
            dz_ref[...] = du * xv * (s * (1.0 + z * (1.0 - s)))
        else:
            dx = du
        if has_add:
            dx = dx + a_ref[...]
        dx_ref[...] = dx
        if bf16_copy:
            refs.pop(0)[...] = dx.astype(BF16)

    xs = pl.BlockSpec((tq, W), lambda j, i: (i, j))
    gs = pl.BlockSpec((1, W), (lambda j, i: (0, 0)) if shared else (lambda j, i: (0, j)))
    in_specs = [xs] + ([xs] if gated else []) + [gs, xs] + ([xs] if has_add else [])
    args = (x,) + ((gate,) if gated else ()) + (g2, dy) + ((add,) if has_add else ())
    out_specs = [xs] + ([xs] if gated else []) + [gs] + ([xs] if bf16_copy else [])
    out_shape = [jax.ShapeDtypeStruct((T, C), F32)] + ([jax.ShapeDtypeStruct((T, C), F32)] if gated else []) \
        + [jax.ShapeDtypeStruct((1, GC), F32)] + ([jax.ShapeDtypeStruct((T, C), BF16)] if bf16_copy else [])
    return _pcall(body, name=name, grid=(nb, T // tq), in_specs=in_specs, out_specs=out_specs, out_shape=out_shape,
                  compiler_params=_params(("arbitrary", "arbitrary")))(*args)


def _inv_freq_row():
    inv = 1.0 / (ROPE_THETA ** (np.arange(0, MLA_ROPE, 2, dtype=np.float32) / MLA_ROPE))
    row = np.zeros((1, LANES), np.float32)
    row[0, :32] = inv
    row[0, 32:64] = inv
    return jnp.asarray(row)


def rope_tables(pos):
    T = pos.shape[0]
    tq = _tile(T, (512, 256, 128))

    def body(p_ref, f_ref, c_ref, s_ref):
        ang = p_ref[...] * f_ref[...]
        lane = lax.broadcasted_iota(jnp.int32, ang.shape, 1)
        c_ref[...] = jnp.where(lane < 64, jnp.cos(ang), 0.0)
        sn = jnp.sin(ang)
        s_ref[...] = jnp.where(lane < 32, -sn, jnp.where(lane < 64, sn, 0.0))

    ts = pl.BlockSpec((tq, LANES), lambda i: (i, 0))
    return _pcall(body, name="rope_tables", grid=(T // tq,),
                  in_specs=[pl.BlockSpec((tq, 1), lambda i: (i, 0)), pl.BlockSpec((1, LANES), lambda i: (0, 0))],
                  out_specs=[ts, ts], out_shape=[jax.ShapeDtypeStruct((T, LANES), F32)] * 2,
                  compiler_params=_params(("parallel",)))(pos, _inv_freq_row())


def _swap_halves(r):
    lane = lax.broadcasted_iota(jnp.int32, r.shape, 1)
    return jnp.where(lane < 32, pltpu.roll(r, 96, 1), pltpu.roll(r, 32, 1))


def qk_fwd(nope_arr, nope_blk0, rope_arr, rope_blk0, rope_shared, gn, gr, cos, sin, name):
    T = nope_arr.shape[0]
    tq = _tile(T, (256, 128))

    col = lambda blk: slice(blk * LANES, (blk + 1) * LANES)

    def body(n_ref, r_ref, gn_ref, gr_ref, c_ref, s_ref, o_ref):
        for h in range(MLA_HEADS):
            n = n_ref[:, col(nope_blk0(h))]
            r = r_ref[...] if rope_shared else n_ref[:, col(rope_blk0(h))]
            ms = (jnp.sum(n * n, axis=-1, keepdims=True) + jnp.sum(r * r, axis=-1, keepdims=True)) * (1.0 / MLA_QK)
            rstd = lax.rsqrt(ms + EPS)
            rn = r * rstd * gr_ref[...]
            rr = rn * c_ref[...] + _swap_halves(rn) * s_ref[...]
            o_ref[:, col(2 * h)] = (n * rstd * gn_ref[...]).astype(BF16)
            o_ref[:, col(2 * h + 1)] = rr.astype(BF16)

    wide = lambda arr: pl.BlockSpec((tq, arr.shape[1]), lambda i: (i, 0))
    assert rope_shared or rope_arr is nope_arr
    rspec = pl.BlockSpec((tq, LANES), lambda i: (i, rope_blk0)) if rope_shared else pl.BlockSpec((8, LANES), lambda i: (0, 0))
    row = pl.BlockSpec((1, LANES), lambda i: (0, 0))
    tab = pl.BlockSpec((tq, LANES), lambda i: (i, 0))
    return _pcall(body, name=name, grid=(T // tq,),
                  in_specs=[wide(nope_arr), rspec, row, row, tab, tab],
                  out_specs=pl.BlockSpec((tq, MLA_HEADS * 2 * LANES), lambda i: (i, 0)),
                  out_shape=jax.ShapeDtypeStruct((T, MLA_HEADS * 2 * LANES), BF16),
                  compiler_params=_params(("parallel",)))(nope_arr, rope_arr, gn, gr, cos, sin)


def qk_bwd(nope_arr, nope_blk0, rope_arr, rope_blk0, rope_shared, gn, gr, cos, sin, dout, name, fill=None, fill_blk=None):
    T = nope_arr.shape[0]
    tq = _tile(T, (256, 128))
    col = lambda blk: slice(blk * LANES, (blk + 1) * LANES)

    def body(*refs):
        refs = list(refs)
        n_ref, r_ref, gn_ref, gr_ref, c_ref, s_ref, d_ref = refs[:7]
        f_ref = refs[7] if rope_shared else None
        outs = refs[8:] if rope_shared else refs[7:]
        draw_ref = outs[0]
        dsh_ref = outs[1] if rope_shared else None
        dgn_ref, dgr_ref = outs[-2:]

        @pl.when(pl.program_id(0) == 0)
        def _():
            dgn_ref[...] = jnp.zeros_like(dgn_ref)
            dgr_ref[...] = jnp.zeros_like(dgr_ref)

        dgn = jnp.zeros((1, LANES), F32)
        dgr = jnp.zeros((1, LANES), F32)
        dshared = jnp.zeros((tq, LANES), F32)
        for h in range(MLA_HEADS):
            n = n_ref[:, col(nope_blk0(h))]
            r = r_ref[...] if rope_shared else n_ref[:, col(rope_blk0(h))]
            ms = (jnp.sum(n * n, axis=-1, keepdims=True) + jnp.sum(r * r, axis=-1, keepdims=True)) * (1.0 / MLA_QK)
            rstd = lax.rsqrt(ms + EPS)
            nh = n * rstd
            rh = r * rstd
            d_n = d_ref[:, col(2 * h)].astype(F32)
            d_rr = d_ref[:, col(2 * h + 1)].astype(F32)
            d_rn = d_rr * c_ref[...] - _swap_halves(d_rr) * s_ref[...]
            dgn = dgn + jnp.sum(d_n * nh, axis=0, keepdims=True)
            dgr = dgr + jnp.sum(d_rn * rh, axis=0, keepdims=True)
            dng = d_n * gn_ref[...]
            drg = d_rn * gr_ref[...]
            mean = (jnp.sum(dng * nh, axis=-1, keepdims=True) + jnp.sum(drg * rh, axis=-1, keepdims=True)) * (1.0 / MLA_QK)
            draw_ref[:, col(nope_blk0(h))] = (rstd * (dng - nh * mean)).astype(BF16)
            drope = rstd * (drg - rh * mean)
            if rope_shared:
                dshared = dshared + drope
                draw_ref[:, col(fill_blk(h))] = f_ref[:, col(h)].astype(BF16)
            else:
                draw_ref[:, col(rope_blk0(h))] = drope.astype(BF16)
        dgn_ref[...] += dgn
        dgr_ref[...] += dgr
        if rope_shared:
            dsh_ref[...] = dshared

    wide = lambda arr: pl.BlockSpec((tq, arr.shape[1]), lambda i: (i, 0))
    assert rope_shared or rope_arr is nope_arr
    rspec = pl.BlockSpec((tq, LANES), lambda i: (i, rope_blk0)) if rope_shared else pl.BlockSpec((8, LANES), lambda i: (0, 0))
    row = pl.BlockSpec((1, LANES), lambda i: (0, 0))
    tab = pl.BlockSpec((tq, LANES), lambda i: (i, 0))
    raw = pl.BlockSpec((tq, 2 * MLA_HEADS * LANES), lambda i: (i, 0))
    in_specs = [wide(nope_arr), rspec, row, row, tab, tab, wide(dout)] + ([wide(fill)] if rope_shared else [])
    out_specs = [raw] + ([tab] if rope_shared else []) + [row, row]
    out_shape = ([jax.ShapeDtypeStruct((T, 2 * MLA_HEADS * LANES), BF16)]
                 + ([jax.ShapeDtypeStruct((T, LANES), F32)] if rope_shared else [])
                 + [jax.ShapeDtypeStruct((1, LANES), F32)] * 2)
    args = (nope_arr, rope_arr, gn, gr, cos, sin, dout) + ((fill,) if rope_shared else ())
    return _pcall(body, name=name, grid=(T // tq,), in_specs=in_specs, out_specs=out_specs, out_shape=out_shape,
                  compiler_params=_params(("arbitrary",)))(*args)


def _scores(q, k, scale, causal, row0):
    s = lax.dot_general(q, k, (((1,), (1,)), ((), ())), preferred_element_type=F32) * scale
    if causal:
        qi = row0 + lax.broadcasted_iota(jnp.int32, s.shape, 0)
        ki = lax.broadcasted_iota(jnp.int32, s.shape, 1)
        s = jnp.where(qi >= ki, s, -1e30)
    return s


def attn_fwd(q, k, v, H, dqk, dv, scale, causal, name, v_idx=lambda h: h):
    B, S, _ = q.shape
    Sk = k.shape[1]
    tq = _tile(S, tuple(t for t in (2048, 1024, 512, 256, 128) if t * Sk <= ATTN_SCORE_ELEMS))

    def tile(q_ref, k_ref, v_ref, o_ref, row0, kl):
        s = _scores(q_ref[0], k_ref[0, :kl], scale, causal, row0)
        p = jnp.exp(s - jnp.max(s, axis=-1, keepdims=True))
        l = jnp.sum(p, axis=-1, keepdims=True)
        o = lax.dot_general(p.astype(BF16), v_ref[0, :kl], (((1,), (0,)), ((), ())), preferred_element_type=F32)
        o_ref[0] = (o / l).astype(BF16)

    def body(q_ref, k_ref, v_ref, o_ref):
        if not causal:
            tile(q_ref, k_ref, v_ref, o_ref, 0, Sk)
            return
        for qi in range(S // tq):
            @pl.when(pl.program_id(2) == qi)
            def _(qi=qi):
                tile(q_ref, k_ref, v_ref, o_ref, qi * tq, (qi + 1) * tq)

    return _pcall(body, name=name, grid=(B, H, S // tq),
                  in_specs=[pl.BlockSpec((1, tq, dqk), lambda b, h, i: (b, i, h)),
                            pl.BlockSpec((1, Sk, dqk), lambda b, h, i: (b, 0, h)),
                            pl.BlockSpec((1, Sk, dv), lambda b, h, i: (b, 0, v_idx(h)))],
                  out_specs=pl.BlockSpec((1, tq, dv), lambda b, h, i: (b, i, h)),
                  out_shape=jax.ShapeDtypeStruct((B, S, H * dv), BF16),
                  compiler_params=_params(("parallel", "parallel", "parallel")))(q, k, v)


def attn_bwd(q, k, v, do, H, dqk, dv, scale, causal, name, v_idx=lambda h: h):
    B, S, _ = q.shape
    Sk = k.shape[1]
    tq = _tile(S, tuple(t for t in (2048, 1024, 512, 256, 128) if t * Sk <= ATTN_SCORE_ELEMS))

    def body(q_ref, k_ref, v_ref, do_ref, dq_ref, dk_ref, dv_ref):
        i = pl.program_id(2)

        @pl.when(i == 0)
        def _():
            dk_ref[...] = jnp.zeros_like(dk_ref)
            dv_ref[...] = jnp.zeros_like(dv_ref)

        def tile(row0, kl):
            qv, kv, vv = q_ref[0], k_ref[0, :kl], v_ref[0, :kl]
            s = _scores(qv, kv, scale, causal, row0)
            p = jnp.exp(s - jnp.max(s, axis=-1, keepdims=True))
            p = p / jnp.sum(p, axis=-1, keepdims=True)
            dob = do_ref[0].astype(BF16)
            pb = p.astype(BF16)
            dv_ref[0, :kl] += lax.dot_general(pb, dob, (((0,), (0,)), ((), ())), preferred_element_type=F32)
            dp = lax.dot_general(dob, vv, (((1,), (1,)), ((), ())), preferred_element_type=F32)
            delta = jnp.sum(p * dp, axis=-1, keepdims=True)
            ds = (p * (dp - delta) * scale).astype(BF16)
            dq_ref[0] = lax.dot_general(ds, kv, (((1,), (0,)), ((), ())), preferred_element_type=F32)
            dk_ref[0, :kl] += lax.dot_general(ds, qv, (((0,), (0,)), ((), ())), preferred_element_type=F32)

        if not causal:
            tile(0, Sk)
            return
        for qi in range(S // tq):
            @pl.when(i == qi)
            def _(qi=qi):
                tile(qi * tq, (qi + 1) * tq)

    qs = pl.BlockSpec((1, tq, dqk), lambda b, h, i: (b, i, h))
    ks = pl.BlockSpec((1, Sk, dqk), lambda b, h, i: (b, 0, h))
    vs = pl.BlockSpec((1, Sk, dv), lambda b, h, i: (b, 0, h))
    vin = pl.BlockSpec((1, Sk, dv), lambda b, h, i: (b, 0, v_idx(h)))
    return _pcall(body, name=name, grid=(B, H, S // tq),
                  in_specs=[qs, ks, vin, pl.BlockSpec((1, tq, dv), lambda b, h, i: (b, i, h))],
                  out_specs=[qs, ks, vs],
                  out_shape=[jax.ShapeDtypeStruct((B, S, H * dqk), F32), jax.ShapeDtypeStruct((B, Sk, H * dqk), F32),
                             jax.ShapeDtypeStruct((B, Sk, H * dv), F32)],
                  compiler_params=_params(("parallel", "parallel", "arbitrary")))(q, k, v, do)


def _shift_down(u, j):
    if j == 0:
        return u
    row = lax.broadcasted_iota(jnp.int32, u.shape, 0)
    return jnp.where(row >= j, pltpu.roll(u, j, 0), 0.0)


def _shift_up(u, j):
    if j == 0:
        return u
    n = u.shape[0]
    row = lax.broadcasted_iota(jnp.int32, u.shape, 0)
    return jnp.where(row < n - j, pltpu.roll(u, n - j, 0), 0.0)


def conv_fwd(u, col0_blk, w8, b, name):
    B, S, _ = u.shape
    tc = 256
    nb = SSD_CONV_DIM // tc

    def body(u_ref, w_ref, b_ref, o_ref):
        uv = u_ref[0]
        pre = b_ref[...] + jnp.zeros_like(uv)
        for kk in range(SSD_CONV):
            pre = pre + w_ref[kk:kk + 1, :] * _shift_down(uv, SSD_CONV - 1 - kk)
        o_ref[0] = pre * _sigmoid(pre)

    return _pcall(body, name=name, grid=(B, nb),
                  in_specs=[pl.BlockSpec((1, S, tc), lambda bb, j: (bb, 0, col0_blk + j)),
                            pl.BlockSpec((8, tc), lambda bb, j: (0, j)), pl.BlockSpec((1, tc), lambda bb, j: (0, j))],
                  out_specs=pl.BlockSpec((1, S, tc), lambda bb, j: (bb, 0, j)),
                  out_shape=jax.ShapeDtypeStruct((B, S, SSD_CONV_DIM), F32),
                  compiler_params=_params(("parallel", "parallel")))(u, w8, b)


def conv_bwd(u, col0_blk, w8, b, dy, name):
    B, S, _ = u.shape
    tc = 256
    nb = SSD_CONV_DIM // tc

    def body(u_ref, w_ref, b_ref, dy_ref, du_ref, dw_ref, db_ref):
        bb = pl.program_id(1)

        @pl.when(bb == 0)
        def _():
            dw_ref[...] = jnp.zeros_like(dw_ref)
            db_ref[...] = jnp.zeros_like(db_ref)

        uv = u_ref[0]
        pre = b_ref[...] + jnp.zeros_like(uv)
        for kk in range(SSD_CONV):
            pre = pre + w_ref[kk:kk + 1, :] * _shift_down(uv, SSD_CONV - 1 - kk)
        s = _sigmoid(pre)
        dpre = dy_ref[0] * (s * (1.0 + pre * (1.0 - s)))
        du = jnp.zeros_like(uv)
        for kk in range(SSD_CONV):
            j = SSD_CONV - 1 - kk
            du = du + w_ref[kk:kk + 1, :] * _shift_up(dpre, j)
            dw_ref[kk:kk + 1, :] += jnp.sum(dpre * _shift_down(uv, j), axis=0, keepdims=True)
        db_ref[...] += jnp.sum(dpre, axis=0, keepdims=True)
        du_ref[0] = du

    return _pcall(body, name=name, grid=(nb, B),
                  in_specs=[pl.BlockSpec((1, S, tc), lambda j, bb: (bb, 0, col0_blk + j)),
                            pl.BlockSpec((8, tc), lambda j, bb: (0, j)), pl.BlockSpec((1, tc), lambda j, bb: (0, j)),
                            pl.BlockSpec((1, S, tc), lambda j, bb: (bb, 0, j))],
                  out_specs=[pl.BlockSpec((1, S, tc), lambda j, bb: (bb, 0, j)),
                             pl.BlockSpec((8, tc), lambda j, bb: (0, j)), pl.BlockSpec((1, tc), lambda j, bb: (0, j))],
                  out_shape=[jax.ShapeDtypeStruct((B, S, SSD_CONV_DIM), F32), jax.ShapeDtypeStruct((8, SSD_CONV_DIM), F32),
                             jax.ShapeDtypeStruct((1, SSD_CONV_DIM), F32)],
                  compiler_params=_params(("arbitrary", "arbitrary")))(u, w8, b, dy)


def _expand_mat():
    e = np.zeros((LANES, SSD_GW), np.float32)
    for h in range(SSD_HPG):
        e[h, h * SSD_HEAD_DIM:(h + 1) * SSD_HEAD_DIM] = 1.0
    return jnp.asarray(e)


def _tri_mat():
    return jnp.asarray(np.tril(np.ones((SSD_CHUNK, SSD_CHUNK), np.float32)))


def _split3(x):
    hi = x.astype(BF16)
    r = x - hi.astype(F32)
    mid = r.astype(BF16)
    return hi, mid, (r - mid.astype(F32)).astype(BF16)


def _dotf(a, b, dn=(((1,), (0,)), ((), ())), sel="b"):
    if sel == "b":
        s = b.astype(BF16)
        parts = [lax.dot_general(p, s, dn, preferred_element_type=F32) for p in _split3(a)]
    else:
        s = a.astype(BF16)
        parts = [lax.dot_general(s, p, dn, preferred_element_type=F32) for p in _split3(b)]
    return parts[0] + parts[1] + parts[2]


def _dotb(a, b, dn=(((1,), (0,)), ((), ()))):
    return lax.dot_general(a.astype(BF16), b.astype(BF16), dn, preferred_element_type=F32)


NT = (((1,), (1,)), ((), ()))
TN = (((0,), (0,)), ((), ()))


def _ssd_common(dt_ref, par_ref, tri_ref, e_ref):
    dtr = dt_ref[0]
    xb = dtr + par_ref[0:1, :]
    dt = jnp.maximum(xb, 0.0) + jnp.log(1.0 + jnp.exp(-jnp.abs(xb)))
    A = -jnp.exp(par_ref[1:2, :])
    a = dt * A
    acs = _dotf(tri_ref[...], a, sel="a")
    alast = acs[SSD_CHUNK - 1:SSD_CHUNK, :]
    E = e_ref[...]
    acsE = _dotf(acs, E)
    alastE = acsE[SSD_CHUNK - 1:SSD_CHUNK, :]
    return dict(xb=xb, dt=dt, A=A, acs=acs, acsT=acs.T, ealast=jnp.exp(alast), w=jnp.exp(alast - acs),
                dtE=_dotf(dt, E), eacsE=jnp.exp(acsE), wE=jnp.exp(alastE - acsE), ealastE=jnp.exp(alastE),
                DE=_dotf(par_ref[...], E)[2:3, :])


def _decay(c, e):
    diff = c["acs"][:, e:e + 1] - c["acsT"][e:e + 1, :]
    li = lax.broadcasted_iota(jnp.int32, diff.shape, 0)
    si = lax.broadcasted_iota(jnp.int32, diff.shape, 1)
    return jnp.where(li >= si, jnp.exp(jnp.minimum(diff, 0.0)), 0.0)


def ssd_fwd(xbc, dtp, par, name):
    B, S, _ = xbc.shape
    nc = S // SSD_CHUNK
    L, N, GW, P = SSD_CHUNK, SSD_STATE, SSD_GW, SSD_HEAD_DIM

    def body(x_ref, b_ref, c_ref, dt_ref, par_ref, tri_ref, e_ref, y_ref, st_ref, S_scr):
        @pl.when(pl.program_id(2) == 0)
        def _():
            S_scr[...] = jnp.zeros_like(S_scr)

        Sin = S_scr[...]
        st_ref[0, 0, 0] = Sin
        c = _ssd_common(dt_ref, par_ref.at[0], tri_ref, e_ref)
        X = x_ref[0]
        Bm = b_ref[0]
        Cm = c_ref[0]
        xdt = X * c["dtE"]
        CB = _dotb(Cm, Bm, NT)
        y_ref[0] = c["eacsE"] * _dotb(Cm, Sin) + c["DE"] * X
        for e in range(SSD_HPG):
            M = CB * _decay(c, e)
            y_ref[0, :, e * P:(e + 1) * P] += _dotb(M, xdt[:, e * P:(e + 1) * P])
        S_scr[...] = c["ealastE"] * Sin + _dotb(Bm, xdt * c["wE"], TN)

    return _pcall(
        body, name=name, grid=(SSD_GROUPS, B, nc),
        in_specs=[pl.BlockSpec((1, L, GW), lambda g, b, ci: (b, ci, g)),
                  pl.BlockSpec((1, L, N), lambda g, b, ci: (b, ci, 8 + g)),
                  pl.BlockSpec((1, L, N), lambda g, b, ci: (b, ci, 10 + g)),
                  pl.BlockSpec((1, L, LANES), lambda g, b, ci: (b, ci, g)),
                  pl.BlockSpec((1, 8, LANES), lambda g, b, ci: (g, 0, 0)),
                  pl.BlockSpec((L, L), lambda g, b, ci: (0, 0)),
                  pl.BlockSpec((LANES, GW), lambda g, b, ci: (0, 0))],
        out_specs=[pl.BlockSpec((1, L, GW), lambda g, b, ci: (b, ci, g)),
                   pl.BlockSpec((1, 1, 1, N, GW), lambda g, b, ci: (g, b, ci, 0, 0))],
        out_shape=[jax.ShapeDtypeStruct((B, S, SSD_INNER), F32), jax.ShapeDtypeStruct((SSD_GROUPS, B, nc, N, GW), F32)],
        scratch_shapes=[pltpu.VMEM((N, GW), F32)],
        compiler_params=_params(("arbitrary", "arbitrary", "arbitrary")))(xbc, xbc, xbc, dtp, par, _tri_mat(), _expand_mat())


def ssd_bwd(xbc, dtp, par, states, dy, name):
    B, S, _ = xbc.shape
    nc = S // SSD_CHUNK
    L, N, GW, P = SSD_CHUNK, SSD_STATE, SSD_GW, SSD_HEAD_DIM

    def body(x_ref, b_ref, c_ref, dt_ref, par_ref, tri_ref, e_ref, st_ref, dy_ref,
             dx_ref, db_ref, dc_ref, ddt_ref, dpar_ref, dS_scr, dxdt_scr):
        @pl.when(pl.program_id(2) == 0)
        def _():
            dS_scr[...] = jnp.zeros_like(dS_scr)

        @pl.when((pl.program_id(1) == 0) & (pl.program_id(2) == 0))
        def _():
            dpar_ref[...] = jnp.zeros_like(dpar_ref)

        c = _ssd_common(dt_ref, par_ref.at[0], tri_ref, e_ref)
        E = e_ref[...]
        red = lambda t: _dotf(t, E, NT)
        red_row = lambda t: red(jnp.broadcast_to(t, (8, GW)))[0:1, :]
        X, Bm, Cm = x_ref[0], b_ref[0], c_ref[0]
        Sin = st_ref[0, 0, 0]
        dY = dy_ref[0]
        dSo = dS_scr[...]
        xdt = X * c["dtE"]
        CB = _dotb(Cm, Bm, NT)
        dD = red_row(jnp.sum(dY * X, axis=0, keepdims=True))
        yoff = c["eacsE"] * _dotb(Cm, Sin)
        dacs = red(dY * yoff)
        dYe = dY * c["eacsE"]
        dC = _dotb(dYe, Sin, NT)
        dSin = _dotb(Cm, dYe, TN) + c["ealastE"] * dSo
        dB = _dotb(xdt * c["wE"], dSo, NT)
        dxw = _dotb(Bm, dSo)
        t = red(dxw * xdt) * c["w"]
        dacs = dacs - t
        dalast = jnp.sum(t, axis=0, keepdims=True) + c["ealast"] * red_row(jnp.sum(Sin * dSo, axis=0, keepdims=True))
        dxdt_scr[...] = dxw * c["wE"]
        dCB = jnp.zeros((L, L), F32)
        dacsT = jnp.zeros((LANES, L), F32)
        lane = lax.broadcasted_iota(jnp.int32, (1, LANES), 1)
        subl = lax.broadcasted_iota(jnp.int32, (LANES, 1), 0)
        for e in range(SSD_HPG):
            Lm = _decay(c, e)
            M = CB * Lm
            dYh = dY[:, e * P:(e + 1) * P]
            dM = _dotb(dYh, xdt[:, e * P:(e + 1) * P], NT)
            dxdt_scr[:, e * P:(e + 1) * P] += _dotb(M, dYh, TN)
            dCBe = dM * Lm
            dCB = dCB + dCBe
            Gm = dCBe * CB
            dacs = dacs + jnp.sum(Gm, axis=1, keepdims=True) * (lane == e).astype(F32)
            dacsT = dacsT - jnp.sum(Gm, axis=0, keepdims=True) * (subl == e).astype(F32)
        dC = dC + _dotb(dCB, Bm)
        dB = dB + _dotb(dCB, Cm, TN)
        dacs = dacs + dacsT.T
        row = lax.broadcasted_iota(jnp.int32, (L, 1), 0)
        dacs = dacs + jnp.where(row == L - 1, dalast, 0.0)
        da = _dotf(tri_ref[...], dacs, TN, sel="a")
        dxdt = dxdt_scr[...]
        ddt = da * c["A"] + red(dxdt * X)
        dA = jnp.sum(da * c["dt"], axis=0, keepdims=True)
        ddtr = ddt * _sigmoid(c["xb"])
        dx_ref[0] = c["DE"] * dY + dxdt * c["dtE"]
        db_ref[0] = dB
        dc_ref[0] = dC
        ddt_ref[0] = ddtr
        dpar_ref[0, 0:1, :] += jnp.sum(ddtr, axis=0, keepdims=True)
        dpar_ref[0, 1:2, :] += dA * c["A"]
        dpar_ref[0, 2:3, :] += dD
        dS_scr[...] = dSin

    rc = lambda ci: nc - 1 - ci
    xspec = pl.BlockSpec((1, L, GW), lambda g, b, ci: (b, rc(ci), g))
    return _pcall(
        body, name=name, grid=(SSD_GROUPS, B, nc),
        in_specs=[xspec,
                  pl.BlockSpec((1, L, N), lambda g, b, ci: (b, rc(ci), 8 + g)),
                  pl.BlockSpec((1, L, N), lambda g, b, ci: (b, rc(ci), 10 + g)),
                  pl.BlockSpec((1, L, LANES), lambda g, b, ci: (b, rc(ci), g)),
                  pl.BlockSpec((1, 8, LANES), lambda g, b, ci: (g, 0, 0)),
                  pl.BlockSpec((L, L), lambda g, b, ci: (0, 0)),
                  pl.BlockSpec((LANES, GW), lambda g, b, ci: (0, 0)),
                  pl.BlockSpec((1, 1, 1, N, GW), lambda g, b, ci: (g, b, rc(ci), 0, 0)),
                  xspec],
        out_specs=[xspec,
                   pl.BlockSpec((1, L, N), lambda g, b, ci: (b, rc(ci), g)),
                   pl.BlockSpec((1, L, N), lambda g, b, ci: (b, rc(ci), g)),
                   pl.BlockSpec((1, L, LANES), lambda g, b, ci: (b, rc(ci), g)),
                   pl.BlockSpec((1, 8, LANES), lambda g, b, ci: (g, 0, 0))],
        out_shape=[jax.ShapeDtypeStruct((B, S, SSD_INNER), F32), jax.ShapeDtypeStruct((B, S, SSD_GROUPS * N), F32),
                   jax.ShapeDtypeStruct((B, S, SSD_GROUPS * N), F32), jax.ShapeDtypeStruct((B, S, SSD_GROUPS * LANES), F32),
                   jax.ShapeDtypeStruct((SSD_GROUPS, 8, LANES), F32)],
        scratch_shapes=[pltpu.VMEM((N, GW), F32), pltpu.VMEM((L, GW), F32)],
        compiler_params=_params(("arbitrary", "arbitrary", "arbitrary")))(
            xbc, xbc, xbc, dtp, par, _tri_mat(), _expand_mat(), states, dy)


def gate_up_act(h, wg, wu, name):
    T, D = h.shape
    c = wg.shape[-1]
    tm = _tile(T, (512, 256, 128))

    def body(h_ref, g_w, u_w, g_ref, u_ref, a_ref):
        hv = h_ref[...]
        g = jnp.dot(hv, g_w[...], preferred_element_type=F32)
        u = jnp.dot(hv, u_w[...], preferred_element_type=F32)
        g_ref[...] = g.astype(BF16)
        u_ref[...] = u.astype(BF16)
        a_ref[...] = (g * _sigmoid(g) * u).astype(BF16)

    ws = pl.BlockSpec((None, None, D, c), lambda i, j: (0, j, 0, 0))
    os_ = pl.BlockSpec((tm, c), lambda i, j: (i, j))
    return _pcall(body, name=name, grid=(T // tm, 4), in_specs=[pl.BlockSpec((tm, D), lambda i, j: (i, 0)), ws, ws],
                  out_specs=[os_, os_, os_], out_shape=[jax.ShapeDtypeStruct((T, 4 * c), BF16)] * 3,
                  compiler_params=_params(("parallel", "parallel")))(h, wg, wu)


def gate_up_dx(dg, du, wg, wu, name):
    T = dg.shape[0]
    _, _, D, c = wg.shape
    tm, tn = _tile(T, (1024, 512, 256, 128)), _tile(D, (1024, 512, 256, 128))

    def body(g_ref, u_ref, gw_ref, uw_ref, o_ref, acc):
        k = pl.program_id(2)
        d = (lax.dot_general(g_ref[...], gw_ref[...], NT, preferred_element_type=F32)
             + lax.dot_general(u_ref[...], uw_ref[...], NT, preferred_element_type=F32))

        @pl.when(k == 0)
        def _():
            acc[...] = d

        @pl.when((k > 0) & (k < 3))
        def _():
            acc[...] += d

        @pl.when(k == 3)
        def _():
            o_ref[...] = acc[...] + d

    a_spec = pl.BlockSpec((tm, c), lambda i, j, k: (i, k))
    w_spec = pl.BlockSpec((None, None, tn, c), lambda i, j, k: (0, k, j, 0))
    return _pcall(body, name=name, grid=(T // tm, D // tn, 4), in_specs=[a_spec, a_spec, w_spec, w_spec],
                  out_specs=pl.BlockSpec((tm, tn), lambda i, j, k: (i, j)), out_shape=jax.ShapeDtypeStruct((T, D), F32),
                  scratch_shapes=[pltpu.VMEM((tm, tn), F32)],
                  compiler_params=_params(("parallel", "parallel", "arbitrary")))(dg, du, wg, wu)


def down_dx_swiglu(dy, w_down, gate, up, name, dep=None):
    T, D = dy.shape
    Fh = w_down.shape[0]
    tm, tn = _tile(T, (1024, 512, 256, 128)), 512

    def body(dy_ref, w_ref, g_ref, u_ref, *rest):
        dg_ref, du_ref = rest[-2:]
        d = lax.dot_general(dy_ref[...], w_ref[...], NT, preferred_element_type=F32)
        g = g_ref[...].astype(F32)
        u = u_ref[...].astype(F32)
        s = _sigmoid(g)
        dg_ref[...] = (d * u * (s * (1.0 + g * (1.0 - s)))).astype(BF16)
        du_ref[...] = (d * g * s).astype(BF16)

    ts = pl.BlockSpec((tm, tn), lambda i, j: (i, j))
    in_specs = [pl.BlockSpec((tm, D), lambda i, j: (i, 0)), pl.BlockSpec((tn, D), lambda i, j: (j, 0)), ts, ts]
    args = (dy, w_down, gate, up)
    if dep is not None:
        in_specs, args = in_specs + [pl.BlockSpec(memory_space=pl.ANY)], args + (dep,)
    return _pcall(body, name=name, grid=(T // tm, Fh // tn), in_specs=in_specs, out_specs=[ts, ts],
                  out_shape=[jax.ShapeDtypeStruct((T, Fh), BF16)] * 2,
                  compiler_params=_params(("parallel", "parallel")))(*args)


def loss_head(y, target):
    T, D = y.shape
    tq = _tile(T, (256, 128))

    def body(y_ref, t_ref, dy_ref, l_ref, dyb_ref):
        @pl.when(pl.program_id(0) == 0)
        def _():
            l_ref[...] = jnp.zeros_like(l_ref)

        e = y_ref[...] - t_ref[...]
        dy = e * (1.0 / D)
        dy_ref[...] = dy
        dyb_ref[...] = dy.astype(BF16)
        l_ref[...] += 0.5 * jnp.sum(jnp.mean(e * e, axis=-1, keepdims=True))

    xs = pl.BlockSpec((tq, D), lambda i: (i, 0))
    return _pcall(body, name="loss_head", grid=(T // tq,), in_specs=[xs, xs],
                  out_specs=[xs, pl.BlockSpec((8, LANES), lambda i: (0, 0)), xs],
                  out_shape=[jax.ShapeDtypeStruct((T, D), F32), jax.ShapeDtypeStruct((8, LANES), F32),
                             jax.ShapeDtypeStruct((T, D), BF16)],
                  compiler_params=_params(("arbitrary",)))(y, target)


def sum_arrays(arrs, out_dtype, name):
    R, C = arrs[0].shape
    tr = _tile(R, (512, 256, 128, 64, 32, 16, 8))
    n = len(arrs)

    def body(*refs):
        acc = refs[0][...].astype(F32)
        for r in refs[1:n]:
            acc = acc + r[...].astype(F32)
        refs[n][...] = acc.astype(out_dtype)

    xs = pl.BlockSpec((tr, C), lambda i: (i, 0))
    return _pcall(body, name=name, grid=(R // tr,), in_specs=[xs] * n, out_specs=xs,
                  out_shape=jax.ShapeDtypeStruct((R, C), out_dtype), compiler_params=_params(("parallel",)))(*arrs)


def adamw(w, g, m, v, name):
    R, C = w.shape
    tr = _tile(R, (256, 128, 64, 32, 16, 8))
    c1 = 1.0 - ADAM_B1 ** ADAM_STEP
    c2 = 1.0 - ADAM_B2 ** ADAM_STEP

    def body(w_ref, g_ref, m_ref, v_ref, d_ref, nm_ref, nv_ref):
        gv = g_ref[...]
        mn = ADAM_B1 * m_ref[...] + (1.0 - ADAM_B1) * gv
        vn = ADAM_B2 * v_ref[...] + (1.0 - ADAM_B2) * (gv * gv)
        nm_ref[...] = mn
        nv_ref[...] = vn
        d_ref[...] = -ADAM_LR * ((mn / c1) / (jnp.sqrt(vn / c2) + ADAM_EPS) + ADAM_WD * w_ref[...])

    xs = pl.BlockSpec((tr, C), lambda i: (i, 0))
    return _pcall(body, name=name, grid=(R // tr,), in_specs=[xs] * 4, out_specs=[xs] * 3,
                  out_shape=[jax.ShapeDtypeStruct((R, C), F32)] * 3, compiler_params=_params(("parallel",)))(w, g, m, v)


def adamw_layer(w, g, m, v, l, prev, name, g_own=None):
    if g_own is not None:
        return _adamw_layer_halves(w, g, g_own, m, v, l, prev, name)
    L, r, c = w.shape
    tr = _tile(r, (256, 128, 64, 32, 16, 8))
    c1 = 1.0 - ADAM_B1 ** ADAM_STEP
    c2 = 1.0 - ADAM_B2 ** ADAM_STEP
    chained = prev is not None

    def body(w_ref, g_ref, m_ref, v_ref, *rest):
        go_ref, d_ref, nm_ref, nv_ref = rest[-4:]
        gv = g_ref[...]
        mn = ADAM_B1 * m_ref[...] + (1.0 - ADAM_B1) * gv
        vn = ADAM_B2 * v_ref[...] + (1.0 - ADAM_B2) * (gv * gv)
        go_ref[...] = gv
        nm_ref[...] = mn
        nv_ref[...] = vn
        d_ref[...] = -ADAM_LR * ((mn / c1) / (jnp.sqrt(vn / c2) + ADAM_EPS) + ADAM_WD * w_ref[...])

    ls = pl.BlockSpec((None, tr, c), lambda i: (l, i, 0))
    gs = pl.BlockSpec((tr, c), lambda i: (i, 0))
    anys = [pl.BlockSpec(memory_space=pl.ANY)] * 4 if chained else []
    return _pcall(body, name=name, grid=(r // tr,), in_specs=[ls, gs, ls, ls] + anys, out_specs=[ls] * 4,
                  out_shape=[jax.ShapeDtypeStruct((L, r, c), F32)] * 4,
                  input_output_aliases={4: 0, 5: 1, 6: 2, 7: 3} if chained else {},
                  compiler_params=_params(("parallel",)))(w, g, m, v, *(prev if chained else ()))


def _adamw_layer_halves(w, g_join, g_own, m, v, l, prev, name):
    L, r, c = w.shape
    h = r // 2
    tr = _tile(h, (256, 128, 64, 32, 16, 8))
    nh = h // tr
    c1 = 1.0 - ADAM_B1 ** ADAM_STEP
    c2 = 1.0 - ADAM_B2 ** ADAM_STEP
    chained = prev is not None

    def body(c_ref, w_ref, gj_ref, go_ref, m_ref, v_ref, *rest):
        gout_ref, d_ref, nm_ref, nv_ref = rest[-4:]
        i = pl.program_id(0)
        mine = (i >= c_ref[0] * nh) & (i < (c_ref[0] + 1) * nh)
        gv = jnp.where(mine, go_ref[...], gj_ref[...])
        mn = ADAM_B1 * m_ref[...] + (1.0 - ADAM_B1) * gv
        vn = ADAM_B2 * v_ref[...] + (1.0 - ADAM_B2) * (gv * gv)
        gout_ref[...] = gv
        nm_ref[...] = mn
        nv_ref[...] = vn
        d_ref[...] = -ADAM_LR * ((mn / c1) / (jnp.sqrt(vn / c2) + ADAM_EPS) + ADAM_WD * w_ref[...])

    ls = pl.BlockSpec((None, tr, c), lambda i, cs: (l, i, 0))
    gj = pl.BlockSpec((tr, c), lambda i, cs: (i, 0))
    go = pl.BlockSpec((tr, c), lambda i, cs: (jnp.clip(i - cs[0] * nh, 0, nh - 1), 0))
    anys = [pl.BlockSpec(memory_space=pl.ANY)] * 4 if chained else []
    spec = pltpu.PrefetchScalarGridSpec(num_scalar_prefetch=1, grid=(r // tr,), in_specs=[ls, gj, go, ls, ls] + anys,
                                        out_specs=[ls] * 4)
    return _pcall(body, name=name, grid_spec=spec, out_shape=[jax.ShapeDtypeStruct((L, r, c), F32)] * 4,
                  input_output_aliases={6: 0, 7: 1, 8: 2, 9: 3} if chained else {},
                  compiler_params=_params(("arbitrary",)))(
                      lax.axis_index("c").reshape(1).astype(jnp.int32), w, g_join, g_own, m, v, *(prev if chained else ()))


def add_pair(g, got, name):
    _, r, c = g.shape
    h = r // 2

    def body(c_ref, g_ref, o_ref, out_ref):
        del c_ref
        out_ref[...] = (g_ref[...].astype(F32) + o_ref[...].astype(F32)).astype(BF16)

    spec = pltpu.PrefetchScalarGridSpec(
        num_scalar_prefetch=1, grid=(4,),
        in_specs=[pl.BlockSpec((None, h, c), lambda k, cs: (k, cs[0], 0)), pl.BlockSpec((None, h, c), lambda k, cs: (k, 0, 0))],
        out_specs=pl.BlockSpec((None, h, c), lambda k, cs: (k, 0, 0)))
    return _pcall(body, name=name, grid_spec=spec, out_shape=jax.ShapeDtypeStruct((4, h, c), BF16),
                  compiler_params=_params(("arbitrary",)))(lax.axis_index("c").reshape(1).astype(jnp.int32), g, got)


def add_chips(p, b, name):
    _, h, c = p.shape
    tr = _tile(h, (256, 128, 64, 32, 16, 8))

    def body(me_ref, p_ref, b0, b1, b2, b3, out_ref):
        me = me_ref[0]
        acc = jnp.zeros(out_ref.shape, F32)
        for k, bk in enumerate((b0, b1, b2, b3)):
            acc = acc + jnp.where(me == k, p_ref[...], bk[...]).astype(F32)
        out_ref[...] = acc

    slot = lambda k: pl.BlockSpec((None, tr, c), lambda i, ms: (jnp.where(ms[0] == k, (k + 1) % 4, k), i, 0))
    spec = pltpu.PrefetchScalarGridSpec(
        num_scalar_prefetch=1, grid=(h // tr,),
        in_specs=[pl.BlockSpec((None, tr, c), lambda i, ms: (ms[0], i, 0)), slot(0), slot(1), slot(2), slot(3)],
        out_specs=pl.BlockSpec((tr, c), lambda i, ms: (i, 0)))
    me = (2 * lax.axis_index("x") + lax.axis_index("y")).reshape(1).astype(jnp.int32)
    return _pcall(body, name=name, grid_spec=spec, out_shape=jax.ShapeDtypeStruct((h, c), F32),
                  compiler_params=_params(("arbitrary",)))(me, p, b, b, b, b)


ANY = pl.BlockSpec(memory_space=pl.ANY)


def _place():
    x, y, c = lax.axis_index("x"), lax.axis_index("y"), lax.axis_index("c")
    chips = [(1 - x, y), (x, 1 - y), (1 - x, 1 - y)]
    return x, y, c, chips


def _rcopy(src, dst, ssem, rsem, dev):
    return pltpu.make_async_remote_copy(src_ref=src, dst_ref=dst, send_sem=ssem, recv_sem=rsem, device_id=dev,
                                        device_id_type=MESH)


def _half(c, h):
    return pl.ds(pl.multiple_of(c * h, 64), h)


def _me_chip():
    return 2 * lax.axis_index("x") + lax.axis_index("y")


HBM = pl.BlockSpec(memory_space=pltpu.HBM)
SEM = pl.BlockSpec(memory_space=pltpu.SEMAPHORE)
EFFECT = pltpu.SideEffectType.DATAFLOW_SIDE_EFFECTING


def _ici_copy(kind, src, land, x, y, c, px, py, ssem, rsem, receiving):
    me, peer = 2 * x + y, 2 * px + py
    if kind == "pair":
        return _rcopy(src.at[:, _half(1 - c, land.shape[1])], land, ssem, rsem, (x, y, 1 - c))
    if kind == "gather_both":
        rows = _half(1 - c if receiving else c, land.shape[1] // 2)
        return _rcopy(land.at[me, rows], land.at[peer if receiving else me, rows], ssem, rsem, (px, py, 1 - c))
    if kind == "gather":
        rows = _half(c, land.shape[1] // 2)
        s, d = land.at[me, rows], land.at[peer if receiving else me, rows]
    else:
        s, d = src.at[me if receiving else peer], land.at[peer if receiving else me]
    return _rcopy(s, d, ssem, rsem, (px, py, c))


def _copy_plan(kind):
    if kind == "pair":
        return [("pair", 0)]
    if kind == "gather_all":
        return [("gather", j) for j in range(3)] + [("gather_both", j) for j in range(3)]
    return [(kind, j) for j in range(3)]


def ici_start(kind, srcs, lands, name):
    n, ns = len(lands), len(srcs)
    plan = _copy_plan(kind)
    P = len(plan)
    m = P * n
    na = ns + n

    def body(*refs):
        src, land, token = (refs[:ns] if ns else [None] * n), refs[ns:na], refs[-1]
        ssem, rsem = refs[na:na + m], refs[na + m:na + 2 * m]
        x, y, c, chips = _place()
        for w in range(n):
            for j, (kd, ci) in enumerate(plan):
                px, py = chips[ci]
                _ici_copy(kd, src[w], land[w], x, y, c, px, py, ssem[P * w + j], rsem[P * w + j], False).start()
        token[...] = jnp.zeros_like(token)

    arrs = [pltpu.with_memory_space_constraint(a, pltpu.HBM) for a in list(srcs) + list(lands)]
    outs = _pcall(
        body, name=name,
        out_shape=(*[pltpu.SemaphoreType.DMA(())] * (2 * m), *[pltpu.HBM(a.shape, a.dtype) for a in arrs],
                   jax.ShapeDtypeStruct((8, LANES), F32)),
        in_specs=[HBM] * na,
        out_specs=(*([SEM] * (2 * m)), *([HBM] * na), pl.BlockSpec(memory_space=pltpu.VMEM)),
        input_output_aliases={i: 2 * m + i for i in range(na)},
        compiler_params=pltpu.CompilerParams(has_side_effects=EFFECT))(*arrs)
    return dict(kind=kind, ssem=list(outs[:m]), rsem=list(outs[m:2 * m]), srcs=list(outs[2 * m:2 * m + ns]),
                lands=list(outs[2 * m + ns:2 * m + na])), outs[-1]


def ici_wait(state, after, name):
    kind, n, ns = state["kind"], len(state["lands"]), len(state["srcs"])
    plan = _copy_plan(kind)
    P = len(plan)
    m = P * n
    na = ns + n

    def body(*refs):
        src, land = (refs[:ns] if ns else [None] * n), refs[ns:na]
        ssem, rsem = refs[na:na + m], refs[na + m:na + 2 * m]
        x, y, c, chips = _place()
        for w in range(n):
            for j, (kd, ci) in enumerate(plan):
                px, py = chips[ci]
                _ici_copy(kd, src[w], land[w], x, y, c, px, py, ssem[P * w + j], rsem[P * w + j], False).wait_send()
                _ici_copy(kd, src[w], land[w], x, y, c, px, py, ssem[P * w + j], rsem[P * w + j], True).wait_recv()

    outs = _pcall(
        body, name=name,
        out_shape=tuple(pltpu.HBM(a.shape, a.dtype) for a in state["srcs"] + state["lands"]),
        in_specs=[HBM] * na + [SEM] * (2 * m) + [ANY], out_specs=tuple([HBM] * na),
        input_output_aliases={i: i for i in range(na)},
        compiler_params=pltpu.CompilerParams(has_side_effects=EFFECT))(
            *state["srcs"], *state["lands"], *state["ssem"], *state["rsem"], after)
    return list(outs[:ns]), list(outs[ns:])


def forward_pair(lands, name):
    n = len(lands)

    def body(*refs):
        land, (ssem, rsem) = refs[:n], refs[2 * n:]
        x, y, c, chips = _place()
        sib = (x, y, 1 - c)
        cps = []
        for w in range(n):
            h = land[w].shape[1] // 2
            for j, (px, py) in enumerate(chips):
                part = land[w].at[2 * px + py, _half(c, h)]
                cp = _rcopy(part, part, ssem.at[w, j], rsem.at[w, j], sib)
                cp.start()
                cps.append(cp)
        for w in range(n):
            h = land[w].shape[1] // 2
            for j, (px, py) in enumerate(chips):
                part = land[w].at[2 * px + py, _half(1 - c, h)]
                _rcopy(part, part, ssem.at[w, j], rsem.at[w, j], sib).wait_recv()
        for cp in cps:
            cp.wait_send()

    return _pcall(body, name=name, in_specs=[ANY] * n, out_specs=[ANY] * n,
                  out_shape=[jax.ShapeDtypeStruct(z.shape, z.dtype) for z in lands],
                  input_output_aliases={i: i for i in range(n)},
                  scratch_shapes=[pltpu.SemaphoreType.DMA((n, 3)), pltpu.SemaphoreType.DMA((n, 3))])(*lands)


def join_pair(rs, name):
    n = len(rs)

    def body(*refs):
        srcs, outs, (ssem, rsem) = refs[:n], refs[n:2 * n], refs[2 * n:]
        x, y, c, _ = _place()
        cps = []
        for w in range(n):
            h = srcs[w].shape[0]
            cp = _rcopy(srcs[w], outs[w].at[_half(c, h)], ssem.at[w], rsem.at[w], (x, y, 1 - c))
            cp.start()
            cps.append(cp)
        for w in range(n):
            h = srcs[w].shape[0]
            _rcopy(srcs[w], outs[w].at[_half(1 - c, h)], ssem.at[w], rsem.at[w], (x, y, 1 - c)).wait_recv()
        for cp in cps:
            cp.wait_send()

    outs = _pcall(body, name=name, in_specs=[ANY] * n, out_specs=[ANY] * n,
                  out_shape=[jax.ShapeDtypeStruct((2 * r.shape[0], r.shape[1]), r.dtype) for r in rs],
                  scratch_shapes=[pltpu.SemaphoreType.DMA((n,)), pltpu.SemaphoreType.DMA((n,))])(*rs)
    return outs


def fill_own_half(joined, own):
    return lax.dynamic_update_slice(joined, own, (lax.axis_index("c") * own.shape[0], 0))


def gather_all_devices(v, name):
    R, C = v.shape

    def body(src, out, ssem, rsem, lsem):
        x, y, c, _ = _place()
        me = 4 * x + 2 * y + c
        local = pltpu.make_async_copy(src, out.at[me], lsem)
        local.start()
        sends = []
        for j in range(1, 8):
            dx, dy, dc = (j >> 2) & 1, (j >> 1) & 1, j & 1
            px, py, pc = (1 - x if dx else x), (1 - y if dy else y), (1 - c if dc else c)
            cp = _rcopy(src, out.at[me], ssem.at[j - 1], rsem.at[j - 1], (px, py, pc))
            cp.start()
            sends.append(cp)
        for j in range(1, 8):
            dx, dy, dc = (j >> 2) & 1, (j >> 1) & 1, j & 1
            px, py, pc = (1 - x if dx else x), (1 - y if dy else y), (1 - c if dc else c)
            _rcopy(src, out.at[4 * px + 2 * py + pc], ssem.at[j - 1], rsem.at[j - 1], (px, py, pc)).wait_recv()
        for cp in sends:
            cp.wait_send()
        local.wait()

    return _pcall(body, name=name, in_specs=[ANY], out_specs=ANY, out_shape=jax.ShapeDtypeStruct((8, R, C), v.dtype),
                  scratch_shapes=[pltpu.SemaphoreType.DMA((7,)), pltpu.SemaphoreType.DMA((7,)), pltpu.SemaphoreType.DMA(())])(v)


def reduce_scatter_pair(gw, tag):
    names = list(gw)
    gs = [gw[k] for k in names]
    lands = [lax.empty((4, g.shape[1] // 2, g.shape[2]), g.dtype) for g in gs]
    state, token = ici_start("pair", gs, lands, "rs_pair_start_" + tag)
    state["names"] = names
    return state, token


def reduce_scatter_begin(pair_state, after, tag):
    names = pair_state["names"]
    gs, got = ici_wait(pair_state, after, "rs_pair_wait_" + tag)
    ps = [add_pair(g, o, "rs_add_pair_%s_%s" % (k, tag)) for k, g, o in zip(names, gs, got)]
    state, token = ici_start("scatter", ps, [lax.empty(p.shape, p.dtype) for p in ps], "rs_chips_start_" + tag)
    state["names"] = names
    return state, token


def reduce_scatter_end(state, after, tag):
    names = state["names"]
    ps, bs = ici_wait(state, after, "rs_chips_wait_" + tag)
    rs = [add_chips(p, b, "rs_add_chips_%s_%s" % (k, tag)) for k, p, b in zip(names, ps, bs)]
    return dict(zip(names, zip(join_pair(rs, "rs_join_" + tag), rs)))


GATHERED = ("w_in", "w_q_b", "w_kv_b", "w_out", "w_xq", "w_xk", "w_xv", "w_xo", "w_gate", "w_up", "w_down")
KIND = {"w_q_b": "col", "w_kv_b": "col", "w_out": "row", "w_xq": "row", "w_xk": "row", "w_xv": "row", "w_xo": "col",
        "w_gate": "col", "w_up": "col", "w_down": "row"}


def wq_to_storage(w):
    lead = w.shape[:-1]
    w = w.reshape(lead + (2, MLA_QK))
    rope = jnp.pad(w[..., MLA_NOPE:], [(0, 0)] * (w.ndim - 1) + [(0, LANES - MLA_ROPE)])
    return jnp.concatenate([w[..., :MLA_NOPE].reshape(lead + (2 * LANES,)), rope.reshape(lead + (2 * LANES,))], axis=-1)


def wq_from_storage(g):
    lead = g.shape[:-1]
    nope = g[..., :2 * LANES].reshape(lead + (2, LANES))
    rope = g[..., 2 * LANES:].reshape(lead + (2, LANES))[..., :MLA_ROPE]
    return jnp.concatenate([nope, rope], axis=-1).reshape(lead + (2 * MLA_QK,))


def wcat_from_storage(w_in_l):
    w = w_in_l.transpose(1, 0, 2).reshape(D_MODEL, IN_COLS)
    zc = lambda n: jnp.zeros((D_MODEL, n), w.dtype)
    return jnp.concatenate([w[:, :2560], w[:, 2576:], zc(64), w[:, 2560:2576], zc(112)], axis=1)


def w_in_grad_storage(dwcat):
    g = jnp.concatenate([dwcat[:, :2560], dwcat[:, C_DT:C_DT + 16], dwcat[:, 2560:3648]], axis=1)
    return g.reshape(D_MODEL, 4, IN_COLS // 4).transpose(1, 0, 2)


def _ssd_par(sm):
    rows = jnp.stack([sm["dt_bias"], sm["a_log"], sm["d_skip"]]).reshape(3, SSD_GROUPS, SSD_HPG).transpose(1, 0, 2)
    return jnp.pad(rows, ((0, 0), (0, 5), (0, LANES - SSD_HPG)))


def _dt_pad(proj, B, S):
    dt = proj[:, C_DT:C_DT + 16].reshape(B, S, SSD_GROUPS, SSD_HPG)
    return jnp.pad(dt, ((0, 0), (0, 0), (0, 0), (0, LANES - SSD_HPG))).reshape(B, S, SSD_GROUPS * LANES)


def _mla_gains(g):
    return g[:MLA_NOPE].reshape(1, LANES), jnp.pad(g[MLA_NOPE:], (0, LANES - MLA_ROPE)).reshape(1, LANES)


_Q_NOPE = lambda h: 4 * (h // 2) + h % 2
_Q_ROPE = lambda h: 4 * (h // 2) + 2 + h % 2
_K_NOPE = lambda h: 2 * h
_V_BLK = lambda h: 2 * h + 1


def layer_fwd(x, memf, cos, sin, G, kw, sm, B, S, l, dep=None):
    T = x.shape[0]
    n = lambda s: "%s_l%d" % (s, l)
    wv = lambda k: (KIND[k], 0)
    sv = dict(x0=x)
    h1 = norm_fwd(x, sm["attn_norm_g"], D_MODEL, False, n("attn_norm"), dep=dep)
    proj = mm(h1, kw["wcat"], "nn", n("in_proj"))
    mark = getattr(G, "mark", lambda v: None)
    mark(proj)
    sv.update(h1=h1, proj=proj)
    proj3 = proj.reshape(B, S, CAT_COLS)
    xbc = conv_fwd(proj3, C_XBC // 256, kw["conv_w"], sm["conv_b"].reshape(1, -1), n("conv"))
    dtp = _dt_pad(proj, B, S)
    par = _ssd_par(sm)
    y_raw, states = ssd_fwd(xbc, dtp, par, n("ssd"))
    z = proj[:, C_Z:C_Z + SSD_INNER]
    y_ssd = norm_fwd(y_raw.reshape(T, SSD_INNER), sm["ssd_norm_g"], SSD_INNER // SSD_GROUPS, False, n("ssd_norm"), gate=z)
    sv.update(xbc=xbc, states=states, y_raw=y_raw)
    qn = norm_fwd(proj[:, C_QA:C_QA + Q_LORA], sm["q_a_norm_g"], Q_LORA, False, n("q_a_norm"))
    kvn = norm_fwd(proj[:, C_KVA:C_KVA + KV_LORA], sm["kv_a_norm_g"], KV_LORA, False, n("kv_a_norm"))
    q_raw = mm(qn, G["w_q_b"], "nn", n("q_b"), wv=wv("w_q_b"))
    kv_raw = mm(kvn, G["w_kv_b"], "nn", n("kv_b"), wv=wv("w_kv_b"))
    gqn, gqr = _mla_gains(sm["mla_q_norm_g"])
    gkn, gkr = _mla_gains(sm["mla_k_norm_g"])
    qh = qk_fwd(q_raw, _Q_NOPE, q_raw, _Q_ROPE, False, gqn, gqr, cos, sin, n("q_norm_rope"))
    kh = qk_fwd(kv_raw, _K_NOPE, proj, C_KR // LANES, True, gkn, gkr, cos, sin, n("k_norm_rope"))
    vv = kv_raw.astype(BF16)
    y_mla = attn_fwd(qh.reshape(B, S, -1), kh.reshape(B, S, -1), vv.reshape(B, S, -1), MLA_HEADS, 2 * LANES, MLA_V,
                     MLA_QK ** -0.5, True, n("mla_attn"), v_idx=_V_BLK)
    ycat = jnp.concatenate([y_ssd, y_mla.reshape(T, -1)], axis=1)
    x1 = mm(ycat, G["w_out"], "nn", n("out_proj"), add=x, wv=wv("w_out"))
    sv.update(qn=qn, kvn=kvn, q_raw=q_raw, kv_raw=kv_raw, qh=qh, kh=kh, vv=vv, ycat=ycat, x1=x1)
    h2 = norm_fwd(x1, sm["xattn_norm_g"], D_MODEL, False, n("xattn_norm"))
    mn = norm_fwd(memf, sm["mem_norm_g"], D_MODEL, False, n("mem_norm"))
    xq = mm(h2, G["w_xq"], "nn", n("xq"), wv=wv("w_xq"))
    xk = mm(mn, G["w_xk"], "nn", n("xk"), wv=wv("w_xk"))
    xv = mm(mn, G["w_xv"], "nn", n("xv"), wv=wv("w_xv"), out_dtype=BF16)
    xqn = norm_fwd(xq, sm["xq_norm_g"], X_HEAD_DIM, True, n("xq_norm"))
    xkn = norm_fwd(xk, sm["xk_norm_g"], X_HEAD_DIM, True, n("xk_norm"))
    ML = memf.shape[0] // B
    xo = attn_fwd(xqn.reshape(B, S, -1), xkn.reshape(B, ML, -1), xv.reshape(B, ML, -1), X_HEADS, X_HEAD_DIM, X_HEAD_DIM,
                  X_HEAD_DIM ** -0.5, False, n("x_attn"))
    x2 = mm(xo.reshape(T, -1), G["w_xo"], "nn", n("xo_proj"), add=x1, wv=wv("w_xo"))
    mark(x2)
    sv.update(h2=h2, mn=mn, xq=xq, xk=xk, xqn=xqn, xkn=xkn, xv=xv, xo=xo, x2=x2)
    h3 = norm_fwd(x2, sm["ffn_norm_g"], D_MODEL, False, n("ffn_norm"))
    gate, up, act = gate_up_act(h3, G["w_gate"], G["w_up"], n("gate_up"))
    x3 = mm(act, G["w_down"], "nn", n("down"), add=x2, wv=wv("w_down"))
    sv.update(h3=h3, gate=gate, up=up, act=act)
    return x3, sv


def layer_bwd(dx3, dx3b, sv, memf, cos, sin, G, kw, sm, B, S, l, dep=None, rs_hook=None):
    T = dx3.shape[0]
    n = lambda s: "%s_bwd_l%d" % (s, l)
    wv = lambda k: (KIND[k], 0)
    dw = lambda k, a, b: mm(a, b, "tn", n(k + "_dw"), out_dtype=BF16, out_wv=KIND[k])
    gw, gs = {}, {}
    dgate, dup = down_dx_swiglu(dx3b, G["w_down"][0].reshape(FFN, D_MODEL), sv["gate"], sv["up"], n("down_dx"), dep=dep)
    gw["w_down"] = dw("w_down", sv["act"], dx3b)
    dh3 = gate_up_dx(dgate, dup, G["w_gate"], G["w_up"], n("gate_up_dx"))
    gw["w_gate"] = dw("w_gate", sv["h3"], dgate)
    gw["w_up"] = dw("w_up", sv["h3"], dup)
    dx2, gs["ffn_norm_g"], dx2b = norm_bwd(sv["x2"], sm["ffn_norm_g"], dh3, D_MODEL, False, n("ffn_norm"), add=dx3,
                                           bf16_copy=True)
    ffn_token = rs_hook({k: gw.pop(k) for k in ("w_down", "w_gate", "w_up")}) if rs_hook is not None else None
    dxo = mm(dx2b, G["w_xo"], "nt", n("xo_dx"), wv=wv("w_xo"), dep=ffn_token)
    gw["w_xo"] = dw("w_xo", sv["xo"].reshape(T, -1), dx2b)
    ML = memf.shape[0] // B
    dxqn, dxkn, dxv = attn_bwd(sv["xqn"].reshape(B, S, -1), sv["xkn"].reshape(B, ML, -1), sv["xv"].reshape(B, ML, -1),
                               dxo.reshape(B, S, -1), X_HEADS, X_HEAD_DIM, X_HEAD_DIM, X_HEAD_DIM ** -0.5, False, n("x_attn"))
    dxq, gs["xq_norm_g"] = norm_bwd(sv["xq"], sm["xq_norm_g"], dxqn.reshape(T, -1), X_HEAD_DIM, True, n("xq_norm"))
    dxk, gs["xk_norm_g"] = norm_bwd(sv["xk"], sm["xk_norm_g"], dxkn.reshape(B * ML, -1), X_HEAD_DIM, True, n("xk_norm"))
    dxv = dxv.reshape(B * ML, -1)
    dmn = mm(dxk, G["w_xk"], "nt", n("xk_dx"), wv=wv("w_xk"))
    dmn = mm(dxv, G["w_xv"], "nt", n("xv_dx"), wv=wv("w_xv"), add=dmn)
    gw["w_xk"] = dw("w_xk", sv["mn"], dxk)
    gw["w_xv"] = dw("w_xv", sv["mn"], dxv)
    _, gs["mem_norm_g"] = norm_bwd(memf, sm["mem_norm_g"], dmn, D_MODEL, False, n("mem_norm"))
    dh2 = mm(dxq, G["w_xq"], "nt", n("xq_dx"), wv=wv("w_xq"))
    gw["w_xq"] = dw("w_xq", sv["h2"], dxq)
    dx1, gs["xattn_norm_g"], dx1b = norm_bwd(sv["x1"], sm["xattn_norm_g"], dh2, D_MODEL, False, n("xattn_norm"), add=dx2,
                                             bf16_copy=True)
    dycat = mm(dx1b, G["w_out"], "nt", n("out_dx"), wv=wv("w_out"))
    gw["w_out"] = dw("w_out", sv["ycat"], dx1b)
    mid_token = rs_hook(None, after=dycat) if rs_hook is not None else None
    dqh, dkh, dvv = attn_bwd(sv["qh"].reshape(B, S, -1), sv["kh"].reshape(B, S, -1), sv["vv"].reshape(B, S, -1),
                             dycat[:, SSD_INNER:].reshape(B, S, -1), MLA_HEADS, 2 * LANES, MLA_V, MLA_QK ** -0.5, True,
                             n("mla_attn"), v_idx=_V_BLK)
    gqn, gqr = _mla_gains(sm["mla_q_norm_g"])
    gkn, gkr = _mla_gains(sm["mla_k_norm_g"])
    dq_raw, dgqn, dgqr = qk_bwd(sv["q_raw"], _Q_NOPE, sv["q_raw"], _Q_ROPE, False, gqn, gqr, cos, sin,
                                dqh.reshape(T, -1), n("q_norm_rope"))
    dkv_raw, dk_rope, dgkn, dgkr = qk_bwd(sv["kv_raw"], _K_NOPE, sv["proj"], C_KR // LANES, True, gkn, gkr, cos, sin,
                                          dkh.reshape(T, -1), n("k_norm_rope"), fill=dvv.reshape(T, -1), fill_blk=_V_BLK)
    gs["mla_q_norm_g"] = jnp.concatenate([dgqn[0], dgqr[0, :MLA_ROPE]])
    gs["mla_k_norm_g"] = jnp.concatenate([dgkn[0], dgkr[0, :MLA_ROPE]])
    dqn = mm(dq_raw, G["w_q_b"], "nt", n("q_b_dx"), wv=wv("w_q_b"), dep=mid_token)
    gw["w_q_b"] = dw("w_q_b", sv["qn"], dq_raw)
    dkvn = mm(dkv_raw, G["w_kv_b"], "nt", n("kv_b_dx"), wv=wv("w_kv_b"))
    gw["w_kv_b"] = dw("w_kv_b", sv["kvn"], dkv_raw)
    proj = sv["proj"]
    dq_a, gs["q_a_norm_g"] = norm_bwd(proj[:, C_QA:C_QA + Q_LORA], sm["q_a_norm_g"], dqn, Q_LORA, False, n("q_a_norm"))
    dkv_a, gs["kv_a_norm_g"] = norm_bwd(proj[:, C_KVA:C_KVA + KV_LORA], sm["kv_a_norm_g"], dkvn, KV_LORA, False,
                                        n("kv_a_norm"))
    z = proj[:, C_Z:C_Z + SSD_INNER]
    dy_raw, dz, gs["ssd_norm_g"] = norm_bwd(sv["y_raw"].reshape(T, -1), sm["ssd_norm_g"], dycat[:, :SSD_INNER],
                                            SSD_INNER // SSD_GROUPS, False, n("ssd_norm"), gate=z)
    dtp = _dt_pad(proj, B, S)
    par = _ssd_par(sm)
    dxs, dBm, dCm, ddt, dpar = ssd_bwd(sv["xbc"], dtp, par, sv["states"], dy_raw.reshape(B, S, -1), n("ssd"))
    gs["dt_bias"] = dpar[:, 0, :SSD_HPG].reshape(-1)
    gs["a_log"] = dpar[:, 1, :SSD_HPG].reshape(-1)
    gs["d_skip"] = dpar[:, 2, :SSD_HPG].reshape(-1)
    dxbc_act = jnp.concatenate([dxs, dBm, dCm], axis=2)
    dxbc, dconv_w, dconv_b = conv_bwd(proj.reshape(B, S, -1), C_XBC // 256, kw["conv_w"], sm["conv_b"].reshape(1, -1), dxbc_act,
                                      n("conv"))
    gs["conv_w"] = dconv_w[:SSD_CONV]
    gs["conv_b"] = dconv_b[0]
    ddt16 = ddt.reshape(T, SSD_GROUPS, LANES)[:, :, :SSD_HPG].reshape(T, 16)
    dproj = jnp.concatenate([dz, dxbc.reshape(T, -1), dq_a, dkv_a, dk_rope, ddt16, jnp.zeros((T, 112), F32)],
                            axis=1).astype(BF16)
    dh1 = mm(dproj, kw["wcat"], "nt", n("in_dx"))
    gw["w_in"] = w_in_grad_storage(mm(sv["h1"], dproj, "tn", n("in_dw"), out_dtype=BF16))
    dx0, gs["attn_norm_g"], dx0b = norm_bwd(sv["x0"], sm["attn_norm_g"], dh1, D_MODEL, False, n("attn_norm"), add=dx1,
                                            bf16_copy=True)
    gs = {k: v.reshape(-1) for k, v in gs.items()}
    return dx0, dx0b, gw, gs


def _small_pack(vals):
    flat = jnp.concatenate([vals[l][k].reshape(-1) for l in range(DEPTH) for k in SMALL_RS])
    n = flat.shape[0]
    rows = -(-n // LANES)
    rows = -(-rows // 8) * 8
    return jnp.pad(flat, (0, rows * LANES - n)).reshape(rows, LANES), n


def _small_unpack(flat, shapes):
    v = flat.reshape(-1)
    out, off = [dict() for _ in range(DEPTH)], 0
    for l in range(DEPTH):
        for k in SMALL_RS:
            sz = shapes[k]
            out[l][k] = v[off:off + sz]
            off += sz
    return out


def _adam_2d(a):
    if a.ndim == 2:
        return a
    return a.reshape(a.shape[0] * a.shape[1], a.shape[2])


def kernel(x, mem, positions, attn_norm_g, w_in, conv_w, conv_b, dt_bias, a_log, d_skip, ssd_norm_g, q_a_norm_g, w_q_b, kv_a_norm_g, w_kv_b, mla_q_norm_g, mla_k_norm_g, w_out, xattn_norm_g, mem_norm_g, w_xq, w_xk, w_xv, xq_norm_g, xk_norm_g, w_xo, ffn_norm_g, w_gate, w_up, w_down, loss_target, m_attn_norm_g, m_w_in, m_conv_w, m_conv_b, m_dt_bias, m_a_log, m_d_skip, m_ssd_norm_g, m_q_a_norm_g, m_w_q_b, m_kv_a_norm_g, m_w_kv_b, m_mla_q_norm_g, m_mla_k_norm_g, m_w_out, m_xattn_norm_g, m_mem_norm_g, m_w_xq, m_w_xk, m_w_xv, m_xq_norm_g, m_xk_norm_g, m_w_xo, m_ffn_norm_g, m_w_gate, m_w_up, m_w_down, v_attn_norm_g, v_w_in, v_conv_w, v_conv_b, v_dt_bias, v_a_log, v_d_skip, v_ssd_norm_g, v_q_a_norm_g, v_w_q_b, v_kv_a_norm_g, v_w_kv_b, v_mla_q_norm_g, v_mla_k_norm_g, v_w_out, v_xattn_norm_g, v_mem_norm_g, v_w_xq, v_w_xk, v_w_xv, v_xq_norm_g, v_xk_norm_g, v_w_xo, v_ffn_norm_g, v_w_gate, v_w_up, v_w_down):
    loc = locals()
    W = {k: loc[k] for k in WEIGHTS}
    Mo = {k: loc["m_" + k] for k in WEIGHTS}
    Vo = {k: loc["v_" + k] for k in WEIGHTS}
    B, S, D = x.shape
    T = B * S
    xf = x.reshape(T, D)
    memf = mem.reshape(-1, D)
    tf = loss_target.reshape(T, D)
    cos, sin = rope_tables(positions.reshape(T, 1).astype(F32))

    cw = gather_all_devices(conv_w.reshape(DEPTH * SSD_CONV, -1), "gather_conv_w")[0::2]
    cw = cw.transpose(1, 0, 2).reshape(DEPTH, SSD_CONV, SSD_CONV_DIM)
    sms = [{k: W[k][l] for k in SMALL} for l in range(DEPTH)]
    me = _me_chip()

    cw, w_in_first = lax.optimization_barrier((cw, W["w_in"][0]))

    def gather_start(l):
        shards = [(wq_to_storage(W[k][l]) if k == "w_q_b" else W[k][l]).astype(BF16) for k in GATHERED]
        if l == 0:
            shards[0] = w_in_first.astype(BF16)
        lands = [lax.dynamic_update_slice(lax.empty((4,) + s.shape, BF16), s[None], (me, 0, 0)) for s in shards]
        return ici_start("gather" if l == 0 else "gather_all", [], lands, "gather_start_l%d" % l)

    def gather_end(state, names, after, tag):
        idx = [GATHERED.index(k) for k in names]
        P = len(_copy_plan(state["kind"]))
        pick = lambda sems: [sems[P * w + j] for w in idx for j in range(P)]
        sub = dict(kind=state["kind"], srcs=[], lands=[state["lands"][w] for w in idx],
                   ssem=pick(state["ssem"]), rsem=pick(state["rsem"]))
        _, lands = ici_wait(sub, after, "gather_wait_" + tag)
        if state["kind"] == "gather":
            lands = forward_pair(lands, "gather_forward_" + tag)
        return {k: z[None] for k, z in zip(names, lands)}

    class Weights(dict):
        def __init__(self, state, groups, after, tag):
            super().__init__(gather_end(state, groups[0], after, tag + "a"))
            self.state, self.later, self.after, self.tag = state, list(groups[1:]), after, tag

        def mark(self, value):
            self.after = value

        def __missing__(self, k):
            names = self.later.pop(0)
            self.update(gather_end(self.state, names, self.after, self.tag + "bcdef"[len(self.later)]))
            return self[k]

    FIRST_GROUPS = (("w_in",), ("w_q_b", "w_kv_b", "w_out", "w_xq", "w_xk", "w_xv", "w_xo"), ("w_gate", "w_up", "w_down"))

    saved, Gs, kws = [], [], []
    h = xf
    state, token = gather_start(0)
    for l in range(DEPTH):
        Gl = Weights(state, FIRST_GROUPS if l == 0 else (GATHERED,), h, "l%d" % l)
        kwl = dict(wcat=wcat_from_storage(Gl["w_in"][0]), conv_w=jnp.pad(cw[l], ((0, 8 - SSD_CONV), (0, 0))))
        Gs.append(Gl)
        kws.append(kwl)
        if l + 1 < DEPTH:
            state, token = gather_start(l + 1)
        h, sv = layer_fwd(h, memf, cos, sin, Gl, kwl, sms[l], B, S, l, dep=token if l + 1 < DEPTH else None)
        saved.append(sv)
    dh, lpart, dhb = loss_head(h, tf)
    loss = lax.psum(lpart[0, 0], ("x", "y", "c"))

    grads, delta, new_m, new_v = {}, {}, {}, {}
    chain = {k: None for k in GATHERED}
    gsmall = [None] * DEPTH
    pending = [[] for _ in range(DEPTH)]

    in_pair = []

    def flush(after):
        if not in_pair:
            return None
        pair_state, l, tag = in_pair.pop()
        rs_state, tok = reduce_scatter_begin(pair_state, after, tag)
        pending[l].append((rs_state, tag))
        return tok

    def begin(gw, l, tag, after=None):
        if gw is None:
            return flush(after)
        tok_chips = flush(next(iter(gw.values())))
        pair_state, tok = reduce_scatter_pair(gw, "l%d%s" % (l, tag))
        in_pair.append((pair_state, l, "l%d%s" % (l, tag)))
        return tok if tok_chips is None else tok + tok_chips

    def finish(l, after):
        for rs_state, tag in pending[l]:
            red = reduce_scatter_end(rs_state, after, tag)
            for k, (joined, own) in red.items():
                if k == "w_q_b":
                    joined, own = wq_from_storage(fill_own_half(joined, own)), None
                chain[k] = adamw_layer(W[k], joined, Mo[k], Vo[k], l, chain[k], "adamw_%s_l%d" % (k, l), g_own=own)

    token = None
    for l in reversed(range(DEPTH)):
        dh, dhb, gw, gs = layer_bwd(dh, dhb, saved[l], memf, cos, sin, Gs[l], kws[l], sms[l], B, S, l, dep=token,
                                    rs_hook=functools.partial(begin, l=l, tag="_ffn"))
        saved[l] = None
        gsmall[l] = gs
        token = begin(gw, l, "_mix")
        if l == 0:
            token = token + flush(dh)
        if l + 1 < DEPTH:
            finish(l + 1, token)
    grad_x = dh.reshape(B, S, D)

    spack, _ = _small_pack(gsmall)
    allv = gather_all_devices(spack, "gather_small_grads")
    ssum = sum_arrays([allv[i] for i in range(8)], F32, "sum_small_grads")
    sizes = {k: int(np.prod(W[k].shape[1:])) for k in SMALL}
    sizes["conv_w"] = SSD_CONV * SSD_CONV_DIM
    gsm = _small_unpack(ssum, sizes)
    for k in SMALL_RS:
        g = jnp.stack([gsm[l][k] for l in range(DEPTH)])
        if k == "conv_w":
            cs = SSD_CONV_DIM // 4
            g = lax.dynamic_slice_in_dim(g.reshape(DEPTH, SSD_CONV, SSD_CONV_DIM), _me_chip() * cs, cs, axis=2)
        shp = W[k].shape
        grads[k] = g.reshape(shp)
        d, nm, nv = adamw(_adam_2d(W[k]), _adam_2d(grads[k]), _adam_2d(Mo[k]), _adam_2d(Vo[k]), "adamw_" + k)
        delta[k], new_m[k], new_v[k] = d.reshape(shp), nm.reshape(shp), nv.reshape(shp)

    finish(0, lax.optimization_barrier((delta["ffn_norm_g"], chain["w_down"][1], chain["w_gate"][1], chain["w_up"][1]))[0])
    for k in GATHERED:
        grads[k], delta[k], new_m[k], new_v[k] = chain[k]

    return (loss, grad_x, *[grads[k] for k in WEIGHTS], *[delta[k] for k in WEIGHTS], *[new_m[k] for k in WEIGHTS],
            *[new_v[k] for k in WEIGHTS])
```

```python
import functools
import math

import numpy as np
import jax
import jax.numpy as jnp
from jax import lax
from jax.experimental import pallas as pl
from jax.experimental.pallas import tpu as pltpu

F32 = jnp.float32
BF16 = jnp.bfloat16
MESH = pl.DeviceIdType.MESH
HI = lax.Precision.HIGHEST

D_MODEL = 2048
DEPTH = 4
SSD_INNER = 1024
SSD_HEAD_DIM = 64
SSD_GROUPS = 2
SSD_HPG = 8
SSD_STATE = 128
SSD_CONV = 4
SSD_CHUNK = 128
SSD_CONV_DIM = 1536
SSD_GW = SSD_HPG * SSD_HEAD_DIM
MLA_HEADS = 8
MLA_NOPE = 128
MLA_ROPE = 64
MLA_QK = 192
MLA_V = 128
Q_LORA = 512
KV_LORA = 512
ROPE_THETA = 10000.0
X_HEADS = 4
X_HEAD_DIM = 128
X_INNER = 512
FFN = 5632
IN_COLS = 3664
EPS = 1e-6
ADAM_LR, ADAM_B1, ADAM_B2, ADAM_EPS, ADAM_WD, ADAM_STEP = 0.001, 0.9, 0.999, 1e-08, 0.01, 10

C_Z, C_XBC, C_QA, C_KVA, C_KR, C_DT, CAT_COLS = 0, 1024, 2560, 3072, 3584, 3712, 3840

LANES = 128
VMEM_LIMIT = 56 * 1024 * 1024
MM_VMEM_BUDGET = 40 * 1024 * 1024
MM_STEP_BYTES = 1024 * 1024
NORM_BLOCK_ELEMS = 512 * 1024
ATTN_SCORE_ELEMS = 1024 * 1024

SMALL = ("attn_norm_g", "conv_b", "dt_bias", "a_log", "d_skip", "ssd_norm_g", "q_a_norm_g", "kv_a_norm_g",
         "mla_q_norm_g", "mla_k_norm_g", "xattn_norm_g", "mem_norm_g", "xq_norm_g", "xk_norm_g", "ffn_norm_g")
SMALL_RS = SMALL + ("conv_w",)
WEIGHTS = ("attn_norm_g", "w_in", "conv_w", "conv_b", "dt_bias", "a_log", "d_skip", "ssd_norm_g", "q_a_norm_g", "w_q_b",
           "kv_a_norm_g", "w_kv_b", "mla_q_norm_g", "mla_k_norm_g", "w_out", "xattn_norm_g", "mem_norm_g", "w_xq", "w_xk",
           "w_xv", "xq_norm_g", "xk_norm_g", "w_xo", "ffn_norm_g", "w_gate", "w_up", "w_down")


def _pcall(body, **kw):
    return pl.pallas_call(body, **kw)


def _params(sem):
    return pltpu.CompilerParams(dimension_semantics=sem, vmem_limit_bytes=VMEM_LIMIT)


def _tile(n, cands):
    for t in cands:
        if n % t == 0:
            return t
    return n


def _sigmoid(z):
    return 1.0 / (1.0 + jnp.exp(-z))


def _mtile(n, cap):
    if n % LANES:
        return n
    q = n // LANES
    t = LANES * max(d for d in range(1, q + 1) if q % d == 0 and LANES * d <= cap)
    return n if (t < 512 and n <= 2048) else t


def _divisor_tiles(n, cap):
    if n % LANES:
        return [n]
    q = n // LANES
    return [LANES * d for d in range(1, q + 1) if q % d == 0 and LANES * d <= cap] or [n]


def _mm_tiles(M, N, K, m_unit, n_unit, k_unit, a_item, b_item, o_item, has_add):
    best = None
    for tm in _divisor_tiles(m_unit, 2048):
        for tn in _divisor_tiles(n_unit, 2048):
            for tk in _divisor_tiles(k_unit, 2048):
                nk = K // tk
                vmem = 2 * (tm * tk * a_item + tk * tn * b_item + tm * tn * (o_item + (4 if has_add else 0)))
                vmem += tm * tn * 4 * (2 if nk > 1 else 1) + (tm * tk * 2 if a_item == 4 else 0) + (tk * tn * 2 if b_item == 4 else 0)
                if vmem > MM_VMEM_BUDGET:
                    continue
                steps = (M // tm) * (N // tn) * nk
                traffic = (M * K * a_item * (N // tn if nk > 1 else 1) + K * N * b_item * (M // tm)
                           + M * N * (o_item + (4 if has_add else 0)) + steps * MM_STEP_BYTES)
                if nk > 1:
                    traffic += steps * tm * tn * 4
                if best is None or traffic < best[0]:
                    best = (traffic, tm, tn, tk)
    assert best is not None, (M, N, K)
    return best[1:]


def mm(a, b, mode, name, add=None, out_dtype=F32, wv=None, out_wv=None, dep=None):
    kind, l = wv if wv is not None else (None, None)
    if kind == "row":
        b, kind = b[l].reshape(-1, b.shape[-1]), None
    if out_wv == "row":
        res = mm(a, b, mode, name, add=add, out_dtype=out_dtype, wv=wv if kind else None, dep=dep)
        return res.reshape(4, res.shape[0] // 4, res.shape[1])
    if kind == "col":
        _, _, r, c = b.shape
        bshape = (r, 4 * c)
    else:
        bshape = b.shape
    if mode == "nn":
        (M, K), (K2, N) = a.shape, bshape
    elif mode == "nt":
        (M, K), (N, K2) = a.shape, bshape
    else:
        (K, M), (K2, N) = a.shape, bshape
    assert K == K2, (a.shape, bshape, mode)
    m_unit = M
    n_unit = N // 4 if (out_wv == "col" or (kind == "col" and mode == "nn")) else N
    k_unit = K // 4 if (kind == "col" and mode == "nt") else K
    tm, tn, tk = _mm_tiles(M, N, K, m_unit, n_unit, k_unit, a.dtype.itemsize, b.dtype.itemsize,
                           jnp.dtype(out_dtype).itemsize, add is not None)
    nk = K // tk
    qm, qn, qk = m_unit // tm, n_unit // tn, k_unit // tk
    dn = {"nn": (((1,), (0,)), ((), ())), "nt": (((1,), (1,)), ((), ())), "tn": (((0,), (0,)), ((), ()))}[mode]
    has_add = add is not None

    def body(a_ref, b_ref, *rest):
        rest = list(rest)
        c_ref = rest.pop(0) if has_add else None
        if dep is not None:
            rest.pop(0)
        o_ref = rest.pop(0)
        d = lax.dot_general(a_ref[...].astype(BF16), b_ref[...].astype(BF16), dn, preferred_element_type=F32)

        def finish(res):
            if has_add:
                res = res + c_ref[...].astype(F32)
            o_ref[...] = res.astype(out_dtype)

        if nk == 1:
            finish(d)
            return
        acc = rest.pop(0)
        k = pl.program_id(2)

        @pl.when(k == 0)
        def _():
            acc[...] = d

        if nk > 2:
            @pl.when((k > 0) & (k < nk - 1))
            def _():
                acc[...] += d

        @pl.when(k == nk - 1)
        def _():
            finish(acc[...] + d)

    if mode == "tn":
        a_spec = pl.BlockSpec((tk, tm), lambda i, j, k: (k, i))
    else:
        a_spec = pl.BlockSpec((tm, tk), lambda i, j, k: (i, k))
    if kind is None:
        b_spec = (pl.BlockSpec((tn, tk), lambda i, j, k: (j, k)) if mode == "nt"
                  else pl.BlockSpec((tk, tn), lambda i, j, k: (k, j)))
    elif mode == "nn":
        b_spec = pl.BlockSpec((None, None, tk, tn), lambda i, j, k: (l, j // qn, k, j % qn))
    elif mode == "nt":
        b_spec = pl.BlockSpec((None, None, tn, tk), lambda i, j, k: (l, k // qk, j, k % qk))
    else:
        raise ValueError((mode, kind))
    if out_wv == "col":
        o_spec = pl.BlockSpec((None, tm, tn), lambda i, j, k: (j // qn, i, j % qn))
        out_shape = jax.ShapeDtypeStruct((4, M, N // 4), out_dtype)
    else:
        o_spec = pl.BlockSpec((tm, tn), lambda i, j, k: (i, j))
        out_shape = jax.ShapeDtypeStruct((M, N), out_dtype)
    in_specs = [a_spec, b_spec] + ([o_spec] if has_add else []) + ([pl.BlockSpec(memory_space=pl.ANY)] if dep is not None else [])
    args = (a, b) + ((add,) if has_add else ()) + ((dep,) if dep is not None else ())
    return _pcall(
        body, name=name, grid=(M // tm, N // tn, nk), in_specs=in_specs, out_specs=o_spec,
        out_shape=out_shape, scratch_shapes=[pltpu.VMEM((tm, tn), F32)] if nk > 1 else [],
        compiler_params=_params(("parallel", "parallel", "arbitrary")))(*args)


def norm_fwd(x, gain, W, shared, name, gate=None, out_dtype=BF16, dep=None):
    T, C = x.shape
    nb = C // W
    tq = _tile(T, tuple(t for t in (4096, 2048, 1024, 512, 256, 128) if t * W <= NORM_BLOCK_ELEMS))
    g2 = gain.reshape(1, -1).astype(F32)
    gated = gate is not None

    def body(*refs):
        o_ref = refs[-1]
        if gated:
            x_ref, z_ref, g_ref = refs[:3]
        else:
            x_ref, g_ref = refs[:2]
        u = x_ref[...].astype(F32)
        if gated:
            z = z_ref[...]
            u = u * (z * _sigmoid(z))
        ms = jnp.mean(u * u, axis=-1, keepdims=True)
        o_ref[...] = (u * lax.rsqrt(ms + EPS) * g_ref[...]).astype(out_dtype)

    xs = pl.BlockSpec((tq, W), lambda j, i: (i, j))
    gs = pl.BlockSpec((1, W), (lambda j, i: (0, 0)) if shared else (lambda j, i: (0, j)))
    in_specs = [xs] + ([xs] if gated else []) + [gs] + ([pl.BlockSpec(memory_space=pl.ANY)] if dep is not None else [])
    args = (x,) + ((gate,) if gated else ()) + (g2,) + ((dep,) if dep is not None else ())
    return _pcall(body, name=name, grid=(nb, T // tq), in_specs=in_specs, out_specs=xs,
                  out_shape=jax.ShapeDtypeStruct((T, C), out_dtype),
                  compiler_params=_params(("parallel", "parallel")))(*args)


def norm_bwd(x, gain, dy, W, shared, name, gate=None, add=None, bf16_copy=False):
    T, C = x.shape
    nb = C // W
    tq = _tile(T, tuple(t for t in (4096, 2048, 1024, 512, 256, 128) if t * W <= NORM_BLOCK_ELEMS))
    g2 = gain.reshape(1, -1).astype(F32)
    gated = gate is not None
    has_add = add is not None
    GC = W if shared else C

    def body(*refs):
        refs = list(refs)
        x_ref = refs.pop(0)
        z_ref = refs.pop(0) if gated else None
        g_ref = refs.pop(0)
        dy_ref = refs.pop(0)
        a_ref = refs.pop(0) if has_add else None
        dx_ref = refs.pop(0)
        dz_ref = refs.pop(0) if gated else None
        dg_ref = refs.pop(0)
        j, i = pl.program_id(0), pl.program_id(1)
        first = (i == 0) & (j == 0) if shared else (i == 0)

        @pl.when(first)
        def _():
            dg_ref[...] = jnp.zeros_like(dg_ref)

        xv = x_ref[...].astype(F32)
        if gated:
            z = z_ref[...]
            s = _sigmoid(z)
            u = xv * (z * s)
        else:
            u = xv
        rstd = lax.rsqrt(jnp.mean(u * u, axis=-1, keepdims=True) + EPS)
        uh = u * rstd
        dyv = dy_ref[...].astype(F32)
        dg_ref[...] += jnp.sum(dyv * uh, axis=0, keepdims=True)
        dyg = dyv * g_ref[...]
        du = rstd * (dyg - uh * jnp.mean(dyg * uh, axis=-1, keepdims=True))
        if gated:
            dx = du * (z * s)
            dz_ref[...] = du * xv * (s * (1.0 + z * (1.0 - s)))
        else:
            dx = du
        if has_add:
            dx = dx + a_ref[...]
        dx_ref[...] = dx
        if bf16_copy:
            refs.pop(0)[...] = dx.astype(BF16)

    xs = pl.BlockSpec((tq, W), lambda j, i: (i, j))
    gs = pl.BlockSpec((1, W), (lambda j, i: (0, 0)) if shared else (lambda j, i: (0, j)))
    in_specs = [xs] + ([xs] if gated else []) + [gs, xs] + ([xs] if has_add else [])
    args = (x,) + ((gate,) if gated else ()) + (g2, dy) + ((add,) if has_add else ())
    out_specs = [xs] + ([xs] if gated else []) + [gs] + ([xs] if bf16_copy else [])
    out_shape = [jax.ShapeDtypeStruct((T, C), F32)] + ([jax.ShapeDtypeStruct((T, C), F32)] if gated else []) \
        + [jax.ShapeDtypeStruct((1, GC), F32)] + ([jax.ShapeDtypeStruct((T, C), BF16)] if bf16_copy else [])
    return _pcall(body, name=name, grid=(nb, T // tq), in_specs=in_specs, out_specs=out_specs, out_shape=out_shape,
                  compiler_params=_params(("arbitrary", "arbitrary")))(*args)


def _inv_freq_row():
    inv = 1.0 / (ROPE_THETA ** (np.arange(0, MLA_ROPE, 2, dtype=np.float32) / MLA_ROPE))
    row = np.zeros((1, LANES), np.float32)
    row[0, :32] = inv
    row[0, 32:64] = inv
    return jnp.asarray(row)


def rope_tables(pos):
    T = pos.shape[0]
    tq = _tile(T, (512, 256, 128))

    def body(p_ref, f_ref, c_ref, s_ref):
        ang = p_ref[...] * f_ref[...]
        lane = lax.broadcasted_iota(jnp.int32, ang.shape, 1)
        c_ref[...] = jnp.where(lane < 64, jnp.cos(ang), 0.0)
        sn = jnp.sin(ang)
        s_ref[...] = jnp.where(lane < 32, -sn, jnp.where(lane < 64, sn, 0.0))

    ts = pl.BlockSpec((tq, LANES), lambda i: (i, 0))
    return _pcall(body, name="rope_tables", grid=(T // tq,),
                  in_specs=[pl.BlockSpec((tq, 1), lambda i: (i, 0)), pl.BlockSpec((1, LANES), lambda i: (0, 0))],
                  out_specs=[ts, ts], out_shape=[jax.ShapeDtypeStruct((T, LANES), F32)] * 2,
                  compiler_params=_params(("parallel",)))(pos, _inv_freq_row())


def _swap_halves(r):
    lane = lax.broadcasted_iota(jnp.int32, r.shape, 1)
    return jnp.where(lane < 32, pltpu.roll(r, 96, 1), pltpu.roll(r, 32, 1))


def qk_fwd(nope_arr, nope_blk0, rope_arr, rope_blk0, rope_shared, gn, gr, cos, sin, name):
    T = nope_arr.shape[0]
    tq = _tile(T, (256, 128))

    col = lambda blk: slice(blk * LANES, (blk + 1) * LANES)

    def body(n_ref, r_ref, gn_ref, gr_ref, c_ref, s_ref, o_ref):
        for h in range(MLA_HEADS):
            n = n_ref[:, col(nope_blk0(h))]
            r = r_ref[...] if rope_shared else n_ref[:, col(rope_blk0(h))]
            ms = (jnp.sum(n * n, axis=-1, keepdims=True) + jnp.sum(r * r, axis=-1, keepdims=True)) * (1.0 / MLA_QK)
            rstd = lax.rsqrt(ms + EPS)
            rn = r * rstd * gr_ref[...]
            rr = rn * c_ref[...] + _swap_halves(rn) * s_ref[...]
            o_ref[:, col(2 * h)] = (n * rstd * gn_ref[...]).astype(BF16)
            o_ref[:, col(2 * h + 1)] = rr.astype(BF16)

    wide = lambda arr: pl.BlockSpec((tq, arr.shape[1]), lambda i: (i, 0))
    assert rope_shared or rope_arr is nope_arr
    rspec = pl.BlockSpec((tq, LANES), lambda i: (i, rope_blk0)) if rope_shared else pl.BlockSpec((8, LANES), lambda i: (0, 0))
    row = pl.BlockSpec((1, LANES), lambda i: (0, 0))
    tab = pl.BlockSpec((tq, LANES), lambda i: (i, 0))
    return _pcall(body, name=name, grid=(T // tq,),
                  in_specs=[wide(nope_arr), rspec, row, row, tab, tab],
                  out_specs=pl.BlockSpec((tq, MLA_HEADS * 2 * LANES), lambda i: (i, 0)),
                  out_shape=jax.ShapeDtypeStruct((T, MLA_HEADS * 2 * LANES), BF16),
                  compiler_params=_params(("parallel",)))(nope_arr, rope_arr, gn, gr, cos, sin)


def qk_bwd(nope_arr, nope_blk0, rope_arr, rope_blk0, rope_shared, gn, gr, cos, sin, dout, name, fill=None, fill_blk=None):
    T = nope_arr.shape[0]
    tq = _tile(T, (256, 128))
    col = lambda blk: slice(blk * LANES, (blk + 1) * LANES)

    def body(*refs):
        refs = list(refs)
        n_ref, r_ref, gn_ref, gr_ref, c_ref, s_ref, d_ref = refs[:7]
        f_ref = refs[7] if rope_shared else None
        outs = refs[8:] if rope_shared else refs[7:]
        draw_ref = outs[0]
        dsh_ref = outs[1] if rope_shared else None
        dgn_ref, dgr_ref = outs[-2:]

        @pl.when(pl.program_id(0) == 0)
        def _():
            dgn_ref[...] = jnp.zeros_like(dgn_ref)
            dgr_ref[...] = jnp.zeros_like(dgr_ref)

        dgn = jnp.zeros((1, LANES), F32)
        dgr = jnp.zeros((1, LANES), F32)
        dshared = jnp.zeros((tq, LANES), F32)
        for h in range(MLA_HEADS):
            n = n_ref[:, col(nope_blk0(h))]
            r = r_ref[...] if rope_shared else n_ref[:, col(rope_blk0(h))]
            ms = (jnp.sum(n * n, axis=-1, keepdims=True) + jnp.sum(r * r, axis=-1, keepdims=True)) * (1.0 / MLA_QK)
            rstd = lax.rsqrt(ms + EPS)
            nh = n * rstd
            rh = r * rstd
            d_n = d_ref[:, col(2 * h)].astype(F32)
            d_rr = d_ref[:, col(2 * h + 1)].astype(F32)
            d_rn = d_rr * c_ref[...] - _swap_halves(d_rr) * s_ref[...]
            dgn = dgn + jnp.sum(d_n * nh, axis=0, keepdims=True)
            dgr = dgr + jnp.sum(d_rn * rh, axis=0, keepdims=True)
            dng = d_n * gn_ref[...]
            drg = d_rn * gr_ref[...]
            mean = (jnp.sum(dng * nh, axis=-1, keepdims=True) + jnp.sum(drg * rh, axis=-1, keepdims=True)) * (1.0 / MLA_QK)
            draw_ref[:, col(nope_blk0(h))] = (rstd * (dng - nh * mean)).astype(BF16)
            drope = rstd * (drg - rh * mean)
            if rope_shared:
                dshared = dshared + drope
                draw_ref[:, col(fill_blk(h))] = f_ref[:, col(h)].astype(BF16)
            else:
                draw_ref[:, col(rope_blk0(h))] = drope.astype(BF16)
        dgn_ref[...] += dgn
        dgr_ref[...] += dgr
        if rope_shared:
            dsh_ref[...] = dshared

    wide = lambda arr: pl.BlockSpec((tq, arr.shape[1]), lambda i: (i, 0))
    assert rope_shared or rope_arr is nope_arr
    rspec = pl.BlockSpec((tq, LANES), lambda i: (i, rope_blk0)) if rope_shared else pl.BlockSpec((8, LANES), lambda i: (0, 0))
    row = pl.BlockSpec((1, LANES), lambda i: (0, 0))
    tab = pl.BlockSpec((tq, LANES), lambda i: (i, 0))
    raw = pl.BlockSpec((tq, 2 * MLA_HEADS * LANES), lambda i: (i, 0))
    in_specs = [wide(nope_arr), rspec, row, row, tab, tab, wide(dout)] + ([wide(fill)] if rope_shared else [])
    out_specs = [raw] + ([tab] if rope_shared else []) + [row, row]
    out_shape = ([jax.ShapeDtypeStruct((T, 2 * MLA_HEADS * LANES), BF16)]
                 + ([jax.ShapeDtypeStruct((T, LANES), F32)] if rope_shared else [])
                 + [jax.ShapeDtypeStruct((1, LANES), F32)] * 2)
    args = (nope_arr, rope_arr, gn, gr, cos, sin, dout) + ((fill,) if rope_shared else ())
    return _pcall(body, name=name, grid=(T // tq,), in_specs=in_specs, out_specs=out_specs, out_shape=out_shape,
                  compiler_params=_params(("arbitrary",)))(*args)


def _scores(q, k, scale, causal, row0):
    s = lax.dot_general(q, k, (((1,), (1,)), ((), ())), preferred_element_type=F32) * scale
    if causal:
        qi = row0 + lax.broadcasted_iota(jnp.int32, s.shape, 0)
        ki = lax.broadcasted_iota(jnp.int32, s.shape, 1)
        s = jnp.where(qi >= ki, s, -1e30)
    return s


def attn_fwd(q, k, v, H, dqk, dv, scale, causal, name, v_idx=lambda h: h):
    B, S, _ = q.shape
    Sk = k.shape[1]
    tq = _tile(S, tuple(t for t in (2048, 1024, 512, 256, 128) if t * Sk <= ATTN_SCORE_ELEMS))

    def tile(q_ref, k_ref, v_ref, o_ref, row0, kl):
        s = _scores(q_ref[0], k_ref[0, :kl], scale, causal, row0)
        p = jnp.exp(s - jnp.max(s, axis=-1, keepdims=True))
        l = jnp.sum(p, axis=-1, keepdims=True)
        o = lax.dot_general(p.astype(BF16), v_ref[0, :kl], (((1,), (0,)), ((), ())), preferred_element_type=F32)
        o_ref[0] = (o / l).astype(BF16)

    def body(q_ref, k_ref, v_ref, o_ref):
        if not causal:
            tile(q_ref, k_ref, v_ref, o_ref, 0, Sk)
            return
        for qi in range(S // tq):
            @pl.when(pl.program_id(2) == qi)
            def _(qi=qi):
                tile(q_ref, k_ref, v_ref, o_ref, qi * tq, (qi + 1) * tq)

    return _pcall(body, name=name, grid=(B, H, S // tq),
                  in_specs=[pl.BlockSpec((1, tq, dqk), lambda b, h, i: (b, i, h)),
                            pl.BlockSpec((1, Sk, dqk), lambda b, h, i: (b, 0, h)),
                            pl.BlockSpec((1, Sk, dv), lambda b, h, i: (b, 0, v_idx(h)))],
                  out_specs=pl.BlockSpec((1, tq, dv), lambda b, h, i: (b, i, h)),
                  out_shape=jax.ShapeDtypeStruct((B, S, H * dv), BF16),
                  compiler_params=_params(("parallel", "parallel", "parallel")))(q, k, v)


def attn_bwd(q, k, v, do, H, dqk, dv, scale, causal, name, v_idx=lambda h: h):
    B, S, _ = q.shape
    Sk = k.shape[1]
    tq = _tile(S, tuple(t for t in (2048, 1024, 512, 256, 128) if t * Sk <= ATTN_SCORE_ELEMS))

    def body(q_ref, k_ref, v_ref, do_ref, dq_ref, dk_ref, dv_ref):
        i = pl.program_id(2)

        @pl.when(i == 0)
        def _():
            dk_ref[...] = jnp.zeros_like(dk_ref)
            dv_ref[...] = jnp.zeros_like(dv_ref)

        def tile(row0, kl):
            qv, kv, vv = q_ref[0], k_ref[0, :kl], v_ref[0, :kl]
            s = _scores(qv, kv, scale, causal, row0)
            p = jnp.exp(s - jnp.max(s, axis=-1, keepdims=True))
            p = p / jnp.sum(p, axis=-1, keepdims=True)
            dob = do_ref[0].astype(BF16)
            pb = p.astype(BF16)
            dv_ref[0, :kl] += lax.dot_general(pb, dob, (((0,), (0,)), ((), ())), preferred_element_type=F32)
            dp = lax.dot_general(dob, vv, (((1,), (1,)), ((), ())), preferred_element_type=F32)
            delta = jnp.sum(p * dp, axis=-1, keepdims=True)
            ds = (p * (dp - delta) * scale).astype(BF16)
            dq_ref[0] = lax.dot_general(ds, kv, (((1,), (0,)), ((), ())), preferred_element_type=F32)
            dk_ref[0, :kl] += lax.dot_general(ds, qv, (((0,), (0,)), ((), ())), preferred_element_type=F32)

        if not causal:
            tile(0, Sk)
            return
        for qi in range(S // tq):
            @pl.when(i == qi)
            def _(qi=qi):
                tile(qi * tq, (qi + 1) * tq)

    qs = pl.BlockSpec((1, tq, dqk), lambda b, h, i: (b, i, h))
    ks = pl.BlockSpec((1, Sk, dqk), lambda b, h, i: (b, 0, h))
    vs = pl.BlockSpec((1, Sk, dv), lambda b, h, i: (b, 0, h))
    vin = pl.BlockSpec((1, Sk, dv), lambda b, h, i: (b, 0, v_idx(h)))
    return _pcall(body, name=name, grid=(B, H, S // tq),
                  in_specs=[qs, ks, vin, pl.BlockSpec((1, tq, dv), lambda b, h, i: (b, i, h))],
                  out_specs=[qs, ks, vs],
                  out_shape=[jax.ShapeDtypeStruct((B, S, H * dqk), F32), jax.ShapeDtypeStruct((B, Sk, H * dqk), F32),
                             jax.ShapeDtypeStruct((B, Sk, H * dv), F32)],
                  compiler_params=_params(("parallel", "parallel", "arbitrary")))(q, k, v, do)


def _shift_down(u, j):
    if j == 0:
        return u
    row = lax.broadcasted_iota(jnp.int32, u.shape, 0)
    return jnp.where(row >= j, pltpu.roll(u, j, 0), 0.0)


def _shift_up(u, j):
    if j == 0:
        return u
    n = u.shape[0]
    row = lax.broadcasted_iota(jnp.int32, u.shape, 0)
    return jnp.where(row < n - j, pltpu.roll(u, n - j, 0), 0.0)


def conv_fwd(u, col0_blk, w8, b, name):
    B, S, _ = u.shape
    tc = 256
    nb = SSD_CONV_DIM // tc

    def body(u_ref, w_ref, b_ref, o_ref):
        uv = u_ref[0]
        pre = b_ref[...] + jnp.zeros_like(uv)
        for kk in range(SSD_CONV):
            pre = pre + w_ref[kk:kk + 1, :] * _shift_down(uv, SSD_CONV - 1 - kk)
        o_ref[0] = pre * _sigmoid(pre)

    return _pcall(body, name=name, grid=(B, nb),
                  in_specs=[pl.BlockSpec((1, S, tc), lambda bb, j: (bb, 0, col0_blk + j)),
                            pl.BlockSpec((8, tc), lambda bb, j: (0, j)), pl.BlockSpec((1, tc), lambda bb, j: (0, j))],
                  out_specs=pl.BlockSpec((1, S, tc), lambda bb, j: (bb, 0, j)),
                  out_shape=jax.ShapeDtypeStruct((B, S, SSD_CONV_DIM), F32),
                  compiler_params=_params(("parallel", "parallel")))(u, w8, b)


def conv_bwd(u, col0_blk, w8, b, dy, name):
    B, S, _ = u.shape
    tc = 256
    nb = SSD_CONV_DIM // tc

    def body(u_ref, w_ref, b_ref, dy_ref, du_ref, dw_ref, db_ref):
        bb = pl.program_id(1)

        @pl.when(bb == 0)
        def _():
            dw_ref[...] = jnp.zeros_like(dw_ref)
            db_ref[...] = jnp.zeros_like(db_ref)

        uv = u_ref[0]
        pre = b_ref[...] + jnp.zeros_like(uv)
        for kk in range(SSD_CONV):
            pre = pre + w_ref[kk:kk + 1, :] * _shift_down(uv, SSD_CONV - 1 - kk)
        s = _sigmoid(pre)
        dpre = dy_ref[0] * (s * (1.0 + pre * (1.0 - s)))
        du = jnp.zeros_like(uv)
        for kk in range(SSD_CONV):
            j = SSD_CONV - 1 - kk
            du = du + w_ref[kk:kk + 1, :] * _shift_up(dpre, j)
            dw_ref[kk:kk + 1, :] += jnp.sum(dpre * _shift_down(uv, j), axis=0, keepdims=True)
        db_ref[...] += jnp.sum(dpre, axis=0, keepdims=True)
        du_ref[0] = du

    return _pcall(body, name=name, grid=(nb, B),
                  in_specs=[pl.BlockSpec((1, S, tc), lambda j, bb: (bb, 0, col0_blk + j)),
                            pl.BlockSpec((8, tc), lambda j, bb: (0, j)), pl.BlockSpec((1, tc), lambda j, bb: (0, j)),
                            pl.BlockSpec((1, S, tc), lambda j, bb: (bb, 0, j))],
                  out_specs=[pl.BlockSpec((1, S, tc), lambda j, bb: (bb, 0, j)),
                             pl.BlockSpec((8, tc), lambda j, bb: (0, j)), pl.BlockSpec((1, tc), lambda j, bb: (0, j))],
                  out_shape=[jax.ShapeDtypeStruct((B, S, SSD_CONV_DIM), F32), jax.ShapeDtypeStruct((8, SSD_CONV_DIM), F32),
                             jax.ShapeDtypeStruct((1, SSD_CONV_DIM), F32)],
                  compiler_params=_params(("arbitrary", "arbitrary")))(u, w8, b, dy)


def _expand_mat():
    e = np.zeros((LANES, SSD_GW), np.float32)
    for h in range(SSD_HPG):
        e[h, h * SSD_HEAD_DIM:(h + 1) * SSD_HEAD_DIM] = 1.0
    return jnp.asarray(e)


def _tri_mat():
    return jnp.asarray(np.tril(np.ones((SSD_CHUNK, SSD_CHUNK), np.float32)))


def _split3(x):
    hi = x.astype(BF16)
    r = x - hi.astype(F32)
    mid = r.astype(BF16)
    return hi, mid, (r - mid.astype(F32)).astype(BF16)


def _dotf(a, b, dn=(((1,), (0,)), ((), ())), sel="b"):
    if sel == "b":
        s = b.astype(BF16)
        parts = [lax.dot_general(p, s, dn, preferred_element_type=F32) for p in _split3(a)]
    else:
        s = a.astype(BF16)
        parts = [lax.dot_general(s, p, dn, preferred_element_type=F32) for p in _split3(b)]
    return parts[0] + parts[1] + parts[2]


def _dotb(a, b, dn=(((1,), (0,)), ((), ()))):
    return lax.dot_general(a.astype(BF16), b.astype(BF16), dn, preferred_element_type=F32)


NT = (((1,), (1,)), ((), ()))
TN = (((0,), (0,)), ((), ()))


def _ssd_common(dt_ref, par_ref, tri_ref, e_ref):
    dtr = dt_ref[0]
    xb = dtr + par_ref[0:1, :]
    dt = jnp.maximum(xb, 0.0) + jnp.log(1.0 + jnp.exp(-jnp.abs(xb)))
    A = -jnp.exp(par_ref[1:2, :])
    a = dt * A
    acs = _dotf(tri_ref[...], a, sel="a")
    alast = acs[SSD_CHUNK - 1:SSD_CHUNK, :]
    E = e_ref[...]
    acsE = _dotf(acs, E)
    alastE = acsE[SSD_CHUNK - 1:SSD_CHUNK, :]
    return dict(xb=xb, dt=dt, A=A, acs=acs, acsT=acs.T, ealast=jnp.exp(alast), w=jnp.exp(alast - acs),
                dtE=_dotf(dt, E), eacsE=jnp.exp(acsE), wE=jnp.exp(alastE - acsE), ealastE=jnp.exp(alastE),
                DE=_dotf(par_ref[...], E)[2:3, :])


def _decay(c, e):
    diff = c["acs"][:, e:e + 1] - c["acsT"][e:e + 1, :]
    li = lax.broadcasted_iota(jnp.int32, diff.shape, 0)
    si = lax.broadcasted_iota(jnp.int32, diff.shape, 1)
    return jnp.where(li >= si, jnp.exp(jnp.minimum(diff, 0.0)), 0.0)


def ssd_fwd(xbc, dtp, par, name):
    B, S, _ = xbc.shape
    nc = S // SSD_CHUNK
    L, N, GW, P = SSD_CHUNK, SSD_STATE, SSD_GW, SSD_HEAD_DIM

    def body(x_ref, b_ref, c_ref, dt_ref, par_ref, tri_ref, e_ref, y_ref, st_ref, S_scr):
        @pl.when(pl.program_id(2) == 0)
        def _():
            S_scr[...] = jnp.zeros_like(S_scr)

        Sin = S_scr[...]
        st_ref[0, 0, 0] = Sin
        c = _ssd_common(dt_ref, par_ref.at[0], tri_ref, e_ref)
        X = x_ref[0]
        Bm = b_ref[0]
        Cm = c_ref[0]
        xdt = X * c["dtE"]
        CB = _dotb(Cm, Bm, NT)
        y_ref[0] = c["eacsE"] * _dotb(Cm, Sin) + c["DE"] * X
        for e in range(SSD_HPG):
            M = CB * _decay(c, e)
            y_ref[0, :, e * P:(e + 1) * P] += _dotb(M, xdt[:, e * P:(e + 1) * P])
        S_scr[...] = c["ealastE"] * Sin + _dotb(Bm, xdt * c["wE"], TN)

    return _pcall(
        body, name=name, grid=(SSD_GROUPS, B, nc),
        in_specs=[pl.BlockSpec((1, L, GW), lambda g, b, ci: (b, ci, g)),
                  pl.BlockSpec((1, L, N), lambda g, b, ci: (b, ci, 8 + g)),
                  pl.BlockSpec((1, L, N), lambda g, b, ci: (b, ci, 10 + g)),
                  pl.BlockSpec((1, L, LANES), lambda g, b, ci: (b, ci, g)),
                  pl.BlockSpec((1, 8, LANES), lambda g, b, ci: (g, 0, 0)),
                  pl.BlockSpec((L, L), lambda g, b, ci: (0, 0)),
                  pl.BlockSpec((LANES, GW), lambda g, b, ci: (0, 0))],
        out_specs=[pl.BlockSpec((1, L, GW), lambda g, b, ci: (b, ci, g)),
                   pl.BlockSpec((1, 1, 1, N, GW), lambda g, b, ci: (g, b, ci, 0, 0))],
        out_shape=[jax.ShapeDtypeStruct((B, S, SSD_INNER), F32), jax.ShapeDtypeStruct((SSD_GROUPS, B, nc, N, GW), F32)],
        scratch_shapes=[pltpu.VMEM((N, GW), F32)],
        compiler_params=_params(("arbitrary", "arbitrary", "arbitrary")))(xbc, xbc, xbc, dtp, par, _tri_mat(), _expand_mat())


def ssd_bwd(xbc, dtp, par, states, dy, name):
    B, S, _ = xbc.shape
    nc = S // SSD_CHUNK
    L, N, GW, P = SSD_CHUNK, SSD_STATE, SSD_GW, SSD_HEAD_DIM

    def body(x_ref, b_ref, c_ref, dt_ref, par_ref, tri_ref, e_ref, st_ref, dy_ref,
             dx_ref, db_ref, dc_ref, ddt_ref, dpar_ref, dS_scr, dxdt_scr):
        @pl.when(pl.program_id(2) == 0)
        def _():
            dS_scr[...] = jnp.zeros_like(dS_scr)

        @pl.when((pl.program_id(1) == 0) & (pl.program_id(2) == 0))
        def _():
            dpar_ref[...] = jnp.zeros_like(dpar_ref)

        c = _ssd_common(dt_ref, par_ref.at[0], tri_ref, e_ref)
        E = e_ref[...]
        red = lambda t: _dotf(t, E, NT)
        red_row = lambda t: red(jnp.broadcast_to(t, (8, GW)))[0:1, :]
        X, Bm, Cm = x_ref[0], b_ref[0], c_ref[0]
        Sin = st_ref[0, 0, 0]
        dY = dy_ref[0]
        dSo = dS_scr[...]
        xdt = X * c["dtE"]
        CB = _dotb(Cm, Bm, NT)
        dD = red_row(jnp.sum(dY * X, axis=0, keepdims=True))
        yoff = c["eacsE"] * _dotb(Cm, Sin)
        dacs = red(dY * yoff)
        dYe = dY * c["eacsE"]
        dC = _dotb(dYe, Sin, NT)
        dSin = _dotb(Cm, dYe, TN) + c["ealastE"] * dSo
        dB = _dotb(xdt * c["wE"], dSo, NT)
        dxw = _dotb(Bm, dSo)
        t = red(dxw * xdt) * c["w"]
        dacs = dacs - t
        dalast = jnp.sum(t, axis=0, keepdims=True) + c["ealast"] * red_row(jnp.sum(Sin * dSo, axis=0, keepdims=True))
        dxdt_scr[...] = dxw * c["wE"]
        dCB = jnp.zeros((L, L), F32)
        dacsT = jnp.zeros((LANES, L), F32)
        lane = lax.broadcasted_iota(jnp.int32, (1, LANES), 1)
        subl = lax.broadcasted_iota(jnp.int32, (LANES, 1), 0)
        for e in range(SSD_HPG):
            Lm = _decay(c, e)
            M = CB * Lm
            dYh = dY[:, e * P:(e + 1) * P]
            dM = _dotb(dYh, xdt[:, e * P:(e + 1) * P], NT)
            dxdt_scr[:, e * P:(e + 1) * P] += _dotb(M, dYh, TN)
            dCBe = dM * Lm
            dCB = dCB + dCBe
            Gm = dCBe * CB
            dacs = dacs + jnp.sum(Gm, axis=1, keepdims=True) * (lane == e).astype(F32)
            dacsT = dacsT - jnp.sum(Gm, axis=0, keepdims=True) * (subl == e).astype(F32)
        dC = dC + _dotb(dCB, Bm)
        dB = dB + _dotb(dCB, Cm, TN)
        dacs = dacs + dacsT.T
        row = lax.broadcasted_iota(jnp.int32, (L, 1), 0)
        dacs = dacs + jnp.where(row == L - 1, dalast, 0.0)
        da = _dotf(tri_ref[...], dacs, TN, sel="a")
        dxdt = dxdt_scr[...]
        ddt = da * c["A"] + red(dxdt * X)
        dA = jnp.sum(da * c["dt"], axis=0, keepdims=True)
        ddtr = ddt * _sigmoid(c["xb"])
        dx_ref[0] = c["DE"] * dY + dxdt * c["dtE"]
        db_ref[0] = dB
        dc_ref[0] = dC
        ddt_ref[0] = ddtr
        dpar_ref[0, 0:1, :] += jnp.sum(ddtr, axis=0, keepdims=True)
        dpar_ref[0, 1:2, :] += dA * c["A"]
        dpar_ref[0, 2:3, :] += dD
        dS_scr[...] = dSin

    rc = lambda ci: nc - 1 - ci
    xspec = pl.BlockSpec((1, L, GW), lambda g, b, ci: (b, rc(ci), g))
    return _pcall(
        body, name=name, grid=(SSD_GROUPS, B, nc),
        in_specs=[xspec,
                  pl.BlockSpec((1, L, N), lambda g, b, ci: (b, rc(ci), 8 + g)),
                  pl.BlockSpec((1, L, N), lambda g, b, ci: (b, rc(ci), 10 + g)),
                  pl.BlockSpec((1, L, LANES), lambda g, b, ci: (b, rc(ci), g)),
                  pl.BlockSpec((1, 8, LANES), lambda g, b, ci: (g, 0, 0)),
                  pl.BlockSpec((L, L), lambda g, b, ci: (0, 0)),
                  pl.BlockSpec((LANES, GW), lambda g, b, ci: (0, 0)),
                  pl.BlockSpec((1, 1, 1, N, GW), lambda g, b, ci: (g, b, rc(ci), 0, 0)),
                  xspec],
        out_specs=[xspec,
                   pl.BlockSpec((1, L, N), lambda g, b, ci: (b, rc(ci), g)),
                   pl.BlockSpec((1, L, N), lambda g, b, ci: (b, rc(ci), g)),
                   pl.BlockSpec((1, L, LANES), lambda g, b, ci: (b, rc(ci), g)),
                   pl.BlockSpec((1, 8, LANES), lambda g, b, ci: (g, 0, 0))],
        out_shape=[jax.ShapeDtypeStruct((B, S, SSD_INNER), F32), jax.ShapeDtypeStruct((B, S, SSD_GROUPS * N), F32),
                   jax.ShapeDtypeStruct((B, S, SSD_GROUPS * N), F32), jax.ShapeDtypeStruct((B, S, SSD_GROUPS * LANES), F32),
                   jax.ShapeDtypeStruct((SSD_GROUPS, 8, LANES), F32)],
        scratch_shapes=[pltpu.VMEM((N, GW), F32), pltpu.VMEM((L, GW), F32)],
        compiler_params=_params(("arbitrary", "arbitrary", "arbitrary")))(
            xbc, xbc, xbc, dtp, par, _tri_mat(), _expand_mat(), states, dy)


def gate_up_act(h, wg, wu, name):
    T, D = h.shape
    c = wg.shape[-1]
    tm = _tile(T, (512, 256, 128))

    def body(h_ref, g_w, u_w, g_ref, u_ref, a_ref):
        hv = h_ref[...]
        g = jnp.dot(hv, g_w[...], preferred_element_type=F32)
        u = jnp.dot(hv, u_w[...], preferred_element_type=F32)
        g_ref[...] = g.astype(BF16)
        u_ref[...] = u.astype(BF16)
        a_ref[...] = (g * _sigmoid(g) * u).astype(BF16)

    ws = pl.BlockSpec((None, None, D, c), lambda i, j: (0, j, 0, 0))
    os_ = pl.BlockSpec((tm, c), lambda i, j: (i, j))
    return _pcall(body, name=name, grid=(T // tm, 4), in_specs=[pl.BlockSpec((tm, D), lambda i, j: (i, 0)), ws, ws],
                  out_specs=[os_, os_, os_], out_shape=[jax.ShapeDtypeStruct((T, 4 * c), BF16)] * 3,
                  compiler_params=_params(("parallel", "parallel")))(h, wg, wu)


def gate_up_dx(dg, du, wg, wu, name):
    T = dg.shape[0]
    _, _, D, c = wg.shape
    tm, tn = _tile(T, (1024, 512, 256, 128)), _tile(D, (1024, 512, 256, 128))

    def body(g_ref, u_ref, gw_ref, uw_ref, o_ref, acc):
        k = pl.program_id(2)
        d = (lax.dot_general(g_ref[...], gw_ref[...], NT, preferred_element_type=F32)
             + lax.dot_general(u_ref[...], uw_ref[...], NT, preferred_element_type=F32))

        @pl.when(k == 0)
        def _():
            acc[...] = d

        @pl.when((k > 0) & (k < 3))
        def _():
            acc[...] += d

        @pl.when(k == 3)
        def _():
            o_ref[...] = acc[...] + d

    a_spec = pl.BlockSpec((tm, c), lambda i, j, k: (i, k))
    w_spec = pl.BlockSpec((None, None, tn, c), lambda i, j, k: (0, k, j, 0))
    return _pcall(body, name=name, grid=(T // tm, D // tn, 4), in_specs=[a_spec, a_spec, w_spec, w_spec],
                  out_specs=pl.BlockSpec((tm, tn), lambda i, j, k: (i, j)), out_shape=jax.ShapeDtypeStruct((T, D), F32),
                  scratch_shapes=[pltpu.VMEM((tm, tn), F32)],
                  compiler_params=_params(("parallel", "parallel", "arbitrary")))(dg, du, wg, wu)


def down_dx_swiglu(dy, w_down, gate, up, name, dep=None):
    T, D = dy.shape
    Fh = w_down.shape[0]
    tm, tn = _tile(T, (1024, 512, 256, 128)), 512

    def body(dy_ref, w_ref, g_ref, u_ref, *rest):
        dg_ref, du_ref = rest[-2:]
        d = lax.dot_general(dy_ref[...], w_ref[...], NT, preferred_element_type=F32)
        g = g_ref[...].astype(F32)
        u = u_ref[...].astype(F32)
        s = _sigmoid(g)
        dg_ref[...] = (d * u * (s * (1.0 + g * (1.0 - s)))).astype(BF16)
        du_ref[...] = (d * g * s).astype(BF16)

    ts = pl.BlockSpec((tm, tn), lambda i, j: (i, j))
    in_specs = [pl.BlockSpec((tm, D), lambda i, j: (i, 0)), pl.BlockSpec((tn, D), lambda i, j: (j, 0)), ts, ts]
    args = (dy, w_down, gate, up)
    if dep is not None:
        in_specs, args = in_specs + [pl.BlockSpec(memory_space=pl.ANY)], args + (dep,)
    return _pcall(body, name=name, grid=(T // tm, Fh // tn), in_specs=in_specs, out_specs=[ts, ts],
                  out_shape=[jax.ShapeDtypeStruct((T, Fh), BF16)] * 2,
                  compiler_params=_params(("parallel", "parallel")))(*args)


def swiglu_fwd(g, u, name):
    T = g.shape[0]
    tq, tc = _tile(T, (512, 256, 128)), 512
    nf = FFN // tc

    def body(g_ref, u_ref, o_ref):
        gv = g_ref[...].astype(F32)
        o_ref[...] = (gv * _sigmoid(gv) * u_ref[...].astype(F32)).astype(BF16)

    lo = pl.BlockSpec((tq, tc), lambda i, j: (i, j))
    return _pcall(body, name=name, grid=(T // tq, nf), in_specs=[lo, lo], out_specs=lo,
                  out_shape=jax.ShapeDtypeStruct((T, FFN), BF16), compiler_params=_params(("parallel", "parallel")))(g, u)


def swiglu_bwd(gate, up, dact, name):
    T = gate.shape[0]
    tq, tc = _tile(T, (512, 256, 128)), 512
    nf = FFN // tc

    def body(g_ref, u_ref, d_ref, dg_ref, du_ref):
        g = g_ref[...].astype(F32)
        u = u_ref[...].astype(F32)
        d = d_ref[...]
        s = _sigmoid(g)
        dg_ref[...] = (d * u * (s * (1.0 + g * (1.0 - s)))).astype(BF16)
        du_ref[...] = (d * g * s).astype(BF16)

    lo = pl.BlockSpec((tq, tc), lambda i, j: (i, j))
    return _pcall(body, name=name, grid=(T // tq, nf), in_specs=[lo, lo, lo], out_specs=[lo, lo],
                  out_shape=[jax.ShapeDtypeStruct((T, FFN), BF16)] * 2,
                  compiler_params=_params(("parallel", "parallel")))(gate, up, dact)


def loss_head(y, target):
    T, D = y.shape
    tq = _tile(T, (256, 128))

    def body(y_ref, t_ref, dy_ref, l_ref, dyb_ref):
        @pl.when(pl.program_id(0) == 0)
        def _():
            l_ref[...] = jnp.zeros_like(l_ref)

        e = y_ref[...] - t_ref[...]
        dy = e * (1.0 / D)
        dy_ref[...] = dy
        dyb_ref[...] = dy.astype(BF16)
        l_ref[...] += 0.5 * jnp.sum(jnp.mean(e * e, axis=-1, keepdims=True))

    xs = pl.BlockSpec((tq, D), lambda i: (i, 0))
    return _pcall(body, name="loss_head", grid=(T // tq,), in_specs=[xs, xs],
                  out_specs=[xs, pl.BlockSpec((8, LANES), lambda i: (0, 0)), xs],
                  out_shape=[jax.ShapeDtypeStruct((T, D), F32), jax.ShapeDtypeStruct((8, LANES), F32),
                             jax.ShapeDtypeStruct((T, D), BF16)],
                  compiler_params=_params(("arbitrary",)))(y, target)


def sum_arrays(arrs, out_dtype, name):
    R, C = arrs[0].shape
    tr = _tile(R, (512, 256, 128, 64, 32, 16, 8))
    n = len(arrs)

    def body(*refs):
        acc = refs[0][...].astype(F32)
        for r in refs[1:n]:
            acc = acc + r[...].astype(F32)
        refs[n][...] = acc.astype(out_dtype)

    xs = pl.BlockSpec((tr, C), lambda i: (i, 0))
    return _pcall(body, name=name, grid=(R // tr,), in_specs=[xs] * n, out_specs=xs,
                  out_shape=jax.ShapeDtypeStruct((R, C), out_dtype), compiler_params=_params(("parallel",)))(*arrs)


def adamw(w, g, m, v, name):
    R, C = w.shape
    tr = _tile(R, (256, 128, 64, 32, 16, 8))
    c1 = 1.0 - ADAM_B1 ** ADAM_STEP
    c2 = 1.0 - ADAM_B2 ** ADAM_STEP

    def body(w_ref, g_ref, m_ref, v_ref, d_ref, nm_ref, nv_ref):
        gv = g_ref[...]
        mn = ADAM_B1 * m_ref[...] + (1.0 - ADAM_B1) * gv
        vn = ADAM_B2 * v_ref[...] + (1.0 - ADAM_B2) * (gv * gv)
        nm_ref[...] = mn
        nv_ref[...] = vn
        d_ref[...] = -ADAM_LR * ((mn / c1) / (jnp.sqrt(vn / c2) + ADAM_EPS) + ADAM_WD * w_ref[...])

    xs = pl.BlockSpec((tr, C), lambda i: (i, 0))
    return _pcall(body, name=name, grid=(R // tr,), in_specs=[xs] * 4, out_specs=[xs] * 3,
                  out_shape=[jax.ShapeDtypeStruct((R, C), F32)] * 3, compiler_params=_params(("parallel",)))(w, g, m, v)


def adamw_layer(w, g, m, v, l, prev, name, g_own=None):
    if g_own is not None:
        return _adamw_layer_halves(w, g, g_own, m, v, l, prev, name)
    L, r, c = w.shape
    tr = _tile(r, (256, 128, 64, 32, 16, 8))
    c1 = 1.0 - ADAM_B1 ** ADAM_STEP
    c2 = 1.0 - ADAM_B2 ** ADAM_STEP
    chained = prev is not None

    def body(w_ref, g_ref, m_ref, v_ref, *rest):
        go_ref, d_ref, nm_ref, nv_ref = rest[-4:]
        gv = g_ref[...]
        mn = ADAM_B1 * m_ref[...] + (1.0 - ADAM_B1) * gv
        vn = ADAM_B2 * v_ref[...] + (1.0 - ADAM_B2) * (gv * gv)
        go_ref[...] = gv
        nm_ref[...] = mn
        nv_ref[...] = vn
        d_ref[...] = -ADAM_LR * ((mn / c1) / (jnp.sqrt(vn / c2) + ADAM_EPS) + ADAM_WD * w_ref[...])

    ls = pl.BlockSpec((None, tr, c), lambda i: (l, i, 0))
    gs = pl.BlockSpec((tr, c), lambda i: (i, 0))
    anys = [pl.BlockSpec(memory_space=pl.ANY)] * 4 if chained else []
    return _pcall(body, name=name, grid=(r // tr,), in_specs=[ls, gs, ls, ls] + anys, out_specs=[ls] * 4,
                  out_shape=[jax.ShapeDtypeStruct((L, r, c), F32)] * 4,
                  input_output_aliases={4: 0, 5: 1, 6: 2, 7: 3} if chained else {},
                  compiler_params=_params(("parallel",)))(w, g, m, v, *(prev if chained else ()))


def _adamw_layer_halves(w, g_join, g_own, m, v, l, prev, name):
    L, r, c = w.shape
    h = r // 2
    tr = _tile(h, (256, 128, 64, 32, 16, 8))
    nh = h // tr
    c1 = 1.0 - ADAM_B1 ** ADAM_STEP
    c2 = 1.0 - ADAM_B2 ** ADAM_STEP
    chained = prev is not None

    def body(c_ref, w_ref, gj_ref, go_ref, m_ref, v_ref, *rest):
        gout_ref, d_ref, nm_ref, nv_ref = rest[-4:]
        i = pl.program_id(0)
        mine = (i >= c_ref[0] * nh) & (i < (c_ref[0] + 1) * nh)
        gv = jnp.where(mine, go_ref[...], gj_ref[...])
        mn = ADAM_B1 * m_ref[...] + (1.0 - ADAM_B1) * gv
        vn = ADAM_B2 * v_ref[...] + (1.0 - ADAM_B2) * (gv * gv)
        gout_ref[...] = gv
        nm_ref[...] = mn
        nv_ref[...] = vn
        d_ref[...] = -ADAM_LR * ((mn / c1) / (jnp.sqrt(vn / c2) + ADAM_EPS) + ADAM_WD * w_ref[...])

    ls = pl.BlockSpec((None, tr, c), lambda i, cs: (l, i, 0))
    gj = pl.BlockSpec((tr, c), lambda i, cs: (i, 0))
    go = pl.BlockSpec((tr, c), lambda i, cs: (jnp.clip(i - cs[0] * nh, 0, nh - 1), 0))
    anys = [pl.BlockSpec(memory_space=pl.ANY)] * 4 if chained else []
    spec = pltpu.PrefetchScalarGridSpec(num_scalar_prefetch=1, grid=(r // tr,), in_specs=[ls, gj, go, ls, ls] + anys,
                                        out_specs=[ls] * 4)
    return _pcall(body, name=name, grid_spec=spec, out_shape=[jax.ShapeDtypeStruct((L, r, c), F32)] * 4,
                  input_output_aliases={6: 0, 7: 1, 8: 2, 9: 3} if chained else {},
                  compiler_params=_params(("arbitrary",)))(
                      lax.axis_index("c").reshape(1).astype(jnp.int32), w, g_join, g_own, m, v, *(prev if chained else ()))


def add_pair(g, got, name):
    _, r, c = g.shape
    h = r // 2

    def body(c_ref, g_ref, o_ref, out_ref):
        del c_ref
        out_ref[...] = (g_ref[...].astype(F32) + o_ref[...].astype(F32)).astype(BF16)

    spec = pltpu.PrefetchScalarGridSpec(
        num_scalar_prefetch=1, grid=(4,),
        in_specs=[pl.BlockSpec((None, h, c), lambda k, cs: (k, cs[0], 0)), pl.BlockSpec((None, h, c), lambda k, cs: (k, 0, 0))],
        out_specs=pl.BlockSpec((None, h, c), lambda k, cs: (k, 0, 0)))
    return _pcall(body, name=name, grid_spec=spec, out_shape=jax.ShapeDtypeStruct((4, h, c), BF16),
                  compiler_params=_params(("arbitrary",)))(lax.axis_index("c").reshape(1).astype(jnp.int32), g, got)


def add_chips(p, b, name):
    _, h, c = p.shape
    tr = _tile(h, (256, 128, 64, 32, 16, 8))

    def body(me_ref, p_ref, b0, b1, b2, b3, out_ref):
        me = me_ref[0]
        acc = jnp.zeros(out_ref.shape, F32)
        for k, bk in enumerate((b0, b1, b2, b3)):
            acc = acc + jnp.where(me == k, p_ref[...], bk[...]).astype(F32)
        out_ref[...] = acc

    slot = lambda k: pl.BlockSpec((None, tr, c), lambda i, ms: (jnp.where(ms[0] == k, (k + 1) % 4, k), i, 0))
    spec = pltpu.PrefetchScalarGridSpec(
        num_scalar_prefetch=1, grid=(h // tr,),
        in_specs=[pl.BlockSpec((None, tr, c), lambda i, ms: (ms[0], i, 0)), slot(0), slot(1), slot(2), slot(3)],
        out_specs=pl.BlockSpec((tr, c), lambda i, ms: (i, 0)))
    me = (2 * lax.axis_index("x") + lax.axis_index("y")).reshape(1).astype(jnp.int32)
    return _pcall(body, name=name, grid_spec=spec, out_shape=jax.ShapeDtypeStruct((h, c), F32),
                  compiler_params=_params(("arbitrary",)))(me, p, b, b, b, b)


ANY = pl.BlockSpec(memory_space=pl.ANY)


def _place():
    x, y, c = lax.axis_index("x"), lax.axis_index("y"), lax.axis_index("c")
    chips = [(1 - x, y), (x, 1 - y), (1 - x, 1 - y)]
    return x, y, c, chips


def _rcopy(src, dst, ssem, rsem, dev):
    return pltpu.make_async_remote_copy(src_ref=src, dst_ref=dst, send_sem=ssem, recv_sem=rsem, device_id=dev,
                                        device_id_type=MESH)


def _half(c, h):
    return pl.ds(pl.multiple_of(c * h, 64), h)


def _me_chip():
    return 2 * lax.axis_index("x") + lax.axis_index("y")


def all_gather_weights(shards, name):
    n = len(shards)
    L = shards[0].shape[0]
    hl = L // 2

    def body(*refs):
        srcs, outs, (ssem, rsem) = refs[:n], refs[n:2 * n], refs[2 * n:]
        x, y, c, chips = _place()
        me = 2 * x + y
        sib = (x, y, 1 - c)
        mine, other = pl.ds(c * hl, hl), pl.ds((1 - c) * hl, hl)
        sends = []
        for w in range(n):
            for j, (px, py) in enumerate(chips):
                cp = _rcopy(srcs[w].at[mine], outs[w].at[mine, me], ssem.at[w, j], rsem.at[w, j], (px, py, c))
                cp.start()
                sends.append(cp)
        for j, (px, py) in enumerate(chips):
            slot = 2 * px + py
            for w in range(n):
                _rcopy(srcs[w].at[mine], outs[w].at[mine, slot], ssem.at[w, j], rsem.at[w, j], (px, py, c)).wait_recv()
                cp = _rcopy(outs[w].at[mine, slot], outs[w].at[mine, slot], ssem.at[w, 3 + j], rsem.at[w, 3 + j], sib)
                cp.start()
                sends.append(cp)
        for j, (px, py) in enumerate(chips):
            slot = 2 * px + py
            for w in range(n):
                _rcopy(outs[w].at[other, slot], outs[w].at[other, slot], ssem.at[w, 3 + j], rsem.at[w, 3 + j], sib).wait_recv()
        for cp in sends:
            cp.wait_send()

    outs = _pcall(body, name=name, in_specs=[ANY] * n, out_specs=[ANY] * n,
                  out_shape=[jax.ShapeDtypeStruct((L, 4) + s.shape[1:], s.dtype) for s in shards],
                  scratch_shapes=[pltpu.SemaphoreType.DMA((n, 6)), pltpu.SemaphoreType.DMA((n, 6))])(*shards)
    me = _me_chip()
    return [lax.dynamic_update_slice(o, s[:, None], (0, me, 0, 0)) for o, s in zip(outs, shards)]


HBM = pl.BlockSpec(memory_space=pltpu.HBM)
SEM = pl.BlockSpec(memory_space=pltpu.SEMAPHORE)
EFFECT = pltpu.SideEffectType.DATAFLOW_SIDE_EFFECTING


def _ici_copy(kind, src, land, x, y, c, px, py, ssem, rsem, receiving):
    me, peer = 2 * x + y, 2 * px + py
    if kind == "pair":
        return _rcopy(src.at[:, _half(1 - c, land.shape[1])], land, ssem, rsem, (x, y, 1 - c))
    if kind == "gather":
        rows = _half(c, land.shape[1] // 2)
        s, d = land.at[me, rows], land.at[peer if receiving else me, rows]
    else:
        s, d = src.at[me if receiving else peer], land.at[peer if receiving else me]
    return _rcopy(s, d, ssem, rsem, (px, py, c))


def ici_start(kind, srcs, lands, name):
    n, ns = len(lands), len(srcs)
    P = 1 if kind == "pair" else 3
    m = P * n
    na = ns + n

    def body(*refs):
        src, land, token = (refs[:ns] if ns else [None] * n), refs[ns:na], refs[-1]
        ssem, rsem = refs[na:na + m], refs[na + m:na + 2 * m]
        x, y, c, chips = _place()
        for w in range(n):
            for j, (px, py) in enumerate(chips[:P]):
                _ici_copy(kind, src[w], land[w], x, y, c, px, py, ssem[P * w + j], rsem[P * w + j], False).start()
        token[...] = jnp.zeros_like(token)

    arrs = [pltpu.with_memory_space_constraint(a, pltpu.HBM) for a in list(srcs) + list(lands)]
    outs = _pcall(
        body, name=name,
        out_shape=(*[pltpu.SemaphoreType.DMA(())] * (2 * m), *[pltpu.HBM(a.shape, a.dtype) for a in arrs],
                   jax.ShapeDtypeStruct((8, LANES), F32)),
        in_specs=[HBM] * na,
        out_specs=(*([SEM] * (2 * m)), *([HBM] * na), pl.BlockSpec(memory_space=pltpu.VMEM)),
        input_output_aliases={i: 2 * m + i for i in range(na)},
        compiler_params=pltpu.CompilerParams(has_side_effects=EFFECT))(*arrs)
    return dict(kind=kind, ssem=list(outs[:m]), rsem=list(outs[m:2 * m]), srcs=list(outs[2 * m:2 * m + ns]),
                lands=list(outs[2 * m + ns:2 * m + na])), outs[-1]


def ici_wait(state, after, name):
    kind, n, ns = state["kind"], len(state["lands"]), len(state["srcs"])
    P = 1 if kind == "pair" else 3
    m = P * n
    na = ns + n

    def body(*refs):
        src, land = (refs[:ns] if ns else [None] * n), refs[ns:na]
        ssem, rsem = refs[na:na + m], refs[na + m:na + 2 * m]
        x, y, c, chips = _place()
        for w in range(n):
            for j, (px, py) in enumerate(chips[:P]):
                _ici_copy(kind, src[w], land[w], x, y, c, px, py, ssem[P * w + j], rsem[P * w + j], False).wait_send()
                _ici_copy(kind, src[w], land[w], x, y, c, px, py, ssem[P * w + j], rsem[P * w + j], True).wait_recv()

    outs = _pcall(
        body, name=name,
        out_shape=tuple(pltpu.HBM(a.shape, a.dtype) for a in state["srcs"] + state["lands"]),
        in_specs=[HBM] * na + [SEM] * (2 * m) + [ANY], out_specs=tuple([HBM] * na),
        input_output_aliases={i: i for i in range(na)},
        compiler_params=pltpu.CompilerParams(has_side_effects=EFFECT))(
            *state["srcs"], *state["lands"], *state["ssem"], *state["rsem"], after)
    return list(outs[:ns]), list(outs[ns:])


def forward_pair(lands, name):
    n = len(lands)

    def body(*refs):
        land, (ssem, rsem) = refs[:n], refs[2 * n:]
        x, y, c, chips = _place()
        sib = (x, y, 1 - c)
        cps = []
        for w in range(n):
            h = land[w].shape[1] // 2
            for j, (px, py) in enumerate(chips):
                part = land[w].at[2 * px + py, _half(c, h)]
                cp = _rcopy(part, part, ssem.at[w, j], rsem.at[w, j], sib)
                cp.start()
                cps.append(cp)
        for w in range(n):
            h = land[w].shape[1] // 2
            for j, (px, py) in enumerate(chips):
                part = land[w].at[2 * px + py, _half(1 - c, h)]
                _rcopy(part, part, ssem.at[w, j], rsem.at[w, j], sib).wait_recv()
        for cp in cps:
            cp.wait_send()

    return _pcall(body, name=name, in_specs=[ANY] * n, out_specs=[ANY] * n,
                  out_shape=[jax.ShapeDtypeStruct(z.shape, z.dtype) for z in lands],
                  input_output_aliases={i: i for i in range(n)},
                  scratch_shapes=[pltpu.SemaphoreType.DMA((n, 3)), pltpu.SemaphoreType.DMA((n, 3))])(*lands)


def exchange_pair(gs, name):
    n = len(gs)

    def body(*refs):
        srcs, outs, (ssem, rsem) = refs[:n], refs[n:2 * n], refs[2 * n:]
        x, y, c, _ = _place()
        cps = []
        for w in range(n):
            h = srcs[w].shape[1] // 2
            cp = _rcopy(srcs[w].at[:, _half(1 - c, h)], outs[w], ssem.at[w], rsem.at[w], (x, y, 1 - c))
            cp.start()
            cps.append(cp)
        for cp in cps:
            cp.wait()

    return _pcall(body, name=name, in_specs=[ANY] * n, out_specs=[ANY] * n,
                  out_shape=[jax.ShapeDtypeStruct((4, g.shape[1] // 2, g.shape[2]), g.dtype) for g in gs],
                  scratch_shapes=[pltpu.SemaphoreType.DMA((n,)), pltpu.SemaphoreType.DMA((n,))])(*gs)


def scatter_chips(ps, name):
    n = len(ps)

    def body(*refs):
        srcs, outs, (ssem, rsem) = refs[:n], refs[n:2 * n], refs[2 * n:]
        x, y, c, chips = _place()
        me = 2 * x + y
        sends = []
        for w in range(n):
            for j, (px, py) in enumerate(chips):
                cp = _rcopy(srcs[w].at[2 * px + py], outs[w].at[me], ssem.at[w, j], rsem.at[w, j], (px, py, c))
                cp.start()
                sends.append(cp)
        for w in range(n):
            for j, (px, py) in enumerate(chips):
                _rcopy(srcs[w].at[me], outs[w].at[2 * px + py], ssem.at[w, j], rsem.at[w, j], (px, py, c)).wait_recv()
        for cp in sends:
            cp.wait_send()

    return _pcall(body, name=name, in_specs=[ANY] * n, out_specs=[ANY] * n,
                  out_shape=[jax.ShapeDtypeStruct(p.shape, p.dtype) for p in ps],
                  scratch_shapes=[pltpu.SemaphoreType.DMA((n, 3)), pltpu.SemaphoreType.DMA((n, 3))])(*ps)


def join_pair(rs, name):
    n = len(rs)

    def body(*refs):
        srcs, outs, (ssem, rsem) = refs[:n], refs[n:2 * n], refs[2 * n:]
        x, y, c, _ = _place()
        cps = []
        for w in range(n):
            h = srcs[w].shape[0]
            cp = _rcopy(srcs[w], outs[w].at[_half(c, h)], ssem.at[w], rsem.at[w], (x, y, 1 - c))
            cp.start()
            cps.append(cp)
        for w in range(n):
            h = srcs[w].shape[0]
            _rcopy(srcs[w], outs[w].at[_half(1 - c, h)], ssem.at[w], rsem.at[w], (x, y, 1 - c)).wait_recv()
        for cp in cps:
            cp.wait_send()

    outs = _pcall(body, name=name, in_specs=[ANY] * n, out_specs=[ANY] * n,
                  out_shape=[jax.ShapeDtypeStruct((2 * r.shape[0], r.shape[1]), r.dtype) for r in rs],
                  scratch_shapes=[pltpu.SemaphoreType.DMA((n,)), pltpu.SemaphoreType.DMA((n,))])(*rs)
    return outs


def fill_own_half(joined, own):
    return lax.dynamic_update_slice(joined, own, (lax.axis_index("c") * own.shape[0], 0))


def gather_all_devices(v, name):
    R, C = v.shape

    def body(src, out, ssem, rsem, lsem):
        x, y, c, _ = _place()
        me = 4 * x + 2 * y + c
        local = pltpu.make_async_copy(src, out.at[me], lsem)
        local.start()
        sends = []
        for j in range(1, 8):
            dx, dy, dc = (j >> 2) & 1, (j >> 1) & 1, j & 1
            px, py, pc = (1 - x if dx else x), (1 - y if dy else y), (1 - c if dc else c)
            cp = _rcopy(src, out.at[me], ssem.at[j - 1], rsem.at[j - 1], (px, py, pc))
            cp.start()
            sends.append(cp)
        for j in range(1, 8):
            dx, dy, dc = (j >> 2) & 1, (j >> 1) & 1, j & 1
            px, py, pc = (1 - x if dx else x), (1 - y if dy else y), (1 - c if dc else c)
            _rcopy(src, out.at[4 * px + 2 * py + pc], ssem.at[j - 1], rsem.at[j - 1], (px, py, pc)).wait_recv()
        for cp in sends:
            cp.wait_send()
        local.wait()

    return _pcall(body, name=name, in_specs=[ANY], out_specs=ANY, out_shape=jax.ShapeDtypeStruct((8, R, C), v.dtype),
                  scratch_shapes=[pltpu.SemaphoreType.DMA((7,)), pltpu.SemaphoreType.DMA((7,)), pltpu.SemaphoreType.DMA(())])(v)


def reduce_scatter_pair(gw, tag):
    names = list(gw)
    gs = [gw[k] for k in names]
    lands = [lax.empty((4, g.shape[1] // 2, g.shape[2]), g.dtype) for g in gs]
    state, token = ici_start("pair", gs, lands, "rs_pair_start_" + tag)
    state["names"] = names
    return state, token


def reduce_scatter_begin(pair_state, after, tag):
    names = pair_state["names"]
    gs, got = ici_wait(pair_state, after, "rs_pair_wait_" + tag)
    ps = [add_pair(g, o, "rs_add_pair_%s_%s" % (k, tag)) for k, g, o in zip(names, gs, got)]
    state, token = ici_start("scatter", ps, [lax.empty(p.shape, p.dtype) for p in ps], "rs_chips_start_" + tag)
    state["names"] = names
    return state, token


def reduce_scatter_end(states, after, tag):
    names, rs = [], []
    for state, gtag in states:
        ps, bs = ici_wait(state, after, "rs_chips_wait_" + gtag)
        names += state["names"]
        rs += [add_chips(p, b, "rs_add_chips_%s_%s" % (k, gtag)) for k, p, b in zip(state["names"], ps, bs)]
    return dict(zip(names, zip(join_pair(rs, "rs_join_" + tag), rs)))


GATHERED = ("w_in", "w_q_b", "w_kv_b", "w_out", "w_xq", "w_xk", "w_xv", "w_xo", "w_gate", "w_up", "w_down")
KIND = {"w_q_b": "col", "w_kv_b": "col", "w_out": "row", "w_xq": "row", "w_xk": "row", "w_xv": "row", "w_xo": "col",
        "w_gate": "col", "w_up": "col", "w_down": "row"}


def wq_to_storage(w):
    lead = w.shape[:-1]
    w = w.reshape(lead + (2, MLA_QK))
    rope = jnp.pad(w[..., MLA_NOPE:], [(0, 0)] * (w.ndim - 1) + [(0, LANES - MLA_ROPE)])
    return jnp.concatenate([w[..., :MLA_NOPE].reshape(lead + (2 * LANES,)), rope.reshape(lead + (2 * LANES,))], axis=-1)


def wq_from_storage(g):
    lead = g.shape[:-1]
    nope = g[..., :2 * LANES].reshape(lead + (2, LANES))
    rope = g[..., 2 * LANES:].reshape(lead + (2, LANES))[..., :MLA_ROPE]
    return jnp.concatenate([nope, rope], axis=-1).reshape(lead + (2 * MLA_QK,))


def wcat_from_storage(w_in_l):
    w = w_in_l.transpose(1, 0, 2).reshape(D_MODEL, IN_COLS)
    zc = lambda n: jnp.zeros((D_MODEL, n), w.dtype)
    return jnp.concatenate([w[:, :2560], w[:, 2576:], zc(64), w[:, 2560:2576], zc(112)], axis=1)


def w_in_grad_storage(dwcat):
    g = jnp.concatenate([dwcat[:, :2560], dwcat[:, C_DT:C_DT + 16], dwcat[:, 2560:3648]], axis=1)
    return g.reshape(D_MODEL, 4, IN_COLS // 4).transpose(1, 0, 2)


def _ssd_par(sm):
    rows = jnp.stack([sm["dt_bias"], sm["a_log"], sm["d_skip"]]).reshape(3, SSD_GROUPS, SSD_HPG).transpose(1, 0, 2)
    return jnp.pad(rows, ((0, 0), (0, 5), (0, LANES - SSD_HPG)))


def _dt_pad(proj, B, S):
    dt = proj[:, C_DT:C_DT + 16].reshape(B, S, SSD_GROUPS, SSD_HPG)
    return jnp.pad(dt, ((0, 0), (0, 0), (0, 0), (0, LANES - SSD_HPG))).reshape(B, S, SSD_GROUPS * LANES)


def _mla_gains(g):
    return g[:MLA_NOPE].reshape(1, LANES), jnp.pad(g[MLA_NOPE:], (0, LANES - MLA_ROPE)).reshape(1, LANES)


_Q_NOPE = lambda h: 4 * (h // 2) + h % 2
_Q_ROPE = lambda h: 4 * (h // 2) + 2 + h % 2
_K_NOPE = lambda h: 2 * h
_V_BLK = lambda h: 2 * h + 1


def layer_fwd(x, memf, cos, sin, G, kw, sm, B, S, l, dep=None):
    T = x.shape[0]
    n = lambda s: "%s_l%d" % (s, l)
    wv = lambda k: (KIND[k], 0)
    sv = dict(x0=x)
    h1 = norm_fwd(x, sm["attn_norm_g"], D_MODEL, False, n("attn_norm"), dep=dep)
    proj = mm(h1, kw["wcat"], "nn", n("in_proj"))
    mark = getattr(G, "mark", lambda v: None)
    mark(proj)
    sv.update(h1=h1, proj=proj)
    proj3 = proj.reshape(B, S, CAT_COLS)
    xbc = conv_fwd(proj3, C_XBC // 256, kw["conv_w"], sm["conv_b"].reshape(1, -1), n("conv"))
    dtp = _dt_pad(proj, B, S)
    par = _ssd_par(sm)
    y_raw, states = ssd_fwd(xbc, dtp, par, n("ssd"))
    z = proj[:, C_Z:C_Z + SSD_INNER]
    y_ssd = norm_fwd(y_raw.reshape(T, SSD_INNER), sm["ssd_norm_g"], SSD_INNER // SSD_GROUPS, False, n("ssd_norm"), gate=z)
    sv.update(xbc=xbc, states=states, y_raw=y_raw)
    qn = norm_fwd(proj[:, C_QA:C_QA + Q_LORA], sm["q_a_norm_g"], Q_LORA, False, n("q_a_norm"))
    kvn = norm_fwd(proj[:, C_KVA:C_KVA + KV_LORA], sm["kv_a_norm_g"], KV_LORA, False, n("kv_a_norm"))
    q_raw = mm(qn, G["w_q_b"], "nn", n("q_b"), wv=wv("w_q_b"))
    kv_raw = mm(kvn, G["w_kv_b"], "nn", n("kv_b"), wv=wv("w_kv_b"))
    gqn, gqr = _mla_gains(sm["mla_q_norm_g"])
    gkn, gkr = _mla_gains(sm["mla_k_norm_g"])
    qh = qk_fwd(q_raw, _Q_NOPE, q_raw, _Q_ROPE, False, gqn, gqr, cos, sin, n("q_norm_rope"))
    kh = qk_fwd(kv_raw, _K_NOPE, proj, C_KR // LANES, True, gkn, gkr, cos, sin, n("k_norm_rope"))
    vv = kv_raw.astype(BF16)
    y_mla = attn_fwd(qh.reshape(B, S, -1), kh.reshape(B, S, -1), vv.reshape(B, S, -1), MLA_HEADS, 2 * LANES, MLA_V,
                     MLA_QK ** -0.5, True, n("mla_attn"), v_idx=_V_BLK)
    ycat = jnp.concatenate([y_ssd, y_mla.reshape(T, -1)], axis=1)
    x1 = mm(ycat, G["w_out"], "nn", n("out_proj"), add=x, wv=wv("w_out"))
    sv.update(qn=qn, kvn=kvn, q_raw=q_raw, kv_raw=kv_raw, qh=qh, kh=kh, vv=vv, ycat=ycat, x1=x1)
    h2 = norm_fwd(x1, sm["xattn_norm_g"], D_MODEL, False, n("xattn_norm"))
    mn = norm_fwd(memf, sm["mem_norm_g"], D_MODEL, False, n("mem_norm"))
    xq = mm(h2, G["w_xq"], "nn", n("xq"), wv=wv("w_xq"))
    xk = mm(mn, G["w_xk"], "nn", n("xk"), wv=wv("w_xk"))
    xv = mm(mn, G["w_xv"], "nn", n("xv"), wv=wv("w_xv"), out_dtype=BF16)
    xqn = norm_fwd(xq, sm["xq_norm_g"], X_HEAD_DIM, True, n("xq_norm"))
    xkn = norm_fwd(xk, sm["xk_norm_g"], X_HEAD_DIM, True, n("xk_norm"))
    ML = memf.shape[0] // B
    xo = attn_fwd(xqn.reshape(B, S, -1), xkn.reshape(B, ML, -1), xv.reshape(B, ML, -1), X_HEADS, X_HEAD_DIM, X_HEAD_DIM,
                  X_HEAD_DIM ** -0.5, False, n("x_attn"))
    x2 = mm(xo.reshape(T, -1), G["w_xo"], "nn", n("xo_proj"), add=x1, wv=wv("w_xo"))
    mark(x2)
    sv.update(h2=h2, mn=mn, xq=xq, xk=xk, xqn=xqn, xkn=xkn, xv=xv, xo=xo, x2=x2)
    h3 = norm_fwd(x2, sm["ffn_norm_g"], D_MODEL, False, n("ffn_norm"))
    gate, up, act = gate_up_act(h3, G["w_gate"], G["w_up"], n("gate_up"))
    x3 = mm(act, G["w_down"], "nn", n("down"), add=x2, wv=wv("w_down"))
    sv.update(h3=h3, gate=gate, up=up, act=act)
    return x3, sv


def layer_bwd(dx3, dx3b, sv, memf, cos, sin, G, kw, sm, B, S, l, dep=None, rs_hook=None):
    T = dx3.shape[0]
    n = lambda s: "%s_bwd_l%d" % (s, l)
    wv = lambda k: (KIND[k], 0)
    dw = lambda k, a, b: mm(a, b, "tn", n(k + "_dw"), out_dtype=BF16, out_wv=KIND[k])
    gw, gs = {}, {}
    dgate, dup = down_dx_swiglu(dx3b, G["w_down"][0].reshape(FFN, D_MODEL), sv["gate"], sv["up"], n("down_dx"), dep=dep)
    gw["w_down"] = dw("w_down", sv["act"], dx3b)
    dh3 = gate_up_dx(dgate, dup, G["w_gate"], G["w_up"], n("gate_up_dx"))
    gw["w_gate"] = dw("w_gate", sv["h3"], dgate)
    gw["w_up"] = dw("w_up", sv["h3"], dup)
    dx2, gs["ffn_norm_g"], dx2b = norm_bwd(sv["x2"], sm["ffn_norm_g"], dh3, D_MODEL, False, n("ffn_norm"), add=dx3,
                                           bf16_copy=True)
    ffn_token = rs_hook({k: gw.pop(k) for k in ("w_down", "w_gate", "w_up")}) if rs_hook is not None else None
    dxo = mm(dx2b, G["w_xo"], "nt", n("xo_dx"), wv=wv("w_xo"), dep=ffn_token)
    gw["w_xo"] = dw("w_xo", sv["xo"].reshape(T, -1), dx2b)
    ML = memf.shape[0] // B
    dxqn, dxkn, dxv = attn_bwd(sv["xqn"].reshape(B, S, -1), sv["xkn"].reshape(B, ML, -1), sv["xv"].reshape(B, ML, -1),
                               dxo.reshape(B, S, -1), X_HEADS, X_HEAD_DIM, X_HEAD_DIM, X_HEAD_DIM ** -0.5, False, n("x_attn"))
    dxq, gs["xq_norm_g"] = norm_bwd(sv["xq"], sm["xq_norm_g"], dxqn.reshape(T, -1), X_HEAD_DIM, True, n("xq_norm"))
    dxk, gs["xk_norm_g"] = norm_bwd(sv["xk"], sm["xk_norm_g"], dxkn.reshape(B * ML, -1), X_HEAD_DIM, True, n("xk_norm"))
    dxv = dxv.reshape(B * ML, -1)
    dmn = mm(dxk, G["w_xk"], "nt", n("xk_dx"), wv=wv("w_xk"))
    dmn = mm(dxv, G["w_xv"], "nt", n("xv_dx"), wv=wv("w_xv"), add=dmn)
    gw["w_xk"] = dw("w_xk", sv["mn"], dxk)
    gw["w_xv"] = dw("w_xv", sv["mn"], dxv)
    _, gs["mem_norm_g"] = norm_bwd(memf, sm["mem_norm_g"], dmn, D_MODEL, False, n("mem_norm"))
    dh2 = mm(dxq, G["w_xq"], "nt", n("xq_dx"), wv=wv("w_xq"))
    gw["w_xq"] = dw("w_xq", sv["h2"], dxq)
    dx1, gs["xattn_norm_g"], dx1b = norm_bwd(sv["x1"], sm["xattn_norm_g"], dh2, D_MODEL, False, n("xattn_norm"), add=dx2,
                                             bf16_copy=True)
    dycat = mm(dx1b, G["w_out"], "nt", n("out_dx"), wv=wv("w_out"))
    gw["w_out"] = dw("w_out", sv["ycat"], dx1b)
    mid_token = rs_hook(None, after=dycat) if rs_hook is not None else None
    dqh, dkh, dvv = attn_bwd(sv["qh"].reshape(B, S, -1), sv["kh"].reshape(B, S, -1), sv["vv"].reshape(B, S, -1),
                             dycat[:, SSD_INNER:].reshape(B, S, -1), MLA_HEADS, 2 * LANES, MLA_V, MLA_QK ** -0.5, True,
                             n("mla_attn"), v_idx=_V_BLK)
    gqn, gqr = _mla_gains(sm["mla_q_norm_g"])
    gkn, gkr = _mla_gains(sm["mla_k_norm_g"])
    dq_raw, dgqn, dgqr = qk_bwd(sv["q_raw"], _Q_NOPE, sv["q_raw"], _Q_ROPE, False, gqn, gqr, cos, sin,
                                dqh.reshape(T, -1), n("q_norm_rope"))
    dkv_raw, dk_rope, dgkn, dgkr = qk_bwd(sv["kv_raw"], _K_NOPE, sv["proj"], C_KR // LANES, True, gkn, gkr, cos, sin,
                                          dkh.reshape(T, -1), n("k_norm_rope"), fill=dvv.reshape(T, -1), fill_blk=_V_BLK)
    gs["mla_q_norm_g"] = jnp.concatenate([dgqn[0], dgqr[0, :MLA_ROPE]])
    gs["mla_k_norm_g"] = jnp.concatenate([dgkn[0], dgkr[0, :MLA_ROPE]])
    dqn = mm(dq_raw, G["w_q_b"], "nt", n("q_b_dx"), wv=wv("w_q_b"), dep=mid_token)
    gw["w_q_b"] = dw("w_q_b", sv["qn"], dq_raw)
    dkvn = mm(dkv_raw, G["w_kv_b"], "nt", n("kv_b_dx"), wv=wv("w_kv_b"))
    gw["w_kv_b"] = dw("w_kv_b", sv["kvn"], dkv_raw)
    proj = sv["proj"]
    dq_a, gs["q_a_norm_g"] = norm_bwd(proj[:, C_QA:C_QA + Q_LORA], sm["q_a_norm_g"], dqn, Q_LORA, False, n("q_a_norm"))
    dkv_a, gs["kv_a_norm_g"] = norm_bwd(proj[:, C_KVA:C_KVA + KV_LORA], sm["kv_a_norm_g"], dkvn, KV_LORA, False,
                                        n("kv_a_norm"))
    z = proj[:, C_Z:C_Z + SSD_INNER]
    dy_raw, dz, gs["ssd_norm_g"] = norm_bwd(sv["y_raw"].reshape(T, -1), sm["ssd_norm_g"], dycat[:, :SSD_INNER],
                                            SSD_INNER // SSD_GROUPS, False, n("ssd_norm"), gate=z)
    dtp = _dt_pad(proj, B, S)
    par = _ssd_par(sm)
    dxs, dBm, dCm, ddt, dpar = ssd_bwd(sv["xbc"], dtp, par, sv["states"], dy_raw.reshape(B, S, -1), n("ssd"))
    gs["dt_bias"] = dpar[:, 0, :SSD_HPG].reshape(-1)
    gs["a_log"] = dpar[:, 1, :SSD_HPG].reshape(-1)
    gs["d_skip"] = dpar[:, 2, :SSD_HPG].reshape(-1)
    dxbc_act = jnp.concatenate([dxs, dBm, dCm], axis=2)
    dxbc, dconv_w, dconv_b = conv_bwd(proj.reshape(B, S, -1), C_XBC // 256, kw["conv_w"], sm["conv_b"].reshape(1, -1), dxbc_act,
                                      n("conv"))
    gs["conv_w"] = dconv_w[:SSD_CONV]
    gs["conv_b"] = dconv_b[0]
    ddt16 = ddt.reshape(T, SSD_GROUPS, LANES)[:, :, :SSD_HPG].reshape(T, 16)
    dproj = jnp.concatenate([dz, dxbc.reshape(T, -1), dq_a, dkv_a, dk_rope, ddt16, jnp.zeros((T, 112), F32)],
                            axis=1).astype(BF16)
    dh1 = mm(dproj, kw["wcat"], "nt", n("in_dx"))
    gw["w_in"] = w_in_grad_storage(mm(sv["h1"], dproj, "tn", n("in_dw"), out_dtype=BF16))
    dx0, gs["attn_norm_g"], dx0b = norm_bwd(sv["x0"], sm["attn_norm_g"], dh1, D_MODEL, False, n("attn_norm"), add=dx1,
                                            bf16_copy=True)
    gs = {k: v.reshape(-1) for k, v in gs.items()}
    return dx0, dx0b, gw, gs


def _small_pack(vals):
    flat = jnp.concatenate([vals[l][k].reshape(-1) for l in range(DEPTH) for k in SMALL_RS])
    n = flat.shape[0]
    rows = -(-n // LANES)
    rows = -(-rows // 8) * 8
    return jnp.pad(flat, (0, rows * LANES - n)).reshape(rows, LANES), n


def _small_unpack(flat, shapes):
    v = flat.reshape(-1)
    out, off = [dict() for _ in range(DEPTH)], 0
    for l in range(DEPTH):
        for k in SMALL_RS:
            sz = shapes[k]
            out[l][k] = v[off:off + sz]
            off += sz
    return out


def _adam_2d(a):
    if a.ndim == 2:
        return a
    return a.reshape(a.shape[0] * a.shape[1], a.shape[2])


def kernel(x, mem, positions, attn_norm_g, w_in, conv_w, conv_b, dt_bias, a_log, d_skip, ssd_norm_g, q_a_norm_g, w_q_b, kv_a_norm_g, w_kv_b, mla_q_norm_g, mla_k_norm_g, w_out, xattn_norm_g, mem_norm_g, w_xq, w_xk, w_xv, xq_norm_g, xk_norm_g, w_xo, ffn_norm_g, w_gate, w_up, w_down, loss_target, m_attn_norm_g, m_w_in, m_conv_w, m_conv_b, m_dt_bias, m_a_log, m_d_skip, m_ssd_norm_g, m_q_a_norm_g, m_w_q_b, m_kv_a_norm_g, m_w_kv_b, m_mla_q_norm_g, m_mla_k_norm_g, m_w_out, m_xattn_norm_g, m_mem_norm_g, m_w_xq, m_w_xk, m_w_xv, m_xq_norm_g, m_xk_norm_g, m_w_xo, m_ffn_norm_g, m_w_gate, m_w_up, m_w_down, v_attn_norm_g, v_w_in, v_conv_w, v_conv_b, v_dt_bias, v_a_log, v_d_skip, v_ssd_norm_g, v_q_a_norm_g, v_w_q_b, v_kv_a_norm_g, v_w_kv_b, v_mla_q_norm_g, v_mla_k_norm_g, v_w_out, v_xattn_norm_g, v_mem_norm_g, v_w_xq, v_w_xk, v_w_xv, v_xq_norm_g, v_xk_norm_g, v_w_xo, v_ffn_norm_g, v_w_gate, v_w_up, v_w_down):
    loc = locals()
    W = {k: loc[k] for k in WEIGHTS}
    Mo = {k: loc["m_" + k] for k in WEIGHTS}
    Vo = {k: loc["v_" + k] for k in WEIGHTS}
    B, S, D = x.shape
    T = B * S
    xf = x.reshape(T, D)
    memf = mem.reshape(-1, D)
    tf = loss_target.reshape(T, D)
    cos, sin = rope_tables(positions.reshape(T, 1).astype(F32))

    cw = gather_all_devices(conv_w.reshape(DEPTH * SSD_CONV, -1), "gather_conv_w")[0::2]
    cw = cw.transpose(1, 0, 2).reshape(DEPTH, SSD_CONV, SSD_CONV_DIM)
    sms = [{k: W[k][l] for k in SMALL} for l in range(DEPTH)]
    me = _me_chip()

    cw, w_in_first = lax.optimization_barrier((cw, W["w_in"][0]))

    def gather_start(l):
        shards = [(wq_to_storage(W[k][l]) if k == "w_q_b" else W[k][l]).astype(BF16) for k in GATHERED]
        if l == 0:
            shards[0] = w_in_first.astype(BF16)
        lands = [lax.dynamic_update_slice(lax.empty((4,) + s.shape, BF16), s[None], (me, 0, 0)) for s in shards]
        return ici_start("gather", [], lands, "gather_start_l%d" % l)

    def gather_end(state, names, after, tag):
        idx = [GATHERED.index(k) for k in names]
        pick3 = lambda sems: [sems[3 * w + j] for w in idx for j in range(3)]
        sub = dict(kind=state["kind"], srcs=[], lands=[state["lands"][w] for w in idx],
                   ssem=pick3(state["ssem"]), rsem=pick3(state["rsem"]))
        _, lands = ici_wait(sub, after, "gather_wait_" + tag)
        return {k: z[None] for k, z in zip(names, forward_pair(lands, "gather_forward_" + tag))}

    class Weights(dict):
        def __init__(self, state, groups, after, tag):
            super().__init__(gather_end(state, groups[0], after, tag + "a"))
            self.state, self.later, self.after, self.tag = state, list(groups[1:]), after, tag

        def mark(self, value):
            self.after = value

        def __missing__(self, k):
            names = self.later.pop(0)
            self.update(gather_end(self.state, names, self.after, self.tag + "bcdef"[len(self.later)]))
            return self[k]

    FIRST_GROUPS = (("w_in",), ("w_q_b", "w_kv_b", "w_out", "w_xq", "w_xk", "w_xv", "w_xo"), ("w_gate", "w_up", "w_down"))

    saved, Gs, kws = [], [], []
    h = xf
    state, token = gather_start(0)
    for l in range(DEPTH):
        Gl = Weights(state, FIRST_GROUPS if l == 0 else (GATHERED,), h, "l%d" % l)
        kwl = dict(wcat=wcat_from_storage(Gl["w_in"][0]), conv_w=jnp.pad(cw[l], ((0, 8 - SSD_CONV), (0, 0))))
        Gs.append(Gl)
        kws.append(kwl)
        if l + 1 < DEPTH:
            state, token = gather_start(l + 1)
        h, sv = layer_fwd(h, memf, cos, sin, Gl, kwl, sms[l], B, S, l, dep=token if l + 1 < DEPTH else None)
        saved.append(sv)
    dh, lpart, dhb = loss_head(h, tf)
    loss = lax.psum(lpart[0, 0], ("x", "y", "c"))

    grads, delta, new_m, new_v = {}, {}, {}, {}
    chain = {k: None for k in GATHERED}
    gsmall = [None] * DEPTH
    pending = [[] for _ in range(DEPTH)]

    in_pair = []

    def flush(after):
        if not in_pair:
            return None
        pair_state, l, tag = in_pair.pop()
        rs_state, tok = reduce_scatter_begin(pair_state, after, tag)
        pending[l].append((rs_state, tag))
        return tok

    def begin(gw, l, tag, after=None):
        if gw is None:
            return flush(after)
        tok_chips = flush(next(iter(gw.values())))
        pair_state, tok = reduce_scatter_pair(gw, "l%d%s" % (l, tag))
        in_pair.append((pair_state, l, "l%d%s" % (l, tag)))
        return tok if tok_chips is None else tok + tok_chips

    def finish(l, after):
        for k, (joined, own) in reduce_scatter_end(pending[l], after, "l%d" % l).items():
            if k == "w_q_b":
                joined, own = wq_from_storage(fill_own_half(joined, own)), None
            chain[k] = adamw_layer(W[k], joined, Mo[k], Vo[k], l, chain[k], "adamw_%s_l%d" % (k, l), g_own=own)

    token = None
    for l in reversed(range(DEPTH)):
        dh, dhb, gw, gs = layer_bwd(dh, dhb, saved[l], memf, cos, sin, Gs[l], kws[l], sms[l], B, S, l, dep=token,
                                    rs_hook=functools.partial(begin, l=l, tag="_ffn"))
        saved[l] = None
        gsmall[l] = gs
        token = begin(gw, l, "_mix")
        if l == 0:
            token = token + flush(dh)
        if l + 1 < DEPTH:
            finish(l + 1, token)
    grad_x = dh.reshape(B, S, D)

    spack, _ = _small_pack(gsmall)
    allv = gather_all_devices(spack, "gather_small_grads")
    ssum = sum_arrays([allv[i] for i in range(8)], F32, "sum_small_grads")
    sizes = {k: int(np.prod(W[k].shape[1:])) for k in SMALL}
    sizes["conv_w"] = SSD_CONV * SSD_CONV_DIM
    gsm = _small_unpack(ssum, sizes)
    for k in SMALL_RS:
        g = jnp.stack([gsm[l][k] for l in range(DEPTH)])
        if k == "conv_w":
            cs = SSD_CONV_DIM // 4
            g = lax.dynamic_slice_in_dim(g.reshape(DEPTH, SSD_CONV, SSD_CONV_DIM), _me_chip() * cs, cs, axis=2)
        shp = W[k].shape
        grads[k] = g.reshape(shp)
        d, nm, nv = adamw(_adam_2d(W[k]), _adam_2d(grads[k]), _adam_2d(Mo[k]), _adam_2d(Vo[k]), "adamw_" + k)
        delta[k], new_m[k], new_v[k] = d.reshape(shp), nm.reshape(shp), nv.reshape(shp)

    finish(0, lax.optimization_barrier((delta["ffn_norm_g"], chain["w_down"][1], chain["w_gate"][1], chain["w_up"][1]))[0])
    for k in GATHERED:
        grads[k], delta[k], new_m[k], new_v[k] = chain[k]

    return (loss, grad_x, *[grads[k] for k in WEIGHTS], *[delta[k] for k in WEIGHTS], *[new_m[k] for k in WEIGHTS],
            *[new_v[k] for k in WEIGHTS])
```
